```python
import jax, jax.numpy as jnp
from jax import lax
import numpy as np

D_MODEL = 1024
BATCH = 8
SEQ = 4096
DEPTH = 2

CTX_LEN = 256
GRID_W = 64
HEAD_DIM = 64
N_HEADS = 8
N_KV_HEADS = 2
GQA = N_HEADS // N_KV_HEADS
ATTN_WIDTH = N_HEADS * HEAD_DIM
KV_WIDTH = N_KV_HEADS * HEAD_DIM
WINDOW = 128
ATTN_BLOCK = 128
ATTN_SCALE = HEAD_DIM ** -0.5
ROPE_BASE = 10000.0
CONV_WIDTH = D_MODEL // 4
GM_WIDTH = D_MODEL // 4
GM_GROUPS = 4
GM_HEAD = GM_WIDTH // GM_GROUPS
CHUNK = 128
MIX_WIDTH = ATTN_WIDTH + CONV_WIDTH + GM_WIDTH
IN_WIDTH = ATTN_WIDTH + 2 * KV_WIDTH + 3 * CONV_WIDTH + 2 * GM_WIDTH
SPLIT_POINTS = (ATTN_WIDTH,
                ATTN_WIDTH + KV_WIDTH,
                ATTN_WIDTH + 2 * KV_WIDTH,
                ATTN_WIDTH + 2 * KV_WIDTH + CONV_WIDTH,
                ATTN_WIDTH + 2 * KV_WIDTH + 2 * CONV_WIDTH,
                ATTN_WIDTH + 2 * KV_WIDTH + 3 * CONV_WIDTH,
                ATTN_WIDTH + 2 * KV_WIDTH + 3 * CONV_WIDTH + GM_WIDTH)
N_GROUPS = 4
EXP_PER_GROUP = 8
N_EXPERTS = N_GROUPS * EXP_PER_GROUP
TOP_K = 2
D_EXPERT = D_MODEL // 2
MOE_BLOCK = 128
ALPHA = (2 * DEPTH) ** 0.25
BETA = (8 * DEPTH) ** -0.25
LN_EPS = 1e-6
NEG_INF = -1e30

kernel_name = 'hybrid_dit_conv_swa_gmlp_hmoe'


def layer_norm(x, g=None, b=None):
    xf = x.astype(jnp.float32)
    mu = jnp.mean(xf, axis=-1, keepdims=True)
    var = jnp.mean(jnp.square(xf - mu), axis=-1, keepdims=True)
    y = (xf - mu) * lax.rsqrt(var + LN_EPS)
    if g is not None:
        y = y * g.astype(jnp.float32) + b.astype(jnp.float32)
    return y.astype(x.dtype)


def _rope(x, pos):
    m = x.shape[-1] // 2
    freqs = ROPE_BASE ** (-jnp.arange(m, dtype=jnp.float32) / m)
    ang = pos.astype(jnp.float32)[:, None] * freqs[None, :]
    cos = jnp.cos(ang)[None, :, None, :]
    sin = jnp.sin(ang)[None, :, None, :]
    xf = x.astype(jnp.float32)
    a, b = xf[..., :m], xf[..., m:]
    return jnp.concatenate([a * cos - b * sin, b * cos + a * sin], axis=-1).astype(x.dtype)


def axial_rope(x, row, col):
    half = x.shape[-1] // 2
    return jnp.concatenate([_rope(x[..., :half], row), _rope(x[..., half:], col)], axis=-1)


def short_conv(u, w):
    up = jnp.pad(u, ((0, 0), (1, 1), (0, 0)))
    return up[:, :-2] * w[0] + up[:, 1:-1] * w[1] + up[:, 2:] * w[2]


def local_mixers(cb, cc, cx, gu, gv, conv_w, gm_ws, gm_bs):
    y_conv = cb * short_conv(cc * cx, conv_w)
    b_, l_ = gu.shape[:2]
    u = jax.nn.gelu(gu)
    v = layer_norm(jax.nn.gelu(gv)).reshape(b_, l_ // CHUNK, CHUNK, GM_GROUPS, GM_HEAD)
    s = jnp.einsum('gpq,bnqgc->bnpgc', gm_ws, v) + gm_bs.T[None, None, :, :, None]
    y_gm = u * s.reshape(b_, l_, GM_WIDTH).astype(u.dtype)
    return jnp.concatenate([y_conv, y_gm], axis=-1)


def sink_softmax(s, sink):
    col = jnp.broadcast_to(sink.astype(jnp.float32).reshape(N_KV_HEADS, GQA)[None, :, :, None, None],
                           s.shape[:-1] + (1,))
    return jax.nn.softmax(jnp.concatenate([col, s], axis=-1), axis=-1)[..., 1:]


def window_attention(q, k, v, k_ctx, v_ctx, sink):
    b_, s_ = q.shape[:2]
    nb = s_ // ATTN_BLOCK
    pad = ((0, 0), (ATTN_BLOCK, ATTN_BLOCK), (0, 0), (0, 0))
    kp, vp = jnp.pad(k, pad), jnp.pad(v, pad)
    span = 3 * ATTN_BLOCK
    rel = (jnp.arange(span)[None, :] - ATTN_BLOCK) - jnp.arange(ATTN_BLOCK)[:, None]
    band = jnp.abs(rel) <= WINDOW

    def one_block(n):
        start = n * ATTN_BLOCK
        qn = lax.dynamic_slice_in_dim(q, start, ATTN_BLOCK, axis=1).reshape(
            b_, ATTN_BLOCK, N_KV_HEADS, GQA, HEAD_DIM)
        kn = lax.dynamic_slice_in_dim(kp, start, span, axis=1)
        vn = lax.dynamic_slice_in_dim(vp, start, span, axis=1)
        kpos = start - ATTN_BLOCK + jnp.arange(span)
        valid = band & ((kpos >= 0) & (kpos < s_))[None, :]
        s_loc = jnp.einsum('bqkgd,bskd->bkgqs', qn, kn, preferred_element_type=jnp.float32) * ATTN_SCALE
        s_loc = jnp.where(valid, s_loc, NEG_INF)
        s_ctx = jnp.einsum('bqkgd,bckd->bkgqc', qn, k_ctx, preferred_element_type=jnp.float32) * ATTN_SCALE
        p = sink_softmax(jnp.concatenate([s_loc, s_ctx], axis=-1), sink).astype(v.dtype)
        o = (jnp.einsum('bkgqs,bskd->bqkgd', p[..., :span], vn)
             + jnp.einsum('bkgqc,bckd->bqkgd', p[..., span:], v_ctx))
        return o.reshape(b_, ATTN_BLOCK, ATTN_WIDTH)

    out = lax.map(one_block, jnp.arange(nb))
    return out.transpose(1, 0, 2, 3).reshape(b_, s_, ATTN_WIDTH)


def context_attention(q, k, v, sink):
    b_, c_ = q.shape[:2]
    qg = q.reshape(b_, c_, N_KV_HEADS, GQA, HEAD_DIM)
    s = jnp.einsum('bqkgd,bckd->bkgqc', qg, k, preferred_element_type=jnp.float32) * ATTN_SCALE
    p = sink_softmax(s, sink).astype(v.dtype)
    o = jnp.einsum('bkgqc,bckd->bqkgd', p, v)
    return o.reshape(b_, c_, ATTN_WIDTH)


def hier_moe(h, w_rg, b_rg, w_re, b_re, w1, w3, w2):
    n, d_ = h.shape
    hf = h.astype(jnp.float32)
    g_logit = hf @ w_rg.astype(jnp.float32) + b_rg.astype(jnp.float32)
    g_val, g_idx = lax.top_k(g_logit, 1)
    p_group = jnp.exp(g_val[:, 0] - jax.nn.logsumexp(g_logit, axis=-1))
    e_logit = (hf @ w_re.astype(jnp.float32)).reshape(n, N_GROUPS, EXP_PER_GROUP) \
        + b_re.astype(jnp.float32).reshape(N_GROUPS, EXP_PER_GROUP)
    e_logit = jnp.take_along_axis(e_logit, g_idx[:, :, None], axis=1)[:, 0]
    e_val, e_idx = lax.top_k(e_logit, TOP_K)
    gates = p_group[:, None] * jax.nn.softmax(e_val, axis=-1)
    expert = g_idx * EXP_PER_GROUP + e_idx

    a = n * TOP_K
    flat_e = expert.reshape(-1)
    flat_g = gates.reshape(-1)
    order = jnp.argsort(flat_e)
    se = flat_e[order]
    counts = jnp.bincount(flat_e, length=N_EXPERTS)
    starts = jnp.cumsum(counts) - counts
    padded = (counts + MOE_BLOCK - 1) // MOE_BLOCK * MOE_BLOCK
    pends = jnp.cumsum(padded)
    pstarts = pends - padded
    dest = pstarts[se] + jnp.arange(a) - starts[se]
    n_blocks = -(-a // MOE_BLOCK) + N_EXPERTS
    slot_tok = jnp.zeros(n_blocks * MOE_BLOCK, jnp.int32).at[dest].set((order // TOP_K).astype(jnp.int32))
    slot_gate = jnp.zeros(n_blocks * MOE_BLOCK, h.dtype).at[dest].set(flat_g[order].astype(h.dtype))
    block_e = jnp.minimum(jnp.searchsorted(pends, jnp.arange(n_blocks) * MOE_BLOCK, side='right'),
                          N_EXPERTS - 1)
    xs = h[slot_tok].reshape(n_blocks, MOE_BLOCK, d_)

    def run(args):
        xb, e = args
        return (jax.nn.silu(xb @ w1[e]) * (xb @ w3[e])) @ w2[e]

    ys = lax.map(run, (xs, block_e)).reshape(-1, d_)
    return jnp.zeros_like(h).at[slot_tok].add(ys * slot_gate[:, None])


def setup_inputs(seed: int = 0) -> dict:
    key = jax.random.key(seed)
    ks = jax.random.split(key, 23)

    def nrm(k, shape, s):
        return jax.random.normal(k, shape, jnp.float32) * s

    d = D_MODEL
    return {
        'x': nrm(ks[0], (BATCH, SEQ, d), 1.0),
        'c': nrm(ks[1], (BATCH, d), 1.0),
        'ctx': nrm(ks[2], (BATCH, CTX_LEN, d), 1.0),
        'c_ctx': nrm(ks[3], (d,), 1.0),
        'w_ada': nrm(ks[4], (DEPTH, d, 6 * d), d ** -0.5),
        'b_ada': nrm(ks[5], (DEPTH, 6 * d), 0.02),
        'w_in': nrm(ks[6], (DEPTH, d, IN_WIDTH), d ** -0.5),
        'conv_w': nrm(ks[7], (DEPTH, 3, CONV_WIDTH), 3 ** -0.5),
        'attn_sink': nrm(ks[8], (DEPTH, N_HEADS), 0.5),
        'gm_ws': nrm(ks[9], (DEPTH, GM_GROUPS, CHUNK, CHUNK), CHUNK ** -0.5),
        'gm_bs': 1.0 + nrm(ks[10], (DEPTH, GM_GROUPS, CHUNK), 0.02),
        'w_out': nrm(ks[11], (DEPTH, MIX_WIDTH, d), BETA * MIX_WIDTH ** -0.5),
        'ln1_g': 1.0 + nrm(ks[12], (DEPTH, d), 0.02),
        'ln1_b': nrm(ks[13], (DEPTH, d), 0.02),
        'w_rg': nrm(ks[14], (DEPTH, d, N_GROUPS), d ** -0.5),
        'b_rg': nrm(ks[15], (DEPTH, N_GROUPS), 0.01),
        'w_re': nrm(ks[16], (DEPTH, d, N_EXPERTS), d ** -0.5),
        'b_re': nrm(ks[17], (DEPTH, N_EXPERTS), 0.01),
        'w1': nrm(ks[18], (DEPTH, N_EXPERTS, d, D_EXPERT), d ** -0.5),
        'w3': nrm(ks[19], (DEPTH, N_EXPERTS, d, D_EXPERT), d ** -0.5),
        'w2': nrm(ks[20], (DEPTH, N_EXPERTS, D_EXPERT, d), BETA * D_EXPERT ** -0.5),
        'ln2_g': 1.0 + nrm(ks[21], (DEPTH, d), 0.02),
        'ln2_b': nrm(ks[22], (DEPTH, d), 0.02),
    }


def reference(x, c, ctx, c_ctx, w_ada, b_ada, w_in, conv_w, attn_sink, gm_ws, gm_bs, w_out,
              ln1_g, ln1_b, w_rg, b_rg, w_re, b_re, w1, w3, w2, ln2_g, ln2_b):
    b_, s_, d_ = x.shape
    c_len = ctx.shape[1]
    rows = s_ // GRID_W
    row = jnp.repeat(jnp.arange(rows), GRID_W)
    col = jnp.tile(jnp.arange(GRID_W), rows)
    c_act = jax.nn.silu(c)
    cc_act = jax.nn.silu(c_ctx)

    for l in range(DEPTH):
        last = l == DEPTH - 1
        mod_x = c_act @ w_ada[l] + b_ada[l]
        sh1, sc1, g1, sh2, sc2, g2 = jnp.split(mod_x[:, None, :], 6, axis=-1)
        n_mod = 2 if last else 6
        mod_c = (cc_act @ w_ada[l][:, :n_mod * d_] + b_ada[l][:n_mod * d_]).reshape(n_mod, d_)

        hx = x * (1 + sc1) + sh1
        hc = ctx * (1 + mod_c[1]) + mod_c[0]
        qx, kx, vx, cbx, ccx, cvx, gux, gvx = jnp.split(hx @ w_in[l], SPLIT_POINTS, axis=-1)
        qx = axial_rope(qx.reshape(b_, s_, N_HEADS, HEAD_DIM), row, col)
        kx = axial_rope(kx.reshape(b_, s_, N_KV_HEADS, HEAD_DIM), row, col)
        vx = vx.reshape(b_, s_, N_KV_HEADS, HEAD_DIM)
        if last:
            kc, vc = jnp.split(hc @ w_in[l][:, ATTN_WIDTH:ATTN_WIDTH + 2 * KV_WIDTH], 2, axis=-1)
        else:
            qc, kc, vc, cbc, ccc, cvc, guc, gvc = jnp.split(hc @ w_in[l], SPLIT_POINTS, axis=-1)
        kc = kc.reshape(b_, c_len, N_KV_HEADS, HEAD_DIM)
        vc = vc.reshape(b_, c_len, N_KV_HEADS, HEAD_DIM)

        att_x = window_attention(qx, kx, vx, kc, vc, attn_sink[l])
        o_x = jnp.concatenate(
            [att_x, local_mixers(cbx, ccx, cvx, gux, gvx, conv_w[l], gm_ws[l], gm_bs[l])], axis=-1) @ w_out[l]
        if not last:
            att_c = context_attention(qc, kc, vc, attn_sink[l])
            o_c = jnp.concatenate(
                [att_c, local_mixers(cbc, ccc, cvc, guc, gvc, conv_w[l], gm_ws[l], gm_bs[l])], axis=-1) @ w_out[l]
            ctx = layer_norm(ALPHA * ctx + mod_c[2] * o_c, ln1_g[l], ln1_b[l])
        x = layer_norm(ALPHA * x + g1 * o_x, ln1_g[l], ln1_b[l])

        hx = (x * (1 + sc2) + sh2).reshape(b_ * s_, d_)
        if last:
            tokens = hx
        else:
            hc = (ctx * (1 + mod_c[4]) + mod_c[3]).reshape(b_ * c_len, d_)
            tokens = jnp.concatenate([hx, hc], axis=0)
        y = hier_moe(tokens, w_rg[l], b_rg[l], w_re[l], b_re[l], w1[l], w3[l], w2[l])
        if not last:
            ctx = layer_norm(ALPHA * ctx + mod_c[5] * y[b_ * s_:].reshape(b_, c_len, d_), ln2_g[l], ln2_b[l])
        x = layer_norm(ALPHA * x + g2 * y[:b_ * s_].reshape(b_, s_, d_), ln2_g[l], ln2_b[l])

    return x
```

```python
import functools

import jax
import jax.numpy as jnp
from jax import lax
from jax.experimental import pallas as pl
from jax.experimental.pallas import tpu as pltpu

F32 = jnp.float32
BF16 = jnp.bfloat16

D_MODEL = 1024
GRID_W = 64
HEAD_DIM = 64
N_HEADS = 8
N_KV_HEADS = 2
ATTN_WIDTH = N_HEADS * HEAD_DIM
KV_WIDTH = N_KV_HEADS * HEAD_DIM
WINDOW = 128
ATTN_SCALE = HEAD_DIM ** -0.5
ROPE_BASE = 10000.0
CONV_WIDTH = D_MODEL // 4
GM_WIDTH = D_MODEL // 4
GM_GROUPS = 4
GM_HEAD = GM_WIDTH // GM_GROUPS
CHUNK = 128
LOCAL_WIDTH = CONV_WIDTH + GM_WIDTH
IN_WIDTH = ATTN_WIDTH + 2 * KV_WIDTH + 3 * CONV_WIDTH + 2 * GM_WIDTH
QKV_WIDTH = ATTN_WIDTH + 2 * KV_WIDTH
N_GROUPS = 4
EXP_PER_GROUP = 8
N_EXPERTS = N_GROUPS * EXP_PER_GROUP
TOP_K = 2
D_EXPERT = D_MODEL // 2
MOE_BLOCK = 128
LN_EPS = 1e-6
NEG_INF = -1e30

LANES = 128
SUBLANES = 8
VMEM_LIMIT_BYTES = 48 * 1024 * 1024

ROUTER_COLS = LANES
ROPE_HALF_PAIR = HEAD_DIM // 4


def _bdot(a, b):
    return jnp.dot(a, b, preferred_element_type=F32)


def _split_bf16(a):
    hi = a.astype(BF16)
    lo = (a - hi.astype(F32)).astype(BF16)
    return hi, lo


def _layer_norm(r):
    mu = jnp.mean(r, axis=-1, keepdims=True)
    d = r - mu
    var = jnp.mean(d * d, axis=-1, keepdims=True)
    return d * lax.rsqrt(var + LN_EPS)


def _params(*sem):
    return pltpu.CompilerParams(dimension_semantics=sem, vmem_limit_bytes=VMEM_LIMIT_BYTES)


ADA_ROWS = 16
ADA_TILE = 1536


def _ada_kernel(c_ref, w_ref, b_ref, o_ref):
    act = jax.nn.silu(c_ref[...])
    ah, al = _split_bf16(act)
    wh, wl = _split_bf16(w_ref[...])
    o_ref[...] = _bdot(ah, wh) + _bdot(ah, wl) + _bdot(al, wh) + b_ref[...]


def _ada_call(cin, w_ada, b_ada):
    depth, d, n = w_ada.shape
    return pl.pallas_call(
        _ada_kernel,
        grid=(depth, n // ADA_TILE),
        in_specs=[
            pl.BlockSpec((ADA_ROWS, d), lambda l, j: (0, 0)),
            pl.BlockSpec((None, d, ADA_TILE), lambda l, j: (l, 0, j)),
            pl.BlockSpec((None, 1, ADA_TILE), lambda l, j: (l, 0, j)),
        ],
        out_specs=pl.BlockSpec((None, ADA_ROWS, ADA_TILE), lambda l, j: (l, 0, j)),
        out_shape=jax.ShapeDtypeStruct((depth, ADA_ROWS, n), F32),
        compiler_params=_params("parallel", "parallel"),
        name="ada_modulation",
    )(cin, w_ada, b_ada.reshape(depth, 1, n))


def _inproj_kernel(x_ref, xp_ref, xn_ref, sc_ref, sh_ref, w_ref, cos_ref, sa_ref, sb_ref,
                   cw_ref, gw_ref, gb_ref,
                   q_ref, k_ref, ksw_ref, v_ref, vsw_ref, y_ref, *, ts):
    t = pl.program_id(1)
    nt = pl.num_programs(1)
    sc = 1.0 + sc_ref[...]
    sh = sh_ref[...]
    hx = (x_ref[...] * sc + sh).astype(BF16)

    pq = _bdot(hx, w_ref[:, 0:QKV_WIDTH])
    cos = cos_ref[...]
    sa = sa_ref[...]
    sb = sb_ref[...]

    def rope(z):
        return (z * cos + pltpu.roll(z, ROPE_HALF_PAIR, 1) * sa
                + pltpu.roll(z, LANES - ROPE_HALF_PAIR, 1) * sb)

    for j in range(ATTN_WIDTH // LANES):
        sl = slice(j * LANES, (j + 1) * LANES)
        q_ref[:, sl] = (rope(pq[:, sl]) * ATTN_SCALE).astype(BF16)
    kr = rope(pq[:, ATTN_WIDTH:ATTN_WIDTH + KV_WIDTH])
    k_ref[...] = kr.astype(BF16)
    ksw_ref[...] = pltpu.roll(kr, HEAD_DIM, 1).astype(BF16)
    vv = pq[:, ATTN_WIDTH + KV_WIDTH:QKV_WIDTH]
    v_ref[...] = vv.astype(BF16)
    vsw_ref[...] = pltpu.roll(vv, HEAD_DIM, 1).astype(BF16)

    pm = _bdot(hx, w_ref[:, QKV_WIDTH:IN_WIDTH])
    cb = pm[:, 0:CONV_WIDTH]
    u = pm[:, CONV_WIDTH:2 * CONV_WIDTH] * pm[:, 2 * CONV_WIDTH:3 * CONV_WIDTH]

    halo = jnp.concatenate([xp_ref[...], xn_ref[...]], axis=0)
    hh = (halo * sc + sh).astype(BF16)
    ph = _bdot(hh, w_ref[:, QKV_WIDTH + CONV_WIDTH:QKV_WIDTH + 3 * CONV_WIDTH])
    uh = ph[:, 0:CONV_WIDTH] * ph[:, CONV_WIDTH:2 * CONV_WIDTH]
    up_row = jnp.where(t > 0, uh[SUBLANES - 1:SUBLANES, :], 0.0)
    dn_row = jnp.where(t < nt - 1, uh[SUBLANES:SUBLANES + 1, :], 0.0)
    ridx = lax.broadcasted_iota(jnp.int32, (ts, CONV_WIDTH), 0)
    u_up = jnp.where(ridx == 0, up_row, pltpu.roll(u, 1, 0))
    u_dn = jnp.where(ridx == ts - 1, dn_row, pltpu.roll(u, ts - 1, 0))
    cw = cw_ref[...]
    y_conv = cb * (u_up * cw[0:1, :] + u * cw[1:2, :] + u_dn * cw[2:3, :])
    y_ref[:, 0:CONV_WIDTH] = y_conv.astype(BF16)

    gu = jax.nn.gelu(pm[:, 3 * CONV_WIDTH:3 * CONV_WIDTH + GM_WIDTH])
    gv = _layer_norm(jax.nn.gelu(pm[:, 3 * CONV_WIDTH + GM_WIDTH:3 * CONV_WIDTH + 2 * GM_WIDTH])).astype(BF16)
    lane = lax.broadcasted_iota(jnp.int32, (CHUNK, LANES), 1)
    zero = jnp.zeros((CHUNK, LANES), BF16)
    for c in range(ts // CHUNK):
        rows = slice(c * CHUNK, (c + 1) * CHUNK)
        for j in range(GM_WIDTH // LANES):
            cols = slice(j * LANES, (j + 1) * LANES)
            vp = gv[rows, cols]
            s = (_bdot(gw_ref[2 * j], jnp.where(lane < GM_HEAD, vp, zero))
                 + _bdot(gw_ref[2 * j + 1], jnp.where(lane >= GM_HEAD, vp, zero))
                 + gb_ref[:, cols])
            y_ref[rows, CONV_WIDTH + j * LANES:CONV_WIDTH + (j + 1) * LANES] = (gu[rows, cols] * s).astype(BF16)


def _inproj_call(x2d, row_off, n_seq, seq_len, ts, sc, sh, w_in, cos, sa, sb, conv_w, gm_w, gm_b):
    nt = seq_len // ts
    off_t = row_off // ts
    off_8 = row_off // SUBLANES
    last_8 = x2d.shape[0] // SUBLANES - 1
    per_tile_8 = ts // SUBLANES
    n_out = n_seq * seq_len
    d = D_MODEL

    def tile_idx(b, t):
        return (off_t + b * nt + t, 0)

    def prev_idx(b, t):
        return (jnp.maximum(off_8 + (b * nt + t) * per_tile_8 - 1, 0), 0)

    def next_idx(b, t):
        return (jnp.minimum(off_8 + (b * nt + t + 1) * per_tile_8, last_8), 0)

    def out_idx(b, t):
        return (b * nt + t, 0)

    vec = pl.BlockSpec((None, 1, d), lambda b, t: (b, 0, 0))
    rope_spec = pl.BlockSpec((ts, LANES), lambda b, t: (t, 0))
    kv_spec = pl.BlockSpec((ts, KV_WIDTH), out_idx)
    kv_shape = jax.ShapeDtypeStruct((n_out, KV_WIDTH), BF16)
    return pl.pallas_call(
        functools.partial(_inproj_kernel, ts=ts),
        grid=(n_seq, nt),
        in_specs=[
            pl.BlockSpec((ts, d), tile_idx),
            pl.BlockSpec((SUBLANES, d), prev_idx),
            pl.BlockSpec((SUBLANES, d), next_idx),
            vec, vec,
            pl.BlockSpec((d, IN_WIDTH), lambda b, t: (0, 0)),
            rope_spec, rope_spec, rope_spec,
            pl.BlockSpec((3, CONV_WIDTH), lambda b, t: (0, 0)),
            pl.BlockSpec((GM_GROUPS, CHUNK, CHUNK), lambda b, t: (0, 0, 0)),
            pl.BlockSpec((CHUNK, GM_WIDTH), lambda b, t: (0, 0)),
        ],
        out_specs=[
            pl.BlockSpec((ts, ATTN_WIDTH), out_idx),
            kv_spec, kv_spec, kv_spec, kv_spec,
            pl.BlockSpec((ts, LOCAL_WIDTH), out_idx),
        ],
        out_shape=[
            jax.ShapeDtypeStruct((n_out, ATTN_WIDTH), BF16),
            kv_shape, kv_shape, kv_shape, kv_shape,
            jax.ShapeDtypeStruct((n_out, LOCAL_WIDTH), BF16),
        ],
        compiler_params=_params("parallel", "parallel"),
        name="in_projection",
    )(x2d, x2d, x2d, sc, sh, w_in, cos, sa, sb, conv_w, gm_w, gm_b)


def _attn_body(q, keys, keys_sw, vals, vals_sw, bias, sink_ref, o_ref, tq):
    nk = keys.shape[0]
    lane = lax.broadcasted_iota(jnp.int32, (nk, LANES), 1)
    lo = lane < HEAD_DIM
    zero = jnp.zeros((nk, LANES), BF16)
    rid = lax.broadcasted_iota(jnp.int32, (2 * tq, 1), 0)
    nt_dims = (((1,), (1,)), ((), ()))
    gqa = N_HEADS // N_KV_HEADS
    for h in range(N_KV_HEADS):
        k_own, k_oth = (keys, keys_sw) if h == 0 else (keys_sw, keys)
        v_own, v_oth = (vals, vals_sw) if h == 0 else (vals_sw, vals)
        kz = (jnp.where(lo, k_own, zero), jnp.where(lo, zero, k_oth))
        vz = (jnp.where(lo, v_own, zero), jnp.where(lo, zero, v_oth))
        c0 = h * gqa * HEAD_DIM
        qs = jnp.concatenate([q[:, c0:c0 + LANES], q[:, c0 + LANES:c0 + 2 * LANES]], axis=0)
        acc = None
        for par in range(2):
            s = lax.dot_general(qs, kz[par], nt_dims, preferred_element_type=F32)
            if bias is not None:
                s = s + bias
            sink = jnp.where(rid < tq, sink_ref[h * gqa + par], sink_ref[h * gqa + 2 + par])
            m = jnp.maximum(jnp.max(s, axis=-1, keepdims=True), sink)
            p = jnp.exp(s - m)
            den = jnp.sum(p, axis=-1, keepdims=True) + jnp.exp(sink - m)
            o = _bdot(p.astype(BF16), vz[par]) / den
            acc = o if acc is None else acc + o
        o_ref[:, c0:c0 + LANES] = acc[0:tq].astype(BF16)
        o_ref[:, c0 + LANES:c0 + 2 * LANES] = acc[tq:2 * tq].astype(BF16)


def _win_attn_kernel(sink_ref, q_ref, kp_ref, kc_ref, kn_ref, kx_ref,
                     ksp_ref, ksc_ref, ksn_ref, ksx_ref,
                     vp_ref, vc_ref, vn_ref, vx_ref,
                     vsp_ref, vsc_ref, vsn_ref, vsx_ref, o_ref, *, tq, ctx_len):
    n = pl.program_id(1)
    nb = pl.num_programs(1)
    row = lax.broadcasted_iota(jnp.int32, (tq, tq), 0)
    col = lax.broadcasted_iota(jnp.int32, (tq, tq), 1)
    b_prev = jnp.where((col >= row) & (n > 0), 0.0, NEG_INF)
    b_next = jnp.where((col <= row) & (n < nb - 1), 0.0, NEG_INF)
    b_half = jnp.concatenate([b_prev, jnp.zeros((tq, tq), F32), b_next, jnp.zeros((tq, ctx_len), F32)], axis=1)
    bias = jnp.concatenate([b_half, b_half], axis=0)

    def cat(a, b, c, d):
        return jnp.concatenate([a[...], b[...], c[...], d[...]], axis=0)

    _attn_body(q_ref[...], cat(kp_ref, kc_ref, kn_ref, kx_ref), cat(ksp_ref, ksc_ref, ksn_ref, ksx_ref),
               cat(vp_ref, vc_ref, vn_ref, vx_ref), cat(vsp_ref, vsc_ref, vsn_ref, vsx_ref),
               bias, sink_ref, o_ref, tq)


def _win_attn_call(sink, q, k, ksw, v, vsw, kx, kxsw, vx, vxsw, n_seq, seq_len, ctx_len):
    tq = WINDOW
    nb = seq_len // tq

    def cur(b, n):
        return (b * nb + n, 0)

    def prev(b, n):
        return (b * nb + jnp.maximum(n - 1, 0), 0)

    def nxt(b, n):
        return (b * nb + jnp.minimum(n + 1, nb - 1), 0)

    def ctx(b, n):
        return (b, 0)

    def kv_specs():
        return [pl.BlockSpec((tq, KV_WIDTH), prev), pl.BlockSpec((tq, KV_WIDTH), cur),
                pl.BlockSpec((tq, KV_WIDTH), nxt), pl.BlockSpec((ctx_len, KV_WIDTH), ctx)]

    return pl.pallas_call(
        functools.partial(_win_attn_kernel, tq=tq, ctx_len=ctx_len),
        grid=(n_seq, nb),
        in_specs=[pl.BlockSpec(memory_space=pltpu.SMEM), pl.BlockSpec((tq, ATTN_WIDTH), cur)]
        + kv_specs() + kv_specs() + kv_specs() + kv_specs(),
        out_specs=pl.BlockSpec((tq, ATTN_WIDTH), cur),
        out_shape=jax.ShapeDtypeStruct((n_seq * seq_len, ATTN_WIDTH), BF16),
        compiler_params=_params("parallel", "parallel"),
        name="window_attention",
    )(sink, q, k, k, k, kx, ksw, ksw, ksw, kxsw, v, v, v, vx, vsw, vsw, vsw, vxsw)


def _ctx_attn_kernel(sink_ref, q_ref, k_ref, ks_ref, v_ref, vs_ref, o_ref, *, tq):
    _attn_body(q_ref[...], k_ref[...], ks_ref[...], v_ref[...], vs_ref[...], None, sink_ref, o_ref, tq)


def _ctx_attn_call(sink, q, k, ksw, v, vsw, n_seq, ctx_len):
    def blk(w):
        return pl.BlockSpec((ctx_len, w), lambda b: (b, 0))

    return pl.pallas_call(
        functools.partial(_ctx_attn_kernel, tq=ctx_len),
        grid=(n_seq,),
        in_specs=[pl.BlockSpec(memory_space=pltpu.SMEM), blk(ATTN_WIDTH),
                  blk(KV_WIDTH), blk(KV_WIDTH), blk(KV_WIDTH), blk(KV_WIDTH)],
        out_specs=blk(ATTN_WIDTH),
        out_shape=jax.ShapeDtypeStruct((n_seq * ctx_len, ATTN_WIDTH), BF16),
        compiler_params=_params("parallel"),
        name="context_attention",
    )(sink, q, k, ksw, v, vsw)


OUT_TILE = 256


def _outproj_kernel(attx_ref, attc_ref, yx_ref, yc_ref, xx_ref, xc_ref, wo_ref, g_ref, lng_ref, lnb_ref,
                    sc_ref, sh_ref, wr_ref, br_ref, x1_ref, h2_ref, lg_ref, *, alpha, n_lat_tiles):
    is_lat = pl.program_id(0) < n_lat_tiles
    att = jnp.where(is_lat, attx_ref[...], attc_ref[...])
    yloc = jnp.where(is_lat, yx_ref[...], yc_ref[...])
    xin = jnp.where(is_lat, xx_ref[...], xc_ref[...])
    o = _bdot(att, wo_ref[0:ATTN_WIDTH, :]) + _bdot(yloc, wo_ref[ATTN_WIDTH:, :])
    x1 = _layer_norm(alpha * xin + g_ref[...] * o) * lng_ref[...] + lnb_ref[...]
    x1_ref[...] = x1
    h2 = x1 * (1.0 + sc_ref[...]) + sh_ref[...]
    h2_ref[...] = h2
    hh, hl = _split_bf16(h2)
    t1 = _bdot(hh, wr_ref[...])
    lg_ref[...] = (t1[:, 0:ROUTER_COLS] + t1[:, ROUTER_COLS:] + _bdot(hl, wr_ref[:, 0:ROUTER_COLS])
                   + br_ref[...])


def _outproj_call(att_x, att_c, y_x, y_c, x_arr, x_off, c_arr, c_off, n_lat, n_ctx, seq_len,
                  w_out, gvecs, lng, lnb, scvecs, shvecs, wr, br, alpha):
    ts = OUT_TILE
    d = D_MODEL
    n_lat_tiles = n_lat // ts
    n_tiles = (n_lat + n_ctx) // ts
    tiles_per_seq = seq_len // ts
    n_seq = n_lat // seq_len
    xo, co = x_off // ts, c_off // ts

    def lat_loc(i):
        return (jnp.minimum(i, n_lat_tiles - 1), 0)

    def ctx_loc(i):
        return (jnp.maximum(i - n_lat_tiles, 0), 0)

    def lat_in(i):
        return (xo + jnp.minimum(i, n_lat_tiles - 1), 0)

    def ctx_in(i):
        return (co + jnp.maximum(i - n_lat_tiles, 0), 0)

    def vec_idx(i):
        return (jnp.where(i < n_lat_tiles, i // tiles_per_seq, n_seq), 0, 0)

    vecb = pl.BlockSpec((None, 1, d), vec_idx)
    vec0 = pl.BlockSpec((1, d), lambda i: (0, 0))
    return pl.pallas_call(
        functools.partial(_outproj_kernel, alpha=alpha, n_lat_tiles=n_lat_tiles),
        grid=(n_tiles,),
        in_specs=[
            pl.BlockSpec((ts, ATTN_WIDTH), lat_loc), pl.BlockSpec((ts, ATTN_WIDTH), ctx_loc),
            pl.BlockSpec((ts, LOCAL_WIDTH), lat_loc), pl.BlockSpec((ts, LOCAL_WIDTH), ctx_loc),
            pl.BlockSpec((ts, d), lat_in), pl.BlockSpec((ts, d), ctx_in),
            pl.BlockSpec((d, d), lambda i: (0, 0)),
            vecb, vec0, vec0, vecb, vecb,
            pl.BlockSpec((d, 2 * ROUTER_COLS), lambda i: (0, 0)),
            pl.BlockSpec((1, ROUTER_COLS), lambda i: (0, 0)),
        ],
        out_specs=[pl.BlockSpec((ts, d), lambda i: (i, 0)), pl.BlockSpec((ts, d), lambda i: (i, 0)),
                   pl.BlockSpec((ts, ROUTER_COLS), lambda i: (i, 0))],
        out_shape=[jax.ShapeDtypeStruct((n_lat + n_ctx, d), F32), jax.ShapeDtypeStruct((n_lat + n_ctx, d), F32),
                   jax.ShapeDtypeStruct((n_lat + n_ctx, ROUTER_COLS), F32)],
        compiler_params=_params("parallel"),
        name="out_projection",
    )(att_x, att_c, y_x, y_c, x_arr, c_arr, w_out, gvecs, lng, lnb, scvecs, shvecs, wr, br)


def _moe_kernel(be_ref, tok0_ref, tokn_ref, gate_ref, h_hbm, w1_ref, w3_ref, w2_ref, o_ref, xbuf, sem):
    i = pl.program_id(0)
    nb = pl.num_programs(0)

    def row_copy(tok, slot, r):
        return pltpu.make_async_copy(h_hbm.at[pl.ds(tok, 1)], xbuf.at[slot, pl.ds(r, 1)], sem.at[slot])

    def issue(tok_ref, slot):
        def body(r, carry):
            row_copy(tok_ref[0, 0, r], slot, r).start()
            return carry
        lax.fori_loop(0, MOE_BLOCK, body, 0, unroll=8)

    @pl.when(i == 0)
    def _():
        issue(tok0_ref, 0)

    @pl.when(i + 1 < nb)
    def _():
        issue(tokn_ref, (i + 1) % 2)

    slot = i % 2
    pltpu.make_async_copy(xbuf.at[slot], xbuf.at[slot], sem.at[slot]).wait()
    x = xbuf[slot].astype(BF16)
    hmid = jax.nn.silu(_bdot(x, w1_ref[...])) * _bdot(x, w3_ref[...])
    y = _bdot(hmid.astype(BF16), w2_ref[...])
    g = gate_ref[...]
    for j in range(D_MODEL // LANES):
        sl = slice(j * LANES, (j + 1) * LANES)
        o_ref[:, sl] = y[:, sl] * g


def _moe_call(block_e, slot_tok, gate_b, h2, w1, w3, w2):
    n_blocks = block_e.shape[0]
    d = D_MODEL
    tok3 = slot_tok.reshape(n_blocks, 1, MOE_BLOCK)
    grid_spec = pltpu.PrefetchScalarGridSpec(
        num_scalar_prefetch=1,
        grid=(n_blocks,),
        in_specs=[
            pl.BlockSpec((1, 1, MOE_BLOCK), lambda i, be: (0, 0, 0), memory_space=pltpu.SMEM),
            pl.BlockSpec((1, 1, MOE_BLOCK), lambda i, be: (jnp.minimum(i + 1, n_blocks - 1), 0, 0),
                         memory_space=pltpu.SMEM),
            pl.BlockSpec((MOE_BLOCK, LANES), lambda i, be: (i, 0)),
            pl.BlockSpec(memory_space=pl.ANY),
            pl.BlockSpec((None, d, D_EXPERT), lambda i, be: (be[i], 0, 0)),
            pl.BlockSpec((None, d, D_EXPERT), lambda i, be: (be[i], 0, 0)),
            pl.BlockSpec((None, D_EXPERT, d), lambda i, be: (be[i], 0, 0)),
        ],
        out_specs=pl.BlockSpec((MOE_BLOCK, d), lambda i, be: (i, 0)),
        scratch_shapes=[pltpu.VMEM((2, MOE_BLOCK, d), F32), pltpu.SemaphoreType.DMA((2,))],
    )
    return pl.pallas_call(
        _moe_kernel,
        grid_spec=grid_spec,
        out_shape=jax.ShapeDtypeStruct((n_blocks * MOE_BLOCK, d), F32),
        compiler_params=_params("arbitrary"),
        name="moe_experts",
    )(block_e, tok3, tok3, gate_b, h2, w1, w3, w2)


COMBINE_TILE = 256


def _combine_kernel(pos0_ref, posn_ref, x1_ref, g_ref, lng_ref, lnb_ref, ys_hbm, o_ref, ybuf, sem, *, alpha):
    i = pl.program_id(0)
    nb = pl.num_programs(0)

    def row_copy(p, slot, r):
        return pltpu.make_async_copy(ys_hbm.at[pl.ds(p, 1)], ybuf.at[slot, pl.ds(r, 1)], sem.at[slot])

    def issue(pos_ref, slot):
        def body(r, carry):
            row_copy(pos_ref[0, 0, r], slot, r).start()
            return carry
        lax.fori_loop(0, TOP_K * COMBINE_TILE, body, 0, unroll=8)

    @pl.when(i == 0)
    def _():
        issue(pos0_ref, 0)

    @pl.when(i + 1 < nb)
    def _():
        issue(posn_ref, (i + 1) % 2)

    slot = i % 2
    pltpu.make_async_copy(ybuf.at[slot], ybuf.at[slot], sem.at[slot]).wait()
    y = ybuf[slot, 0:COMBINE_TILE, :] + ybuf[slot, COMBINE_TILE:, :]
    o_ref[...] = _layer_norm(alpha * x1_ref[...] + g_ref[...] * y) * lng_ref[...] + lnb_ref[...]


def _combine_call(pos, x1, gvecs, lng, lnb, ys, n_tok, tiles_per_seq, n_seq, alpha):
    d = D_MODEL
    n_tiles = n_tok // COMBINE_TILE
    n_lat_tiles = tiles_per_seq * n_seq

    def g_idx(i):
        return (jnp.where(i < n_lat_tiles, i // tiles_per_seq, n_seq), 0, 0)

    smem_blk = (1, 1, TOP_K * COMBINE_TILE)
    return pl.pallas_call(
        functools.partial(_combine_kernel, alpha=alpha),
        grid=(n_tiles,),
        in_specs=[
            pl.BlockSpec(smem_blk, lambda i: (0, 0, 0), memory_space=pltpu.SMEM),
            pl.BlockSpec(smem_blk, lambda i: (jnp.minimum(i + 1, n_tiles - 1), 0, 0), memory_space=pltpu.SMEM),
            pl.BlockSpec((COMBINE_TILE, d), lambda i: (i, 0)),
            pl.BlockSpec((None, 1, d), g_idx),
            pl.BlockSpec((1, d), lambda i: (0, 0)),
            pl.BlockSpec((1, d), lambda i: (0, 0)),
            pl.BlockSpec(memory_space=pl.ANY),
        ],
        out_specs=pl.BlockSpec((COMBINE_TILE, d), lambda i: (i, 0)),
        out_shape=jax.ShapeDtypeStruct((n_tok, d), F32),
        scratch_shapes=[pltpu.VMEM((2, TOP_K * COMBINE_TILE, d), F32), pltpu.SemaphoreType.DMA((2,))],
        compiler_params=_params("arbitrary"),
        name="moe_combine",
    )(pos, pos, x1, gvecs, lng, lnb, ys)


def _route(logits):
    n = logits.shape[0]
    g_logit = logits[:, 0:N_GROUPS]
    g_val, g_idx = lax.top_k(g_logit, 1)
    p_group = jnp.exp(g_val[:, 0] - jax.nn.logsumexp(g_logit, axis=-1))
    e_logit = logits[:, N_GROUPS:N_GROUPS + N_EXPERTS].reshape(n, N_GROUPS, EXP_PER_GROUP)
    e_logit = jnp.take_along_axis(e_logit, g_idx[:, :, None], axis=1)[:, 0]
    e_val, e_idx = lax.top_k(e_logit, TOP_K)
    gates = p_group[:, None] * jax.nn.softmax(e_val, axis=-1)
    expert = g_idx * EXP_PER_GROUP + e_idx

    a = n * TOP_K
    flat_e = expert.reshape(-1)
    flat_g = gates.reshape(-1)
    order = jnp.argsort(flat_e)
    se = flat_e[order]
    counts = jnp.bincount(flat_e, length=N_EXPERTS)
    starts = jnp.cumsum(counts) - counts
    padded = (counts + MOE_BLOCK - 1) // MOE_BLOCK * MOE_BLOCK
    pends = jnp.cumsum(padded)
    pstarts = pends - padded
    dest = (pstarts[se] + jnp.arange(a) - starts[se]).astype(jnp.int32)
    n_blocks = -(-a // MOE_BLOCK) + N_EXPERTS
    n_slots = n_blocks * MOE_BLOCK
    slot_tok = jnp.zeros(n_slots, jnp.int32).at[dest].set((order // TOP_K).astype(jnp.int32))
    slot_gate = jnp.zeros(n_slots, F32).at[dest].set(flat_g[order])
    block_e = jnp.minimum(jnp.searchsorted(pends, jnp.arange(n_blocks) * MOE_BLOCK, side='right'),
                          N_EXPERTS - 1).astype(jnp.int32)
    pos = jnp.zeros(a, jnp.int32).at[order].set(dest).reshape(n, TOP_K)
    return block_e, slot_tok, slot_gate, pos


def _rope_tables(seq_len):
    m = HEAD_DIM // 4
    freqs = ROPE_BASE ** (-jnp.arange(m, dtype=F32) / m)
    t = jnp.arange(seq_len)
    row = (t // GRID_W).astype(F32)[:, None] * freqs[None, :]
    col = (t % GRID_W).astype(F32)[:, None] * freqs[None, :]
    cos = jnp.concatenate([jnp.cos(row), jnp.cos(row), jnp.cos(col), jnp.cos(col)], axis=-1)
    zero = jnp.zeros_like(row)
    sin_a = jnp.concatenate([zero, jnp.sin(row), zero, jnp.sin(col)], axis=-1)
    sin_b = jnp.concatenate([-jnp.sin(row), zero, -jnp.sin(col), zero], axis=-1)
    rep = LANES // HEAD_DIM
    return jnp.tile(cos, (1, rep)), jnp.tile(sin_a, (1, rep)), jnp.tile(sin_b, (1, rep))


def kernel(x, c, ctx, c_ctx, w_ada, b_ada, w_in, conv_w, attn_sink, gm_ws, gm_bs, w_out, ln1_g, ln1_b,
           w_rg, b_rg, w_re, b_re, w1, w3, w2, ln2_g, ln2_b):
    b_, s_, d_ = x.shape
    c_len = ctx.shape[1]
    depth = w_ada.shape[0]
    alpha = (2 * depth) ** 0.25
    n_lat = b_ * s_
    n_ctx = b_ * c_len
    ts = 512

    cin = jnp.zeros((ADA_ROWS, d_), F32).at[0:b_].set(c).at[b_].set(c_ctx)
    mod = _ada_call(cin, w_ada, b_ada)

    cos, sin_a, sin_b = _rope_tables(s_)
    ones_c = jnp.ones((c_len, LANES), F32)
    zeros_c = jnp.zeros((c_len, LANES), F32)

    x_flat = x.reshape(n_lat, d_)
    c_flat = ctx.reshape(n_ctx, d_)
    x_off, c_off = 0, 0
    x_arr, c_arr = x_flat, c_flat

    for l in range(depth):
        last = l == depth - 1
        mx = mod[l, 0:b_].reshape(b_, 6, 1, d_)
        sh1, sc1, g1, sh2, sc2, g2 = (mx[:, i] for i in range(6))
        mc = jnp.broadcast_to(mod[l, b_].reshape(1, 6, 1, d_), (b_, 6, 1, d_))
        w_in_l = w_in[l].astype(BF16)
        gm_w = gm_ws[l].astype(BF16)
        gm_b = jnp.repeat(gm_bs[l].T, GM_HEAD, axis=1)
        w_out_l = w_out[l].astype(BF16)
        sink = attn_sink[l]

        qx, kx, kxs, vx, vxs, yx = _inproj_call(x_arr, x_off, b_, s_, ts, sc1, sh1, w_in_l, cos, sin_a, sin_b,
                                                conv_w[l], gm_w, gm_b)
        qc, kc, kcs, vc, vcs, yc = _inproj_call(c_arr, c_off, b_, c_len, c_len, mc[:, 1], mc[:, 0], w_in_l,
                                                ones_c, zeros_c, zeros_c, conv_w[l], gm_w, gm_b)
        att_x = _win_attn_call(sink, qx, kx, kxs, vx, vxs, kc, kcs, vc, vcs, b_, s_, c_len)

        w_r = jnp.zeros((d_, ROUTER_COLS), F32).at[:, 0:N_GROUPS].set(w_rg[l]) \
            .at[:, N_GROUPS:N_GROUPS + N_EXPERTS].set(w_re[l])
        w_r_hi = w_r.astype(BF16)
        w_r_lo = (w_r - w_r_hi.astype(F32)).astype(BF16)
        wr = jnp.concatenate([w_r_hi, w_r_lo], axis=1)
        br = jnp.zeros((1, ROUTER_COLS), F32).at[0, 0:N_GROUPS].set(b_rg[l]) \
            .at[0, N_GROUPS:N_GROUPS + N_EXPERTS].set(b_re[l])
        lng1, lnb1 = ln1_g[l].reshape(1, d_), ln1_b[l].reshape(1, d_)
        lng2, lnb2 = ln2_g[l].reshape(1, d_), ln2_b[l].reshape(1, d_)

        n_tot = n_lat if last else n_lat + n_ctx
        gvecs1 = jnp.concatenate([g1, mc[0:1, 2]], axis=0)
        scvecs2 = jnp.concatenate([sc2, mc[0:1, 4]], axis=0)
        shvecs2 = jnp.concatenate([sh2, mc[0:1, 3]], axis=0)
        if last:
            x1, h2, logits = _outproj_call(att_x, att_x, yx, yx, x_arr, x_off, x_arr, x_off, n_lat, 0, s_,
                                           w_out_l, gvecs1, lng1, lnb1, scvecs2, shvecs2, wr, br, alpha)
        else:
            att_c = _ctx_attn_call(sink, qc, kc, kcs, vc, vcs, b_, c_len)
            x1, h2, logits = _outproj_call(att_x, att_c, yx, yc, x_arr, x_off, c_arr, c_off, n_lat, n_ctx, s_,
                                           w_out_l, gvecs1, lng1, lnb1, scvecs2, shvecs2, wr, br, alpha)

        block_e, slot_tok, slot_gate, pos = _route(logits)
        gate_b = jnp.broadcast_to(slot_gate[:, None], (slot_gate.shape[0], LANES))
        ys = _moe_call(block_e, slot_tok, gate_b, h2, w1[l].astype(BF16), w3[l].astype(BF16), w2[l].astype(BF16))

        n_tiles = n_tot // COMBINE_TILE
        pos_t = pos.reshape(n_tiles, COMBINE_TILE, TOP_K).transpose(0, 2, 1).reshape(n_tiles, 1, TOP_K * COMBINE_TILE)
        gvecs = jnp.concatenate([g2, mc[0:1, 5]], axis=0)
        out = _combine_call(pos_t, x1, gvecs, lng2, lnb2, ys, n_tot, s_ // COMBINE_TILE, b_, alpha)
        x_arr, x_off = out, 0
        c_arr, c_off = out, n_lat

    return x_arr.reshape(b_, s_, d_)
```

```python
import functools

import jax
import jax.numpy as jnp
from jax import lax
from jax.experimental import pallas as pl
from jax.experimental.pallas import tpu as pltpu

F32 = jnp.float32
BF16 = jnp.bfloat16

D_MODEL = 1024
GRID_W = 64
HEAD_DIM = 64
N_HEADS = 8
N_KV_HEADS = 2
ATTN_WIDTH = N_HEADS * HEAD_DIM
KV_WIDTH = N_KV_HEADS * HEAD_DIM
WINDOW = 128
ATTN_SCALE = HEAD_DIM ** -0.5
ROPE_BASE = 10000.0
CONV_WIDTH = D_MODEL // 4
GM_WIDTH = D_MODEL // 4
GM_GROUPS = 4
GM_HEAD = GM_WIDTH // GM_GROUPS
CHUNK = 128
LOCAL_WIDTH = CONV_WIDTH + GM_WIDTH
IN_WIDTH = ATTN_WIDTH + 2 * KV_WIDTH + 3 * CONV_WIDTH + 2 * GM_WIDTH
QKV_WIDTH = ATTN_WIDTH + 2 * KV_WIDTH
N_GROUPS = 4
EXP_PER_GROUP = 8
N_EXPERTS = N_GROUPS * EXP_PER_GROUP
TOP_K = 2
D_EXPERT = D_MODEL // 2
MOE_BLOCK = 128
LN_EPS = 1e-6
NEG_INF = -1e30

LANES = 128
SUBLANES = 8
VMEM_LIMIT_BYTES = 48 * 1024 * 1024

ROUTER_COLS = LANES
ROUTE_EXPERT = 0
ROUTE_GATE = TOP_K
ROUTE_RANK = 2 * TOP_K
ROPE_HALF_PAIR = HEAD_DIM // 4


def _bdot(a, b):
    return jnp.dot(a, b, preferred_element_type=F32)


def _split_bf16(a):
    hi = a.astype(BF16)
    lo = (a - hi.astype(F32)).astype(BF16)
    return hi, lo


def _layer_norm(r):
    mu = jnp.mean(r, axis=-1, keepdims=True)
    d = r - mu
    var = jnp.mean(d * d, axis=-1, keepdims=True)
    return d * lax.rsqrt(var + LN_EPS)


def _params(*sem):
    return pltpu.CompilerParams(dimension_semantics=sem, vmem_limit_bytes=VMEM_LIMIT_BYTES)


ADA_ROWS = 16
ADA_TILE = 1536


def _ada_kernel(c_ref, w_ref, b_ref, o_ref):
    act = jax.nn.silu(c_ref[...])
    ah, al = _split_bf16(act)
    wh, wl = _split_bf16(w_ref[...])
    o_ref[...] = _bdot(ah, wh) + _bdot(ah, wl) + _bdot(al, wh) + b_ref[...]


def _ada_call(cin, w_ada, b_ada):
    depth, d, n = w_ada.shape
    return pl.pallas_call(
        _ada_kernel,
        grid=(depth, n // ADA_TILE),
        in_specs=[
            pl.BlockSpec((ADA_ROWS, d), lambda l, j: (0, 0)),
            pl.BlockSpec((None, d, ADA_TILE), lambda l, j: (l, 0, j)),
            pl.BlockSpec((None, 1, ADA_TILE), lambda l, j: (l, 0, j)),
        ],
        out_specs=pl.BlockSpec((None, ADA_ROWS, ADA_TILE), lambda l, j: (l, 0, j)),
        out_shape=jax.ShapeDtypeStruct((depth, ADA_ROWS, n), F32),
        compiler_params=_params("parallel", "parallel"),
        name="ada_modulation",
    )(cin, w_ada, b_ada.reshape(depth, 1, n))


def _inproj_kernel(x_ref, xp_ref, xn_ref, sc_ref, sh_ref, w_ref, cos_ref, sa_ref, sb_ref,
                   cw_ref, gw_ref, gb_ref,
                   q_ref, k_ref, ksw_ref, v_ref, vsw_ref, y_ref, *, ts):
    t = pl.program_id(1)
    nt = pl.num_programs(1)
    sc = 1.0 + sc_ref[...]
    sh = sh_ref[...]
    hx = (x_ref[...] * sc + sh).astype(BF16)

    pq = _bdot(hx, w_ref[:, 0:QKV_WIDTH])
    cos = cos_ref[...]
    sa = sa_ref[...]
    sb = sb_ref[...]

    def rope(z):
        return (z * cos + pltpu.roll(z, ROPE_HALF_PAIR, 1) * sa
                + pltpu.roll(z, LANES - ROPE_HALF_PAIR, 1) * sb)

    for j in range(ATTN_WIDTH // LANES):
        sl = slice(j * LANES, (j + 1) * LANES)
        q_ref[:, sl] = (rope(pq[:, sl]) * ATTN_SCALE).astype(BF16)
    kr = rope(pq[:, ATTN_WIDTH:ATTN_WIDTH + KV_WIDTH])
    k_ref[...] = kr.astype(BF16)
    ksw_ref[...] = pltpu.roll(kr, HEAD_DIM, 1).astype(BF16)
    vv = pq[:, ATTN_WIDTH + KV_WIDTH:QKV_WIDTH]
    v_ref[...] = vv.astype(BF16)
    vsw_ref[...] = pltpu.roll(vv, HEAD_DIM, 1).astype(BF16)

    pm = _bdot(hx, w_ref[:, QKV_WIDTH:IN_WIDTH])
    cb = pm[:, 0:CONV_WIDTH]
    u = pm[:, CONV_WIDTH:2 * CONV_WIDTH] * pm[:, 2 * CONV_WIDTH:3 * CONV_WIDTH]

    halo = jnp.concatenate([xp_ref[...], xn_ref[...]], axis=0)
    hh = (halo * sc + sh).astype(BF16)
    ph = _bdot(hh, w_ref[:, QKV_WIDTH + CONV_WIDTH:QKV_WIDTH + 3 * CONV_WIDTH])
    uh = ph[:, 0:CONV_WIDTH] * ph[:, CONV_WIDTH:2 * CONV_WIDTH]
    up_row = jnp.where(t > 0, uh[SUBLANES - 1:SUBLANES, :], 0.0)
    dn_row = jnp.where(t < nt - 1, uh[SUBLANES:SUBLANES + 1, :], 0.0)
    ridx = lax.broadcasted_iota(jnp.int32, (ts, CONV_WIDTH), 0)
    u_up = jnp.where(ridx == 0, up_row, pltpu.roll(u, 1, 0))
    u_dn = jnp.where(ridx == ts - 1, dn_row, pltpu.roll(u, ts - 1, 0))
    cw = cw_ref[...]
    y_conv = cb * (u_up * cw[0:1, :] + u * cw[1:2, :] + u_dn * cw[2:3, :])
    y_ref[:, 0:CONV_WIDTH] = y_conv.astype(BF16)

    gu = jax.nn.gelu(pm[:, 3 * CONV_WIDTH:3 * CONV_WIDTH + GM_WIDTH])
    gv = _layer_norm(jax.nn.gelu(pm[:, 3 * CONV_WIDTH + GM_WIDTH:3 * CONV_WIDTH + 2 * GM_WIDTH])).astype(BF16)
    lane = lax.broadcasted_iota(jnp.int32, (CHUNK, LANES), 1)
    zero = jnp.zeros((CHUNK, LANES), BF16)
    for c in range(ts // CHUNK):
        rows = slice(c * CHUNK, (c + 1) * CHUNK)
        for j in range(GM_WIDTH // LANES):
            cols = slice(j * LANES, (j + 1) * LANES)
            vp = gv[rows, cols]
            s = (_bdot(gw_ref[2 * j], jnp.where(lane < GM_HEAD, vp, zero))
                 + _bdot(gw_ref[2 * j + 1], jnp.where(lane >= GM_HEAD, vp, zero))
                 + gb_ref[:, cols])
            y_ref[rows, CONV_WIDTH + j * LANES:CONV_WIDTH + (j + 1) * LANES] = (gu[rows, cols] * s).astype(BF16)


def _inproj_call(x2d, row_off, n_seq, seq_len, ts, sc, sh, w_in, cos, sa, sb, conv_w, gm_w, gm_b):
    nt = seq_len // ts
    off_t = row_off // ts
    off_8 = row_off // SUBLANES
    last_8 = x2d.shape[0] // SUBLANES - 1
    per_tile_8 = ts // SUBLANES
    n_out = n_seq * seq_len
    d = D_MODEL

    def tile_idx(b, t):
        return (off_t + b * nt + t, 0)

    def prev_idx(b, t):
        return (jnp.maximum(off_8 + (b * nt + t) * per_tile_8 - 1, 0), 0)

    def next_idx(b, t):
        return (jnp.minimum(off_8 + (b * nt + t + 1) * per_tile_8, last_8), 0)

    def out_idx(b, t):
        return (b * nt + t, 0)

    vec = pl.BlockSpec((None, 1, d), lambda b, t: (b, 0, 0))
    rope_spec = pl.BlockSpec((ts, LANES), lambda b, t: (t, 0))
    kv_spec = pl.BlockSpec((ts, KV_WIDTH), out_idx)
    kv_shape = jax.ShapeDtypeStruct((n_out, KV_WIDTH), BF16)
    return pl.pallas_call(
        functools.partial(_inproj_kernel, ts=ts),
        grid=(n_seq, nt),
        in_specs=[
            pl.BlockSpec((ts, d), tile_idx),
            pl.BlockSpec((SUBLANES, d), prev_idx),
            pl.BlockSpec((SUBLANES, d), next_idx),
            vec, vec,
            pl.BlockSpec((d, IN_WIDTH), lambda b, t: (0, 0)),
            rope_spec, rope_spec, rope_spec,
            pl.BlockSpec((3, CONV_WIDTH), lambda b, t: (0, 0)),
            pl.BlockSpec((GM_GROUPS, CHUNK, CHUNK), lambda b, t: (0, 0, 0)),
            pl.BlockSpec((CHUNK, GM_WIDTH), lambda b, t: (0, 0)),
        ],
        out_specs=[
            pl.BlockSpec((ts, ATTN_WIDTH), out_idx),
            kv_spec, kv_spec, kv_spec, kv_spec,
            pl.BlockSpec((ts, LOCAL_WIDTH), out_idx),
        ],
        out_shape=[
            jax.ShapeDtypeStruct((n_out, ATTN_WIDTH), BF16),
            kv_shape, kv_shape, kv_shape, kv_shape,
            jax.ShapeDtypeStruct((n_out, LOCAL_WIDTH), BF16),
        ],
        compiler_params=_params("parallel", "parallel"),
        name="in_projection",
    )(x2d, x2d, x2d, sc, sh, w_in, cos, sa, sb, conv_w, gm_w, gm_b)


def _attn_body(q, keys, keys_sw, vals, vals_sw, bias, sink_ref, o_ref, tq):
    nk = keys.shape[0]
    lane = lax.broadcasted_iota(jnp.int32, (nk, LANES), 1)
    lo = lane < HEAD_DIM
    zero = jnp.zeros((nk, LANES), BF16)
    rid = lax.broadcasted_iota(jnp.int32, (2 * tq, 1), 0)
    nt_dims = (((1,), (1,)), ((), ()))
    gqa = N_HEADS // N_KV_HEADS
    for h in range(N_KV_HEADS):
        k_own, k_oth = (keys, keys_sw) if h == 0 else (keys_sw, keys)
        v_own, v_oth = (vals, vals_sw) if h == 0 else (vals_sw, vals)
        kz = (jnp.where(lo, k_own, zero), jnp.where(lo, zero, k_oth))
        vz = (jnp.where(lo, v_own, zero), jnp.where(lo, zero, v_oth))
        c0 = h * gqa * HEAD_DIM
        qs = jnp.concatenate([q[:, c0:c0 + LANES], q[:, c0 + LANES:c0 + 2 * LANES]], axis=0)
        acc = None
        for par in range(2):
            s = lax.dot_general(qs, kz[par], nt_dims, preferred_element_type=F32)
            if bias is not None:
                s = s + bias
            sink = jnp.where(rid < tq, sink_ref[h * gqa + par], sink_ref[h * gqa + 2 + par])
            m = jnp.maximum(jnp.max(s, axis=-1, keepdims=True), sink)
            p = jnp.exp(s - m)
            den = jnp.sum(p, axis=-1, keepdims=True) + jnp.exp(sink - m)
            o = _bdot(p.astype(BF16), vz[par]) / den
            acc = o if acc is None else acc + o
        o_ref[:, c0:c0 + LANES] = acc[0:tq].astype(BF16)
        o_ref[:, c0 + LANES:c0 + 2 * LANES] = acc[tq:2 * tq].astype(BF16)


def _win_attn_kernel(sink_ref, q_ref, kp_ref, kc_ref, kn_ref, kx_ref,
                     ksp_ref, ksc_ref, ksn_ref, ksx_ref,
                     vp_ref, vc_ref, vn_ref, vx_ref,
                     vsp_ref, vsc_ref, vsn_ref, vsx_ref, o_ref, *, tq, ctx_len):
    n = pl.program_id(1)
    nb = pl.num_programs(1)
    row = lax.broadcasted_iota(jnp.int32, (tq, tq), 0)
    col = lax.broadcasted_iota(jnp.int32, (tq, tq), 1)
    b_prev = jnp.where((col >= row) & (n > 0), 0.0, NEG_INF)
    b_next = jnp.where((col <= row) & (n < nb - 1), 0.0, NEG_INF)
    b_half = jnp.concatenate([b_prev, jnp.zeros((tq, tq), F32), b_next, jnp.zeros((tq, ctx_len), F32)], axis=1)
    bias = jnp.concatenate([b_half, b_half], axis=0)

    def cat(a, b, c, d):
        return jnp.concatenate([a[...], b[...], c[...], d[...]], axis=0)

    _attn_body(q_ref[...], cat(kp_ref, kc_ref, kn_ref, kx_ref), cat(ksp_ref, ksc_ref, ksn_ref, ksx_ref),
               cat(vp_ref, vc_ref, vn_ref, vx_ref), cat(vsp_ref, vsc_ref, vsn_ref, vsx_ref),
               bias, sink_ref, o_ref, tq)


def _win_attn_call(sink, q, k, ksw, v, vsw, kx, kxsw, vx, vxsw, n_seq, seq_len, ctx_len):
    tq = WINDOW
    nb = seq_len // tq

    def cur(b, n):
        return (b * nb + n, 0)

    def prev(b, n):
        return (b * nb + jnp.maximum(n - 1, 0), 0)

    def nxt(b, n):
        return (b * nb + jnp.minimum(n + 1, nb - 1), 0)

    def ctx(b, n):
        return (b, 0)

    def kv_specs():
        return [pl.BlockSpec((tq, KV_WIDTH), prev), pl.BlockSpec((tq, KV_WIDTH), cur),
                pl.BlockSpec((tq, KV_WIDTH), nxt), pl.BlockSpec((ctx_len, KV_WIDTH), ctx)]

    return pl.pallas_call(
        functools.partial(_win_attn_kernel, tq=tq, ctx_len=ctx_len),
        grid=(n_seq, nb),
        in_specs=[pl.BlockSpec(memory_space=pltpu.SMEM), pl.BlockSpec((tq, ATTN_WIDTH), cur)]
        + kv_specs() + kv_specs() + kv_specs() + kv_specs(),
        out_specs=pl.BlockSpec((tq, ATTN_WIDTH), cur),
        out_shape=jax.ShapeDtypeStruct((n_seq * seq_len, ATTN_WIDTH), BF16),
        compiler_params=_params("parallel", "parallel"),
        name="window_attention",
    )(sink, q, k, k, k, kx, ksw, ksw, ksw, kxsw, v, v, v, vx, vsw, vsw, vsw, vxsw)


def _ctx_attn_kernel(sink_ref, q_ref, k_ref, ks_ref, v_ref, vs_ref, o_ref, *, tq):
    _attn_body(q_ref[...], k_ref[...], ks_ref[...], v_ref[...], vs_ref[...], None, sink_ref, o_ref, tq)


def _ctx_attn_call(sink, q, k, ksw, v, vsw, n_seq, ctx_len):
    def blk(w):
        return pl.BlockSpec((ctx_len, w), lambda b: (b, 0))

    return pl.pallas_call(
        functools.partial(_ctx_attn_kernel, tq=ctx_len),
        grid=(n_seq,),
        in_specs=[pl.BlockSpec(memory_space=pltpu.SMEM), blk(ATTN_WIDTH),
                  blk(KV_WIDTH), blk(KV_WIDTH), blk(KV_WIDTH), blk(KV_WIDTH)],
        out_specs=blk(ATTN_WIDTH),
        out_shape=jax.ShapeDtypeStruct((n_seq * ctx_len, ATTN_WIDTH), BF16),
        compiler_params=_params("parallel"),
        name="context_attention",
    )(sink, q, k, ksw, v, vsw)


OUT_TILE = 256


def _outproj_kernel(attx_ref, attc_ref, yx_ref, yc_ref, xx_ref, xc_ref, wo_ref, g_ref, lng_ref, lnb_ref,
                    sc_ref, sh_ref, wr_ref, br_ref, x1_ref, h2p_ref, route_ref, cnt_ref, *, alpha, n_lat_tiles):
    is_lat = pl.program_id(0) < n_lat_tiles
    att = jnp.where(is_lat, attx_ref[...], attc_ref[...])
    yloc = jnp.where(is_lat, yx_ref[...], yc_ref[...])
    xin = jnp.where(is_lat, xx_ref[...], xc_ref[...])
    o = _bdot(att, wo_ref[0:ATTN_WIDTH, :]) + _bdot(yloc, wo_ref[ATTN_WIDTH:, :])
    x1 = _layer_norm(alpha * xin + g_ref[...] * o) * lng_ref[...] + lnb_ref[...]
    x1_ref[...] = x1
    h2 = x1 * (1.0 + sc_ref[...]) + sh_ref[...]
    half = D_MODEL // 2
    hi_bits = lax.bitcast_convert_type(h2[:, 0:half].astype(BF16).astype(F32), jnp.uint32)
    lo_bits = lax.bitcast_convert_type(h2[:, half:].astype(BF16).astype(F32), jnp.uint32)
    h2p_ref[...] = hi_bits | (lo_bits >> 16)

    hh, hl = _split_bf16(h2)
    t1 = _bdot(hh, wr_ref[...])
    lg = t1[:, 0:ROUTER_COLS] + t1[:, ROUTER_COLS:] + _bdot(hl, wr_ref[:, 0:ROUTER_COLS]) + br_ref[...]

    ts = lg.shape[0]
    lane = lax.broadcasted_iota(jnp.int32, (ts, ROUTER_COLS), 1).astype(F32)
    big = jnp.float32(ROUTER_COLS)

    def top1(v):
        m = jnp.max(v, axis=-1, keepdims=True)
        return m, jnp.min(jnp.where(v == m, lane, big), axis=-1, keepdims=True)

    gl = jnp.where(lane < N_GROUPS, lg, NEG_INF)
    g_val, g_idx = top1(gl)
    lse = g_val + jnp.log(jnp.sum(jnp.exp(gl - g_val), axis=-1, keepdims=True))
    p_group = jnp.exp(g_val - lse)
    e_lo = N_GROUPS + EXP_PER_GROUP * g_idx
    el = jnp.where((lane >= e_lo) & (lane < e_lo + EXP_PER_GROUP), lg, NEG_INF)
    e1, l1 = top1(el)
    e2, l2 = top1(jnp.where(lane == l1, NEG_INF, el))
    z = jnp.exp(e2 - e1)
    gate1 = p_group / (1.0 + z)
    gate2 = p_group * z / (1.0 + z)
    x1id = l1 - N_GROUPS
    x2id = l2 - N_GROUPS

    sel1 = lane == x1id
    sel2 = lane == x2id
    onehot = jnp.where(sel1 | sel2, 1.0, 0.0)
    r_i = lax.broadcasted_iota(jnp.int32, (ts, ts), 0)
    c_i = lax.broadcasted_iota(jnp.int32, (ts, ts), 1)
    before = jnp.where(c_i < r_i, 1.0, 0.0).astype(BF16)
    prefix = _bdot(before, onehot.astype(BF16))
    rank1 = jnp.sum(jnp.where(sel1, prefix, 0.0), axis=-1, keepdims=True)
    rank2 = jnp.sum(jnp.where(sel2, prefix, 0.0), axis=-1, keepdims=True)
    route = jnp.zeros((ts, ROUTER_COLS), F32)
    for col, val in enumerate((x1id, x2id, gate1, gate2, rank1, rank2)):
        route = jnp.where(lane == col, val, route)
    route_ref[...] = route
    cnt_ref[...] = jnp.broadcast_to(jnp.sum(onehot, axis=0, keepdims=True), (SUBLANES, ROUTER_COLS))


def _outproj_call(att_x, att_c, y_x, y_c, x_arr, x_off, c_arr, c_off, n_lat, n_ctx, seq_len,
                  w_out, gvecs, lng, lnb, scvecs, shvecs, wr, br, alpha):
    ts = OUT_TILE
    d = D_MODEL
    n_lat_tiles = n_lat // ts
    n_tiles = (n_lat + n_ctx) // ts
    tiles_per_seq = seq_len // ts
    n_seq = n_lat // seq_len
    xo, co = x_off // ts, c_off // ts

    def lat_loc(i):
        return (jnp.minimum(i, n_lat_tiles - 1), 0)

    def ctx_loc(i):
        return (jnp.maximum(i - n_lat_tiles, 0), 0)

    def lat_in(i):
        return (xo + jnp.minimum(i, n_lat_tiles - 1), 0)

    def ctx_in(i):
        return (co + jnp.maximum(i - n_lat_tiles, 0), 0)

    def vec_idx(i):
        return (jnp.where(i < n_lat_tiles, i // tiles_per_seq, n_seq), 0, 0)

    vecb = pl.BlockSpec((None, 1, d), vec_idx)
    vec0 = pl.BlockSpec((1, d), lambda i: (0, 0))
    return pl.pallas_call(
        functools.partial(_outproj_kernel, alpha=alpha, n_lat_tiles=n_lat_tiles),
        grid=(n_tiles,),
        in_specs=[
            pl.BlockSpec((ts, ATTN_WIDTH), lat_loc), pl.BlockSpec((ts, ATTN_WIDTH), ctx_loc),
            pl.BlockSpec((ts, LOCAL_WIDTH), lat_loc), pl.BlockSpec((ts, LOCAL_WIDTH), ctx_loc),
            pl.BlockSpec((ts, d), lat_in), pl.BlockSpec((ts, d), ctx_in),
            pl.BlockSpec((d, d), lambda i: (0, 0)),
            vecb, vec0, vec0, vecb, vecb,
            pl.BlockSpec((d, 2 * ROUTER_COLS), lambda i: (0, 0)),
            pl.BlockSpec((1, ROUTER_COLS), lambda i: (0, 0)),
        ],
        out_specs=[pl.BlockSpec((ts, d), lambda i: (i, 0)), pl.BlockSpec((ts, d // 2), lambda i: (i, 0)),
                   pl.BlockSpec((ts, ROUTER_COLS), lambda i: (i, 0)),
                   pl.BlockSpec((None, SUBLANES, ROUTER_COLS), lambda i: (i, 0, 0))],
        out_shape=[jax.ShapeDtypeStruct((n_lat + n_ctx, d), F32),
                   jax.ShapeDtypeStruct((n_lat + n_ctx, d // 2), jnp.uint32),
                   jax.ShapeDtypeStruct((n_lat + n_ctx, ROUTER_COLS), F32),
                   jax.ShapeDtypeStruct((n_tiles, SUBLANES, ROUTER_COLS), F32)],
        compiler_params=_params("parallel"),
        name="out_projection",
    )(att_x, att_c, y_x, y_c, x_arr, c_arr, w_out, gvecs, lng, lnb, scvecs, shvecs, wr, br)


DISPATCH_TILE = 512


def _dispatch_kernel(pos_ref, h_ref, xs_in, xs_out, sem):
    del xs_in

    def body(t, carry):
        for k in range(TOP_K):
            pltpu.make_async_copy(h_ref.at[pl.ds(t, 1)], xs_out.at[pl.ds(pos_ref[0, 0, TOP_K * t + k], 1)],
                                  sem).start()
        return carry

    lax.fori_loop(0, DISPATCH_TILE, body, 0, unroll=4)
    for k in range(TOP_K):
        pltpu.make_async_copy(h_ref, h_ref, sem).wait()


def _dispatch_call(pos, h2p, n_slots):
    n_tok, half = h2p.shape
    n_tiles = n_tok // DISPATCH_TILE
    xs0 = jnp.zeros((n_slots, half), jnp.uint32)
    return pl.pallas_call(
        _dispatch_kernel,
        grid=(n_tiles,),
        in_specs=[
            pl.BlockSpec((1, 1, TOP_K * DISPATCH_TILE), lambda i: (i, 0, 0), memory_space=pltpu.SMEM),
            pl.BlockSpec((DISPATCH_TILE, half), lambda i: (i, 0)),
            pl.BlockSpec(memory_space=pl.ANY),
        ],
        out_specs=pl.BlockSpec(memory_space=pl.ANY),
        out_shape=jax.ShapeDtypeStruct((n_slots, half), jnp.uint32),
        scratch_shapes=[pltpu.SemaphoreType.DMA(())],
        input_output_aliases={2: 0},
        compiler_params=_params("arbitrary"),
        name="moe_dispatch",
    )(pos, h2p, xs0)


def _moe_kernel(be_ref, nu_ref, xs_ref, w1_ref, w3_ref, w2_ref, o_ref):
    i = pl.program_id(0)
    half = D_MODEL // 2

    @pl.when(i < nu_ref[0])
    def _():
        p = xs_ref[...]
        xa = lax.bitcast_convert_type(p & jnp.uint32(0xFFFF0000), F32).astype(BF16)
        xb = lax.bitcast_convert_type(p << 16, F32).astype(BF16)
        h1 = _bdot(xa, w1_ref[0:half, :]) + _bdot(xb, w1_ref[half:, :])
        h3 = _bdot(xa, w3_ref[0:half, :]) + _bdot(xb, w3_ref[half:, :])
        o_ref[...] = _bdot((jax.nn.silu(h1) * h3).astype(BF16), w2_ref[...])

    @pl.when(i >= nu_ref[0])
    def _():
        o_ref[...] = jnp.zeros_like(o_ref)


def _moe_call(block_e, n_used, xs, w1, w3, w2):
    n_blocks = block_e.shape[0]
    d = D_MODEL
    grid_spec = pltpu.PrefetchScalarGridSpec(
        num_scalar_prefetch=2,
        grid=(n_blocks,),
        in_specs=[
            pl.BlockSpec((MOE_BLOCK, d // 2), lambda i, be, nu: (i, 0)),
            pl.BlockSpec((None, d, D_EXPERT), lambda i, be, nu: (be[i], 0, 0)),
            pl.BlockSpec((None, d, D_EXPERT), lambda i, be, nu: (be[i], 0, 0)),
            pl.BlockSpec((None, D_EXPERT, d), lambda i, be, nu: (be[i], 0, 0)),
        ],
        out_specs=pl.BlockSpec((MOE_BLOCK, d), lambda i, be, nu: (i, 0)),
    )
    return pl.pallas_call(
        _moe_kernel,
        grid_spec=grid_spec,
        out_shape=jax.ShapeDtypeStruct((n_blocks * MOE_BLOCK, d), F32),
        compiler_params=_params("arbitrary"),
        name="moe_experts",
    )(block_e, n_used, xs, w1, w3, w2)


COMBINE_TILE = 256


def _combine_kernel(pos0_ref, posn_ref, x1_ref, route_ref, g_ref, lng_ref, lnb_ref, ys_hbm, o_ref, ybuf, sem,
                    *, alpha):
    i = pl.program_id(0)
    nb = pl.num_programs(0)

    def row_copy(p, slot, r):
        return pltpu.make_async_copy(ys_hbm.at[pl.ds(p, 1)], ybuf.at[slot, pl.ds(r, 1)], sem.at[slot])

    def issue(pos_ref, slot):
        def body(r, carry):
            row_copy(pos_ref[0, 0, r], slot, r).start()
            return carry
        lax.fori_loop(0, TOP_K * COMBINE_TILE, body, 0, unroll=8)

    @pl.when(i == 0)
    def _():
        issue(pos0_ref, 0)

    @pl.when(i + 1 < nb)
    def _():
        issue(posn_ref, (i + 1) % 2)

    slot = i % 2
    pltpu.make_async_copy(ybuf.at[slot], ybuf.at[slot], sem.at[slot]).wait()
    route = route_ref[...]
    y = (ybuf[slot, 0:COMBINE_TILE, :] * route[:, ROUTE_GATE:ROUTE_GATE + 1]
         + ybuf[slot, COMBINE_TILE:, :] * route[:, ROUTE_GATE + 1:ROUTE_GATE + 2])
    o_ref[...] = _layer_norm(alpha * x1_ref[...] + g_ref[...] * y) * lng_ref[...] + lnb_ref[...]


def _combine_call(pos, x1, route, gvecs, lng, lnb, ys, n_tok, tiles_per_seq, n_seq, alpha):
    d = D_MODEL
    n_tiles = n_tok // COMBINE_TILE
    n_lat_tiles = tiles_per_seq * n_seq

    def g_idx(i):
        return (jnp.where(i < n_lat_tiles, i // tiles_per_seq, n_seq), 0, 0)

    smem_blk = (1, 1, TOP_K * COMBINE_TILE)
    return pl.pallas_call(
        functools.partial(_combine_kernel, alpha=alpha),
        grid=(n_tiles,),
        in_specs=[
            pl.BlockSpec(smem_blk, lambda i: (0, 0, 0), memory_space=pltpu.SMEM),
            pl.BlockSpec(smem_blk, lambda i: (jnp.minimum(i + 1, n_tiles - 1), 0, 0), memory_space=pltpu.SMEM),
            pl.BlockSpec((COMBINE_TILE, d), lambda i: (i, 0)),
            pl.BlockSpec((COMBINE_TILE, ROUTER_COLS), lambda i: (i, 0)),
            pl.BlockSpec((None, 1, d), g_idx),
            pl.BlockSpec((1, d), lambda i: (0, 0)),
            pl.BlockSpec((1, d), lambda i: (0, 0)),
            pl.BlockSpec(memory_space=pl.ANY),
        ],
        out_specs=pl.BlockSpec((COMBINE_TILE, d), lambda i: (i, 0)),
        out_shape=jax.ShapeDtypeStruct((n_tok, d), F32),
        scratch_shapes=[pltpu.VMEM((2, TOP_K * COMBINE_TILE, d), F32), pltpu.SemaphoreType.DMA((2,))],
        compiler_params=_params("arbitrary"),
        name="moe_combine",
    )(pos, pos, x1, route, gvecs, lng, lnb, ys)


def _slots(route, tile_counts):
    n = route.shape[0]
    n_tiles = tile_counts.shape[0]
    tc = tile_counts.astype(jnp.int32)
    counts = jnp.sum(tc, axis=0)
    padded = (counts + MOE_BLOCK - 1) // MOE_BLOCK * MOE_BLOCK
    pends = jnp.cumsum(padded)
    base = (pends - padded)[None, :] + jnp.cumsum(tc, axis=0) - tc
    n_blocks = -(-n * TOP_K // MOE_BLOCK) + N_EXPERTS
    block_e = jnp.minimum(jnp.searchsorted(pends[0:N_EXPERTS], jnp.arange(n_blocks) * MOE_BLOCK, side='right'),
                          N_EXPERTS - 1).astype(jnp.int32)
    n_used = (pends[N_EXPERTS - 1] // MOE_BLOCK).astype(jnp.int32).reshape(1)
    r3 = route.reshape(n_tiles, n // n_tiles, ROUTER_COLS)
    lane = lax.broadcasted_iota(jnp.int32, r3.shape, 2)
    pos = []
    for k in range(TOP_K):
        e = r3[:, :, ROUTE_EXPERT + k].astype(jnp.int32)
        b = jnp.sum(jnp.where(lane == e[:, :, None], base[:, None, :], 0), axis=-1)
        pos.append((b + r3[:, :, ROUTE_RANK + k].astype(jnp.int32)).reshape(n))
    return block_e, n_used, jnp.stack(pos, axis=-1)


def _rope_tables(seq_len):
    m = HEAD_DIM // 4
    freqs = ROPE_BASE ** (-jnp.arange(m, dtype=F32) / m)
    t = jnp.arange(seq_len)
    row = (t // GRID_W).astype(F32)[:, None] * freqs[None, :]
    col = (t % GRID_W).astype(F32)[:, None] * freqs[None, :]
    cos = jnp.concatenate([jnp.cos(row), jnp.cos(row), jnp.cos(col), jnp.cos(col)], axis=-1)
    zero = jnp.zeros_like(row)
    sin_a = jnp.concatenate([zero, jnp.sin(row), zero, jnp.sin(col)], axis=-1)
    sin_b = jnp.concatenate([-jnp.sin(row), zero, -jnp.sin(col), zero], axis=-1)
    rep = LANES // HEAD_DIM
    return jnp.tile(cos, (1, rep)), jnp.tile(sin_a, (1, rep)), jnp.tile(sin_b, (1, rep))


def kernel(x, c, ctx, c_ctx, w_ada, b_ada, w_in, conv_w, attn_sink, gm_ws, gm_bs, w_out, ln1_g, ln1_b,
           w_rg, b_rg, w_re, b_re, w1, w3, w2, ln2_g, ln2_b):
    b_, s_, d_ = x.shape
    c_len = ctx.shape[1]
    depth = w_ada.shape[0]
    alpha = (2 * depth) ** 0.25
    n_lat = b_ * s_
    n_ctx = b_ * c_len
    ts = 512

    cin = jnp.zeros((ADA_ROWS, d_), F32).at[0:b_].set(c).at[b_].set(c_ctx)
    mod = _ada_call(cin, w_ada, b_ada)

    cos, sin_a, sin_b = _rope_tables(s_)
    ones_c = jnp.ones((c_len, LANES), F32)
    zeros_c = jnp.zeros((c_len, LANES), F32)

    x_flat = x.reshape(n_lat, d_)
    c_flat = ctx.reshape(n_ctx, d_)
    x_off, c_off = 0, 0
    x_arr, c_arr = x_flat, c_flat

    for l in range(depth):
        last = l == depth - 1
        mx = mod[l, 0:b_].reshape(b_, 6, 1, d_)
        sh1, sc1, g1, sh2, sc2, g2 = (mx[:, i] for i in range(6))
        mc = jnp.broadcast_to(mod[l, b_].reshape(1, 6, 1, d_), (b_, 6, 1, d_))
        w_in_l = w_in[l].astype(BF16)
        gm_w = gm_ws[l].astype(BF16)
        gm_b = jnp.repeat(gm_bs[l].T, GM_HEAD, axis=1)
        w_out_l = w_out[l].astype(BF16)
        sink = attn_sink[l]

        qx, kx, kxs, vx, vxs, yx = _inproj_call(x_arr, x_off, b_, s_, ts, sc1, sh1, w_in_l, cos, sin_a, sin_b,
                                                conv_w[l], gm_w, gm_b)
        qc, kc, kcs, vc, vcs, yc = _inproj_call(c_arr, c_off, b_, c_len, c_len, mc[:, 1], mc[:, 0], w_in_l,
                                                ones_c, zeros_c, zeros_c, conv_w[l], gm_w, gm_b)
        att_x = _win_attn_call(sink, qx, kx, kxs, vx, vxs, kc, kcs, vc, vcs, b_, s_, c_len)

        w_r = jnp.zeros((d_, ROUTER_COLS), F32).at[:, 0:N_GROUPS].set(w_rg[l]) \
            .at[:, N_GROUPS:N_GROUPS + N_EXPERTS].set(w_re[l])
        w_r_hi = w_r.astype(BF16)
        w_r_lo = (w_r - w_r_hi.astype(F32)).astype(BF16)
        wr = jnp.concatenate([w_r_hi, w_r_lo], axis=1)
        br = jnp.zeros((1, ROUTER_COLS), F32).at[0, 0:N_GROUPS].set(b_rg[l]) \
            .at[0, N_GROUPS:N_GROUPS + N_EXPERTS].set(b_re[l])
        lng1, lnb1 = ln1_g[l].reshape(1, d_), ln1_b[l].reshape(1, d_)
        lng2, lnb2 = ln2_g[l].reshape(1, d_), ln2_b[l].reshape(1, d_)

        n_tot = n_lat if last else n_lat + n_ctx
        gvecs1 = jnp.concatenate([g1, mc[0:1, 2]], axis=0)
        scvecs2 = jnp.concatenate([sc2, mc[0:1, 4]], axis=0)
        shvecs2 = jnp.concatenate([sh2, mc[0:1, 3]], axis=0)
        if last:
            x1, h2p, route, tcnt = _outproj_call(att_x, att_x, yx, yx, x_arr, x_off, x_arr, x_off, n_lat, 0, s_,
                                                 w_out_l, gvecs1, lng1, lnb1, scvecs2, shvecs2, wr, br, alpha)
        else:
            att_c = _ctx_attn_call(sink, qc, kc, kcs, vc, vcs, b_, c_len)
            x1, h2p, route, tcnt = _outproj_call(att_x, att_c, yx, yc, x_arr, x_off, c_arr, c_off, n_lat, n_ctx,
                                                 s_, w_out_l, gvecs1, lng1, lnb1, scvecs2, shvecs2, wr, br, alpha)

        block_e, n_used, pos = _slots(route, tcnt[:, 0, :])
        n_slots = block_e.shape[0] * MOE_BLOCK
        xs = _dispatch_call(pos.reshape(n_tot // DISPATCH_TILE, 1, TOP_K * DISPATCH_TILE), h2p, n_slots)
        ys = _moe_call(block_e, n_used, xs, w1[l].astype(BF16), w3[l].astype(BF16), w2[l].astype(BF16))

        n_tiles = n_tot // COMBINE_TILE
        pos_t = pos.reshape(n_tiles, COMBINE_TILE, TOP_K).transpose(0, 2, 1).reshape(n_tiles, 1, TOP_K * COMBINE_TILE)
        gvecs = jnp.concatenate([g2, mc[0:1, 5]], axis=0)
        out = _combine_call(pos_t, x1, route, gvecs, lng2, lnb2, ys, n_tot, s_ // COMBINE_TILE, b_, alpha)
        x_arr, x_off = out, 0
        c_arr, c_off = out, n_lat

    return x_arr.reshape(b_, s_, d_)
```

```python
import functools

import jax
import jax.numpy as jnp
from jax import lax
from jax.experimental import pallas as pl
from jax.experimental.pallas import tpu as pltpu

F32 = jnp.float32
BF16 = jnp.bfloat16

D_MODEL = 1024
GRID_W = 64
HEAD_DIM = 64
N_HEADS = 8
N_KV_HEADS = 2
ATTN_WIDTH = N_HEADS * HEAD_DIM
KV_WIDTH = N_KV_HEADS * HEAD_DIM
WINDOW = 128
ATTN_SCALE = HEAD_DIM ** -0.5
ROPE_BASE = 10000.0
CONV_WIDTH = D_MODEL // 4
GM_WIDTH = D_MODEL // 4
GM_GROUPS = 4
GM_HEAD = GM_WIDTH // GM_GROUPS
CHUNK = 128
LOCAL_WIDTH = CONV_WIDTH + GM_WIDTH
IN_WIDTH = ATTN_WIDTH + 2 * KV_WIDTH + 3 * CONV_WIDTH + 2 * GM_WIDTH
QKV_WIDTH = ATTN_WIDTH + 2 * KV_WIDTH
N_GROUPS = 4
EXP_PER_GROUP = 8
N_EXPERTS = N_GROUPS * EXP_PER_GROUP
TOP_K = 2
D_EXPERT = D_MODEL // 2
MOE_BLOCK = 256
LN_EPS = 1e-6
NEG_INF = -1e30

LANES = 128
SUBLANES = 8
VMEM_LIMIT_BYTES = 48 * 1024 * 1024

ROUTER_COLS = LANES
ROUTE_EXPERT = 0
ROUTE_GATE = TOP_K
ROUTE_RANK = 2 * TOP_K
ROPE_HALF_PAIR = HEAD_DIM // 4


def _bdot(a, b):
    return jnp.dot(a, b, preferred_element_type=F32)


def _split_bf16(a):
    hi = a.astype(BF16)
    lo = (a - hi.astype(F32)).astype(BF16)
    return hi, lo


def _layer_norm(r):
    mu = jnp.mean(r, axis=-1, keepdims=True)
    d = r - mu
    var = jnp.mean(d * d, axis=-1, keepdims=True)
    return d * lax.rsqrt(var + LN_EPS)


def _params(*sem):
    return pltpu.CompilerParams(dimension_semantics=sem, vmem_limit_bytes=VMEM_LIMIT_BYTES)


ADA_ROWS = 16
ADA_TILE = 1536


def _ada_kernel(c_ref, w_ref, b_ref, o_ref):
    act = jax.nn.silu(c_ref[...])
    ah, al = _split_bf16(act)
    wh, wl = _split_bf16(w_ref[...])
    o_ref[...] = _bdot(ah, wh) + _bdot(ah, wl) + _bdot(al, wh) + b_ref[...]


def _ada_call(cin, w_ada, b_ada):
    depth, d, n = w_ada.shape
    return pl.pallas_call(
        _ada_kernel,
        grid=(depth, n // ADA_TILE),
        in_specs=[
            pl.BlockSpec((ADA_ROWS, d), lambda l, j: (0, 0)),
            pl.BlockSpec((None, d, ADA_TILE), lambda l, j: (l, 0, j)),
            pl.BlockSpec((None, 1, ADA_TILE), lambda l, j: (l, 0, j)),
        ],
        out_specs=pl.BlockSpec((None, ADA_ROWS, ADA_TILE), lambda l, j: (l, 0, j)),
        out_shape=jax.ShapeDtypeStruct((depth, ADA_ROWS, n), F32),
        compiler_params=_params("parallel", "parallel"),
        name="ada_modulation",
    )(cin, w_ada, b_ada.reshape(depth, 1, n))


def _inproj_kernel(x_ref, xp_ref, xn_ref, sc_ref, sh_ref, wf_ref, cos_ref, sa_ref, sb_ref,
                   cw_ref, gw_ref, gb_ref,
                   q_ref, k_ref, ksw_ref, v_ref, vsw_ref, y_ref, w_ref, *, ts):
    t = pl.program_id(1)
    nt = pl.num_programs(1)

    @pl.when((pl.program_id(0) == 0) & (t == 0))
    def _():
        w_ref[...] = wf_ref[...].astype(BF16)

    sc = 1.0 + sc_ref[...]
    sh = sh_ref[...]
    hx = (x_ref[...] * sc + sh).astype(BF16)

    pq = _bdot(hx, w_ref[:, 0:QKV_WIDTH])
    cos = cos_ref[...]
    sa = sa_ref[...]
    sb = sb_ref[...]

    def rope(z):
        return (z * cos + pltpu.roll(z, ROPE_HALF_PAIR, 1) * sa
                + pltpu.roll(z, LANES - ROPE_HALF_PAIR, 1) * sb)

    for j in range(ATTN_WIDTH // LANES):
        sl = slice(j * LANES, (j + 1) * LANES)
        q_ref[:, sl] = (rope(pq[:, sl]) * ATTN_SCALE).astype(BF16)
    kr = rope(pq[:, ATTN_WIDTH:ATTN_WIDTH + KV_WIDTH])
    k_ref[...] = kr.astype(BF16)
    ksw_ref[...] = pltpu.roll(kr, HEAD_DIM, 1).astype(BF16)
    vv = pq[:, ATTN_WIDTH + KV_WIDTH:QKV_WIDTH]
    v_ref[...] = vv.astype(BF16)
    vsw_ref[...] = pltpu.roll(vv, HEAD_DIM, 1).astype(BF16)

    pm = _bdot(hx, w_ref[:, QKV_WIDTH:IN_WIDTH])
    cb = pm[:, 0:CONV_WIDTH]
    u = pm[:, CONV_WIDTH:2 * CONV_WIDTH] * pm[:, 2 * CONV_WIDTH:3 * CONV_WIDTH]

    halo = jnp.concatenate([xp_ref[...], xn_ref[...]], axis=0)
    hh = (halo * sc + sh).astype(BF16)
    ph = _bdot(hh, w_ref[:, QKV_WIDTH + CONV_WIDTH:QKV_WIDTH + 3 * CONV_WIDTH])
    uh = ph[:, 0:CONV_WIDTH] * ph[:, CONV_WIDTH:2 * CONV_WIDTH]
    up_row = jnp.where(t > 0, uh[SUBLANES - 1:SUBLANES, :], 0.0)
    dn_row = jnp.where(t < nt - 1, uh[SUBLANES:SUBLANES + 1, :], 0.0)
    ridx = lax.broadcasted_iota(jnp.int32, (ts, CONV_WIDTH), 0)
    u_up = jnp.where(ridx == 0, up_row, pltpu.roll(u, 1, 0))
    u_dn = jnp.where(ridx == ts - 1, dn_row, pltpu.roll(u, ts - 1, 0))
    cw = cw_ref[...]
    y_conv = cb * (u_up * cw[0:1, :] + u * cw[1:2, :] + u_dn * cw[2:3, :])
    y_ref[:, 0:CONV_WIDTH] = y_conv.astype(BF16)

    gu = jax.nn.gelu(pm[:, 3 * CONV_WIDTH:3 * CONV_WIDTH + GM_WIDTH])
    gv = _layer_norm(jax.nn.gelu(pm[:, 3 * CONV_WIDTH + GM_WIDTH:3 * CONV_WIDTH + 2 * GM_WIDTH])).astype(BF16)
    lane = lax.broadcasted_iota(jnp.int32, (CHUNK, LANES), 1)
    zero = jnp.zeros((CHUNK, LANES), BF16)
    for c in range(ts // CHUNK):
        rows = slice(c * CHUNK, (c + 1) * CHUNK)
        for j in range(GM_WIDTH // LANES):
            cols = slice(j * LANES, (j + 1) * LANES)
            vp = gv[rows, cols]
            s = (_bdot(gw_ref[2 * j], jnp.where(lane < GM_HEAD, vp, zero))
                 + _bdot(gw_ref[2 * j + 1], jnp.where(lane >= GM_HEAD, vp, zero))
                 + gb_ref[:, cols])
            y_ref[rows, CONV_WIDTH + j * LANES:CONV_WIDTH + (j + 1) * LANES] = (gu[rows, cols] * s).astype(BF16)


def _inproj_call(x2d, row_off, n_seq, seq_len, ts, sc, sh, w_in, layer, cos, sa, sb, conv_w, gm_w, gm_b):
    nt = seq_len // ts
    off_t = row_off // ts
    off_8 = row_off // SUBLANES
    last_8 = x2d.shape[0] // SUBLANES - 1
    per_tile_8 = ts // SUBLANES
    n_out = n_seq * seq_len
    d = D_MODEL

    def tile_idx(b, t):
        return (off_t + b * nt + t, 0)

    def prev_idx(b, t):
        return (jnp.maximum(off_8 + (b * nt + t) * per_tile_8 - 1, 0), 0)

    def next_idx(b, t):
        return (jnp.minimum(off_8 + (b * nt + t + 1) * per_tile_8, last_8), 0)

    def out_idx(b, t):
        return (b * nt + t, 0)

    vec = pl.BlockSpec((None, 1, d), lambda b, t: (b, 0, 0))
    rope_spec = pl.BlockSpec((ts, LANES), lambda b, t: (t, 0))
    kv_spec = pl.BlockSpec((ts, KV_WIDTH), out_idx)
    kv_shape = jax.ShapeDtypeStruct((n_out, KV_WIDTH), BF16)
    return pl.pallas_call(
        functools.partial(_inproj_kernel, ts=ts),
        grid=(n_seq, nt),
        in_specs=[
            pl.BlockSpec((ts, d), tile_idx),
            pl.BlockSpec((SUBLANES, d), prev_idx),
            pl.BlockSpec((SUBLANES, d), next_idx),
            vec, vec,
            pl.BlockSpec((None, d, IN_WIDTH), lambda b, t: (layer, 0, 0)),
            rope_spec, rope_spec, rope_spec,
            pl.BlockSpec((3, CONV_WIDTH), lambda b, t: (0, 0)),
            pl.BlockSpec((GM_GROUPS, CHUNK, CHUNK), lambda b, t: (0, 0, 0)),
            pl.BlockSpec((CHUNK, GM_WIDTH), lambda b, t: (0, 0)),
        ],
        out_specs=[
            pl.BlockSpec((ts, ATTN_WIDTH), out_idx),
            kv_spec, kv_spec, kv_spec, kv_spec,
            pl.BlockSpec((ts, LOCAL_WIDTH), out_idx),
        ],
        out_shape=[
            jax.ShapeDtypeStruct((n_out, ATTN_WIDTH), BF16),
            kv_shape, kv_shape, kv_shape, kv_shape,
            jax.ShapeDtypeStruct((n_out, LOCAL_WIDTH), BF16),
        ],
        scratch_shapes=[pltpu.VMEM((d, IN_WIDTH), BF16)],
        compiler_params=_params("arbitrary", "arbitrary"),
        name="in_projection",
    )(x2d, x2d, x2d, sc, sh, w_in, cos, sa, sb, conv_w, gm_w, gm_b)


def _attn_body(q, keys, keys_sw, vals, vals_sw, bias, sink_ref, o_ref, tq):
    nk = keys.shape[0]
    lane = lax.broadcasted_iota(jnp.int32, (nk, LANES), 1)
    lo = lane < HEAD_DIM
    zero = jnp.zeros((nk, LANES), BF16)
    rid = lax.broadcasted_iota(jnp.int32, (2 * tq, 1), 0)
    nt_dims = (((1,), (1,)), ((), ()))
    gqa = N_HEADS // N_KV_HEADS
    for h in range(N_KV_HEADS):
        k_own, k_oth = (keys, keys_sw) if h == 0 else (keys_sw, keys)
        v_own, v_oth = (vals, vals_sw) if h == 0 else (vals_sw, vals)
        kz = (jnp.where(lo, k_own, zero), jnp.where(lo, zero, k_oth))
        vz = (jnp.where(lo, v_own, zero), jnp.where(lo, zero, v_oth))
        c0 = h * gqa * HEAD_DIM
        qs = jnp.concatenate([q[:, c0:c0 + LANES], q[:, c0 + LANES:c0 + 2 * LANES]], axis=0)
        acc = None
        for par in range(2):
            s = lax.dot_general(qs, kz[par], nt_dims, preferred_element_type=F32)
            if bias is not None:
                s = s + bias
            sink = jnp.where(rid < tq, sink_ref[h * gqa + par], sink_ref[h * gqa + 2 + par])
            m = jnp.maximum(jnp.max(s, axis=-1, keepdims=True), sink)
            p = jnp.exp(s - m)
            den = jnp.sum(p, axis=-1, keepdims=True) + jnp.exp(sink - m)
            o = _bdot(p.astype(BF16), vz[par]) / den
            acc = o if acc is None else acc + o
        o_ref[:, c0:c0 + LANES] = acc[0:tq].astype(BF16)
        o_ref[:, c0 + LANES:c0 + 2 * LANES] = acc[tq:2 * tq].astype(BF16)


def _win_attn_kernel(sink_ref, q_ref, kp_ref, kc_ref, kn_ref, kx_ref,
                     ksp_ref, ksc_ref, ksn_ref, ksx_ref,
                     vp_ref, vc_ref, vn_ref, vx_ref,
                     vsp_ref, vsc_ref, vsn_ref, vsx_ref, o_ref, *, tq, ctx_len):
    n = pl.program_id(1)
    nb = pl.num_programs(1)
    row = lax.broadcasted_iota(jnp.int32, (tq, tq), 0)
    col = lax.broadcasted_iota(jnp.int32, (tq, tq), 1)
    b_prev = jnp.where((col >= row) & (n > 0), 0.0, NEG_INF)
    b_next = jnp.where((col <= row) & (n < nb - 1), 0.0, NEG_INF)
    b_half = jnp.concatenate([b_prev, jnp.zeros((tq, tq), F32), b_next, jnp.zeros((tq, ctx_len), F32)], axis=1)
    bias = jnp.concatenate([b_half, b_half], axis=0)

    def cat(a, b, c, d):
        return jnp.concatenate([a[...], b[...], c[...], d[...]], axis=0)

    _attn_body(q_ref[...], cat(kp_ref, kc_ref, kn_ref, kx_ref), cat(ksp_ref, ksc_ref, ksn_ref, ksx_ref),
               cat(vp_ref, vc_ref, vn_ref, vx_ref), cat(vsp_ref, vsc_ref, vsn_ref, vsx_ref),
               bias, sink_ref, o_ref, tq)


def _win_attn_call(sink, q, k, ksw, v, vsw, kx, kxsw, vx, vxsw, n_seq, seq_len, ctx_len):
    tq = WINDOW
    nb = seq_len // tq

    def cur(b, n):
        return (b * nb + n, 0)

    def prev(b, n):
        return (b * nb + jnp.maximum(n - 1, 0), 0)

    def nxt(b, n):
        return (b * nb + jnp.minimum(n + 1, nb - 1), 0)

    def ctx(b, n):
        return (b, 0)

    def kv_specs():
        return [pl.BlockSpec((tq, KV_WIDTH), prev), pl.BlockSpec((tq, KV_WIDTH), cur),
                pl.BlockSpec((tq, KV_WIDTH), nxt), pl.BlockSpec((ctx_len, KV_WIDTH), ctx)]

    return pl.pallas_call(
        functools.partial(_win_attn_kernel, tq=tq, ctx_len=ctx_len),
        grid=(n_seq, nb),
        in_specs=[pl.BlockSpec(memory_space=pltpu.SMEM), pl.BlockSpec((tq, ATTN_WIDTH), cur)]
        + kv_specs() + kv_specs() + kv_specs() + kv_specs(),
        out_specs=pl.BlockSpec((tq, ATTN_WIDTH), cur),
        out_shape=jax.ShapeDtypeStruct((n_seq * seq_len, ATTN_WIDTH), BF16),
        compiler_params=_params("parallel", "parallel"),
        name="window_attention",
    )(sink, q, k, k, k, kx, ksw, ksw, ksw, kxsw, v, v, v, vx, vsw, vsw, vsw, vxsw)


def _ctx_attn_kernel(sink_ref, q_ref, k_ref, ks_ref, v_ref, vs_ref, o_ref, *, tq):
    _attn_body(q_ref[...], k_ref[...], ks_ref[...], v_ref[...], vs_ref[...], None, sink_ref, o_ref, tq)


def _ctx_attn_call(sink, q, k, ksw, v, vsw, n_seq, ctx_len):
    def blk(w):
        return pl.BlockSpec((ctx_len, w), lambda b: (b, 0))

    return pl.pallas_call(
        functools.partial(_ctx_attn_kernel, tq=ctx_len),
        grid=(n_seq,),
        in_specs=[pl.BlockSpec(memory_space=pltpu.SMEM), blk(ATTN_WIDTH),
                  blk(KV_WIDTH), blk(KV_WIDTH), blk(KV_WIDTH), blk(KV_WIDTH)],
        out_specs=blk(ATTN_WIDTH),
        out_shape=jax.ShapeDtypeStruct((n_seq * ctx_len, ATTN_WIDTH), BF16),
        compiler_params=_params("parallel"),
        name="context_attention",
    )(sink, q, k, ksw, v, vsw)


OUT_TILE = 256


def _outproj_kernel(attx_ref, attc_ref, yx_ref, yc_ref, xx_ref, xc_ref, wof_ref, g_ref, lng_ref, lnb_ref,
                    sc_ref, sh_ref, wr_ref, br_ref, x1_ref, h2p_ref, route_ref, cnt_ref, wo_ref,
                    *, alpha, n_lat_tiles):
    @pl.when(pl.program_id(0) == 0)
    def _():
        wo_ref[...] = wof_ref[...].astype(BF16)

    is_lat = pl.program_id(0) < n_lat_tiles
    att = jnp.where(is_lat, attx_ref[...], attc_ref[...])
    yloc = jnp.where(is_lat, yx_ref[...], yc_ref[...])
    xin = jnp.where(is_lat, xx_ref[...], xc_ref[...])
    o = _bdot(att, wo_ref[0:ATTN_WIDTH, :]) + _bdot(yloc, wo_ref[ATTN_WIDTH:, :])
    x1 = _layer_norm(alpha * xin + g_ref[...] * o) * lng_ref[...] + lnb_ref[...]
    x1_ref[...] = x1
    h2 = x1 * (1.0 + sc_ref[...]) + sh_ref[...]
    half = D_MODEL // 2
    hi_bits = lax.bitcast_convert_type(h2[:, 0:half].astype(BF16).astype(F32), jnp.uint32)
    lo_bits = lax.bitcast_convert_type(h2[:, half:].astype(BF16).astype(F32), jnp.uint32)
    h2p_ref[...] = hi_bits | (lo_bits >> 16)

    hh, hl = _split_bf16(h2)
    t1 = _bdot(hh, wr_ref[...])
    lg = t1[:, 0:ROUTER_COLS] + t1[:, ROUTER_COLS:] + _bdot(hl, wr_ref[:, 0:ROUTER_COLS]) + br_ref[...]

    ts = lg.shape[0]
    lane = lax.broadcasted_iota(jnp.int32, (ts, ROUTER_COLS), 1).astype(F32)
    big = jnp.float32(ROUTER_COLS)

    def top1(v):
        m = jnp.max(v, axis=-1, keepdims=True)
        return m, jnp.min(jnp.where(v == m, lane, big), axis=-1, keepdims=True)

    gl = jnp.where(lane < N_GROUPS, lg, NEG_INF)
    g_val, g_idx = top1(gl)
    lse = g_val + jnp.log(jnp.sum(jnp.exp(gl - g_val), axis=-1, keepdims=True))
    p_group = jnp.exp(g_val - lse)
    e_lo = N_GROUPS + EXP_PER_GROUP * g_idx
    el = jnp.where((lane >= e_lo) & (lane < e_lo + EXP_PER_GROUP), lg, NEG_INF)
    e1, l1 = top1(el)
    e2, l2 = top1(jnp.where(lane == l1, NEG_INF, el))
    z = jnp.exp(e2 - e1)
    gate1 = p_group / (1.0 + z)
    gate2 = p_group * z / (1.0 + z)
    x1id = l1 - N_GROUPS
    x2id = l2 - N_GROUPS

    sel1 = lane == x1id
    sel2 = lane == x2id
    onehot = jnp.where(sel1 | sel2, 1.0, 0.0)
    r_i = lax.broadcasted_iota(jnp.int32, (ts, ts), 0)
    c_i = lax.broadcasted_iota(jnp.int32, (ts, ts), 1)
    before = jnp.where(c_i < r_i, 1.0, 0.0).astype(BF16)
    prefix = _bdot(before, onehot.astype(BF16))
    rank1 = jnp.sum(jnp.where(sel1, prefix, 0.0), axis=-1, keepdims=True)
    rank2 = jnp.sum(jnp.where(sel2, prefix, 0.0), axis=-1, keepdims=True)
    route = jnp.zeros((ts, ROUTER_COLS), F32)
    for col, val in enumerate((x1id, x2id, gate1, gate2, rank1, rank2)):
        route = jnp.where(lane == col, val, route)
    route_ref[...] = route
    cnt_ref[...] = jnp.broadcast_to(jnp.sum(onehot, axis=0, keepdims=True), (SUBLANES, ROUTER_COLS))


def _outproj_call(att_x, att_c, y_x, y_c, x_arr, x_off, c_arr, c_off, n_lat, n_ctx, seq_len,
                  w_out, layer, gvecs, lng, lnb, scvecs, shvecs, wr, br, alpha):
    ts = OUT_TILE
    d = D_MODEL
    n_lat_tiles = n_lat // ts
    n_tiles = (n_lat + n_ctx) // ts
    tiles_per_seq = seq_len // ts
    n_seq = n_lat // seq_len
    xo, co = x_off // ts, c_off // ts

    def lat_loc(i):
        return (jnp.minimum(i, n_lat_tiles - 1), 0)

    def ctx_loc(i):
        return (jnp.maximum(i - n_lat_tiles, 0), 0)

    def lat_in(i):
        return (xo + jnp.minimum(i, n_lat_tiles - 1), 0)

    def ctx_in(i):
        return (co + jnp.maximum(i - n_lat_tiles, 0), 0)

    def vec_idx(i):
        return (jnp.where(i < n_lat_tiles, i // tiles_per_seq, n_seq), 0, 0)

    vecb = pl.BlockSpec((None, 1, d), vec_idx)
    vec0 = pl.BlockSpec((1, d), lambda i: (0, 0))
    return pl.pallas_call(
        functools.partial(_outproj_kernel, alpha=alpha, n_lat_tiles=n_lat_tiles),
        grid=(n_tiles,),
        in_specs=[
            pl.BlockSpec((ts, ATTN_WIDTH), lat_loc), pl.BlockSpec((ts, ATTN_WIDTH), ctx_loc),
            pl.BlockSpec((ts, LOCAL_WIDTH), lat_loc), pl.BlockSpec((ts, LOCAL_WIDTH), ctx_loc),
            pl.BlockSpec((ts, d), lat_in), pl.BlockSpec((ts, d), ctx_in),
            pl.BlockSpec((None, d, d), lambda i: (layer, 0, 0)),
            vecb, vec0, vec0, vecb, vecb,
            pl.BlockSpec((d, 2 * ROUTER_COLS), lambda i: (0, 0)),
            pl.BlockSpec((1, ROUTER_COLS), lambda i: (0, 0)),
        ],
        out_specs=[pl.BlockSpec((ts, d), lambda i: (i, 0)), pl.BlockSpec((ts, d // 2), lambda i: (i, 0)),
                   pl.BlockSpec((ts, ROUTER_COLS), lambda i: (i, 0)),
                   pl.BlockSpec((None, SUBLANES, ROUTER_COLS), lambda i: (i, 0, 0))],
        out_shape=[jax.ShapeDtypeStruct((n_lat + n_ctx, d), F32),
                   jax.ShapeDtypeStruct((n_lat + n_ctx, d // 2), jnp.uint32),
                   jax.ShapeDtypeStruct((n_lat + n_ctx, ROUTER_COLS), F32),
                   jax.ShapeDtypeStruct((n_tiles, SUBLANES, ROUTER_COLS), F32)],
        scratch_shapes=[pltpu.VMEM((d, d), BF16)],
        compiler_params=_params("arbitrary"),
        name="out_projection",
    )(att_x, att_c, y_x, y_c, x_arr, c_arr, w_out, gvecs, lng, lnb, scvecs, shvecs, wr, br)


DISPATCH_TILE = 512


def _dispatch_kernel(pos_ref, h_ref, xs_in, xs_out, sem):
    del xs_in

    def body(t, carry):
        for k in range(TOP_K):
            pltpu.make_async_copy(h_ref.at[pl.ds(t, 1)], xs_out.at[pl.ds(pos_ref[0, 0, TOP_K * t + k], 1)],
                                  sem).start()
        return carry

    lax.fori_loop(0, DISPATCH_TILE, body, 0, unroll=4)
    for k in range(TOP_K):
        pltpu.make_async_copy(h_ref, h_ref, sem).wait()


def _dispatch_call(pos, h2p, n_slots):
    n_tok, half = h2p.shape
    n_tiles = n_tok // DISPATCH_TILE
    xs0 = jnp.zeros((n_slots, half), jnp.uint32)
    return pl.pallas_call(
        _dispatch_kernel,
        grid=(n_tiles,),
        in_specs=[
            pl.BlockSpec((1, 1, TOP_K * DISPATCH_TILE), lambda i: (i, 0, 0), memory_space=pltpu.SMEM),
            pl.BlockSpec((DISPATCH_TILE, half), lambda i: (i, 0)),
            pl.BlockSpec(memory_space=pl.ANY),
        ],
        out_specs=pl.BlockSpec(memory_space=pl.ANY),
        out_shape=jax.ShapeDtypeStruct((n_slots, half), jnp.uint32),
        scratch_shapes=[pltpu.SemaphoreType.DMA(())],
        input_output_aliases={2: 0},
        compiler_params=_params("arbitrary"),
        name="moe_dispatch",
    )(pos, h2p, xs0)


def _moe_kernel(be_ref, nu_ref, xs_ref, w1_ref, w3_ref, w2_ref, o_ref, w1b, w3b, w2b):
    i = pl.program_id(0)
    half = D_MODEL // 2

    @pl.when((i == 0) | (be_ref[i] != be_ref[jnp.maximum(i - 1, 0)]))
    def _():
        w1b[...] = w1_ref[...].astype(BF16)
        w3b[...] = w3_ref[...].astype(BF16)
        w2b[...] = w2_ref[...].astype(BF16)

    @pl.when(i < nu_ref[0])
    def _():
        p = xs_ref[...]
        xa = lax.bitcast_convert_type(p & jnp.uint32(0xFFFF0000), F32).astype(BF16)
        xb = lax.bitcast_convert_type(p << 16, F32).astype(BF16)
        h1 = _bdot(xa, w1b[0:half, :]) + _bdot(xb, w1b[half:, :])
        h3 = _bdot(xa, w3b[0:half, :]) + _bdot(xb, w3b[half:, :])
        o_ref[...] = _bdot((jax.nn.silu(h1) * h3).astype(BF16), w2b[...])

    @pl.when(i >= nu_ref[0])
    def _():
        o_ref[...] = jnp.zeros_like(o_ref)


def _moe_call(block_e, n_used, xs, w1, w3, w2, layer):
    n_blocks = block_e.shape[0]
    d = D_MODEL
    grid_spec = pltpu.PrefetchScalarGridSpec(
        num_scalar_prefetch=2,
        grid=(n_blocks,),
        in_specs=[
            pl.BlockSpec((MOE_BLOCK, d // 2), lambda i, be, nu: (i, 0)),
            pl.BlockSpec((None, None, d, D_EXPERT), lambda i, be, nu: (layer, be[i], 0, 0)),
            pl.BlockSpec((None, None, d, D_EXPERT), lambda i, be, nu: (layer, be[i], 0, 0)),
            pl.BlockSpec((None, None, D_EXPERT, d), lambda i, be, nu: (layer, be[i], 0, 0)),
        ],
        out_specs=pl.BlockSpec((MOE_BLOCK, d), lambda i, be, nu: (i, 0)),
        scratch_shapes=[pltpu.VMEM((d, D_EXPERT), BF16), pltpu.VMEM((d, D_EXPERT), BF16),
                        pltpu.VMEM((D_EXPERT, d), BF16)],
    )
    return pl.pallas_call(
        _moe_kernel,
        grid_spec=grid_spec,
        out_shape=jax.ShapeDtypeStruct((n_blocks * MOE_BLOCK, d), F32),
        compiler_params=_params("arbitrary"),
        name="moe_experts",
    )(block_e, n_used, xs, w1, w3, w2)


COMBINE_TILE = 256


def _combine_kernel(pos0_ref, posn_ref, x1_ref, route_ref, g_ref, lng_ref, lnb_ref, ys_hbm, o_ref, ybuf, sem,
                    *, alpha):
    i = pl.program_id(0)
    nb = pl.num_programs(0)

    def row_copy(p, slot, r):
        return pltpu.make_async_copy(ys_hbm.at[pl.ds(p, 1)], ybuf.at[slot, pl.ds(r, 1)], sem.at[slot])

    def issue(pos_ref, slot):
        def body(r, carry):
            row_copy(pos_ref[0, 0, r], slot, r).start()
            return carry
        lax.fori_loop(0, TOP_K * COMBINE_TILE, body, 0, unroll=8)

    @pl.when(i == 0)
    def _():
        issue(pos0_ref, 0)

    @pl.when(i + 1 < nb)
    def _():
        issue(posn_ref, (i + 1) % 2)

    slot = i % 2
    pltpu.make_async_copy(ybuf.at[slot], ybuf.at[slot], sem.at[slot]).wait()
    route = route_ref[...]
    y = (ybuf[slot, 0:COMBINE_TILE, :] * route[:, ROUTE_GATE:ROUTE_GATE + 1]
         + ybuf[slot, COMBINE_TILE:, :] * route[:, ROUTE_GATE + 1:ROUTE_GATE + 2])
    o_ref[...] = _layer_norm(alpha * x1_ref[...] + g_ref[...] * y) * lng_ref[...] + lnb_ref[...]


def _combine_call(pos, x1, route, gvecs, lng, lnb, ys, n_tok, tiles_per_seq, n_seq, alpha):
    d = D_MODEL
    n_tiles = n_tok // COMBINE_TILE
    n_lat_tiles = tiles_per_seq * n_seq

    def g_idx(i):
        return (jnp.where(i < n_lat_tiles, i // tiles_per_seq, n_seq), 0, 0)

    smem_blk = (1, 1, TOP_K * COMBINE_TILE)
    return pl.pallas_call(
        functools.partial(_combine_kernel, alpha=alpha),
        grid=(n_tiles,),
        in_specs=[
            pl.BlockSpec(smem_blk, lambda i: (0, 0, 0), memory_space=pltpu.SMEM),
            pl.BlockSpec(smem_blk, lambda i: (jnp.minimum(i + 1, n_tiles - 1), 0, 0), memory_space=pltpu.SMEM),
            pl.BlockSpec((COMBINE_TILE, d), lambda i: (i, 0)),
            pl.BlockSpec((COMBINE_TILE, ROUTER_COLS), lambda i: (i, 0)),
            pl.BlockSpec((None, 1, d), g_idx),
            pl.BlockSpec((1, d), lambda i: (0, 0)),
            pl.BlockSpec((1, d), lambda i: (0, 0)),
            pl.BlockSpec(memory_space=pl.ANY),
        ],
        out_specs=pl.BlockSpec((COMBINE_TILE, d), lambda i: (i, 0)),
        out_shape=jax.ShapeDtypeStruct((n_tok, d), F32),
        scratch_shapes=[pltpu.VMEM((2, TOP_K * COMBINE_TILE, d), F32), pltpu.SemaphoreType.DMA((2,))],
        compiler_params=_params("arbitrary"),
        name="moe_combine",
    )(pos, pos, x1, route, gvecs, lng, lnb, ys)


def _slots(route, tile_counts):
    n = route.shape[0]
    n_tiles = tile_counts.shape[0]
    tc = tile_counts.astype(jnp.int32)
    counts = jnp.sum(tc, axis=0)
    padded = (counts + MOE_BLOCK - 1) // MOE_BLOCK * MOE_BLOCK
    pends = jnp.cumsum(padded)
    base = (pends - padded)[None, :] + jnp.cumsum(tc, axis=0) - tc
    n_blocks = -(-n * TOP_K // MOE_BLOCK) + N_EXPERTS
    block_start = jnp.arange(n_blocks, dtype=jnp.int32) * MOE_BLOCK
    block_e = jnp.minimum(jnp.sum(pends[None, 0:N_EXPERTS] <= block_start[:, None], axis=1),
                          N_EXPERTS - 1).astype(jnp.int32)
    n_used = (pends[N_EXPERTS - 1] // MOE_BLOCK).astype(jnp.int32).reshape(1)
    r3 = route.reshape(n_tiles, n // n_tiles, ROUTER_COLS)
    lane = lax.broadcasted_iota(jnp.int32, r3.shape, 2)
    pos = []
    for k in range(TOP_K):
        e = r3[:, :, ROUTE_EXPERT + k].astype(jnp.int32)
        b = jnp.sum(jnp.where(lane == e[:, :, None], base[:, None, :], 0), axis=-1)
        pos.append((b + r3[:, :, ROUTE_RANK + k].astype(jnp.int32)).reshape(n))
    return block_e, n_used, jnp.stack(pos, axis=-1)


def _rope_tables(seq_len):
    m = HEAD_DIM // 4
    freqs = ROPE_BASE ** (-jnp.arange(m, dtype=F32) / m)
    t = jnp.arange(seq_len)
    row = (t // GRID_W).astype(F32)[:, None] * freqs[None, :]
    col = (t % GRID_W).astype(F32)[:, None] * freqs[None, :]
    cos = jnp.concatenate([jnp.cos(row), jnp.cos(row), jnp.cos(col), jnp.cos(col)], axis=-1)
    zero = jnp.zeros_like(row)
    sin_a = jnp.concatenate([zero, jnp.sin(row), zero, jnp.sin(col)], axis=-1)
    sin_b = jnp.concatenate([-jnp.sin(row), zero, -jnp.sin(col), zero], axis=-1)
    rep = LANES // HEAD_DIM
    return jnp.tile(cos, (1, rep)), jnp.tile(sin_a, (1, rep)), jnp.tile(sin_b, (1, rep))


def kernel(x, c, ctx, c_ctx, w_ada, b_ada, w_in, conv_w, attn_sink, gm_ws, gm_bs, w_out, ln1_g, ln1_b,
           w_rg, b_rg, w_re, b_re, w1, w3, w2, ln2_g, ln2_b):
    b_, s_, d_ = x.shape
    c_len = ctx.shape[1]
    depth = w_ada.shape[0]
    alpha = (2 * depth) ** 0.25
    n_lat = b_ * s_
    n_ctx = b_ * c_len
    ts = 512

    cin = jnp.zeros((ADA_ROWS, d_), F32).at[0:b_].set(c).at[b_].set(c_ctx)
    mod = _ada_call(cin, w_ada, b_ada)

    cos, sin_a, sin_b = _rope_tables(s_)
    ones_c = jnp.ones((c_len, LANES), F32)
    zeros_c = jnp.zeros((c_len, LANES), F32)

    x_flat = x.reshape(n_lat, d_)
    c_flat = ctx.reshape(n_ctx, d_)
    x_off, c_off = 0, 0
    x_arr, c_arr = x_flat, c_flat

    for l in range(depth):
        last = l == depth - 1
        mx = mod[l, 0:b_].reshape(b_, 6, 1, d_)
        sh1, sc1, g1, sh2, sc2, g2 = (mx[:, i] for i in range(6))
        mc = jnp.broadcast_to(mod[l, b_].reshape(1, 6, 1, d_), (b_, 6, 1, d_))
        gm_w = gm_ws[l].astype(BF16)
        gm_b = jnp.repeat(gm_bs[l].T, GM_HEAD, axis=1)
        sink = attn_sink[l]

        qx, kx, kxs, vx, vxs, yx = _inproj_call(x_arr, x_off, b_, s_, ts, sc1, sh1, w_in, l, cos, sin_a, sin_b,
                                                conv_w[l], gm_w, gm_b)
        qc, kc, kcs, vc, vcs, yc = _inproj_call(c_arr, c_off, b_, c_len, c_len, mc[:, 1], mc[:, 0], w_in, l,
                                                ones_c, zeros_c, zeros_c, conv_w[l], gm_w, gm_b)
        att_x = _win_attn_call(sink, qx, kx, kxs, vx, vxs, kc, kcs, vc, vcs, b_, s_, c_len)

        w_r = jnp.zeros((d_, ROUTER_COLS), F32).at[:, 0:N_GROUPS].set(w_rg[l]) \
            .at[:, N_GROUPS:N_GROUPS + N_EXPERTS].set(w_re[l])
        w_r_hi = w_r.astype(BF16)
        w_r_lo = (w_r - w_r_hi.astype(F32)).astype(BF16)
        wr = jnp.concatenate([w_r_hi, w_r_lo], axis=1)
        br = jnp.zeros((1, ROUTER_COLS), F32).at[0, 0:N_GROUPS].set(b_rg[l]) \
            .at[0, N_GROUPS:N_GROUPS + N_EXPERTS].set(b_re[l])
        lng1, lnb1 = ln1_g[l].reshape(1, d_), ln1_b[l].reshape(1, d_)
        lng2, lnb2 = ln2_g[l].reshape(1, d_), ln2_b[l].reshape(1, d_)

        n_tot = n_lat if last else n_lat + n_ctx
        gvecs1 = jnp.concatenate([g1, mc[0:1, 2]], axis=0)
        scvecs2 = jnp.concatenate([sc2, mc[0:1, 4]], axis=0)
        shvecs2 = jnp.concatenate([sh2, mc[0:1, 3]], axis=0)
        if last:
            x1, h2p, route, tcnt = _outproj_call(att_x, att_x, yx, yx, x_arr, x_off, x_arr, x_off, n_lat, 0, s_,
                                                 w_out, l, gvecs1, lng1, lnb1, scvecs2, shvecs2, wr, br, alpha)
        else:
            att_c = _ctx_attn_call(sink, qc, kc, kcs, vc, vcs, b_, c_len)
            x1, h2p, route, tcnt = _outproj_call(att_x, att_c, yx, yc, x_arr, x_off, c_arr, c_off, n_lat, n_ctx,
                                                 s_, w_out, l, gvecs1, lng1, lnb1, scvecs2, shvecs2, wr, br, alpha)

        block_e, n_used, pos = _slots(route, tcnt[:, 0, :])
        n_slots = block_e.shape[0] * MOE_BLOCK
        xs = _dispatch_call(pos.reshape(n_tot // DISPATCH_TILE, 1, TOP_K * DISPATCH_TILE), h2p, n_slots)
        ys = _moe_call(block_e, n_used, xs, w1, w3, w2, l)

        n_tiles = n_tot // COMBINE_TILE
        pos_t = pos.reshape(n_tiles, COMBINE_TILE, TOP_K).transpose(0, 2, 1).reshape(n_tiles, 1, TOP_K * COMBINE_TILE)
        gvecs = jnp.concatenate([g2, mc[0:1, 5]], axis=0)
        out = _combine_call(pos_t, x1, route, gvecs, lng2, lnb2, ys, n_tot, s_ // COMBINE_TILE, b_, alpha)
        x_arr, x_off = out, 0
        c_arr, c_off = out, n_lat

    return x_arr.reshape(b_, s_, d_)
```

```python
import functools

import jax
import jax.numpy as jnp
from jax import lax
from jax.experimental import pallas as pl
from jax.experimental.pallas import tpu as pltpu

F32 = jnp.float32
BF16 = jnp.bfloat16

D_MODEL = 1024
GRID_W = 64
HEAD_DIM = 64
N_HEADS = 8
N_KV_HEADS = 2
ATTN_WIDTH = N_HEADS * HEAD_DIM
KV_WIDTH = N_KV_HEADS * HEAD_DIM
WINDOW = 128
ATTN_SCALE = HEAD_DIM ** -0.5
ROPE_BASE = 10000.0
CONV_WIDTH = D_MODEL // 4
GM_WIDTH = D_MODEL // 4
GM_GROUPS = 4
GM_HEAD = GM_WIDTH // GM_GROUPS
CHUNK = 128
LOCAL_WIDTH = CONV_WIDTH + GM_WIDTH
IN_WIDTH = ATTN_WIDTH + 2 * KV_WIDTH + 3 * CONV_WIDTH + 2 * GM_WIDTH
QKV_WIDTH = ATTN_WIDTH + 2 * KV_WIDTH
N_GROUPS = 4
EXP_PER_GROUP = 8
N_EXPERTS = N_GROUPS * EXP_PER_GROUP
TOP_K = 2
D_EXPERT = D_MODEL // 2
MOE_BLOCK = 256
LN_EPS = 1e-6
NEG_INF = -1e30

LANES = 128
SUBLANES = 8
VMEM_LIMIT_BYTES = 48 * 1024 * 1024
N_DMA_PRIORITIES = 2

ROW_TILE_ROWS = D_MODEL // LANES
assert ROW_TILE_ROWS == SUBLANES

ROUTER_COLS = LANES
ROUTE_EXPERT = 0
ROUTE_GATE = TOP_K
ROUTE_RANK = 2 * TOP_K
ROPE_HALF_PAIR = HEAD_DIM // 4


def _bdot(a, b):
    return jnp.dot(a, b, preferred_element_type=F32)


def _split_bf16(a):
    hi = a.astype(BF16)
    lo = (a - hi.astype(F32)).astype(BF16)
    return hi, lo


def _layer_norm(r):
    mu = jnp.mean(r, axis=-1, keepdims=True)
    d = r - mu
    var = jnp.mean(d * d, axis=-1, keepdims=True)
    return d * lax.rsqrt(var + LN_EPS)


def _params(*sem):
    return pltpu.CompilerParams(dimension_semantics=sem, vmem_limit_bytes=VMEM_LIMIT_BYTES)


ADA_ROWS = 16
ADA_TILE = 1536


def _ada_kernel(c_ref, w_ref, b_ref, o_ref):
    act = jax.nn.silu(c_ref[...])
    ah, al = _split_bf16(act)
    wh, wl = _split_bf16(w_ref[...])
    o_ref[...] = _bdot(ah, wh) + _bdot(ah, wl) + _bdot(al, wh) + b_ref[...]


def _ada_call(cin, w_ada, b_ada):
    depth, d, n = w_ada.shape
    return pl.pallas_call(
        _ada_kernel,
        grid=(depth, n // ADA_TILE),
        in_specs=[
            pl.BlockSpec((ADA_ROWS, d), lambda l, j: (0, 0)),
            pl.BlockSpec((None, d, ADA_TILE), lambda l, j: (l, 0, j)),
            pl.BlockSpec((None, 1, ADA_TILE), lambda l, j: (l, 0, j)),
        ],
        out_specs=pl.BlockSpec((None, ADA_ROWS, ADA_TILE), lambda l, j: (l, 0, j)),
        out_shape=jax.ShapeDtypeStruct((depth, ADA_ROWS, n), F32),
        compiler_params=_params("parallel", "parallel"),
        name="ada_modulation",
    )(cin, w_ada, b_ada.reshape(depth, 1, n))


def _inproj_kernel(x_ref, xp_ref, xn_ref, sc_ref, sh_ref, wf_ref, cos_ref, sa_ref, sb_ref,
                   cw_ref, gw_ref, gb_ref,
                   q_ref, k_ref, ksw_ref, v_ref, vsw_ref, y_ref, w_ref, *, ts):
    t = pl.program_id(1)
    nt = pl.num_programs(1)

    @pl.when((pl.program_id(0) == 0) & (t == 0))
    def _():
        w_ref[...] = wf_ref[...].astype(BF16)

    sc = 1.0 + sc_ref[...]
    sh = sh_ref[...]
    hx = (x_ref[...] * sc + sh).astype(BF16)

    pq = _bdot(hx, w_ref[:, 0:QKV_WIDTH])
    cos = cos_ref[...]
    sa = sa_ref[...]
    sb = sb_ref[...]

    def rope(z):
        return (z * cos + pltpu.roll(z, ROPE_HALF_PAIR, 1) * sa
                + pltpu.roll(z, LANES - ROPE_HALF_PAIR, 1) * sb)

    for j in range(ATTN_WIDTH // LANES):
        sl = slice(j * LANES, (j + 1) * LANES)
        q_ref[:, sl] = (rope(pq[:, sl]) * ATTN_SCALE).astype(BF16)
    kr = rope(pq[:, ATTN_WIDTH:ATTN_WIDTH + KV_WIDTH])
    k_ref[...] = kr.astype(BF16)
    ksw_ref[...] = pltpu.roll(kr, HEAD_DIM, 1).astype(BF16)
    vv = pq[:, ATTN_WIDTH + KV_WIDTH:QKV_WIDTH]
    v_ref[...] = vv.astype(BF16)
    vsw_ref[...] = pltpu.roll(vv, HEAD_DIM, 1).astype(BF16)

    pm = _bdot(hx, w_ref[:, QKV_WIDTH:IN_WIDTH])
    cb = pm[:, 0:CONV_WIDTH]
    u = pm[:, CONV_WIDTH:2 * CONV_WIDTH] * pm[:, 2 * CONV_WIDTH:3 * CONV_WIDTH]

    halo = jnp.concatenate([xp_ref[...], xn_ref[...]], axis=0)
    hh = (halo * sc + sh).astype(BF16)
    ph = _bdot(hh, w_ref[:, QKV_WIDTH + CONV_WIDTH:QKV_WIDTH + 3 * CONV_WIDTH])
    uh = ph[:, 0:CONV_WIDTH] * ph[:, CONV_WIDTH:2 * CONV_WIDTH]
    up_row = jnp.where(t > 0, uh[SUBLANES - 1:SUBLANES, :], 0.0)
    dn_row = jnp.where(t < nt - 1, uh[SUBLANES:SUBLANES + 1, :], 0.0)
    ridx = lax.broadcasted_iota(jnp.int32, (ts, CONV_WIDTH), 0)
    u_up = jnp.where(ridx == 0, up_row, pltpu.roll(u, 1, 0))
    u_dn = jnp.where(ridx == ts - 1, dn_row, pltpu.roll(u, ts - 1, 0))
    cw = cw_ref[...]
    y_conv = cb * (u_up * cw[0:1, :] + u * cw[1:2, :] + u_dn * cw[2:3, :])
    y_ref[:, 0:CONV_WIDTH] = y_conv.astype(BF16)

    gu = jax.nn.gelu(pm[:, 3 * CONV_WIDTH:3 * CONV_WIDTH + GM_WIDTH])
    gv = _layer_norm(jax.nn.gelu(pm[:, 3 * CONV_WIDTH + GM_WIDTH:3 * CONV_WIDTH + 2 * GM_WIDTH])).astype(BF16)
    lane = lax.broadcasted_iota(jnp.int32, (CHUNK, LANES), 1)
    zero = jnp.zeros((CHUNK, LANES), BF16)
    for c in range(ts // CHUNK):
        rows = slice(c * CHUNK, (c + 1) * CHUNK)
        for j in range(GM_WIDTH // LANES):
            cols = slice(j * LANES, (j + 1) * LANES)
            vp = gv[rows, cols]
            s = (_bdot(gw_ref[2 * j], jnp.where(lane < GM_HEAD, vp, zero))
                 + _bdot(gw_ref[2 * j + 1], jnp.where(lane >= GM_HEAD, vp, zero))
                 + gb_ref[:, cols])
            y_ref[rows, CONV_WIDTH + j * LANES:CONV_WIDTH + (j + 1) * LANES] = (gu[rows, cols] * s).astype(BF16)


def _inproj_call(x2d, row_off, n_seq, seq_len, ts, sc, sh, w_in, layer, cos, sa, sb, conv_w, gm_w, gm_b):
    nt = seq_len // ts
    off_t = row_off // ts
    off_8 = row_off // SUBLANES
    last_8 = x2d.shape[0] // SUBLANES - 1
    per_tile_8 = ts // SUBLANES
    n_out = n_seq * seq_len
    d = D_MODEL

    def tile_idx(b, t):
        return (off_t + b * nt + t, 0)

    def prev_idx(b, t):
        return (jnp.maximum(off_8 + (b * nt + t) * per_tile_8 - 1, 0), 0)

    def next_idx(b, t):
        return (jnp.minimum(off_8 + (b * nt + t + 1) * per_tile_8, last_8), 0)

    def out_idx(b, t):
        return (b * nt + t, 0)

    vec = pl.BlockSpec((None, 1, d), lambda b, t: (b, 0, 0))
    rope_spec = pl.BlockSpec((ts, LANES), lambda b, t: (t, 0))
    kv_spec = pl.BlockSpec((ts, KV_WIDTH), out_idx)
    kv_shape = jax.ShapeDtypeStruct((n_out, KV_WIDTH), BF16)
    return pl.pallas_call(
        functools.partial(_inproj_kernel, ts=ts),
        grid=(n_seq, nt),
        in_specs=[
            pl.BlockSpec((ts, d), tile_idx),
            pl.BlockSpec((SUBLANES, d), prev_idx),
            pl.BlockSpec((SUBLANES, d), next_idx),
            vec, vec,
            pl.BlockSpec((None, d, IN_WIDTH), lambda b, t: (layer, 0, 0)),
            rope_spec, rope_spec, rope_spec,
            pl.BlockSpec((3, CONV_WIDTH), lambda b, t: (0, 0)),
            pl.BlockSpec((GM_GROUPS, CHUNK, CHUNK), lambda b, t: (0, 0, 0)),
            pl.BlockSpec((CHUNK, GM_WIDTH), lambda b, t: (0, 0)),
        ],
        out_specs=[
            pl.BlockSpec((ts, ATTN_WIDTH), out_idx),
            kv_spec, kv_spec, kv_spec, kv_spec,
            pl.BlockSpec((ts, LOCAL_WIDTH), out_idx),
        ],
        out_shape=[
            jax.ShapeDtypeStruct((n_out, ATTN_WIDTH), BF16),
            kv_shape, kv_shape, kv_shape, kv_shape,
            jax.ShapeDtypeStruct((n_out, LOCAL_WIDTH), BF16),
        ],
        scratch_shapes=[pltpu.VMEM((d, IN_WIDTH), BF16)],
        compiler_params=_params("arbitrary", "arbitrary"),
        name="in_projection",
    )(x2d, x2d, x2d, sc, sh, w_in, cos, sa, sb, conv_w, gm_w, gm_b)


def _attn_body(q, keys, keys_sw, vals, vals_sw, bias, sink_ref, o_ref, tq):
    nk = keys.shape[0]
    lane = lax.broadcasted_iota(jnp.int32, (nk, LANES), 1)
    lo = lane < HEAD_DIM
    zero = jnp.zeros((nk, LANES), BF16)
    rid = lax.broadcasted_iota(jnp.int32, (2 * tq, 1), 0)
    nt_dims = (((1,), (1,)), ((), ()))
    gqa = N_HEADS // N_KV_HEADS
    for h in range(N_KV_HEADS):
        k_own, k_oth = (keys, keys_sw) if h == 0 else (keys_sw, keys)
        v_own, v_oth = (vals, vals_sw) if h == 0 else (vals_sw, vals)
        kz = (jnp.where(lo, k_own, zero), jnp.where(lo, zero, k_oth))
        vz = (jnp.where(lo, v_own, zero), jnp.where(lo, zero, v_oth))
        c0 = h * gqa * HEAD_DIM
        qs = jnp.concatenate([q[:, c0:c0 + LANES], q[:, c0 + LANES:c0 + 2 * LANES]], axis=0)
        acc = None
        for par in range(2):
            s = lax.dot_general(qs, kz[par], nt_dims, preferred_element_type=F32)
            if bias is not None:
                s = s + bias
            sink = jnp.where(rid < tq, sink_ref[h * gqa + par], sink_ref[h * gqa + 2 + par])
            m = jnp.maximum(jnp.max(s, axis=-1, keepdims=True), sink)
            p = jnp.exp(s - m)
            den = jnp.sum(p, axis=-1, keepdims=True) + jnp.exp(sink - m)
            o = _bdot(p.astype(BF16), vz[par]) / den
            acc = o if acc is None else acc + o
        o_ref[:, c0:c0 + LANES] = acc[0:tq].astype(BF16)
        o_ref[:, c0 + LANES:c0 + 2 * LANES] = acc[tq:2 * tq].astype(BF16)


def _win_attn_kernel(sink_ref, q_ref, kp_ref, kc_ref, kn_ref, kx_ref,
                     ksp_ref, ksc_ref, ksn_ref, ksx_ref,
                     vp_ref, vc_ref, vn_ref, vx_ref,
                     vsp_ref, vsc_ref, vsn_ref, vsx_ref, o_ref, *, tq, ctx_len):
    n = pl.program_id(1)
    nb = pl.num_programs(1)
    row = lax.broadcasted_iota(jnp.int32, (tq, tq), 0)
    col = lax.broadcasted_iota(jnp.int32, (tq, tq), 1)
    b_prev = jnp.where((col >= row) & (n > 0), 0.0, NEG_INF)
    b_next = jnp.where((col <= row) & (n < nb - 1), 0.0, NEG_INF)
    b_half = jnp.concatenate([b_prev, jnp.zeros((tq, tq), F32), b_next, jnp.zeros((tq, ctx_len), F32)], axis=1)
    bias = jnp.concatenate([b_half, b_half], axis=0)

    def cat(a, b, c, d):
        return jnp.concatenate([a[...], b[...], c[...], d[...]], axis=0)

    _attn_body(q_ref[...], cat(kp_ref, kc_ref, kn_ref, kx_ref), cat(ksp_ref, ksc_ref, ksn_ref, ksx_ref),
               cat(vp_ref, vc_ref, vn_ref, vx_ref), cat(vsp_ref, vsc_ref, vsn_ref, vsx_ref),
               bias, sink_ref, o_ref, tq)


def _win_attn_call(sink, q, k, ksw, v, vsw, kx, kxsw, vx, vxsw, n_seq, seq_len, ctx_len):
    tq = WINDOW
    nb = seq_len // tq

    def cur(b, n):
        return (b * nb + n, 0)

    def prev(b, n):
        return (b * nb + jnp.maximum(n - 1, 0), 0)

    def nxt(b, n):
        return (b * nb + jnp.minimum(n + 1, nb - 1), 0)

    def ctx(b, n):
        return (b, 0)

    def kv_specs():
        return [pl.BlockSpec((tq, KV_WIDTH), prev), pl.BlockSpec((tq, KV_WIDTH), cur),
                pl.BlockSpec((tq, KV_WIDTH), nxt), pl.BlockSpec((ctx_len, KV_WIDTH), ctx)]

    return pl.pallas_call(
        functools.partial(_win_attn_kernel, tq=tq, ctx_len=ctx_len),
        grid=(n_seq, nb),
        in_specs=[pl.BlockSpec(memory_space=pltpu.SMEM), pl.BlockSpec((tq, ATTN_WIDTH), cur)]
        + kv_specs() + kv_specs() + kv_specs() + kv_specs(),
        out_specs=pl.BlockSpec((tq, ATTN_WIDTH), cur),
        out_shape=jax.ShapeDtypeStruct((n_seq * seq_len, ATTN_WIDTH), BF16),
        compiler_params=_params("parallel", "parallel"),
        name="window_attention",
    )(sink, q, k, k, k, kx, ksw, ksw, ksw, kxsw, v, v, v, vx, vsw, vsw, vsw, vxsw)


def _ctx_attn_kernel(sink_ref, q_ref, k_ref, ks_ref, v_ref, vs_ref, o_ref, *, tq):
    _attn_body(q_ref[...], k_ref[...], ks_ref[...], v_ref[...], vs_ref[...], None, sink_ref, o_ref, tq)


def _ctx_attn_call(sink, q, k, ksw, v, vsw, n_seq, ctx_len):
    def blk(w):
        return pl.BlockSpec((ctx_len, w), lambda b: (b, 0))

    return pl.pallas_call(
        functools.partial(_ctx_attn_kernel, tq=ctx_len),
        grid=(n_seq,),
        in_specs=[pl.BlockSpec(memory_space=pltpu.SMEM), blk(ATTN_WIDTH),
                  blk(KV_WIDTH), blk(KV_WIDTH), blk(KV_WIDTH), blk(KV_WIDTH)],
        out_specs=blk(ATTN_WIDTH),
        out_shape=jax.ShapeDtypeStruct((n_seq * ctx_len, ATTN_WIDTH), BF16),
        compiler_params=_params("parallel"),
        name="context_attention",
    )(sink, q, k, ksw, v, vsw)


OUT_TILE = 256


def _outproj_kernel(attx_ref, attc_ref, yx_ref, yc_ref, xx_ref, xc_ref, wof_ref, g_ref, lng_ref, lnb_ref,
                    sc_ref, sh_ref, wr_ref, br_ref, x1_ref, h2t_ref, route_ref, cnt_ref, wo_ref,
                    *, alpha, n_lat_tiles):
    @pl.when(pl.program_id(0) == 0)
    def _():
        wo_ref[...] = wof_ref[...].astype(BF16)

    is_lat = pl.program_id(0) < n_lat_tiles
    att = jnp.where(is_lat, attx_ref[...], attc_ref[...])
    yloc = jnp.where(is_lat, yx_ref[...], yc_ref[...])
    xin = jnp.where(is_lat, xx_ref[...], xc_ref[...])
    o = _bdot(att, wo_ref[0:ATTN_WIDTH, :]) + _bdot(yloc, wo_ref[ATTN_WIDTH:, :])
    x1 = _layer_norm(alpha * xin + g_ref[...] * o) * lng_ref[...] + lnb_ref[...]
    x1_ref[...] = x1
    h2 = x1 * (1.0 + sc_ref[...]) + sh_ref[...]
    for s in range(ROW_TILE_ROWS):
        h2t_ref[:, s, :] = h2[:, s * LANES:(s + 1) * LANES]

    hh, hl = _split_bf16(h2)
    t1 = _bdot(hh, wr_ref[...])
    lg = t1[:, 0:ROUTER_COLS] + t1[:, ROUTER_COLS:] + _bdot(hl, wr_ref[:, 0:ROUTER_COLS]) + br_ref[...]

    ts = lg.shape[0]
    lane = lax.broadcasted_iota(jnp.int32, (ts, ROUTER_COLS), 1).astype(F32)
    big = jnp.float32(ROUTER_COLS)

    def top1(v):
        m = jnp.max(v, axis=-1, keepdims=True)
        return m, jnp.min(jnp.where(v == m, lane, big), axis=-1, keepdims=True)

    gl = jnp.where(lane < N_GROUPS, lg, NEG_INF)
    g_val, g_idx = top1(gl)
    lse = g_val + jnp.log(jnp.sum(jnp.exp(gl - g_val), axis=-1, keepdims=True))
    p_group = jnp.exp(g_val - lse)
    e_lo = N_GROUPS + EXP_PER_GROUP * g_idx
    el = jnp.where((lane >= e_lo) & (lane < e_lo + EXP_PER_GROUP), lg, NEG_INF)
    e1, l1 = top1(el)
    e2, l2 = top1(jnp.where(lane == l1, NEG_INF, el))
    z = jnp.exp(e2 - e1)
    gate1 = p_group / (1.0 + z)
    gate2 = p_group * z / (1.0 + z)
    x1id = l1 - N_GROUPS
    x2id = l2 - N_GROUPS

    sel1 = lane == x1id
    sel2 = lane == x2id
    onehot = jnp.where(sel1 | sel2, 1.0, 0.0)
    r_i = lax.broadcasted_iota(jnp.int32, (ts, ts), 0)
    c_i = lax.broadcasted_iota(jnp.int32, (ts, ts), 1)
    before = jnp.where(c_i < r_i, 1.0, 0.0).astype(BF16)
    prefix = _bdot(before, onehot.astype(BF16))
    rank1 = jnp.sum(jnp.where(sel1, prefix, 0.0), axis=-1, keepdims=True)
    rank2 = jnp.sum(jnp.where(sel2, prefix, 0.0), axis=-1, keepdims=True)
    route = jnp.zeros((ts, ROUTER_COLS), F32)
    for col, val in enumerate((x1id, x2id, gate1, gate2, rank1, rank2)):
        route = jnp.where(lane == col, val, route)
    route_ref[...] = route
    cnt_ref[...] = jnp.broadcast_to(jnp.sum(onehot, axis=0, keepdims=True), (SUBLANES, ROUTER_COLS))


def _outproj_call(att_x, att_c, y_x, y_c, x_arr, x_off, c_arr, c_off, n_lat, n_ctx, seq_len,
                  w_out, layer, gvecs, lng, lnb, scvecs, shvecs, wr, br, alpha):
    ts = OUT_TILE
    d = D_MODEL
    n_lat_tiles = n_lat // ts
    n_tiles = (n_lat + n_ctx) // ts
    tiles_per_seq = seq_len // ts
    n_seq = n_lat // seq_len
    xo, co = x_off // ts, c_off // ts

    def lat_loc(i):
        return (jnp.minimum(i, n_lat_tiles - 1), 0)

    def ctx_loc(i):
        return (jnp.maximum(i - n_lat_tiles, 0), 0)

    def lat_in(i):
        return (xo + jnp.minimum(i, n_lat_tiles - 1), 0)

    def ctx_in(i):
        return (co + jnp.maximum(i - n_lat_tiles, 0), 0)

    def vec_idx(i):
        return (jnp.where(i < n_lat_tiles, i // tiles_per_seq, n_seq), 0, 0)

    vecb = pl.BlockSpec((None, 1, d), vec_idx)
    vec0 = pl.BlockSpec((1, d), lambda i: (0, 0))
    return pl.pallas_call(
        functools.partial(_outproj_kernel, alpha=alpha, n_lat_tiles=n_lat_tiles),
        grid=(n_tiles,),
        in_specs=[
            pl.BlockSpec((ts, ATTN_WIDTH), lat_loc), pl.BlockSpec((ts, ATTN_WIDTH), ctx_loc),
            pl.BlockSpec((ts, LOCAL_WIDTH), lat_loc), pl.BlockSpec((ts, LOCAL_WIDTH), ctx_loc),
            pl.BlockSpec((ts, d), lat_in), pl.BlockSpec((ts, d), ctx_in),
            pl.BlockSpec((None, d, d), lambda i: (layer, 0, 0)),
            vecb, vec0, vec0, vecb, vecb,
            pl.BlockSpec((d, 2 * ROUTER_COLS), lambda i: (0, 0)),
            pl.BlockSpec((1, ROUTER_COLS), lambda i: (0, 0)),
        ],
        out_specs=[pl.BlockSpec((ts, d), lambda i: (i, 0)),
                   pl.BlockSpec((ts, ROW_TILE_ROWS, LANES), lambda i: (i, 0, 0)),
                   pl.BlockSpec((ts, ROUTER_COLS), lambda i: (i, 0)),
                   pl.BlockSpec((None, SUBLANES, ROUTER_COLS), lambda i: (i, 0, 0))],
        out_shape=[jax.ShapeDtypeStruct((n_lat + n_ctx, d), F32),
                   jax.ShapeDtypeStruct((n_lat + n_ctx, ROW_TILE_ROWS, LANES), F32),
                   jax.ShapeDtypeStruct((n_lat + n_ctx, ROUTER_COLS), F32),
                   jax.ShapeDtypeStruct((n_tiles, SUBLANES, ROUTER_COLS), F32)],
        scratch_shapes=[pltpu.VMEM((d, d), BF16)],
        compiler_params=_params("arbitrary"),
        name="out_projection",
    )(att_x, att_c, y_x, y_c, x_arr, c_arr, w_out, gvecs, lng, lnb, scvecs, shvecs, wr, br)


DISPATCH_TILE = 512


def _dispatch_kernel(seg_ref, pos_ref, h_ref, xs_out, zbuf, zsem, sem):
    @pl.when(pl.program_id(0) == 0)
    def _():
        zbuf[...] = jnp.zeros_like(zbuf)

        def zero_copy(e):
            return pltpu.make_async_copy(zbuf, xs_out.at[pl.ds(seg_ref[0, e] - MOE_BLOCK, MOE_BLOCK)], zsem)

        for e in range(N_EXPERTS):
            @pl.when(seg_ref[1, e] > 0)
            def _():
                zero_copy(e).start()
        for e in range(N_EXPERTS):
            @pl.when(seg_ref[1, e] > 0)
            def _():
                zero_copy(e).wait()

        def tail_copy(b):
            return pltpu.make_async_copy(zbuf, xs_out.at[pl.ds(b * MOE_BLOCK, MOE_BLOCK)], zsem)

        n_blocks = xs_out.shape[0] // MOE_BLOCK
        first_unused = seg_ref[0, N_EXPERTS - 1] // MOE_BLOCK
        lax.fori_loop(first_unused, n_blocks, lambda b, c: (tail_copy(b).start(), c)[1], 0)
        lax.fori_loop(first_unused, n_blocks, lambda b, c: (tail_copy(b).wait(), c)[1], 0)

    def body(t, carry):
        for k in range(TOP_K):
            pltpu.make_async_copy(h_ref.at[t], xs_out.at[pos_ref[0, 0, TOP_K * t + k]], sem).start(priority=k)
        return carry

    lax.fori_loop(0, DISPATCH_TILE, body, 0, unroll=4)
    for k in range(TOP_K):
        pltpu.make_async_copy(h_ref, h_ref, sem).wait()


def _dispatch_call(seg, pos, h2t, n_slots):
    n_tok = h2t.shape[0]
    n_tiles = n_tok // DISPATCH_TILE
    row = (ROW_TILE_ROWS, LANES)
    grid_spec = pltpu.PrefetchScalarGridSpec(
        num_scalar_prefetch=1,
        grid=(n_tiles,),
        in_specs=[
            pl.BlockSpec((1, 1, TOP_K * DISPATCH_TILE), lambda i, seg: (i, 0, 0), memory_space=pltpu.SMEM),
            pl.BlockSpec((DISPATCH_TILE,) + row, lambda i, seg: (i, 0, 0)),
        ],
        out_specs=pl.BlockSpec(memory_space=pl.ANY),
        scratch_shapes=[pltpu.VMEM((MOE_BLOCK,) + row, F32), pltpu.SemaphoreType.DMA(()),
                        pltpu.SemaphoreType.DMA(())],
    )
    return pl.pallas_call(
        _dispatch_kernel,
        grid_spec=grid_spec,
        out_shape=jax.ShapeDtypeStruct((n_slots,) + row, F32),
        compiler_params=_params("arbitrary"),
        name="moe_dispatch",
    )(seg, pos, h2t)


def _moe_kernel(be_ref, nu_ref, xs_ref, w1_ref, w3_ref, w2_ref, o_ref, w1b, w3b, w2b):
    i = pl.program_id(0)

    @pl.when((i == 0) | (be_ref[i] != be_ref[jnp.maximum(i - 1, 0)]))
    def _():
        w1b[...] = w1_ref[...].astype(BF16)
        w3b[...] = w3_ref[...].astype(BF16)
        w2b[...] = w2_ref[...].astype(BF16)

    @pl.when(i < nu_ref[0])
    def _():
        x = jnp.concatenate([xs_ref[:, s, :].astype(BF16) for s in range(ROW_TILE_ROWS)], axis=1)
        y = _bdot((jax.nn.silu(_bdot(x, w1b[...])) * _bdot(x, w3b[...])).astype(BF16), w2b[...])
        for s in range(ROW_TILE_ROWS):
            o_ref[:, s, :] = y[:, s * LANES:(s + 1) * LANES]

    @pl.when(i >= nu_ref[0])
    def _():
        o_ref[...] = jnp.zeros_like(o_ref)


def _moe_call(block_e, n_used, xs, w1, w3, w2, layer):
    n_blocks = block_e.shape[0]
    d = D_MODEL
    grid_spec = pltpu.PrefetchScalarGridSpec(
        num_scalar_prefetch=2,
        grid=(n_blocks,),
        in_specs=[
            pl.BlockSpec((MOE_BLOCK, ROW_TILE_ROWS, LANES), lambda i, be, nu: (jnp.minimum(i, nu[0] - 1), 0, 0)),
            pl.BlockSpec((None, None, d, D_EXPERT), lambda i, be, nu: (layer, be[i], 0, 0)),
            pl.BlockSpec((None, None, d, D_EXPERT), lambda i, be, nu: (layer, be[i], 0, 0)),
            pl.BlockSpec((None, None, D_EXPERT, d), lambda i, be, nu: (layer, be[i], 0, 0)),
        ],
        out_specs=pl.BlockSpec((MOE_BLOCK, ROW_TILE_ROWS, LANES), lambda i, be, nu: (i, 0, 0)),
        scratch_shapes=[pltpu.VMEM((d, D_EXPERT), BF16), pltpu.VMEM((d, D_EXPERT), BF16),
                        pltpu.VMEM((D_EXPERT, d), BF16)],
    )
    return pl.pallas_call(
        _moe_kernel,
        grid_spec=grid_spec,
        out_shape=jax.ShapeDtypeStruct((n_blocks * MOE_BLOCK, ROW_TILE_ROWS, LANES), F32),
        compiler_params=_params("arbitrary"),
        name="moe_experts",
    )(block_e, n_used, xs, w1, w3, w2)


COMBINE_TILE = 256


def _combine_kernel(pos0_ref, posn_ref, x1_ref, route_ref, g_ref, lng_ref, lnb_ref, ys_hbm, o_ref, ybuf, sem,
                    *, alpha):
    i = pl.program_id(0)
    nb = pl.num_programs(0)

    def row_copy(p, slot, r):
        return pltpu.make_async_copy(ys_hbm.at[p], ybuf.at[slot, r], sem.at[slot])

    def issue(pos_ref, slot):
        def body(j, carry):
            for k in range(N_DMA_PRIORITIES):
                r = N_DMA_PRIORITIES * j + k
                row_copy(pos_ref[0, 0, r], slot, r).start(priority=k)
            return carry
        lax.fori_loop(0, TOP_K * COMBINE_TILE // N_DMA_PRIORITIES, body, 0, unroll=4)

    @pl.when(i == 0)
    def _():
        issue(pos0_ref, 0)

    @pl.when(i + 1 < nb)
    def _():
        issue(posn_ref, (i + 1) % 2)

    slot = i % 2
    pltpu.make_async_copy(ybuf.at[slot], ybuf.at[slot], sem.at[slot]).wait()
    route = route_ref[...]
    gate1 = route[:, ROUTE_GATE:ROUTE_GATE + 1]
    gate2 = route[:, ROUTE_GATE + 1:ROUTE_GATE + 2]
    y = jnp.concatenate([ybuf[slot, 0:COMBINE_TILE, s, :] * gate1 + ybuf[slot, COMBINE_TILE:, s, :] * gate2
                         for s in range(ROW_TILE_ROWS)], axis=1)
    o_ref[...] = _layer_norm(alpha * x1_ref[...] + g_ref[...] * y) * lng_ref[...] + lnb_ref[...]


def _combine_call(pos, x1, route, gvecs, lng, lnb, ys, n_tok, tiles_per_seq, n_seq, alpha):
    d = D_MODEL
    n_tiles = n_tok // COMBINE_TILE
    n_lat_tiles = tiles_per_seq * n_seq

    def g_idx(i):
        return (jnp.where(i < n_lat_tiles, i // tiles_per_seq, n_seq), 0, 0)

    smem_blk = (1, 1, TOP_K * COMBINE_TILE)
    return pl.pallas_call(
        functools.partial(_combine_kernel, alpha=alpha),
        grid=(n_tiles,),
        in_specs=[
            pl.BlockSpec(smem_blk, lambda i: (0, 0, 0), memory_space=pltpu.SMEM),
            pl.BlockSpec(smem_blk, lambda i: (jnp.minimum(i + 1, n_tiles - 1), 0, 0), memory_space=pltpu.SMEM),
            pl.BlockSpec((COMBINE_TILE, d), lambda i: (i, 0)),
            pl.BlockSpec((COMBINE_TILE, ROUTER_COLS), lambda i: (i, 0)),
            pl.BlockSpec((None, 1, d), g_idx),
            pl.BlockSpec((1, d), lambda i: (0, 0)),
            pl.BlockSpec((1, d), lambda i: (0, 0)),
            pl.BlockSpec(memory_space=pl.ANY),
        ],
        out_specs=pl.BlockSpec((COMBINE_TILE, d), lambda i: (i, 0)),
        out_shape=jax.ShapeDtypeStruct((n_tok, d), F32),
        scratch_shapes=[pltpu.VMEM((2, TOP_K * COMBINE_TILE, ROW_TILE_ROWS, LANES), F32),
                        pltpu.SemaphoreType.DMA((2,))],
        compiler_params=_params("arbitrary"),
        name="moe_combine",
    )(pos, pos, x1, route, gvecs, lng, lnb, ys)


def _slots(route, tile_counts):
    n = route.shape[0]
    n_tiles = tile_counts.shape[0]
    tc = tile_counts.astype(jnp.int32)
    counts = jnp.sum(tc, axis=0)
    padded = (counts + MOE_BLOCK - 1) // MOE_BLOCK * MOE_BLOCK
    pends = jnp.cumsum(padded)
    base = (pends - padded)[None, :] + jnp.cumsum(tc, axis=0) - tc
    n_blocks = -(-n * TOP_K // MOE_BLOCK) + N_EXPERTS
    block_start = jnp.arange(n_blocks, dtype=jnp.int32) * MOE_BLOCK
    block_e = jnp.minimum(jnp.sum(pends[None, 0:N_EXPERTS] <= block_start[:, None], axis=1),
                          N_EXPERTS - 1).astype(jnp.int32)
    n_used = (pends[N_EXPERTS - 1] // MOE_BLOCK).astype(jnp.int32).reshape(1)
    seg = jnp.stack([pends[0:N_EXPERTS], padded[0:N_EXPERTS]]).astype(jnp.int32)
    r3 = route.reshape(n_tiles, n // n_tiles, ROUTER_COLS)
    lane = lax.broadcasted_iota(jnp.int32, r3.shape, 2)
    pos = []
    for k in range(TOP_K):
        e = r3[:, :, ROUTE_EXPERT + k].astype(jnp.int32)
        b = jnp.sum(jnp.where(lane == e[:, :, None], base[:, None, :], 0), axis=-1)
        pos.append((b + r3[:, :, ROUTE_RANK + k].astype(jnp.int32)).reshape(n))
    return block_e, n_used, seg, jnp.stack(pos, axis=-1)


def _rope_tables(seq_len):
    m = HEAD_DIM // 4
    freqs = ROPE_BASE ** (-jnp.arange(m, dtype=F32) / m)
    t = jnp.arange(seq_len)
    row = (t // GRID_W).astype(F32)[:, None] * freqs[None, :]
    col = (t % GRID_W).astype(F32)[:, None] * freqs[None, :]
    cos = jnp.concatenate([jnp.cos(row), jnp.cos(row), jnp.cos(col), jnp.cos(col)], axis=-1)
    zero = jnp.zeros_like(row)
    sin_a = jnp.concatenate([zero, jnp.sin(row), zero, jnp.sin(col)], axis=-1)
    sin_b = jnp.concatenate([-jnp.sin(row), zero, -jnp.sin(col), zero], axis=-1)
    rep = LANES // HEAD_DIM
    return jnp.tile(cos, (1, rep)), jnp.tile(sin_a, (1, rep)), jnp.tile(sin_b, (1, rep))


def kernel(x, c, ctx, c_ctx, w_ada, b_ada, w_in, conv_w, attn_sink, gm_ws, gm_bs, w_out, ln1_g, ln1_b,
           w_rg, b_rg, w_re, b_re, w1, w3, w2, ln2_g, ln2_b):
    b_, s_, d_ = x.shape
    c_len = ctx.shape[1]
    depth = w_ada.shape[0]
    alpha = (2 * depth) ** 0.25
    n_lat = b_ * s_
    n_ctx = b_ * c_len
    ts = 512

    cin = jnp.zeros((ADA_ROWS, d_), F32).at[0:b_].set(c).at[b_].set(c_ctx)
    mod = _ada_call(cin, w_ada, b_ada)

    cos, sin_a, sin_b = _rope_tables(s_)
    ones_c = jnp.ones((c_len, LANES), F32)
    zeros_c = jnp.zeros((c_len, LANES), F32)

    x_flat = x.reshape(n_lat, d_)
    c_flat = ctx.reshape(n_ctx, d_)
    x_off, c_off = 0, 0
    x_arr, c_arr = x_flat, c_flat

    for l in range(depth):
        last = l == depth - 1
        mx = mod[l, 0:b_].reshape(b_, 6, 1, d_)
        sh1, sc1, g1, sh2, sc2, g2 = (mx[:, i] for i in range(6))
        mc = jnp.broadcast_to(mod[l, b_].reshape(1, 6, 1, d_), (b_, 6, 1, d_))
        gm_w = gm_ws[l].astype(BF16)
        gm_b = jnp.repeat(gm_bs[l].T, GM_HEAD, axis=1)
        sink = attn_sink[l]

        qx, kx, kxs, vx, vxs, yx = _inproj_call(x_arr, x_off, b_, s_, ts, sc1, sh1, w_in, l, cos, sin_a, sin_b,
                                                conv_w[l], gm_w, gm_b)
        qc, kc, kcs, vc, vcs, yc = _inproj_call(c_arr, c_off, b_, c_len, c_len, mc[:, 1], mc[:, 0], w_in, l,
                                                ones_c, zeros_c, zeros_c, conv_w[l], gm_w, gm_b)
        att_x = _win_attn_call(sink, qx, kx, kxs, vx, vxs, kc, kcs, vc, vcs, b_, s_, c_len)

        w_r = jnp.zeros((d_, ROUTER_COLS), F32).at[:, 0:N_GROUPS].set(w_rg[l]) \
            .at[:, N_GROUPS:N_GROUPS + N_EXPERTS].set(w_re[l])
        w_r_hi = w_r.astype(BF16)
        w_r_lo = (w_r - w_r_hi.astype(F32)).astype(BF16)
        wr = jnp.concatenate([w_r_hi, w_r_lo], axis=1)
        br = jnp.zeros((1, ROUTER_COLS), F32).at[0, 0:N_GROUPS].set(b_rg[l]) \
            .at[0, N_GROUPS:N_GROUPS + N_EXPERTS].set(b_re[l])
        lng1, lnb1 = ln1_g[l].reshape(1, d_), ln1_b[l].reshape(1, d_)
        lng2, lnb2 = ln2_g[l].reshape(1, d_), ln2_b[l].reshape(1, d_)

        n_tot = n_lat if last else n_lat + n_ctx
        gvecs1 = jnp.concatenate([g1, mc[0:1, 2]], axis=0)
        scvecs2 = jnp.concatenate([sc2, mc[0:1, 4]], axis=0)
        shvecs2 = jnp.concatenate([sh2, mc[0:1, 3]], axis=0)
        if last:
            x1, h2t, route, tcnt = _outproj_call(att_x, att_x, yx, yx, x_arr, x_off, x_arr, x_off, n_lat, 0, s_,
                                                 w_out, l, gvecs1, lng1, lnb1, scvecs2, shvecs2, wr, br, alpha)
        else:
            att_c = _ctx_attn_call(sink, qc, kc, kcs, vc, vcs, b_, c_len)
            x1, h2t, route, tcnt = _outproj_call(att_x, att_c, yx, yc, x_arr, x_off, c_arr, c_off, n_lat, n_ctx,
                                                 s_, w_out, l, gvecs1, lng1, lnb1, scvecs2, shvecs2, wr, br, alpha)

        block_e, n_used, seg, pos = _slots(route, tcnt[:, 0, :])
        n_slots = block_e.shape[0] * MOE_BLOCK
        xs = _dispatch_call(seg, pos.reshape(n_tot // DISPATCH_TILE, 1, TOP_K * DISPATCH_TILE), h2t, n_slots)
        ys = _moe_call(block_e, n_used, xs, w1, w3, w2, l)

        n_tiles = n_tot // COMBINE_TILE
        pos_t = pos.reshape(n_tiles, COMBINE_TILE, TOP_K).transpose(0, 2, 1).reshape(n_tiles, 1, TOP_K * COMBINE_TILE)
        gvecs = jnp.concatenate([g2, mc[0:1, 5]], axis=0)
        out = _combine_call(pos_t, x1, route, gvecs, lng2, lnb2, ys, n_tot, s_ // COMBINE_TILE, b_, alpha)
        x_arr, x_off = out, 0
        c_arr, c_off = out, n_lat

    return x_arr.reshape(b_, s_, d_)
```

```python
import functools

import jax
import jax.numpy as jnp
from jax import lax
from jax.experimental import pallas as pl
from jax.experimental.pallas import tpu as pltpu

F32 = jnp.float32
BF16 = jnp.bfloat16

D_MODEL = 1024
GRID_W = 64
HEAD_DIM = 64
N_HEADS = 8
N_KV_HEADS = 2
ATTN_WIDTH = N_HEADS * HEAD_DIM
KV_WIDTH = N_KV_HEADS * HEAD_DIM
WINDOW = 128
ATTN_SCALE = HEAD_DIM ** -0.5
ROPE_BASE = 10000.0
CONV_WIDTH = D_MODEL // 4
GM_WIDTH = D_MODEL // 4
GM_GROUPS = 4
GM_HEAD = GM_WIDTH // GM_GROUPS
CHUNK = 128
LOCAL_WIDTH = CONV_WIDTH + GM_WIDTH
IN_WIDTH = ATTN_WIDTH + 2 * KV_WIDTH + 3 * CONV_WIDTH + 2 * GM_WIDTH
QKV_WIDTH = ATTN_WIDTH + 2 * KV_WIDTH
N_GROUPS = 4
EXP_PER_GROUP = 8
N_EXPERTS = N_GROUPS * EXP_PER_GROUP
TOP_K = 2
D_EXPERT = D_MODEL // 2
MOE_BLOCK = 256
LN_EPS = 1e-6
NEG_INF = -1e30

LANES = 128
SUBLANES = 8
VMEM_LIMIT_BYTES = 48 * 1024 * 1024
N_DMA_PRIORITIES = 2

ROW_TILE_ROWS = D_MODEL // LANES
assert ROW_TILE_ROWS == SUBLANES

ROUTER_COLS = LANES
ROUTE_EXPERT = 0
ROUTE_GATE = TOP_K
ROUTE_RANK = 2 * TOP_K
ROPE_HALF_PAIR = HEAD_DIM // 4


def _bdot(a, b):
    return jnp.dot(a, b, preferred_element_type=F32)


def _split_bf16(a):
    hi = a.astype(BF16)
    lo = (a - hi.astype(F32)).astype(BF16)
    return hi, lo


def _layer_norm(r):
    mu = jnp.mean(r, axis=-1, keepdims=True)
    d = r - mu
    var = jnp.mean(d * d, axis=-1, keepdims=True)
    return d * lax.rsqrt(var + LN_EPS)


def _params(*sem):
    return pltpu.CompilerParams(dimension_semantics=sem, vmem_limit_bytes=VMEM_LIMIT_BYTES)


def _rows_to_tiles(rows_ref, tiles_ref, sem):
    return [pltpu.make_async_copy(rows_ref.at[:, pl.ds(s * LANES, LANES)], tiles_ref.at[:, s, :], sem)
            for s in range(ROW_TILE_ROWS)]


def _tiles_to_rows(tiles_ref, rows_ref, sem):
    return [pltpu.make_async_copy(tiles_ref.at[:, s, :], rows_ref.at[:, pl.ds(s * LANES, LANES)], sem)
            for s in range(ROW_TILE_ROWS)]


def _start_all(copies):
    for cp in copies:
        cp.start()


def _wait_all(copies):
    for cp in copies:
        cp.wait()


ADA_ROWS = 16
ADA_TILE = 1536


def _ada_kernel(c_ref, w_ref, b_ref, o_ref):
    act = jax.nn.silu(c_ref[...])
    ah, al = _split_bf16(act)
    wh, wl = _split_bf16(w_ref[...])
    o_ref[...] = _bdot(ah, wh) + _bdot(ah, wl) + _bdot(al, wh) + b_ref[...]


def _ada_call(cin, w_ada, b_ada):
    depth, d, n = w_ada.shape
    return pl.pallas_call(
        _ada_kernel,
        grid=(depth, n // ADA_TILE),
        in_specs=[
            pl.BlockSpec((ADA_ROWS, d), lambda l, j: (0, 0)),
            pl.BlockSpec((None, d, ADA_TILE), lambda l, j: (l, 0, j)),
            pl.BlockSpec((None, 1, ADA_TILE), lambda l, j: (l, 0, j)),
        ],
        out_specs=pl.BlockSpec((None, ADA_ROWS, ADA_TILE), lambda l, j: (l, 0, j)),
        out_shape=jax.ShapeDtypeStruct((depth, ADA_ROWS, n), F32),
        compiler_params=_params("parallel", "parallel"),
        name="ada_modulation",
    )(cin, w_ada, b_ada.reshape(depth, 1, n))


def _inproj_kernel(x_ref, xp_ref, xn_ref, sc_ref, sh_ref, wf_ref, cos_ref, sa_ref, sb_ref,
                   cw_ref, gw_ref, gb_ref,
                   q_ref, k_ref, ksw_ref, v_ref, vsw_ref, y_ref, w_ref, *, ts):
    t = pl.program_id(1)
    nt = pl.num_programs(1)

    @pl.when((pl.program_id(0) == 0) & (t == 0))
    def _():
        w_ref[...] = wf_ref[...].astype(BF16)

    sc = 1.0 + sc_ref[...]
    sh = sh_ref[...]
    hx = (x_ref[...] * sc + sh).astype(BF16)

    pq = _bdot(hx, w_ref[:, 0:QKV_WIDTH])
    cos = cos_ref[...]
    sa = sa_ref[...]
    sb = sb_ref[...]

    def rope(z):
        return (z * cos + pltpu.roll(z, ROPE_HALF_PAIR, 1) * sa
                + pltpu.roll(z, LANES - ROPE_HALF_PAIR, 1) * sb)

    for j in range(ATTN_WIDTH // LANES):
        sl = slice(j * LANES, (j + 1) * LANES)
        q_ref[:, sl] = (rope(pq[:, sl]) * ATTN_SCALE).astype(BF16)
    kr = rope(pq[:, ATTN_WIDTH:ATTN_WIDTH + KV_WIDTH])
    k_ref[...] = kr.astype(BF16)
    ksw_ref[...] = pltpu.roll(kr, HEAD_DIM, 1).astype(BF16)
    vv = pq[:, ATTN_WIDTH + KV_WIDTH:QKV_WIDTH]
    v_ref[...] = vv.astype(BF16)
    vsw_ref[...] = pltpu.roll(vv, HEAD_DIM, 1).astype(BF16)

    pm = _bdot(hx, w_ref[:, QKV_WIDTH:IN_WIDTH])
    cb = pm[:, 0:CONV_WIDTH]
    u = pm[:, CONV_WIDTH:2 * CONV_WIDTH] * pm[:, 2 * CONV_WIDTH:3 * CONV_WIDTH]

    halo = jnp.concatenate([xp_ref[...], xn_ref[...]], axis=0)
    hh = (halo * sc + sh).astype(BF16)
    ph = _bdot(hh, w_ref[:, QKV_WIDTH + CONV_WIDTH:QKV_WIDTH + 3 * CONV_WIDTH])
    uh = ph[:, 0:CONV_WIDTH] * ph[:, CONV_WIDTH:2 * CONV_WIDTH]
    up_row = jnp.where(t > 0, uh[SUBLANES - 1:SUBLANES, :], 0.0)
    dn_row = jnp.where(t < nt - 1, uh[SUBLANES:SUBLANES + 1, :], 0.0)
    ridx = lax.broadcasted_iota(jnp.int32, (ts, CONV_WIDTH), 0)
    u_up = jnp.where(ridx == 0, up_row, pltpu.roll(u, 1, 0))
    u_dn = jnp.where(ridx == ts - 1, dn_row, pltpu.roll(u, ts - 1, 0))
    cw = cw_ref[...]
    y_conv = cb * (u_up * cw[0:1, :] + u * cw[1:2, :] + u_dn * cw[2:3, :])
    y_ref[:, 0:CONV_WIDTH] = y_conv.astype(BF16)

    gu = jax.nn.gelu(pm[:, 3 * CONV_WIDTH:3 * CONV_WIDTH + GM_WIDTH])
    gv = _layer_norm(jax.nn.gelu(pm[:, 3 * CONV_WIDTH + GM_WIDTH:3 * CONV_WIDTH + 2 * GM_WIDTH])).astype(BF16)
    lane = lax.broadcasted_iota(jnp.int32, (CHUNK, LANES), 1)
    zero = jnp.zeros((CHUNK, LANES), BF16)
    for c in range(ts // CHUNK):
        rows = slice(c * CHUNK, (c + 1) * CHUNK)
        for j in range(GM_WIDTH // LANES):
            cols = slice(j * LANES, (j + 1) * LANES)
            vp = gv[rows, cols]
            s = (_bdot(gw_ref[2 * j], jnp.where(lane < GM_HEAD, vp, zero))
                 + _bdot(gw_ref[2 * j + 1], jnp.where(lane >= GM_HEAD, vp, zero))
                 + gb_ref[:, cols])
            y_ref[rows, CONV_WIDTH + j * LANES:CONV_WIDTH + (j + 1) * LANES] = (gu[rows, cols] * s).astype(BF16)


def _inproj_call(x2d, row_off, n_seq, seq_len, ts, sc, sh, w_in, layer, cos, sa, sb, conv_w, gm_w, gm_b):
    nt = seq_len // ts
    off_t = row_off // ts
    off_8 = row_off // SUBLANES
    last_8 = x2d.shape[0] // SUBLANES - 1
    per_tile_8 = ts // SUBLANES
    n_out = n_seq * seq_len
    d = D_MODEL

    def tile_idx(b, t):
        return (off_t + b * nt + t, 0)

    def prev_idx(b, t):
        return (jnp.maximum(off_8 + (b * nt + t) * per_tile_8 - 1, 0), 0)

    def next_idx(b, t):
        return (jnp.minimum(off_8 + (b * nt + t + 1) * per_tile_8, last_8), 0)

    def out_idx(b, t):
        return (b * nt + t, 0)

    vec = pl.BlockSpec((None, 1, d), lambda b, t: (b, 0, 0))
    rope_spec = pl.BlockSpec((ts, LANES), lambda b, t: (t, 0))
    kv_spec = pl.BlockSpec((ts, KV_WIDTH), out_idx)
    kv_shape = jax.ShapeDtypeStruct((n_out, KV_WIDTH), BF16)
    return pl.pallas_call(
        functools.partial(_inproj_kernel, ts=ts),
        grid=(n_seq, nt),
        in_specs=[
            pl.BlockSpec((ts, d), tile_idx),
            pl.BlockSpec((SUBLANES, d), prev_idx),
            pl.BlockSpec((SUBLANES, d), next_idx),
            vec, vec,
            pl.BlockSpec((None, d, IN_WIDTH), lambda b, t: (layer, 0, 0)),
            rope_spec, rope_spec, rope_spec,
            pl.BlockSpec((3, CONV_WIDTH), lambda b, t: (0, 0)),
            pl.BlockSpec((GM_GROUPS, CHUNK, CHUNK), lambda b, t: (0, 0, 0)),
            pl.BlockSpec((CHUNK, GM_WIDTH), lambda b, t: (0, 0)),
        ],
        out_specs=[
            pl.BlockSpec((ts, ATTN_WIDTH), out_idx),
            kv_spec, kv_spec, kv_spec, kv_spec,
            pl.BlockSpec((ts, LOCAL_WIDTH), out_idx),
        ],
        out_shape=[
            jax.ShapeDtypeStruct((n_out, ATTN_WIDTH), BF16),
            kv_shape, kv_shape, kv_shape, kv_shape,
            jax.ShapeDtypeStruct((n_out, LOCAL_WIDTH), BF16),
        ],
        scratch_shapes=[pltpu.VMEM((d, IN_WIDTH), BF16)],
        compiler_params=_params("arbitrary", "arbitrary"),
        name="in_projection",
    )(x2d, x2d, x2d, sc, sh, w_in, cos, sa, sb, conv_w, gm_w, gm_b)


def _attn_body(q, keys, keys_sw, vals, vals_sw, bias, sink_ref, o_ref, tq):
    nk = keys.shape[0]
    lane = lax.broadcasted_iota(jnp.int32, (nk, LANES), 1)
    lo = lane < HEAD_DIM
    zero = jnp.zeros((nk, LANES), BF16)
    rid = lax.broadcasted_iota(jnp.int32, (2 * tq, 1), 0)
    nt_dims = (((1,), (1,)), ((), ()))
    gqa = N_HEADS // N_KV_HEADS
    for h in range(N_KV_HEADS):
        k_own, k_oth = (keys, keys_sw) if h == 0 else (keys_sw, keys)
        v_own, v_oth = (vals, vals_sw) if h == 0 else (vals_sw, vals)
        kz = (jnp.where(lo, k_own, zero), jnp.where(lo, zero, k_oth))
        vz = (jnp.where(lo, v_own, zero), jnp.where(lo, zero, v_oth))
        c0 = h * gqa * HEAD_DIM
        qs = jnp.concatenate([q[:, c0:c0 + LANES], q[:, c0 + LANES:c0 + 2 * LANES]], axis=0)
        acc = None
        for par in range(2):
            s = lax.dot_general(qs, kz[par], nt_dims, preferred_element_type=F32)
            if bias is not None:
                s = s + bias
            sink = jnp.where(rid < tq, sink_ref[h * gqa + par], sink_ref[h * gqa + 2 + par])
            m = jnp.maximum(jnp.max(s, axis=-1, keepdims=True), sink)
            p = jnp.exp(s - m)
            den = jnp.sum(p, axis=-1, keepdims=True) + jnp.exp(sink - m)
            o = _bdot(p.astype(BF16), vz[par]) / den
            acc = o if acc is None else acc + o
        o_ref[:, c0:c0 + LANES] = acc[0:tq].astype(BF16)
        o_ref[:, c0 + LANES:c0 + 2 * LANES] = acc[tq:2 * tq].astype(BF16)


def _win_attn_kernel(sink_ref, q_ref, kp_ref, kc_ref, kn_ref, kx_ref,
                     ksp_ref, ksc_ref, ksn_ref, ksx_ref,
                     vp_ref, vc_ref, vn_ref, vx_ref,
                     vsp_ref, vsc_ref, vsn_ref, vsx_ref, o_ref, *, tq, ctx_len):
    n = pl.program_id(1)
    nb = pl.num_programs(1)
    row = lax.broadcasted_iota(jnp.int32, (tq, tq), 0)
    col = lax.broadcasted_iota(jnp.int32, (tq, tq), 1)
    b_prev = jnp.where((col >= row) & (n > 0), 0.0, NEG_INF)
    b_next = jnp.where((col <= row) & (n < nb - 1), 0.0, NEG_INF)
    b_half = jnp.concatenate([b_prev, jnp.zeros((tq, tq), F32), b_next, jnp.zeros((tq, ctx_len), F32)], axis=1)
    bias = jnp.concatenate([b_half, b_half], axis=0)

    def cat(a, b, c, d):
        return jnp.concatenate([a[...], b[...], c[...], d[...]], axis=0)

    _attn_body(q_ref[...], cat(kp_ref, kc_ref, kn_ref, kx_ref), cat(ksp_ref, ksc_ref, ksn_ref, ksx_ref),
               cat(vp_ref, vc_ref, vn_ref, vx_ref), cat(vsp_ref, vsc_ref, vsn_ref, vsx_ref),
               bias, sink_ref, o_ref, tq)


def _win_attn_call(sink, q, k, ksw, v, vsw, kx, kxsw, vx, vxsw, n_seq, seq_len, ctx_len):
    tq = WINDOW
    nb = seq_len // tq

    def cur(b, n):
        return (b * nb + n, 0)

    def prev(b, n):
        return (b * nb + jnp.maximum(n - 1, 0), 0)

    def nxt(b, n):
        return (b * nb + jnp.minimum(n + 1, nb - 1), 0)

    def ctx(b, n):
        return (b, 0)

    def kv_specs():
        return [pl.BlockSpec((tq, KV_WIDTH), prev), pl.BlockSpec((tq, KV_WIDTH), cur),
                pl.BlockSpec((tq, KV_WIDTH), nxt), pl.BlockSpec((ctx_len, KV_WIDTH), ctx)]

    return pl.pallas_call(
        functools.partial(_win_attn_kernel, tq=tq, ctx_len=ctx_len),
        grid=(n_seq, nb),
        in_specs=[pl.BlockSpec(memory_space=pltpu.SMEM), pl.BlockSpec((tq, ATTN_WIDTH), cur)]
        + kv_specs() + kv_specs() + kv_specs() + kv_specs(),
        out_specs=pl.BlockSpec((tq, ATTN_WIDTH), cur),
        out_shape=jax.ShapeDtypeStruct((n_seq * seq_len, ATTN_WIDTH), BF16),
        compiler_params=_params("parallel", "parallel"),
        name="window_attention",
    )(sink, q, k, k, k, kx, ksw, ksw, ksw, kxsw, v, v, v, vx, vsw, vsw, vsw, vxsw)


def _ctx_attn_kernel(sink_ref, q_ref, k_ref, ks_ref, v_ref, vs_ref, o_ref, *, tq):
    _attn_body(q_ref[...], k_ref[...], ks_ref[...], v_ref[...], vs_ref[...], None, sink_ref, o_ref, tq)


def _ctx_attn_call(sink, q, k, ksw, v, vsw, n_seq, ctx_len):
    def blk(w):
        return pl.BlockSpec((ctx_len, w), lambda b: (b, 0))

    return pl.pallas_call(
        functools.partial(_ctx_attn_kernel, tq=ctx_len),
        grid=(n_seq,),
        in_specs=[pl.BlockSpec(memory_space=pltpu.SMEM), blk(ATTN_WIDTH),
                  blk(KV_WIDTH), blk(KV_WIDTH), blk(KV_WIDTH), blk(KV_WIDTH)],
        out_specs=blk(ATTN_WIDTH),
        out_shape=jax.ShapeDtypeStruct((n_seq * ctx_len, ATTN_WIDTH), BF16),
        compiler_params=_params("parallel"),
        name="context_attention",
    )(sink, q, k, ksw, v, vsw)


OUT_TILE = 256


def _outproj_kernel(attx_ref, attc_ref, yx_ref, yc_ref, xx_ref, xc_ref, wof_ref, g_ref, lng_ref, lnb_ref,
                    sc_ref, sh_ref, wr_ref, br_ref, x1_ref, h2t_hbm, route_ref, cnt_ref, wo_ref, hbuf, hsem,
                    *, alpha, n_lat_tiles):
    @pl.when(pl.program_id(0) == 0)
    def _():
        wo_ref[...] = wof_ref[...].astype(BF16)

    is_lat = pl.program_id(0) < n_lat_tiles
    att = jnp.where(is_lat, attx_ref[...], attc_ref[...])
    yloc = jnp.where(is_lat, yx_ref[...], yc_ref[...])
    xin = jnp.where(is_lat, xx_ref[...], xc_ref[...])
    o = _bdot(att, wo_ref[0:ATTN_WIDTH, :]) + _bdot(yloc, wo_ref[ATTN_WIDTH:, :])
    x1 = _layer_norm(alpha * xin + g_ref[...] * o) * lng_ref[...] + lnb_ref[...]
    x1_ref[...] = x1
    h2 = x1 * (1.0 + sc_ref[...]) + sh_ref[...]
    step = pl.program_id(0)
    slot = step % 2
    ts = h2.shape[0]

    def h2_copies(at_step, at_slot):
        return _rows_to_tiles(hbuf.at[at_slot], h2t_hbm.at[pl.ds(at_step * ts, ts)], hsem.at[at_slot])

    hbuf[slot] = h2
    _start_all(h2_copies(step, slot))

    @pl.when(step >= 1)
    def _():
        _wait_all(h2_copies(step - 1, 1 - slot))

    @pl.when(step == pl.num_programs(0) - 1)
    def _():
        _wait_all(h2_copies(step, slot))

    hh, hl = _split_bf16(h2)
    t1 = _bdot(hh, wr_ref[...])
    lg = t1[:, 0:ROUTER_COLS] + t1[:, ROUTER_COLS:] + _bdot(hl, wr_ref[:, 0:ROUTER_COLS]) + br_ref[...]

    ts = lg.shape[0]
    lane = lax.broadcasted_iota(jnp.int32, (ts, ROUTER_COLS), 1).astype(F32)
    big = jnp.float32(ROUTER_COLS)

    def top1(v):
        m = jnp.max(v, axis=-1, keepdims=True)
        return m, jnp.min(jnp.where(v == m, lane, big), axis=-1, keepdims=True)

    gl = jnp.where(lane < N_GROUPS, lg, NEG_INF)
    g_val, g_idx = top1(gl)
    lse = g_val + jnp.log(jnp.sum(jnp.exp(gl - g_val), axis=-1, keepdims=True))
    p_group = jnp.exp(g_val - lse)
    e_lo = N_GROUPS + EXP_PER_GROUP * g_idx
    el = jnp.where((lane >= e_lo) & (lane < e_lo + EXP_PER_GROUP), lg, NEG_INF)
    e1, l1 = top1(el)
    e2, l2 = top1(jnp.where(lane == l1, NEG_INF, el))
    z = jnp.exp(e2 - e1)
    gate1 = p_group / (1.0 + z)
    gate2 = p_group * z / (1.0 + z)
    x1id = l1 - N_GROUPS
    x2id = l2 - N_GROUPS

    sel1 = lane == x1id
    sel2 = lane == x2id
    onehot = jnp.where(sel1 | sel2, 1.0, 0.0)
    r_i = lax.broadcasted_iota(jnp.int32, (ts, ts), 0)
    c_i = lax.broadcasted_iota(jnp.int32, (ts, ts), 1)
    before = jnp.where(c_i < r_i, 1.0, 0.0).astype(BF16)
    prefix = _bdot(before, onehot.astype(BF16))
    rank1 = jnp.sum(jnp.where(sel1, prefix, 0.0), axis=-1, keepdims=True)
    rank2 = jnp.sum(jnp.where(sel2, prefix, 0.0), axis=-1, keepdims=True)
    route = jnp.zeros((ts, ROUTER_COLS), F32)
    for col, val in enumerate((x1id, x2id, gate1, gate2, rank1, rank2)):
        route = jnp.where(lane == col, val, route)
    route_ref[...] = route
    cnt_ref[...] = jnp.broadcast_to(jnp.sum(onehot, axis=0, keepdims=True), (SUBLANES, ROUTER_COLS))


def _outproj_call(att_x, att_c, y_x, y_c, x_arr, x_off, c_arr, c_off, n_lat, n_ctx, seq_len,
                  w_out, layer, gvecs, lng, lnb, scvecs, shvecs, wr, br, alpha):
    ts = OUT_TILE
    d = D_MODEL
    n_lat_tiles = n_lat // ts
    n_tiles = (n_lat + n_ctx) // ts
    tiles_per_seq = seq_len // ts
    n_seq = n_lat // seq_len
    xo, co = x_off // ts, c_off // ts

    def lat_loc(i):
        return (jnp.minimum(i, n_lat_tiles - 1), 0)

    def ctx_loc(i):
        return (jnp.maximum(i - n_lat_tiles, 0), 0)

    def lat_in(i):
        return (xo + jnp.minimum(i, n_lat_tiles - 1), 0)

    def ctx_in(i):
        return (co + jnp.maximum(i - n_lat_tiles, 0), 0)

    def vec_idx(i):
        return (jnp.where(i < n_lat_tiles, i // tiles_per_seq, n_seq), 0, 0)

    vecb = pl.BlockSpec((None, 1, d), vec_idx)
    vec0 = pl.BlockSpec((1, d), lambda i: (0, 0))
    return pl.pallas_call(
        functools.partial(_outproj_kernel, alpha=alpha, n_lat_tiles=n_lat_tiles),
        grid=(n_tiles,),
        in_specs=[
            pl.BlockSpec((ts, ATTN_WIDTH), lat_loc), pl.BlockSpec((ts, ATTN_WIDTH), ctx_loc),
            pl.BlockSpec((ts, LOCAL_WIDTH), lat_loc), pl.BlockSpec((ts, LOCAL_WIDTH), ctx_loc),
            pl.BlockSpec((ts, d), lat_in), pl.BlockSpec((ts, d), ctx_in),
            pl.BlockSpec((None, d, d), lambda i: (layer, 0, 0)),
            vecb, vec0, vec0, vecb, vecb,
            pl.BlockSpec((d, 2 * ROUTER_COLS), lambda i: (0, 0)),
            pl.BlockSpec((1, ROUTER_COLS), lambda i: (0, 0)),
        ],
        out_specs=[pl.BlockSpec((ts, d), lambda i: (i, 0)),
                   pl.BlockSpec(memory_space=pl.ANY),
                   pl.BlockSpec((ts, ROUTER_COLS), lambda i: (i, 0)),
                   pl.BlockSpec((None, SUBLANES, ROUTER_COLS), lambda i: (i, 0, 0))],
        out_shape=[jax.ShapeDtypeStruct((n_lat + n_ctx, d), F32),
                   jax.ShapeDtypeStruct((n_lat + n_ctx, ROW_TILE_ROWS, LANES), F32),
                   jax.ShapeDtypeStruct((n_lat + n_ctx, ROUTER_COLS), F32),
                   jax.ShapeDtypeStruct((n_tiles, SUBLANES, ROUTER_COLS), F32)],
        scratch_shapes=[pltpu.VMEM((d, d), BF16), pltpu.VMEM((2, ts, d), F32), pltpu.SemaphoreType.DMA((2,))],
        compiler_params=_params("arbitrary"),
        name="out_projection",
    )(att_x, att_c, y_x, y_c, x_arr, c_arr, w_out, gvecs, lng, lnb, scvecs, shvecs, wr, br)


DISPATCH_TILE = 512


def _dispatch_kernel(seg_ref, pos_ref, h_ref, xs_out, zbuf, zsem, sem):
    @pl.when(pl.program_id(0) == 0)
    def _():
        zbuf[...] = jnp.zeros_like(zbuf)

        def zero_copy(e):
            return pltpu.make_async_copy(zbuf, xs_out.at[pl.ds(seg_ref[0, e] - MOE_BLOCK, MOE_BLOCK)], zsem)

        for e in range(N_EXPERTS):
            @pl.when(seg_ref[1, e] > 0)
            def _():
                zero_copy(e).start()
        for e in range(N_EXPERTS):
            @pl.when(seg_ref[1, e] > 0)
            def _():
                zero_copy(e).wait()

        def tail_copy(b):
            return pltpu.make_async_copy(zbuf, xs_out.at[pl.ds(b * MOE_BLOCK, MOE_BLOCK)], zsem)

        n_blocks = xs_out.shape[0] // MOE_BLOCK
        first_unused = seg_ref[0, N_EXPERTS - 1] // MOE_BLOCK
        lax.fori_loop(first_unused, n_blocks, lambda b, c: (tail_copy(b).start(), c)[1], 0)
        lax.fori_loop(first_unused, n_blocks, lambda b, c: (tail_copy(b).wait(), c)[1], 0)

    def body(t, carry):
        for k in range(TOP_K):
            pltpu.make_async_copy(h_ref.at[t], xs_out.at[pos_ref[0, 0, TOP_K * t + k]], sem).start(priority=k)
        return carry

    lax.fori_loop(0, DISPATCH_TILE, body, 0, unroll=4)
    for k in range(TOP_K):
        pltpu.make_async_copy(h_ref, h_ref, sem).wait()


def _dispatch_call(seg, pos, h2t, n_slots):
    n_tok = h2t.shape[0]
    n_tiles = n_tok // DISPATCH_TILE
    row = (ROW_TILE_ROWS, LANES)
    grid_spec = pltpu.PrefetchScalarGridSpec(
        num_scalar_prefetch=1,
        grid=(n_tiles,),
        in_specs=[
            pl.BlockSpec((1, 1, TOP_K * DISPATCH_TILE), lambda i, seg: (i, 0, 0), memory_space=pltpu.SMEM),
            pl.BlockSpec((DISPATCH_TILE,) + row, lambda i, seg: (i, 0, 0)),
        ],
        out_specs=pl.BlockSpec(memory_space=pl.ANY),
        scratch_shapes=[pltpu.VMEM((MOE_BLOCK,) + row, F32), pltpu.SemaphoreType.DMA(()),
                        pltpu.SemaphoreType.DMA(())],
    )
    return pl.pallas_call(
        _dispatch_kernel,
        grid_spec=grid_spec,
        out_shape=jax.ShapeDtypeStruct((n_slots,) + row, F32),
        compiler_params=_params("arbitrary"),
        name="moe_dispatch",
    )(seg, pos, h2t)


def _moe_kernel(be_ref, nu_ref, xs_hbm, w1_ref, w3_ref, w2_ref, ys_hbm, w1b, w3b, w2b, xbuf, obuf, isem, osem):
    i = pl.program_id(0)
    n_used = nu_ref[0]
    slot = i % 2

    def in_copies(blk, at_slot):
        return _tiles_to_rows(xs_hbm.at[pl.ds(blk * MOE_BLOCK, MOE_BLOCK)], xbuf.at[at_slot], isem.at[at_slot])

    def out_copies(blk, at_slot):
        return _rows_to_tiles(obuf.at[at_slot], ys_hbm.at[pl.ds(blk * MOE_BLOCK, MOE_BLOCK)], osem.at[at_slot])

    @pl.when(i == 0)
    def _():
        _start_all(in_copies(0, 0))

    @pl.when(i + 1 < n_used)
    def _():
        _start_all(in_copies(i + 1, 1 - slot))

    @pl.when((i == 0) | (be_ref[i] != be_ref[jnp.maximum(i - 1, 0)]))
    def _():
        w1b[...] = w1_ref[...].astype(BF16)
        w3b[...] = w3_ref[...].astype(BF16)
        w2b[...] = w2_ref[...].astype(BF16)

    @pl.when(i < n_used)
    def _():
        _wait_all(in_copies(i, slot))
        x = xbuf[slot].astype(BF16)
        obuf[slot] = _bdot((jax.nn.silu(_bdot(x, w1b[...])) * _bdot(x, w3b[...])).astype(BF16), w2b[...])
        _start_all(out_copies(i, slot))

        @pl.when(i >= 1)
        def _():
            _wait_all(out_copies(i - 1, 1 - slot))

        @pl.when(i == n_used - 1)
        def _():
            _wait_all(out_copies(i, slot))

    @pl.when(i >= n_used)
    def _():
        obuf[slot] = jnp.zeros(obuf.shape[1:], F32)
        _start_all(out_copies(i, slot))
        _wait_all(out_copies(i, slot))


def _moe_call(block_e, n_used, xs, w1, w3, w2, layer):
    n_blocks = block_e.shape[0]
    d = D_MODEL
    grid_spec = pltpu.PrefetchScalarGridSpec(
        num_scalar_prefetch=2,
        grid=(n_blocks,),
        in_specs=[
            pl.BlockSpec(memory_space=pl.ANY),
            pl.BlockSpec((None, None, d, D_EXPERT), lambda i, be, nu: (layer, be[i], 0, 0)),
            pl.BlockSpec((None, None, d, D_EXPERT), lambda i, be, nu: (layer, be[i], 0, 0)),
            pl.BlockSpec((None, None, D_EXPERT, d), lambda i, be, nu: (layer, be[i], 0, 0)),
        ],
        out_specs=pl.BlockSpec(memory_space=pl.ANY),
        scratch_shapes=[pltpu.VMEM((d, D_EXPERT), BF16), pltpu.VMEM((d, D_EXPERT), BF16),
                        pltpu.VMEM((D_EXPERT, d), BF16),
                        pltpu.VMEM((2, MOE_BLOCK, d), F32), pltpu.VMEM((2, MOE_BLOCK, d), F32),
                        pltpu.SemaphoreType.DMA((2,)), pltpu.SemaphoreType.DMA((2,))],
    )
    return pl.pallas_call(
        _moe_kernel,
        grid_spec=grid_spec,
        out_shape=jax.ShapeDtypeStruct((n_blocks * MOE_BLOCK, ROW_TILE_ROWS, LANES), F32),
        compiler_params=_params("arbitrary"),
        name="moe_experts",
    )(block_e, n_used, xs, w1, w3, w2)


COMBINE_TILE = 256


def _combine_kernel(pos0_ref, posn_ref, x1_ref, route_ref, g_ref, lng_ref, lnb_ref, ys_hbm, o_ref, ybuf, sem,
                    rbuf, rsem, *, alpha):
    i = pl.program_id(0)
    nb = pl.num_programs(0)

    def row_copy(p, slot, r):
        return pltpu.make_async_copy(ys_hbm.at[p], ybuf.at[slot, r], sem.at[slot])

    def issue(pos_ref, slot):
        def body(j, carry):
            for k in range(N_DMA_PRIORITIES):
                r = N_DMA_PRIORITIES * j + k
                row_copy(pos_ref[0, 0, r], slot, r).start(priority=k)
            return carry
        lax.fori_loop(0, TOP_K * COMBINE_TILE // N_DMA_PRIORITIES, body, 0, unroll=4)

    @pl.when(i == 0)
    def _():
        issue(pos0_ref, 0)

    slot = i % 2
    pltpu.make_async_copy(ybuf.at[slot], ybuf.at[slot], sem.at[slot]).wait()
    relayout = _tiles_to_rows(ybuf.at[slot], rbuf, rsem)
    _start_all(relayout)

    @pl.when(i + 1 < nb)
    def _():
        issue(posn_ref, 1 - slot)

    _wait_all(relayout)
    route = route_ref[...]
    y = (rbuf[0:COMBINE_TILE, :] * route[:, ROUTE_GATE:ROUTE_GATE + 1]
         + rbuf[COMBINE_TILE:, :] * route[:, ROUTE_GATE + 1:ROUTE_GATE + 2])
    o_ref[...] = _layer_norm(alpha * x1_ref[...] + g_ref[...] * y) * lng_ref[...] + lnb_ref[...]


def _combine_call(pos, x1, route, gvecs, lng, lnb, ys, n_tok, tiles_per_seq, n_seq, alpha):
    d = D_MODEL
    n_tiles = n_tok // COMBINE_TILE
    n_lat_tiles = tiles_per_seq * n_seq

    def g_idx(i):
        return (jnp.where(i < n_lat_tiles, i // tiles_per_seq, n_seq), 0, 0)

    smem_blk = (1, 1, TOP_K * COMBINE_TILE)
    return pl.pallas_call(
        functools.partial(_combine_kernel, alpha=alpha),
        grid=(n_tiles,),
        in_specs=[
            pl.BlockSpec(smem_blk, lambda i: (0, 0, 0), memory_space=pltpu.SMEM),
            pl.BlockSpec(smem_blk, lambda i: (jnp.minimum(i + 1, n_tiles - 1), 0, 0), memory_space=pltpu.SMEM),
            pl.BlockSpec((COMBINE_TILE, d), lambda i: (i, 0)),
            pl.BlockSpec((COMBINE_TILE, ROUTER_COLS), lambda i: (i, 0)),
            pl.BlockSpec((None, 1, d), g_idx),
            pl.BlockSpec((1, d), lambda i: (0, 0)),
            pl.BlockSpec((1, d), lambda i: (0, 0)),
            pl.BlockSpec(memory_space=pl.ANY),
        ],
        out_specs=pl.BlockSpec((COMBINE_TILE, d), lambda i: (i, 0)),
        out_shape=jax.ShapeDtypeStruct((n_tok, d), F32),
        scratch_shapes=[pltpu.VMEM((2, TOP_K * COMBINE_TILE, ROW_TILE_ROWS, LANES), F32),
                        pltpu.SemaphoreType.DMA((2,)),
                        pltpu.VMEM((TOP_K * COMBINE_TILE, d), F32), pltpu.SemaphoreType.DMA(())],
        compiler_params=_params("arbitrary"),
        name="moe_combine",
    )(pos, pos, x1, route, gvecs, lng, lnb, ys)


def _slots(route, tile_counts):
    n = route.shape[0]
    n_tiles = tile_counts.shape[0]
    tc = tile_counts.astype(jnp.int32)
    counts = jnp.sum(tc, axis=0)
    padded = (counts + MOE_BLOCK - 1) // MOE_BLOCK * MOE_BLOCK
    pends = jnp.cumsum(padded)
    base = (pends - padded)[None, :] + jnp.cumsum(tc, axis=0) - tc
    n_blocks = -(-n * TOP_K // MOE_BLOCK) + N_EXPERTS
    block_start = jnp.arange(n_blocks, dtype=jnp.int32) * MOE_BLOCK
    block_e = jnp.minimum(jnp.sum(pends[None, 0:N_EXPERTS] <= block_start[:, None], axis=1),
                          N_EXPERTS - 1).astype(jnp.int32)
    n_used = (pends[N_EXPERTS - 1] // MOE_BLOCK).astype(jnp.int32).reshape(1)
    seg = jnp.stack([pends[0:N_EXPERTS], padded[0:N_EXPERTS]]).astype(jnp.int32)
    r3 = route.reshape(n_tiles, n // n_tiles, ROUTER_COLS)
    lane = lax.broadcasted_iota(jnp.int32, r3.shape, 2)
    pos = []
    for k in range(TOP_K):
        e = r3[:, :, ROUTE_EXPERT + k].astype(jnp.int32)
        b = jnp.sum(jnp.where(lane == e[:, :, None], base[:, None, :], 0), axis=-1)
        pos.append((b + r3[:, :, ROUTE_RANK + k].astype(jnp.int32)).reshape(n))
    return block_e, n_used, seg, jnp.stack(pos, axis=-1)


def _rope_tables(seq_len):
    m = HEAD_DIM // 4
    freqs = ROPE_BASE ** (-jnp.arange(m, dtype=F32) / m)
    t = jnp.arange(seq_len)
    row = (t // GRID_W).astype(F32)[:, None] * freqs[None, :]
    col = (t % GRID_W).astype(F32)[:, None] * freqs[None, :]
    cos = jnp.concatenate([jnp.cos(row), jnp.cos(row), jnp.cos(col), jnp.cos(col)], axis=-1)
    zero = jnp.zeros_like(row)
    sin_a = jnp.concatenate([zero, jnp.sin(row), zero, jnp.sin(col)], axis=-1)
    sin_b = jnp.concatenate([-jnp.sin(row), zero, -jnp.sin(col), zero], axis=-1)
    rep = LANES // HEAD_DIM
    return jnp.tile(cos, (1, rep)), jnp.tile(sin_a, (1, rep)), jnp.tile(sin_b, (1, rep))


def kernel(x, c, ctx, c_ctx, w_ada, b_ada, w_in, conv_w, attn_sink, gm_ws, gm_bs, w_out, ln1_g, ln1_b,
           w_rg, b_rg, w_re, b_re, w1, w3, w2, ln2_g, ln2_b):
    b_, s_, d_ = x.shape
    c_len = ctx.shape[1]
    depth = w_ada.shape[0]
    alpha = (2 * depth) ** 0.25
    n_lat = b_ * s_
    n_ctx = b_ * c_len
    ts = 512

    cin = jnp.zeros((ADA_ROWS, d_), F32).at[0:b_].set(c).at[b_].set(c_ctx)
    mod = _ada_call(cin, w_ada, b_ada)

    cos, sin_a, sin_b = _rope_tables(s_)
    ones_c = jnp.ones((c_len, LANES), F32)
    zeros_c = jnp.zeros((c_len, LANES), F32)

    x_flat = x.reshape(n_lat, d_)
    c_flat = ctx.reshape(n_ctx, d_)
    x_off, c_off = 0, 0
    x_arr, c_arr = x_flat, c_flat

    for l in range(depth):
        last = l == depth - 1
        mx = mod[l, 0:b_].reshape(b_, 6, 1, d_)
        sh1, sc1, g1, sh2, sc2, g2 = (mx[:, i] for i in range(6))
        mc = jnp.broadcast_to(mod[l, b_].reshape(1, 6, 1, d_), (b_, 6, 1, d_))
        gm_w = gm_ws[l].astype(BF16)
        gm_b = jnp.repeat(gm_bs[l].T, GM_HEAD, axis=1)
        sink = attn_sink[l]

        qx, kx, kxs, vx, vxs, yx = _inproj_call(x_arr, x_off, b_, s_, ts, sc1, sh1, w_in, l, cos, sin_a, sin_b,
                                                conv_w[l], gm_w, gm_b)
        qc, kc, kcs, vc, vcs, yc = _inproj_call(c_arr, c_off, b_, c_len, c_len, mc[:, 1], mc[:, 0], w_in, l,
                                                ones_c, zeros_c, zeros_c, conv_w[l], gm_w, gm_b)
        att_x = _win_attn_call(sink, qx, kx, kxs, vx, vxs, kc, kcs, vc, vcs, b_, s_, c_len)

        w_r = jnp.zeros((d_, ROUTER_COLS), F32).at[:, 0:N_GROUPS].set(w_rg[l]) \
            .at[:, N_GROUPS:N_GROUPS + N_EXPERTS].set(w_re[l])
        w_r_hi = w_r.astype(BF16)
        w_r_lo = (w_r - w_r_hi.astype(F32)).astype(BF16)
        wr = jnp.concatenate([w_r_hi, w_r_lo], axis=1)
        br = jnp.zeros((1, ROUTER_COLS), F32).at[0, 0:N_GROUPS].set(b_rg[l]) \
            .at[0, N_GROUPS:N_GROUPS + N_EXPERTS].set(b_re[l])
        lng1, lnb1 = ln1_g[l].reshape(1, d_), ln1_b[l].reshape(1, d_)
        lng2, lnb2 = ln2_g[l].reshape(1, d_), ln2_b[l].reshape(1, d_)

        n_tot = n_lat if last else n_lat + n_ctx
        gvecs1 = jnp.concatenate([g1, mc[0:1, 2]], axis=0)
        scvecs2 = jnp.concatenate([sc2, mc[0:1, 4]], axis=0)
        shvecs2 = jnp.concatenate([sh2, mc[0:1, 3]], axis=0)
        if last:
            x1, h2t, route, tcnt = _outproj_call(att_x, att_x, yx, yx, x_arr, x_off, x_arr, x_off, n_lat, 0, s_,
                                                 w_out, l, gvecs1, lng1, lnb1, scvecs2, shvecs2, wr, br, alpha)
        else:
            att_c = _ctx_attn_call(sink, qc, kc, kcs, vc, vcs, b_, c_len)
            x1, h2t, route, tcnt = _outproj_call(att_x, att_c, yx, yc, x_arr, x_off, c_arr, c_off, n_lat, n_ctx,
                                                 s_, w_out, l, gvecs1, lng1, lnb1, scvecs2, shvecs2, wr, br, alpha)

        block_e, n_used, seg, pos = _slots(route, tcnt[:, 0, :])
        n_slots = block_e.shape[0] * MOE_BLOCK
        xs = _dispatch_call(seg, pos.reshape(n_tot // DISPATCH_TILE, 1, TOP_K * DISPATCH_TILE), h2t, n_slots)
        ys = _moe_call(block_e, n_used, xs, w1, w3, w2, l)

        n_tiles = n_tot // COMBINE_TILE
        pos_t = pos.reshape(n_tiles, COMBINE_TILE, TOP_K).transpose(0, 2, 1).reshape(n_tiles, 1, TOP_K * COMBINE_TILE)
        gvecs = jnp.concatenate([g2, mc[0:1, 5]], axis=0)
        out = _combine_call(pos_t, x1, route, gvecs, lng2, lnb2, ys, n_tot, s_ // COMBINE_TILE, b_, alpha)
        x_arr, x_off = out, 0
        c_arr, c_off = out, n_lat

    return x_arr.reshape(b_, s_, d_)
```

```python
import functools

import jax
import jax.numpy as jnp
from jax import lax
from jax.experimental import pallas as pl
from jax.experimental.pallas import tpu as pltpu

F32 = jnp.float32
BF16 = jnp.bfloat16

D_MODEL = 1024
GRID_W = 64
HEAD_DIM = 64
N_HEADS = 8
N_KV_HEADS = 2
ATTN_WIDTH = N_HEADS * HEAD_DIM
KV_WIDTH = N_KV_HEADS * HEAD_DIM
WINDOW = 128
ATTN_SCALE = HEAD_DIM ** -0.5
ROPE_BASE = 10000.0
CONV_WIDTH = D_MODEL // 4
GM_WIDTH = D_MODEL // 4
GM_GROUPS = 4
GM_HEAD = GM_WIDTH // GM_GROUPS
CHUNK = 128
LOCAL_WIDTH = CONV_WIDTH + GM_WIDTH
IN_WIDTH = ATTN_WIDTH + 2 * KV_WIDTH + 3 * CONV_WIDTH + 2 * GM_WIDTH
QKV_WIDTH = ATTN_WIDTH + 2 * KV_WIDTH
N_GROUPS = 4
EXP_PER_GROUP = 8
N_EXPERTS = N_GROUPS * EXP_PER_GROUP
TOP_K = 2
D_EXPERT = D_MODEL // 2
MOE_BLOCK = 256
LN_EPS = 1e-6
NEG_INF = -1e30

LANES = 128
SUBLANES = 8
VMEM_LIMIT_BYTES = 48 * 1024 * 1024
N_DMA_PRIORITIES = 2

ROW_TILE_ROWS = D_MODEL // LANES
assert ROW_TILE_ROWS == SUBLANES

ROUTER_COLS = LANES
ROUTE_EXPERT = 0
ROUTE_GATE = TOP_K
ROUTE_RANK = 2 * TOP_K
ROPE_HALF_PAIR = HEAD_DIM // 4


def _bdot(a, b):
    return jnp.dot(a, b, preferred_element_type=F32)


def _split_bf16(a):
    hi = a.astype(BF16)
    lo = (a - hi.astype(F32)).astype(BF16)
    return hi, lo


def _layer_norm(r):
    mu = jnp.mean(r, axis=-1, keepdims=True)
    d = r - mu
    var = jnp.mean(d * d, axis=-1, keepdims=True)
    return d * lax.rsqrt(var + LN_EPS)


def _params(*sem):
    return pltpu.CompilerParams(dimension_semantics=sem, vmem_limit_bytes=VMEM_LIMIT_BYTES)


def _rows_to_tiles(rows_ref, tiles_ref, sem):
    return [pltpu.make_async_copy(rows_ref.at[:, pl.ds(s * LANES, LANES)], tiles_ref.at[:, s, :], sem)
            for s in range(ROW_TILE_ROWS)]


def _tiles_to_rows(tiles_ref, rows_ref, sem):
    return [pltpu.make_async_copy(tiles_ref.at[:, s, :], rows_ref.at[:, pl.ds(s * LANES, LANES)], sem)
            for s in range(ROW_TILE_ROWS)]


def _start_all(copies):
    for cp in copies:
        cp.start()


def _wait_all(copies):
    for cp in copies:
        cp.wait()


ADA_ROWS = 16
ADA_TILE = 1536


def _ada_kernel(c_ref, w_ref, b_ref, o_ref):
    act = jax.nn.silu(c_ref[...])
    ah, al = _split_bf16(act)
    wh, wl = _split_bf16(w_ref[...])
    o_ref[...] = _bdot(ah, wh) + _bdot(ah, wl) + _bdot(al, wh) + b_ref[...]


def _ada_call(cin, w_ada, b_ada):
    depth, d, n = w_ada.shape
    return pl.pallas_call(
        _ada_kernel,
        grid=(depth, n // ADA_TILE),
        in_specs=[
            pl.BlockSpec((ADA_ROWS, d), lambda l, j: (0, 0)),
            pl.BlockSpec((None, d, ADA_TILE), lambda l, j: (l, 0, j)),
            pl.BlockSpec((None, 1, ADA_TILE), lambda l, j: (l, 0, j)),
        ],
        out_specs=pl.BlockSpec((None, ADA_ROWS, ADA_TILE), lambda l, j: (l, 0, j)),
        out_shape=jax.ShapeDtypeStruct((depth, ADA_ROWS, n), F32),
        compiler_params=_params("parallel", "parallel"),
        name="ada_modulation",
    )(cin, w_ada, b_ada.reshape(depth, 1, n))


def _inproj_kernel(x_ref, xp_ref, xn_ref, sc_ref, sh_ref, wf_ref, cos_ref, sa_ref, sb_ref,
                   cw_ref, gw_ref, gb_ref,
                   q_ref, k_ref, ksw_ref, v_ref, vsw_ref, y_ref, w_ref, *, ts):
    t = pl.program_id(1)
    nt = pl.num_programs(1)

    @pl.when((pl.program_id(0) == 0) & (t == 0))
    def _():
        w_ref[...] = wf_ref[...].astype(BF16)

    sc = 1.0 + sc_ref[...]
    sh = sh_ref[...]
    hx = (x_ref[...] * sc + sh).astype(BF16)

    pq = _bdot(hx, w_ref[:, 0:QKV_WIDTH])
    cos = cos_ref[...]
    sa = sa_ref[...]
    sb = sb_ref[...]

    def rope(z):
        return (z * cos + pltpu.roll(z, ROPE_HALF_PAIR, 1) * sa
                + pltpu.roll(z, LANES - ROPE_HALF_PAIR, 1) * sb)

    for j in range(ATTN_WIDTH // LANES):
        sl = slice(j * LANES, (j + 1) * LANES)
        q_ref[:, sl] = (rope(pq[:, sl]) * ATTN_SCALE).astype(BF16)
    kr = rope(pq[:, ATTN_WIDTH:ATTN_WIDTH + KV_WIDTH])
    k_ref[...] = kr.astype(BF16)
    ksw_ref[...] = pltpu.roll(kr, HEAD_DIM, 1).astype(BF16)
    vv = pq[:, ATTN_WIDTH + KV_WIDTH:QKV_WIDTH]
    v_ref[...] = vv.astype(BF16)
    vsw_ref[...] = pltpu.roll(vv, HEAD_DIM, 1).astype(BF16)

    pm = _bdot(hx, w_ref[:, QKV_WIDTH:IN_WIDTH])
    cb = pm[:, 0:CONV_WIDTH]
    u = pm[:, CONV_WIDTH:2 * CONV_WIDTH] * pm[:, 2 * CONV_WIDTH:3 * CONV_WIDTH]

    halo = jnp.concatenate([xp_ref[...], xn_ref[...]], axis=0)
    hh = (halo * sc + sh).astype(BF16)
    ph = _bdot(hh, w_ref[:, QKV_WIDTH + CONV_WIDTH:QKV_WIDTH + 3 * CONV_WIDTH])
    uh = ph[:, 0:CONV_WIDTH] * ph[:, CONV_WIDTH:2 * CONV_WIDTH]
    up_row = jnp.where(t > 0, uh[SUBLANES - 1:SUBLANES, :], 0.0)
    dn_row = jnp.where(t < nt - 1, uh[SUBLANES:SUBLANES + 1, :], 0.0)
    ridx = lax.broadcasted_iota(jnp.int32, (ts, CONV_WIDTH), 0)
    u_up = jnp.where(ridx == 0, up_row, pltpu.roll(u, 1, 0))
    u_dn = jnp.where(ridx == ts - 1, dn_row, pltpu.roll(u, ts - 1, 0))
    cw = cw_ref[...]
    y_conv = cb * (u_up * cw[0:1, :] + u * cw[1:2, :] + u_dn * cw[2:3, :])
    y_ref[:, 0:CONV_WIDTH] = y_conv.astype(BF16)

    gu = jax.nn.gelu(pm[:, 3 * CONV_WIDTH:3 * CONV_WIDTH + GM_WIDTH])
    gv = _layer_norm(jax.nn.gelu(pm[:, 3 * CONV_WIDTH + GM_WIDTH:3 * CONV_WIDTH + 2 * GM_WIDTH])).astype(BF16)
    lane = lax.broadcasted_iota(jnp.int32, (CHUNK, LANES), 1)
    zero = jnp.zeros((CHUNK, LANES), BF16)
    for c in range(ts // CHUNK):
        rows = slice(c * CHUNK, (c + 1) * CHUNK)
        for j in range(GM_WIDTH // LANES):
            cols = slice(j * LANES, (j + 1) * LANES)
            vp = gv[rows, cols]
            s = (_bdot(gw_ref[2 * j], jnp.where(lane < GM_HEAD, vp, zero))
                 + _bdot(gw_ref[2 * j + 1], jnp.where(lane >= GM_HEAD, vp, zero))
                 + gb_ref[:, cols])
            y_ref[rows, CONV_WIDTH + j * LANES:CONV_WIDTH + (j + 1) * LANES] = (gu[rows, cols] * s).astype(BF16)


def _inproj_call(x2d, row_off, n_seq, seq_len, ts, sc, sh, w_in, layer, cos, sa, sb, conv_w, gm_w, gm_b):
    nt = seq_len // ts
    off_t = row_off // ts
    off_8 = row_off // SUBLANES
    last_8 = x2d.shape[0] // SUBLANES - 1
    per_tile_8 = ts // SUBLANES
    n_out = n_seq * seq_len
    d = D_MODEL

    def tile_idx(b, t):
        return (off_t + b * nt + t, 0)

    def prev_idx(b, t):
        return (jnp.maximum(off_8 + (b * nt + t) * per_tile_8 - 1, 0), 0)

    def next_idx(b, t):
        return (jnp.minimum(off_8 + (b * nt + t + 1) * per_tile_8, last_8), 0)

    def out_idx(b, t):
        return (b * nt + t, 0)

    vec = pl.BlockSpec((None, 1, d), lambda b, t: (b, 0, 0))
    rope_spec = pl.BlockSpec((ts, LANES), lambda b, t: (t, 0))
    kv_spec = pl.BlockSpec((ts, KV_WIDTH), out_idx)
    kv_shape = jax.ShapeDtypeStruct((n_out, KV_WIDTH), BF16)
    return pl.pallas_call(
        functools.partial(_inproj_kernel, ts=ts),
        grid=(n_seq, nt),
        in_specs=[
            pl.BlockSpec((ts, d), tile_idx),
            pl.BlockSpec((SUBLANES, d), prev_idx),
            pl.BlockSpec((SUBLANES, d), next_idx),
            vec, vec,
            pl.BlockSpec((None, d, IN_WIDTH), lambda b, t: (layer, 0, 0)),
            rope_spec, rope_spec, rope_spec,
            pl.BlockSpec((3, CONV_WIDTH), lambda b, t: (0, 0)),
            pl.BlockSpec((GM_GROUPS, CHUNK, CHUNK), lambda b, t: (0, 0, 0)),
            pl.BlockSpec((CHUNK, GM_WIDTH), lambda b, t: (0, 0)),
        ],
        out_specs=[
            pl.BlockSpec((ts, ATTN_WIDTH), out_idx),
            kv_spec, kv_spec, kv_spec, kv_spec,
            pl.BlockSpec((ts, LOCAL_WIDTH), out_idx),
        ],
        out_shape=[
            jax.ShapeDtypeStruct((n_out, ATTN_WIDTH), BF16),
            kv_shape, kv_shape, kv_shape, kv_shape,
            jax.ShapeDtypeStruct((n_out, LOCAL_WIDTH), BF16),
        ],
        scratch_shapes=[pltpu.VMEM((d, IN_WIDTH), BF16)],
        compiler_params=_params("arbitrary", "arbitrary"),
        name="in_projection",
    )(x2d, x2d, x2d, sc, sh, w_in, cos, sa, sb, conv_w, gm_w, gm_b)


def _attn_body(q, keys, keys_sw, vals, vals_sw, bias, sink_ref, o_ref, tq):
    nk = keys.shape[0]
    lane = lax.broadcasted_iota(jnp.int32, (nk, LANES), 1)
    lo = lane < HEAD_DIM
    zero = jnp.zeros((nk, LANES), BF16)
    rid = lax.broadcasted_iota(jnp.int32, (2 * tq, 1), 0)
    nt_dims = (((1,), (1,)), ((), ()))
    gqa = N_HEADS // N_KV_HEADS
    for h in range(N_KV_HEADS):
        k_own, k_oth = (keys, keys_sw) if h == 0 else (keys_sw, keys)
        v_own, v_oth = (vals, vals_sw) if h == 0 else (vals_sw, vals)
        kz = (jnp.where(lo, k_own, zero), jnp.where(lo, zero, k_oth))
        vz = (jnp.where(lo, v_own, zero), jnp.where(lo, zero, v_oth))
        c0 = h * gqa * HEAD_DIM
        qs = jnp.concatenate([q[:, c0:c0 + LANES], q[:, c0 + LANES:c0 + 2 * LANES]], axis=0)
        acc = None
        for par in range(2):
            s = lax.dot_general(qs, kz[par], nt_dims, preferred_element_type=F32)
            if bias is not None:
                s = s + bias
            sink = jnp.where(rid < tq, sink_ref[h * gqa + par], sink_ref[h * gqa + 2 + par])
            m = jnp.maximum(jnp.max(s, axis=-1, keepdims=True), sink)
            p = jnp.exp(s - m)
            den = jnp.sum(p, axis=-1, keepdims=True) + jnp.exp(sink - m)
            o = _bdot(p.astype(BF16), vz[par]) / den
            acc = o if acc is None else acc + o
        o_ref[:, c0:c0 + LANES] = acc[0:tq].astype(BF16)
        o_ref[:, c0 + LANES:c0 + 2 * LANES] = acc[tq:2 * tq].astype(BF16)


def _win_attn_kernel(sink_ref, q_ref, kp_ref, kc_ref, kn_ref, kx_ref,
                     ksp_ref, ksc_ref, ksn_ref, ksx_ref,
                     vp_ref, vc_ref, vn_ref, vx_ref,
                     vsp_ref, vsc_ref, vsn_ref, vsx_ref, o_ref, *, tq, ctx_len):
    n = pl.program_id(1)
    nb = pl.num_programs(1)
    row = lax.broadcasted_iota(jnp.int32, (tq, tq), 0)
    col = lax.broadcasted_iota(jnp.int32, (tq, tq), 1)
    b_prev = jnp.where((col >= row) & (n > 0), 0.0, NEG_INF)
    b_next = jnp.where((col <= row) & (n < nb - 1), 0.0, NEG_INF)
    b_half = jnp.concatenate([b_prev, jnp.zeros((tq, tq), F32), b_next, jnp.zeros((tq, ctx_len), F32)], axis=1)
    bias = jnp.concatenate([b_half, b_half], axis=0)

    def cat(a, b, c, d):
        return jnp.concatenate([a[...], b[...], c[...], d[...]], axis=0)

    _attn_body(q_ref[...], cat(kp_ref, kc_ref, kn_ref, kx_ref), cat(ksp_ref, ksc_ref, ksn_ref, ksx_ref),
               cat(vp_ref, vc_ref, vn_ref, vx_ref), cat(vsp_ref, vsc_ref, vsn_ref, vsx_ref),
               bias, sink_ref, o_ref, tq)


def _win_attn_call(sink, q, k, ksw, v, vsw, kx, kxsw, vx, vxsw, n_seq, seq_len, ctx_len):
    tq = WINDOW
    nb = seq_len // tq

    def cur(b, n):
        return (b * nb + n, 0)

    def prev(b, n):
        return (b * nb + jnp.maximum(n - 1, 0), 0)

    def nxt(b, n):
        return (b * nb + jnp.minimum(n + 1, nb - 1), 0)

    def ctx(b, n):
        return (b, 0)

    def kv_specs():
        return [pl.BlockSpec((tq, KV_WIDTH), prev), pl.BlockSpec((tq, KV_WIDTH), cur),
                pl.BlockSpec((tq, KV_WIDTH), nxt), pl.BlockSpec((ctx_len, KV_WIDTH), ctx)]

    return pl.pallas_call(
        functools.partial(_win_attn_kernel, tq=tq, ctx_len=ctx_len),
        grid=(n_seq, nb),
        in_specs=[pl.BlockSpec(memory_space=pltpu.SMEM), pl.BlockSpec((tq, ATTN_WIDTH), cur)]
        + kv_specs() + kv_specs() + kv_specs() + kv_specs(),
        out_specs=pl.BlockSpec((tq, ATTN_WIDTH), cur),
        out_shape=jax.ShapeDtypeStruct((n_seq * seq_len, ATTN_WIDTH), BF16),
        compiler_params=_params("parallel", "parallel"),
        name="window_attention",
    )(sink, q, k, k, k, kx, ksw, ksw, ksw, kxsw, v, v, v, vx, vsw, vsw, vsw, vxsw)


def _ctx_attn_kernel(sink_ref, q_ref, k_ref, ks_ref, v_ref, vs_ref, o_ref, *, tq):
    _attn_body(q_ref[...], k_ref[...], ks_ref[...], v_ref[...], vs_ref[...], None, sink_ref, o_ref, tq)


def _ctx_attn_call(sink, q, k, ksw, v, vsw, n_seq, ctx_len):
    def blk(w):
        return pl.BlockSpec((ctx_len, w), lambda b: (b, 0))

    return pl.pallas_call(
        functools.partial(_ctx_attn_kernel, tq=ctx_len),
        grid=(n_seq,),
        in_specs=[pl.BlockSpec(memory_space=pltpu.SMEM), blk(ATTN_WIDTH),
                  blk(KV_WIDTH), blk(KV_WIDTH), blk(KV_WIDTH), blk(KV_WIDTH)],
        out_specs=blk(ATTN_WIDTH),
        out_shape=jax.ShapeDtypeStruct((n_seq * ctx_len, ATTN_WIDTH), BF16),
        compiler_params=_params("parallel"),
        name="context_attention",
    )(sink, q, k, ksw, v, vsw)


OUT_TILE = 256


def _outproj_kernel(attx_ref, attc_ref, yx_ref, yc_ref, xx_ref, xc_ref, wof_ref, g_ref, lng_ref, lnb_ref,
                    sc_ref, sh_ref, wr_ref, br_ref, x1_ref, h2v_ref, route_ref, cnt_ref, wo_ref,
                    *, alpha, n_lat_tiles):
    @pl.when(pl.program_id(0) == 0)
    def _():
        wo_ref[...] = wof_ref[...].astype(BF16)

    is_lat = pl.program_id(0) < n_lat_tiles
    att = jnp.where(is_lat, attx_ref[...], attc_ref[...])
    yloc = jnp.where(is_lat, yx_ref[...], yc_ref[...])
    xin = jnp.where(is_lat, xx_ref[...], xc_ref[...])
    o = _bdot(att, wo_ref[0:ATTN_WIDTH, :]) + _bdot(yloc, wo_ref[ATTN_WIDTH:, :])
    x1 = _layer_norm(alpha * xin + g_ref[...] * o) * lng_ref[...] + lnb_ref[...]
    x1_ref[...] = x1
    h2 = x1 * (1.0 + sc_ref[...]) + sh_ref[...]
    for c in range(ROW_TILE_ROWS):
        h2v_ref[:, c, :, :] = h2[:, c * LANES:(c + 1) * LANES].reshape(h2.shape[0] // SUBLANES, SUBLANES, LANES)

    hh, hl = _split_bf16(h2)
    t1 = _bdot(hh, wr_ref[...])
    lg = t1[:, 0:ROUTER_COLS] + t1[:, ROUTER_COLS:] + _bdot(hl, wr_ref[:, 0:ROUTER_COLS]) + br_ref[...]

    ts = lg.shape[0]
    lane = lax.broadcasted_iota(jnp.int32, (ts, ROUTER_COLS), 1).astype(F32)
    big = jnp.float32(ROUTER_COLS)

    def top1(v):
        m = jnp.max(v, axis=-1, keepdims=True)
        return m, jnp.min(jnp.where(v == m, lane, big), axis=-1, keepdims=True)

    gl = jnp.where(lane < N_GROUPS, lg, NEG_INF)
    g_val, g_idx = top1(gl)
    lse = g_val + jnp.log(jnp.sum(jnp.exp(gl - g_val), axis=-1, keepdims=True))
    p_group = jnp.exp(g_val - lse)
    e_lo = N_GROUPS + EXP_PER_GROUP * g_idx
    el = jnp.where((lane >= e_lo) & (lane < e_lo + EXP_PER_GROUP), lg, NEG_INF)
    e1, l1 = top1(el)
    e2, l2 = top1(jnp.where(lane == l1, NEG_INF, el))
    z = jnp.exp(e2 - e1)
    gate1 = p_group / (1.0 + z)
    gate2 = p_group * z / (1.0 + z)
    x1id = l1 - N_GROUPS
    x2id = l2 - N_GROUPS

    sel1 = lane == x1id
    sel2 = lane == x2id
    onehot = jnp.where(sel1 | sel2, 1.0, 0.0)
    r_i = lax.broadcasted_iota(jnp.int32, (ts, ts), 0)
    c_i = lax.broadcasted_iota(jnp.int32, (ts, ts), 1)
    before = jnp.where(c_i < r_i, 1.0, 0.0).astype(BF16)
    prefix = _bdot(before, onehot.astype(BF16))
    rank1 = jnp.sum(jnp.where(sel1, prefix, 0.0), axis=-1, keepdims=True)
    rank2 = jnp.sum(jnp.where(sel2, prefix, 0.0), axis=-1, keepdims=True)
    route = jnp.zeros((ts, ROUTER_COLS), F32)
    for col, val in enumerate((x1id, x2id, gate1, gate2, rank1, rank2)):
        route = jnp.where(lane == col, val, route)
    route_ref[...] = route
    cnt_ref[...] = jnp.broadcast_to(jnp.sum(onehot, axis=0, keepdims=True), (SUBLANES, ROUTER_COLS))


def _outproj_call(att_x, att_c, y_x, y_c, x_arr, x_off, c_arr, c_off, n_lat, n_ctx, seq_len,
                  w_out, layer, gvecs, lng, lnb, scvecs, shvecs, wr, br, alpha):
    ts = OUT_TILE
    d = D_MODEL
    n_lat_tiles = n_lat // ts
    n_tiles = (n_lat + n_ctx) // ts
    tiles_per_seq = seq_len // ts
    n_seq = n_lat // seq_len
    xo, co = x_off // ts, c_off // ts

    def lat_loc(i):
        return (jnp.minimum(i, n_lat_tiles - 1), 0)

    def ctx_loc(i):
        return (jnp.maximum(i - n_lat_tiles, 0), 0)

    def lat_in(i):
        return (xo + jnp.minimum(i, n_lat_tiles - 1), 0)

    def ctx_in(i):
        return (co + jnp.maximum(i - n_lat_tiles, 0), 0)

    def vec_idx(i):
        return (jnp.where(i < n_lat_tiles, i // tiles_per_seq, n_seq), 0, 0)

    vecb = pl.BlockSpec((None, 1, d), vec_idx)
    vec0 = pl.BlockSpec((1, d), lambda i: (0, 0))
    return pl.pallas_call(
        functools.partial(_outproj_kernel, alpha=alpha, n_lat_tiles=n_lat_tiles),
        grid=(n_tiles,),
        in_specs=[
            pl.BlockSpec((ts, ATTN_WIDTH), lat_loc), pl.BlockSpec((ts, ATTN_WIDTH), ctx_loc),
            pl.BlockSpec((ts, LOCAL_WIDTH), lat_loc), pl.BlockSpec((ts, LOCAL_WIDTH), ctx_loc),
            pl.BlockSpec((ts, d), lat_in), pl.BlockSpec((ts, d), ctx_in),
            pl.BlockSpec((None, d, d), lambda i: (layer, 0, 0)),
            vecb, vec0, vec0, vecb, vecb,
            pl.BlockSpec((d, 2 * ROUTER_COLS), lambda i: (0, 0)),
            pl.BlockSpec((1, ROUTER_COLS), lambda i: (0, 0)),
        ],
        out_specs=[pl.BlockSpec((ts, d), lambda i: (i, 0)),
                   pl.BlockSpec((ts // SUBLANES, ROW_TILE_ROWS, SUBLANES, LANES), lambda i: (i, 0, 0, 0)),
                   pl.BlockSpec((ts, ROUTER_COLS), lambda i: (i, 0)),
                   pl.BlockSpec((None, SUBLANES, ROUTER_COLS), lambda i: (i, 0, 0))],
        out_shape=[jax.ShapeDtypeStruct((n_lat + n_ctx, d), F32),
                   jax.ShapeDtypeStruct(((n_lat + n_ctx) // SUBLANES, ROW_TILE_ROWS, SUBLANES, LANES), F32),
                   jax.ShapeDtypeStruct((n_lat + n_ctx, ROUTER_COLS), F32),
                   jax.ShapeDtypeStruct((n_tiles, SUBLANES, ROUTER_COLS), F32)],
        scratch_shapes=[pltpu.VMEM((d, d), BF16)],
        compiler_params=_params("arbitrary"),
        name="out_projection",
    )(att_x, att_c, y_x, y_c, x_arr, c_arr, w_out, gvecs, lng, lnb, scvecs, shvecs, wr, br)


DISPATCH_TILE = 512


def _dispatch_kernel(seg_ref, pos_ref, h_ref, xs_out, zbuf, zsem, sem):
    @pl.when(pl.program_id(0) == 0)
    def _():
        zbuf[...] = jnp.zeros_like(zbuf)

        def zero_copy(e):
            return pltpu.make_async_copy(zbuf, xs_out.at[pl.ds(seg_ref[0, e] - MOE_BLOCK, MOE_BLOCK)], zsem)

        for e in range(N_EXPERTS):
            @pl.when(seg_ref[1, e] > 0)
            def _():
                zero_copy(e).start()
        for e in range(N_EXPERTS):
            @pl.when(seg_ref[1, e] > 0)
            def _():
                zero_copy(e).wait()

        def tail_copy(b):
            return pltpu.make_async_copy(zbuf, xs_out.at[pl.ds(b * MOE_BLOCK, MOE_BLOCK)], zsem)

        n_blocks = xs_out.shape[0] // MOE_BLOCK
        first_unused = seg_ref[0, N_EXPERTS - 1] // MOE_BLOCK
        lax.fori_loop(first_unused, n_blocks, lambda b, c: (tail_copy(b).start(), c)[1], 0)
        lax.fori_loop(first_unused, n_blocks, lambda b, c: (tail_copy(b).wait(), c)[1], 0)

    def body(j, carry):
        for s in range(SUBLANES):
            t = SUBLANES * j + s
            for k in range(TOP_K):
                pltpu.make_async_copy(h_ref.at[j, :, s, :], xs_out.at[pos_ref[0, 0, TOP_K * t + k]],
                                      sem).start(priority=k)
        return carry

    lax.fori_loop(0, DISPATCH_TILE // SUBLANES, body, 0)
    for k in range(TOP_K):
        pltpu.make_async_copy(h_ref, h_ref, sem).wait()


def _dispatch_call(seg, pos, h2t, n_slots):
    n_tok = h2t.shape[0] * SUBLANES
    n_tiles = n_tok // DISPATCH_TILE
    row = (ROW_TILE_ROWS, LANES)
    grid_spec = pltpu.PrefetchScalarGridSpec(
        num_scalar_prefetch=1,
        grid=(n_tiles,),
        in_specs=[
            pl.BlockSpec((1, 1, TOP_K * DISPATCH_TILE), lambda i, seg: (i, 0, 0), memory_space=pltpu.SMEM),
            pl.BlockSpec((DISPATCH_TILE // SUBLANES, ROW_TILE_ROWS, SUBLANES, LANES), lambda i, seg: (i, 0, 0, 0)),
        ],
        out_specs=pl.BlockSpec(memory_space=pl.ANY),
        scratch_shapes=[pltpu.VMEM((MOE_BLOCK,) + row, F32), pltpu.SemaphoreType.DMA(()),
                        pltpu.SemaphoreType.DMA(())],
    )
    return pl.pallas_call(
        _dispatch_kernel,
        grid_spec=grid_spec,
        out_shape=jax.ShapeDtypeStruct((n_slots,) + row, F32),
        compiler_params=_params("arbitrary"),
        name="moe_dispatch",
    )(seg, pos, h2t)


def _moe_kernel(be_ref, nu_ref, xs_hbm, w1_ref, w3_ref, w2_ref, ys_hbm, w1b, w3b, w2b, xbuf, obuf, isem, osem):
    i = pl.program_id(0)
    n_used = nu_ref[0]
    slot = i % 2

    def in_copies(blk, at_slot):
        return _tiles_to_rows(xs_hbm.at[pl.ds(blk * MOE_BLOCK, MOE_BLOCK)], xbuf.at[at_slot], isem.at[at_slot])

    def out_copies(blk, at_slot):
        return _rows_to_tiles(obuf.at[at_slot], ys_hbm.at[pl.ds(blk * MOE_BLOCK, MOE_BLOCK)], osem.at[at_slot])

    @pl.when(i == 0)
    def _():
        _start_all(in_copies(0, 0))

    @pl.when(i + 1 < n_used)
    def _():
        _start_all(in_copies(i + 1, 1 - slot))

    @pl.when((i == 0) | (be_ref[i] != be_ref[jnp.maximum(i - 1, 0)]))
    def _():
        w1b[...] = w1_ref[...].astype(BF16)
        w3b[...] = w3_ref[...].astype(BF16)
        w2b[...] = w2_ref[...].astype(BF16)

    @pl.when(i < n_used)
    def _():
        _wait_all(in_copies(i, slot))
        x = xbuf[slot].astype(BF16)
        obuf[slot] = _bdot((jax.nn.silu(_bdot(x, w1b[...])) * _bdot(x, w3b[...])).astype(BF16), w2b[...])
        _start_all(out_copies(i, slot))

        @pl.when(i >= 1)
        def _():
            _wait_all(out_copies(i - 1, 1 - slot))

        @pl.when(i == n_used - 1)
        def _():
            _wait_all(out_copies(i, slot))

    @pl.when(i >= n_used)
    def _():
        obuf[slot] = jnp.zeros(obuf.shape[1:], F32)
        _start_all(out_copies(i, slot))
        _wait_all(out_copies(i, slot))


def _moe_call(block_e, n_used, xs, w1, w3, w2, layer):
    n_blocks = block_e.shape[0]
    d = D_MODEL
    grid_spec = pltpu.PrefetchScalarGridSpec(
        num_scalar_prefetch=2,
        grid=(n_blocks,),
        in_specs=[
            pl.BlockSpec(memory_space=pl.ANY),
            pl.BlockSpec((None, None, d, D_EXPERT), lambda i, be, nu: (layer, be[i], 0, 0)),
            pl.BlockSpec((None, None, d, D_EXPERT), lambda i, be, nu: (layer, be[i], 0, 0)),
            pl.BlockSpec((None, None, D_EXPERT, d), lambda i, be, nu: (layer, be[i], 0, 0)),
        ],
        out_specs=pl.BlockSpec(memory_space=pl.ANY),
        scratch_shapes=[pltpu.VMEM((d, D_EXPERT), BF16), pltpu.VMEM((d, D_EXPERT), BF16),
                        pltpu.VMEM((D_EXPERT, d), BF16),
                        pltpu.VMEM((2, MOE_BLOCK, d), F32), pltpu.VMEM((2, MOE_BLOCK, d), F32),
                        pltpu.SemaphoreType.DMA((2,)), pltpu.SemaphoreType.DMA((2,))],
    )
    return pl.pallas_call(
        _moe_kernel,
        grid_spec=grid_spec,
        out_shape=jax.ShapeDtypeStruct((n_blocks * MOE_BLOCK, ROW_TILE_ROWS, LANES), F32),
        compiler_params=_params("arbitrary"),
        name="moe_experts",
    )(block_e, n_used, xs, w1, w3, w2)


COMBINE_TILE = 256


def _combine_kernel(pos0_ref, posn_ref, x1_ref, route_ref, g_ref, lng_ref, lnb_ref, ys_hbm, o_ref, ybuf, sem,
                    *, alpha):
    i = pl.program_id(0)
    nb = pl.num_programs(0)
    n_rows = TOP_K * COMBINE_TILE

    def issue(pos_ref, slot):
        def body(j, carry):
            for s in range(SUBLANES):
                pltpu.make_async_copy(ys_hbm.at[pos_ref[0, 0, SUBLANES * j + s]], ybuf.at[slot, j, :, s, :],
                                      sem.at[slot]).start(priority=s % N_DMA_PRIORITIES)
            return carry
        lax.fori_loop(0, n_rows // SUBLANES, body, 0)

    @pl.when(i == 0)
    def _():
        issue(pos0_ref, 0)

    @pl.when(i + 1 < nb)
    def _():
        issue(posn_ref, (i + 1) % 2)

    slot = i % 2
    pltpu.make_async_copy(ybuf.at[slot], ybuf.at[slot], sem.at[slot]).wait()
    yrows = jnp.concatenate([ybuf[slot, :, c, :, :].reshape(n_rows, LANES) for c in range(ROW_TILE_ROWS)], axis=1)
    route = route_ref[...]
    y = (yrows[0:COMBINE_TILE, :] * route[:, ROUTE_GATE:ROUTE_GATE + 1]
         + yrows[COMBINE_TILE:, :] * route[:, ROUTE_GATE + 1:ROUTE_GATE + 2])
    o_ref[...] = _layer_norm(alpha * x1_ref[...] + g_ref[...] * y) * lng_ref[...] + lnb_ref[...]


def _combine_call(pos, x1, route, gvecs, lng, lnb, ys, n_tok, tiles_per_seq, n_seq, alpha):
    d = D_MODEL
    n_tiles = n_tok // COMBINE_TILE
    n_lat_tiles = tiles_per_seq * n_seq

    def g_idx(i):
        return (jnp.where(i < n_lat_tiles, i // tiles_per_seq, n_seq), 0, 0)

    smem_blk = (1, 1, TOP_K * COMBINE_TILE)
    return pl.pallas_call(
        functools.partial(_combine_kernel, alpha=alpha),
        grid=(n_tiles,),
        in_specs=[
            pl.BlockSpec(smem_blk, lambda i: (0, 0, 0), memory_space=pltpu.SMEM),
            pl.BlockSpec(smem_blk, lambda i: (jnp.minimum(i + 1, n_tiles - 1), 0, 0), memory_space=pltpu.SMEM),
            pl.BlockSpec((COMBINE_TILE, d), lambda i: (i, 0)),
            pl.BlockSpec((COMBINE_TILE, ROUTER_COLS), lambda i: (i, 0)),
            pl.BlockSpec((None, 1, d), g_idx),
            pl.BlockSpec((1, d), lambda i: (0, 0)),
            pl.BlockSpec((1, d), lambda i: (0, 0)),
            pl.BlockSpec(memory_space=pl.ANY),
        ],
        out_specs=pl.BlockSpec((COMBINE_TILE, d), lambda i: (i, 0)),
        out_shape=jax.ShapeDtypeStruct((n_tok, d), F32),
        scratch_shapes=[pltpu.VMEM((2, TOP_K * COMBINE_TILE // SUBLANES, ROW_TILE_ROWS, SUBLANES, LANES), F32),
                        pltpu.SemaphoreType.DMA((2,))],
        compiler_params=_params("arbitrary"),
        name="moe_combine",
    )(pos, pos, x1, route, gvecs, lng, lnb, ys)


def _slots(route, tile_counts):
    n = route.shape[0]
    n_tiles = tile_counts.shape[0]
    tc = tile_counts.astype(jnp.int32)
    counts = jnp.sum(tc, axis=0)
    padded = (counts + MOE_BLOCK - 1) // MOE_BLOCK * MOE_BLOCK
    pends = jnp.cumsum(padded)
    base = (pends - padded)[None, :] + jnp.cumsum(tc, axis=0) - tc
    n_blocks = -(-n * TOP_K // MOE_BLOCK) + N_EXPERTS
    block_start = jnp.arange(n_blocks, dtype=jnp.int32) * MOE_BLOCK
    block_e = jnp.minimum(jnp.sum(pends[None, 0:N_EXPERTS] <= block_start[:, None], axis=1),
                          N_EXPERTS - 1).astype(jnp.int32)
    n_used = (pends[N_EXPERTS - 1] // MOE_BLOCK).astype(jnp.int32).reshape(1)
    seg = jnp.stack([pends[0:N_EXPERTS], padded[0:N_EXPERTS]]).astype(jnp.int32)
    r3 = route.reshape(n_tiles, n // n_tiles, ROUTER_COLS)
    lane = lax.broadcasted_iota(jnp.int32, r3.shape, 2)
    pos = []
    for k in range(TOP_K):
        e = r3[:, :, ROUTE_EXPERT + k].astype(jnp.int32)
        b = jnp.sum(jnp.where(lane == e[:, :, None], base[:, None, :], 0), axis=-1)
        pos.append((b + r3[:, :, ROUTE_RANK + k].astype(jnp.int32)).reshape(n))
    return block_e, n_used, seg, jnp.stack(pos, axis=-1)


def _rope_tables(seq_len):
    m = HEAD_DIM // 4
    freqs = ROPE_BASE ** (-jnp.arange(m, dtype=F32) / m)
    t = jnp.arange(seq_len)
    row = (t // GRID_W).astype(F32)[:, None] * freqs[None, :]
    col = (t % GRID_W).astype(F32)[:, None] * freqs[None, :]
    cos = jnp.concatenate([jnp.cos(row), jnp.cos(row), jnp.cos(col), jnp.cos(col)], axis=-1)
    zero = jnp.zeros_like(row)
    sin_a = jnp.concatenate([zero, jnp.sin(row), zero, jnp.sin(col)], axis=-1)
    sin_b = jnp.concatenate([-jnp.sin(row), zero, -jnp.sin(col), zero], axis=-1)
    rep = LANES // HEAD_DIM
    return jnp.tile(cos, (1, rep)), jnp.tile(sin_a, (1, rep)), jnp.tile(sin_b, (1, rep))


def kernel(x, c, ctx, c_ctx, w_ada, b_ada, w_in, conv_w, attn_sink, gm_ws, gm_bs, w_out, ln1_g, ln1_b,
           w_rg, b_rg, w_re, b_re, w1, w3, w2, ln2_g, ln2_b):
    b_, s_, d_ = x.shape
    c_len = ctx.shape[1]
    depth = w_ada.shape[0]
    alpha = (2 * depth) ** 0.25
    n_lat = b_ * s_
    n_ctx = b_ * c_len
    ts = 512

    cin = jnp.zeros((ADA_ROWS, d_), F32).at[0:b_].set(c).at[b_].set(c_ctx)
    mod = _ada_call(cin, w_ada, b_ada)

    cos, sin_a, sin_b = _rope_tables(s_)
    ones_c = jnp.ones((c_len, LANES), F32)
    zeros_c = jnp.zeros((c_len, LANES), F32)

    x_flat = x.reshape(n_lat, d_)
    c_flat = ctx.reshape(n_ctx, d_)
    x_off, c_off = 0, 0
    x_arr, c_arr = x_flat, c_flat

    for l in range(depth):
        last = l == depth - 1
        mx = mod[l, 0:b_].reshape(b_, 6, 1, d_)
        sh1, sc1, g1, sh2, sc2, g2 = (mx[:, i] for i in range(6))
        mc = jnp.broadcast_to(mod[l, b_].reshape(1, 6, 1, d_), (b_, 6, 1, d_))
        gm_w = gm_ws[l].astype(BF16)
        gm_b = jnp.repeat(gm_bs[l].T, GM_HEAD, axis=1)
        sink = attn_sink[l]

        qx, kx, kxs, vx, vxs, yx = _inproj_call(x_arr, x_off, b_, s_, ts, sc1, sh1, w_in, l, cos, sin_a, sin_b,
                                                conv_w[l], gm_w, gm_b)
        qc, kc, kcs, vc, vcs, yc = _inproj_call(c_arr, c_off, b_, c_len, c_len, mc[:, 1], mc[:, 0], w_in, l,
                                                ones_c, zeros_c, zeros_c, conv_w[l], gm_w, gm_b)
        att_x = _win_attn_call(sink, qx, kx, kxs, vx, vxs, kc, kcs, vc, vcs, b_, s_, c_len)

        w_r = jnp.zeros((d_, ROUTER_COLS), F32).at[:, 0:N_GROUPS].set(w_rg[l]) \
            .at[:, N_GROUPS:N_GROUPS + N_EXPERTS].set(w_re[l])
        w_r_hi = w_r.astype(BF16)
        w_r_lo = (w_r - w_r_hi.astype(F32)).astype(BF16)
        wr = jnp.concatenate([w_r_hi, w_r_lo], axis=1)
        br = jnp.zeros((1, ROUTER_COLS), F32).at[0, 0:N_GROUPS].set(b_rg[l]) \
            .at[0, N_GROUPS:N_GROUPS + N_EXPERTS].set(b_re[l])
        lng1, lnb1 = ln1_g[l].reshape(1, d_), ln1_b[l].reshape(1, d_)
        lng2, lnb2 = ln2_g[l].reshape(1, d_), ln2_b[l].reshape(1, d_)

        n_tot = n_lat if last else n_lat + n_ctx
        gvecs1 = jnp.concatenate([g1, mc[0:1, 2]], axis=0)
        scvecs2 = jnp.concatenate([sc2, mc[0:1, 4]], axis=0)
        shvecs2 = jnp.concatenate([sh2, mc[0:1, 3]], axis=0)
        if last:
            x1, h2t, route, tcnt = _outproj_call(att_x, att_x, yx, yx, x_arr, x_off, x_arr, x_off, n_lat, 0, s_,
                                                 w_out, l, gvecs1, lng1, lnb1, scvecs2, shvecs2, wr, br, alpha)
        else:
            att_c = _ctx_attn_call(sink, qc, kc, kcs, vc, vcs, b_, c_len)
            x1, h2t, route, tcnt = _outproj_call(att_x, att_c, yx, yc, x_arr, x_off, c_arr, c_off, n_lat, n_ctx,
                                                 s_, w_out, l, gvecs1, lng1, lnb1, scvecs2, shvecs2, wr, br, alpha)

        block_e, n_used, seg, pos = _slots(route, tcnt[:, 0, :])
        n_slots = block_e.shape[0] * MOE_BLOCK
        xs = _dispatch_call(seg, pos.reshape(n_tot // DISPATCH_TILE, 1, TOP_K * DISPATCH_TILE), h2t, n_slots)
        ys = _moe_call(block_e, n_used, xs, w1, w3, w2, l)

        n_tiles = n_tot // COMBINE_TILE
        pos_t = pos.reshape(n_tiles, COMBINE_TILE, TOP_K).transpose(0, 2, 1).reshape(n_tiles, 1, TOP_K * COMBINE_TILE)
        gvecs = jnp.concatenate([g2, mc[0:1, 5]], axis=0)
        out = _combine_call(pos_t, x1, route, gvecs, lng2, lnb2, ys, n_tot, s_ // COMBINE_TILE, b_, alpha)
        x_arr, x_off = out, 0
        c_arr, c_off = out, n_lat

    return x_arr.reshape(b_, s_, d_)
```

```python
import functools

import jax
import jax.numpy as jnp
from jax import lax
from jax.experimental import pallas as pl
from jax.experimental.pallas import tpu as pltpu

F32 = jnp.float32
BF16 = jnp.bfloat16

D_MODEL = 1024
GRID_W = 64
HEAD_DIM = 64
N_HEADS = 8
N_KV_HEADS = 2
ATTN_WIDTH = N_HEADS * HEAD_DIM
KV_WIDTH = N_KV_HEADS * HEAD_DIM
WINDOW = 128
ATTN_SCALE = HEAD_DIM ** -0.5
ROPE_BASE = 10000.0
CONV_WIDTH = D_MODEL // 4
GM_WIDTH = D_MODEL // 4
GM_GROUPS = 4
GM_HEAD = GM_WIDTH // GM_GROUPS
CHUNK = 128
LOCAL_WIDTH = CONV_WIDTH + GM_WIDTH
IN_WIDTH = ATTN_WIDTH + 2 * KV_WIDTH + 3 * CONV_WIDTH + 2 * GM_WIDTH
QKV_WIDTH = ATTN_WIDTH + 2 * KV_WIDTH
N_GROUPS = 4
EXP_PER_GROUP = 8
N_EXPERTS = N_GROUPS * EXP_PER_GROUP
TOP_K = 2
D_EXPERT = D_MODEL // 2
MOE_BLOCK = 512
LN_EPS = 1e-6
NEG_INF = -1e30

LANES = 128
SUBLANES = 8
VMEM_LIMIT_BYTES = 48 * 1024 * 1024
N_DMA_PRIORITIES = 2

ROW_TILE_ROWS = D_MODEL // LANES
assert ROW_TILE_ROWS == SUBLANES

ROUTER_COLS = LANES
ROUTE_EXPERT = 0
ROUTE_GATE = TOP_K
ROUTE_RANK = 2 * TOP_K
ROPE_HALF_PAIR = HEAD_DIM // 4


def _bdot(a, b):
    return jnp.dot(a, b, preferred_element_type=F32)


def _split_bf16(a):
    hi = a.astype(BF16)
    lo = (a - hi.astype(F32)).astype(BF16)
    return hi, lo


def _layer_norm(r):
    mu = jnp.mean(r, axis=-1, keepdims=True)
    d = r - mu
    var = jnp.mean(d * d, axis=-1, keepdims=True)
    return d * lax.rsqrt(var + LN_EPS)


def _params(*sem):
    return pltpu.CompilerParams(dimension_semantics=sem, vmem_limit_bytes=VMEM_LIMIT_BYTES)


def _rows_to_tiles(rows_ref, tiles_ref, sem):
    return [pltpu.make_async_copy(rows_ref.at[:, pl.ds(s * LANES, LANES)], tiles_ref.at[:, s, :], sem)
            for s in range(ROW_TILE_ROWS)]


def _tiles_to_rows(tiles_ref, rows_ref, sem):
    return [pltpu.make_async_copy(tiles_ref.at[:, s, :], rows_ref.at[:, pl.ds(s * LANES, LANES)], sem)
            for s in range(ROW_TILE_ROWS)]


def _start_all(copies):
    for cp in copies:
        cp.start()


def _wait_all(copies):
    for cp in copies:
        cp.wait()


ADA_ROWS = 16
ADA_TILE = 1536


def _ada_kernel(c_ref, w_ref, b_ref, o_ref):
    act = jax.nn.silu(c_ref[...])
    ah, al = _split_bf16(act)
    wh, wl = _split_bf16(w_ref[...])
    o_ref[...] = _bdot(ah, wh) + _bdot(ah, wl) + _bdot(al, wh) + b_ref[...]


def _ada_call(cin, w_ada, b_ada):
    depth, d, n = w_ada.shape
    return pl.pallas_call(
        _ada_kernel,
        grid=(depth, n // ADA_TILE),
        in_specs=[
            pl.BlockSpec((ADA_ROWS, d), lambda l, j: (0, 0)),
            pl.BlockSpec((None, d, ADA_TILE), lambda l, j: (l, 0, j)),
            pl.BlockSpec((None, 1, ADA_TILE), lambda l, j: (l, 0, j)),
        ],
        out_specs=pl.BlockSpec((None, ADA_ROWS, ADA_TILE), lambda l, j: (l, 0, j)),
        out_shape=jax.ShapeDtypeStruct((depth, ADA_ROWS, n), F32),
        compiler_params=_params("parallel", "parallel"),
        name="ada_modulation",
    )(cin, w_ada, b_ada.reshape(depth, 1, n))


def _inproj_kernel(x_ref, xp_ref, xn_ref, sc_ref, sh_ref, wf_ref, cos_ref, sa_ref, sb_ref,
                   cw_ref, gw_ref, gb_ref,
                   q_ref, k_ref, ksw_ref, v_ref, vsw_ref, y_ref, w_ref, *, ts):
    t = pl.program_id(1)
    nt = pl.num_programs(1)

    @pl.when((pl.program_id(0) == 0) & (t == 0))
    def _():
        w_ref[...] = wf_ref[...].astype(BF16)

    sc = 1.0 + sc_ref[...]
    sh = sh_ref[...]
    hx = (x_ref[...] * sc + sh).astype(BF16)

    pq = _bdot(hx, w_ref[:, 0:QKV_WIDTH])
    cos = cos_ref[...]
    sa = sa_ref[...]
    sb = sb_ref[...]

    def rope(z):
        return (z * cos + pltpu.roll(z, ROPE_HALF_PAIR, 1) * sa
                + pltpu.roll(z, LANES - ROPE_HALF_PAIR, 1) * sb)

    for j in range(ATTN_WIDTH // LANES):
        sl = slice(j * LANES, (j + 1) * LANES)
        q_ref[:, sl] = (rope(pq[:, sl]) * ATTN_SCALE).astype(BF16)
    kr = rope(pq[:, ATTN_WIDTH:ATTN_WIDTH + KV_WIDTH])
    k_ref[...] = kr.astype(BF16)
    ksw_ref[...] = pltpu.roll(kr, HEAD_DIM, 1).astype(BF16)
    vv = pq[:, ATTN_WIDTH + KV_WIDTH:QKV_WIDTH]
    v_ref[...] = vv.astype(BF16)
    vsw_ref[...] = pltpu.roll(vv, HEAD_DIM, 1).astype(BF16)

    pm = _bdot(hx, w_ref[:, QKV_WIDTH:IN_WIDTH])
    cb = pm[:, 0:CONV_WIDTH]
    u = pm[:, CONV_WIDTH:2 * CONV_WIDTH] * pm[:, 2 * CONV_WIDTH:3 * CONV_WIDTH]

    halo = jnp.concatenate([xp_ref[...], xn_ref[...]], axis=0)
    hh = (halo * sc + sh).astype(BF16)
    ph = _bdot(hh, w_ref[:, QKV_WIDTH + CONV_WIDTH:QKV_WIDTH + 3 * CONV_WIDTH])
    uh = ph[:, 0:CONV_WIDTH] * ph[:, CONV_WIDTH:2 * CONV_WIDTH]
    up_row = jnp.where(t > 0, uh[SUBLANES - 1:SUBLANES, :], 0.0)
    dn_row = jnp.where(t < nt - 1, uh[SUBLANES:SUBLANES + 1, :], 0.0)
    ridx = lax.broadcasted_iota(jnp.int32, (ts, CONV_WIDTH), 0)
    u_up = jnp.where(ridx == 0, up_row, pltpu.roll(u, 1, 0))
    u_dn = jnp.where(ridx == ts - 1, dn_row, pltpu.roll(u, ts - 1, 0))
    cw = cw_ref[...]
    y_conv = cb * (u_up * cw[0:1, :] + u * cw[1:2, :] + u_dn * cw[2:3, :])
    y_ref[:, 0:CONV_WIDTH] = y_conv.astype(BF16)

    gu = jax.nn.gelu(pm[:, 3 * CONV_WIDTH:3 * CONV_WIDTH + GM_WIDTH])
    gv = _layer_norm(jax.nn.gelu(pm[:, 3 * CONV_WIDTH + GM_WIDTH:3 * CONV_WIDTH + 2 * GM_WIDTH])).astype(BF16)
    lane = lax.broadcasted_iota(jnp.int32, (CHUNK, LANES), 1)
    zero = jnp.zeros((CHUNK, LANES), BF16)
    for c in range(ts // CHUNK):
        rows = slice(c * CHUNK, (c + 1) * CHUNK)
        for j in range(GM_WIDTH // LANES):
            cols = slice(j * LANES, (j + 1) * LANES)
            vp = gv[rows, cols]
            s = (_bdot(gw_ref[2 * j], jnp.where(lane < GM_HEAD, vp, zero))
                 + _bdot(gw_ref[2 * j + 1], jnp.where(lane >= GM_HEAD, vp, zero))
                 + gb_ref[:, cols])
            y_ref[rows, CONV_WIDTH + j * LANES:CONV_WIDTH + (j + 1) * LANES] = (gu[rows, cols] * s).astype(BF16)


def _inproj_call(x2d, row_off, n_seq, seq_len, ts, sc, sh, w_in, layer, cos, sa, sb, conv_w, gm_w, gm_b):
    nt = seq_len // ts
    off_t = row_off // ts
    off_8 = row_off // SUBLANES
    last_8 = x2d.shape[0] // SUBLANES - 1
    per_tile_8 = ts // SUBLANES
    n_out = n_seq * seq_len
    d = D_MODEL

    def tile_idx(b, t):
        return (off_t + b * nt + t, 0)

    def prev_idx(b, t):
        return (jnp.maximum(off_8 + (b * nt + t) * per_tile_8 - 1, 0), 0)

    def next_idx(b, t):
        return (jnp.minimum(off_8 + (b * nt + t + 1) * per_tile_8, last_8), 0)

    def out_idx(b, t):
        return (b * nt + t, 0)

    vec = pl.BlockSpec((None, 1, d), lambda b, t: (b, 0, 0))
    rope_spec = pl.BlockSpec((ts, LANES), lambda b, t: (t, 0))
    kv_spec = pl.BlockSpec((ts, KV_WIDTH), out_idx)
    kv_shape = jax.ShapeDtypeStruct((n_out, KV_WIDTH), BF16)
    return pl.pallas_call(
        functools.partial(_inproj_kernel, ts=ts),
        grid=(n_seq, nt),
        in_specs=[
            pl.BlockSpec((ts, d), tile_idx),
            pl.BlockSpec((SUBLANES, d), prev_idx),
            pl.BlockSpec((SUBLANES, d), next_idx),
            vec, vec,
            pl.BlockSpec((None, d, IN_WIDTH), lambda b, t: (layer, 0, 0)),
            rope_spec, rope_spec, rope_spec,
            pl.BlockSpec((3, CONV_WIDTH), lambda b, t: (0, 0)),
            pl.BlockSpec((GM_GROUPS, CHUNK, CHUNK), lambda b, t: (0, 0, 0)),
            pl.BlockSpec((CHUNK, GM_WIDTH), lambda b, t: (0, 0)),
        ],
        out_specs=[
            pl.BlockSpec((ts, ATTN_WIDTH), out_idx),
            kv_spec, kv_spec, kv_spec, kv_spec,
            pl.BlockSpec((ts, LOCAL_WIDTH), out_idx),
        ],
        out_shape=[
            jax.ShapeDtypeStruct((n_out, ATTN_WIDTH), BF16),
            kv_shape, kv_shape, kv_shape, kv_shape,
            jax.ShapeDtypeStruct((n_out, LOCAL_WIDTH), BF16),
        ],
        scratch_shapes=[pltpu.VMEM((d, IN_WIDTH), BF16)],
        compiler_params=_params("arbitrary", "arbitrary"),
        name="in_projection",
    )(x2d, x2d, x2d, sc, sh, w_in, cos, sa, sb, conv_w, gm_w, gm_b)


def _attn_body(q, keys, keys_sw, vals, vals_sw, bias, sink_ref, o_ref, tq):
    nk = keys.shape[0]
    lane = lax.broadcasted_iota(jnp.int32, (nk, LANES), 1)
    lo = lane < HEAD_DIM
    zero = jnp.zeros((nk, LANES), BF16)
    rid = lax.broadcasted_iota(jnp.int32, (2 * tq, 1), 0)
    nt_dims = (((1,), (1,)), ((), ()))
    gqa = N_HEADS // N_KV_HEADS
    for h in range(N_KV_HEADS):
        k_own, k_oth = (keys, keys_sw) if h == 0 else (keys_sw, keys)
        v_own, v_oth = (vals, vals_sw) if h == 0 else (vals_sw, vals)
        kz = (jnp.where(lo, k_own, zero), jnp.where(lo, zero, k_oth))
        vz = (jnp.where(lo, v_own, zero), jnp.where(lo, zero, v_oth))
        c0 = h * gqa * HEAD_DIM
        qs = jnp.concatenate([q[:, c0:c0 + LANES], q[:, c0 + LANES:c0 + 2 * LANES]], axis=0)
        acc = None
        for par in range(2):
            s = lax.dot_general(qs, kz[par], nt_dims, preferred_element_type=F32)
            if bias is not None:
                s = s + bias
            sink = jnp.where(rid < tq, sink_ref[h * gqa + par], sink_ref[h * gqa + 2 + par])
            m = jnp.maximum(jnp.max(s, axis=-1, keepdims=True), sink)
            p = jnp.exp(s - m)
            den = jnp.sum(p, axis=-1, keepdims=True) + jnp.exp(sink - m)
            o = _bdot(p.astype(BF16), vz[par]) / den
            acc = o if acc is None else acc + o
        o_ref[:, c0:c0 + LANES] = acc[0:tq].astype(BF16)
        o_ref[:, c0 + LANES:c0 + 2 * LANES] = acc[tq:2 * tq].astype(BF16)


def _win_attn_kernel(sink_ref, q_ref, kp_ref, kc_ref, kn_ref, kx_ref,
                     ksp_ref, ksc_ref, ksn_ref, ksx_ref,
                     vp_ref, vc_ref, vn_ref, vx_ref,
                     vsp_ref, vsc_ref, vsn_ref, vsx_ref, o_ref, *, tq, ctx_len):
    n = pl.program_id(1)
    nb = pl.num_programs(1)
    row = lax.broadcasted_iota(jnp.int32, (tq, tq), 0)
    col = lax.broadcasted_iota(jnp.int32, (tq, tq), 1)
    b_prev = jnp.where((col >= row) & (n > 0), 0.0, NEG_INF)
    b_next = jnp.where((col <= row) & (n < nb - 1), 0.0, NEG_INF)
    b_half = jnp.concatenate([b_prev, jnp.zeros((tq, tq), F32), b_next, jnp.zeros((tq, ctx_len), F32)], axis=1)
    bias = jnp.concatenate([b_half, b_half], axis=0)

    def cat(a, b, c, d):
        return jnp.concatenate([a[...], b[...], c[...], d[...]], axis=0)

    _attn_body(q_ref[...], cat(kp_ref, kc_ref, kn_ref, kx_ref), cat(ksp_ref, ksc_ref, ksn_ref, ksx_ref),
               cat(vp_ref, vc_ref, vn_ref, vx_ref), cat(vsp_ref, vsc_ref, vsn_ref, vsx_ref),
               bias, sink_ref, o_ref, tq)


def _win_attn_call(sink, q, k, ksw, v, vsw, kx, kxsw, vx, vxsw, n_seq, seq_len, ctx_len):
    tq = WINDOW
    nb = seq_len // tq

    def cur(b, n):
        return (b * nb + n, 0)

    def prev(b, n):
        return (b * nb + jnp.maximum(n - 1, 0), 0)

    def nxt(b, n):
        return (b * nb + jnp.minimum(n + 1, nb - 1), 0)

    def ctx(b, n):
        return (b, 0)

    def kv_specs():
        return [pl.BlockSpec((tq, KV_WIDTH), prev), pl.BlockSpec((tq, KV_WIDTH), cur),
                pl.BlockSpec((tq, KV_WIDTH), nxt), pl.BlockSpec((ctx_len, KV_WIDTH), ctx)]

    return pl.pallas_call(
        functools.partial(_win_attn_kernel, tq=tq, ctx_len=ctx_len),
        grid=(n_seq, nb),
        in_specs=[pl.BlockSpec(memory_space=pltpu.SMEM), pl.BlockSpec((tq, ATTN_WIDTH), cur)]
        + kv_specs() + kv_specs() + kv_specs() + kv_specs(),
        out_specs=pl.BlockSpec((tq, ATTN_WIDTH), cur),
        out_shape=jax.ShapeDtypeStruct((n_seq * seq_len, ATTN_WIDTH), BF16),
        compiler_params=_params("parallel", "parallel"),
        name="window_attention",
    )(sink, q, k, k, k, kx, ksw, ksw, ksw, kxsw, v, v, v, vx, vsw, vsw, vsw, vxsw)


def _ctx_attn_kernel(sink_ref, q_ref, k_ref, ks_ref, v_ref, vs_ref, o_ref, *, tq):
    _attn_body(q_ref[...], k_ref[...], ks_ref[...], v_ref[...], vs_ref[...], None, sink_ref, o_ref, tq)


def _ctx_attn_call(sink, q, k, ksw, v, vsw, n_seq, ctx_len):
    def blk(w):
        return pl.BlockSpec((ctx_len, w), lambda b: (b, 0))

    return pl.pallas_call(
        functools.partial(_ctx_attn_kernel, tq=ctx_len),
        grid=(n_seq,),
        in_specs=[pl.BlockSpec(memory_space=pltpu.SMEM), blk(ATTN_WIDTH),
                  blk(KV_WIDTH), blk(KV_WIDTH), blk(KV_WIDTH), blk(KV_WIDTH)],
        out_specs=blk(ATTN_WIDTH),
        out_shape=jax.ShapeDtypeStruct((n_seq * ctx_len, ATTN_WIDTH), BF16),
        compiler_params=_params("parallel"),
        name="context_attention",
    )(sink, q, k, ksw, v, vsw)


OUT_TILE = 512


def _outproj_kernel(attx_ref, attc_ref, yx_ref, yc_ref, xx_ref, xc_ref, wof_ref, g_ref, lng_ref, lnb_ref,
                    sc_ref, sh_ref, wr_ref, br_ref, x1_ref, h2v_ref, route_ref, cnt_ref, wo_ref,
                    *, alpha, n_lat_tiles):
    @pl.when(pl.program_id(0) == 0)
    def _():
        wo_ref[...] = wof_ref[...].astype(BF16)

    is_lat = pl.program_id(0) < n_lat_tiles
    att = jnp.where(is_lat, attx_ref[...], attc_ref[...])
    yloc = jnp.where(is_lat, yx_ref[...], yc_ref[...])
    xin = jnp.where(is_lat, xx_ref[...], xc_ref[...])
    o = _bdot(att, wo_ref[0:ATTN_WIDTH, :]) + _bdot(yloc, wo_ref[ATTN_WIDTH:, :])
    x1 = _layer_norm(alpha * xin + g_ref[...] * o) * lng_ref[...] + lnb_ref[...]
    x1_ref[...] = x1
    h2 = x1 * (1.0 + sc_ref[...]) + sh_ref[...]
    for c in range(ROW_TILE_ROWS):
        h2v_ref[:, c, :, :] = h2[:, c * LANES:(c + 1) * LANES].reshape(h2.shape[0] // SUBLANES, SUBLANES, LANES)

    hh, hl = _split_bf16(h2)
    t1 = _bdot(hh, wr_ref[...])
    lg = t1[:, 0:ROUTER_COLS] + t1[:, ROUTER_COLS:] + _bdot(hl, wr_ref[:, 0:ROUTER_COLS]) + br_ref[...]

    ts = lg.shape[0]
    lane = lax.broadcasted_iota(jnp.int32, (ts, ROUTER_COLS), 1).astype(F32)
    big = jnp.float32(ROUTER_COLS)

    def top1(v):
        m = jnp.max(v, axis=-1, keepdims=True)
        return m, jnp.min(jnp.where(v == m, lane, big), axis=-1, keepdims=True)

    gl = jnp.where(lane < N_GROUPS, lg, NEG_INF)
    g_val, g_idx = top1(gl)
    lse = g_val + jnp.log(jnp.sum(jnp.exp(gl - g_val), axis=-1, keepdims=True))
    p_group = jnp.exp(g_val - lse)
    e_lo = N_GROUPS + EXP_PER_GROUP * g_idx
    el = jnp.where((lane >= e_lo) & (lane < e_lo + EXP_PER_GROUP), lg, NEG_INF)
    e1, l1 = top1(el)
    e2, l2 = top1(jnp.where(lane == l1, NEG_INF, el))
    z = jnp.exp(e2 - e1)
    gate1 = p_group / (1.0 + z)
    gate2 = p_group * z / (1.0 + z)
    x1id = l1 - N_GROUPS
    x2id = l2 - N_GROUPS

    sel1 = lane == x1id
    sel2 = lane == x2id
    onehot = jnp.where(sel1 | sel2, 1.0, 0.0)
    r_i = lax.broadcasted_iota(jnp.int32, (ts, ts), 0)
    c_i = lax.broadcasted_iota(jnp.int32, (ts, ts), 1)
    before = jnp.where(c_i < r_i, 1.0, 0.0).astype(BF16)
    prefix = _bdot(before, onehot.astype(BF16))
    rank1 = jnp.sum(jnp.where(sel1, prefix, 0.0), axis=-1, keepdims=True)
    rank2 = jnp.sum(jnp.where(sel2, prefix, 0.0), axis=-1, keepdims=True)
    route = jnp.zeros((ts, ROUTER_COLS), F32)
    for col, val in enumerate((x1id, x2id, gate1, gate2, rank1, rank2)):
        route = jnp.where(lane == col, val, route)
    route_ref[...] = route
    cnt_ref[...] = jnp.broadcast_to(jnp.sum(onehot, axis=0, keepdims=True), (SUBLANES, ROUTER_COLS))


def _outproj_call(att_x, att_c, y_x, y_c, x_arr, x_off, c_arr, c_off, n_lat, n_ctx, seq_len,
                  w_out, layer, gvecs, lng, lnb, scvecs, shvecs, wr, br, alpha):
    ts = OUT_TILE
    d = D_MODEL
    n_lat_tiles = n_lat // ts
    n_tiles = (n_lat + n_ctx) // ts
    tiles_per_seq = seq_len // ts
    n_seq = n_lat // seq_len
    xo, co = x_off // ts, c_off // ts

    def lat_loc(i):
        return (jnp.minimum(i, n_lat_tiles - 1), 0)

    def ctx_loc(i):
        return (jnp.maximum(i - n_lat_tiles, 0), 0)

    def lat_in(i):
        return (xo + jnp.minimum(i, n_lat_tiles - 1), 0)

    def ctx_in(i):
        return (co + jnp.maximum(i - n_lat_tiles, 0), 0)

    def vec_idx(i):
        return (jnp.where(i < n_lat_tiles, i // tiles_per_seq, n_seq), 0, 0)

    vecb = pl.BlockSpec((None, 1, d), vec_idx)
    vec0 = pl.BlockSpec((1, d), lambda i: (0, 0))
    return pl.pallas_call(
        functools.partial(_outproj_kernel, alpha=alpha, n_lat_tiles=n_lat_tiles),
        grid=(n_tiles,),
        in_specs=[
            pl.BlockSpec((ts, ATTN_WIDTH), lat_loc), pl.BlockSpec((ts, ATTN_WIDTH), ctx_loc),
            pl.BlockSpec((ts, LOCAL_WIDTH), lat_loc), pl.BlockSpec((ts, LOCAL_WIDTH), ctx_loc),
            pl.BlockSpec((ts, d), lat_in), pl.BlockSpec((ts, d), ctx_in),
            pl.BlockSpec((None, d, d), lambda i: (layer, 0, 0)),
            vecb, vec0, vec0, vecb, vecb,
            pl.BlockSpec((d, 2 * ROUTER_COLS), lambda i: (0, 0)),
            pl.BlockSpec((1, ROUTER_COLS), lambda i: (0, 0)),
        ],
        out_specs=[pl.BlockSpec((ts, d), lambda i: (i, 0)),
                   pl.BlockSpec((ts // SUBLANES, ROW_TILE_ROWS, SUBLANES, LANES), lambda i: (i, 0, 0, 0)),
                   pl.BlockSpec((ts, ROUTER_COLS), lambda i: (i, 0)),
                   pl.BlockSpec((None, SUBLANES, ROUTER_COLS), lambda i: (i, 0, 0))],
        out_shape=[jax.ShapeDtypeStruct((n_lat + n_ctx, d), F32),
                   jax.ShapeDtypeStruct(((n_lat + n_ctx) // SUBLANES, ROW_TILE_ROWS, SUBLANES, LANES), F32),
                   jax.ShapeDtypeStruct((n_lat + n_ctx, ROUTER_COLS), F32),
                   jax.ShapeDtypeStruct((n_tiles, SUBLANES, ROUTER_COLS), F32)],
        scratch_shapes=[pltpu.VMEM((d, d), BF16)],
        compiler_params=_params("arbitrary"),
        name="out_projection",
    )(att_x, att_c, y_x, y_c, x_arr, c_arr, w_out, gvecs, lng, lnb, scvecs, shvecs, wr, br)


DISPATCH_TILE = 512


def _dispatch_kernel(seg_ref, pos_ref, h_ref, xs_out, zbuf, zsem, sem):
    @pl.when(pl.program_id(0) == 0)
    def _():
        zbuf[...] = jnp.zeros_like(zbuf)

        def zero_copy(e):
            return pltpu.make_async_copy(zbuf, xs_out.at[pl.ds(seg_ref[0, e] - MOE_BLOCK, MOE_BLOCK)], zsem)

        for e in range(N_EXPERTS):
            @pl.when(seg_ref[1, e] > 0)
            def _():
                zero_copy(e).start()
        for e in range(N_EXPERTS):
            @pl.when(seg_ref[1, e] > 0)
            def _():
                zero_copy(e).wait()

        def tail_copy(b):
            return pltpu.make_async_copy(zbuf, xs_out.at[pl.ds(b * MOE_BLOCK, MOE_BLOCK)], zsem)

        n_blocks = xs_out.shape[0] // MOE_BLOCK
        first_unused = seg_ref[0, N_EXPERTS - 1] // MOE_BLOCK
        lax.fori_loop(first_unused, n_blocks, lambda b, c: (tail_copy(b).start(), c)[1], 0)
        lax.fori_loop(first_unused, n_blocks, lambda b, c: (tail_copy(b).wait(), c)[1], 0)

    def body(j, carry):
        for s in range(SUBLANES):
            t = SUBLANES * j + s
            for k in range(TOP_K):
                pltpu.make_async_copy(h_ref.at[j, :, s, :], xs_out.at[pos_ref[0, 0, TOP_K * t + k]],
                                      sem).start(priority=k)
        return carry

    lax.fori_loop(0, DISPATCH_TILE // SUBLANES, body, 0)
    for k in range(TOP_K):
        pltpu.make_async_copy(h_ref, h_ref, sem).wait()


def _dispatch_call(seg, pos, h2t, n_slots):
    n_tok = h2t.shape[0] * SUBLANES
    n_tiles = n_tok // DISPATCH_TILE
    row = (ROW_TILE_ROWS, LANES)
    grid_spec = pltpu.PrefetchScalarGridSpec(
        num_scalar_prefetch=1,
        grid=(n_tiles,),
        in_specs=[
            pl.BlockSpec((1, 1, TOP_K * DISPATCH_TILE), lambda i, seg: (i, 0, 0), memory_space=pltpu.SMEM),
            pl.BlockSpec((DISPATCH_TILE // SUBLANES, ROW_TILE_ROWS, SUBLANES, LANES), lambda i, seg: (i, 0, 0, 0)),
        ],
        out_specs=pl.BlockSpec(memory_space=pl.ANY),
        scratch_shapes=[pltpu.VMEM((MOE_BLOCK,) + row, F32), pltpu.SemaphoreType.DMA(()),
                        pltpu.SemaphoreType.DMA(())],
    )
    return pl.pallas_call(
        _dispatch_kernel,
        grid_spec=grid_spec,
        out_shape=jax.ShapeDtypeStruct((n_slots,) + row, F32),
        compiler_params=_params("arbitrary"),
        name="moe_dispatch",
    )(seg, pos, h2t)


def _moe_kernel(be_ref, nu_ref, xs_hbm, w1_ref, w3_ref, w2_ref, ys_hbm, w1b, w3b, w2b, xbuf, obuf, isem, osem):
    i = pl.program_id(0)
    n_used = nu_ref[0]
    slot = i % 2

    def in_copies(blk, at_slot):
        return _tiles_to_rows(xs_hbm.at[pl.ds(blk * MOE_BLOCK, MOE_BLOCK)], xbuf.at[at_slot], isem.at[at_slot])

    def out_copies(blk, at_slot):
        return _rows_to_tiles(obuf.at[at_slot], ys_hbm.at[pl.ds(blk * MOE_BLOCK, MOE_BLOCK)], osem.at[at_slot])

    @pl.when(i == 0)
    def _():
        _start_all(in_copies(0, 0))

    @pl.when(i + 1 < n_used)
    def _():
        _start_all(in_copies(i + 1, 1 - slot))

    @pl.when((i == 0) | (be_ref[i] != be_ref[jnp.maximum(i - 1, 0)]))
    def _():
        w1b[...] = w1_ref[...].astype(BF16)
        w3b[...] = w3_ref[...].astype(BF16)
        w2b[...] = w2_ref[...].astype(BF16)

    @pl.when(i < n_used)
    def _():
        _wait_all(in_copies(i, slot))
        x = xbuf[slot].astype(BF16)
        obuf[slot] = _bdot((jax.nn.silu(_bdot(x, w1b[...])) * _bdot(x, w3b[...])).astype(BF16), w2b[...])
        _start_all(out_copies(i, slot))

        @pl.when(i >= 1)
        def _():
            _wait_all(out_copies(i - 1, 1 - slot))

        @pl.when(i == n_used - 1)
        def _():
            _wait_all(out_copies(i, slot))

    @pl.when(i >= n_used)
    def _():
        obuf[slot] = jnp.zeros(obuf.shape[1:], F32)
        _start_all(out_copies(i, slot))
        _wait_all(out_copies(i, slot))


def _moe_call(block_e, n_used, xs, w1, w3, w2, layer):
    n_blocks = block_e.shape[0]
    d = D_MODEL
    grid_spec = pltpu.PrefetchScalarGridSpec(
        num_scalar_prefetch=2,
        grid=(n_blocks,),
        in_specs=[
            pl.BlockSpec(memory_space=pl.ANY),
            pl.BlockSpec((None, None, d, D_EXPERT), lambda i, be, nu: (layer, be[i], 0, 0)),
            pl.BlockSpec((None, None, d, D_EXPERT), lambda i, be, nu: (layer, be[i], 0, 0)),
            pl.BlockSpec((None, None, D_EXPERT, d), lambda i, be, nu: (layer, be[i], 0, 0)),
        ],
        out_specs=pl.BlockSpec(memory_space=pl.ANY),
        scratch_shapes=[pltpu.VMEM((d, D_EXPERT), BF16), pltpu.VMEM((d, D_EXPERT), BF16),
                        pltpu.VMEM((D_EXPERT, d), BF16),
                        pltpu.VMEM((2, MOE_BLOCK, d), F32), pltpu.VMEM((2, MOE_BLOCK, d), F32),
                        pltpu.SemaphoreType.DMA((2,)), pltpu.SemaphoreType.DMA((2,))],
    )
    return pl.pallas_call(
        _moe_kernel,
        grid_spec=grid_spec,
        out_shape=jax.ShapeDtypeStruct((n_blocks * MOE_BLOCK, ROW_TILE_ROWS, LANES), F32),
        compiler_params=_params("arbitrary"),
        name="moe_experts",
    )(block_e, n_used, xs, w1, w3, w2)


COMBINE_TILE = 256


def _combine_kernel(pos0_ref, posn_ref, x1_ref, route_ref, g_ref, lng_ref, lnb_ref, ys_hbm, o_ref, ybuf, sem,
                    *, alpha):
    i = pl.program_id(0)
    nb = pl.num_programs(0)
    n_rows = TOP_K * COMBINE_TILE

    def issue(pos_ref, slot):
        def body(j, carry):
            for s in range(SUBLANES):
                pltpu.make_async_copy(ys_hbm.at[pos_ref[0, 0, SUBLANES * j + s]], ybuf.at[slot, j, :, s, :],
                                      sem.at[slot]).start(priority=s % N_DMA_PRIORITIES)
            return carry
        lax.fori_loop(0, n_rows // SUBLANES, body, 0)

    @pl.when(i == 0)
    def _():
        issue(pos0_ref, 0)

    @pl.when(i + 1 < nb)
    def _():
        issue(posn_ref, (i + 1) % 2)

    slot = i % 2
    pltpu.make_async_copy(ybuf.at[slot], ybuf.at[slot], sem.at[slot]).wait()
    yrows = jnp.concatenate([ybuf[slot, :, c, :, :].reshape(n_rows, LANES) for c in range(ROW_TILE_ROWS)], axis=1)
    route = route_ref[...]
    y = (yrows[0:COMBINE_TILE, :] * route[:, ROUTE_GATE:ROUTE_GATE + 1]
         + yrows[COMBINE_TILE:, :] * route[:, ROUTE_GATE + 1:ROUTE_GATE + 2])
    o_ref[...] = _layer_norm(alpha * x1_ref[...] + g_ref[...] * y) * lng_ref[...] + lnb_ref[...]


def _combine_call(pos, x1, route, gvecs, lng, lnb, ys, n_tok, tiles_per_seq, n_seq, alpha):
    d = D_MODEL
    n_tiles = n_tok // COMBINE_TILE
    n_lat_tiles = tiles_per_seq * n_seq

    def g_idx(i):
        return (jnp.where(i < n_lat_tiles, i // tiles_per_seq, n_seq), 0, 0)

    smem_blk = (1, 1, TOP_K * COMBINE_TILE)
    return pl.pallas_call(
        functools.partial(_combine_kernel, alpha=alpha),
        grid=(n_tiles,),
        in_specs=[
            pl.BlockSpec(smem_blk, lambda i: (0, 0, 0), memory_space=pltpu.SMEM),
            pl.BlockSpec(smem_blk, lambda i: (jnp.minimum(i + 1, n_tiles - 1), 0, 0), memory_space=pltpu.SMEM),
            pl.BlockSpec((COMBINE_TILE, d), lambda i: (i, 0)),
            pl.BlockSpec((COMBINE_TILE, ROUTER_COLS), lambda i: (i, 0)),
            pl.BlockSpec((None, 1, d), g_idx),
            pl.BlockSpec((1, d), lambda i: (0, 0)),
            pl.BlockSpec((1, d), lambda i: (0, 0)),
            pl.BlockSpec(memory_space=pl.ANY),
        ],
        out_specs=pl.BlockSpec((COMBINE_TILE, d), lambda i: (i, 0)),
        out_shape=jax.ShapeDtypeStruct((n_tok, d), F32),
        scratch_shapes=[pltpu.VMEM((2, TOP_K * COMBINE_TILE // SUBLANES, ROW_TILE_ROWS, SUBLANES, LANES), F32),
                        pltpu.SemaphoreType.DMA((2,))],
        compiler_params=_params("arbitrary"),
        name="moe_combine",
    )(pos, pos, x1, route, gvecs, lng, lnb, ys)


def _slots(route, tile_counts):
    n = route.shape[0]
    n_tiles = tile_counts.shape[0]
    tc = tile_counts.astype(jnp.int32)
    counts = jnp.sum(tc, axis=0)
    padded = (counts + MOE_BLOCK - 1) // MOE_BLOCK * MOE_BLOCK
    pends = jnp.cumsum(padded)
    base = (pends - padded)[None, :] + jnp.cumsum(tc, axis=0) - tc
    n_blocks = -(-n * TOP_K // MOE_BLOCK) + N_EXPERTS
    block_start = jnp.arange(n_blocks, dtype=jnp.int32) * MOE_BLOCK
    block_e = jnp.minimum(jnp.sum(pends[None, 0:N_EXPERTS] <= block_start[:, None], axis=1),
                          N_EXPERTS - 1).astype(jnp.int32)
    n_used = (pends[N_EXPERTS - 1] // MOE_BLOCK).astype(jnp.int32).reshape(1)
    seg = jnp.stack([pends[0:N_EXPERTS], padded[0:N_EXPERTS]]).astype(jnp.int32)
    r3 = route.reshape(n_tiles, n // n_tiles, ROUTER_COLS)
    lane = lax.broadcasted_iota(jnp.int32, r3.shape, 2)
    pos = []
    for k in range(TOP_K):
        e = r3[:, :, ROUTE_EXPERT + k].astype(jnp.int32)
        b = jnp.sum(jnp.where(lane == e[:, :, None], base[:, None, :], 0), axis=-1)
        pos.append((b + r3[:, :, ROUTE_RANK + k].astype(jnp.int32)).reshape(n))
    return block_e, n_used, seg, jnp.stack(pos, axis=-1)


def _rope_tables(seq_len):
    m = HEAD_DIM // 4
    freqs = ROPE_BASE ** (-jnp.arange(m, dtype=F32) / m)
    t = jnp.arange(seq_len)
    row = (t // GRID_W).astype(F32)[:, None] * freqs[None, :]
    col = (t % GRID_W).astype(F32)[:, None] * freqs[None, :]
    cos = jnp.concatenate([jnp.cos(row), jnp.cos(row), jnp.cos(col), jnp.cos(col)], axis=-1)
    zero = jnp.zeros_like(row)
    sin_a = jnp.concatenate([zero, jnp.sin(row), zero, jnp.sin(col)], axis=-1)
    sin_b = jnp.concatenate([-jnp.sin(row), zero, -jnp.sin(col), zero], axis=-1)
    rep = LANES // HEAD_DIM
    return jnp.tile(cos, (1, rep)), jnp.tile(sin_a, (1, rep)), jnp.tile(sin_b, (1, rep))


def kernel(x, c, ctx, c_ctx, w_ada, b_ada, w_in, conv_w, attn_sink, gm_ws, gm_bs, w_out, ln1_g, ln1_b,
           w_rg, b_rg, w_re, b_re, w1, w3, w2, ln2_g, ln2_b):
    b_, s_, d_ = x.shape
    c_len = ctx.shape[1]
    depth = w_ada.shape[0]
    alpha = (2 * depth) ** 0.25
    n_lat = b_ * s_
    n_ctx = b_ * c_len
    ts = 1024

    cin = jnp.zeros((ADA_ROWS, d_), F32).at[0:b_].set(c).at[b_].set(c_ctx)
    mod = _ada_call(cin, w_ada, b_ada)

    cos, sin_a, sin_b = _rope_tables(s_)
    ones_c = jnp.ones((c_len, LANES), F32)
    zeros_c = jnp.zeros((c_len, LANES), F32)

    x_flat = x.reshape(n_lat, d_)
    c_flat = ctx.reshape(n_ctx, d_)
    x_off, c_off = 0, 0
    x_arr, c_arr = x_flat, c_flat

    for l in range(depth):
        last = l == depth - 1
        mx = mod[l, 0:b_].reshape(b_, 6, 1, d_)
        sh1, sc1, g1, sh2, sc2, g2 = (mx[:, i] for i in range(6))
        mc = jnp.broadcast_to(mod[l, b_].reshape(1, 6, 1, d_), (b_, 6, 1, d_))
        gm_w = gm_ws[l].astype(BF16)
        gm_b = jnp.repeat(gm_bs[l].T, GM_HEAD, axis=1)
        sink = attn_sink[l]

        qx, kx, kxs, vx, vxs, yx = _inproj_call(x_arr, x_off, b_, s_, ts, sc1, sh1, w_in, l, cos, sin_a, sin_b,
                                                conv_w[l], gm_w, gm_b)
        qc, kc, kcs, vc, vcs, yc = _inproj_call(c_arr, c_off, b_, c_len, c_len, mc[:, 1], mc[:, 0], w_in, l,
                                                ones_c, zeros_c, zeros_c, conv_w[l], gm_w, gm_b)
        att_x = _win_attn_call(sink, qx, kx, kxs, vx, vxs, kc, kcs, vc, vcs, b_, s_, c_len)

        w_r = jnp.zeros((d_, ROUTER_COLS), F32).at[:, 0:N_GROUPS].set(w_rg[l]) \
            .at[:, N_GROUPS:N_GROUPS + N_EXPERTS].set(w_re[l])
        w_r_hi = w_r.astype(BF16)
        w_r_lo = (w_r - w_r_hi.astype(F32)).astype(BF16)
        wr = jnp.concatenate([w_r_hi, w_r_lo], axis=1)
        br = jnp.zeros((1, ROUTER_COLS), F32).at[0, 0:N_GROUPS].set(b_rg[l]) \
            .at[0, N_GROUPS:N_GROUPS + N_EXPERTS].set(b_re[l])
        lng1, lnb1 = ln1_g[l].reshape(1, d_), ln1_b[l].reshape(1, d_)
        lng2, lnb2 = ln2_g[l].reshape(1, d_), ln2_b[l].reshape(1, d_)

        n_tot = n_lat if last else n_lat + n_ctx
        gvecs1 = jnp.concatenate([g1, mc[0:1, 2]], axis=0)
        scvecs2 = jnp.concatenate([sc2, mc[0:1, 4]], axis=0)
        shvecs2 = jnp.concatenate([sh2, mc[0:1, 3]], axis=0)
        if last:
            x1, h2t, route, tcnt = _outproj_call(att_x, att_x, yx, yx, x_arr, x_off, x_arr, x_off, n_lat, 0, s_,
                                                 w_out, l, gvecs1, lng1, lnb1, scvecs2, shvecs2, wr, br, alpha)
        else:
            att_c = _ctx_attn_call(sink, qc, kc, kcs, vc, vcs, b_, c_len)
            x1, h2t, route, tcnt = _outproj_call(att_x, att_c, yx, yc, x_arr, x_off, c_arr, c_off, n_lat, n_ctx,
                                                 s_, w_out, l, gvecs1, lng1, lnb1, scvecs2, shvecs2, wr, br, alpha)

        block_e, n_used, seg, pos = _slots(route, tcnt[:, 0, :])
        n_slots = block_e.shape[0] * MOE_BLOCK
        xs = _dispatch_call(seg, pos.reshape(n_tot // DISPATCH_TILE, 1, TOP_K * DISPATCH_TILE), h2t, n_slots)
        ys = _moe_call(block_e, n_used, xs, w1, w3, w2, l)

        n_tiles = n_tot // COMBINE_TILE
        pos_t = pos.reshape(n_tiles, COMBINE_TILE, TOP_K).transpose(0, 2, 1).reshape(n_tiles, 1, TOP_K * COMBINE_TILE)
        gvecs = jnp.concatenate([g2, mc[0:1, 5]], axis=0)
        out = _combine_call(pos_t, x1, route, gvecs, lng2, lnb2, ys, n_tot, s_ // COMBINE_TILE, b_, alpha)
        x_arr, x_off = out, 0
        c_arr, c_off = out, n_lat

    return x_arr.reshape(b_, s_, d_)
```

```python
import functools

import jax
import jax.numpy as jnp
from jax import lax
from jax.experimental import pallas as pl
from jax.experimental.pallas import tpu as pltpu

F32 = jnp.float32
BF16 = jnp.bfloat16

D_MODEL = 1024
GRID_W = 64
HEAD_DIM = 64
N_HEADS = 8
N_KV_HEADS = 2
ATTN_WIDTH = N_HEADS * HEAD_DIM
KV_WIDTH = N_KV_HEADS * HEAD_DIM
WINDOW = 128
ATTN_SCALE = HEAD_DIM ** -0.5
ROPE_BASE = 10000.0
CONV_WIDTH = D_MODEL // 4
GM_WIDTH = D_MODEL // 4
GM_GROUPS = 4
GM_HEAD = GM_WIDTH // GM_GROUPS
CHUNK = 128
LOCAL_WIDTH = CONV_WIDTH + GM_WIDTH
IN_WIDTH = ATTN_WIDTH + 2 * KV_WIDTH + 3 * CONV_WIDTH + 2 * GM_WIDTH
QKV_WIDTH = ATTN_WIDTH + 2 * KV_WIDTH
N_GROUPS = 4
EXP_PER_GROUP = 8
N_EXPERTS = N_GROUPS * EXP_PER_GROUP
TOP_K = 2
D_EXPERT = D_MODEL // 2
MOE_BLOCK = 512
LN_EPS = 1e-6
NEG_INF = -1e30

LANES = 128
SUBLANES = 8
VMEM_LIMIT_BYTES = 48 * 1024 * 1024
N_DMA_PRIORITIES = 2

ROW_TILE_ROWS = D_MODEL // LANES
assert ROW_TILE_ROWS == SUBLANES

ROUTER_COLS = LANES
ROUTE_EXPERT = 0
ROUTE_GATE = TOP_K
ROUTE_RANK = 2 * TOP_K
ROPE_HALF_PAIR = HEAD_DIM // 4


def _bdot(a, b):
    return jnp.dot(a, b, preferred_element_type=F32)


def _split_bf16(a):
    hi = a.astype(BF16)
    lo = (a - hi.astype(F32)).astype(BF16)
    return hi, lo


def _layer_norm(r):
    mu = jnp.mean(r, axis=-1, keepdims=True)
    d = r - mu
    var = jnp.mean(d * d, axis=-1, keepdims=True)
    return d * lax.rsqrt(var + LN_EPS)


def _params(*sem):
    return pltpu.CompilerParams(dimension_semantics=sem, vmem_limit_bytes=VMEM_LIMIT_BYTES)


def _rows_to_tiles(rows_ref, tiles_ref, sem):
    return [pltpu.make_async_copy(rows_ref.at[:, pl.ds(s * LANES, LANES)], tiles_ref.at[:, s, :], sem)
            for s in range(ROW_TILE_ROWS)]


def _tiles_to_rows(tiles_ref, rows_ref, sem):
    return [pltpu.make_async_copy(tiles_ref.at[:, s, :], rows_ref.at[:, pl.ds(s * LANES, LANES)], sem)
            for s in range(ROW_TILE_ROWS)]


def _start_all(copies):
    for cp in copies:
        cp.start()


def _wait_all(copies):
    for cp in copies:
        cp.wait()


ADA_ROWS = 16
ADA_TILE = 1536


def _ada_kernel(c_ref, w_ref, b_ref, o_ref):
    act = jax.nn.silu(c_ref[...])
    ah, al = _split_bf16(act)
    wh, wl = _split_bf16(w_ref[...])
    o_ref[...] = _bdot(ah, wh) + _bdot(ah, wl) + _bdot(al, wh) + b_ref[...]


def _ada_call(cin, w_ada, b_ada):
    depth, d, n = w_ada.shape
    return pl.pallas_call(
        _ada_kernel,
        grid=(depth, n // ADA_TILE),
        in_specs=[
            pl.BlockSpec((ADA_ROWS, d), lambda l, j: (0, 0)),
            pl.BlockSpec((None, d, ADA_TILE), lambda l, j: (l, 0, j)),
            pl.BlockSpec((None, 1, ADA_TILE), lambda l, j: (l, 0, j)),
        ],
        out_specs=pl.BlockSpec((None, ADA_ROWS, ADA_TILE), lambda l, j: (l, 0, j)),
        out_shape=jax.ShapeDtypeStruct((depth, ADA_ROWS, n), F32),
        compiler_params=_params("parallel", "parallel"),
        name="ada_modulation",
    )(cin, w_ada, b_ada.reshape(depth, 1, n))


def _inproj_kernel(x_ref, xp_ref, xn_ref, sc_ref, sh_ref, wf_ref, cos_ref, sa_ref, sb_ref,
                   cw_ref, gw_ref, gb_ref,
                   q_ref, k_ref, ksw_ref, v_ref, vsw_ref, y_ref, w_ref, *, ts):
    t = pl.program_id(1)
    nt = pl.num_programs(1)

    @pl.when((pl.program_id(0) == 0) & (t == 0))
    def _():
        w_ref[...] = wf_ref[...].astype(BF16)

    sc = 1.0 + sc_ref[...]
    sh = sh_ref[...]
    hx = (x_ref[...] * sc + sh).astype(BF16)

    pq = _bdot(hx, w_ref[:, 0:QKV_WIDTH])
    cos = cos_ref[...]
    sa = sa_ref[...]
    sb = sb_ref[...]

    def rope(z):
        return (z * cos + pltpu.roll(z, ROPE_HALF_PAIR, 1) * sa
                + pltpu.roll(z, LANES - ROPE_HALF_PAIR, 1) * sb)

    for j in range(ATTN_WIDTH // LANES):
        sl = slice(j * LANES, (j + 1) * LANES)
        q_ref[:, sl] = (rope(pq[:, sl]) * ATTN_SCALE).astype(BF16)
    kr = rope(pq[:, ATTN_WIDTH:ATTN_WIDTH + KV_WIDTH])
    k_ref[...] = kr.astype(BF16)
    ksw_ref[...] = pltpu.roll(kr, HEAD_DIM, 1).astype(BF16)
    vv = pq[:, ATTN_WIDTH + KV_WIDTH:QKV_WIDTH]
    v_ref[...] = vv.astype(BF16)
    vsw_ref[...] = pltpu.roll(vv, HEAD_DIM, 1).astype(BF16)

    pm = _bdot(hx, w_ref[:, QKV_WIDTH:IN_WIDTH])
    cb = pm[:, 0:CONV_WIDTH]
    u = pm[:, CONV_WIDTH:2 * CONV_WIDTH] * pm[:, 2 * CONV_WIDTH:3 * CONV_WIDTH]

    halo = jnp.concatenate([xp_ref[...], xn_ref[...]], axis=0)
    hh = (halo * sc + sh).astype(BF16)
    ph = _bdot(hh, w_ref[:, QKV_WIDTH + CONV_WIDTH:QKV_WIDTH + 3 * CONV_WIDTH])
    uh = ph[:, 0:CONV_WIDTH] * ph[:, CONV_WIDTH:2 * CONV_WIDTH]
    up_row = jnp.where(t > 0, uh[SUBLANES - 1:SUBLANES, :], 0.0)
    dn_row = jnp.where(t < nt - 1, uh[SUBLANES:SUBLANES + 1, :], 0.0)
    ridx = lax.broadcasted_iota(jnp.int32, (ts, CONV_WIDTH), 0)
    u_up = jnp.where(ridx == 0, up_row, pltpu.roll(u, 1, 0))
    u_dn = jnp.where(ridx == ts - 1, dn_row, pltpu.roll(u, ts - 1, 0))
    cw = cw_ref[...]
    y_conv = cb * (u_up * cw[0:1, :] + u * cw[1:2, :] + u_dn * cw[2:3, :])
    y_ref[:, 0:CONV_WIDTH] = y_conv.astype(BF16)

    gu = jax.nn.gelu(pm[:, 3 * CONV_WIDTH:3 * CONV_WIDTH + GM_WIDTH])
    gv = _layer_norm(jax.nn.gelu(pm[:, 3 * CONV_WIDTH + GM_WIDTH:3 * CONV_WIDTH + 2 * GM_WIDTH])).astype(BF16)
    lane = lax.broadcasted_iota(jnp.int32, (CHUNK, LANES), 1)
    zero = jnp.zeros((CHUNK, LANES), BF16)
    for c in range(ts // CHUNK):
        rows = slice(c * CHUNK, (c + 1) * CHUNK)
        for j in range(GM_WIDTH // LANES):
            cols = slice(j * LANES, (j + 1) * LANES)
            vp = gv[rows, cols]
            s = (_bdot(gw_ref[2 * j], jnp.where(lane < GM_HEAD, vp, zero))
                 + _bdot(gw_ref[2 * j + 1], jnp.where(lane >= GM_HEAD, vp, zero))
                 + gb_ref[:, cols])
            y_ref[rows, CONV_WIDTH + j * LANES:CONV_WIDTH + (j + 1) * LANES] = (gu[rows, cols] * s).astype(BF16)


def _inproj_call(x2d, row_off, n_seq, seq_len, ts, sc, sh, w_in, layer, cos, sa, sb, conv_w, gm_w, gm_b):
    nt = seq_len // ts
    off_t = row_off // ts
    off_8 = row_off // SUBLANES
    last_8 = x2d.shape[0] // SUBLANES - 1
    per_tile_8 = ts // SUBLANES
    n_out = n_seq * seq_len
    d = D_MODEL

    def tile_idx(b, t):
        return (off_t + b * nt + t, 0)

    def prev_idx(b, t):
        return (jnp.maximum(off_8 + (b * nt + t) * per_tile_8 - 1, 0), 0)

    def next_idx(b, t):
        return (jnp.minimum(off_8 + (b * nt + t + 1) * per_tile_8, last_8), 0)

    def out_idx(b, t):
        return (b * nt + t, 0)

    vec = pl.BlockSpec((None, 1, d), lambda b, t: (b, 0, 0))
    rope_spec = pl.BlockSpec((ts, LANES), lambda b, t: (t, 0))
    kv_spec = pl.BlockSpec((ts, KV_WIDTH), out_idx)
    kv_shape = jax.ShapeDtypeStruct((n_out, KV_WIDTH), BF16)
    return pl.pallas_call(
        functools.partial(_inproj_kernel, ts=ts),
        grid=(n_seq, nt),
        in_specs=[
            pl.BlockSpec((ts, d), tile_idx),
            pl.BlockSpec((SUBLANES, d), prev_idx),
            pl.BlockSpec((SUBLANES, d), next_idx),
            vec, vec,
            pl.BlockSpec((None, d, IN_WIDTH), lambda b, t: (layer, 0, 0)),
            rope_spec, rope_spec, rope_spec,
            pl.BlockSpec((3, CONV_WIDTH), lambda b, t: (0, 0)),
            pl.BlockSpec((GM_GROUPS, CHUNK, CHUNK), lambda b, t: (0, 0, 0)),
            pl.BlockSpec((CHUNK, GM_WIDTH), lambda b, t: (0, 0)),
        ],
        out_specs=[
            pl.BlockSpec((ts, ATTN_WIDTH), out_idx),
            kv_spec, kv_spec, kv_spec, kv_spec,
            pl.BlockSpec((ts, LOCAL_WIDTH), out_idx),
        ],
        out_shape=[
            jax.ShapeDtypeStruct((n_out, ATTN_WIDTH), BF16),
            kv_shape, kv_shape, kv_shape, kv_shape,
            jax.ShapeDtypeStruct((n_out, LOCAL_WIDTH), BF16),
        ],
        scratch_shapes=[pltpu.VMEM((d, IN_WIDTH), BF16)],
        compiler_params=_params("arbitrary", "arbitrary"),
        name="in_projection",
    )(x2d, x2d, x2d, sc, sh, w_in, cos, sa, sb, conv_w, gm_w, gm_b)


def _attn_body(q, keys, keys_sw, vals, vals_sw, bias, sink_ref, o_ref, tq):
    nk = keys.shape[0]
    lane = lax.broadcasted_iota(jnp.int32, (nk, LANES), 1)
    lo = lane < HEAD_DIM
    zero = jnp.zeros((nk, LANES), BF16)
    rid = lax.broadcasted_iota(jnp.int32, (2 * tq, 1), 0)
    nt_dims = (((1,), (1,)), ((), ()))
    gqa = N_HEADS // N_KV_HEADS
    for h in range(N_KV_HEADS):
        k_own, k_oth = (keys, keys_sw) if h == 0 else (keys_sw, keys)
        v_own, v_oth = (vals, vals_sw) if h == 0 else (vals_sw, vals)
        kz = (jnp.where(lo, k_own, zero), jnp.where(lo, zero, k_oth))
        vz = (jnp.where(lo, v_own, zero), jnp.where(lo, zero, v_oth))
        c0 = h * gqa * HEAD_DIM
        qs = jnp.concatenate([q[:, c0:c0 + LANES], q[:, c0 + LANES:c0 + 2 * LANES]], axis=0)
        acc = None
        for par in range(2):
            s = lax.dot_general(qs, kz[par], nt_dims, preferred_element_type=F32)
            if bias is not None:
                s = s + bias
            sink = jnp.where(rid < tq, sink_ref[h * gqa + par], sink_ref[h * gqa + 2 + par])
            m = jnp.maximum(jnp.max(s, axis=-1, keepdims=True), sink)
            p = jnp.exp(s - m)
            den = jnp.sum(p, axis=-1, keepdims=True) + jnp.exp(sink - m)
            o = _bdot(p.astype(BF16), vz[par]) / den
            acc = o if acc is None else acc + o
        o_ref[:, c0:c0 + LANES] = acc[0:tq].astype(BF16)
        o_ref[:, c0 + LANES:c0 + 2 * LANES] = acc[tq:2 * tq].astype(BF16)


def _win_attn_kernel(sink_ref, q_ref, kp_ref, kc_ref, kn_ref, kx_ref,
                     ksp_ref, ksc_ref, ksn_ref, ksx_ref,
                     vp_ref, vc_ref, vn_ref, vx_ref,
                     vsp_ref, vsc_ref, vsn_ref, vsx_ref, o_ref, *, tq, ctx_len):
    n = pl.program_id(1)
    nb = pl.num_programs(1)
    row = lax.broadcasted_iota(jnp.int32, (tq, tq), 0)
    col = lax.broadcasted_iota(jnp.int32, (tq, tq), 1)
    b_prev = jnp.where((col >= row) & (n > 0), 0.0, NEG_INF)
    b_next = jnp.where((col <= row) & (n < nb - 1), 0.0, NEG_INF)
    b_half = jnp.concatenate([b_prev, jnp.zeros((tq, tq), F32), b_next, jnp.zeros((tq, ctx_len), F32)], axis=1)
    bias = jnp.concatenate([b_half, b_half], axis=0)

    def cat(a, b, c, d):
        return jnp.concatenate([a[...], b[...], c[...], d[...]], axis=0)

    _attn_body(q_ref[...], cat(kp_ref, kc_ref, kn_ref, kx_ref), cat(ksp_ref, ksc_ref, ksn_ref, ksx_ref),
               cat(vp_ref, vc_ref, vn_ref, vx_ref), cat(vsp_ref, vsc_ref, vsn_ref, vsx_ref),
               bias, sink_ref, o_ref, tq)


def _win_attn_call(sink, q, k, ksw, v, vsw, kx, kxsw, vx, vxsw, n_seq, seq_len, ctx_len):
    tq = WINDOW
    nb = seq_len // tq

    def cur(b, n):
        return (b * nb + n, 0)

    def prev(b, n):
        return (b * nb + jnp.maximum(n - 1, 0), 0)

    def nxt(b, n):
        return (b * nb + jnp.minimum(n + 1, nb - 1), 0)

    def ctx(b, n):
        return (b, 0)

    def kv_specs():
        return [pl.BlockSpec((tq, KV_WIDTH), prev), pl.BlockSpec((tq, KV_WIDTH), cur),
                pl.BlockSpec((tq, KV_WIDTH), nxt), pl.BlockSpec((ctx_len, KV_WIDTH), ctx)]

    return pl.pallas_call(
        functools.partial(_win_attn_kernel, tq=tq, ctx_len=ctx_len),
        grid=(n_seq, nb),
        in_specs=[pl.BlockSpec(memory_space=pltpu.SMEM), pl.BlockSpec((tq, ATTN_WIDTH), cur)]
        + kv_specs() + kv_specs() + kv_specs() + kv_specs(),
        out_specs=pl.BlockSpec((tq, ATTN_WIDTH), cur),
        out_shape=jax.ShapeDtypeStruct((n_seq * seq_len, ATTN_WIDTH), BF16),
        compiler_params=_params("parallel", "parallel"),
        name="window_attention",
    )(sink, q, k, k, k, kx, ksw, ksw, ksw, kxsw, v, v, v, vx, vsw, vsw, vsw, vxsw)


def _ctx_attn_kernel(sink_ref, q_ref, k_ref, ks_ref, v_ref, vs_ref, o_ref, *, tq):
    _attn_body(q_ref[...], k_ref[...], ks_ref[...], v_ref[...], vs_ref[...], None, sink_ref, o_ref, tq)


def _ctx_attn_call(sink, q, k, ksw, v, vsw, n_seq, ctx_len):
    def blk(w):
        return pl.BlockSpec((ctx_len, w), lambda b: (b, 0))

    return pl.pallas_call(
        functools.partial(_ctx_attn_kernel, tq=ctx_len),
        grid=(n_seq,),
        in_specs=[pl.BlockSpec(memory_space=pltpu.SMEM), blk(ATTN_WIDTH),
                  blk(KV_WIDTH), blk(KV_WIDTH), blk(KV_WIDTH), blk(KV_WIDTH)],
        out_specs=blk(ATTN_WIDTH),
        out_shape=jax.ShapeDtypeStruct((n_seq * ctx_len, ATTN_WIDTH), BF16),
        compiler_params=_params("parallel"),
        name="context_attention",
    )(sink, q, k, ksw, v, vsw)


OUT_TILE = 512


def _outproj_kernel(attx_ref, attc_ref, yx_ref, yc_ref, xx_ref, xc_ref, wof_ref, g_ref, lng_ref, lnb_ref,
                    sc_ref, sh_ref, wr_ref, br_ref, x1_ref, h2v_ref, route_ref, routet_ref, cnt_ref, wo_ref,
                    *, alpha, n_lat_tiles):
    @pl.when(pl.program_id(0) == 0)
    def _():
        wo_ref[...] = wof_ref[...].astype(BF16)

    is_lat = pl.program_id(0) < n_lat_tiles
    att = jnp.where(is_lat, attx_ref[...], attc_ref[...])
    yloc = jnp.where(is_lat, yx_ref[...], yc_ref[...])
    xin = jnp.where(is_lat, xx_ref[...], xc_ref[...])
    o = _bdot(att, wo_ref[0:ATTN_WIDTH, :]) + _bdot(yloc, wo_ref[ATTN_WIDTH:, :])
    x1 = _layer_norm(alpha * xin + g_ref[...] * o) * lng_ref[...] + lnb_ref[...]
    x1_ref[...] = x1
    h2 = x1 * (1.0 + sc_ref[...]) + sh_ref[...]
    for c in range(ROW_TILE_ROWS):
        h2v_ref[:, c, :, :] = h2[:, c * LANES:(c + 1) * LANES].reshape(h2.shape[0] // SUBLANES, SUBLANES, LANES)

    hh, hl = _split_bf16(h2)
    t1 = _bdot(hh, wr_ref[...])
    lg = t1[:, 0:ROUTER_COLS] + t1[:, ROUTER_COLS:] + _bdot(hl, wr_ref[:, 0:ROUTER_COLS]) + br_ref[...]

    ts = lg.shape[0]
    lane = lax.broadcasted_iota(jnp.int32, (ts, ROUTER_COLS), 1).astype(F32)
    big = jnp.float32(ROUTER_COLS)

    def top1(v):
        m = jnp.max(v, axis=-1, keepdims=True)
        return m, jnp.min(jnp.where(v == m, lane, big), axis=-1, keepdims=True)

    gl = jnp.where(lane < N_GROUPS, lg, NEG_INF)
    g_val, g_idx = top1(gl)
    lse = g_val + jnp.log(jnp.sum(jnp.exp(gl - g_val), axis=-1, keepdims=True))
    p_group = jnp.exp(g_val - lse)
    e_lo = N_GROUPS + EXP_PER_GROUP * g_idx
    el = jnp.where((lane >= e_lo) & (lane < e_lo + EXP_PER_GROUP), lg, NEG_INF)
    e1, l1 = top1(el)
    e2, l2 = top1(jnp.where(lane == l1, NEG_INF, el))
    z = jnp.exp(e2 - e1)
    gate1 = p_group / (1.0 + z)
    gate2 = p_group * z / (1.0 + z)
    x1id = l1 - N_GROUPS
    x2id = l2 - N_GROUPS

    sel1 = lane == x1id
    sel2 = lane == x2id
    onehot = jnp.where(sel1 | sel2, 1.0, 0.0)
    r_i = lax.broadcasted_iota(jnp.int32, (ts, ts), 0)
    c_i = lax.broadcasted_iota(jnp.int32, (ts, ts), 1)
    before = jnp.where(c_i < r_i, 1.0, 0.0).astype(BF16)
    prefix = _bdot(before, onehot.astype(BF16))
    rank1 = jnp.sum(jnp.where(sel1, prefix, 0.0), axis=-1, keepdims=True)
    rank2 = jnp.sum(jnp.where(sel2, prefix, 0.0), axis=-1, keepdims=True)
    route = jnp.zeros((ts, ROUTER_COLS), F32)
    for col, val in enumerate((x1id, x2id, gate1, gate2, rank1, rank2)):
        route = jnp.where(lane == col, val, route)
    route_ref[...] = route
    routet_ref[...] = route.T[0:SUBLANES, :]
    cnt_ref[...] =jnp.broadcast_to(jnp.sum(onehot, axis=0, keepdims=True), (SUBLANES, ROUTER_COLS))


def _outproj_call(att_x, att_c, y_x, y_c, x_arr, x_off, c_arr, c_off, n_lat, n_ctx, seq_len,
                  w_out, layer, gvecs, lng, lnb, scvecs, shvecs, wr, br, alpha):
    ts = OUT_TILE
    d = D_MODEL
    n_lat_tiles = n_lat // ts
    n_tiles = (n_lat + n_ctx) // ts
    tiles_per_seq = seq_len // ts
    n_seq = n_lat // seq_len
    xo, co = x_off // ts, c_off // ts

    def lat_loc(i):
        return (jnp.minimum(i, n_lat_tiles - 1), 0)

    def ctx_loc(i):
        return (jnp.maximum(i - n_lat_tiles, 0), 0)

    def lat_in(i):
        return (xo + jnp.minimum(i, n_lat_tiles - 1), 0)

    def ctx_in(i):
        return (co + jnp.maximum(i - n_lat_tiles, 0), 0)

    def vec_idx(i):
        return (jnp.where(i < n_lat_tiles, i // tiles_per_seq, n_seq), 0, 0)

    vecb = pl.BlockSpec((None, 1, d), vec_idx)
    vec0 = pl.BlockSpec((1, d), lambda i: (0, 0))
    return pl.pallas_call(
        functools.partial(_outproj_kernel, alpha=alpha, n_lat_tiles=n_lat_tiles),
        grid=(n_tiles,),
        in_specs=[
            pl.BlockSpec((ts, ATTN_WIDTH), lat_loc), pl.BlockSpec((ts, ATTN_WIDTH), ctx_loc),
            pl.BlockSpec((ts, LOCAL_WIDTH), lat_loc), pl.BlockSpec((ts, LOCAL_WIDTH), ctx_loc),
            pl.BlockSpec((ts, d), lat_in), pl.BlockSpec((ts, d), ctx_in),
            pl.BlockSpec((None, d, d), lambda i: (layer, 0, 0)),
            vecb, vec0, vec0, vecb, vecb,
            pl.BlockSpec((d, 2 * ROUTER_COLS), lambda i: (0, 0)),
            pl.BlockSpec((1, ROUTER_COLS), lambda i: (0, 0)),
        ],
        out_specs=[pl.BlockSpec((ts, d), lambda i: (i, 0)),
                   pl.BlockSpec((ts // SUBLANES, ROW_TILE_ROWS, SUBLANES, LANES), lambda i: (i, 0, 0, 0)),
                   pl.BlockSpec((ts, ROUTER_COLS), lambda i: (i, 0)),
                   pl.BlockSpec((None, SUBLANES, ts), lambda i: (i, 0, 0)),
                   pl.BlockSpec((None, SUBLANES, ROUTER_COLS), lambda i: (i, 0, 0))],
        out_shape=[jax.ShapeDtypeStruct((n_lat + n_ctx, d), F32),
                   jax.ShapeDtypeStruct(((n_lat + n_ctx) // SUBLANES, ROW_TILE_ROWS, SUBLANES, LANES), F32),
                   jax.ShapeDtypeStruct((n_lat + n_ctx, ROUTER_COLS), F32),
                   jax.ShapeDtypeStruct((n_tiles, SUBLANES, ts), F32),
                   jax.ShapeDtypeStruct((n_tiles, SUBLANES, ROUTER_COLS), F32)],
        scratch_shapes=[pltpu.VMEM((d, d), BF16)],
        compiler_params=_params("arbitrary"),
        name="out_projection",
    )(att_x, att_c, y_x, y_c, x_arr, c_arr, w_out, gvecs, lng, lnb, scvecs, shvecs, wr, br)


DISPATCH_TILE = 512


def _dispatch_kernel(seg_ref, pos_ref, h_ref, xs_out, zbuf, zsem, sem):
    @pl.when(pl.program_id(0) == 0)
    def _():
        zbuf[...] = jnp.zeros_like(zbuf)

        def zero_copy(e):
            return pltpu.make_async_copy(zbuf, xs_out.at[pl.ds(seg_ref[0, e] - MOE_BLOCK, MOE_BLOCK)], zsem)

        for e in range(N_EXPERTS):
            @pl.when(seg_ref[1, e] > 0)
            def _():
                zero_copy(e).start()
        for e in range(N_EXPERTS):
            @pl.when(seg_ref[1, e] > 0)
            def _():
                zero_copy(e).wait()

        def tail_copy(b):
            return pltpu.make_async_copy(zbuf, xs_out.at[pl.ds(b * MOE_BLOCK, MOE_BLOCK)], zsem)

        n_blocks = xs_out.shape[0] // MOE_BLOCK
        first_unused = seg_ref[0, N_EXPERTS - 1] // MOE_BLOCK
        lax.fori_loop(first_unused, n_blocks, lambda b, c: (tail_copy(b).start(), c)[1], 0)
        lax.fori_loop(first_unused, n_blocks, lambda b, c: (tail_copy(b).wait(), c)[1], 0)

    def body(j, carry):
        for s in range(SUBLANES):
            t = SUBLANES * j + s
            for k in range(TOP_K):
                pltpu.make_async_copy(h_ref.at[j, :, s, :], xs_out.at[pos_ref[0, 0, k * DISPATCH_TILE + t]],
                                      sem).start(priority=k)
        return carry

    lax.fori_loop(0, DISPATCH_TILE // SUBLANES, body, 0)
    for k in range(TOP_K):
        pltpu.make_async_copy(h_ref, h_ref, sem).wait()


def _dispatch_call(seg, pos, h2t, n_slots):
    n_tok = h2t.shape[0] * SUBLANES
    n_tiles = n_tok // DISPATCH_TILE
    row = (ROW_TILE_ROWS, LANES)
    grid_spec = pltpu.PrefetchScalarGridSpec(
        num_scalar_prefetch=1,
        grid=(n_tiles,),
        in_specs=[
            pl.BlockSpec((1, 1, TOP_K * DISPATCH_TILE), lambda i, seg: (i, 0, 0), memory_space=pltpu.SMEM),
            pl.BlockSpec((DISPATCH_TILE // SUBLANES, ROW_TILE_ROWS, SUBLANES, LANES), lambda i, seg: (i, 0, 0, 0)),
        ],
        out_specs=pl.BlockSpec(memory_space=pl.ANY),
        scratch_shapes=[pltpu.VMEM((MOE_BLOCK,) + row, F32), pltpu.SemaphoreType.DMA(()),
                        pltpu.SemaphoreType.DMA(())],
    )
    return pl.pallas_call(
        _dispatch_kernel,
        grid_spec=grid_spec,
        out_shape=jax.ShapeDtypeStruct((n_slots,) + row, F32),
        compiler_params=_params("arbitrary"),
        name="moe_dispatch",
    )(seg, pos, h2t)


def _moe_kernel(be_ref, nu_ref, xs_hbm, w1_ref, w3_ref, w2_ref, ys_hbm, w1b, w3b, w2b, xbuf, obuf, isem, osem):
    i = pl.program_id(0)
    n_used = nu_ref[0]
    slot = i % 2

    def in_copies(blk, at_slot):
        return _tiles_to_rows(xs_hbm.at[pl.ds(blk * MOE_BLOCK, MOE_BLOCK)], xbuf.at[at_slot], isem.at[at_slot])

    def out_copies(blk, at_slot):
        return _rows_to_tiles(obuf.at[at_slot], ys_hbm.at[pl.ds(blk * MOE_BLOCK, MOE_BLOCK)], osem.at[at_slot])

    @pl.when(i == 0)
    def _():
        _start_all(in_copies(0, 0))

    @pl.when(i + 1 < n_used)
    def _():
        _start_all(in_copies(i + 1, 1 - slot))

    @pl.when((i == 0) | (be_ref[i] != be_ref[jnp.maximum(i - 1, 0)]))
    def _():
        w1b[...] = w1_ref[...].astype(BF16)
        w3b[...] = w3_ref[...].astype(BF16)
        w2b[...] = w2_ref[...].astype(BF16)

    @pl.when(i < n_used)
    def _():
        _wait_all(in_copies(i, slot))
        x = xbuf[slot].astype(BF16)
        obuf[slot] = _bdot((jax.nn.silu(_bdot(x, w1b[...])) * _bdot(x, w3b[...])).astype(BF16), w2b[...])
        _start_all(out_copies(i, slot))

        @pl.when(i >= 1)
        def _():
            _wait_all(out_copies(i - 1, 1 - slot))

        @pl.when(i == n_used - 1)
        def _():
            _wait_all(out_copies(i, slot))

    @pl.when(i >= n_used)
    def _():
        obuf[slot] = jnp.zeros(obuf.shape[1:], F32)
        _start_all(out_copies(i, slot))
        _wait_all(out_copies(i, slot))


def _moe_call(block_e, n_used, xs, w1, w3, w2, layer):
    n_blocks = block_e.shape[0]
    d = D_MODEL
    grid_spec = pltpu.PrefetchScalarGridSpec(
        num_scalar_prefetch=2,
        grid=(n_blocks,),
        in_specs=[
            pl.BlockSpec(memory_space=pl.ANY),
            pl.BlockSpec((None, None, d, D_EXPERT), lambda i, be, nu: (layer, be[i], 0, 0)),
            pl.BlockSpec((None, None, d, D_EXPERT), lambda i, be, nu: (layer, be[i], 0, 0)),
            pl.BlockSpec((None, None, D_EXPERT, d), lambda i, be, nu: (layer, be[i], 0, 0)),
        ],
        out_specs=pl.BlockSpec(memory_space=pl.ANY),
        scratch_shapes=[pltpu.VMEM((d, D_EXPERT), BF16), pltpu.VMEM((d, D_EXPERT), BF16),
                        pltpu.VMEM((D_EXPERT, d), BF16),
                        pltpu.VMEM((2, MOE_BLOCK, d), F32), pltpu.VMEM((2, MOE_BLOCK, d), F32),
                        pltpu.SemaphoreType.DMA((2,)), pltpu.SemaphoreType.DMA((2,))],
    )
    return pl.pallas_call(
        _moe_kernel,
        grid_spec=grid_spec,
        out_shape=jax.ShapeDtypeStruct((n_blocks * MOE_BLOCK, ROW_TILE_ROWS, LANES), F32),
        compiler_params=_params("arbitrary"),
        name="moe_experts",
    )(block_e, n_used, xs, w1, w3, w2)


COMBINE_TILE = 256


def _combine_kernel(pos0_ref, posn_ref, x1_ref, route_ref, g_ref, lng_ref, lnb_ref, ys_hbm, o_ref, ybuf, sem,
                    *, alpha):
    i = pl.program_id(0)
    nb = pl.num_programs(0)
    n_rows = TOP_K * COMBINE_TILE

    def issue(pos_ref, slot):
        def body(j, carry):
            for s in range(SUBLANES):
                pltpu.make_async_copy(ys_hbm.at[pos_ref[0, 0, SUBLANES * j + s]], ybuf.at[slot, j, :, s, :],
                                      sem.at[slot]).start(priority=s % N_DMA_PRIORITIES)
            return carry
        lax.fori_loop(0, n_rows // SUBLANES, body, 0)

    @pl.when(i == 0)
    def _():
        issue(pos0_ref, 0)

    @pl.when(i + 1 < nb)
    def _():
        issue(posn_ref, (i + 1) % 2)

    slot = i % 2
    pltpu.make_async_copy(ybuf.at[slot], ybuf.at[slot], sem.at[slot]).wait()
    yrows = jnp.concatenate([ybuf[slot, :, c, :, :].reshape(n_rows, LANES) for c in range(ROW_TILE_ROWS)], axis=1)
    route = route_ref[...]
    y = (yrows[0:COMBINE_TILE, :] * route[:, ROUTE_GATE:ROUTE_GATE + 1]
         + yrows[COMBINE_TILE:, :] * route[:, ROUTE_GATE + 1:ROUTE_GATE + 2])
    o_ref[...] = _layer_norm(alpha * x1_ref[...] + g_ref[...] * y) * lng_ref[...] + lnb_ref[...]


def _combine_call(pos, x1, route, gvecs, lng, lnb, ys, n_tok, tiles_per_seq, n_seq, alpha):
    d = D_MODEL
    n_tiles = n_tok // COMBINE_TILE
    n_lat_tiles = tiles_per_seq * n_seq

    def g_idx(i):
        return (jnp.where(i < n_lat_tiles, i // tiles_per_seq, n_seq), 0, 0)

    smem_blk = (1, 1, TOP_K * COMBINE_TILE)
    return pl.pallas_call(
        functools.partial(_combine_kernel, alpha=alpha),
        grid=(n_tiles,),
        in_specs=[
            pl.BlockSpec(smem_blk, lambda i: (0, 0, 0), memory_space=pltpu.SMEM),
            pl.BlockSpec(smem_blk, lambda i: (jnp.minimum(i + 1, n_tiles - 1), 0, 0), memory_space=pltpu.SMEM),
            pl.BlockSpec((COMBINE_TILE, d), lambda i: (i, 0)),
            pl.BlockSpec((COMBINE_TILE, ROUTER_COLS), lambda i: (i, 0)),
            pl.BlockSpec((None, 1, d), g_idx),
            pl.BlockSpec((1, d), lambda i: (0, 0)),
            pl.BlockSpec((1, d), lambda i: (0, 0)),
            pl.BlockSpec(memory_space=pl.ANY),
        ],
        out_specs=pl.BlockSpec((COMBINE_TILE, d), lambda i: (i, 0)),
        out_shape=jax.ShapeDtypeStruct((n_tok, d), F32),
        scratch_shapes=[pltpu.VMEM((2, TOP_K * COMBINE_TILE // SUBLANES, ROW_TILE_ROWS, SUBLANES, LANES), F32),
                        pltpu.SemaphoreType.DMA((2,))],
        compiler_params=_params("arbitrary"),
        name="moe_combine",
    )(pos, pos, x1, route, gvecs, lng, lnb, ys)


def _slots(route_t, tile_counts):
    n_tiles = tile_counts.shape[0]
    n = n_tiles * route_t.shape[2]
    tc = tile_counts.astype(jnp.int32)
    counts = jnp.sum(tc, axis=0)
    padded = (counts + MOE_BLOCK - 1) // MOE_BLOCK * MOE_BLOCK
    pends = jnp.cumsum(padded)
    base = (pends - padded)[None, :] + jnp.cumsum(tc, axis=0) - tc
    n_blocks = -(-n * TOP_K // MOE_BLOCK) + N_EXPERTS
    block_start = jnp.arange(n_blocks, dtype=jnp.int32) * MOE_BLOCK
    block_e = jnp.minimum(jnp.sum(pends[None, 0:N_EXPERTS] <= block_start[:, None], axis=1),
                          N_EXPERTS - 1).astype(jnp.int32)
    n_used = (pends[N_EXPERTS - 1] // MOE_BLOCK).astype(jnp.int32).reshape(1)
    seg = jnp.stack([pends[0:N_EXPERTS], padded[0:N_EXPERTS]]).astype(jnp.int32)
    pos = []
    for k in range(TOP_K):
        e = route_t[:, ROUTE_EXPERT + k, :].astype(jnp.int32)
        b = jnp.zeros_like(e)
        for x in range(N_EXPERTS):
            b = jnp.where(e == x, base[:, x:x + 1], b)
        pos.append((b + route_t[:, ROUTE_RANK + k, :].astype(jnp.int32)).reshape(n))
    return block_e, n_used, seg, pos


def _rope_tables(seq_len):
    m = HEAD_DIM // 4
    freqs = ROPE_BASE ** (-jnp.arange(m, dtype=F32) / m)
    t = jnp.arange(seq_len)
    row = (t // GRID_W).astype(F32)[:, None] * freqs[None, :]
    col = (t % GRID_W).astype(F32)[:, None] * freqs[None, :]
    cos = jnp.concatenate([jnp.cos(row), jnp.cos(row), jnp.cos(col), jnp.cos(col)], axis=-1)
    zero = jnp.zeros_like(row)
    sin_a = jnp.concatenate([zero, jnp.sin(row), zero, jnp.sin(col)], axis=-1)
    sin_b = jnp.concatenate([-jnp.sin(row), zero, -jnp.sin(col), zero], axis=-1)
    rep = LANES // HEAD_DIM
    return jnp.tile(cos, (1, rep)), jnp.tile(sin_a, (1, rep)), jnp.tile(sin_b, (1, rep))


def kernel(x, c, ctx, c_ctx, w_ada, b_ada, w_in, conv_w, attn_sink, gm_ws, gm_bs, w_out, ln1_g, ln1_b,
           w_rg, b_rg, w_re, b_re, w1, w3, w2, ln2_g, ln2_b):
    b_, s_, d_ = x.shape
    c_len = ctx.shape[1]
    depth = w_ada.shape[0]
    alpha = (2 * depth) ** 0.25
    n_lat = b_ * s_
    n_ctx = b_ * c_len
    ts = 1024

    cin = jnp.zeros((ADA_ROWS, d_), F32).at[0:b_].set(c).at[b_].set(c_ctx)
    mod = _ada_call(cin, w_ada, b_ada)

    cos, sin_a, sin_b = _rope_tables(s_)
    ones_c = jnp.ones((c_len, LANES), F32)
    zeros_c = jnp.zeros((c_len, LANES), F32)

    x_flat = x.reshape(n_lat, d_)
    c_flat = ctx.reshape(n_ctx, d_)
    x_off, c_off = 0, 0
    x_arr, c_arr = x_flat, c_flat

    for l in range(depth):
        last = l == depth - 1
        mx = mod[l, 0:b_].reshape(b_, 6, 1, d_)
        sh1, sc1, g1, sh2, sc2, g2 = (mx[:, i] for i in range(6))
        mc = jnp.broadcast_to(mod[l, b_].reshape(1, 6, 1, d_), (b_, 6, 1, d_))
        gm_w = gm_ws[l].astype(BF16)
        gm_b = jnp.repeat(gm_bs[l].T, GM_HEAD, axis=1)
        sink = attn_sink[l]

        qx, kx, kxs, vx, vxs, yx = _inproj_call(x_arr, x_off, b_, s_, ts, sc1, sh1, w_in, l, cos, sin_a, sin_b,
                                                conv_w[l], gm_w, gm_b)
        qc, kc, kcs, vc, vcs, yc = _inproj_call(c_arr, c_off, b_, c_len, c_len, mc[:, 1], mc[:, 0], w_in, l,
                                                ones_c, zeros_c, zeros_c, conv_w[l], gm_w, gm_b)
        att_x = _win_attn_call(sink, qx, kx, kxs, vx, vxs, kc, kcs, vc, vcs, b_, s_, c_len)

        w_r = jnp.zeros((d_, ROUTER_COLS), F32).at[:, 0:N_GROUPS].set(w_rg[l]) \
            .at[:, N_GROUPS:N_GROUPS + N_EXPERTS].set(w_re[l])
        w_r_hi = w_r.astype(BF16)
        w_r_lo = (w_r - w_r_hi.astype(F32)).astype(BF16)
        wr = jnp.concatenate([w_r_hi, w_r_lo], axis=1)
        br = jnp.zeros((1, ROUTER_COLS), F32).at[0, 0:N_GROUPS].set(b_rg[l]) \
            .at[0, N_GROUPS:N_GROUPS + N_EXPERTS].set(b_re[l])
        lng1, lnb1 = ln1_g[l].reshape(1, d_), ln1_b[l].reshape(1, d_)
        lng2, lnb2 = ln2_g[l].reshape(1, d_), ln2_b[l].reshape(1, d_)

        n_tot = n_lat if last else n_lat + n_ctx
        gvecs1 = jnp.concatenate([g1, mc[0:1, 2]], axis=0)
        scvecs2 = jnp.concatenate([sc2, mc[0:1, 4]], axis=0)
        shvecs2 = jnp.concatenate([sh2, mc[0:1, 3]], axis=0)
        if last:
            x1, h2t, route, route_t, tcnt = _outproj_call(att_x, att_x, yx, yx, x_arr, x_off, x_arr, x_off, n_lat, 0, s_,
                                                 w_out, l, gvecs1, lng1, lnb1, scvecs2, shvecs2, wr, br, alpha)
        else:
            att_c = _ctx_attn_call(sink, qc, kc, kcs, vc, vcs, b_, c_len)
            x1, h2t, route, route_t, tcnt = _outproj_call(att_x, att_c, yx, yc, x_arr, x_off, c_arr, c_off, n_lat, n_ctx,
                                                 s_, w_out, l, gvecs1, lng1, lnb1, scvecs2, shvecs2, wr, br, alpha)

        block_e, n_used, seg, pos = _slots(route_t, tcnt[:, 0, :])
        n_slots = block_e.shape[0] * MOE_BLOCK

        def per_tile(tile):
            return jnp.concatenate([p.reshape(n_tot // tile, 1, tile) for p in pos], axis=2)

        xs = _dispatch_call(seg, per_tile(DISPATCH_TILE), h2t, n_slots)
        ys = _moe_call(block_e, n_used, xs, w1, w3, w2, l)

        gvecs = jnp.concatenate([g2, mc[0:1, 5]], axis=0)
        out = _combine_call(per_tile(COMBINE_TILE), x1, route, gvecs, lng2, lnb2, ys, n_tot, s_ // COMBINE_TILE,
                            b_, alpha)
        x_arr, x_off = out, 0
        c_arr, c_off = out, n_lat

    return x_arr.reshape(b_, s_, d_)
```

```python
import functools

import jax
import jax.numpy as jnp
from jax import lax
from jax.experimental import pallas as pl
from jax.experimental.pallas import tpu as pltpu

F32 = jnp.float32
BF16 = jnp.bfloat16

D_MODEL = 1024
GRID_W = 64
HEAD_DIM = 64
N_HEADS = 8
N_KV_HEADS = 2
ATTN_WIDTH = N_HEADS * HEAD_DIM
KV_WIDTH = N_KV_HEADS * HEAD_DIM
WINDOW = 128
ATTN_SCALE = HEAD_DIM ** -0.5
LOG2E = 1.4426950408889634
ROPE_BASE = 10000.0
CONV_WIDTH = D_MODEL // 4
GM_WIDTH = D_MODEL // 4
GM_GROUPS = 4
GM_HEAD = GM_WIDTH // GM_GROUPS
CHUNK = 128
LOCAL_WIDTH = CONV_WIDTH + GM_WIDTH
IN_WIDTH = ATTN_WIDTH + 2 * KV_WIDTH + 3 * CONV_WIDTH + 2 * GM_WIDTH
QKV_WIDTH = ATTN_WIDTH + 2 * KV_WIDTH
N_GROUPS = 4
EXP_PER_GROUP = 8
N_EXPERTS = N_GROUPS * EXP_PER_GROUP
TOP_K = 2
D_EXPERT = D_MODEL // 2
MOE_BLOCK = 512
LN_EPS = 1e-6
NEG_INF = -1e30

LANES = 128
SUBLANES = 8
VMEM_LIMIT_BYTES = 48 * 1024 * 1024
N_DMA_PRIORITIES = 2

ROW_TILE_ROWS = D_MODEL // LANES
assert ROW_TILE_ROWS == SUBLANES

ROUTER_COLS = LANES
ROUTE_EXPERT = 0
ROUTE_GATE = TOP_K
ROUTE_RANK = 2 * TOP_K
ROPE_HALF_PAIR = HEAD_DIM // 4


def _bdot(a, b):
    return jnp.dot(a, b, preferred_element_type=F32)


def _split_bf16(a):
    hi = a.astype(BF16)
    lo = (a - hi.astype(F32)).astype(BF16)
    return hi, lo


def _layer_norm(r):
    mu = jnp.mean(r, axis=-1, keepdims=True)
    d = r - mu
    var = jnp.mean(d * d, axis=-1, keepdims=True)
    return d * lax.rsqrt(var + LN_EPS)


def _params(*sem, flags=None):
    return pltpu.CompilerParams(dimension_semantics=sem, vmem_limit_bytes=VMEM_LIMIT_BYTES, flags=flags)


def _rows_to_tiles(rows_ref, tiles_ref, sem):
    return [pltpu.make_async_copy(rows_ref.at[:, pl.ds(s * LANES, LANES)], tiles_ref.at[:, s, :], sem)
            for s in range(ROW_TILE_ROWS)]


def _tiles_to_rows(tiles_ref, rows_ref, sem):
    return [pltpu.make_async_copy(tiles_ref.at[:, s, :], rows_ref.at[:, pl.ds(s * LANES, LANES)], sem)
            for s in range(ROW_TILE_ROWS)]


def _start_all(copies):
    for cp in copies:
        cp.start()


def _wait_all(copies):
    for cp in copies:
        cp.wait()


ADA_ROWS = 16
ADA_TILE = 1536


def _ada_kernel(c_ref, w_ref, b_ref, o_ref):
    act = jax.nn.silu(c_ref[...])
    ah, al = _split_bf16(act)
    wh, wl = _split_bf16(w_ref[...])
    o_ref[...] = _bdot(ah, wh) + _bdot(ah, wl) + _bdot(al, wh) + b_ref[...]


def _ada_call(cin, w_ada, b_ada):
    depth, d, n = w_ada.shape
    return pl.pallas_call(
        _ada_kernel,
        grid=(depth, n // ADA_TILE),
        in_specs=[
            pl.BlockSpec((ADA_ROWS, d), lambda l, j: (0, 0)),
            pl.BlockSpec((None, d, ADA_TILE), lambda l, j: (l, 0, j)),
            pl.BlockSpec((None, 1, ADA_TILE), lambda l, j: (l, 0, j)),
        ],
        out_specs=pl.BlockSpec((None, ADA_ROWS, ADA_TILE), lambda l, j: (l, 0, j)),
        out_shape=jax.ShapeDtypeStruct((depth, ADA_ROWS, n), F32),
        compiler_params=_params("parallel", "parallel"),
        name="ada_modulation",
    )(cin, w_ada, b_ada.reshape(depth, 1, n))


def _inproj_kernel(x_ref, xp_ref, xn_ref, sc_ref, sh_ref, wf_ref, cos_ref, sa_ref, sb_ref,
                   cw_ref, gw_ref, gb_ref,
                   q_ref, k_ref, ksw_ref, v_ref, vsw_ref, y_ref, w_ref, *, ts):
    t = pl.program_id(1)
    nt = pl.num_programs(1)

    @pl.when((pl.program_id(0) == 0) & (t == 0))
    def _():
        w_ref[...] = wf_ref[...].astype(BF16)

    sc = 1.0 + sc_ref[...]
    sh = sh_ref[...]
    hx = (x_ref[...] * sc + sh).astype(BF16)

    pq = _bdot(hx, w_ref[:, 0:QKV_WIDTH])
    cos = cos_ref[...]
    sa = sa_ref[...]
    sb = sb_ref[...]

    def rope(z):
        return (z * cos + pltpu.roll(z, ROPE_HALF_PAIR, 1) * sa
                + pltpu.roll(z, LANES - ROPE_HALF_PAIR, 1) * sb)

    for j in range(ATTN_WIDTH // LANES):
        sl = slice(j * LANES, (j + 1) * LANES)
        q_ref[:, sl] = (rope(pq[:, sl]) * (ATTN_SCALE * LOG2E)).astype(BF16)
    kr = rope(pq[:, ATTN_WIDTH:ATTN_WIDTH + KV_WIDTH])
    k_ref[...] = kr.astype(BF16)
    ksw_ref[...] = pltpu.roll(kr, HEAD_DIM, 1).astype(BF16)
    vv = pq[:, ATTN_WIDTH + KV_WIDTH:QKV_WIDTH]
    v_ref[...] = vv.astype(BF16)
    vsw_ref[...] = pltpu.roll(vv, HEAD_DIM, 1).astype(BF16)

    pm = _bdot(hx, w_ref[:, QKV_WIDTH:IN_WIDTH])
    cb = pm[:, 0:CONV_WIDTH]
    u = pm[:, CONV_WIDTH:2 * CONV_WIDTH] * pm[:, 2 * CONV_WIDTH:3 * CONV_WIDTH]

    halo = jnp.concatenate([xp_ref[...], xn_ref[...]], axis=0)
    hh = (halo * sc + sh).astype(BF16)
    ph = _bdot(hh, w_ref[:, QKV_WIDTH + CONV_WIDTH:QKV_WIDTH + 3 * CONV_WIDTH])
    uh = ph[:, 0:CONV_WIDTH] * ph[:, CONV_WIDTH:2 * CONV_WIDTH]
    up_row = jnp.where(t > 0, uh[SUBLANES - 1:SUBLANES, :], 0.0)
    dn_row = jnp.where(t < nt - 1, uh[SUBLANES:SUBLANES + 1, :], 0.0)
    ridx = lax.broadcasted_iota(jnp.int32, (ts, CONV_WIDTH), 0)
    u_up = jnp.where(ridx == 0, up_row, pltpu.roll(u, 1, 0))
    u_dn = jnp.where(ridx == ts - 1, dn_row, pltpu.roll(u, ts - 1, 0))
    cw = cw_ref[...]
    y_conv = cb * (u_up * cw[0:1, :] + u * cw[1:2, :] + u_dn * cw[2:3, :])
    y_ref[:, 0:CONV_WIDTH] = y_conv.astype(BF16)

    gu = jax.nn.gelu(pm[:, 3 * CONV_WIDTH:3 * CONV_WIDTH + GM_WIDTH])
    gv = _layer_norm(jax.nn.gelu(pm[:, 3 * CONV_WIDTH + GM_WIDTH:3 * CONV_WIDTH + 2 * GM_WIDTH])).astype(BF16)
    lane = lax.broadcasted_iota(jnp.int32, (CHUNK, LANES), 1)
    zero = jnp.zeros((CHUNK, LANES), BF16)
    for c in range(ts // CHUNK):
        rows = slice(c * CHUNK, (c + 1) * CHUNK)
        for j in range(GM_WIDTH // LANES):
            cols = slice(j * LANES, (j + 1) * LANES)
            vp = gv[rows, cols]
            s = (_bdot(gw_ref[2 * j], jnp.where(lane < GM_HEAD, vp, zero))
                 + _bdot(gw_ref[2 * j + 1], jnp.where(lane >= GM_HEAD, vp, zero))
                 + gb_ref[:, cols])
            y_ref[rows, CONV_WIDTH + j * LANES:CONV_WIDTH + (j + 1) * LANES] = (gu[rows, cols] * s).astype(BF16)


def _inproj_call(x2d, row_off, n_seq, seq_len, ts, sc, sh, w_in, layer, cos, sa, sb, conv_w, gm_w, gm_b):
    nt = seq_len // ts
    off_t = row_off // ts
    off_8 = row_off // SUBLANES
    last_8 = x2d.shape[0] // SUBLANES - 1
    per_tile_8 = ts // SUBLANES
    n_out = n_seq * seq_len
    d = D_MODEL

    def tile_idx(b, t):
        return (off_t + b * nt + t, 0)

    def prev_idx(b, t):
        return (jnp.maximum(off_8 + (b * nt + t) * per_tile_8 - 1, 0), 0)

    def next_idx(b, t):
        return (jnp.minimum(off_8 + (b * nt + t + 1) * per_tile_8, last_8), 0)

    def out_idx(b, t):
        return (b * nt + t, 0)

    vec = pl.BlockSpec((None, 1, d), lambda b, t: (b, 0, 0))
    rope_spec = pl.BlockSpec((ts, LANES), lambda b, t: (t, 0))
    kv_spec = pl.BlockSpec((ts, KV_WIDTH), out_idx)
    kv_shape = jax.ShapeDtypeStruct((n_out, KV_WIDTH), BF16)
    return pl.pallas_call(
        functools.partial(_inproj_kernel, ts=ts),
        grid=(n_seq, nt),
        in_specs=[
            pl.BlockSpec((ts, d), tile_idx),
            pl.BlockSpec((SUBLANES, d), prev_idx),
            pl.BlockSpec((SUBLANES, d), next_idx),
            vec, vec,
            pl.BlockSpec((None, d, IN_WIDTH), lambda b, t: (layer, 0, 0)),
            rope_spec, rope_spec, rope_spec,
            pl.BlockSpec((3, CONV_WIDTH), lambda b, t: (0, 0)),
            pl.BlockSpec((GM_GROUPS, CHUNK, CHUNK), lambda b, t: (0, 0, 0)),
            pl.BlockSpec((CHUNK, GM_WIDTH), lambda b, t: (0, 0)),
        ],
        out_specs=[
            pl.BlockSpec((ts, ATTN_WIDTH), out_idx),
            kv_spec, kv_spec, kv_spec, kv_spec,
            pl.BlockSpec((ts, LOCAL_WIDTH), out_idx),
        ],
        out_shape=[
            jax.ShapeDtypeStruct((n_out, ATTN_WIDTH), BF16),
            kv_shape, kv_shape, kv_shape, kv_shape,
            jax.ShapeDtypeStruct((n_out, LOCAL_WIDTH), BF16),
        ],
        scratch_shapes=[pltpu.VMEM((d, IN_WIDTH), BF16)],
        compiler_params=_params("arbitrary", "arbitrary"),
        name="in_projection",
    )(x2d, x2d, x2d, sc, sh, w_in, cos, sa, sb, conv_w, gm_w, gm_b)


def _attn_body(q, keys, keys_sw, vals, vals_sw, halo_bias, sink_ref, o_ref, tq):
    nk = keys.shape[0]
    lane = lax.broadcasted_iota(jnp.int32, (nk, LANES), 1)
    lo = lane < HEAD_DIM
    zero = jnp.zeros((nk, LANES), BF16)
    rid = lax.broadcasted_iota(jnp.int32, (2 * tq, 1), 0)
    out_lo = lax.broadcasted_iota(jnp.int32, (2 * tq, LANES), 1) < HEAD_DIM
    nt_dims = (((1,), (1,)), ((), ()))
    gqa = N_HEADS // N_KV_HEADS
    biases = [None] * (nk // LANES)
    if halo_bias is not None:
        assert tq == LANES
        biases[0], biases[2] = halo_bias
    for h in range(N_KV_HEADS):
        k_own, k_oth = (keys, keys_sw) if h == 0 else (keys_sw, keys)
        v_own, v_oth = (vals, vals_sw) if h == 0 else (vals_sw, vals)
        kz = jnp.concatenate([jnp.where(lo, k_own, zero), jnp.where(lo, zero, k_oth)], axis=0)
        vz = jnp.concatenate([jnp.where(lo, v_own, zero), jnp.where(lo, zero, v_oth)], axis=0)
        c0 = h * gqa * HEAD_DIM
        qs = jnp.concatenate([q[:, c0:c0 + LANES], q[:, c0 + LANES:c0 + 2 * LANES]], axis=0)
        s_all = lax.dot_general(qs, kz, nt_dims, preferred_element_type=F32)
        probs, inv_den = [], []
        for par in range(2):
            tiles = []
            for j, bias in enumerate(biases):
                tile = s_all[:, par * nk + j * LANES:par * nk + (j + 1) * LANES]
                tiles.append(tile if bias is None else tile + bias)
            sink = jnp.where(rid < tq, sink_ref[h * gqa + par], sink_ref[h * gqa + 2 + par]) * LOG2E
            tile_max = tiles[0]
            for tile in tiles[1:]:
                tile_max = jnp.maximum(tile_max, tile)
            m = jnp.maximum(jnp.max(tile_max, axis=-1, keepdims=True), sink)
            tile_sum = None
            for tile in tiles:
                p = jnp.exp2(tile - m)
                tile_sum = p if tile_sum is None else tile_sum + p
                probs.append(p.astype(BF16))
            inv_den.append(1.0 / (jnp.sum(tile_sum, axis=-1, keepdims=True) + jnp.exp2(sink - m)))
        o = _bdot(jnp.concatenate(probs, axis=1), vz) * jnp.where(out_lo, inv_den[0], inv_den[1])
        o_ref[:, c0:c0 + LANES] = o[0:tq].astype(BF16)
        o_ref[:, c0 + LANES:c0 + 2 * LANES] = o[tq:2 * tq].astype(BF16)


def _win_attn_kernel(sink_ref, q_ref, kp_ref, kc_ref, kn_ref, kx_ref,
                     ksp_ref, ksc_ref, ksn_ref, ksx_ref,
                     vp_ref, vc_ref, vn_ref, vx_ref,
                     vsp_ref, vsc_ref, vsn_ref, vsx_ref, o_ref, *, tq, ctx_len):
    n = pl.program_id(1)
    nb = pl.num_programs(1)
    row = lax.broadcasted_iota(jnp.int32, (tq, tq), 0)
    col = lax.broadcasted_iota(jnp.int32, (tq, tq), 1)
    b_prev = jnp.where((col >= row) & (n > 0), 0.0, NEG_INF)
    b_next = jnp.where((col <= row) & (n < nb - 1), 0.0, NEG_INF)
    halo_bias = (jnp.concatenate([b_prev, b_prev], axis=0), jnp.concatenate([b_next, b_next], axis=0))

    def cat(a, b, c, d):
        return jnp.concatenate([a[...], b[...], c[...], d[...]], axis=0)

    _attn_body(q_ref[...], cat(kp_ref, kc_ref, kn_ref, kx_ref), cat(ksp_ref, ksc_ref, ksn_ref, ksx_ref),
               cat(vp_ref, vc_ref, vn_ref, vx_ref), cat(vsp_ref, vsc_ref, vsn_ref, vsx_ref),
               halo_bias, sink_ref, o_ref, tq)


def _win_attn_call(sink, q, k, ksw, v, vsw, kx, kxsw, vx, vxsw, n_seq, seq_len, ctx_len):
    tq = WINDOW
    nb = seq_len // tq

    def cur(b, n):
        return (b * nb + n, 0)

    def prev(b, n):
        return (b * nb + jnp.maximum(n - 1, 0), 0)

    def nxt(b, n):
        return (b * nb + jnp.minimum(n + 1, nb - 1), 0)

    def ctx(b, n):
        return (b, 0)

    def kv_specs():
        return [pl.BlockSpec((tq, KV_WIDTH), prev), pl.BlockSpec((tq, KV_WIDTH), cur),
                pl.BlockSpec((tq, KV_WIDTH), nxt), pl.BlockSpec((ctx_len, KV_WIDTH), ctx)]

    return pl.pallas_call(
        functools.partial(_win_attn_kernel, tq=tq, ctx_len=ctx_len),
        grid=(n_seq, nb),
        in_specs=[pl.BlockSpec(memory_space=pltpu.SMEM), pl.BlockSpec((tq, ATTN_WIDTH), cur)]
        + kv_specs() + kv_specs() + kv_specs() + kv_specs(),
        out_specs=pl.BlockSpec((tq, ATTN_WIDTH), cur),
        out_shape=jax.ShapeDtypeStruct((n_seq * seq_len, ATTN_WIDTH), BF16),
        compiler_params=_params("parallel", "parallel"),
        name="window_attention",
    )(sink, q, k, k, k, kx, ksw, ksw, ksw, kxsw, v, v, v, vx, vsw, vsw, vsw, vxsw)


def _ctx_attn_kernel(sink_ref, q_ref, k_ref, ks_ref, v_ref, vs_ref, o_ref, *, tq):
    _attn_body(q_ref[...], k_ref[...], ks_ref[...], v_ref[...], vs_ref[...], None, sink_ref, o_ref, tq)


def _ctx_attn_call(sink, q, k, ksw, v, vsw, n_seq, ctx_len):
    def blk(w):
        return pl.BlockSpec((ctx_len, w), lambda b: (b, 0))

    return pl.pallas_call(
        functools.partial(_ctx_attn_kernel, tq=ctx_len),
        grid=(n_seq,),
        in_specs=[pl.BlockSpec(memory_space=pltpu.SMEM), blk(ATTN_WIDTH),
                  blk(KV_WIDTH), blk(KV_WIDTH), blk(KV_WIDTH), blk(KV_WIDTH)],
        out_specs=blk(ATTN_WIDTH),
        out_shape=jax.ShapeDtypeStruct((n_seq * ctx_len, ATTN_WIDTH), BF16),
        compiler_params=_params("parallel"),
        name="context_attention",
    )(sink, q, k, ksw, v, vsw)


OUT_TILE = 512


def _outproj_kernel(attx_ref, attc_ref, yx_ref, yc_ref, xx_ref, xc_ref, wof_ref, g_ref, lng_ref, lnb_ref,
                    sc_ref, sh_ref, wr_ref, br_ref, x1_ref, h2v_ref, route_ref, routet_ref, cnt_ref, wo_ref,
                    *, alpha, n_lat_tiles):
    @pl.when(pl.program_id(0) == 0)
    def _():
        wo_ref[...] = wof_ref[...].astype(BF16)

    is_lat = pl.program_id(0) < n_lat_tiles
    att = jnp.where(is_lat, attx_ref[...], attc_ref[...])
    yloc = jnp.where(is_lat, yx_ref[...], yc_ref[...])
    xin = jnp.where(is_lat, xx_ref[...], xc_ref[...])
    o = _bdot(att, wo_ref[0:ATTN_WIDTH, :]) + _bdot(yloc, wo_ref[ATTN_WIDTH:, :])
    x1 = _layer_norm(alpha * xin + g_ref[...] * o) * lng_ref[...] + lnb_ref[...]
    x1_ref[...] = x1
    h2 = x1 * (1.0 + sc_ref[...]) + sh_ref[...]
    for c in range(ROW_TILE_ROWS):
        h2v_ref[:, c, :, :] = h2[:, c * LANES:(c + 1) * LANES].reshape(h2.shape[0] // SUBLANES, SUBLANES, LANES)

    hh, hl = _split_bf16(h2)
    t1 = _bdot(hh, wr_ref[...])
    lg = t1[:, 0:ROUTER_COLS] + t1[:, ROUTER_COLS:] + _bdot(hl, wr_ref[:, 0:ROUTER_COLS]) + br_ref[...]

    ts = lg.shape[0]
    lane = lax.broadcasted_iota(jnp.int32, (ts, ROUTER_COLS), 1).astype(F32)
    big = jnp.float32(ROUTER_COLS)

    def top1(v):
        m = jnp.max(v, axis=-1, keepdims=True)
        return m, jnp.min(jnp.where(v == m, lane, big), axis=-1, keepdims=True)

    gl = jnp.where(lane < N_GROUPS, lg, NEG_INF)
    g_val, g_idx = top1(gl)
    lse = g_val + jnp.log(jnp.sum(jnp.exp(gl - g_val), axis=-1, keepdims=True))
    p_group = jnp.exp(g_val - lse)
    e_lo = N_GROUPS + EXP_PER_GROUP * g_idx
    el = jnp.where((lane >= e_lo) & (lane < e_lo + EXP_PER_GROUP), lg, NEG_INF)
    e1, l1 = top1(el)
    e2, l2 = top1(jnp.where(lane == l1, NEG_INF, el))
    z = jnp.exp(e2 - e1)
    gate1 = p_group / (1.0 + z)
    gate2 = p_group * z / (1.0 + z)
    x1id = l1 - N_GROUPS
    x2id = l2 - N_GROUPS

    sel1 = lane == x1id
    sel2 = lane == x2id
    onehot = jnp.where(sel1 | sel2, 1.0, 0.0)
    r_i = lax.broadcasted_iota(jnp.int32, (ts, ts), 0)
    c_i = lax.broadcasted_iota(jnp.int32, (ts, ts), 1)
    before = jnp.where(c_i < r_i, 1.0, 0.0).astype(BF16)
    prefix = _bdot(before, onehot.astype(BF16))
    rank1 = jnp.sum(jnp.where(sel1, prefix, 0.0), axis=-1, keepdims=True)
    rank2 = jnp.sum(jnp.where(sel2, prefix, 0.0), axis=-1, keepdims=True)
    route = jnp.zeros((ts, ROUTER_COLS), F32)
    for col, val in enumerate((x1id, x2id, gate1, gate2, rank1, rank2)):
        route = jnp.where(lane == col, val, route)
    route_ref[...] = route
    routet_ref[...] = route.T[0:SUBLANES, :]
    cnt_ref[...] =jnp.broadcast_to(jnp.sum(onehot, axis=0, keepdims=True), (SUBLANES, ROUTER_COLS))


def _outproj_call(att_x, att_c, y_x, y_c, x_arr, x_off, c_arr, c_off, n_lat, n_ctx, seq_len,
                  w_out, layer, gvecs, lng, lnb, scvecs, shvecs, wr, br, alpha):
    ts = OUT_TILE
    d = D_MODEL
    n_lat_tiles = n_lat // ts
    n_tiles = (n_lat + n_ctx) // ts
    tiles_per_seq = seq_len // ts
    n_seq = n_lat // seq_len
    xo, co = x_off // ts, c_off // ts

    def lat_loc(i):
        return (jnp.minimum(i, n_lat_tiles - 1), 0)

    def ctx_loc(i):
        return (jnp.maximum(i - n_lat_tiles, 0), 0)

    def lat_in(i):
        return (xo + jnp.minimum(i, n_lat_tiles - 1), 0)

    def ctx_in(i):
        return (co + jnp.maximum(i - n_lat_tiles, 0), 0)

    def vec_idx(i):
        return (jnp.where(i < n_lat_tiles, i // tiles_per_seq, n_seq), 0, 0)

    vecb = pl.BlockSpec((None, 1, d), vec_idx)
    vec0 = pl.BlockSpec((1, d), lambda i: (0, 0))
    return pl.pallas_call(
        functools.partial(_outproj_kernel, alpha=alpha, n_lat_tiles=n_lat_tiles),
        grid=(n_tiles,),
        in_specs=[
            pl.BlockSpec((ts, ATTN_WIDTH), lat_loc), pl.BlockSpec((ts, ATTN_WIDTH), ctx_loc),
            pl.BlockSpec((ts, LOCAL_WIDTH), lat_loc), pl.BlockSpec((ts, LOCAL_WIDTH), ctx_loc),
            pl.BlockSpec((ts, d), lat_in), pl.BlockSpec((ts, d), ctx_in),
            pl.BlockSpec((None, d, d), lambda i: (layer, 0, 0)),
            vecb, vec0, vec0, vecb, vecb,
            pl.BlockSpec((d, 2 * ROUTER_COLS), lambda i: (0, 0)),
            pl.BlockSpec((1, ROUTER_COLS), lambda i: (0, 0)),
        ],
        out_specs=[pl.BlockSpec((ts, d), lambda i: (i, 0)),
                   pl.BlockSpec((ts // SUBLANES, ROW_TILE_ROWS, SUBLANES, LANES), lambda i: (i, 0, 0, 0)),
                   pl.BlockSpec((ts, ROUTER_COLS), lambda i: (i, 0)),
                   pl.BlockSpec((None, SUBLANES, ts), lambda i: (i, 0, 0)),
                   pl.BlockSpec((None, SUBLANES, ROUTER_COLS), lambda i: (i, 0, 0))],
        out_shape=[jax.ShapeDtypeStruct((n_lat + n_ctx, d), F32),
                   jax.ShapeDtypeStruct(((n_lat + n_ctx) // SUBLANES, ROW_TILE_ROWS, SUBLANES, LANES), F32),
                   jax.ShapeDtypeStruct((n_lat + n_ctx, ROUTER_COLS), F32),
                   jax.ShapeDtypeStruct((n_tiles, SUBLANES, ts), F32),
                   jax.ShapeDtypeStruct((n_tiles, SUBLANES, ROUTER_COLS), F32)],
        scratch_shapes=[pltpu.VMEM((d, d), BF16)],
        compiler_params=_params("arbitrary"),
        name="out_projection",
    )(att_x, att_c, y_x, y_c, x_arr, c_arr, w_out, gvecs, lng, lnb, scvecs, shvecs, wr, br)


DISPATCH_TILE = 512


def _dispatch_kernel(seg_ref, pos_ref, h_ref, xs_out, zbuf, zsem, sem):
    @pl.when(pl.program_id(0) == 0)
    def _():
        zbuf[...] = jnp.zeros_like(zbuf)

        def zero_copy(e):
            return pltpu.make_async_copy(zbuf, xs_out.at[pl.ds(seg_ref[0, e] - MOE_BLOCK, MOE_BLOCK)], zsem)

        for e in range(N_EXPERTS):
            @pl.when(seg_ref[1, e] > 0)
            def _():
                zero_copy(e).start()
        for e in range(N_EXPERTS):
            @pl.when(seg_ref[1, e] > 0)
            def _():
                zero_copy(e).wait()

        def tail_copy(b):
            return pltpu.make_async_copy(zbuf, xs_out.at[pl.ds(b * MOE_BLOCK, MOE_BLOCK)], zsem)

        n_blocks = xs_out.shape[0] // MOE_BLOCK
        first_unused = seg_ref[0, N_EXPERTS - 1] // MOE_BLOCK
        lax.fori_loop(first_unused, n_blocks, lambda b, c: (tail_copy(b).start(), c)[1], 0)
        lax.fori_loop(first_unused, n_blocks, lambda b, c: (tail_copy(b).wait(), c)[1], 0)

    def body(j, carry):
        for s in range(SUBLANES):
            t = SUBLANES * j + s
            for k in range(TOP_K):
                pltpu.make_async_copy(h_ref.at[j, :, s, :], xs_out.at[pos_ref[0, 0, k * DISPATCH_TILE + t]],
                                      sem).start(priority=k)
        return carry

    lax.fori_loop(0, DISPATCH_TILE // SUBLANES, body, 0)
    for k in range(TOP_K):
        pltpu.make_async_copy(h_ref, h_ref, sem).wait()


def _dispatch_call(seg, pos, h2t, n_slots):
    n_tok = h2t.shape[0] * SUBLANES
    n_tiles = n_tok // DISPATCH_TILE
    row = (ROW_TILE_ROWS, LANES)
    grid_spec = pltpu.PrefetchScalarGridSpec(
        num_scalar_prefetch=1,
        grid=(n_tiles,),
        in_specs=[
            pl.BlockSpec((1, 1, TOP_K * DISPATCH_TILE), lambda i, seg: (i, 0, 0), memory_space=pltpu.SMEM),
            pl.BlockSpec((DISPATCH_TILE // SUBLANES, ROW_TILE_ROWS, SUBLANES, LANES), lambda i, seg: (i, 0, 0, 0)),
        ],
        out_specs=pl.BlockSpec(memory_space=pl.ANY),
        scratch_shapes=[pltpu.VMEM((MOE_BLOCK,) + row, F32), pltpu.SemaphoreType.DMA(()),
                        pltpu.SemaphoreType.DMA(())],
    )
    return pl.pallas_call(
        _dispatch_kernel,
        grid_spec=grid_spec,
        out_shape=jax.ShapeDtypeStruct((n_slots,) + row, F32),
        compiler_params=_params("arbitrary"),
        name="moe_dispatch",
    )(seg, pos, h2t)


def _moe_kernel(be_ref, nu_ref, xs_hbm, w1_ref, w3_ref, w2_ref, ys_hbm, w1b, w3b, w2b, xbuf, obuf, isem, osem):
    i = pl.program_id(0)
    n_used = nu_ref[0]
    slot = i % 2

    def in_copies(blk, at_slot):
        return _tiles_to_rows(xs_hbm.at[pl.ds(blk * MOE_BLOCK, MOE_BLOCK)], xbuf.at[at_slot], isem.at[at_slot])

    def out_copies(blk, at_slot):
        return _rows_to_tiles(obuf.at[at_slot], ys_hbm.at[pl.ds(blk * MOE_BLOCK, MOE_BLOCK)], osem.at[at_slot])

    @pl.when(i == 0)
    def _():
        _start_all(in_copies(0, 0))

    @pl.when(i + 1 < n_used)
    def _():
        _start_all(in_copies(i + 1, 1 - slot))

    @pl.when((i == 0) | (be_ref[i] != be_ref[jnp.maximum(i - 1, 0)]))
    def _():
        w1b[...] = w1_ref[...].astype(BF16)
        w3b[...] = w3_ref[...].astype(BF16)
        w2b[...] = w2_ref[...].astype(BF16)

    @pl.when(i < n_used)
    def _():
        _wait_all(in_copies(i, slot))
        x = xbuf[slot].astype(BF16)
        obuf[slot] = _bdot((jax.nn.silu(_bdot(x, w1b[...])) * _bdot(x, w3b[...])).astype(BF16), w2b[...])
        _start_all(out_copies(i, slot))

        @pl.when(i >= 1)
        def _():
            _wait_all(out_copies(i - 1, 1 - slot))

        @pl.when(i == n_used - 1)
        def _():
            _wait_all(out_copies(i, slot))

    @pl.when(i >= n_used)
    def _():
        obuf[slot] = jnp.zeros(obuf.shape[1:], F32)
        _start_all(out_copies(i, slot))
        _wait_all(out_copies(i, slot))


def _moe_call(block_e, n_used, xs, w1, w3, w2, layer):
    n_blocks = block_e.shape[0]
    d = D_MODEL
    grid_spec = pltpu.PrefetchScalarGridSpec(
        num_scalar_prefetch=2,
        grid=(n_blocks,),
        in_specs=[
            pl.BlockSpec(memory_space=pl.ANY),
            pl.BlockSpec((None, None, d, D_EXPERT), lambda i, be, nu: (layer, be[i], 0, 0)),
            pl.BlockSpec((None, None, d, D_EXPERT), lambda i, be, nu: (layer, be[i], 0, 0)),
            pl.BlockSpec((None, None, D_EXPERT, d), lambda i, be, nu: (layer, be[i], 0, 0)),
        ],
        out_specs=pl.BlockSpec(memory_space=pl.ANY),
        scratch_shapes=[pltpu.VMEM((d, D_EXPERT), BF16), pltpu.VMEM((d, D_EXPERT), BF16),
                        pltpu.VMEM((D_EXPERT, d), BF16),
                        pltpu.VMEM((2, MOE_BLOCK, d), F32), pltpu.VMEM((2, MOE_BLOCK, d), F32),
                        pltpu.SemaphoreType.DMA((2,)), pltpu.SemaphoreType.DMA((2,))],
    )
    return pl.pallas_call(
        _moe_kernel,
        grid_spec=grid_spec,
        out_shape=jax.ShapeDtypeStruct((n_blocks * MOE_BLOCK, ROW_TILE_ROWS, LANES), F32),
        compiler_params=_params("arbitrary"),
        name="moe_experts",
    )(block_e, n_used, xs, w1, w3, w2)


COMBINE_TILE = 256


def _combine_kernel(pos0_ref, posn_ref, x1_ref, route_ref, g_ref, lng_ref, lnb_ref, ys_hbm, o_ref, ybuf, sem,
                    *, alpha):
    i = pl.program_id(0)
    nb = pl.num_programs(0)
    n_rows = TOP_K * COMBINE_TILE

    def issue(pos_ref, slot):
        def body(j, carry):
            for s in range(SUBLANES):
                pltpu.make_async_copy(ys_hbm.at[pos_ref[0, 0, SUBLANES * j + s]], ybuf.at[slot, j, :, s, :],
                                      sem.at[slot]).start(priority=s % N_DMA_PRIORITIES)
            return carry
        lax.fori_loop(0, n_rows // SUBLANES, body, 0)

    @pl.when(i == 0)
    def _():
        issue(pos0_ref, 0)

    @pl.when(i + 1 < nb)
    def _():
        issue(posn_ref, (i + 1) % 2)

    slot = i % 2
    pltpu.make_async_copy(ybuf.at[slot], ybuf.at[slot], sem.at[slot]).wait()
    yrows = jnp.concatenate([ybuf[slot, :, c, :, :].reshape(n_rows, LANES) for c in range(ROW_TILE_ROWS)], axis=1)
    route = route_ref[...]
    y = (yrows[0:COMBINE_TILE, :] * route[:, ROUTE_GATE:ROUTE_GATE + 1]
         + yrows[COMBINE_TILE:, :] * route[:, ROUTE_GATE + 1:ROUTE_GATE + 2])
    o_ref[...] = _layer_norm(alpha * x1_ref[...] + g_ref[...] * y) * lng_ref[...] + lnb_ref[...]


def _combine_call(pos, x1, route, gvecs, lng, lnb, ys, n_tok, tiles_per_seq, n_seq, alpha):
    d = D_MODEL
    n_tiles = n_tok // COMBINE_TILE
    n_lat_tiles = tiles_per_seq * n_seq

    def g_idx(i):
        return (jnp.where(i < n_lat_tiles, i // tiles_per_seq, n_seq), 0, 0)

    smem_blk = (1, 1, TOP_K * COMBINE_TILE)
    return pl.pallas_call(
        functools.partial(_combine_kernel, alpha=alpha),
        grid=(n_tiles,),
        in_specs=[
            pl.BlockSpec(smem_blk, lambda i: (0, 0, 0), memory_space=pltpu.SMEM),
            pl.BlockSpec(smem_blk, lambda i: (jnp.minimum(i + 1, n_tiles - 1), 0, 0), memory_space=pltpu.SMEM),
            pl.BlockSpec((COMBINE_TILE, d), lambda i: (i, 0)),
            pl.BlockSpec((COMBINE_TILE, ROUTER_COLS), lambda i: (i, 0)),
            pl.BlockSpec((None, 1, d), g_idx),
            pl.BlockSpec((1, d), lambda i: (0, 0)),
            pl.BlockSpec((1, d), lambda i: (0, 0)),
            pl.BlockSpec(memory_space=pl.ANY),
        ],
        out_specs=pl.BlockSpec((COMBINE_TILE, d), lambda i: (i, 0)),
        out_shape=jax.ShapeDtypeStruct((n_tok, d), F32),
        scratch_shapes=[pltpu.VMEM((2, TOP_K * COMBINE_TILE // SUBLANES, ROW_TILE_ROWS, SUBLANES, LANES), F32),
                        pltpu.SemaphoreType.DMA((2,))],
        compiler_params=_params("arbitrary"),
        name="moe_combine",
    )(pos, pos, x1, route, gvecs, lng, lnb, ys)


def _slots(route_t, tile_counts):
    n_tiles = tile_counts.shape[0]
    n = n_tiles * route_t.shape[2]
    tc = tile_counts.astype(jnp.int32)
    counts = jnp.sum(tc, axis=0)
    padded = (counts + MOE_BLOCK - 1) // MOE_BLOCK * MOE_BLOCK
    pends = jnp.cumsum(padded)
    base = (pends - padded)[None, :] + jnp.cumsum(tc, axis=0) - tc
    n_blocks = -(-n * TOP_K // MOE_BLOCK) + N_EXPERTS
    block_start = jnp.arange(n_blocks, dtype=jnp.int32) * MOE_BLOCK
    block_e = jnp.minimum(jnp.sum(pends[None, 0:N_EXPERTS] <= block_start[:, None], axis=1),
                          N_EXPERTS - 1).astype(jnp.int32)
    n_used = (pends[N_EXPERTS - 1] // MOE_BLOCK).astype(jnp.int32).reshape(1)
    seg = jnp.stack([pends[0:N_EXPERTS], padded[0:N_EXPERTS]]).astype(jnp.int32)
    pos = []
    for k in range(TOP_K):
        e = route_t[:, ROUTE_EXPERT + k, :].astype(jnp.int32)
        b = jnp.zeros_like(e)
        for x in range(N_EXPERTS):
            b = jnp.where(e == x, base[:, x:x + 1], b)
        pos.append((b + route_t[:, ROUTE_RANK + k, :].astype(jnp.int32)).reshape(n))
    return block_e, n_used, seg, pos


def _rope_tables(seq_len):
    m = HEAD_DIM // 4
    freqs = ROPE_BASE ** (-jnp.arange(m, dtype=F32) / m)
    t = jnp.arange(seq_len)
    row = (t // GRID_W).astype(F32)[:, None] * freqs[None, :]
    col = (t % GRID_W).astype(F32)[:, None] * freqs[None, :]
    cos = jnp.concatenate([jnp.cos(row), jnp.cos(row), jnp.cos(col), jnp.cos(col)], axis=-1)
    zero = jnp.zeros_like(row)
    sin_a = jnp.concatenate([zero, jnp.sin(row), zero, jnp.sin(col)], axis=-1)
    sin_b = jnp.concatenate([-jnp.sin(row), zero, -jnp.sin(col), zero], axis=-1)
    rep = LANES // HEAD_DIM
    return jnp.tile(cos, (1, rep)), jnp.tile(sin_a, (1, rep)), jnp.tile(sin_b, (1, rep))


def kernel(x, c, ctx, c_ctx, w_ada, b_ada, w_in, conv_w, attn_sink, gm_ws, gm_bs, w_out, ln1_g, ln1_b,
           w_rg, b_rg, w_re, b_re, w1, w3, w2, ln2_g, ln2_b):
    b_, s_, d_ = x.shape
    c_len = ctx.shape[1]
    depth = w_ada.shape[0]
    alpha = (2 * depth) ** 0.25
    n_lat = b_ * s_
    n_ctx = b_ * c_len
    ts = 1024

    cin = jnp.zeros((ADA_ROWS, d_), F32).at[0:b_].set(c).at[b_].set(c_ctx)
    mod = _ada_call(cin, w_ada, b_ada)

    cos, sin_a, sin_b = _rope_tables(s_)
    ones_c = jnp.ones((c_len, LANES), F32)
    zeros_c = jnp.zeros((c_len, LANES), F32)

    x_flat = x.reshape(n_lat, d_)
    c_flat = ctx.reshape(n_ctx, d_)
    x_off, c_off = 0, 0
    x_arr, c_arr = x_flat, c_flat

    for l in range(depth):
        last = l == depth - 1
        mx = mod[l, 0:b_].reshape(b_, 6, 1, d_)
        sh1, sc1, g1, sh2, sc2, g2 = (mx[:, i] for i in range(6))
        mc = jnp.broadcast_to(mod[l, b_].reshape(1, 6, 1, d_), (b_, 6, 1, d_))
        gm_w = gm_ws[l].astype(BF16)
        gm_b = jnp.repeat(gm_bs[l].T, GM_HEAD, axis=1)
        sink = attn_sink[l]

        qx, kx, kxs, vx, vxs, yx = _inproj_call(x_arr, x_off, b_, s_, ts, sc1, sh1, w_in, l, cos, sin_a, sin_b,
                                                conv_w[l], gm_w, gm_b)
        qc, kc, kcs, vc, vcs, yc = _inproj_call(c_arr, c_off, b_, c_len, c_len, mc[:, 1], mc[:, 0], w_in, l,
                                                ones_c, zeros_c, zeros_c, conv_w[l], gm_w, gm_b)
        att_x = _win_attn_call(sink, qx, kx, kxs, vx, vxs, kc, kcs, vc, vcs, b_, s_, c_len)

        w_r = jnp.zeros((d_, ROUTER_COLS), F32).at[:, 0:N_GROUPS].set(w_rg[l]) \
            .at[:, N_GROUPS:N_GROUPS + N_EXPERTS].set(w_re[l])
        w_r_hi = w_r.astype(BF16)
        w_r_lo = (w_r - w_r_hi.astype(F32)).astype(BF16)
        wr = jnp.concatenate([w_r_hi, w_r_lo], axis=1)
        br = jnp.zeros((1, ROUTER_COLS), F32).at[0, 0:N_GROUPS].set(b_rg[l]) \
            .at[0, N_GROUPS:N_GROUPS + N_EXPERTS].set(b_re[l])
        lng1, lnb1 = ln1_g[l].reshape(1, d_), ln1_b[l].reshape(1, d_)
        lng2, lnb2 = ln2_g[l].reshape(1, d_), ln2_b[l].reshape(1, d_)

        n_tot = n_lat if last else n_lat + n_ctx
        gvecs1 = jnp.concatenate([g1, mc[0:1, 2]], axis=0)
        scvecs2 = jnp.concatenate([sc2, mc[0:1, 4]], axis=0)
        shvecs2 = jnp.concatenate([sh2, mc[0:1, 3]], axis=0)
        if last:
            x1, h2t, route, route_t, tcnt = _outproj_call(att_x, att_x, yx, yx, x_arr, x_off, x_arr, x_off, n_lat, 0, s_,
                                                 w_out, l, gvecs1, lng1, lnb1, scvecs2, shvecs2, wr, br, alpha)
        else:
            att_c = _ctx_attn_call(sink, qc, kc, kcs, vc, vcs, b_, c_len)
            x1, h2t, route, route_t, tcnt = _outproj_call(att_x, att_c, yx, yc, x_arr, x_off, c_arr, c_off, n_lat, n_ctx,
                                                 s_, w_out, l, gvecs1, lng1, lnb1, scvecs2, shvecs2, wr, br, alpha)

        block_e, n_used, seg, pos = _slots(route_t, tcnt[:, 0, :])
        n_slots = block_e.shape[0] * MOE_BLOCK

        def per_tile(tile):
            return jnp.concatenate([p.reshape(n_tot // tile, 1, tile) for p in pos], axis=2)

        xs = _dispatch_call(seg, per_tile(DISPATCH_TILE), h2t, n_slots)
        ys = _moe_call(block_e, n_used, xs, w1, w3, w2, l)

        gvecs = jnp.concatenate([g2, mc[0:1, 5]], axis=0)
        out = _combine_call(per_tile(COMBINE_TILE), x1, route, gvecs, lng2, lnb2, ys, n_tot, s_ // COMBINE_TILE,
                            b_, alpha)
        x_arr, x_off = out, 0
        c_arr, c_off = out, n_lat

    return x_arr.reshape(b_, s_, d_)
```

```python
import functools

import jax
import jax.numpy as jnp
from jax import lax
from jax.experimental import pallas as pl
from jax.experimental.pallas import tpu as pltpu

F32 = jnp.float32
BF16 = jnp.bfloat16

D_MODEL = 1024
GRID_W = 64
HEAD_DIM = 64
N_HEADS = 8
N_KV_HEADS = 2
ATTN_WIDTH = N_HEADS * HEAD_DIM
KV_WIDTH = N_KV_HEADS * HEAD_DIM
WINDOW = 128
ATTN_SCALE = HEAD_DIM ** -0.5
LOG2E = 1.4426950408889634
ROPE_BASE = 10000.0
CONV_WIDTH = D_MODEL // 4
GM_WIDTH = D_MODEL // 4
GM_GROUPS = 4
GM_HEAD = GM_WIDTH // GM_GROUPS
CHUNK = 128
LOCAL_WIDTH = CONV_WIDTH + GM_WIDTH
IN_WIDTH = ATTN_WIDTH + 2 * KV_WIDTH + 3 * CONV_WIDTH + 2 * GM_WIDTH
QKV_WIDTH = ATTN_WIDTH + 2 * KV_WIDTH
N_GROUPS = 4
EXP_PER_GROUP = 8
N_EXPERTS = N_GROUPS * EXP_PER_GROUP
TOP_K = 2
D_EXPERT = D_MODEL // 2
MOE_BLOCK = 512
LN_EPS = 1e-6
NEG_INF = -1e30

LANES = 128
SUBLANES = 8
VMEM_LIMIT_BYTES = 48 * 1024 * 1024
N_DMA_PRIORITIES = 2

ROW_TILE_ROWS = D_MODEL // LANES
assert ROW_TILE_ROWS == SUBLANES

ROUTER_COLS = LANES
ROUTE_EXPERT = 0
ROUTE_GATE = TOP_K
ROUTE_RANK = 2 * TOP_K
ROPE_HALF_PAIR = HEAD_DIM // 4


def _bdot(a, b):
    return jnp.dot(a, b, preferred_element_type=F32)


def _split_bf16(a):
    hi = a.astype(BF16)
    lo = (a - hi.astype(F32)).astype(BF16)
    return hi, lo


def _layer_norm(r):
    mu = jnp.mean(r, axis=-1, keepdims=True)
    d = r - mu
    var = jnp.mean(d * d, axis=-1, keepdims=True)
    return d * lax.rsqrt(var + LN_EPS)


def _params(*sem, flags=None):
    return pltpu.CompilerParams(dimension_semantics=sem, vmem_limit_bytes=VMEM_LIMIT_BYTES, flags=flags)


def _start_all(copies):
    for cp in copies:
        cp.start()


def _wait_all(copies):
    for cp in copies:
        cp.wait()


ADA_ROWS = 16
ADA_TILE = 1536


def _ada_kernel(c_ref, w_ref, b_ref, o_ref):
    act = jax.nn.silu(c_ref[...])
    ah, al = _split_bf16(act)
    wh, wl = _split_bf16(w_ref[...])
    o_ref[...] = _bdot(ah, wh) + _bdot(ah, wl) + _bdot(al, wh) + b_ref[...]


def _ada_call(cin, w_ada, b_ada):
    depth, d, n = w_ada.shape
    return pl.pallas_call(
        _ada_kernel,
        grid=(depth, n // ADA_TILE),
        in_specs=[
            pl.BlockSpec((ADA_ROWS, d), lambda l, j: (0, 0)),
            pl.BlockSpec((None, d, ADA_TILE), lambda l, j: (l, 0, j)),
            pl.BlockSpec((None, 1, ADA_TILE), lambda l, j: (l, 0, j)),
        ],
        out_specs=pl.BlockSpec((None, ADA_ROWS, ADA_TILE), lambda l, j: (l, 0, j)),
        out_shape=jax.ShapeDtypeStruct((depth, ADA_ROWS, n), F32),
        compiler_params=_params("parallel", "parallel"),
        name="ada_modulation",
    )(cin, w_ada, b_ada.reshape(depth, 1, n))


def _inproj_kernel(x_ref, xp_ref, xn_ref, sc_ref, sh_ref, wf_ref, cos_ref, sa_ref, sb_ref,
                   cw_ref, gw_ref, gb_ref,
                   q_ref, k_ref, ksw_ref, v_ref, vsw_ref, y_ref, w_ref, *, ts):
    t = pl.program_id(1)
    nt = pl.num_programs(1)

    @pl.when((pl.program_id(0) == 0) & (t == 0))
    def _():
        w_ref[...] = wf_ref[...].astype(BF16)

    sc = 1.0 + sc_ref[...]
    sh = sh_ref[...]
    hx = (x_ref[...] * sc + sh).astype(BF16)

    pq = _bdot(hx, w_ref[:, 0:QKV_WIDTH])
    cos = cos_ref[...]
    sa = sa_ref[...]
    sb = sb_ref[...]

    def rope(z):
        return (z * cos + pltpu.roll(z, ROPE_HALF_PAIR, 1) * sa
                + pltpu.roll(z, LANES - ROPE_HALF_PAIR, 1) * sb)

    for j in range(ATTN_WIDTH // LANES):
        sl = slice(j * LANES, (j + 1) * LANES)
        q_ref[:, sl] = (rope(pq[:, sl]) * (ATTN_SCALE * LOG2E)).astype(BF16)
    kr = rope(pq[:, ATTN_WIDTH:ATTN_WIDTH + KV_WIDTH])
    k_ref[...] = kr.astype(BF16)
    ksw_ref[...] = pltpu.roll(kr, HEAD_DIM, 1).astype(BF16)
    vv = pq[:, ATTN_WIDTH + KV_WIDTH:QKV_WIDTH]
    v_ref[...] = vv.astype(BF16)
    vsw_ref[...] = pltpu.roll(vv, HEAD_DIM, 1).astype(BF16)

    pm = _bdot(hx, w_ref[:, QKV_WIDTH:IN_WIDTH])
    cb = pm[:, 0:CONV_WIDTH]
    u = pm[:, CONV_WIDTH:2 * CONV_WIDTH] * pm[:, 2 * CONV_WIDTH:3 * CONV_WIDTH]

    halo = jnp.concatenate([xp_ref[...], xn_ref[...]], axis=0)
    hh = (halo * sc + sh).astype(BF16)
    ph = _bdot(hh, w_ref[:, QKV_WIDTH + CONV_WIDTH:QKV_WIDTH + 3 * CONV_WIDTH])
    uh = ph[:, 0:CONV_WIDTH] * ph[:, CONV_WIDTH:2 * CONV_WIDTH]
    up_row = jnp.where(t > 0, uh[SUBLANES - 1:SUBLANES, :], 0.0)
    dn_row = jnp.where(t < nt - 1, uh[SUBLANES:SUBLANES + 1, :], 0.0)
    ridx = lax.broadcasted_iota(jnp.int32, (ts, CONV_WIDTH), 0)
    u_up = jnp.where(ridx == 0, up_row, pltpu.roll(u, 1, 0))
    u_dn = jnp.where(ridx == ts - 1, dn_row, pltpu.roll(u, ts - 1, 0))
    cw = cw_ref[...]
    y_conv = cb * (u_up * cw[0:1, :] + u * cw[1:2, :] + u_dn * cw[2:3, :])
    y_ref[:, 0:CONV_WIDTH] = y_conv.astype(BF16)

    gu = jax.nn.gelu(pm[:, 3 * CONV_WIDTH:3 * CONV_WIDTH + GM_WIDTH])
    gv = _layer_norm(jax.nn.gelu(pm[:, 3 * CONV_WIDTH + GM_WIDTH:3 * CONV_WIDTH + 2 * GM_WIDTH])).astype(BF16)
    lane = lax.broadcasted_iota(jnp.int32, (CHUNK, LANES), 1)
    zero = jnp.zeros((CHUNK, LANES), BF16)
    for c in range(ts // CHUNK):
        rows = slice(c * CHUNK, (c + 1) * CHUNK)
        for j in range(GM_WIDTH // LANES):
            cols = slice(j * LANES, (j + 1) * LANES)
            vp = gv[rows, cols]
            s = (_bdot(gw_ref[2 * j], jnp.where(lane < GM_HEAD, vp, zero))
                 + _bdot(gw_ref[2 * j + 1], jnp.where(lane >= GM_HEAD, vp, zero))
                 + gb_ref[:, cols])
            y_ref[rows, CONV_WIDTH + j * LANES:CONV_WIDTH + (j + 1) * LANES] = (gu[rows, cols] * s).astype(BF16)


def _inproj_call(x2d, row_off, n_seq, seq_len, ts, sc, sh, w_in, layer, cos, sa, sb, conv_w, gm_w, gm_b):
    nt = seq_len // ts
    off_t = row_off // ts
    off_8 = row_off // SUBLANES
    last_8 = x2d.shape[0] // SUBLANES - 1
    per_tile_8 = ts // SUBLANES
    n_out = n_seq * seq_len
    d = D_MODEL

    def tile_idx(b, t):
        return (off_t + b * nt + t, 0)

    def prev_idx(b, t):
        return (jnp.maximum(off_8 + (b * nt + t) * per_tile_8 - 1, 0), 0)

    def next_idx(b, t):
        return (jnp.minimum(off_8 + (b * nt + t + 1) * per_tile_8, last_8), 0)

    def out_idx(b, t):
        return (b * nt + t, 0)

    vec = pl.BlockSpec((None, 1, d), lambda b, t: (b, 0, 0))
    rope_spec = pl.BlockSpec((ts, LANES), lambda b, t: (t, 0))
    kv_spec = pl.BlockSpec((ts, KV_WIDTH), out_idx)
    kv_shape = jax.ShapeDtypeStruct((n_out, KV_WIDTH), BF16)
    return pl.pallas_call(
        functools.partial(_inproj_kernel, ts=ts),
        grid=(n_seq, nt),
        in_specs=[
            pl.BlockSpec((ts, d), tile_idx),
            pl.BlockSpec((SUBLANES, d), prev_idx),
            pl.BlockSpec((SUBLANES, d), next_idx),
            vec, vec,
            pl.BlockSpec((None, d, IN_WIDTH), lambda b, t: (layer, 0, 0)),
            rope_spec, rope_spec, rope_spec,
            pl.BlockSpec((3, CONV_WIDTH), lambda b, t: (0, 0)),
            pl.BlockSpec((GM_GROUPS, CHUNK, CHUNK), lambda b, t: (0, 0, 0)),
            pl.BlockSpec((CHUNK, GM_WIDTH), lambda b, t: (0, 0)),
        ],
        out_specs=[
            pl.BlockSpec((ts, ATTN_WIDTH), out_idx),
            kv_spec, kv_spec, kv_spec, kv_spec,
            pl.BlockSpec((ts, LOCAL_WIDTH), out_idx),
        ],
        out_shape=[
            jax.ShapeDtypeStruct((n_out, ATTN_WIDTH), BF16),
            kv_shape, kv_shape, kv_shape, kv_shape,
            jax.ShapeDtypeStruct((n_out, LOCAL_WIDTH), BF16),
        ],
        scratch_shapes=[pltpu.VMEM((d, IN_WIDTH), BF16)],
        compiler_params=_params("arbitrary", "arbitrary"),
        name="in_projection",
    )(x2d, x2d, x2d, sc, sh, w_in, cos, sa, sb, conv_w, gm_w, gm_b)


def _attn_body(q, keys, keys_sw, vals, vals_sw, halo_bias, sink_ref, o_ref, tq):
    nk = keys.shape[0]
    lane = lax.broadcasted_iota(jnp.int32, (nk, LANES), 1)
    lo = lane < HEAD_DIM
    zero = jnp.zeros((nk, LANES), BF16)
    rid = lax.broadcasted_iota(jnp.int32, (2 * tq, 1), 0)
    out_lo = lax.broadcasted_iota(jnp.int32, (2 * tq, LANES), 1) < HEAD_DIM
    nt_dims = (((1,), (1,)), ((), ()))
    gqa = N_HEADS // N_KV_HEADS
    biases = [None] * (nk // LANES)
    if halo_bias is not None:
        assert tq == LANES
        biases[0], biases[2] = halo_bias
    for h in range(N_KV_HEADS):
        k_own, k_oth = (keys, keys_sw) if h == 0 else (keys_sw, keys)
        v_own, v_oth = (vals, vals_sw) if h == 0 else (vals_sw, vals)
        kz = jnp.concatenate([jnp.where(lo, k_own, zero), jnp.where(lo, zero, k_oth)], axis=0)
        vz = jnp.concatenate([jnp.where(lo, v_own, zero), jnp.where(lo, zero, v_oth)], axis=0)
        c0 = h * gqa * HEAD_DIM
        qs = jnp.concatenate([q[:, c0:c0 + LANES], q[:, c0 + LANES:c0 + 2 * LANES]], axis=0)
        s_all = lax.dot_general(qs, kz, nt_dims, preferred_element_type=F32)
        probs, inv_den = [], []
        for par in range(2):
            tiles = []
            for j, bias in enumerate(biases):
                tile = s_all[:, par * nk + j * LANES:par * nk + (j + 1) * LANES]
                tiles.append(tile if bias is None else tile + bias)
            sink = jnp.where(rid < tq, sink_ref[h * gqa + par], sink_ref[h * gqa + 2 + par]) * LOG2E
            tile_max = tiles[0]
            for tile in tiles[1:]:
                tile_max = jnp.maximum(tile_max, tile)
            m = jnp.maximum(jnp.max(tile_max, axis=-1, keepdims=True), sink)
            tile_sum = None
            for tile in tiles:
                p = jnp.exp2(tile - m)
                tile_sum = p if tile_sum is None else tile_sum + p
                probs.append(p.astype(BF16))
            inv_den.append(1.0 / (jnp.sum(tile_sum, axis=-1, keepdims=True) + jnp.exp2(sink - m)))
        o = _bdot(jnp.concatenate(probs, axis=1), vz) * jnp.where(out_lo, inv_den[0], inv_den[1])
        o_ref[:, c0:c0 + LANES] = o[0:tq].astype(BF16)
        o_ref[:, c0 + LANES:c0 + 2 * LANES] = o[tq:2 * tq].astype(BF16)


def _win_attn_kernel(sink_ref, q_ref, kp_ref, kc_ref, kn_ref, kx_ref,
                     ksp_ref, ksc_ref, ksn_ref, ksx_ref,
                     vp_ref, vc_ref, vn_ref, vx_ref,
                     vsp_ref, vsc_ref, vsn_ref, vsx_ref, o_ref, *, tq, ctx_len):
    n = pl.program_id(1)
    nb = pl.num_programs(1)
    row = lax.broadcasted_iota(jnp.int32, (tq, tq), 0)
    col = lax.broadcasted_iota(jnp.int32, (tq, tq), 1)
    b_prev = jnp.where((col >= row) & (n > 0), 0.0, NEG_INF)
    b_next = jnp.where((col <= row) & (n < nb - 1), 0.0, NEG_INF)
    halo_bias = (jnp.concatenate([b_prev, b_prev], axis=0), jnp.concatenate([b_next, b_next], axis=0))

    def cat(a, b, c, d):
        return jnp.concatenate([a[...], b[...], c[...], d[...]], axis=0)

    _attn_body(q_ref[...], cat(kp_ref, kc_ref, kn_ref, kx_ref), cat(ksp_ref, ksc_ref, ksn_ref, ksx_ref),
               cat(vp_ref, vc_ref, vn_ref, vx_ref), cat(vsp_ref, vsc_ref, vsn_ref, vsx_ref),
               halo_bias, sink_ref, o_ref, tq)


def _win_attn_call(sink, q, k, ksw, v, vsw, kx, kxsw, vx, vxsw, n_seq, seq_len, ctx_len):
    tq = WINDOW
    nb = seq_len // tq

    def cur(b, n):
        return (b * nb + n, 0)

    def prev(b, n):
        return (b * nb + jnp.maximum(n - 1, 0), 0)

    def nxt(b, n):
        return (b * nb + jnp.minimum(n + 1, nb - 1), 0)

    def ctx(b, n):
        return (b, 0)

    def kv_specs():
        return [pl.BlockSpec((tq, KV_WIDTH), prev), pl.BlockSpec((tq, KV_WIDTH), cur),
                pl.BlockSpec((tq, KV_WIDTH), nxt), pl.BlockSpec((ctx_len, KV_WIDTH), ctx)]

    return pl.pallas_call(
        functools.partial(_win_attn_kernel, tq=tq, ctx_len=ctx_len),
        grid=(n_seq, nb),
        in_specs=[pl.BlockSpec(memory_space=pltpu.SMEM), pl.BlockSpec((tq, ATTN_WIDTH), cur)]
        + kv_specs() + kv_specs() + kv_specs() + kv_specs(),
        out_specs=pl.BlockSpec((tq, ATTN_WIDTH), cur),
        out_shape=jax.ShapeDtypeStruct((n_seq * seq_len, ATTN_WIDTH), BF16),
        compiler_params=_params("parallel", "parallel"),
        name="window_attention",
    )(sink, q, k, k, k, kx, ksw, ksw, ksw, kxsw, v, v, v, vx, vsw, vsw, vsw, vxsw)


def _ctx_attn_kernel(sink_ref, q_ref, k_ref, ks_ref, v_ref, vs_ref, o_ref, *, tq):
    _attn_body(q_ref[...], k_ref[...], ks_ref[...], v_ref[...], vs_ref[...], None, sink_ref, o_ref, tq)


def _ctx_attn_call(sink, q, k, ksw, v, vsw, n_seq, ctx_len):
    def blk(w):
        return pl.BlockSpec((ctx_len, w), lambda b: (b, 0))

    return pl.pallas_call(
        functools.partial(_ctx_attn_kernel, tq=ctx_len),
        grid=(n_seq,),
        in_specs=[pl.BlockSpec(memory_space=pltpu.SMEM), blk(ATTN_WIDTH),
                  blk(KV_WIDTH), blk(KV_WIDTH), blk(KV_WIDTH), blk(KV_WIDTH)],
        out_specs=blk(ATTN_WIDTH),
        out_shape=jax.ShapeDtypeStruct((n_seq * ctx_len, ATTN_WIDTH), BF16),
        compiler_params=_params("parallel"),
        name="context_attention",
    )(sink, q, k, ksw, v, vsw)


OUT_TILE = 512


def _outproj_kernel(attx_ref, attc_ref, yx_ref, yc_ref, xx_ref, xc_ref, wof_ref, g_ref, lng_ref, lnb_ref,
                    sc_ref, sh_ref, wr_ref, br_ref, x1_ref, h2v_ref, route_ref, routet_ref, cnt_ref, wo_ref,
                    *, alpha, n_lat_tiles):
    @pl.when(pl.program_id(0) == 0)
    def _():
        wo_ref[...] = wof_ref[...].astype(BF16)

    is_lat = pl.program_id(0) < n_lat_tiles
    att = jnp.where(is_lat, attx_ref[...], attc_ref[...])
    yloc = jnp.where(is_lat, yx_ref[...], yc_ref[...])
    xin = jnp.where(is_lat, xx_ref[...], xc_ref[...])
    o = _bdot(att, wo_ref[0:ATTN_WIDTH, :]) + _bdot(yloc, wo_ref[ATTN_WIDTH:, :])
    x1 = _layer_norm(alpha * xin + g_ref[...] * o) * lng_ref[...] + lnb_ref[...]
    x1_ref[...] = x1
    h2 = x1 * (1.0 + sc_ref[...]) + sh_ref[...]
    for c in range(ROW_TILE_ROWS):
        h2v_ref[:, c, :, :] = h2[:, c * LANES:(c + 1) * LANES].reshape(h2.shape[0] // SUBLANES, SUBLANES, LANES)

    hh, hl = _split_bf16(h2)
    t1 = _bdot(hh, wr_ref[...])
    lg = t1[:, 0:ROUTER_COLS] + t1[:, ROUTER_COLS:] + _bdot(hl, wr_ref[:, 0:ROUTER_COLS]) + br_ref[...]

    ts = lg.shape[0]
    lane = lax.broadcasted_iota(jnp.int32, (ts, ROUTER_COLS), 1).astype(F32)
    big = jnp.float32(ROUTER_COLS)

    def top1(v):
        m = jnp.max(v, axis=-1, keepdims=True)
        return m, jnp.min(jnp.where(v == m, lane, big), axis=-1, keepdims=True)

    gl = jnp.where(lane < N_GROUPS, lg, NEG_INF)
    g_val, g_idx = top1(gl)
    lse = g_val + jnp.log(jnp.sum(jnp.exp(gl - g_val), axis=-1, keepdims=True))
    p_group = jnp.exp(g_val - lse)
    e_lo = N_GROUPS + EXP_PER_GROUP * g_idx
    el = jnp.where((lane >= e_lo) & (lane < e_lo + EXP_PER_GROUP), lg, NEG_INF)
    e1, l1 = top1(el)
    e2, l2 = top1(jnp.where(lane == l1, NEG_INF, el))
    z = jnp.exp(e2 - e1)
    gate1 = p_group / (1.0 + z)
    gate2 = p_group * z / (1.0 + z)
    x1id = l1 - N_GROUPS
    x2id = l2 - N_GROUPS

    sel1 = lane == x1id
    sel2 = lane == x2id
    onehot = jnp.where(sel1 | sel2, 1.0, 0.0)
    r_i = lax.broadcasted_iota(jnp.int32, (ts, ts), 0)
    c_i = lax.broadcasted_iota(jnp.int32, (ts, ts), 1)
    before = jnp.where(c_i < r_i, 1.0, 0.0).astype(BF16)
    prefix = _bdot(before, onehot.astype(BF16))
    rank1 = jnp.sum(jnp.where(sel1, prefix, 0.0), axis=-1, keepdims=True)
    rank2 = jnp.sum(jnp.where(sel2, prefix, 0.0), axis=-1, keepdims=True)
    route = jnp.zeros((ts, ROUTER_COLS), F32)
    for col, val in enumerate((x1id, x2id, gate1, gate2, rank1, rank2)):
        route = jnp.where(lane == col, val, route)
    route_ref[...] = route
    routet_ref[...] = route.T[0:SUBLANES, :]
    cnt_ref[...] =jnp.broadcast_to(jnp.sum(onehot, axis=0, keepdims=True), (SUBLANES, ROUTER_COLS))


def _outproj_call(att_x, att_c, y_x, y_c, x_arr, x_off, c_arr, c_off, n_lat, n_ctx, seq_len,
                  w_out, layer, gvecs, lng, lnb, scvecs, shvecs, wr, br, alpha):
    ts = OUT_TILE
    d = D_MODEL
    n_lat_tiles = n_lat // ts
    n_tiles = (n_lat + n_ctx) // ts
    tiles_per_seq = seq_len // ts
    n_seq = n_lat // seq_len
    xo, co = x_off // ts, c_off // ts

    def lat_loc(i):
        return (jnp.minimum(i, n_lat_tiles - 1), 0)

    def ctx_loc(i):
        return (jnp.maximum(i - n_lat_tiles, 0), 0)

    def lat_in(i):
        return (xo + jnp.minimum(i, n_lat_tiles - 1), 0)

    def ctx_in(i):
        return (co + jnp.maximum(i - n_lat_tiles, 0), 0)

    def vec_idx(i):
        return (jnp.where(i < n_lat_tiles, i // tiles_per_seq, n_seq), 0, 0)

    vecb = pl.BlockSpec((None, 1, d), vec_idx)
    vec0 = pl.BlockSpec((1, d), lambda i: (0, 0))
    return pl.pallas_call(
        functools.partial(_outproj_kernel, alpha=alpha, n_lat_tiles=n_lat_tiles),
        grid=(n_tiles,),
        in_specs=[
            pl.BlockSpec((ts, ATTN_WIDTH), lat_loc), pl.BlockSpec((ts, ATTN_WIDTH), ctx_loc),
            pl.BlockSpec((ts, LOCAL_WIDTH), lat_loc), pl.BlockSpec((ts, LOCAL_WIDTH), ctx_loc),
            pl.BlockSpec((ts, d), lat_in), pl.BlockSpec((ts, d), ctx_in),
            pl.BlockSpec((None, d, d), lambda i: (layer, 0, 0)),
            vecb, vec0, vec0, vecb, vecb,
            pl.BlockSpec((d, 2 * ROUTER_COLS), lambda i: (0, 0)),
            pl.BlockSpec((1, ROUTER_COLS), lambda i: (0, 0)),
        ],
        out_specs=[pl.BlockSpec((ts, d), lambda i: (i, 0)),
                   pl.BlockSpec((ts // SUBLANES, ROW_TILE_ROWS, SUBLANES, LANES), lambda i: (i, 0, 0, 0)),
                   pl.BlockSpec((ts, ROUTER_COLS), lambda i: (i, 0)),
                   pl.BlockSpec((None, SUBLANES, ts), lambda i: (i, 0, 0)),
                   pl.BlockSpec((None, SUBLANES, ROUTER_COLS), lambda i: (i, 0, 0))],
        out_shape=[jax.ShapeDtypeStruct((n_lat + n_ctx, d), F32),
                   jax.ShapeDtypeStruct(((n_lat + n_ctx) // SUBLANES, ROW_TILE_ROWS, SUBLANES, LANES), F32),
                   jax.ShapeDtypeStruct((n_lat + n_ctx, ROUTER_COLS), F32),
                   jax.ShapeDtypeStruct((n_tiles, SUBLANES, ts), F32),
                   jax.ShapeDtypeStruct((n_tiles, SUBLANES, ROUTER_COLS), F32)],
        scratch_shapes=[pltpu.VMEM((d, d), BF16)],
        compiler_params=_params("arbitrary"),
        name="out_projection",
    )(att_x, att_c, y_x, y_c, x_arr, c_arr, w_out, gvecs, lng, lnb, scvecs, shvecs, wr, br)


DISPATCH_TILE = 512


def _dispatch_kernel(seg_ref, pos_ref, h_ref, xs_out, zbuf, zsem, sem):
    @pl.when(pl.program_id(0) == 0)
    def _():
        zbuf[...] = jnp.zeros_like(zbuf)

        def zero_copy(e):
            return pltpu.make_async_copy(zbuf, xs_out.at[pl.ds(seg_ref[0, e] - MOE_BLOCK, MOE_BLOCK)], zsem)

        for e in range(N_EXPERTS):
            @pl.when(seg_ref[1, e] > 0)
            def _():
                zero_copy(e).start()
        for e in range(N_EXPERTS):
            @pl.when(seg_ref[1, e] > 0)
            def _():
                zero_copy(e).wait()

        def tail_copy(b):
            return pltpu.make_async_copy(zbuf, xs_out.at[pl.ds(b * MOE_BLOCK, MOE_BLOCK)], zsem)

        n_blocks = xs_out.shape[0] // MOE_BLOCK
        first_unused = seg_ref[0, N_EXPERTS - 1] // MOE_BLOCK
        lax.fori_loop(first_unused, n_blocks, lambda b, c: (tail_copy(b).start(), c)[1], 0)
        lax.fori_loop(first_unused, n_blocks, lambda b, c: (tail_copy(b).wait(), c)[1], 0)

    def body(j, carry):
        for s in range(SUBLANES):
            t = SUBLANES * j + s
            for k in range(TOP_K):
                pltpu.make_async_copy(h_ref.at[j, :, s, :], xs_out.at[pos_ref[0, 0, k * DISPATCH_TILE + t]],
                                      sem).start(priority=k)
        return carry

    lax.fori_loop(0, DISPATCH_TILE // SUBLANES, body, 0)
    for k in range(TOP_K):
        pltpu.make_async_copy(h_ref, h_ref, sem).wait()


def _dispatch_call(seg, pos, h2t, n_slots):
    n_tok = h2t.shape[0] * SUBLANES
    n_tiles = n_tok // DISPATCH_TILE
    row = (ROW_TILE_ROWS, LANES)
    grid_spec = pltpu.PrefetchScalarGridSpec(
        num_scalar_prefetch=1,
        grid=(n_tiles,),
        in_specs=[
            pl.BlockSpec((1, 1, TOP_K * DISPATCH_TILE), lambda i, seg: (i, 0, 0), memory_space=pltpu.SMEM),
            pl.BlockSpec((DISPATCH_TILE // SUBLANES, ROW_TILE_ROWS, SUBLANES, LANES), lambda i, seg: (i, 0, 0, 0)),
        ],
        out_specs=pl.BlockSpec(memory_space=pl.ANY),
        scratch_shapes=[pltpu.VMEM((MOE_BLOCK,) + row, F32), pltpu.SemaphoreType.DMA(()),
                        pltpu.SemaphoreType.DMA(())],
    )
    return pl.pallas_call(
        _dispatch_kernel,
        grid_spec=grid_spec,
        out_shape=jax.ShapeDtypeStruct((n_slots,) + row, F32),
        compiler_params=_params("arbitrary"),
        name="moe_dispatch",
    )(seg, pos, h2t)


def _moe_kernel(be_ref, nu_ref, xs_hbm, w1_ref, w3_ref, w2_ref, ys_hbm, w1b, w3b, w2b, xbuf, obuf, isem, osem):
    i = pl.program_id(0)
    n_used = nu_ref[0]
    slot = i % 2
    groups = MOE_BLOCK // SUBLANES

    def in_copies(blk, at_slot):
        return [pltpu.make_async_copy(xs_hbm.at[pl.ds(blk * groups, groups), s], xbuf.at[at_slot, :, :, s, :],
                                      isem.at[at_slot]) for s in range(SUBLANES)]

    def out_copies(blk, at_slot):
        return [pltpu.make_async_copy(obuf.at[at_slot, :, :, s, :], ys_hbm.at[pl.ds(blk * groups, groups), s],
                                      osem.at[at_slot]) for s in range(SUBLANES)]

    def write_rows(at_slot, y):
        for c in range(ROW_TILE_ROWS):
            obuf[at_slot, :, c, :, :] = y[:, c * LANES:(c + 1) * LANES].reshape(groups, SUBLANES, LANES)

    @pl.when(i == 0)
    def _():
        _start_all(in_copies(0, 0))

    @pl.when(i + 1 < n_used)
    def _():
        _start_all(in_copies(i + 1, 1 - slot))

    @pl.when((i == 0) | (be_ref[i] != be_ref[jnp.maximum(i - 1, 0)]))
    def _():
        w1b[...] = w1_ref[...].astype(BF16)
        w3b[...] = w3_ref[...].astype(BF16)
        w2b[...] = w2_ref[...].astype(BF16)

    @pl.when(i < n_used)
    def _():
        _wait_all(in_copies(i, slot))
        x = jnp.concatenate([xbuf[slot, :, c, :, :].reshape(MOE_BLOCK, LANES).astype(BF16)
                             for c in range(ROW_TILE_ROWS)], axis=1)
        write_rows(slot, _bdot((jax.nn.silu(_bdot(x, w1b[...])) * _bdot(x, w3b[...])).astype(BF16), w2b[...]))
        _start_all(out_copies(i, slot))

        @pl.when(i >= 1)
        def _():
            _wait_all(out_copies(i - 1, 1 - slot))

        @pl.when(i == n_used - 1)
        def _():
            _wait_all(out_copies(i, slot))

    @pl.when(i >= n_used)
    def _():
        obuf[slot] = jnp.zeros(obuf.shape[1:], F32)
        _start_all(out_copies(i, slot))
        _wait_all(out_copies(i, slot))


def _moe_call(block_e, n_used, xs, w1, w3, w2, layer):
    n_blocks = block_e.shape[0]
    n_slots = n_blocks * MOE_BLOCK
    d = D_MODEL
    buf_view = (MOE_BLOCK // SUBLANES, ROW_TILE_ROWS, SUBLANES, LANES)
    grid_spec = pltpu.PrefetchScalarGridSpec(
        num_scalar_prefetch=2,
        grid=(n_blocks,),
        in_specs=[
            pl.BlockSpec(memory_space=pl.ANY),
            pl.BlockSpec((None, None, d, D_EXPERT), lambda i, be, nu: (layer, be[i], 0, 0)),
            pl.BlockSpec((None, None, d, D_EXPERT), lambda i, be, nu: (layer, be[i], 0, 0)),
            pl.BlockSpec((None, None, D_EXPERT, d), lambda i, be, nu: (layer, be[i], 0, 0)),
        ],
        out_specs=pl.BlockSpec(memory_space=pl.ANY),
        scratch_shapes=[pltpu.VMEM((d, D_EXPERT), BF16), pltpu.VMEM((d, D_EXPERT), BF16),
                        pltpu.VMEM((D_EXPERT, d), BF16),
                        pltpu.VMEM((2,) + buf_view, F32), pltpu.VMEM((2,) + buf_view, F32),
                        pltpu.SemaphoreType.DMA((2,)), pltpu.SemaphoreType.DMA((2,))],
    )
    slots_view = (n_slots // SUBLANES, SUBLANES, ROW_TILE_ROWS, LANES)
    ys = pl.pallas_call(
        _moe_kernel,
        grid_spec=grid_spec,
        out_shape=jax.ShapeDtypeStruct(slots_view, F32),
        compiler_params=_params("arbitrary"),
        name="moe_experts",
    )(block_e, n_used, xs.reshape(slots_view), w1, w3, w2)
    return ys.reshape(n_slots, ROW_TILE_ROWS, LANES)


COMBINE_TILE = 256


def _combine_kernel(pos0_ref, posn_ref, x1_ref, route_ref, g_ref, lng_ref, lnb_ref, ys_hbm, o_ref, ybuf, sem,
                    *, alpha):
    i = pl.program_id(0)
    nb = pl.num_programs(0)
    n_rows = TOP_K * COMBINE_TILE

    def issue(pos_ref, slot):
        def body(j, carry):
            for s in range(SUBLANES):
                pltpu.make_async_copy(ys_hbm.at[pos_ref[0, 0, SUBLANES * j + s]], ybuf.at[slot, j, :, s, :],
                                      sem.at[slot]).start(priority=s % N_DMA_PRIORITIES)
            return carry
        lax.fori_loop(0, n_rows // SUBLANES, body, 0)

    @pl.when(i == 0)
    def _():
        issue(pos0_ref, 0)

    @pl.when(i + 1 < nb)
    def _():
        issue(posn_ref, (i + 1) % 2)

    slot = i % 2
    pltpu.make_async_copy(ybuf.at[slot], ybuf.at[slot], sem.at[slot]).wait()
    yrows = jnp.concatenate([ybuf[slot, :, c, :, :].reshape(n_rows, LANES) for c in range(ROW_TILE_ROWS)], axis=1)
    route = route_ref[...]
    y = (yrows[0:COMBINE_TILE, :] * route[:, ROUTE_GATE:ROUTE_GATE + 1]
         + yrows[COMBINE_TILE:, :] * route[:, ROUTE_GATE + 1:ROUTE_GATE + 2])
    o_ref[...] = _layer_norm(alpha * x1_ref[...] + g_ref[...] * y) * lng_ref[...] + lnb_ref[...]


def _combine_call(pos, x1, route, gvecs, lng, lnb, ys, n_tok, tiles_per_seq, n_seq, alpha):
    d = D_MODEL
    n_tiles = n_tok // COMBINE_TILE
    n_lat_tiles = tiles_per_seq * n_seq

    def g_idx(i):
        return (jnp.where(i < n_lat_tiles, i // tiles_per_seq, n_seq), 0, 0)

    smem_blk = (1, 1, TOP_K * COMBINE_TILE)
    return pl.pallas_call(
        functools.partial(_combine_kernel, alpha=alpha),
        grid=(n_tiles,),
        in_specs=[
            pl.BlockSpec(smem_blk, lambda i: (0, 0, 0), memory_space=pltpu.SMEM),
            pl.BlockSpec(smem_blk, lambda i: (jnp.minimum(i + 1, n_tiles - 1), 0, 0), memory_space=pltpu.SMEM),
            pl.BlockSpec((COMBINE_TILE, d), lambda i: (i, 0)),
            pl.BlockSpec((COMBINE_TILE, ROUTER_COLS), lambda i: (i, 0)),
            pl.BlockSpec((None, 1, d), g_idx),
            pl.BlockSpec((1, d), lambda i: (0, 0)),
            pl.BlockSpec((1, d), lambda i: (0, 0)),
            pl.BlockSpec(memory_space=pl.ANY),
        ],
        out_specs=pl.BlockSpec((COMBINE_TILE, d), lambda i: (i, 0)),
        out_shape=jax.ShapeDtypeStruct((n_tok, d), F32),
        scratch_shapes=[pltpu.VMEM((2, TOP_K * COMBINE_TILE // SUBLANES, ROW_TILE_ROWS, SUBLANES, LANES), F32),
                        pltpu.SemaphoreType.DMA((2,))],
        compiler_params=_params("arbitrary"),
        name="moe_combine",
    )(pos, pos, x1, route, gvecs, lng, lnb, ys)


def _slots(route_t, tile_counts):
    n_tiles = tile_counts.shape[0]
    n = n_tiles * route_t.shape[2]
    tc = tile_counts.astype(jnp.int32)
    counts = jnp.sum(tc, axis=0)
    padded = (counts + MOE_BLOCK - 1) // MOE_BLOCK * MOE_BLOCK
    pends = jnp.cumsum(padded)
    base = (pends - padded)[None, :] + jnp.cumsum(tc, axis=0) - tc
    n_blocks = -(-n * TOP_K // MOE_BLOCK) + N_EXPERTS
    block_start = jnp.arange(n_blocks, dtype=jnp.int32) * MOE_BLOCK
    block_e = jnp.minimum(jnp.sum(pends[None, 0:N_EXPERTS] <= block_start[:, None], axis=1),
                          N_EXPERTS - 1).astype(jnp.int32)
    n_used = (pends[N_EXPERTS - 1] // MOE_BLOCK).astype(jnp.int32).reshape(1)
    seg = jnp.stack([pends[0:N_EXPERTS], padded[0:N_EXPERTS]]).astype(jnp.int32)
    pos = []
    for k in range(TOP_K):
        e = route_t[:, ROUTE_EXPERT + k, :].astype(jnp.int32)
        b = jnp.zeros_like(e)
        for x in range(N_EXPERTS):
            b = jnp.where(e == x, base[:, x:x + 1], b)
        pos.append((b + route_t[:, ROUTE_RANK + k, :].astype(jnp.int32)).reshape(n))
    return block_e, n_used, seg, pos


def _rope_tables(seq_len):
    m = HEAD_DIM // 4
    freqs = ROPE_BASE ** (-jnp.arange(m, dtype=F32) / m)
    t = jnp.arange(seq_len)
    row = (t // GRID_W).astype(F32)[:, None] * freqs[None, :]
    col = (t % GRID_W).astype(F32)[:, None] * freqs[None, :]
    cos = jnp.concatenate([jnp.cos(row), jnp.cos(row), jnp.cos(col), jnp.cos(col)], axis=-1)
    zero = jnp.zeros_like(row)
    sin_a = jnp.concatenate([zero, jnp.sin(row), zero, jnp.sin(col)], axis=-1)
    sin_b = jnp.concatenate([-jnp.sin(row), zero, -jnp.sin(col), zero], axis=-1)
    rep = LANES // HEAD_DIM
    return jnp.tile(cos, (1, rep)), jnp.tile(sin_a, (1, rep)), jnp.tile(sin_b, (1, rep))


def kernel(x, c, ctx, c_ctx, w_ada, b_ada, w_in, conv_w, attn_sink, gm_ws, gm_bs, w_out, ln1_g, ln1_b,
           w_rg, b_rg, w_re, b_re, w1, w3, w2, ln2_g, ln2_b):
    b_, s_, d_ = x.shape
    c_len = ctx.shape[1]
    depth = w_ada.shape[0]
    alpha = (2 * depth) ** 0.25
    n_lat = b_ * s_
    n_ctx = b_ * c_len
    ts = 1024

    cin = jnp.zeros((ADA_ROWS, d_), F32).at[0:b_].set(c).at[b_].set(c_ctx)
    mod = _ada_call(cin, w_ada, b_ada)

    cos, sin_a, sin_b = _rope_tables(s_)
    ones_c = jnp.ones((c_len, LANES), F32)
    zeros_c = jnp.zeros((c_len, LANES), F32)

    x_flat = x.reshape(n_lat, d_)
    c_flat = ctx.reshape(n_ctx, d_)
    x_off, c_off = 0, 0
    x_arr, c_arr = x_flat, c_flat

    for l in range(depth):
        last = l == depth - 1
        mx = mod[l, 0:b_].reshape(b_, 6, 1, d_)
        sh1, sc1, g1, sh2, sc2, g2 = (mx[:, i] for i in range(6))
        mc = jnp.broadcast_to(mod[l, b_].reshape(1, 6, 1, d_), (b_, 6, 1, d_))
        gm_w = gm_ws[l].astype(BF16)
        gm_b = jnp.repeat(gm_bs[l].T, GM_HEAD, axis=1)
        sink = attn_sink[l]

        qx, kx, kxs, vx, vxs, yx = _inproj_call(x_arr, x_off, b_, s_, ts, sc1, sh1, w_in, l, cos, sin_a, sin_b,
                                                conv_w[l], gm_w, gm_b)
        qc, kc, kcs, vc, vcs, yc = _inproj_call(c_arr, c_off, b_, c_len, c_len, mc[:, 1], mc[:, 0], w_in, l,
                                                ones_c, zeros_c, zeros_c, conv_w[l], gm_w, gm_b)
        att_x = _win_attn_call(sink, qx, kx, kxs, vx, vxs, kc, kcs, vc, vcs, b_, s_, c_len)

        w_r = jnp.zeros((d_, ROUTER_COLS), F32).at[:, 0:N_GROUPS].set(w_rg[l]) \
            .at[:, N_GROUPS:N_GROUPS + N_EXPERTS].set(w_re[l])
        w_r_hi = w_r.astype(BF16)
        w_r_lo = (w_r - w_r_hi.astype(F32)).astype(BF16)
        wr = jnp.concatenate([w_r_hi, w_r_lo], axis=1)
        br = jnp.zeros((1, ROUTER_COLS), F32).at[0, 0:N_GROUPS].set(b_rg[l]) \
            .at[0, N_GROUPS:N_GROUPS + N_EXPERTS].set(b_re[l])
        lng1, lnb1 = ln1_g[l].reshape(1, d_), ln1_b[l].reshape(1, d_)
        lng2, lnb2 = ln2_g[l].reshape(1, d_), ln2_b[l].reshape(1, d_)

        n_tot = n_lat if last else n_lat + n_ctx
        gvecs1 = jnp.concatenate([g1, mc[0:1, 2]], axis=0)
        scvecs2 = jnp.concatenate([sc2, mc[0:1, 4]], axis=0)
        shvecs2 = jnp.concatenate([sh2, mc[0:1, 3]], axis=0)
        if last:
            x1, h2t, route, route_t, tcnt = _outproj_call(att_x, att_x, yx, yx, x_arr, x_off, x_arr, x_off, n_lat, 0, s_,
                                                 w_out, l, gvecs1, lng1, lnb1, scvecs2, shvecs2, wr, br, alpha)
        else:
            att_c = _ctx_attn_call(sink, qc, kc, kcs, vc, vcs, b_, c_len)
            x1, h2t, route, route_t, tcnt = _outproj_call(att_x, att_c, yx, yc, x_arr, x_off, c_arr, c_off, n_lat, n_ctx,
                                                 s_, w_out, l, gvecs1, lng1, lnb1, scvecs2, shvecs2, wr, br, alpha)

        block_e, n_used, seg, pos = _slots(route_t, tcnt[:, 0, :])
        n_slots = block_e.shape[0] * MOE_BLOCK

        def per_tile(tile):
            return jnp.concatenate([p.reshape(n_tot // tile, 1, tile) for p in pos], axis=2)

        xs = _dispatch_call(seg, per_tile(DISPATCH_TILE), h2t, n_slots)
        ys = _moe_call(block_e, n_used, xs, w1, w3, w2, l)

        gvecs = jnp.concatenate([g2, mc[0:1, 5]], axis=0)
        out = _combine_call(per_tile(COMBINE_TILE), x1, route, gvecs, lng2, lnb2, ys, n_tot, s_ // COMBINE_TILE,
                            b_, alpha)
        x_arr, x_off = out, 0
        c_arr, c_off = out, n_lat

    return x_arr.reshape(b_, s_, d_)
```

```python
import functools

import jax
import jax.numpy as jnp
from jax import lax
from jax.experimental import pallas as pl
from jax.experimental.pallas import tpu as pltpu

F32 = jnp.float32
BF16 = jnp.bfloat16

D_MODEL = 1024
GRID_W = 64
HEAD_DIM = 64
N_HEADS = 8
N_KV_HEADS = 2
ATTN_WIDTH = N_HEADS * HEAD_DIM
KV_WIDTH = N_KV_HEADS * HEAD_DIM
WINDOW = 128
ATTN_SCALE = HEAD_DIM ** -0.5
LOG2E = 1.4426950408889634
ROPE_BASE = 10000.0
CONV_WIDTH = D_MODEL // 4
GM_WIDTH = D_MODEL // 4
GM_GROUPS = 4
GM_HEAD = GM_WIDTH // GM_GROUPS
CHUNK = 128
LOCAL_WIDTH = CONV_WIDTH + GM_WIDTH
IN_WIDTH = ATTN_WIDTH + 2 * KV_WIDTH + 3 * CONV_WIDTH + 2 * GM_WIDTH
QKV_WIDTH = ATTN_WIDTH + 2 * KV_WIDTH
N_GROUPS = 4
EXP_PER_GROUP = 8
N_EXPERTS = N_GROUPS * EXP_PER_GROUP
TOP_K = 2
D_EXPERT = D_MODEL // 2
MOE_BLOCK = 512
LN_EPS = 1e-6
NEG_INF = -1e30

LANES = 128
SUBLANES = 8
VMEM_LIMIT_BYTES = 48 * 1024 * 1024
N_DMA_PRIORITIES = 2

ROW_TILE_ROWS = D_MODEL // LANES
assert ROW_TILE_ROWS == SUBLANES
PACKED_TILE_ROWS = ROW_TILE_ROWS // 2
TOKENS_PER_PACKED_TILE = SUBLANES // PACKED_TILE_ROWS
assert TOKENS_PER_PACKED_TILE == 2

ROUTER_COLS = LANES
ROUTE_EXPERT = 0
ROUTE_GATE = TOP_K
ROUTE_RANK = 2 * TOP_K
ROPE_HALF_PAIR = HEAD_DIM // 4


def _bdot(a, b):
    return jnp.dot(a, b, preferred_element_type=F32)


def _split_bf16(a):
    hi = a.astype(BF16)
    lo = (a - hi.astype(F32)).astype(BF16)
    return hi, lo


def _layer_norm(r):
    mu = jnp.mean(r, axis=-1, keepdims=True)
    d = r - mu
    var = jnp.mean(d * d, axis=-1, keepdims=True)
    return d * lax.rsqrt(var + LN_EPS)


def _params(*sem, flags=None):
    return pltpu.CompilerParams(dimension_semantics=sem, vmem_limit_bytes=VMEM_LIMIT_BYTES, flags=flags)


def _start_all(copies):
    for cp in copies:
        cp.start()


def _wait_all(copies):
    for cp in copies:
        cp.wait()


ADA_ROWS = 16
ADA_TILE = 1536


def _ada_kernel(c_ref, w_ref, b_ref, o_ref):
    act = jax.nn.silu(c_ref[...])
    ah, al = _split_bf16(act)
    wh, wl = _split_bf16(w_ref[...])
    o_ref[...] = _bdot(ah, wh) + _bdot(ah, wl) + _bdot(al, wh) + b_ref[...]


def _ada_call(cin, w_ada, b_ada):
    depth, d, n = w_ada.shape
    return pl.pallas_call(
        _ada_kernel,
        grid=(depth, n // ADA_TILE),
        in_specs=[
            pl.BlockSpec((ADA_ROWS, d), lambda l, j: (0, 0)),
            pl.BlockSpec((None, d, ADA_TILE), lambda l, j: (l, 0, j)),
            pl.BlockSpec((None, 1, ADA_TILE), lambda l, j: (l, 0, j)),
        ],
        out_specs=pl.BlockSpec((None, ADA_ROWS, ADA_TILE), lambda l, j: (l, 0, j)),
        out_shape=jax.ShapeDtypeStruct((depth, ADA_ROWS, n), F32),
        compiler_params=_params("parallel", "parallel"),
        name="ada_modulation",
    )(cin, w_ada, b_ada.reshape(depth, 1, n))


def _inproj_kernel(x_ref, xp_ref, xn_ref, sc_ref, sh_ref, wf_ref, cos_ref, sa_ref, sb_ref,
                   cw_ref, gw_ref, gb_ref,
                   q_ref, k_ref, ksw_ref, v_ref, vsw_ref, y_ref, w_ref, *, ts):
    t = pl.program_id(1)
    nt = pl.num_programs(1)

    @pl.when((pl.program_id(0) == 0) & (t == 0))
    def _():
        w_ref[...] = wf_ref[...].astype(BF16)

    sc = 1.0 + sc_ref[...]
    sh = sh_ref[...]
    hx = (x_ref[...] * sc + sh).astype(BF16)

    pq = _bdot(hx, w_ref[:, 0:QKV_WIDTH])
    cos = cos_ref[...]
    sa = sa_ref[...]
    sb = sb_ref[...]

    def rope(z):
        return (z * cos + pltpu.roll(z, ROPE_HALF_PAIR, 1) * sa
                + pltpu.roll(z, LANES - ROPE_HALF_PAIR, 1) * sb)

    for j in range(ATTN_WIDTH // LANES):
        sl = slice(j * LANES, (j + 1) * LANES)
        q_ref[:, sl] = (rope(pq[:, sl]) * (ATTN_SCALE * LOG2E)).astype(BF16)
    kr = rope(pq[:, ATTN_WIDTH:ATTN_WIDTH + KV_WIDTH])
    k_ref[...] = kr.astype(BF16)
    ksw_ref[...] = pltpu.roll(kr, HEAD_DIM, 1).astype(BF16)
    vv = pq[:, ATTN_WIDTH + KV_WIDTH:QKV_WIDTH]
    v_ref[...] = vv.astype(BF16)
    vsw_ref[...] = pltpu.roll(vv, HEAD_DIM, 1).astype(BF16)

    pm = _bdot(hx, w_ref[:, QKV_WIDTH:IN_WIDTH])
    cb = pm[:, 0:CONV_WIDTH]
    u = pm[:, CONV_WIDTH:2 * CONV_WIDTH] * pm[:, 2 * CONV_WIDTH:3 * CONV_WIDTH]

    halo = jnp.concatenate([xp_ref[...], xn_ref[...]], axis=0)
    hh = (halo * sc + sh).astype(BF16)
    ph = _bdot(hh, w_ref[:, QKV_WIDTH + CONV_WIDTH:QKV_WIDTH + 3 * CONV_WIDTH])
    uh = ph[:, 0:CONV_WIDTH] * ph[:, CONV_WIDTH:2 * CONV_WIDTH]
    up_row = jnp.where(t > 0, uh[SUBLANES - 1:SUBLANES, :], 0.0)
    dn_row = jnp.where(t < nt - 1, uh[SUBLANES:SUBLANES + 1, :], 0.0)
    ridx = lax.broadcasted_iota(jnp.int32, (ts, CONV_WIDTH), 0)
    u_up = jnp.where(ridx == 0, up_row, pltpu.roll(u, 1, 0))
    u_dn = jnp.where(ridx == ts - 1, dn_row, pltpu.roll(u, ts - 1, 0))
    cw = cw_ref[...]
    y_conv = cb * (u_up * cw[0:1, :] + u * cw[1:2, :] + u_dn * cw[2:3, :])
    y_ref[:, 0:CONV_WIDTH] = y_conv.astype(BF16)

    gu = jax.nn.gelu(pm[:, 3 * CONV_WIDTH:3 * CONV_WIDTH + GM_WIDTH])
    gv = _layer_norm(jax.nn.gelu(pm[:, 3 * CONV_WIDTH + GM_WIDTH:3 * CONV_WIDTH + 2 * GM_WIDTH])).astype(BF16)
    lane = lax.broadcasted_iota(jnp.int32, (CHUNK, LANES), 1)
    zero = jnp.zeros((CHUNK, LANES), BF16)
    for c in range(ts // CHUNK):
        rows = slice(c * CHUNK, (c + 1) * CHUNK)
        for j in range(GM_WIDTH // LANES):
            cols = slice(j * LANES, (j + 1) * LANES)
            vp = gv[rows, cols]
            s = (_bdot(gw_ref[2 * j], jnp.where(lane < GM_HEAD, vp, zero))
                 + _bdot(gw_ref[2 * j + 1], jnp.where(lane >= GM_HEAD, vp, zero))
                 + gb_ref[:, cols])
            y_ref[rows, CONV_WIDTH + j * LANES:CONV_WIDTH + (j + 1) * LANES] = (gu[rows, cols] * s).astype(BF16)


def _inproj_call(x2d, row_off, n_seq, seq_len, ts, sc, sh, w_in, layer, cos, sa, sb, conv_w, gm_w, gm_b):
    nt = seq_len // ts
    off_t = row_off // ts
    off_8 = row_off // SUBLANES
    last_8 = x2d.shape[0] // SUBLANES - 1
    per_tile_8 = ts // SUBLANES
    n_out = n_seq * seq_len
    d = D_MODEL

    def tile_idx(b, t):
        return (off_t + b * nt + t, 0)

    def prev_idx(b, t):
        return (jnp.maximum(off_8 + (b * nt + t) * per_tile_8 - 1, 0), 0)

    def next_idx(b, t):
        return (jnp.minimum(off_8 + (b * nt + t + 1) * per_tile_8, last_8), 0)

    def out_idx(b, t):
        return (b * nt + t, 0)

    vec = pl.BlockSpec((None, 1, d), lambda b, t: (b, 0, 0))
    rope_spec = pl.BlockSpec((ts, LANES), lambda b, t: (t, 0))
    kv_spec = pl.BlockSpec((ts, KV_WIDTH), out_idx)
    kv_shape = jax.ShapeDtypeStruct((n_out, KV_WIDTH), BF16)
    return pl.pallas_call(
        functools.partial(_inproj_kernel, ts=ts),
        grid=(n_seq, nt),
        in_specs=[
            pl.BlockSpec((ts, d), tile_idx),
            pl.BlockSpec((SUBLANES, d), prev_idx),
            pl.BlockSpec((SUBLANES, d), next_idx),
            vec, vec,
            pl.BlockSpec((None, d, IN_WIDTH), lambda b, t: (layer, 0, 0)),
            rope_spec, rope_spec, rope_spec,
            pl.BlockSpec((3, CONV_WIDTH), lambda b, t: (0, 0)),
            pl.BlockSpec((GM_GROUPS, CHUNK, CHUNK), lambda b, t: (0, 0, 0)),
            pl.BlockSpec((CHUNK, GM_WIDTH), lambda b, t: (0, 0)),
        ],
        out_specs=[
            pl.BlockSpec((ts, ATTN_WIDTH), out_idx),
            kv_spec, kv_spec, kv_spec, kv_spec,
            pl.BlockSpec((ts, LOCAL_WIDTH), out_idx),
        ],
        out_shape=[
            jax.ShapeDtypeStruct((n_out, ATTN_WIDTH), BF16),
            kv_shape, kv_shape, kv_shape, kv_shape,
            jax.ShapeDtypeStruct((n_out, LOCAL_WIDTH), BF16),
        ],
        scratch_shapes=[pltpu.VMEM((d, IN_WIDTH), BF16)],
        compiler_params=_params("arbitrary", "arbitrary"),
        name="in_projection",
    )(x2d, x2d, x2d, sc, sh, w_in, cos, sa, sb, conv_w, gm_w, gm_b)


def _attn_body(q, keys, keys_sw, vals, vals_sw, halo_bias, sink_ref, o_ref, tq):
    nk = keys.shape[0]
    lane = lax.broadcasted_iota(jnp.int32, (nk, LANES), 1)
    lo = lane < HEAD_DIM
    zero = jnp.zeros((nk, LANES), BF16)
    rid = lax.broadcasted_iota(jnp.int32, (2 * tq, 1), 0)
    out_lo = lax.broadcasted_iota(jnp.int32, (2 * tq, LANES), 1) < HEAD_DIM
    nt_dims = (((1,), (1,)), ((), ()))
    gqa = N_HEADS // N_KV_HEADS
    biases = [None] * (nk // LANES)
    if halo_bias is not None:
        assert tq == LANES
        biases[0], biases[2] = halo_bias
    for h in range(N_KV_HEADS):
        k_own, k_oth = (keys, keys_sw) if h == 0 else (keys_sw, keys)
        v_own, v_oth = (vals, vals_sw) if h == 0 else (vals_sw, vals)
        kz = jnp.concatenate([jnp.where(lo, k_own, zero), jnp.where(lo, zero, k_oth)], axis=0)
        vz = jnp.concatenate([jnp.where(lo, v_own, zero), jnp.where(lo, zero, v_oth)], axis=0)
        c0 = h * gqa * HEAD_DIM
        qs = jnp.concatenate([q[:, c0:c0 + LANES], q[:, c0 + LANES:c0 + 2 * LANES]], axis=0)
        s_all = lax.dot_general(qs, kz, nt_dims, preferred_element_type=F32)
        probs, inv_den = [], []
        for par in range(2):
            tiles = []
            for j, bias in enumerate(biases):
                tile = s_all[:, par * nk + j * LANES:par * nk + (j + 1) * LANES]
                tiles.append(tile if bias is None else tile + bias)
            sink = jnp.where(rid < tq, sink_ref[h * gqa + par], sink_ref[h * gqa + 2 + par]) * LOG2E
            tile_max = tiles[0]
            for tile in tiles[1:]:
                tile_max = jnp.maximum(tile_max, tile)
            m = jnp.maximum(jnp.max(tile_max, axis=-1, keepdims=True), sink)
            tile_sum = None
            for tile in tiles:
                p = jnp.exp2(tile - m)
                tile_sum = p if tile_sum is None else tile_sum + p
                probs.append(p.astype(BF16))
            inv_den.append(1.0 / (jnp.sum(tile_sum, axis=-1, keepdims=True) + jnp.exp2(sink - m)))
        o = _bdot(jnp.concatenate(probs, axis=1), vz) * jnp.where(out_lo, inv_den[0], inv_den[1])
        o_ref[:, c0:c0 + LANES] = o[0:tq].astype(BF16)
        o_ref[:, c0 + LANES:c0 + 2 * LANES] = o[tq:2 * tq].astype(BF16)


def _win_attn_kernel(sink_ref, q_ref, kp_ref, kc_ref, kn_ref, kx_ref,
                     ksp_ref, ksc_ref, ksn_ref, ksx_ref,
                     vp_ref, vc_ref, vn_ref, vx_ref,
                     vsp_ref, vsc_ref, vsn_ref, vsx_ref, o_ref, *, tq, ctx_len):
    n = pl.program_id(1)
    nb = pl.num_programs(1)
    row = lax.broadcasted_iota(jnp.int32, (tq, tq), 0)
    col = lax.broadcasted_iota(jnp.int32, (tq, tq), 1)
    b_prev = jnp.where((col >= row) & (n > 0), 0.0, NEG_INF)
    b_next = jnp.where((col <= row) & (n < nb - 1), 0.0, NEG_INF)
    halo_bias = (jnp.concatenate([b_prev, b_prev], axis=0), jnp.concatenate([b_next, b_next], axis=0))

    def cat(a, b, c, d):
        return jnp.concatenate([a[...], b[...], c[...], d[...]], axis=0)

    _attn_body(q_ref[...], cat(kp_ref, kc_ref, kn_ref, kx_ref), cat(ksp_ref, ksc_ref, ksn_ref, ksx_ref),
               cat(vp_ref, vc_ref, vn_ref, vx_ref), cat(vsp_ref, vsc_ref, vsn_ref, vsx_ref),
               halo_bias, sink_ref, o_ref, tq)


def _win_attn_call(sink, q, k, ksw, v, vsw, kx, kxsw, vx, vxsw, n_seq, seq_len, ctx_len):
    tq = WINDOW
    nb = seq_len // tq

    def cur(b, n):
        return (b * nb + n, 0)

    def prev(b, n):
        return (b * nb + jnp.maximum(n - 1, 0), 0)

    def nxt(b, n):
        return (b * nb + jnp.minimum(n + 1, nb - 1), 0)

    def ctx(b, n):
        return (b, 0)

    def kv_specs():
        return [pl.BlockSpec((tq, KV_WIDTH), prev), pl.BlockSpec((tq, KV_WIDTH), cur),
                pl.BlockSpec((tq, KV_WIDTH), nxt), pl.BlockSpec((ctx_len, KV_WIDTH), ctx)]

    return pl.pallas_call(
        functools.partial(_win_attn_kernel, tq=tq, ctx_len=ctx_len),
        grid=(n_seq, nb),
        in_specs=[pl.BlockSpec(memory_space=pltpu.SMEM), pl.BlockSpec((tq, ATTN_WIDTH), cur)]
        + kv_specs() + kv_specs() + kv_specs() + kv_specs(),
        out_specs=pl.BlockSpec((tq, ATTN_WIDTH), cur),
        out_shape=jax.ShapeDtypeStruct((n_seq * seq_len, ATTN_WIDTH), BF16),
        compiler_params=_params("parallel", "parallel"),
        name="window_attention",
    )(sink, q, k, k, k, kx, ksw, ksw, ksw, kxsw, v, v, v, vx, vsw, vsw, vsw, vxsw)


def _ctx_attn_kernel(sink_ref, q_ref, k_ref, ks_ref, v_ref, vs_ref, o_ref, *, tq):
    _attn_body(q_ref[...], k_ref[...], ks_ref[...], v_ref[...], vs_ref[...], None, sink_ref, o_ref, tq)


def _ctx_attn_call(sink, q, k, ksw, v, vsw, n_seq, ctx_len):
    def blk(w):
        return pl.BlockSpec((ctx_len, w), lambda b: (b, 0))

    return pl.pallas_call(
        functools.partial(_ctx_attn_kernel, tq=ctx_len),
        grid=(n_seq,),
        in_specs=[pl.BlockSpec(memory_space=pltpu.SMEM), blk(ATTN_WIDTH),
                  blk(KV_WIDTH), blk(KV_WIDTH), blk(KV_WIDTH), blk(KV_WIDTH)],
        out_specs=blk(ATTN_WIDTH),
        out_shape=jax.ShapeDtypeStruct((n_seq * ctx_len, ATTN_WIDTH), BF16),
        compiler_params=_params("parallel"),
        name="context_attention",
    )(sink, q, k, ksw, v, vsw)


OUT_TILE = 512


def _outproj_kernel(attx_ref, attc_ref, yx_ref, yc_ref, xx_ref, xc_ref, wof_ref, g_ref, lng_ref, lnb_ref,
                    sc_ref, sh_ref, wr_ref, br_ref, x1_ref, h2v_ref, route_ref, routet_ref, cnt_ref, wo_ref,
                    *, alpha, n_lat_tiles):
    @pl.when(pl.program_id(0) == 0)
    def _():
        wo_ref[...] = wof_ref[...].astype(BF16)

    is_lat = pl.program_id(0) < n_lat_tiles
    att = jnp.where(is_lat, attx_ref[...], attc_ref[...])
    yloc = jnp.where(is_lat, yx_ref[...], yc_ref[...])
    xin = jnp.where(is_lat, xx_ref[...], xc_ref[...])
    o = _bdot(att, wo_ref[0:ATTN_WIDTH, :]) + _bdot(yloc, wo_ref[ATTN_WIDTH:, :])
    x1 = _layer_norm(alpha * xin + g_ref[...] * o) * lng_ref[...] + lnb_ref[...]
    x1_ref[...] = x1
    h2 = x1 * (1.0 + sc_ref[...]) + sh_ref[...]
    half = D_MODEL // 2
    hi_bits = lax.bitcast_convert_type(h2[:, 0:half].astype(BF16).astype(F32), jnp.uint32)
    lo_bits = lax.bitcast_convert_type(h2[:, half:].astype(BF16).astype(F32), jnp.uint32)
    packed = hi_bits | (lo_bits >> 16)
    for c in range(PACKED_TILE_ROWS):
        h2v_ref[:, c, :, :] = packed[:, c * LANES:(c + 1) * LANES].reshape(h2.shape[0] // SUBLANES, SUBLANES, LANES)

    hh, hl = _split_bf16(h2)
    t1 = _bdot(hh, wr_ref[...])
    lg = t1[:, 0:ROUTER_COLS] + t1[:, ROUTER_COLS:] + _bdot(hl, wr_ref[:, 0:ROUTER_COLS]) + br_ref[...]

    ts = lg.shape[0]
    lane = lax.broadcasted_iota(jnp.int32, (ts, ROUTER_COLS), 1).astype(F32)
    big = jnp.float32(ROUTER_COLS)

    def top1(v):
        m = jnp.max(v, axis=-1, keepdims=True)
        return m, jnp.min(jnp.where(v == m, lane, big), axis=-1, keepdims=True)

    gl = jnp.where(lane < N_GROUPS, lg, NEG_INF)
    g_val, g_idx = top1(gl)
    lse = g_val + jnp.log(jnp.sum(jnp.exp(gl - g_val), axis=-1, keepdims=True))
    p_group = jnp.exp(g_val - lse)
    e_lo = N_GROUPS + EXP_PER_GROUP * g_idx
    el = jnp.where((lane >= e_lo) & (lane < e_lo + EXP_PER_GROUP), lg, NEG_INF)
    e1, l1 = top1(el)
    e2, l2 = top1(jnp.where(lane == l1, NEG_INF, el))
    z = jnp.exp(e2 - e1)
    gate1 = p_group / (1.0 + z)
    gate2 = p_group * z / (1.0 + z)
    x1id = l1 - N_GROUPS
    x2id = l2 - N_GROUPS

    sel1 = lane == x1id
    sel2 = lane == x2id
    onehot = jnp.where(sel1 | sel2, 1.0, 0.0)
    r_i = lax.broadcasted_iota(jnp.int32, (ts, ts), 0)
    c_i = lax.broadcasted_iota(jnp.int32, (ts, ts), 1)
    before = jnp.where(c_i < r_i, 1.0, 0.0).astype(BF16)
    prefix = _bdot(before, onehot.astype(BF16))
    rank1 = jnp.sum(jnp.where(sel1, prefix, 0.0), axis=-1, keepdims=True)
    rank2 = jnp.sum(jnp.where(sel2, prefix, 0.0), axis=-1, keepdims=True)
    route = jnp.zeros((ts, ROUTER_COLS), F32)
    for col, val in enumerate((x1id, x2id, gate1, gate2, rank1, rank2)):
        route = jnp.where(lane == col, val, route)
    route_ref[...] = route
    routet_ref[...] = route.T[0:SUBLANES, :]
    cnt_ref[...] =jnp.broadcast_to(jnp.sum(onehot, axis=0, keepdims=True), (SUBLANES, ROUTER_COLS))


def _outproj_call(att_x, att_c, y_x, y_c, x_arr, x_off, c_arr, c_off, n_lat, n_ctx, seq_len,
                  w_out, layer, gvecs, lng, lnb, scvecs, shvecs, wr, br, alpha):
    ts = OUT_TILE
    d = D_MODEL
    n_lat_tiles = n_lat // ts
    n_tiles = (n_lat + n_ctx) // ts
    tiles_per_seq = seq_len // ts
    n_seq = n_lat // seq_len
    xo, co = x_off // ts, c_off // ts

    def lat_loc(i):
        return (jnp.minimum(i, n_lat_tiles - 1), 0)

    def ctx_loc(i):
        return (jnp.maximum(i - n_lat_tiles, 0), 0)

    def lat_in(i):
        return (xo + jnp.minimum(i, n_lat_tiles - 1), 0)

    def ctx_in(i):
        return (co + jnp.maximum(i - n_lat_tiles, 0), 0)

    def vec_idx(i):
        return (jnp.where(i < n_lat_tiles, i // tiles_per_seq, n_seq), 0, 0)

    vecb = pl.BlockSpec((None, 1, d), vec_idx)
    vec0 = pl.BlockSpec((1, d), lambda i: (0, 0))
    return pl.pallas_call(
        functools.partial(_outproj_kernel, alpha=alpha, n_lat_tiles=n_lat_tiles),
        grid=(n_tiles,),
        in_specs=[
            pl.BlockSpec((ts, ATTN_WIDTH), lat_loc), pl.BlockSpec((ts, ATTN_WIDTH), ctx_loc),
            pl.BlockSpec((ts, LOCAL_WIDTH), lat_loc), pl.BlockSpec((ts, LOCAL_WIDTH), ctx_loc),
            pl.BlockSpec((ts, d), lat_in), pl.BlockSpec((ts, d), ctx_in),
            pl.BlockSpec((None, d, d), lambda i: (layer, 0, 0)),
            vecb, vec0, vec0, vecb, vecb,
            pl.BlockSpec((d, 2 * ROUTER_COLS), lambda i: (0, 0)),
            pl.BlockSpec((1, ROUTER_COLS), lambda i: (0, 0)),
        ],
        out_specs=[pl.BlockSpec((ts, d), lambda i: (i, 0)),
                   pl.BlockSpec((ts // SUBLANES, PACKED_TILE_ROWS, SUBLANES, LANES), lambda i: (i, 0, 0, 0)),
                   pl.BlockSpec((ts, ROUTER_COLS), lambda i: (i, 0)),
                   pl.BlockSpec((None, SUBLANES, ts), lambda i: (i, 0, 0)),
                   pl.BlockSpec((None, SUBLANES, ROUTER_COLS), lambda i: (i, 0, 0))],
        out_shape=[jax.ShapeDtypeStruct((n_lat + n_ctx, d), F32),
                   jax.ShapeDtypeStruct(((n_lat + n_ctx) // SUBLANES, PACKED_TILE_ROWS, SUBLANES, LANES),
                                        jnp.uint32),
                   jax.ShapeDtypeStruct((n_lat + n_ctx, ROUTER_COLS), F32),
                   jax.ShapeDtypeStruct((n_tiles, SUBLANES, ts), F32),
                   jax.ShapeDtypeStruct((n_tiles, SUBLANES, ROUTER_COLS), F32)],
        scratch_shapes=[pltpu.VMEM((d, d), BF16)],
        compiler_params=_params("arbitrary"),
        name="out_projection",
    )(att_x, att_c, y_x, y_c, x_arr, c_arr, w_out, gvecs, lng, lnb, scvecs, shvecs, wr, br)


DISPATCH_TILE = 512


def _dispatch_kernel(seg_ref, pos_ref, h_ref, xs_out, zbuf, zsem, sem):
    block_tiles = MOE_BLOCK // TOKENS_PER_PACKED_TILE

    @pl.when(pl.program_id(0) == 0)
    def _():
        zbuf[...] = jnp.zeros_like(zbuf)

        def zero_copy(e):
            first_tile = (seg_ref[0, e] - MOE_BLOCK) // TOKENS_PER_PACKED_TILE
            return pltpu.make_async_copy(zbuf, xs_out.at[pl.ds(first_tile, block_tiles)], zsem)

        for e in range(N_EXPERTS):
            @pl.when(seg_ref[1, e] > 0)
            def _():
                zero_copy(e).start()
        for e in range(N_EXPERTS):
            @pl.when(seg_ref[1, e] > 0)
            def _():
                zero_copy(e).wait()

        def tail_copy(b):
            return pltpu.make_async_copy(zbuf, xs_out.at[pl.ds(b * block_tiles, block_tiles)], zsem)

        n_blocks = xs_out.shape[0] // block_tiles
        first_unused = seg_ref[0, N_EXPERTS - 1] // MOE_BLOCK
        lax.fori_loop(first_unused, n_blocks, lambda b, c: (tail_copy(b).start(), c)[1], 0)
        lax.fori_loop(first_unused, n_blocks, lambda b, c: (tail_copy(b).wait(), c)[1], 0)

    def body(j, carry):
        for s in range(SUBLANES):
            t = SUBLANES * j + s
            for k in range(TOP_K):
                p = pos_ref[0, 0, k * DISPATCH_TILE + t]
                dst = xs_out.at[p >> 1, pl.ds(PACKED_TILE_ROWS * (p & 1), PACKED_TILE_ROWS), :]
                pltpu.make_async_copy(h_ref.at[j, :, s, :], dst, sem).start(priority=k)
        return carry

    lax.fori_loop(0, DISPATCH_TILE // SUBLANES, body, 0)
    for k in range(TOP_K):
        pltpu.make_async_copy(h_ref, h_ref, sem).wait()


def _dispatch_call(seg, pos, h2t, n_slots):
    n_tok = h2t.shape[0] * SUBLANES
    n_tiles = n_tok // DISPATCH_TILE
    grid_spec = pltpu.PrefetchScalarGridSpec(
        num_scalar_prefetch=1,
        grid=(n_tiles,),
        in_specs=[
            pl.BlockSpec((1, 1, TOP_K * DISPATCH_TILE), lambda i, seg: (i, 0, 0), memory_space=pltpu.SMEM),
            pl.BlockSpec((DISPATCH_TILE // SUBLANES, PACKED_TILE_ROWS, SUBLANES, LANES),
                         lambda i, seg: (i, 0, 0, 0)),
        ],
        out_specs=pl.BlockSpec(memory_space=pl.ANY),
        scratch_shapes=[pltpu.VMEM((MOE_BLOCK // TOKENS_PER_PACKED_TILE, SUBLANES, LANES), jnp.uint32),
                        pltpu.SemaphoreType.DMA(()), pltpu.SemaphoreType.DMA(())],
    )
    return pl.pallas_call(
        _dispatch_kernel,
        grid_spec=grid_spec,
        out_shape=jax.ShapeDtypeStruct((n_slots // TOKENS_PER_PACKED_TILE, SUBLANES, LANES), jnp.uint32),
        compiler_params=_params("arbitrary"),
        name="moe_dispatch",
    )(seg, pos, h2t)


def _moe_kernel(be_ref, nu_ref, xs_hbm, w1_ref, w3_ref, w2_ref, ys_hbm, w1b, w3b, w2b, xbuf, obuf, isem, osem):
    i = pl.program_id(0)
    n_used = nu_ref[0]
    slot = i % 2
    groups = MOE_BLOCK // SUBLANES
    half = D_MODEL // 2

    def in_copies(blk, at_slot):
        return [pltpu.make_async_copy(
            xs_hbm.at[pl.ds(blk * groups, groups), s // TOKENS_PER_PACKED_TILE,
                      pl.ds(PACKED_TILE_ROWS * (s % TOKENS_PER_PACKED_TILE), PACKED_TILE_ROWS), :],
            xbuf.at[at_slot, :, :, s, :], isem.at[at_slot]) for s in range(SUBLANES)]

    def out_copies(blk, at_slot):
        return [pltpu.make_async_copy(obuf.at[at_slot, :, :, s, :], ys_hbm.at[pl.ds(blk * groups, groups), s],
                                      osem.at[at_slot]) for s in range(SUBLANES)]

    def write_rows(at_slot, y):
        for c in range(ROW_TILE_ROWS):
            obuf[at_slot, :, c, :, :] = y[:, c * LANES:(c + 1) * LANES].reshape(groups, SUBLANES, LANES)

    @pl.when(i == 0)
    def _():
        _start_all(in_copies(0, 0))

    @pl.when(i + 1 < n_used)
    def _():
        _start_all(in_copies(i + 1, 1 - slot))

    @pl.when((i == 0) | (be_ref[i] != be_ref[jnp.maximum(i - 1, 0)]))
    def _():
        w1b[...] = w1_ref[...].astype(BF16)
        w3b[...] = w3_ref[...].astype(BF16)
        w2b[...] = w2_ref[...].astype(BF16)

    @pl.when(i < n_used)
    def _():
        _wait_all(in_copies(i, slot))
        p = jnp.concatenate([xbuf[slot, :, c, :, :].reshape(MOE_BLOCK, LANES) for c in range(PACKED_TILE_ROWS)],
                            axis=1)
        xa = lax.bitcast_convert_type(p & jnp.uint32(0xFFFF0000), F32).astype(BF16)
        xb = lax.bitcast_convert_type(p << 16, F32).astype(BF16)
        h1 = _bdot(xa, w1b[0:half, :]) + _bdot(xb, w1b[half:, :])
        h3 = _bdot(xa, w3b[0:half, :]) + _bdot(xb, w3b[half:, :])
        write_rows(slot, _bdot((jax.nn.silu(h1) * h3).astype(BF16), w2b[...]))
        _start_all(out_copies(i, slot))

        @pl.when(i >= 1)
        def _():
            _wait_all(out_copies(i - 1, 1 - slot))

        @pl.when(i == n_used - 1)
        def _():
            _wait_all(out_copies(i, slot))

    @pl.when(i >= n_used)
    def _():
        obuf[slot] = jnp.zeros(obuf.shape[1:], F32)
        _start_all(out_copies(i, slot))
        _wait_all(out_copies(i, slot))


def _moe_call(block_e, n_used, xs, w1, w3, w2, layer):
    n_blocks = block_e.shape[0]
    n_slots = n_blocks * MOE_BLOCK
    d = D_MODEL
    groups = MOE_BLOCK // SUBLANES
    grid_spec = pltpu.PrefetchScalarGridSpec(
        num_scalar_prefetch=2,
        grid=(n_blocks,),
        in_specs=[
            pl.BlockSpec(memory_space=pl.ANY),
            pl.BlockSpec((None, None, d, D_EXPERT), lambda i, be, nu: (layer, be[i], 0, 0)),
            pl.BlockSpec((None, None, d, D_EXPERT), lambda i, be, nu: (layer, be[i], 0, 0)),
            pl.BlockSpec((None, None, D_EXPERT, d), lambda i, be, nu: (layer, be[i], 0, 0)),
        ],
        out_specs=pl.BlockSpec(memory_space=pl.ANY),
        scratch_shapes=[pltpu.VMEM((d, D_EXPERT), BF16), pltpu.VMEM((d, D_EXPERT), BF16),
                        pltpu.VMEM((D_EXPERT, d), BF16),
                        pltpu.VMEM((2, groups, PACKED_TILE_ROWS, SUBLANES, LANES), jnp.uint32),
                        pltpu.VMEM((2, groups, ROW_TILE_ROWS, SUBLANES, LANES), F32),
                        pltpu.SemaphoreType.DMA((2,)), pltpu.SemaphoreType.DMA((2,))],
    )
    xs_view = (n_slots // SUBLANES, SUBLANES // TOKENS_PER_PACKED_TILE, SUBLANES, LANES)
    ys_view = (n_slots // SUBLANES, SUBLANES, ROW_TILE_ROWS, LANES)
    ys = pl.pallas_call(
        _moe_kernel,
        grid_spec=grid_spec,
        out_shape=jax.ShapeDtypeStruct(ys_view, F32),
        compiler_params=_params("arbitrary"),
        name="moe_experts",
    )(block_e, n_used, xs.reshape(xs_view), w1, w3, w2)
    return ys.reshape(n_slots, ROW_TILE_ROWS, LANES)


COMBINE_TILE = 256


def _combine_kernel(pos0_ref, posn_ref, x1_ref, route_ref, g_ref, lng_ref, lnb_ref, ys_hbm, o_ref, ybuf, sem,
                    *, alpha):
    i = pl.program_id(0)
    nb = pl.num_programs(0)
    n_rows = TOP_K * COMBINE_TILE

    def issue(pos_ref, slot):
        def body(j, carry):
            for s in range(SUBLANES):
                pltpu.make_async_copy(ys_hbm.at[pos_ref[0, 0, SUBLANES * j + s]], ybuf.at[slot, j, :, s, :],
                                      sem.at[slot]).start(priority=s % N_DMA_PRIORITIES)
            return carry
        lax.fori_loop(0, n_rows // SUBLANES, body, 0)

    @pl.when(i == 0)
    def _():
        issue(pos0_ref, 0)

    @pl.when(i + 1 < nb)
    def _():
        issue(posn_ref, (i + 1) % 2)

    slot = i % 2
    pltpu.make_async_copy(ybuf.at[slot], ybuf.at[slot], sem.at[slot]).wait()
    yrows = jnp.concatenate([ybuf[slot, :, c, :, :].reshape(n_rows, LANES) for c in range(ROW_TILE_ROWS)], axis=1)
    route = route_ref[...]
    y = (yrows[0:COMBINE_TILE, :] * route[:, ROUTE_GATE:ROUTE_GATE + 1]
         + yrows[COMBINE_TILE:, :] * route[:, ROUTE_GATE + 1:ROUTE_GATE + 2])
    o_ref[...] = _layer_norm(alpha * x1_ref[...] + g_ref[...] * y) * lng_ref[...] + lnb_ref[...]


def _combine_call(pos, x1, route, gvecs, lng, lnb, ys, n_tok, tiles_per_seq, n_seq, alpha):
    d = D_MODEL
    n_tiles = n_tok // COMBINE_TILE
    n_lat_tiles = tiles_per_seq * n_seq

    def g_idx(i):
        return (jnp.where(i < n_lat_tiles, i // tiles_per_seq, n_seq), 0, 0)

    smem_blk = (1, 1, TOP_K * COMBINE_TILE)
    return pl.pallas_call(
        functools.partial(_combine_kernel, alpha=alpha),
        grid=(n_tiles,),
        in_specs=[
            pl.BlockSpec(smem_blk, lambda i: (0, 0, 0), memory_space=pltpu.SMEM),
            pl.BlockSpec(smem_blk, lambda i: (jnp.minimum(i + 1, n_tiles - 1), 0, 0), memory_space=pltpu.SMEM),
            pl.BlockSpec((COMBINE_TILE, d), lambda i: (i, 0)),
            pl.BlockSpec((COMBINE_TILE, ROUTER_COLS), lambda i: (i, 0)),
            pl.BlockSpec((None, 1, d), g_idx),
            pl.BlockSpec((1, d), lambda i: (0, 0)),
            pl.BlockSpec((1, d), lambda i: (0, 0)),
            pl.BlockSpec(memory_space=pl.ANY),
        ],
        out_specs=pl.BlockSpec((COMBINE_TILE, d), lambda i: (i, 0)),
        out_shape=jax.ShapeDtypeStruct((n_tok, d), F32),
        scratch_shapes=[pltpu.VMEM((2, TOP_K * COMBINE_TILE // SUBLANES, ROW_TILE_ROWS, SUBLANES, LANES), F32),
                        pltpu.SemaphoreType.DMA((2,))],
        compiler_params=_params("arbitrary"),
        name="moe_combine",
    )(pos, pos, x1, route, gvecs, lng, lnb, ys)


def _slots(route_t, tile_counts):
    n_tiles = tile_counts.shape[0]
    n = n_tiles * route_t.shape[2]
    tc = tile_counts.astype(jnp.int32)
    counts = jnp.sum(tc, axis=0)
    padded = (counts + MOE_BLOCK - 1) // MOE_BLOCK * MOE_BLOCK
    pends = jnp.cumsum(padded)
    base = (pends - padded)[None, :] + jnp.cumsum(tc, axis=0) - tc
    n_blocks = -(-n * TOP_K // MOE_BLOCK) + N_EXPERTS
    block_start = jnp.arange(n_blocks, dtype=jnp.int32) * MOE_BLOCK
    block_e = jnp.minimum(jnp.sum(pends[None, 0:N_EXPERTS] <= block_start[:, None], axis=1),
                          N_EXPERTS - 1).astype(jnp.int32)
    n_used = (pends[N_EXPERTS - 1] // MOE_BLOCK).astype(jnp.int32).reshape(1)
    seg = jnp.stack([pends[0:N_EXPERTS], padded[0:N_EXPERTS]]).astype(jnp.int32)
    pos = []
    for k in range(TOP_K):
        e = route_t[:, ROUTE_EXPERT + k, :].astype(jnp.int32)
        b = jnp.zeros_like(e)
        for x in range(N_EXPERTS):
            b = jnp.where(e == x, base[:, x:x + 1], b)
        pos.append((b + route_t[:, ROUTE_RANK + k, :].astype(jnp.int32)).reshape(n))
    return block_e, n_used, seg, pos


def _rope_tables(seq_len):
    m = HEAD_DIM // 4
    freqs = ROPE_BASE ** (-jnp.arange(m, dtype=F32) / m)
    t = jnp.arange(seq_len)
    row = (t // GRID_W).astype(F32)[:, None] * freqs[None, :]
    col = (t % GRID_W).astype(F32)[:, None] * freqs[None, :]
    cos = jnp.concatenate([jnp.cos(row), jnp.cos(row), jnp.cos(col), jnp.cos(col)], axis=-1)
    zero = jnp.zeros_like(row)
    sin_a = jnp.concatenate([zero, jnp.sin(row), zero, jnp.sin(col)], axis=-1)
    sin_b = jnp.concatenate([-jnp.sin(row), zero, -jnp.sin(col), zero], axis=-1)
    rep = LANES // HEAD_DIM
    return jnp.tile(cos, (1, rep)), jnp.tile(sin_a, (1, rep)), jnp.tile(sin_b, (1, rep))


def kernel(x, c, ctx, c_ctx, w_ada, b_ada, w_in, conv_w, attn_sink, gm_ws, gm_bs, w_out, ln1_g, ln1_b,
           w_rg, b_rg, w_re, b_re, w1, w3, w2, ln2_g, ln2_b):
    b_, s_, d_ = x.shape
    c_len = ctx.shape[1]
    depth = w_ada.shape[0]
    alpha = (2 * depth) ** 0.25
    n_lat = b_ * s_
    n_ctx = b_ * c_len
    ts = 1024

    cin = jnp.zeros((ADA_ROWS, d_), F32).at[0:b_].set(c).at[b_].set(c_ctx)
    mod = _ada_call(cin, w_ada, b_ada)

    cos, sin_a, sin_b = _rope_tables(s_)
    ones_c = jnp.ones((c_len, LANES), F32)
    zeros_c = jnp.zeros((c_len, LANES), F32)

    x_flat = x.reshape(n_lat, d_)
    c_flat = ctx.reshape(n_ctx, d_)
    x_off, c_off = 0, 0
    x_arr, c_arr = x_flat, c_flat

    for l in range(depth):
        last = l == depth - 1
        mx = mod[l, 0:b_].reshape(b_, 6, 1, d_)
        sh1, sc1, g1, sh2, sc2, g2 = (mx[:, i] for i in range(6))
        mc = jnp.broadcast_to(mod[l, b_].reshape(1, 6, 1, d_), (b_, 6, 1, d_))
        gm_w = gm_ws[l].astype(BF16)
        gm_b = jnp.repeat(gm_bs[l].T, GM_HEAD, axis=1)
        sink = attn_sink[l]

        qx, kx, kxs, vx, vxs, yx = _inproj_call(x_arr, x_off, b_, s_, ts, sc1, sh1, w_in, l, cos, sin_a, sin_b,
                                                conv_w[l], gm_w, gm_b)
        qc, kc, kcs, vc, vcs, yc = _inproj_call(c_arr, c_off, b_, c_len, c_len, mc[:, 1], mc[:, 0], w_in, l,
                                                ones_c, zeros_c, zeros_c, conv_w[l], gm_w, gm_b)
        att_x = _win_attn_call(sink, qx, kx, kxs, vx, vxs, kc, kcs, vc, vcs, b_, s_, c_len)

        w_r = jnp.zeros((d_, ROUTER_COLS), F32).at[:, 0:N_GROUPS].set(w_rg[l]) \
            .at[:, N_GROUPS:N_GROUPS + N_EXPERTS].set(w_re[l])
        w_r_hi = w_r.astype(BF16)
        w_r_lo = (w_r - w_r_hi.astype(F32)).astype(BF16)
        wr = jnp.concatenate([w_r_hi, w_r_lo], axis=1)
        br = jnp.zeros((1, ROUTER_COLS), F32).at[0, 0:N_GROUPS].set(b_rg[l]) \
            .at[0, N_GROUPS:N_GROUPS + N_EXPERTS].set(b_re[l])
        lng1, lnb1 = ln1_g[l].reshape(1, d_), ln1_b[l].reshape(1, d_)
        lng2, lnb2 = ln2_g[l].reshape(1, d_), ln2_b[l].reshape(1, d_)

        n_tot = n_lat if last else n_lat + n_ctx
        gvecs1 = jnp.concatenate([g1, mc[0:1, 2]], axis=0)
        scvecs2 = jnp.concatenate([sc2, mc[0:1, 4]], axis=0)
        shvecs2 = jnp.concatenate([sh2, mc[0:1, 3]], axis=0)
        if last:
            x1, h2t, route, route_t, tcnt = _outproj_call(att_x, att_x, yx, yx, x_arr, x_off, x_arr, x_off, n_lat, 0, s_,
                                                 w_out, l, gvecs1, lng1, lnb1, scvecs2, shvecs2, wr, br, alpha)
        else:
            att_c = _ctx_attn_call(sink, qc, kc, kcs, vc, vcs, b_, c_len)
            x1, h2t, route, route_t, tcnt = _outproj_call(att_x, att_c, yx, yc, x_arr, x_off, c_arr, c_off, n_lat, n_ctx,
                                                 s_, w_out, l, gvecs1, lng1, lnb1, scvecs2, shvecs2, wr, br, alpha)

        block_e, n_used, seg, pos = _slots(route_t, tcnt[:, 0, :])
        n_slots = block_e.shape[0] * MOE_BLOCK

        def per_tile(tile):
            return jnp.concatenate([p.reshape(n_tot // tile, 1, tile) for p in pos], axis=2)

        xs = _dispatch_call(seg, per_tile(DISPATCH_TILE), h2t, n_slots)
        ys = _moe_call(block_e, n_used, xs, w1, w3, w2, l)

        gvecs = jnp.concatenate([g2, mc[0:1, 5]], axis=0)
        out = _combine_call(per_tile(COMBINE_TILE), x1, route, gvecs, lng2, lnb2, ys, n_tot, s_ // COMBINE_TILE,
                            b_, alpha)
        x_arr, x_off = out, 0
        c_arr, c_off = out, n_lat

    return x_arr.reshape(b_, s_, d_)
```

```python
import functools

import jax
import jax.numpy as jnp
from jax import lax
from jax.experimental import pallas as pl
from jax.experimental.pallas import tpu as pltpu

F32 = jnp.float32
BF16 = jnp.bfloat16

D_MODEL = 1024
GRID_W = 64
HEAD_DIM = 64
N_HEADS = 8
N_KV_HEADS = 2
ATTN_WIDTH = N_HEADS * HEAD_DIM
KV_WIDTH = N_KV_HEADS * HEAD_DIM
WINDOW = 128
ATTN_SCALE = HEAD_DIM ** -0.5
LOG2E = 1.4426950408889634
ROPE_BASE = 10000.0
CONV_WIDTH = D_MODEL // 4
GM_WIDTH = D_MODEL // 4
GM_GROUPS = 4
GM_HEAD = GM_WIDTH // GM_GROUPS
CHUNK = 128
LOCAL_WIDTH = CONV_WIDTH + GM_WIDTH
IN_WIDTH = ATTN_WIDTH + 2 * KV_WIDTH + 3 * CONV_WIDTH + 2 * GM_WIDTH
QKV_WIDTH = ATTN_WIDTH + 2 * KV_WIDTH
KV_PACK_WIDTH = 4 * KV_WIDTH
N_GROUPS = 4
EXP_PER_GROUP = 8
N_EXPERTS = N_GROUPS * EXP_PER_GROUP
TOP_K = 2
D_EXPERT = D_MODEL // 2
MOE_BLOCK = 512
LN_EPS = 1e-6
NEG_INF = -1e30

LANES = 128
SUBLANES = 8
VMEM_LIMIT_BYTES = 48 * 1024 * 1024
N_DMA_PRIORITIES = 2

ROW_TILE_ROWS = D_MODEL // LANES
assert ROW_TILE_ROWS == SUBLANES
PACKED_TILE_ROWS = ROW_TILE_ROWS // 2
TOKENS_PER_PACKED_TILE = SUBLANES // PACKED_TILE_ROWS
assert TOKENS_PER_PACKED_TILE == 2

ROUTER_COLS = LANES
ROUTE_EXPERT = 0
ROUTE_GATE = TOP_K
ROUTE_RANK = 2 * TOP_K
ROPE_HALF_PAIR = HEAD_DIM // 4


def _bdot(a, b):
    return jnp.dot(a, b, preferred_element_type=F32)


def _split_bf16(a):
    hi = a.astype(BF16)
    lo = (a - hi.astype(F32)).astype(BF16)
    return hi, lo


def _layer_norm(r):
    mu = jnp.mean(r, axis=-1, keepdims=True)
    d = r - mu
    var = jnp.mean(d * d, axis=-1, keepdims=True)
    return d * lax.rsqrt(var + LN_EPS)


def _params(*sem, flags=None):
    return pltpu.CompilerParams(dimension_semantics=sem, vmem_limit_bytes=VMEM_LIMIT_BYTES, flags=flags)


def _start_all(copies):
    for cp in copies:
        cp.start()


def _wait_all(copies):
    for cp in copies:
        cp.wait()


ADA_ROWS = 16
ADA_TILE = 1536


def _ada_kernel(c_ref, w_ref, b_ref, o_ref):
    act = jax.nn.silu(c_ref[...])
    ah, al = _split_bf16(act)
    wh, wl = _split_bf16(w_ref[...])
    o_ref[...] = _bdot(ah, wh) + _bdot(ah, wl) + _bdot(al, wh) + b_ref[...]


def _ada_call(cin, w_ada, b_ada):
    depth, d, n = w_ada.shape
    return pl.pallas_call(
        _ada_kernel,
        grid=(depth, n // ADA_TILE),
        in_specs=[
            pl.BlockSpec((ADA_ROWS, d), lambda l, j: (0, 0)),
            pl.BlockSpec((None, d, ADA_TILE), lambda l, j: (l, 0, j)),
            pl.BlockSpec((None, 1, ADA_TILE), lambda l, j: (l, 0, j)),
        ],
        out_specs=pl.BlockSpec((None, ADA_ROWS, ADA_TILE), lambda l, j: (l, 0, j)),
        out_shape=jax.ShapeDtypeStruct((depth, ADA_ROWS, n), F32),
        compiler_params=_params("parallel", "parallel"),
        name="ada_modulation",
    )(cin, w_ada, b_ada.reshape(depth, 1, n))


def _inproj_kernel(x_ref, xp_ref, xn_ref, sc_ref, sh_ref, wf_ref, cos_ref, sa_ref, sb_ref,
                   cw_ref, gw_ref, gb_ref,
                   q_ref, kv_ref, y_ref, w_ref, *, ts):
    t = pl.program_id(1)
    nt = pl.num_programs(1)

    @pl.when((pl.program_id(0) == 0) & (t == 0))
    def _():
        w_ref[...] = wf_ref[...].astype(BF16)

    sc = 1.0 + sc_ref[...]
    sh = sh_ref[...]
    hx = (x_ref[...] * sc + sh).astype(BF16)

    pq = _bdot(hx, w_ref[:, 0:QKV_WIDTH])
    cos = cos_ref[...]
    sa = sa_ref[...]
    sb = sb_ref[...]

    def rope(z):
        return (z * cos + pltpu.roll(z, ROPE_HALF_PAIR, 1) * sa
                + pltpu.roll(z, LANES - ROPE_HALF_PAIR, 1) * sb)

    for j in range(ATTN_WIDTH // LANES):
        sl = slice(j * LANES, (j + 1) * LANES)
        q_ref[:, sl] = (rope(pq[:, sl]) * (ATTN_SCALE * LOG2E)).astype(BF16)
    kr = rope(pq[:, ATTN_WIDTH:ATTN_WIDTH + KV_WIDTH])
    vv = pq[:, ATTN_WIDTH + KV_WIDTH:QKV_WIDTH]
    for j, part in enumerate((kr, pltpu.roll(kr, HEAD_DIM, 1), vv, pltpu.roll(vv, HEAD_DIM, 1))):
        kv_ref[:, j * KV_WIDTH:(j + 1) * KV_WIDTH] = part.astype(BF16)

    pm = _bdot(hx, w_ref[:, QKV_WIDTH:IN_WIDTH])
    cb = pm[:, 0:CONV_WIDTH]
    u = pm[:, CONV_WIDTH:2 * CONV_WIDTH] * pm[:, 2 * CONV_WIDTH:3 * CONV_WIDTH]

    halo = jnp.concatenate([xp_ref[...], xn_ref[...]], axis=0)
    hh = (halo * sc + sh).astype(BF16)
    ph = _bdot(hh, w_ref[:, QKV_WIDTH + CONV_WIDTH:QKV_WIDTH + 3 * CONV_WIDTH])
    uh = ph[:, 0:CONV_WIDTH] * ph[:, CONV_WIDTH:2 * CONV_WIDTH]
    up_row = jnp.where(t > 0, uh[SUBLANES - 1:SUBLANES, :], 0.0)
    dn_row = jnp.where(t < nt - 1, uh[SUBLANES:SUBLANES + 1, :], 0.0)
    ridx = lax.broadcasted_iota(jnp.int32, (ts, CONV_WIDTH), 0)
    u_up = jnp.where(ridx == 0, up_row, pltpu.roll(u, 1, 0))
    u_dn = jnp.where(ridx == ts - 1, dn_row, pltpu.roll(u, ts - 1, 0))
    cw = cw_ref[...]
    y_conv = cb * (u_up * cw[0:1, :] + u * cw[1:2, :] + u_dn * cw[2:3, :])
    y_ref[:, 0:CONV_WIDTH] = y_conv.astype(BF16)

    gu = jax.nn.gelu(pm[:, 3 * CONV_WIDTH:3 * CONV_WIDTH + GM_WIDTH])
    gv = _layer_norm(jax.nn.gelu(pm[:, 3 * CONV_WIDTH + GM_WIDTH:3 * CONV_WIDTH + 2 * GM_WIDTH])).astype(BF16)
    lane = lax.broadcasted_iota(jnp.int32, (CHUNK, LANES), 1)
    zero = jnp.zeros((CHUNK, LANES), BF16)
    for c in range(ts // CHUNK):
        rows = slice(c * CHUNK, (c + 1) * CHUNK)
        for j in range(GM_WIDTH // LANES):
            cols = slice(j * LANES, (j + 1) * LANES)
            vp = gv[rows, cols]
            s = (_bdot(gw_ref[2 * j], jnp.where(lane < GM_HEAD, vp, zero))
                 + _bdot(gw_ref[2 * j + 1], jnp.where(lane >= GM_HEAD, vp, zero))
                 + gb_ref[:, cols])
            y_ref[rows, CONV_WIDTH + j * LANES:CONV_WIDTH + (j + 1) * LANES] = (gu[rows, cols] * s).astype(BF16)


def _inproj_call(x2d, row_off, n_seq, seq_len, ts, sc, sh, w_in, layer, cos, sa, sb, conv_w, gm_w, gm_b):
    nt = seq_len // ts
    off_t = row_off // ts
    off_8 = row_off // SUBLANES
    last_8 = x2d.shape[0] // SUBLANES - 1
    per_tile_8 = ts // SUBLANES
    n_out = n_seq * seq_len
    d = D_MODEL

    def tile_idx(b, t):
        return (off_t + b * nt + t, 0)

    def prev_idx(b, t):
        return (jnp.maximum(off_8 + (b * nt + t) * per_tile_8 - 1, 0), 0)

    def next_idx(b, t):
        return (jnp.minimum(off_8 + (b * nt + t + 1) * per_tile_8, last_8), 0)

    def out_idx(b, t):
        return (b * nt + t, 0)

    vec = pl.BlockSpec((None, 1, d), lambda b, t: (b, 0, 0))
    rope_spec = pl.BlockSpec((ts, LANES), lambda b, t: (t, 0))
    return pl.pallas_call(
        functools.partial(_inproj_kernel, ts=ts),
        grid=(n_seq, nt),
        in_specs=[
            pl.BlockSpec((ts, d), tile_idx),
            pl.BlockSpec((SUBLANES, d), prev_idx),
            pl.BlockSpec((SUBLANES, d), next_idx),
            vec, vec,
            pl.BlockSpec((None, d, IN_WIDTH), lambda b, t: (layer, 0, 0)),
            rope_spec, rope_spec, rope_spec,
            pl.BlockSpec((3, CONV_WIDTH), lambda b, t: (0, 0)),
            pl.BlockSpec((GM_GROUPS, CHUNK, CHUNK), lambda b, t: (0, 0, 0)),
            pl.BlockSpec((CHUNK, GM_WIDTH), lambda b, t: (0, 0)),
        ],
        out_specs=[
            pl.BlockSpec((ts, ATTN_WIDTH), out_idx),
            pl.BlockSpec((ts, KV_PACK_WIDTH), out_idx),
            pl.BlockSpec((ts, LOCAL_WIDTH), out_idx),
        ],
        out_shape=[
            jax.ShapeDtypeStruct((n_out, ATTN_WIDTH), BF16),
            jax.ShapeDtypeStruct((n_out, KV_PACK_WIDTH), BF16),
            jax.ShapeDtypeStruct((n_out, LOCAL_WIDTH), BF16),
        ],
        scratch_shapes=[pltpu.VMEM((d, IN_WIDTH), BF16)],
        compiler_params=_params("arbitrary", "arbitrary"),
        name="in_projection",
    )(x2d, x2d, x2d, sc, sh, w_in, cos, sa, sb, conv_w, gm_w, gm_b)


def _attn_body(q, keys, keys_sw, vals, vals_sw, halo_bias, sink_ref, o_ref, tq):
    nk = keys.shape[0]
    lane = lax.broadcasted_iota(jnp.int32, (nk, LANES), 1)
    lo = lane < HEAD_DIM
    zero = jnp.zeros((nk, LANES), BF16)
    rid = lax.broadcasted_iota(jnp.int32, (2 * tq, 1), 0)
    out_lo = lax.broadcasted_iota(jnp.int32, (2 * tq, LANES), 1) < HEAD_DIM
    nt_dims = (((1,), (1,)), ((), ()))
    gqa = N_HEADS // N_KV_HEADS
    biases = [None] * (nk // LANES)
    if halo_bias is not None:
        assert tq == LANES
        biases[0], biases[2] = halo_bias
    for h in range(N_KV_HEADS):
        k_own, k_oth = (keys, keys_sw) if h == 0 else (keys_sw, keys)
        v_own, v_oth = (vals, vals_sw) if h == 0 else (vals_sw, vals)
        kz = jnp.concatenate([jnp.where(lo, k_own, zero), jnp.where(lo, zero, k_oth)], axis=0)
        vz = jnp.concatenate([jnp.where(lo, v_own, zero), jnp.where(lo, zero, v_oth)], axis=0)
        c0 = h * gqa * HEAD_DIM
        qs = jnp.concatenate([q[:, c0:c0 + LANES], q[:, c0 + LANES:c0 + 2 * LANES]], axis=0)
        s_all = lax.dot_general(qs, kz, nt_dims, preferred_element_type=F32)
        probs, inv_den = [], []
        for par in range(2):
            tiles = []
            for j, bias in enumerate(biases):
                tile = s_all[:, par * nk + j * LANES:par * nk + (j + 1) * LANES]
                tiles.append(tile if bias is None else tile + bias)
            sink = jnp.where(rid < tq, sink_ref[h * gqa + par], sink_ref[h * gqa + 2 + par]) * LOG2E
            tile_max = tiles[0]
            for tile in tiles[1:]:
                tile_max = jnp.maximum(tile_max, tile)
            m = jnp.maximum(jnp.max(tile_max, axis=-1, keepdims=True), sink)
            tile_sum = None
            for tile in tiles:
                p = jnp.exp2(tile - m)
                tile_sum = p if tile_sum is None else tile_sum + p
                probs.append(p.astype(BF16))
            inv_den.append(1.0 / (jnp.sum(tile_sum, axis=-1, keepdims=True) + jnp.exp2(sink - m)))
        o = _bdot(jnp.concatenate(probs, axis=1), vz) * jnp.where(out_lo, inv_den[0], inv_den[1])
        o_ref[:, c0:c0 + LANES] = o[0:tq].astype(BF16)
        o_ref[:, c0 + LANES:c0 + 2 * LANES] = o[tq:2 * tq].astype(BF16)


def _split_kv(kv):
    return tuple(kv[:, j * KV_WIDTH:(j + 1) * KV_WIDTH] for j in range(4))


def _win_attn_kernel(sink_ref, q_ref, kvp_ref, kvc_ref, kvn_ref, kvx_ref, o_ref, *, tq):
    n = pl.program_id(1)
    nb = pl.num_programs(1)
    row = lax.broadcasted_iota(jnp.int32, (tq, tq), 0)
    col = lax.broadcasted_iota(jnp.int32, (tq, tq), 1)
    b_prev = jnp.where((col >= row) & (n > 0), 0.0, NEG_INF)
    b_next = jnp.where((col <= row) & (n < nb - 1), 0.0, NEG_INF)
    halo_bias = (jnp.concatenate([b_prev, b_prev], axis=0), jnp.concatenate([b_next, b_next], axis=0))

    kv = jnp.concatenate([kvp_ref[...], kvc_ref[...], kvn_ref[...], kvx_ref[...]], axis=0)
    _attn_body(q_ref[...], *_split_kv(kv), halo_bias, sink_ref, o_ref, tq)


def _win_attn_call(sink, q, kv, kv_ctx, n_seq, seq_len, ctx_len):
    tq = WINDOW
    nb = seq_len // tq

    def cur(b, n):
        return (b * nb + n, 0)

    def prev(b, n):
        return (b * nb + jnp.maximum(n - 1, 0), 0)

    def nxt(b, n):
        return (b * nb + jnp.minimum(n + 1, nb - 1), 0)

    return pl.pallas_call(
        functools.partial(_win_attn_kernel, tq=tq),
        grid=(n_seq, nb),
        in_specs=[pl.BlockSpec(memory_space=pltpu.SMEM), pl.BlockSpec((tq, ATTN_WIDTH), cur),
                  pl.BlockSpec((tq, KV_PACK_WIDTH), prev), pl.BlockSpec((tq, KV_PACK_WIDTH), cur),
                  pl.BlockSpec((tq, KV_PACK_WIDTH), nxt),
                  pl.BlockSpec((ctx_len, KV_PACK_WIDTH), lambda b, n: (b, 0))],
        out_specs=pl.BlockSpec((tq, ATTN_WIDTH), cur),
        out_shape=jax.ShapeDtypeStruct((n_seq * seq_len, ATTN_WIDTH), BF16),
        compiler_params=_params("parallel", "parallel"),
        name="window_attention",
    )(sink, q, kv, kv, kv, kv_ctx)


def _ctx_attn_kernel(sink_ref, q_ref, kv_ref, o_ref, *, tq):
    _attn_body(q_ref[...], *_split_kv(kv_ref[...]), None, sink_ref, o_ref, tq)


def _ctx_attn_call(sink, q, kv, n_seq, ctx_len):
    def blk(w):
        return pl.BlockSpec((ctx_len, w), lambda b: (b, 0))

    return pl.pallas_call(
        functools.partial(_ctx_attn_kernel, tq=ctx_len),
        grid=(n_seq,),
        in_specs=[pl.BlockSpec(memory_space=pltpu.SMEM), blk(ATTN_WIDTH), blk(KV_PACK_WIDTH)],
        out_specs=blk(ATTN_WIDTH),
        out_shape=jax.ShapeDtypeStruct((n_seq * ctx_len, ATTN_WIDTH), BF16),
        compiler_params=_params("parallel"),
        name="context_attention",
    )(sink, q, kv)


OUT_TILE = 512


def _outproj_kernel(attx_ref, attc_ref, yx_ref, yc_ref, xx_ref, xc_ref, wof_ref, g_ref, lng_ref, lnb_ref,
                    sc_ref, sh_ref, wr_ref, br_ref, before_ref,
                    x1_ref, h2v_ref, route_ref, routet_ref, cnt_ref, wo_ref,
                    *, alpha, n_lat_tiles):
    @pl.when(pl.program_id(0) == 0)
    def _():
        wo_ref[...] = wof_ref[...].astype(BF16)

    is_lat = pl.program_id(0) < n_lat_tiles
    att = jnp.where(is_lat, attx_ref[...], attc_ref[...])
    yloc = jnp.where(is_lat, yx_ref[...], yc_ref[...])
    xin = jnp.where(is_lat, xx_ref[...], xc_ref[...])
    o = _bdot(att, wo_ref[0:ATTN_WIDTH, :]) + _bdot(yloc, wo_ref[ATTN_WIDTH:, :])
    x1 = _layer_norm(alpha * xin + g_ref[...] * o) * lng_ref[...] + lnb_ref[...]
    x1_ref[...] = x1
    h2 = x1 * (1.0 + sc_ref[...]) + sh_ref[...]
    half = D_MODEL // 2
    hi_bits = lax.bitcast_convert_type(h2[:, 0:half].astype(BF16).astype(F32), jnp.uint32)
    lo_bits = lax.bitcast_convert_type(h2[:, half:].astype(BF16).astype(F32), jnp.uint32)
    packed = hi_bits | (lo_bits >> 16)
    for c in range(PACKED_TILE_ROWS):
        h2v_ref[:, c, :, :] = packed[:, c * LANES:(c + 1) * LANES].reshape(h2.shape[0] // SUBLANES, SUBLANES, LANES)

    hh, hl = _split_bf16(h2)
    t1 = _bdot(hh, wr_ref[...])
    lg = t1[:, 0:ROUTER_COLS] + t1[:, ROUTER_COLS:] + _bdot(hl, wr_ref[:, 0:ROUTER_COLS]) + br_ref[...]

    ts = lg.shape[0]
    lane = lax.broadcasted_iota(jnp.int32, (ts, ROUTER_COLS), 1).astype(F32)
    big = jnp.float32(ROUTER_COLS)

    def top1(v):
        m = jnp.max(v, axis=-1, keepdims=True)
        return m, jnp.min(jnp.where(v == m, lane, big), axis=-1, keepdims=True)

    gl = jnp.where(lane < N_GROUPS, lg, NEG_INF)
    g_val, g_idx = top1(gl)
    lse = g_val + jnp.log(jnp.sum(jnp.exp(gl - g_val), axis=-1, keepdims=True))
    p_group = jnp.exp(g_val - lse)
    e_lo = N_GROUPS + EXP_PER_GROUP * g_idx
    el = jnp.where((lane >= e_lo) & (lane < e_lo + EXP_PER_GROUP), lg, NEG_INF)
    e1, l1 = top1(el)
    e2, l2 = top1(jnp.where(lane == l1, NEG_INF, el))
    z = jnp.exp(e2 - e1)
    gate1 = p_group / (1.0 + z)
    gate2 = p_group * z / (1.0 + z)
    x1id = l1 - N_GROUPS
    x2id = l2 - N_GROUPS

    sel1 = lane == x1id
    sel2 = lane == x2id
    onehot = jnp.where(sel1 | sel2, 1.0, 0.0)
    prefix = _bdot(before_ref[...], onehot.astype(BF16))
    rank1 = jnp.sum(jnp.where(sel1, prefix, 0.0), axis=-1, keepdims=True)
    rank2 = jnp.sum(jnp.where(sel2, prefix, 0.0), axis=-1, keepdims=True)
    route = jnp.zeros((ts, ROUTER_COLS), F32)
    for col, val in enumerate((x1id, x2id, gate1, gate2, rank1, rank2)):
        route = jnp.where(lane == col, val, route)
    route_ref[...] = route
    routet_ref[...] = route.T[0:SUBLANES, :]
    cnt_ref[...] =jnp.broadcast_to(jnp.sum(onehot, axis=0, keepdims=True), (SUBLANES, ROUTER_COLS))


def _outproj_call(att_x, att_c, y_x, y_c, x_arr, x_off, c_arr, c_off, n_lat, n_ctx, seq_len,
                  w_out, layer, gvecs, lng, lnb, scvecs, shvecs, wr, br, alpha):
    ts = OUT_TILE
    d = D_MODEL
    n_lat_tiles = n_lat // ts
    n_tiles = (n_lat + n_ctx) // ts
    tiles_per_seq = seq_len // ts
    n_seq = n_lat // seq_len
    xo, co = x_off // ts, c_off // ts

    def lat_loc(i):
        return (jnp.minimum(i, n_lat_tiles - 1), 0)

    def ctx_loc(i):
        return (jnp.maximum(i - n_lat_tiles, 0), 0)

    def lat_in(i):
        return (xo + jnp.minimum(i, n_lat_tiles - 1), 0)

    def ctx_in(i):
        return (co + jnp.maximum(i - n_lat_tiles, 0), 0)

    def vec_idx(i):
        return (jnp.where(i < n_lat_tiles, i // tiles_per_seq, n_seq), 0, 0)

    vecb = pl.BlockSpec((None, 1, d), vec_idx)
    vec0 = pl.BlockSpec((1, d), lambda i: (0, 0))
    return pl.pallas_call(
        functools.partial(_outproj_kernel, alpha=alpha, n_lat_tiles=n_lat_tiles),
        grid=(n_tiles,),
        in_specs=[
            pl.BlockSpec((ts, ATTN_WIDTH), lat_loc), pl.BlockSpec((ts, ATTN_WIDTH), ctx_loc),
            pl.BlockSpec((ts, LOCAL_WIDTH), lat_loc), pl.BlockSpec((ts, LOCAL_WIDTH), ctx_loc),
            pl.BlockSpec((ts, d), lat_in), pl.BlockSpec((ts, d), ctx_in),
            pl.BlockSpec((None, d, d), lambda i: (layer, 0, 0)),
            vecb, vec0, vec0, vecb, vecb,
            pl.BlockSpec((d, 2 * ROUTER_COLS), lambda i: (0, 0)),
            pl.BlockSpec((1, ROUTER_COLS), lambda i: (0, 0)),
            pl.BlockSpec((ts, ts), lambda i: (0, 0)),
        ],
        out_specs=[pl.BlockSpec((ts, d), lambda i: (i, 0)),
                   pl.BlockSpec((ts // SUBLANES, PACKED_TILE_ROWS, SUBLANES, LANES), lambda i: (i, 0, 0, 0)),
                   pl.BlockSpec((ts, ROUTER_COLS), lambda i: (i, 0)),
                   pl.BlockSpec((None, SUBLANES, ts), lambda i: (i, 0, 0)),
                   pl.BlockSpec((None, SUBLANES, ROUTER_COLS), lambda i: (i, 0, 0))],
        out_shape=[jax.ShapeDtypeStruct((n_lat + n_ctx, d), F32),
                   jax.ShapeDtypeStruct(((n_lat + n_ctx) // SUBLANES, PACKED_TILE_ROWS, SUBLANES, LANES),
                                        jnp.uint32),
                   jax.ShapeDtypeStruct((n_lat + n_ctx, ROUTER_COLS), F32),
                   jax.ShapeDtypeStruct((n_tiles, SUBLANES, ts), F32),
                   jax.ShapeDtypeStruct((n_tiles, SUBLANES, ROUTER_COLS), F32)],
        scratch_shapes=[pltpu.VMEM((d, d), BF16)],
        compiler_params=_params("arbitrary"),
        name="out_projection",
    )(att_x, att_c, y_x, y_c, x_arr, c_arr, w_out, gvecs, lng, lnb, scvecs, shvecs, wr, br,
      jnp.tril(jnp.ones((ts, ts), BF16), -1))


DISPATCH_TILE = 512


def _dispatch_kernel(seg_ref, pos_ref, h_ref, xs_out, zbuf, zsem, sem):
    block_tiles = MOE_BLOCK // TOKENS_PER_PACKED_TILE

    @pl.when(pl.program_id(0) == 0)
    def _():
        zbuf[...] = jnp.zeros_like(zbuf)

        def zero_copy(e):
            first_tile = (seg_ref[0, e] - MOE_BLOCK) // TOKENS_PER_PACKED_TILE
            return pltpu.make_async_copy(zbuf, xs_out.at[pl.ds(first_tile, block_tiles)], zsem)

        for e in range(N_EXPERTS):
            @pl.when(seg_ref[1, e] > 0)
            def _():
                zero_copy(e).start()
        for e in range(N_EXPERTS):
            @pl.when(seg_ref[1, e] > 0)
            def _():
                zero_copy(e).wait()

        def tail_copy(b):
            return pltpu.make_async_copy(zbuf, xs_out.at[pl.ds(b * block_tiles, block_tiles)], zsem)

        n_blocks = xs_out.shape[0] // block_tiles
        first_unused = seg_ref[0, N_EXPERTS - 1] // MOE_BLOCK
        lax.fori_loop(first_unused, n_blocks, lambda b, c: (tail_copy(b).start(), c)[1], 0)
        lax.fori_loop(first_unused, n_blocks, lambda b, c: (tail_copy(b).wait(), c)[1], 0)

    def body(j, carry):
        for s in range(SUBLANES):
            t = SUBLANES * j + s
            for k in range(TOP_K):
                p = pos_ref[0, 0, k * DISPATCH_TILE + t]
                dst = xs_out.at[p >> 1, pl.ds(PACKED_TILE_ROWS * (p & 1), PACKED_TILE_ROWS), :]
                pltpu.make_async_copy(h_ref.at[j, :, s, :], dst, sem).start(priority=k)
        return carry

    lax.fori_loop(0, DISPATCH_TILE // SUBLANES, body, 0)
    for k in range(TOP_K):
        pltpu.make_async_copy(h_ref, h_ref, sem).wait()


def _dispatch_call(seg, pos, h2t, n_slots):
    n_tok = h2t.shape[0] * SUBLANES
    n_tiles = n_tok // DISPATCH_TILE
    grid_spec = pltpu.PrefetchScalarGridSpec(
        num_scalar_prefetch=1,
        grid=(n_tiles,),
        in_specs=[
            pl.BlockSpec((1, 1, TOP_K * DISPATCH_TILE), lambda i, seg: (i, 0, 0), memory_space=pltpu.SMEM),
            pl.BlockSpec((DISPATCH_TILE // SUBLANES, PACKED_TILE_ROWS, SUBLANES, LANES),
                         lambda i, seg: (i, 0, 0, 0)),
        ],
        out_specs=pl.BlockSpec(memory_space=pl.ANY),
        scratch_shapes=[pltpu.VMEM((MOE_BLOCK // TOKENS_PER_PACKED_TILE, SUBLANES, LANES), jnp.uint32),
                        pltpu.SemaphoreType.DMA(()), pltpu.SemaphoreType.DMA(())],
    )
    return pl.pallas_call(
        _dispatch_kernel,
        grid_spec=grid_spec,
        out_shape=jax.ShapeDtypeStruct((n_slots // TOKENS_PER_PACKED_TILE, SUBLANES, LANES), jnp.uint32),
        compiler_params=_params("arbitrary"),
        name="moe_dispatch",
    )(seg, pos, h2t)


def _moe_kernel(be_ref, nu_ref, xs_hbm, w1_ref, w3_ref, w2_ref, ys_hbm, w1b, w3b, w2b, xbuf, obuf, isem, osem):
    i = pl.program_id(0)
    n_used = nu_ref[0]
    slot = i % 2
    groups = MOE_BLOCK // SUBLANES
    half = D_MODEL // 2

    def in_copies(blk, at_slot):
        return [pltpu.make_async_copy(
            xs_hbm.at[pl.ds(blk * groups, groups), s // TOKENS_PER_PACKED_TILE,
                      pl.ds(PACKED_TILE_ROWS * (s % TOKENS_PER_PACKED_TILE), PACKED_TILE_ROWS), :],
            xbuf.at[at_slot, :, :, s, :], isem.at[at_slot]) for s in range(SUBLANES)]

    def out_copies(blk, at_slot):
        return [pltpu.make_async_copy(obuf.at[at_slot, :, :, s, :], ys_hbm.at[pl.ds(blk * groups, groups), s],
                                      osem.at[at_slot]) for s in range(SUBLANES)]

    def write_rows(at_slot, y):
        for c in range(ROW_TILE_ROWS):
            obuf[at_slot, :, c, :, :] = y[:, c * LANES:(c + 1) * LANES].reshape(groups, SUBLANES, LANES)

    @pl.when(i == 0)
    def _():
        _start_all(in_copies(0, 0))

    @pl.when(i + 1 < n_used)
    def _():
        _start_all(in_copies(i + 1, 1 - slot))

    @pl.when((i == 0) | (be_ref[i] != be_ref[jnp.maximum(i - 1, 0)]))
    def _():
        w1b[...] = w1_ref[...].astype(BF16)
        w3b[...] = w3_ref[...].astype(BF16)
        w2b[...] = w2_ref[...].astype(BF16)

    @pl.when(i < n_used)
    def _():
        _wait_all(in_copies(i, slot))
        p = jnp.concatenate([xbuf[slot, :, c, :, :].reshape(MOE_BLOCK, LANES) for c in range(PACKED_TILE_ROWS)],
                            axis=1)
        xa = lax.bitcast_convert_type(p & jnp.uint32(0xFFFF0000), F32).astype(BF16)
        xb = lax.bitcast_convert_type(p << 16, F32).astype(BF16)
        h1 = _bdot(xa, w1b[0:half, :]) + _bdot(xb, w1b[half:, :])
        h3 = _bdot(xa, w3b[0:half, :]) + _bdot(xb, w3b[half:, :])
        write_rows(slot, _bdot((jax.nn.silu(h1) * h3).astype(BF16), w2b[...]))
        _start_all(out_copies(i, slot))

        @pl.when(i >= 1)
        def _():
            _wait_all(out_copies(i - 1, 1 - slot))

        @pl.when(i == n_used - 1)
        def _():
            _wait_all(out_copies(i, slot))

    @pl.when(i >= n_used)
    def _():
        obuf[slot] = jnp.zeros(obuf.shape[1:], F32)
        _start_all(out_copies(i, slot))
        _wait_all(out_copies(i, slot))


def _moe_call(block_e, n_used, xs, w1, w3, w2, layer):
    n_blocks = block_e.shape[0]
    n_slots = n_blocks * MOE_BLOCK
    d = D_MODEL
    groups = MOE_BLOCK // SUBLANES
    grid_spec = pltpu.PrefetchScalarGridSpec(
        num_scalar_prefetch=2,
        grid=(n_blocks,),
        in_specs=[
            pl.BlockSpec(memory_space=pl.ANY),
            pl.BlockSpec((None, None, d, D_EXPERT), lambda i, be, nu: (layer, be[i], 0, 0)),
            pl.BlockSpec((None, None, d, D_EXPERT), lambda i, be, nu: (layer, be[i], 0, 0)),
            pl.BlockSpec((None, None, D_EXPERT, d), lambda i, be, nu: (layer, be[i], 0, 0)),
        ],
        out_specs=pl.BlockSpec(memory_space=pl.ANY),
        scratch_shapes=[pltpu.VMEM((d, D_EXPERT), BF16), pltpu.VMEM((d, D_EXPERT), BF16),
                        pltpu.VMEM((D_EXPERT, d), BF16),
                        pltpu.VMEM((2, groups, PACKED_TILE_ROWS, SUBLANES, LANES), jnp.uint32),
                        pltpu.VMEM((2, groups, ROW_TILE_ROWS, SUBLANES, LANES), F32),
                        pltpu.SemaphoreType.DMA((2,)), pltpu.SemaphoreType.DMA((2,))],
    )
    xs_view = (n_slots // SUBLANES, SUBLANES // TOKENS_PER_PACKED_TILE, SUBLANES, LANES)
    ys_view = (n_slots // SUBLANES, SUBLANES, ROW_TILE_ROWS, LANES)
    ys = pl.pallas_call(
        _moe_kernel,
        grid_spec=grid_spec,
        out_shape=jax.ShapeDtypeStruct(ys_view, F32),
        compiler_params=_params("arbitrary"),
        name="moe_experts",
    )(block_e, n_used, xs.reshape(xs_view), w1, w3, w2)
    return ys.reshape(n_slots, ROW_TILE_ROWS, LANES)


COMBINE_TILE = 256


def _combine_kernel(pos0_ref, posn_ref, x1_ref, route_ref, g_ref, lng_ref, lnb_ref, ys_hbm, o_ref, ybuf, sem,
                    *, alpha):
    i = pl.program_id(0)
    nb = pl.num_programs(0)
    n_rows = TOP_K * COMBINE_TILE

    def issue(pos_ref, slot):
        def body(j, carry):
            for s in range(SUBLANES):
                pltpu.make_async_copy(ys_hbm.at[pos_ref[0, 0, SUBLANES * j + s]], ybuf.at[slot, j, :, s, :],
                                      sem.at[slot]).start(priority=s % N_DMA_PRIORITIES)
            return carry
        lax.fori_loop(0, n_rows // SUBLANES, body, 0)

    @pl.when(i == 0)
    def _():
        issue(pos0_ref, 0)

    @pl.when(i + 1 < nb)
    def _():
        issue(posn_ref, (i + 1) % 2)

    slot = i % 2
    pltpu.make_async_copy(ybuf.at[slot], ybuf.at[slot], sem.at[slot]).wait()
    yrows = jnp.concatenate([ybuf[slot, :, c, :, :].reshape(n_rows, LANES) for c in range(ROW_TILE_ROWS)], axis=1)
    route = route_ref[...]
    y = (yrows[0:COMBINE_TILE, :] * route[:, ROUTE_GATE:ROUTE_GATE + 1]
         + yrows[COMBINE_TILE:, :] * route[:, ROUTE_GATE + 1:ROUTE_GATE + 2])
    o_ref[...] = _layer_norm(alpha * x1_ref[...] + g_ref[...] * y) * lng_ref[...] + lnb_ref[...]


def _combine_call(pos, x1, route, gvecs, lng, lnb, ys, n_tok, tiles_per_seq, n_seq, alpha):
    d = D_MODEL
    n_tiles = n_tok // COMBINE_TILE
    n_lat_tiles = tiles_per_seq * n_seq

    def g_idx(i):
        return (jnp.where(i < n_lat_tiles, i // tiles_per_seq, n_seq), 0, 0)

    smem_blk = (1, 1, TOP_K * COMBINE_TILE)
    return pl.pallas_call(
        functools.partial(_combine_kernel, alpha=alpha),
        grid=(n_tiles,),
        in_specs=[
            pl.BlockSpec(smem_blk, lambda i: (0, 0, 0), memory_space=pltpu.SMEM),
            pl.BlockSpec(smem_blk, lambda i: (jnp.minimum(i + 1, n_tiles - 1), 0, 0), memory_space=pltpu.SMEM),
            pl.BlockSpec((COMBINE_TILE, d), lambda i: (i, 0)),
            pl.BlockSpec((COMBINE_TILE, ROUTER_COLS), lambda i: (i, 0)),
            pl.BlockSpec((None, 1, d), g_idx),
            pl.BlockSpec((1, d), lambda i: (0, 0)),
            pl.BlockSpec((1, d), lambda i: (0, 0)),
            pl.BlockSpec(memory_space=pl.ANY),
        ],
        out_specs=pl.BlockSpec((COMBINE_TILE, d), lambda i: (i, 0)),
        out_shape=jax.ShapeDtypeStruct((n_tok, d), F32),
        scratch_shapes=[pltpu.VMEM((2, TOP_K * COMBINE_TILE // SUBLANES, ROW_TILE_ROWS, SUBLANES, LANES), F32),
                        pltpu.SemaphoreType.DMA((2,))],
        compiler_params=_params("arbitrary"),
        name="moe_combine",
    )(pos, pos, x1, route, gvecs, lng, lnb, ys)


def _slots(route_t, tile_counts):
    n_tiles = tile_counts.shape[0]
    n = n_tiles * route_t.shape[2]
    tc = tile_counts.astype(jnp.int32)
    counts = jnp.sum(tc, axis=0)
    padded = (counts + MOE_BLOCK - 1) // MOE_BLOCK * MOE_BLOCK
    pends = jnp.cumsum(padded)
    base = (pends - padded)[None, :] + jnp.cumsum(tc, axis=0) - tc
    n_blocks = -(-n * TOP_K // MOE_BLOCK) + N_EXPERTS
    block_start = jnp.arange(n_blocks, dtype=jnp.int32) * MOE_BLOCK
    block_e = jnp.minimum(jnp.sum(pends[None, 0:N_EXPERTS] <= block_start[:, None], axis=1),
                          N_EXPERTS - 1).astype(jnp.int32)
    n_used = (pends[N_EXPERTS - 1] // MOE_BLOCK).astype(jnp.int32).reshape(1)
    seg = jnp.stack([pends[0:N_EXPERTS], padded[0:N_EXPERTS]]).astype(jnp.int32)
    pos = []
    for k in range(TOP_K):
        e = route_t[:, ROUTE_EXPERT + k, :].astype(jnp.int32)
        hit = e[:, None, :] == jnp.arange(N_EXPERTS, dtype=jnp.int32)[None, :, None]
        b = jnp.sum(jnp.where(hit, base[:, 0:N_EXPERTS, None], 0), axis=1)
        pos.append((b + route_t[:, ROUTE_RANK + k, :].astype(jnp.int32)).reshape(n))
    return block_e, n_used, seg, pos


def _rope_tables(seq_len):
    m = HEAD_DIM // 4
    freqs = ROPE_BASE ** (-jnp.arange(m, dtype=F32) / m)
    t = jnp.arange(seq_len)
    row = (t // GRID_W).astype(F32)[:, None] * freqs[None, :]
    col = (t % GRID_W).astype(F32)[:, None] * freqs[None, :]
    cos = jnp.concatenate([jnp.cos(row), jnp.cos(row), jnp.cos(col), jnp.cos(col)], axis=-1)
    zero = jnp.zeros_like(row)
    sin_a = jnp.concatenate([zero, jnp.sin(row), zero, jnp.sin(col)], axis=-1)
    sin_b = jnp.concatenate([-jnp.sin(row), zero, -jnp.sin(col), zero], axis=-1)
    rep = LANES // HEAD_DIM
    return jnp.tile(cos, (1, rep)), jnp.tile(sin_a, (1, rep)), jnp.tile(sin_b, (1, rep))


def kernel(x, c, ctx, c_ctx, w_ada, b_ada, w_in, conv_w, attn_sink, gm_ws, gm_bs, w_out, ln1_g, ln1_b,
           w_rg, b_rg, w_re, b_re, w1, w3, w2, ln2_g, ln2_b):
    b_, s_, d_ = x.shape
    c_len = ctx.shape[1]
    depth = w_ada.shape[0]
    alpha = (2 * depth) ** 0.25
    n_lat = b_ * s_
    n_ctx = b_ * c_len
    ts = 1024

    cin = jnp.zeros((ADA_ROWS, d_), F32).at[0:b_].set(c).at[b_].set(c_ctx)
    mod = _ada_call(cin, w_ada, b_ada)

    cos, sin_a, sin_b = _rope_tables(s_)
    ones_c = jnp.ones((c_len, LANES), F32)
    zeros_c = jnp.zeros((c_len, LANES), F32)

    x_flat = x.reshape(n_lat, d_)
    c_flat = ctx.reshape(n_ctx, d_)
    x_off, c_off = 0, 0
    x_arr, c_arr = x_flat, c_flat

    for l in range(depth):
        last = l == depth - 1
        mx = mod[l, 0:b_].reshape(b_, 6, 1, d_)
        sh1, sc1, g1, sh2, sc2, g2 = (mx[:, i] for i in range(6))
        mc = jnp.broadcast_to(mod[l, b_].reshape(1, 6, 1, d_), (b_, 6, 1, d_))
        gm_w = gm_ws[l].astype(BF16)
        gm_b = jnp.repeat(gm_bs[l].T, GM_HEAD, axis=1)
        sink = attn_sink[l]

        qx, kvx, yx = _inproj_call(x_arr, x_off, b_, s_, ts, sc1, sh1, w_in, l, cos, sin_a, sin_b,
                                   conv_w[l], gm_w, gm_b)
        qc, kvc, yc = _inproj_call(c_arr, c_off, b_, c_len, c_len, mc[:, 1], mc[:, 0], w_in, l,
                                   ones_c, zeros_c, zeros_c, conv_w[l], gm_w, gm_b)
        att_x = _win_attn_call(sink, qx, kvx, kvc, b_, s_, c_len)

        w_r = jnp.zeros((d_, ROUTER_COLS), F32).at[:, 0:N_GROUPS].set(w_rg[l]) \
            .at[:, N_GROUPS:N_GROUPS + N_EXPERTS].set(w_re[l])
        w_r_hi = w_r.astype(BF16)
        w_r_lo = (w_r - w_r_hi.astype(F32)).astype(BF16)
        wr = jnp.concatenate([w_r_hi, w_r_lo], axis=1)
        br = jnp.zeros((1, ROUTER_COLS), F32).at[0, 0:N_GROUPS].set(b_rg[l]) \
            .at[0, N_GROUPS:N_GROUPS + N_EXPERTS].set(b_re[l])
        lng1, lnb1 = ln1_g[l].reshape(1, d_), ln1_b[l].reshape(1, d_)
        lng2, lnb2 = ln2_g[l].reshape(1, d_), ln2_b[l].reshape(1, d_)

        n_tot = n_lat if last else n_lat + n_ctx
        gvecs1 = jnp.concatenate([g1, mc[0:1, 2]], axis=0)
        scvecs2 = jnp.concatenate([sc2, mc[0:1, 4]], axis=0)
        shvecs2 = jnp.concatenate([sh2, mc[0:1, 3]], axis=0)
        if last:
            x1, h2t, route, route_t, tcnt = _outproj_call(att_x, att_x, yx, yx, x_arr, x_off, x_arr, x_off, n_lat, 0, s_,
                                                 w_out, l, gvecs1, lng1, lnb1, scvecs2, shvecs2, wr, br, alpha)
        else:
            att_c = _ctx_attn_call(sink, qc, kvc, b_, c_len)
            x1, h2t, route, route_t, tcnt = _outproj_call(att_x, att_c, yx, yc, x_arr, x_off, c_arr, c_off, n_lat, n_ctx,
                                                 s_, w_out, l, gvecs1, lng1, lnb1, scvecs2, shvecs2, wr, br, alpha)

        block_e, n_used, seg, pos = _slots(route_t, tcnt[:, 0, :])
        n_slots = block_e.shape[0] * MOE_BLOCK

        def per_tile(tile):
            return jnp.concatenate([p.reshape(n_tot // tile, 1, tile) for p in pos], axis=2)

        xs = _dispatch_call(seg, per_tile(DISPATCH_TILE), h2t, n_slots)
        ys = _moe_call(block_e, n_used, xs, w1, w3, w2, l)

        gvecs = jnp.concatenate([g2, mc[0:1, 5]], axis=0)
        out = _combine_call(per_tile(COMBINE_TILE), x1, route, gvecs, lng2, lnb2, ys, n_tot, s_ // COMBINE_TILE,
                            b_, alpha)
        x_arr, x_off = out, 0
        c_arr, c_off = out, n_lat

    return x_arr.reshape(b_, s_, d_)
```

```python
import functools

import jax
import jax.numpy as jnp
from jax import lax
from jax.experimental import pallas as pl
from jax.experimental.pallas import tpu as pltpu

F32 = jnp.float32
BF16 = jnp.bfloat16

D_MODEL = 1024
GRID_W = 64
HEAD_DIM = 64
N_HEADS = 8
N_KV_HEADS = 2
ATTN_WIDTH = N_HEADS * HEAD_DIM
KV_WIDTH = N_KV_HEADS * HEAD_DIM
WINDOW = 128
ATTN_SCALE = HEAD_DIM ** -0.5
LOG2E = 1.4426950408889634
ROPE_BASE = 10000.0
CONV_WIDTH = D_MODEL // 4
GM_WIDTH = D_MODEL // 4
GM_GROUPS = 4
GM_HEAD = GM_WIDTH // GM_GROUPS
CHUNK = 128
LOCAL_WIDTH = CONV_WIDTH + GM_WIDTH
IN_WIDTH = ATTN_WIDTH + 2 * KV_WIDTH + 3 * CONV_WIDTH + 2 * GM_WIDTH
QKV_WIDTH = ATTN_WIDTH + 2 * KV_WIDTH
KV_PACK_WIDTH = 4 * KV_WIDTH
N_GROUPS = 4
EXP_PER_GROUP = 8
N_EXPERTS = N_GROUPS * EXP_PER_GROUP
TOP_K = 2
D_EXPERT = D_MODEL // 2
MOE_BLOCK = 512
LN_EPS = 1e-6
NEG_INF = -1e30

LANES = 128
SUBLANES = 8
VMEM_LIMIT_BYTES = 48 * 1024 * 1024
N_DMA_PRIORITIES = 2

ROW_TILE_ROWS = D_MODEL // LANES
assert ROW_TILE_ROWS == SUBLANES
PACKED_TILE_ROWS = ROW_TILE_ROWS // 2
TOKENS_PER_PACKED_TILE = SUBLANES // PACKED_TILE_ROWS
assert TOKENS_PER_PACKED_TILE == 2

ROUTER_COLS = LANES
ROUTE_EXPERT = 0
ROUTE_GATE = TOP_K
ROUTE_RANK = 2 * TOP_K
ROPE_HALF_PAIR = HEAD_DIM // 4


def _bdot(a, b):
    return jnp.dot(a, b, preferred_element_type=F32)


def _split_bf16(a):
    hi = a.astype(BF16)
    lo = (a - hi.astype(F32)).astype(BF16)
    return hi, lo


def _layer_norm(r):
    mu = jnp.mean(r, axis=-1, keepdims=True)
    d = r - mu
    var = jnp.mean(d * d, axis=-1, keepdims=True)
    return d * lax.rsqrt(var + LN_EPS)


def _params(*sem, flags=None):
    return pltpu.CompilerParams(dimension_semantics=sem, vmem_limit_bytes=VMEM_LIMIT_BYTES, flags=flags)


def _start_all(copies):
    for cp in copies:
        cp.start()


def _wait_all(copies):
    for cp in copies:
        cp.wait()


ADA_ROWS = 16
ADA_TILE = 1536


def _ada_kernel(c_ref, w_ref, b_ref, o_ref):
    act = jax.nn.silu(c_ref[...])
    ah, al = _split_bf16(act)
    wh, wl = _split_bf16(w_ref[...])
    o_ref[...] = _bdot(ah, wh) + _bdot(ah, wl) + _bdot(al, wh) + b_ref[...]


def _ada_call(cin, w_ada, b_ada):
    depth, d, n = w_ada.shape
    return pl.pallas_call(
        _ada_kernel,
        grid=(depth, n // ADA_TILE),
        in_specs=[
            pl.BlockSpec((ADA_ROWS, d), lambda l, j: (0, 0)),
            pl.BlockSpec((None, d, ADA_TILE), lambda l, j: (l, 0, j)),
            pl.BlockSpec((None, 1, ADA_TILE), lambda l, j: (l, 0, j)),
        ],
        out_specs=pl.BlockSpec((None, ADA_ROWS, ADA_TILE), lambda l, j: (l, 0, j)),
        out_shape=jax.ShapeDtypeStruct((depth, ADA_ROWS, n), F32),
        compiler_params=_params("parallel", "parallel"),
        name="ada_modulation",
    )(cin, w_ada, b_ada.reshape(depth, 1, n))


def _inproj_kernel(x_ref, xp_ref, xn_ref, sc_ref, sh_ref, wf_ref, cos_ref, sa_ref, sb_ref,
                   cw_ref, gw_ref, gb_ref,
                   q_ref, kv_ref, y_ref, w_ref, *, ts):
    t = pl.program_id(1)
    nt = pl.num_programs(1)

    @pl.when((pl.program_id(0) == 0) & (t == 0))
    def _():
        w_ref[...] = wf_ref[...].astype(BF16)

    sc = 1.0 + sc_ref[...]
    sh = sh_ref[...]
    hx = (x_ref[...] * sc + sh).astype(BF16)

    pq = _bdot(hx, w_ref[:, 0:QKV_WIDTH])
    cos = cos_ref[...]
    sa = sa_ref[...]
    sb = sb_ref[...]

    def rope(z):
        return (z * cos + pltpu.roll(z, ROPE_HALF_PAIR, 1) * sa
                + pltpu.roll(z, LANES - ROPE_HALF_PAIR, 1) * sb)

    for j in range(ATTN_WIDTH // LANES):
        sl = slice(j * LANES, (j + 1) * LANES)
        q_ref[:, sl] = (rope(pq[:, sl]) * (ATTN_SCALE * LOG2E)).astype(BF16)
    kr = rope(pq[:, ATTN_WIDTH:ATTN_WIDTH + KV_WIDTH])
    vv = pq[:, ATTN_WIDTH + KV_WIDTH:QKV_WIDTH]
    for j, part in enumerate((kr, pltpu.roll(kr, HEAD_DIM, 1), vv, pltpu.roll(vv, HEAD_DIM, 1))):
        kv_ref[:, j * KV_WIDTH:(j + 1) * KV_WIDTH] = part.astype(BF16)

    pm = _bdot(hx, w_ref[:, QKV_WIDTH:IN_WIDTH])
    cb = pm[:, 0:CONV_WIDTH]
    u = pm[:, CONV_WIDTH:2 * CONV_WIDTH] * pm[:, 2 * CONV_WIDTH:3 * CONV_WIDTH]

    halo = jnp.concatenate([xp_ref[...], xn_ref[...]], axis=0)
    hh = (halo * sc + sh).astype(BF16)
    ph = _bdot(hh, w_ref[:, QKV_WIDTH + CONV_WIDTH:QKV_WIDTH + 3 * CONV_WIDTH])
    uh = ph[:, 0:CONV_WIDTH] * ph[:, CONV_WIDTH:2 * CONV_WIDTH]
    up_row = jnp.where(t > 0, uh[SUBLANES - 1:SUBLANES, :], 0.0)
    dn_row = jnp.where(t < nt - 1, uh[SUBLANES:SUBLANES + 1, :], 0.0)
    ridx = lax.broadcasted_iota(jnp.int32, (ts, CONV_WIDTH), 0)
    u_up = jnp.where(ridx == 0, up_row, pltpu.roll(u, 1, 0))
    u_dn = jnp.where(ridx == ts - 1, dn_row, pltpu.roll(u, ts - 1, 0))
    cw = cw_ref[...]
    y_conv = cb * (u_up * cw[0:1, :] + u * cw[1:2, :] + u_dn * cw[2:3, :])
    y_ref[:, 0:CONV_WIDTH] = y_conv.astype(BF16)

    gu = jax.nn.gelu(pm[:, 3 * CONV_WIDTH:3 * CONV_WIDTH + GM_WIDTH])
    gv = _layer_norm(jax.nn.gelu(pm[:, 3 * CONV_WIDTH + GM_WIDTH:3 * CONV_WIDTH + 2 * GM_WIDTH])).astype(BF16)
    lane = lax.broadcasted_iota(jnp.int32, (CHUNK, LANES), 1)
    zero = jnp.zeros((CHUNK, LANES), BF16)
    for c in range(ts // CHUNK):
        rows = slice(c * CHUNK, (c + 1) * CHUNK)
        for j in range(GM_WIDTH // LANES):
            cols = slice(j * LANES, (j + 1) * LANES)
            vp = gv[rows, cols]
            s = (_bdot(gw_ref[2 * j], jnp.where(lane < GM_HEAD, vp, zero))
                 + _bdot(gw_ref[2 * j + 1], jnp.where(lane >= GM_HEAD, vp, zero))
                 + gb_ref[:, cols])
            y_ref[rows, CONV_WIDTH + j * LANES:CONV_WIDTH + (j + 1) * LANES] = (gu[rows, cols] * s).astype(BF16)


def _inproj_call(x2d, row_off, n_seq, seq_len, ts, sc, sh, w_in, layer, cos, sa, sb, conv_w, gm_w, gm_b):
    nt = seq_len // ts
    off_t = row_off // ts
    off_8 = row_off // SUBLANES
    last_8 = x2d.shape[0] // SUBLANES - 1
    per_tile_8 = ts // SUBLANES
    n_out = n_seq * seq_len
    d = D_MODEL

    def tile_idx(b, t):
        return (off_t + b * nt + t, 0)

    def prev_idx(b, t):
        return (jnp.maximum(off_8 + (b * nt + t) * per_tile_8 - 1, 0), 0)

    def next_idx(b, t):
        return (jnp.minimum(off_8 + (b * nt + t + 1) * per_tile_8, last_8), 0)

    def out_idx(b, t):
        return (b * nt + t, 0)

    vec = pl.BlockSpec((None, 1, d), lambda b, t: (b, 0, 0))
    rope_spec = pl.BlockSpec((ts, LANES), lambda b, t: (t, 0))
    return pl.pallas_call(
        functools.partial(_inproj_kernel, ts=ts),
        grid=(n_seq, nt),
        in_specs=[
            pl.BlockSpec((ts, d), tile_idx),
            pl.BlockSpec((SUBLANES, d), prev_idx),
            pl.BlockSpec((SUBLANES, d), next_idx),
            vec, vec,
            pl.BlockSpec((None, d, IN_WIDTH), lambda b, t: (layer, 0, 0)),
            rope_spec, rope_spec, rope_spec,
            pl.BlockSpec((3, CONV_WIDTH), lambda b, t: (0, 0)),
            pl.BlockSpec((GM_GROUPS, CHUNK, CHUNK), lambda b, t: (0, 0, 0)),
            pl.BlockSpec((CHUNK, GM_WIDTH), lambda b, t: (0, 0)),
        ],
        out_specs=[
            pl.BlockSpec((ts, ATTN_WIDTH), out_idx),
            pl.BlockSpec((ts, KV_PACK_WIDTH), out_idx),
            pl.BlockSpec((ts, LOCAL_WIDTH), out_idx),
        ],
        out_shape=[
            jax.ShapeDtypeStruct((n_out, ATTN_WIDTH), BF16),
            jax.ShapeDtypeStruct((n_out, KV_PACK_WIDTH), BF16),
            jax.ShapeDtypeStruct((n_out, LOCAL_WIDTH), BF16),
        ],
        scratch_shapes=[pltpu.VMEM((d, IN_WIDTH), BF16)],
        compiler_params=_params("arbitrary", "arbitrary"),
        name="in_projection",
    )(x2d, x2d, x2d, sc, sh, w_in, cos, sa, sb, conv_w, gm_w, gm_b)


def _attn_body(q, keys, keys_sw, vals, vals_sw, halo_bias, sink_ref, o_ref, tq):
    nk = keys.shape[0]
    lane = lax.broadcasted_iota(jnp.int32, (nk, LANES), 1)
    lo = lane < HEAD_DIM
    zero = jnp.zeros((nk, LANES), BF16)
    rid = lax.broadcasted_iota(jnp.int32, (2 * tq, 1), 0)
    out_lo = lax.broadcasted_iota(jnp.int32, (2 * tq, LANES), 1) < HEAD_DIM
    nt_dims = (((1,), (1,)), ((), ()))
    gqa = N_HEADS // N_KV_HEADS
    biases = [None] * (nk // LANES)
    if halo_bias is not None:
        assert tq == LANES
        biases[0], biases[2] = halo_bias
    for h in range(N_KV_HEADS):
        k_own, k_oth = (keys, keys_sw) if h == 0 else (keys_sw, keys)
        v_own, v_oth = (vals, vals_sw) if h == 0 else (vals_sw, vals)
        kz = jnp.concatenate([jnp.where(lo, k_own, zero), jnp.where(lo, zero, k_oth)], axis=0)
        vz = jnp.concatenate([jnp.where(lo, v_own, zero), jnp.where(lo, zero, v_oth)], axis=0)
        c0 = h * gqa * HEAD_DIM
        qs = jnp.concatenate([q[:, c0:c0 + LANES], q[:, c0 + LANES:c0 + 2 * LANES]], axis=0)
        s_all = lax.dot_general(qs, kz, nt_dims, preferred_element_type=F32)
        probs, inv_den = [], []
        for par in range(2):
            tiles = []
            for j, bias in enumerate(biases):
                tile = s_all[:, par * nk + j * LANES:par * nk + (j + 1) * LANES]
                tiles.append(tile if bias is None else tile + bias)
            sink = jnp.where(rid < tq, sink_ref[h * gqa + par], sink_ref[h * gqa + 2 + par]) * LOG2E
            tile_max = tiles[0]
            for tile in tiles[1:]:
                tile_max = jnp.maximum(tile_max, tile)
            m = jnp.maximum(jnp.max(tile_max, axis=-1, keepdims=True), sink)
            tile_sum = None
            for tile in tiles:
                p = jnp.exp2(tile - m)
                tile_sum = p if tile_sum is None else tile_sum + p
                probs.append(p.astype(BF16))
            inv_den.append(1.0 / (jnp.sum(tile_sum, axis=-1, keepdims=True) + jnp.exp2(sink - m)))
        o = _bdot(jnp.concatenate(probs, axis=1), vz) * jnp.where(out_lo, inv_den[0], inv_den[1])
        o_ref[:, c0:c0 + LANES] = o[0:tq].astype(BF16)
        o_ref[:, c0 + LANES:c0 + 2 * LANES] = o[tq:2 * tq].astype(BF16)


def _split_kv(kv):
    return tuple(kv[:, j * KV_WIDTH:(j + 1) * KV_WIDTH] for j in range(4))


def _win_attn_kernel(sink_ref, q_ref, kvp_ref, kvc_ref, kvn_ref, kvx_ref, o_ref, *, tq):
    n = pl.program_id(1)
    nb = pl.num_programs(1)
    row = lax.broadcasted_iota(jnp.int32, (tq, tq), 0)
    col = lax.broadcasted_iota(jnp.int32, (tq, tq), 1)
    b_prev = jnp.where((col >= row) & (n > 0), 0.0, NEG_INF)
    b_next = jnp.where((col <= row) & (n < nb - 1), 0.0, NEG_INF)
    halo_bias = (jnp.concatenate([b_prev, b_prev], axis=0), jnp.concatenate([b_next, b_next], axis=0))

    kv = jnp.concatenate([kvp_ref[...], kvc_ref[...], kvn_ref[...], kvx_ref[...]], axis=0)
    _attn_body(q_ref[...], *_split_kv(kv), halo_bias, sink_ref, o_ref, tq)


def _win_attn_call(sink, q, kv, kv_ctx, n_seq, seq_len, ctx_len):
    tq = WINDOW
    nb = seq_len // tq

    def cur(b, n):
        return (b * nb + n, 0)

    def prev(b, n):
        return (b * nb + jnp.maximum(n - 1, 0), 0)

    def nxt(b, n):
        return (b * nb + jnp.minimum(n + 1, nb - 1), 0)

    return pl.pallas_call(
        functools.partial(_win_attn_kernel, tq=tq),
        grid=(n_seq, nb),
        in_specs=[pl.BlockSpec(memory_space=pltpu.SMEM), pl.BlockSpec((tq, ATTN_WIDTH), cur),
                  pl.BlockSpec((tq, KV_PACK_WIDTH), prev), pl.BlockSpec((tq, KV_PACK_WIDTH), cur),
                  pl.BlockSpec((tq, KV_PACK_WIDTH), nxt),
                  pl.BlockSpec((ctx_len, KV_PACK_WIDTH), lambda b, n: (b, 0))],
        out_specs=pl.BlockSpec((tq, ATTN_WIDTH), cur),
        out_shape=jax.ShapeDtypeStruct((n_seq * seq_len, ATTN_WIDTH), BF16),
        compiler_params=_params("parallel", "parallel"),
        name="window_attention",
    )(sink, q, kv, kv, kv, kv_ctx)


def _ctx_attn_kernel(sink_ref, q_ref, kv_ref, o_ref, *, tq):
    _attn_body(q_ref[...], *_split_kv(kv_ref[...]), None, sink_ref, o_ref, tq)


def _ctx_attn_call(sink, q, kv, n_seq, ctx_len):
    def blk(w):
        return pl.BlockSpec((ctx_len, w), lambda b: (b, 0))

    return pl.pallas_call(
        functools.partial(_ctx_attn_kernel, tq=ctx_len),
        grid=(n_seq,),
        in_specs=[pl.BlockSpec(memory_space=pltpu.SMEM), blk(ATTN_WIDTH), blk(KV_PACK_WIDTH)],
        out_specs=blk(ATTN_WIDTH),
        out_shape=jax.ShapeDtypeStruct((n_seq * ctx_len, ATTN_WIDTH), BF16),
        compiler_params=_params("parallel"),
        name="context_attention",
    )(sink, q, kv)


OUT_TILE = 512


def _outproj_kernel(attx_ref, attc_ref, yx_ref, yc_ref, xx_ref, xc_ref, wof_ref, g_ref, lng_ref, lnb_ref,
                    sc_ref, sh_ref, wr_ref, br_ref, before_ref,
                    x1_ref, h2v_ref, route_ref, routet_ref, cnt_ref, wo_ref,
                    *, alpha, n_lat_tiles):
    @pl.when(pl.program_id(0) == 0)
    def _():
        wo_ref[...] = wof_ref[...].astype(BF16)

    is_lat = pl.program_id(0) < n_lat_tiles
    att = jnp.where(is_lat, attx_ref[...], attc_ref[...])
    yloc = jnp.where(is_lat, yx_ref[...], yc_ref[...])
    xin = jnp.where(is_lat, xx_ref[...], xc_ref[...])
    o = _bdot(att, wo_ref[0:ATTN_WIDTH, :]) + _bdot(yloc, wo_ref[ATTN_WIDTH:, :])
    x1 = _layer_norm(alpha * xin + g_ref[...] * o) * lng_ref[...] + lnb_ref[...]
    x1_ref[...] = x1
    h2 = x1 * (1.0 + sc_ref[...]) + sh_ref[...]
    half = D_MODEL // 2
    hi_bits = lax.bitcast_convert_type(h2[:, 0:half].astype(BF16).astype(F32), jnp.uint32)
    lo_bits = lax.bitcast_convert_type(h2[:, half:].astype(BF16).astype(F32), jnp.uint32)
    packed = hi_bits | (lo_bits >> 16)
    for c in range(PACKED_TILE_ROWS):
        h2v_ref[:, c, :, :] = packed[:, c * LANES:(c + 1) * LANES].reshape(h2.shape[0] // SUBLANES, SUBLANES, LANES)

    hh, hl = _split_bf16(h2)
    t1 = _bdot(hh, wr_ref[...])
    lg = t1[:, 0:ROUTER_COLS] + t1[:, ROUTER_COLS:] + _bdot(hl, wr_ref[:, 0:ROUTER_COLS]) + br_ref[...]

    ts = lg.shape[0]
    lane = lax.broadcasted_iota(jnp.int32, (ts, ROUTER_COLS), 1).astype(F32)
    big = jnp.float32(ROUTER_COLS)

    def top1(v):
        m = jnp.max(v, axis=-1, keepdims=True)
        return m, jnp.min(jnp.where(v == m, lane, big), axis=-1, keepdims=True)

    gl = jnp.where(lane < N_GROUPS, lg, NEG_INF)
    g_val, g_idx = top1(gl)
    lse = g_val + jnp.log(jnp.sum(jnp.exp(gl - g_val), axis=-1, keepdims=True))
    p_group = jnp.exp(g_val - lse)
    e_lo = N_GROUPS + EXP_PER_GROUP * g_idx
    el = jnp.where((lane >= e_lo) & (lane < e_lo + EXP_PER_GROUP), lg, NEG_INF)
    e1, l1 = top1(el)
    e2, l2 = top1(jnp.where(lane == l1, NEG_INF, el))
    z = jnp.exp(e2 - e1)
    gate1 = p_group / (1.0 + z)
    gate2 = p_group * z / (1.0 + z)
    x1id = l1 - N_GROUPS
    x2id = l2 - N_GROUPS

    sel1 = lane == x1id
    sel2 = lane == x2id
    onehot = jnp.where(sel1 | sel2, 1.0, 0.0)
    prefix = _bdot(before_ref[...], onehot.astype(BF16))
    rank1 = jnp.sum(jnp.where(sel1, prefix, 0.0), axis=-1, keepdims=True)
    rank2 = jnp.sum(jnp.where(sel2, prefix, 0.0), axis=-1, keepdims=True)
    route = jnp.zeros((ts, ROUTER_COLS), F32)
    for col, val in enumerate((x1id, x2id, gate1, gate2, rank1, rank2)):
        route = jnp.where(lane == col, val, route)
    route_ref[...] = route
    routet_ref[...] = route.T[0:SUBLANES, :]
    cnt_ref[...] =jnp.broadcast_to(jnp.sum(onehot, axis=0, keepdims=True), (SUBLANES, ROUTER_COLS))


def _outproj_call(att_x, att_c, y_x, y_c, x_arr, x_off, c_arr, c_off, n_lat, n_ctx, seq_len,
                  w_out, layer, gvecs, lng, lnb, scvecs, shvecs, wr, br, alpha):
    ts = OUT_TILE
    d = D_MODEL
    n_lat_tiles = n_lat // ts
    n_tiles = (n_lat + n_ctx) // ts
    tiles_per_seq = seq_len // ts
    n_seq = n_lat // seq_len
    xo, co = x_off // ts, c_off // ts

    def lat_loc(i):
        return (jnp.minimum(i, n_lat_tiles - 1), 0)

    def ctx_loc(i):
        return (jnp.maximum(i - n_lat_tiles, 0), 0)

    def lat_in(i):
        return (xo + jnp.minimum(i, n_lat_tiles - 1), 0)

    def ctx_in(i):
        return (co + jnp.maximum(i - n_lat_tiles, 0), 0)

    def vec_idx(i):
        return (jnp.where(i < n_lat_tiles, i // tiles_per_seq, n_seq), 0, 0)

    vecb = pl.BlockSpec((None, 1, d), vec_idx)
    vec0 = pl.BlockSpec((1, d), lambda i: (0, 0))
    return pl.pallas_call(
        functools.partial(_outproj_kernel, alpha=alpha, n_lat_tiles=n_lat_tiles),
        grid=(n_tiles,),
        in_specs=[
            pl.BlockSpec((ts, ATTN_WIDTH), lat_loc), pl.BlockSpec((ts, ATTN_WIDTH), ctx_loc),
            pl.BlockSpec((ts, LOCAL_WIDTH), lat_loc), pl.BlockSpec((ts, LOCAL_WIDTH), ctx_loc),
            pl.BlockSpec((ts, d), lat_in), pl.BlockSpec((ts, d), ctx_in),
            pl.BlockSpec((None, d, d), lambda i: (layer, 0, 0)),
            vecb, vec0, vec0, vecb, vecb,
            pl.BlockSpec((d, 2 * ROUTER_COLS), lambda i: (0, 0)),
            pl.BlockSpec((1, ROUTER_COLS), lambda i: (0, 0)),
            pl.BlockSpec((ts, ts), lambda i: (0, 0)),
        ],
        out_specs=[pl.BlockSpec((ts, d), lambda i: (i, 0)),
                   pl.BlockSpec((ts // SUBLANES, PACKED_TILE_ROWS, SUBLANES, LANES), lambda i: (i, 0, 0, 0)),
                   pl.BlockSpec((ts, ROUTER_COLS), lambda i: (i, 0)),
                   pl.BlockSpec((None, SUBLANES, ts), lambda i: (i, 0, 0)),
                   pl.BlockSpec((None, SUBLANES, ROUTER_COLS), lambda i: (i, 0, 0))],
        out_shape=[jax.ShapeDtypeStruct((n_lat + n_ctx, d), F32),
                   jax.ShapeDtypeStruct(((n_lat + n_ctx) // SUBLANES, PACKED_TILE_ROWS, SUBLANES, LANES),
                                        jnp.uint32),
                   jax.ShapeDtypeStruct((n_lat + n_ctx, ROUTER_COLS), F32),
                   jax.ShapeDtypeStruct((n_tiles, SUBLANES, ts), F32),
                   jax.ShapeDtypeStruct((n_tiles, SUBLANES, ROUTER_COLS), F32)],
        scratch_shapes=[pltpu.VMEM((d, d), BF16)],
        compiler_params=_params("arbitrary"),
        name="out_projection",
    )(att_x, att_c, y_x, y_c, x_arr, c_arr, w_out, gvecs, lng, lnb, scvecs, shvecs, wr, br,
      jnp.tril(jnp.ones((ts, ts), BF16), -1))


DISPATCH_TILE = 512


def _dispatch_kernel(seg_ref, pos_ref, h_ref, xs_out, zbuf, zsem, sem):
    block_tiles = MOE_BLOCK // TOKENS_PER_PACKED_TILE

    @pl.when(pl.program_id(0) == 0)
    def _():
        zbuf[...] = jnp.zeros_like(zbuf)

        def zero_copy(e):
            first_tile = (seg_ref[0, e] - MOE_BLOCK) // TOKENS_PER_PACKED_TILE
            return pltpu.make_async_copy(zbuf, xs_out.at[pl.ds(first_tile, block_tiles)], zsem)

        for e in range(N_EXPERTS):
            @pl.when(seg_ref[1, e] > 0)
            def _():
                zero_copy(e).start()
        for e in range(N_EXPERTS):
            @pl.when(seg_ref[1, e] > 0)
            def _():
                zero_copy(e).wait()

        def tail_copy(b):
            return pltpu.make_async_copy(zbuf, xs_out.at[pl.ds(b * block_tiles, block_tiles)], zsem)

        n_blocks = xs_out.shape[0] // block_tiles
        first_unused = seg_ref[0, N_EXPERTS - 1] // MOE_BLOCK
        lax.fori_loop(first_unused, n_blocks, lambda b, c: (tail_copy(b).start(), c)[1], 0)
        lax.fori_loop(first_unused, n_blocks, lambda b, c: (tail_copy(b).wait(), c)[1], 0)

    def body(j, carry):
        for s in range(SUBLANES):
            t = SUBLANES * j + s
            for k in range(TOP_K):
                p = pos_ref[0, 0, k * DISPATCH_TILE + t]
                dst = xs_out.at[p >> 1, pl.ds(PACKED_TILE_ROWS * (p & 1), PACKED_TILE_ROWS), :]
                pltpu.make_async_copy(h_ref.at[j, :, s, :], dst, sem).start(priority=k)
        return carry

    lax.fori_loop(0, DISPATCH_TILE // SUBLANES, body, 0)
    for k in range(TOP_K):
        pltpu.make_async_copy(h_ref, h_ref, sem).wait()


def _dispatch_call(seg, pos, h2t, n_slots):
    n_tok = h2t.shape[0] * SUBLANES
    n_tiles = n_tok // DISPATCH_TILE
    grid_spec = pltpu.PrefetchScalarGridSpec(
        num_scalar_prefetch=1,
        grid=(n_tiles,),
        in_specs=[
            pl.BlockSpec((1, 1, TOP_K * DISPATCH_TILE), lambda i, seg: (i, 0, 0), memory_space=pltpu.SMEM),
            pl.BlockSpec((DISPATCH_TILE // SUBLANES, PACKED_TILE_ROWS, SUBLANES, LANES),
                         lambda i, seg: (i, 0, 0, 0)),
        ],
        out_specs=pl.BlockSpec(memory_space=pl.ANY),
        scratch_shapes=[pltpu.VMEM((MOE_BLOCK // TOKENS_PER_PACKED_TILE, SUBLANES, LANES), jnp.uint32),
                        pltpu.SemaphoreType.DMA(()), pltpu.SemaphoreType.DMA(())],
    )
    return pl.pallas_call(
        _dispatch_kernel,
        grid_spec=grid_spec,
        out_shape=jax.ShapeDtypeStruct((n_slots // TOKENS_PER_PACKED_TILE, SUBLANES, LANES), jnp.uint32),
        compiler_params=_params("arbitrary"),
        name="moe_dispatch",
    )(seg, pos, h2t)


def _moe_kernel(be_ref, nu_ref, xs_hbm, w1_ref, w3_ref, w2_ref, ys_hbm, w1b, w3b, w2b, xbuf, obuf, isem, osem):
    i = pl.program_id(0)
    n_used = nu_ref[0]
    slot = i % 2
    groups = MOE_BLOCK // SUBLANES
    half = D_MODEL // 2

    def in_copies(blk, at_slot):
        return [pltpu.make_async_copy(
            xs_hbm.at[pl.ds(blk * groups, groups), s // TOKENS_PER_PACKED_TILE,
                      pl.ds(PACKED_TILE_ROWS * (s % TOKENS_PER_PACKED_TILE), PACKED_TILE_ROWS), :],
            xbuf.at[at_slot, :, :, s, :], isem.at[at_slot]) for s in range(SUBLANES)]

    def out_copies(blk, at_slot):
        return [pltpu.make_async_copy(obuf.at[at_slot, :, :, s, :], ys_hbm.at[pl.ds(blk * groups, groups), s],
                                      osem.at[at_slot]) for s in range(SUBLANES)]

    def write_rows(at_slot, y):
        for c in range(ROW_TILE_ROWS):
            obuf[at_slot, :, c, :, :] = y[:, c * LANES:(c + 1) * LANES].reshape(groups, SUBLANES, LANES)

    @pl.when(i == 0)
    def _():
        _start_all(in_copies(0, 0))

    @pl.when(i + 1 < n_used)
    def _():
        _start_all(in_copies(i + 1, 1 - slot))

    @pl.when((i == 0) | (be_ref[i] != be_ref[jnp.maximum(i - 1, 0)]))
    def _():
        w1b[...] = w1_ref[...].astype(BF16)
        w3b[...] = w3_ref[...].astype(BF16)
        w2b[...] = w2_ref[...].astype(BF16)

    @pl.when(i < n_used)
    def _():
        _wait_all(in_copies(i, slot))
        p = jnp.concatenate([xbuf[slot, :, c, :, :].reshape(MOE_BLOCK, LANES) for c in range(PACKED_TILE_ROWS)],
                            axis=1)
        xa = lax.bitcast_convert_type(p & jnp.uint32(0xFFFF0000), F32).astype(BF16)
        xb = lax.bitcast_convert_type(p << 16, F32).astype(BF16)
        h1 = _bdot(xa, w1b[0:half, :]) + _bdot(xb, w1b[half:, :])
        h3 = _bdot(xa, w3b[0:half, :]) + _bdot(xb, w3b[half:, :])
        write_rows(slot, _bdot((jax.nn.silu(h1) * h3).astype(BF16), w2b[...]))
        _start_all(out_copies(i, slot))

        @pl.when(i >= 1)
        def _():
            _wait_all(out_copies(i - 1, 1 - slot))

        @pl.when(i == n_used - 1)
        def _():
            _wait_all(out_copies(i, slot))

    @pl.when(i >= n_used)
    def _():
        obuf[slot] = jnp.zeros(obuf.shape[1:], F32)
        _start_all(out_copies(i, slot))
        _wait_all(out_copies(i, slot))


def _moe_call(block_e, n_used, xs, w1, w3, w2, layer):
    n_blocks = block_e.shape[0]
    n_slots = n_blocks * MOE_BLOCK
    d = D_MODEL
    groups = MOE_BLOCK // SUBLANES
    grid_spec = pltpu.PrefetchScalarGridSpec(
        num_scalar_prefetch=2,
        grid=(n_blocks,),
        in_specs=[
            pl.BlockSpec(memory_space=pl.ANY),
            pl.BlockSpec((None, None, d, D_EXPERT), lambda i, be, nu: (layer, be[i], 0, 0)),
            pl.BlockSpec((None, None, d, D_EXPERT), lambda i, be, nu: (layer, be[i], 0, 0)),
            pl.BlockSpec((None, None, D_EXPERT, d), lambda i, be, nu: (layer, be[i], 0, 0)),
        ],
        out_specs=pl.BlockSpec(memory_space=pl.ANY),
        scratch_shapes=[pltpu.VMEM((d, D_EXPERT), BF16), pltpu.VMEM((d, D_EXPERT), BF16),
                        pltpu.VMEM((D_EXPERT, d), BF16),
                        pltpu.VMEM((2, groups, PACKED_TILE_ROWS, SUBLANES, LANES), jnp.uint32),
                        pltpu.VMEM((2, groups, ROW_TILE_ROWS, SUBLANES, LANES), F32),
                        pltpu.SemaphoreType.DMA((2,)), pltpu.SemaphoreType.DMA((2,))],
    )
    xs_view = (n_slots // SUBLANES, SUBLANES // TOKENS_PER_PACKED_TILE, SUBLANES, LANES)
    ys_view = (n_slots // SUBLANES, SUBLANES, ROW_TILE_ROWS, LANES)
    ys = pl.pallas_call(
        _moe_kernel,
        grid_spec=grid_spec,
        out_shape=jax.ShapeDtypeStruct(ys_view, F32),
        compiler_params=_params("arbitrary"),
        name="moe_experts",
    )(block_e, n_used, xs.reshape(xs_view), w1, w3, w2)
    return ys.reshape(n_slots, ROW_TILE_ROWS, LANES)


COMBINE_TILE = 256
COMBINE_CHUNK = 64


def _combine_kernel(pos0_ref, posn_ref, x1_ref, route_ref, g_ref, lng_ref, lnb_ref, ys_hbm, o_ref, ybuf, sem,
                    *, alpha):
    i = pl.program_id(0)
    nb = pl.num_programs(0)
    n_groups = TOP_K * COMBINE_TILE // SUBLANES
    n_chunks = COMBINE_TILE // COMBINE_CHUNK
    gpc = COMBINE_CHUNK // SUBLANES

    def issue_group(pos_ref, slot, j):
        for s in range(SUBLANES):
            pltpu.make_async_copy(ys_hbm.at[pos_ref[0, 0, SUBLANES * j + s]], ybuf.at[slot, j, :, s, :],
                                  sem.at[slot]).start(priority=s % N_DMA_PRIORITIES)

    @pl.when(i == 0)
    def _():
        lax.fori_loop(0, n_groups, lambda j, c: (issue_group(pos0_ref, 0, j), c)[1], 0)

    slot = i % 2
    pltpu.make_async_copy(ybuf.at[slot], ybuf.at[slot], sem.at[slot]).wait()
    gain, lng, lnb = g_ref[...], lng_ref[...], lnb_ref[...]
    def chunk_rows(first_group):
        return jnp.concatenate([ybuf[slot, first_group:first_group + gpc, t, :, :].reshape(COMBINE_CHUNK, LANES)
                                for t in range(ROW_TILE_ROWS)], axis=1)

    for c in range(n_chunks):
        rows = slice(c * COMBINE_CHUNK, (c + 1) * COMBINE_CHUNK)
        y1 = chunk_rows(c * gpc)
        y2 = chunk_rows(n_groups // TOP_K + c * gpc)
        route = route_ref[rows, :]
        x1 = x1_ref[rows, :]
        for j in range(c * n_groups // n_chunks, (c + 1) * n_groups // n_chunks):
            issue_group(posn_ref, 1 - slot, j)
        y = y1 * route[:, ROUTE_GATE:ROUTE_GATE + 1] + y2 * route[:, ROUTE_GATE + 1:ROUTE_GATE + 2]
        o_ref[rows, :] = _layer_norm(alpha * x1 + gain * y) * lng + lnb

    @pl.when(i == nb - 1)
    def _():
        pltpu.make_async_copy(ybuf.at[1 - slot], ybuf.at[1 - slot], sem.at[1 - slot]).wait()


def _combine_call(pos, x1, route, gvecs, lng, lnb, ys, n_tok, tiles_per_seq, n_seq, alpha):
    d = D_MODEL
    n_tiles = n_tok // COMBINE_TILE
    n_lat_tiles = tiles_per_seq * n_seq

    def g_idx(i):
        return (jnp.where(i < n_lat_tiles, i // tiles_per_seq, n_seq), 0, 0)

    smem_blk = (1, 1, TOP_K * COMBINE_TILE)
    return pl.pallas_call(
        functools.partial(_combine_kernel, alpha=alpha),
        grid=(n_tiles,),
        in_specs=[
            pl.BlockSpec(smem_blk, lambda i: (0, 0, 0), memory_space=pltpu.SMEM),
            pl.BlockSpec(smem_blk, lambda i: (jnp.minimum(i + 1, n_tiles - 1), 0, 0), memory_space=pltpu.SMEM),
            pl.BlockSpec((COMBINE_TILE, d), lambda i: (i, 0)),
            pl.BlockSpec((COMBINE_TILE, ROUTER_COLS), lambda i: (i, 0)),
            pl.BlockSpec((None, 1, d), g_idx),
            pl.BlockSpec((1, d), lambda i: (0, 0)),
            pl.BlockSpec((1, d), lambda i: (0, 0)),
            pl.BlockSpec(memory_space=pl.ANY),
        ],
        out_specs=pl.BlockSpec((COMBINE_TILE, d), lambda i: (i, 0)),
        out_shape=jax.ShapeDtypeStruct((n_tok, d), F32),
        scratch_shapes=[pltpu.VMEM((2, TOP_K * COMBINE_TILE // SUBLANES, ROW_TILE_ROWS, SUBLANES, LANES), F32),
                        pltpu.SemaphoreType.DMA((2,))],
        compiler_params=_params("arbitrary"),
        name="moe_combine",
    )(pos, pos, x1, route, gvecs, lng, lnb, ys)


def _slots(route_t, tile_counts):
    n_tiles = tile_counts.shape[0]
    n = n_tiles * route_t.shape[2]
    tc = tile_counts.astype(jnp.int32)
    counts = jnp.sum(tc, axis=0)
    padded = (counts + MOE_BLOCK - 1) // MOE_BLOCK * MOE_BLOCK
    pends = jnp.cumsum(padded)
    base = (pends - padded)[None, :] + jnp.cumsum(tc, axis=0) - tc
    n_blocks = -(-n * TOP_K // MOE_BLOCK) + N_EXPERTS
    block_start = jnp.arange(n_blocks, dtype=jnp.int32) * MOE_BLOCK
    block_e = jnp.minimum(jnp.sum(pends[None, 0:N_EXPERTS] <= block_start[:, None], axis=1),
                          N_EXPERTS - 1).astype(jnp.int32)
    n_used = (pends[N_EXPERTS - 1] // MOE_BLOCK).astype(jnp.int32).reshape(1)
    seg = jnp.stack([pends[0:N_EXPERTS], padded[0:N_EXPERTS]]).astype(jnp.int32)
    pos = []
    for k in range(TOP_K):
        e = route_t[:, ROUTE_EXPERT + k, :].astype(jnp.int32)
        hit = e[:, None, :] == jnp.arange(N_EXPERTS, dtype=jnp.int32)[None, :, None]
        b = jnp.sum(jnp.where(hit, base[:, 0:N_EXPERTS, None], 0), axis=1)
        pos.append((b + route_t[:, ROUTE_RANK + k, :].astype(jnp.int32)).reshape(n))
    return block_e, n_used, seg, pos


def _rope_tables(seq_len):
    m = HEAD_DIM // 4
    freqs = ROPE_BASE ** (-jnp.arange(m, dtype=F32) / m)
    t = jnp.arange(seq_len)
    row = (t // GRID_W).astype(F32)[:, None] * freqs[None, :]
    col = (t % GRID_W).astype(F32)[:, None] * freqs[None, :]
    cos = jnp.concatenate([jnp.cos(row), jnp.cos(row), jnp.cos(col), jnp.cos(col)], axis=-1)
    zero = jnp.zeros_like(row)
    sin_a = jnp.concatenate([zero, jnp.sin(row), zero, jnp.sin(col)], axis=-1)
    sin_b = jnp.concatenate([-jnp.sin(row), zero, -jnp.sin(col), zero], axis=-1)
    rep = LANES // HEAD_DIM
    return jnp.tile(cos, (1, rep)), jnp.tile(sin_a, (1, rep)), jnp.tile(sin_b, (1, rep))


def kernel(x, c, ctx, c_ctx, w_ada, b_ada, w_in, conv_w, attn_sink, gm_ws, gm_bs, w_out, ln1_g, ln1_b,
           w_rg, b_rg, w_re, b_re, w1, w3, w2, ln2_g, ln2_b):
    b_, s_, d_ = x.shape
    c_len = ctx.shape[1]
    depth = w_ada.shape[0]
    alpha = (2 * depth) ** 0.25
    n_lat = b_ * s_
    n_ctx = b_ * c_len
    ts = 1024

    cin = jnp.zeros((ADA_ROWS, d_), F32).at[0:b_].set(c).at[b_].set(c_ctx)
    mod = _ada_call(cin, w_ada, b_ada)

    cos, sin_a, sin_b = _rope_tables(s_)
    ones_c = jnp.ones((c_len, LANES), F32)
    zeros_c = jnp.zeros((c_len, LANES), F32)

    x_flat = x.reshape(n_lat, d_)
    c_flat = ctx.reshape(n_ctx, d_)
    x_off, c_off = 0, 0
    x_arr, c_arr = x_flat, c_flat

    for l in range(depth):
        last = l == depth - 1
        mx = mod[l, 0:b_].reshape(b_, 6, 1, d_)
        sh1, sc1, g1, sh2, sc2, g2 = (mx[:, i] for i in range(6))
        mc = jnp.broadcast_to(mod[l, b_].reshape(1, 6, 1, d_), (b_, 6, 1, d_))
        gm_w = gm_ws[l].astype(BF16)
        gm_b = jnp.repeat(gm_bs[l].T, GM_HEAD, axis=1)
        sink = attn_sink[l]

        qx, kvx, yx = _inproj_call(x_arr, x_off, b_, s_, ts, sc1, sh1, w_in, l, cos, sin_a, sin_b,
                                   conv_w[l], gm_w, gm_b)
        qc, kvc, yc = _inproj_call(c_arr, c_off, b_, c_len, c_len, mc[:, 1], mc[:, 0], w_in, l,
                                   ones_c, zeros_c, zeros_c, conv_w[l], gm_w, gm_b)
        att_x = _win_attn_call(sink, qx, kvx, kvc, b_, s_, c_len)

        w_r = jnp.zeros((d_, ROUTER_COLS), F32).at[:, 0:N_GROUPS].set(w_rg[l]) \
            .at[:, N_GROUPS:N_GROUPS + N_EXPERTS].set(w_re[l])
        w_r_hi = w_r.astype(BF16)
        w_r_lo = (w_r - w_r_hi.astype(F32)).astype(BF16)
        wr = jnp.concatenate([w_r_hi, w_r_lo], axis=1)
        br = jnp.zeros((1, ROUTER_COLS), F32).at[0, 0:N_GROUPS].set(b_rg[l]) \
            .at[0, N_GROUPS:N_GROUPS + N_EXPERTS].set(b_re[l])
        lng1, lnb1 = ln1_g[l].reshape(1, d_), ln1_b[l].reshape(1, d_)
        lng2, lnb2 = ln2_g[l].reshape(1, d_), ln2_b[l].reshape(1, d_)

        n_tot = n_lat if last else n_lat + n_ctx
        gvecs1 = jnp.concatenate([g1, mc[0:1, 2]], axis=0)
        scvecs2 = jnp.concatenate([sc2, mc[0:1, 4]], axis=0)
        shvecs2 = jnp.concatenate([sh2, mc[0:1, 3]], axis=0)
        if last:
            x1, h2t, route, route_t, tcnt = _outproj_call(att_x, att_x, yx, yx, x_arr, x_off, x_arr, x_off, n_lat, 0, s_,
                                                 w_out, l, gvecs1, lng1, lnb1, scvecs2, shvecs2, wr, br, alpha)
        else:
            att_c = _ctx_attn_call(sink, qc, kvc, b_, c_len)
            x1, h2t, route, route_t, tcnt = _outproj_call(att_x, att_c, yx, yc, x_arr, x_off, c_arr, c_off, n_lat, n_ctx,
                                                 s_, w_out, l, gvecs1, lng1, lnb1, scvecs2, shvecs2, wr, br, alpha)

        block_e, n_used, seg, pos = _slots(route_t, tcnt[:, 0, :])
        n_slots = block_e.shape[0] * MOE_BLOCK

        def per_tile(tile):
            return jnp.concatenate([p.reshape(n_tot // tile, 1, tile) for p in pos], axis=2)

        xs = _dispatch_call(seg, per_tile(DISPATCH_TILE), h2t, n_slots)
        ys = _moe_call(block_e, n_used, xs, w1, w3, w2, l)

        gvecs = jnp.concatenate([g2, mc[0:1, 5]], axis=0)
        out = _combine_call(per_tile(COMBINE_TILE), x1, route, gvecs, lng2, lnb2, ys, n_tot, s_ // COMBINE_TILE,
                            b_, alpha)
        x_arr, x_off = out, 0
        c_arr, c_off = out, n_lat

    return x_arr.reshape(b_, s_, d_)
```

```python
import functools

import jax
import jax.numpy as jnp
from jax import lax
from jax.experimental import pallas as pl
from jax.experimental.pallas import tpu as pltpu

F32 = jnp.float32
BF16 = jnp.bfloat16

D_MODEL = 1024
GRID_W = 64
HEAD_DIM = 64
N_HEADS = 8
N_KV_HEADS = 2
ATTN_WIDTH = N_HEADS * HEAD_DIM
KV_WIDTH = N_KV_HEADS * HEAD_DIM
WINDOW = 128
ATTN_SCALE = HEAD_DIM ** -0.5
LOG2E = 1.4426950408889634
ROPE_BASE = 10000.0
CONV_WIDTH = D_MODEL // 4
GM_WIDTH = D_MODEL // 4
GM_GROUPS = 4
GM_HEAD = GM_WIDTH // GM_GROUPS
CHUNK = 128
LOCAL_WIDTH = CONV_WIDTH + GM_WIDTH
IN_WIDTH = ATTN_WIDTH + 2 * KV_WIDTH + 3 * CONV_WIDTH + 2 * GM_WIDTH
QKV_WIDTH = ATTN_WIDTH + 2 * KV_WIDTH
KV_PACK_WIDTH = 4 * KV_WIDTH
N_GROUPS = 4
EXP_PER_GROUP = 8
N_EXPERTS = N_GROUPS * EXP_PER_GROUP
TOP_K = 2
D_EXPERT = D_MODEL // 2
MOE_BLOCK = 512
LN_EPS = 1e-6
NEG_INF = -1e30

LANES = 128
SUBLANES = 8
VMEM_LIMIT_BYTES = 48 * 1024 * 1024
N_DMA_PRIORITIES = 2

ROW_TILE_ROWS = D_MODEL // LANES
assert ROW_TILE_ROWS == SUBLANES
PACKED_TILE_ROWS = ROW_TILE_ROWS // 2
TOKENS_PER_PACKED_TILE = SUBLANES // PACKED_TILE_ROWS
assert TOKENS_PER_PACKED_TILE == 2

ROUTER_COLS = LANES
ROUTE_EXPERT = 0
ROUTE_GATE = TOP_K
ROUTE_RANK = 2 * TOP_K
ROPE_HALF_PAIR = HEAD_DIM // 4


def _bdot(a, b):
    return jnp.dot(a, b, preferred_element_type=F32)


def _split_bf16(a):
    hi = a.astype(BF16)
    lo = (a - hi.astype(F32)).astype(BF16)
    return hi, lo


def _layer_norm(r):
    mu = jnp.mean(r, axis=-1, keepdims=True)
    d = r - mu
    var = jnp.mean(d * d, axis=-1, keepdims=True)
    return d * lax.rsqrt(var + LN_EPS)


def _params(*sem, flags=None):
    return pltpu.CompilerParams(dimension_semantics=sem, vmem_limit_bytes=VMEM_LIMIT_BYTES, flags=flags)


def _start_all(copies):
    for cp in copies:
        cp.start()


def _wait_all(copies):
    for cp in copies:
        cp.wait()


ADA_ROWS = 16
ADA_TILE = 1536


def _ada_kernel(c_ref, w_ref, b_ref, o_ref):
    act = jax.nn.silu(c_ref[...])
    ah, al = _split_bf16(act)
    wh, wl = _split_bf16(w_ref[...])
    o_ref[...] = _bdot(ah, wh) + _bdot(ah, wl) + _bdot(al, wh) + b_ref[...]


def _ada_call(cin, w_ada, b_ada):
    depth, d, n = w_ada.shape
    return pl.pallas_call(
        _ada_kernel,
        grid=(depth, n // ADA_TILE),
        in_specs=[
            pl.BlockSpec((ADA_ROWS, d), lambda l, j: (0, 0)),
            pl.BlockSpec((None, d, ADA_TILE), lambda l, j: (l, 0, j)),
            pl.BlockSpec((None, 1, ADA_TILE), lambda l, j: (l, 0, j)),
        ],
        out_specs=pl.BlockSpec((None, ADA_ROWS, ADA_TILE), lambda l, j: (l, 0, j)),
        out_shape=jax.ShapeDtypeStruct((depth, ADA_ROWS, n), F32),
        compiler_params=_params("parallel", "parallel"),
        name="ada_modulation",
    )(cin, w_ada, b_ada.reshape(depth, 1, n))


def _inproj_kernel(x_ref, xp_ref, xn_ref, sc_ref, sh_ref, wf_ref, cos_ref, sa_ref, sb_ref,
                   cw_ref, gw_ref, gb_ref,
                   q_ref, kv_ref, y_ref, w_ref, *, ts):
    t = pl.program_id(1)
    nt = pl.num_programs(1)

    @pl.when((pl.program_id(0) == 0) & (t == 0))
    def _():
        w_ref[...] = wf_ref[...].astype(BF16)

    sc = 1.0 + sc_ref[...]
    sh = sh_ref[...]
    hx = (x_ref[...] * sc + sh).astype(BF16)

    pq = _bdot(hx, w_ref[:, 0:QKV_WIDTH])
    cos = cos_ref[...]
    sa = sa_ref[...]
    sb = sb_ref[...]

    def rope(z):
        return (z * cos + pltpu.roll(z, ROPE_HALF_PAIR, 1) * sa
                + pltpu.roll(z, LANES - ROPE_HALF_PAIR, 1) * sb)

    for j in range(ATTN_WIDTH // LANES):
        sl = slice(j * LANES, (j + 1) * LANES)
        q_ref[:, sl] = (rope(pq[:, sl]) * (ATTN_SCALE * LOG2E)).astype(BF16)
    kr = rope(pq[:, ATTN_WIDTH:ATTN_WIDTH + KV_WIDTH])
    vv = pq[:, ATTN_WIDTH + KV_WIDTH:QKV_WIDTH]
    for j, part in enumerate((kr, pltpu.roll(kr, HEAD_DIM, 1), vv, pltpu.roll(vv, HEAD_DIM, 1))):
        kv_ref[:, j * KV_WIDTH:(j + 1) * KV_WIDTH] = part.astype(BF16)

    pm = _bdot(hx, w_ref[:, QKV_WIDTH:IN_WIDTH])
    cb = pm[:, 0:CONV_WIDTH]
    u = pm[:, CONV_WIDTH:2 * CONV_WIDTH] * pm[:, 2 * CONV_WIDTH:3 * CONV_WIDTH]

    halo = jnp.concatenate([xp_ref[...], xn_ref[...]], axis=0)
    hh = (halo * sc + sh).astype(BF16)
    ph = _bdot(hh, w_ref[:, QKV_WIDTH + CONV_WIDTH:QKV_WIDTH + 3 * CONV_WIDTH])
    uh = ph[:, 0:CONV_WIDTH] * ph[:, CONV_WIDTH:2 * CONV_WIDTH]
    up_row = jnp.where(t > 0, uh[SUBLANES - 1:SUBLANES, :], 0.0)
    dn_row = jnp.where(t < nt - 1, uh[SUBLANES:SUBLANES + 1, :], 0.0)
    ridx = lax.broadcasted_iota(jnp.int32, (ts, CONV_WIDTH), 0)
    u_up = jnp.where(ridx == 0, up_row, pltpu.roll(u, 1, 0))
    u_dn = jnp.where(ridx == ts - 1, dn_row, pltpu.roll(u, ts - 1, 0))
    cw = cw_ref[...]
    y_conv = cb * (u_up * cw[0:1, :] + u * cw[1:2, :] + u_dn * cw[2:3, :])
    y_ref[:, 0:CONV_WIDTH] = y_conv.astype(BF16)

    gu = jax.nn.gelu(pm[:, 3 * CONV_WIDTH:3 * CONV_WIDTH + GM_WIDTH])
    gv = _layer_norm(jax.nn.gelu(pm[:, 3 * CONV_WIDTH + GM_WIDTH:3 * CONV_WIDTH + 2 * GM_WIDTH])).astype(BF16)
    lane = lax.broadcasted_iota(jnp.int32, (CHUNK, LANES), 1)
    zero = jnp.zeros((CHUNK, LANES), BF16)
    for c in range(ts // CHUNK):
        rows = slice(c * CHUNK, (c + 1) * CHUNK)
        for j in range(GM_WIDTH // LANES):
            cols = slice(j * LANES, (j + 1) * LANES)
            vp = gv[rows, cols]
            s = (_bdot(gw_ref[2 * j], jnp.where(lane < GM_HEAD, vp, zero))
                 + _bdot(gw_ref[2 * j + 1], jnp.where(lane >= GM_HEAD, vp, zero))
                 + gb_ref[:, cols])
            y_ref[rows, CONV_WIDTH + j * LANES:CONV_WIDTH + (j + 1) * LANES] = (gu[rows, cols] * s).astype(BF16)


def _inproj_call(x2d, row_off, n_seq, seq_len, ts, sc, sh, w_in, layer, cos, sa, sb, conv_w, gm_w, gm_b):
    nt = seq_len // ts
    off_t = row_off // ts
    off_8 = row_off // SUBLANES
    last_8 = x2d.shape[0] // SUBLANES - 1
    per_tile_8 = ts // SUBLANES
    n_out = n_seq * seq_len
    d = D_MODEL

    def tile_idx(b, t):
        return (off_t + b * nt + t, 0)

    def prev_idx(b, t):
        return (jnp.maximum(off_8 + (b * nt + t) * per_tile_8 - 1, 0), 0)

    def next_idx(b, t):
        return (jnp.minimum(off_8 + (b * nt + t + 1) * per_tile_8, last_8), 0)

    def out_idx(b, t):
        return (b * nt + t, 0)

    vec = pl.BlockSpec((None, 1, d), lambda b, t: (b, 0, 0))
    rope_spec = pl.BlockSpec((ts, LANES), lambda b, t: (t, 0))
    return pl.pallas_call(
        functools.partial(_inproj_kernel, ts=ts),
        grid=(n_seq, nt),
        in_specs=[
            pl.BlockSpec((ts, d), tile_idx),
            pl.BlockSpec((SUBLANES, d), prev_idx),
            pl.BlockSpec((SUBLANES, d), next_idx),
            vec, vec,
            pl.BlockSpec((None, d, IN_WIDTH), lambda b, t: (layer, 0, 0)),
            rope_spec, rope_spec, rope_spec,
            pl.BlockSpec((3, CONV_WIDTH), lambda b, t: (0, 0)),
            pl.BlockSpec((GM_GROUPS, CHUNK, CHUNK), lambda b, t: (0, 0, 0)),
            pl.BlockSpec((CHUNK, GM_WIDTH), lambda b, t: (0, 0)),
        ],
        out_specs=[
            pl.BlockSpec((ts, ATTN_WIDTH), out_idx),
            pl.BlockSpec((ts, KV_PACK_WIDTH), out_idx),
            pl.BlockSpec((ts, LOCAL_WIDTH), out_idx),
        ],
        out_shape=[
            jax.ShapeDtypeStruct((n_out, ATTN_WIDTH), BF16),
            jax.ShapeDtypeStruct((n_out, KV_PACK_WIDTH), BF16),
            jax.ShapeDtypeStruct((n_out, LOCAL_WIDTH), BF16),
        ],
        scratch_shapes=[pltpu.VMEM((d, IN_WIDTH), BF16)],
        compiler_params=_params("arbitrary", "arbitrary"),
        name="in_projection",
    )(x2d, x2d, x2d, sc, sh, w_in, cos, sa, sb, conv_w, gm_w, gm_b)


def _attn_body(q, keys, keys_sw, vals, vals_sw, halo_bias, sink_ref, o_ref, tq):
    nk = keys.shape[0]
    lane = lax.broadcasted_iota(jnp.int32, (nk, LANES), 1)
    lo = lane < HEAD_DIM
    zero = jnp.zeros((nk, LANES), BF16)
    rid = lax.broadcasted_iota(jnp.int32, (2 * tq, 1), 0)
    out_lo = lax.broadcasted_iota(jnp.int32, (2 * tq, LANES), 1) < HEAD_DIM
    nt_dims = (((1,), (1,)), ((), ()))
    gqa = N_HEADS // N_KV_HEADS
    biases = [None] * (nk // LANES)
    if halo_bias is not None:
        assert tq == LANES
        biases[0], biases[2] = halo_bias
    for h in range(N_KV_HEADS):
        k_own, k_oth = (keys, keys_sw) if h == 0 else (keys_sw, keys)
        v_own, v_oth = (vals, vals_sw) if h == 0 else (vals_sw, vals)
        kz = jnp.concatenate([jnp.where(lo, k_own, zero), jnp.where(lo, zero, k_oth)], axis=0)
        vz = jnp.concatenate([jnp.where(lo, v_own, zero), jnp.where(lo, zero, v_oth)], axis=0)
        c0 = h * gqa * HEAD_DIM
        qs = jnp.concatenate([q[:, c0:c0 + LANES], q[:, c0 + LANES:c0 + 2 * LANES]], axis=0)
        s_all = lax.dot_general(qs, kz, nt_dims, preferred_element_type=F32)
        probs, inv_den = [], []
        for par in range(2):
            tiles = []
            for j, bias in enumerate(biases):
                tile = s_all[:, par * nk + j * LANES:par * nk + (j + 1) * LANES]
                tiles.append(tile if bias is None else tile + bias)
            sink = jnp.where(rid < tq, sink_ref[h * gqa + par], sink_ref[h * gqa + 2 + par]) * LOG2E
            tile_max = tiles[0]
            for tile in tiles[1:]:
                tile_max = jnp.maximum(tile_max, tile)
            m = jnp.maximum(jnp.max(tile_max, axis=-1, keepdims=True), sink)
            tile_sum = None
            for tile in tiles:
                p = jnp.exp2(tile - m)
                tile_sum = p if tile_sum is None else tile_sum + p
                probs.append(p.astype(BF16))
            inv_den.append(1.0 / (jnp.sum(tile_sum, axis=-1, keepdims=True) + jnp.exp2(sink - m)))
        o = _bdot(jnp.concatenate(probs, axis=1), vz) * jnp.where(out_lo, inv_den[0], inv_den[1])
        o_ref[:, c0:c0 + LANES] = o[0:tq].astype(BF16)
        o_ref[:, c0 + LANES:c0 + 2 * LANES] = o[tq:2 * tq].astype(BF16)


def _split_kv(kv):
    return tuple(kv[:, j * KV_WIDTH:(j + 1) * KV_WIDTH] for j in range(4))


def _win_attn_kernel(sink_ref, q_ref, kvp_ref, kvc_ref, kvn_ref, kvx_ref, o_ref, *, tq):
    n = pl.program_id(1)
    nb = pl.num_programs(1)
    row = lax.broadcasted_iota(jnp.int32, (tq, tq), 0)
    col = lax.broadcasted_iota(jnp.int32, (tq, tq), 1)
    b_prev = jnp.where((col >= row) & (n > 0), 0.0, NEG_INF)
    b_next = jnp.where((col <= row) & (n < nb - 1), 0.0, NEG_INF)
    halo_bias = (jnp.concatenate([b_prev, b_prev], axis=0), jnp.concatenate([b_next, b_next], axis=0))

    kv = jnp.concatenate([kvp_ref[...], kvc_ref[...], kvn_ref[...], kvx_ref[...]], axis=0)
    _attn_body(q_ref[...], *_split_kv(kv), halo_bias, sink_ref, o_ref, tq)


def _win_attn_call(sink, q, kv, kv_ctx, n_seq, seq_len, ctx_len):
    tq = WINDOW
    nb = seq_len // tq

    def cur(b, n):
        return (b * nb + n, 0)

    def prev(b, n):
        return (b * nb + jnp.maximum(n - 1, 0), 0)

    def nxt(b, n):
        return (b * nb + jnp.minimum(n + 1, nb - 1), 0)

    return pl.pallas_call(
        functools.partial(_win_attn_kernel, tq=tq),
        grid=(n_seq, nb),
        in_specs=[pl.BlockSpec(memory_space=pltpu.SMEM), pl.BlockSpec((tq, ATTN_WIDTH), cur),
                  pl.BlockSpec((tq, KV_PACK_WIDTH), prev), pl.BlockSpec((tq, KV_PACK_WIDTH), cur),
                  pl.BlockSpec((tq, KV_PACK_WIDTH), nxt),
                  pl.BlockSpec((ctx_len, KV_PACK_WIDTH), lambda b, n: (b, 0))],
        out_specs=pl.BlockSpec((tq, ATTN_WIDTH), cur),
        out_shape=jax.ShapeDtypeStruct((n_seq * seq_len, ATTN_WIDTH), BF16),
        compiler_params=_params("parallel", "parallel"),
        name="window_attention",
    )(sink, q, kv, kv, kv, kv_ctx)


def _ctx_attn_kernel(sink_ref, q_ref, kv_ref, o_ref, *, tq):
    _attn_body(q_ref[...], *_split_kv(kv_ref[...]), None, sink_ref, o_ref, tq)


def _ctx_attn_call(sink, q, kv, n_seq, ctx_len):
    def blk(w):
        return pl.BlockSpec((ctx_len, w), lambda b: (b, 0))

    return pl.pallas_call(
        functools.partial(_ctx_attn_kernel, tq=ctx_len),
        grid=(n_seq,),
        in_specs=[pl.BlockSpec(memory_space=pltpu.SMEM), blk(ATTN_WIDTH), blk(KV_PACK_WIDTH)],
        out_specs=blk(ATTN_WIDTH),
        out_shape=jax.ShapeDtypeStruct((n_seq * ctx_len, ATTN_WIDTH), BF16),
        compiler_params=_params("parallel"),
        name="context_attention",
    )(sink, q, kv)


OUT_TILE = 512


def _outproj_kernel(attx_ref, attc_ref, yx_ref, yc_ref, xx_ref, xc_ref, wof_ref, g_ref, lng_ref, lnb_ref,
                    sc_ref, sh_ref, wr_ref, br_ref, before_ref,
                    x1_ref, h2v_ref, route_ref, routet_ref, cnt_ref, wo_ref,
                    *, alpha, n_lat_tiles, has_ctx):
    @pl.when(pl.program_id(0) == 0)
    def _():
        wo_ref[...] = wof_ref[...].astype(BF16)

    if has_ctx:
        is_lat = pl.program_id(0) < n_lat_tiles
        att = jnp.where(is_lat, attx_ref[...], attc_ref[...])
        yloc = jnp.where(is_lat, yx_ref[...], yc_ref[...])
        xin = jnp.where(is_lat, xx_ref[...], xc_ref[...])
    else:
        att, yloc, xin = attx_ref[...], yx_ref[...], xx_ref[...]
    o = _bdot(att, wo_ref[0:ATTN_WIDTH, :]) + _bdot(yloc, wo_ref[ATTN_WIDTH:, :])
    x1 = _layer_norm(alpha * xin + g_ref[...] * o) * lng_ref[...] + lnb_ref[...]
    x1_ref[...] = x1
    h2 = x1 * (1.0 + sc_ref[...]) + sh_ref[...]
    half = D_MODEL // 2
    hi_bits = lax.bitcast_convert_type(h2[:, 0:half].astype(BF16).astype(F32), jnp.uint32)
    lo_bits = lax.bitcast_convert_type(h2[:, half:].astype(BF16).astype(F32), jnp.uint32)
    packed = hi_bits | (lo_bits >> 16)
    for c in range(PACKED_TILE_ROWS):
        h2v_ref[:, c, :, :] = packed[:, c * LANES:(c + 1) * LANES].reshape(h2.shape[0] // SUBLANES, SUBLANES, LANES)

    hh, hl = _split_bf16(h2)
    t1 = _bdot(hh, wr_ref[...])
    lg = t1[:, 0:ROUTER_COLS] + t1[:, ROUTER_COLS:] + _bdot(hl, wr_ref[:, 0:ROUTER_COLS]) + br_ref[...]

    ts = lg.shape[0]
    lane = lax.broadcasted_iota(jnp.int32, (ts, ROUTER_COLS), 1).astype(F32)
    big = jnp.float32(ROUTER_COLS)

    def top1(v):
        m = jnp.max(v, axis=-1, keepdims=True)
        return m, jnp.min(jnp.where(v == m, lane, big), axis=-1, keepdims=True)

    gl = jnp.where(lane < N_GROUPS, lg, NEG_INF)
    g_val, g_idx = top1(gl)
    lse = g_val + jnp.log(jnp.sum(jnp.exp(gl - g_val), axis=-1, keepdims=True))
    p_group = jnp.exp(g_val - lse)
    e_lo = N_GROUPS + EXP_PER_GROUP * g_idx
    el = jnp.where((lane >= e_lo) & (lane < e_lo + EXP_PER_GROUP), lg, NEG_INF)
    e1, l1 = top1(el)
    e2, l2 = top1(jnp.where(lane == l1, NEG_INF, el))
    z = jnp.exp(e2 - e1)
    gate1 = p_group / (1.0 + z)
    gate2 = p_group * z / (1.0 + z)
    x1id = l1 - N_GROUPS
    x2id = l2 - N_GROUPS

    sel1 = lane == x1id
    sel2 = lane == x2id
    onehot = jnp.where(sel1 | sel2, 1.0, 0.0)
    prefix = _bdot(before_ref[...], onehot.astype(BF16))
    rank1 = jnp.sum(jnp.where(sel1, prefix, 0.0), axis=-1, keepdims=True)
    rank2 = jnp.sum(jnp.where(sel2, prefix, 0.0), axis=-1, keepdims=True)
    route = jnp.zeros((ts, ROUTER_COLS), F32)
    for col, val in enumerate((x1id, x2id, gate1, gate2, rank1, rank2)):
        route = jnp.where(lane == col, val, route)
    route_ref[...] = route
    routet_ref[...] = route.T[0:SUBLANES, :]
    cnt_ref[...] =jnp.broadcast_to(jnp.sum(onehot, axis=0, keepdims=True), (SUBLANES, ROUTER_COLS))


def _outproj_call(att_x, att_c, y_x, y_c, x_arr, x_off, c_arr, c_off, n_lat, n_ctx, seq_len,
                  w_out, layer, gvecs, lng, lnb, scvecs, shvecs, wr, br, alpha):
    ts = OUT_TILE
    d = D_MODEL
    n_lat_tiles = n_lat // ts
    n_tiles = (n_lat + n_ctx) // ts
    tiles_per_seq = seq_len // ts
    n_seq = n_lat // seq_len
    xo, co = x_off // ts, c_off // ts

    def lat_loc(i):
        return (jnp.minimum(i, n_lat_tiles - 1), 0)

    def ctx_loc(i):
        return (jnp.maximum(i - n_lat_tiles, 0), 0)

    def lat_in(i):
        return (xo + jnp.minimum(i, n_lat_tiles - 1), 0)

    def ctx_in(i):
        return (co + jnp.maximum(i - n_lat_tiles, 0), 0)

    def vec_idx(i):
        return (jnp.where(i < n_lat_tiles, i // tiles_per_seq, n_seq), 0, 0)

    vecb = pl.BlockSpec((None, 1, d), vec_idx)
    vec0 = pl.BlockSpec((1, d), lambda i: (0, 0))
    return pl.pallas_call(
        functools.partial(_outproj_kernel, alpha=alpha, n_lat_tiles=n_lat_tiles, has_ctx=n_ctx > 0),
        grid=(n_tiles,),
        in_specs=[
            pl.BlockSpec((ts, ATTN_WIDTH), lat_loc), pl.BlockSpec((ts, ATTN_WIDTH), ctx_loc),
            pl.BlockSpec((ts, LOCAL_WIDTH), lat_loc), pl.BlockSpec((ts, LOCAL_WIDTH), ctx_loc),
            pl.BlockSpec((ts, d), lat_in), pl.BlockSpec((ts, d), ctx_in),
            pl.BlockSpec((None, d, d), lambda i: (layer, 0, 0)),
            vecb, vec0, vec0, vecb, vecb,
            pl.BlockSpec((d, 2 * ROUTER_COLS), lambda i: (0, 0)),
            pl.BlockSpec((1, ROUTER_COLS), lambda i: (0, 0)),
            pl.BlockSpec((ts, ts), lambda i: (0, 0)),
        ],
        out_specs=[pl.BlockSpec((ts, d), lambda i: (i, 0)),
                   pl.BlockSpec((ts // SUBLANES, PACKED_TILE_ROWS, SUBLANES, LANES), lambda i: (i, 0, 0, 0)),
                   pl.BlockSpec((ts, ROUTER_COLS), lambda i: (i, 0)),
                   pl.BlockSpec((None, SUBLANES, ts), lambda i: (i, 0, 0)),
                   pl.BlockSpec((None, SUBLANES, ROUTER_COLS), lambda i: (i, 0, 0))],
        out_shape=[jax.ShapeDtypeStruct((n_lat + n_ctx, d), F32),
                   jax.ShapeDtypeStruct(((n_lat + n_ctx) // SUBLANES, PACKED_TILE_ROWS, SUBLANES, LANES),
                                        jnp.uint32),
                   jax.ShapeDtypeStruct((n_lat + n_ctx, ROUTER_COLS), F32),
                   jax.ShapeDtypeStruct((n_tiles, SUBLANES, ts), F32),
                   jax.ShapeDtypeStruct((n_tiles, SUBLANES, ROUTER_COLS), F32)],
        scratch_shapes=[pltpu.VMEM((d, d), BF16)],
        compiler_params=_params("arbitrary"),
        name="out_projection",
    )(att_x, att_c, y_x, y_c, x_arr, c_arr, w_out, gvecs, lng, lnb, scvecs, shvecs, wr, br,
      jnp.tril(jnp.ones((ts, ts), BF16), -1))


DISPATCH_TILE = 512


def _dispatch_kernel(seg_ref, pos_ref, h_ref, xs_out, zbuf, zsem, sem):
    block_tiles = MOE_BLOCK // TOKENS_PER_PACKED_TILE

    @pl.when(pl.program_id(0) == 0)
    def _():
        zbuf[...] = jnp.zeros_like(zbuf)

        def zero_copy(e):
            first_tile = (seg_ref[0, e] - MOE_BLOCK) // TOKENS_PER_PACKED_TILE
            return pltpu.make_async_copy(zbuf, xs_out.at[pl.ds(first_tile, block_tiles)], zsem)

        for e in range(N_EXPERTS):
            @pl.when(seg_ref[1, e] > 0)
            def _():
                zero_copy(e).start()
        for e in range(N_EXPERTS):
            @pl.when(seg_ref[1, e] > 0)
            def _():
                zero_copy(e).wait()

        def tail_copy(b):
            return pltpu.make_async_copy(zbuf, xs_out.at[pl.ds(b * block_tiles, block_tiles)], zsem)

        n_blocks = xs_out.shape[0] // block_tiles
        first_unused = seg_ref[0, N_EXPERTS - 1] // MOE_BLOCK
        lax.fori_loop(first_unused, n_blocks, lambda b, c: (tail_copy(b).start(), c)[1], 0)
        lax.fori_loop(first_unused, n_blocks, lambda b, c: (tail_copy(b).wait(), c)[1], 0)

    def body(j, carry):
        for s in range(SUBLANES):
            t = SUBLANES * j + s
            for k in range(TOP_K):
                p = pos_ref[0, 0, k * DISPATCH_TILE + t]
                dst = xs_out.at[p >> 1, pl.ds(PACKED_TILE_ROWS * (p & 1), PACKED_TILE_ROWS), :]
                pltpu.make_async_copy(h_ref.at[j, :, s, :], dst, sem).start(priority=k)
        return carry

    lax.fori_loop(0, DISPATCH_TILE // SUBLANES, body, 0)
    for k in range(TOP_K):
        pltpu.make_async_copy(h_ref, h_ref, sem).wait()


def _dispatch_call(seg, pos, h2t, n_slots):
    n_tok = h2t.shape[0] * SUBLANES
    n_tiles = n_tok // DISPATCH_TILE
    grid_spec = pltpu.PrefetchScalarGridSpec(
        num_scalar_prefetch=1,
        grid=(n_tiles,),
        in_specs=[
            pl.BlockSpec((1, 1, TOP_K * DISPATCH_TILE), lambda i, seg: (i, 0, 0), memory_space=pltpu.SMEM),
            pl.BlockSpec((DISPATCH_TILE // SUBLANES, PACKED_TILE_ROWS, SUBLANES, LANES),
                         lambda i, seg: (i, 0, 0, 0)),
        ],
        out_specs=pl.BlockSpec(memory_space=pl.ANY),
        scratch_shapes=[pltpu.VMEM((MOE_BLOCK // TOKENS_PER_PACKED_TILE, SUBLANES, LANES), jnp.uint32),
                        pltpu.SemaphoreType.DMA(()), pltpu.SemaphoreType.DMA(())],
    )
    return pl.pallas_call(
        _dispatch_kernel,
        grid_spec=grid_spec,
        out_shape=jax.ShapeDtypeStruct((n_slots // TOKENS_PER_PACKED_TILE, SUBLANES, LANES), jnp.uint32),
        compiler_params=_params("arbitrary"),
        name="moe_dispatch",
    )(seg, pos, h2t)


def _moe_kernel(be_ref, ne_ref, nu_ref, xs_hbm, w1_hbm, w3_hbm, w2_hbm, ys_hbm, w1b, w3b, w2b, w1f, w3f, w2f,
                xbuf, obuf, isem, osem, wsem, *, layer):
    i = pl.program_id(0)
    n_used = nu_ref[0]
    slot = i % 2
    groups = MOE_BLOCK // SUBLANES
    half = D_MODEL // 2

    def in_copies(blk, at_slot):
        return [pltpu.make_async_copy(
            xs_hbm.at[pl.ds(blk * groups, groups), s // TOKENS_PER_PACKED_TILE,
                      pl.ds(PACKED_TILE_ROWS * (s % TOKENS_PER_PACKED_TILE), PACKED_TILE_ROWS), :],
            xbuf.at[at_slot, :, :, s, :], isem.at[at_slot]) for s in range(SUBLANES)]

    def out_copies(blk, at_slot):
        return [pltpu.make_async_copy(obuf.at[at_slot, :, :, s, :], ys_hbm.at[pl.ds(blk * groups, groups), s],
                                      osem.at[at_slot]) for s in range(SUBLANES)]

    def write_rows(at_slot, y):
        for c in range(ROW_TILE_ROWS):
            obuf[at_slot, :, c, :, :] = y[:, c * LANES:(c + 1) * LANES].reshape(groups, SUBLANES, LANES)

    @pl.when(i == 0)
    def _():
        _start_all(in_copies(0, 0))

    @pl.when(i + 1 < n_used)
    def _():
        _start_all(in_copies(i + 1, 1 - slot))

    def weight_copies(e):
        return [pltpu.make_async_copy(src.at[layer, e], dst, wsem)
                for src, dst in ((w1_hbm, w1f), (w3_hbm, w3f), (w2_hbm, w2f))]

    @pl.when(i == 0)
    def _():
        _start_all(weight_copies(be_ref[0]))

    @pl.when((i < n_used) & ((i == 0) | (be_ref[i] != be_ref[jnp.maximum(i - 1, 0)])))
    def _():
        _wait_all(weight_copies(be_ref[i]))
        w1b[...] = w1f[...].astype(BF16)
        w3b[...] = w3f[...].astype(BF16)
        w2b[...] = w2f[...].astype(BF16)

        @pl.when(ne_ref[i] != be_ref[i])
        def _():
            _start_all(weight_copies(ne_ref[i]))

    @pl.when(i < n_used)
    def _():
        _wait_all(in_copies(i, slot))
        p = jnp.concatenate([xbuf[slot, :, c, :, :].reshape(MOE_BLOCK, LANES) for c in range(PACKED_TILE_ROWS)],
                            axis=1)
        xa = lax.bitcast_convert_type(p & jnp.uint32(0xFFFF0000), F32).astype(BF16)
        xb = lax.bitcast_convert_type(p << 16, F32).astype(BF16)
        h1 = _bdot(xa, w1b[0:half, :]) + _bdot(xb, w1b[half:, :])
        h3 = _bdot(xa, w3b[0:half, :]) + _bdot(xb, w3b[half:, :])
        write_rows(slot, _bdot((jax.nn.silu(h1) * h3).astype(BF16), w2b[...]))
        _start_all(out_copies(i, slot))

        @pl.when(i >= 1)
        def _():
            _wait_all(out_copies(i - 1, 1 - slot))

        @pl.when(i == n_used - 1)
        def _():
            _wait_all(out_copies(i, slot))

    @pl.when(i >= n_used)
    def _():
        obuf[slot] = jnp.zeros(obuf.shape[1:], F32)
        _start_all(out_copies(i, slot))
        _wait_all(out_copies(i, slot))


def _moe_call(block_e, next_e, n_used, xs, w1, w3, w2, layer):
    n_blocks = block_e.shape[0]
    n_slots = n_blocks * MOE_BLOCK
    d = D_MODEL
    groups = MOE_BLOCK // SUBLANES
    grid_spec = pltpu.PrefetchScalarGridSpec(
        num_scalar_prefetch=3,
        grid=(n_blocks,),
        in_specs=[pl.BlockSpec(memory_space=pl.ANY)] * 4,
        out_specs=pl.BlockSpec(memory_space=pl.ANY),
        scratch_shapes=[pltpu.VMEM((d, D_EXPERT), BF16), pltpu.VMEM((d, D_EXPERT), BF16),
                        pltpu.VMEM((D_EXPERT, d), BF16),
                        pltpu.VMEM((d, D_EXPERT), F32), pltpu.VMEM((d, D_EXPERT), F32),
                        pltpu.VMEM((D_EXPERT, d), F32),
                        pltpu.VMEM((2, groups, PACKED_TILE_ROWS, SUBLANES, LANES), jnp.uint32),
                        pltpu.VMEM((2, groups, ROW_TILE_ROWS, SUBLANES, LANES), F32),
                        pltpu.SemaphoreType.DMA((2,)), pltpu.SemaphoreType.DMA((2,)),
                        pltpu.SemaphoreType.DMA(())],
    )
    xs_view = (n_slots // SUBLANES, SUBLANES // TOKENS_PER_PACKED_TILE, SUBLANES, LANES)
    ys_view = (n_slots // SUBLANES, SUBLANES, ROW_TILE_ROWS, LANES)
    ys = pl.pallas_call(
        functools.partial(_moe_kernel, layer=layer),
        grid_spec=grid_spec,
        out_shape=jax.ShapeDtypeStruct(ys_view, F32),
        compiler_params=_params("arbitrary"),
        name="moe_experts",
    )(block_e, next_e, n_used, xs.reshape(xs_view), w1, w3, w2)
    return ys.reshape(n_slots, ROW_TILE_ROWS, LANES)


COMBINE_TILE = 256


def _combine_kernel(pos0_ref, posn_ref, x1_ref, route_ref, g_ref, lng_ref, lnb_ref, ys_hbm, o_ref, ybuf, sem,
                    *, alpha):
    i = pl.program_id(0)
    nb = pl.num_programs(0)
    n_rows = TOP_K * COMBINE_TILE

    def issue(pos_ref, slot):
        def body(j, carry):
            for s in range(SUBLANES):
                pltpu.make_async_copy(ys_hbm.at[pos_ref[0, 0, SUBLANES * j + s]], ybuf.at[slot, j, :, s, :],
                                      sem.at[slot]).start(priority=s % N_DMA_PRIORITIES)
            return carry
        lax.fori_loop(0, n_rows // SUBLANES, body, 0)

    @pl.when(i == 0)
    def _():
        issue(pos0_ref, 0)

    @pl.when(i + 1 < nb)
    def _():
        issue(posn_ref, (i + 1) % 2)

    slot = i % 2
    pltpu.make_async_copy(ybuf.at[slot], ybuf.at[slot], sem.at[slot]).wait()
    yrows = jnp.concatenate([ybuf[slot, :, c, :, :].reshape(n_rows, LANES) for c in range(ROW_TILE_ROWS)], axis=1)
    route = route_ref[...]
    y = (yrows[0:COMBINE_TILE, :] * route[:, ROUTE_GATE:ROUTE_GATE + 1]
         + yrows[COMBINE_TILE:, :] * route[:, ROUTE_GATE + 1:ROUTE_GATE + 2])
    o_ref[...] = _layer_norm(alpha * x1_ref[...] + g_ref[...] * y) * lng_ref[...] + lnb_ref[...]


def _combine_call(pos, x1, route, gvecs, lng, lnb, ys, n_tok, tiles_per_seq, n_seq, alpha):
    d = D_MODEL
    n_tiles = n_tok // COMBINE_TILE
    n_lat_tiles = tiles_per_seq * n_seq

    def g_idx(i):
        return (jnp.where(i < n_lat_tiles, i // tiles_per_seq, n_seq), 0, 0)

    smem_blk = (1, 1, TOP_K * COMBINE_TILE)
    return pl.pallas_call(
        functools.partial(_combine_kernel, alpha=alpha),
        grid=(n_tiles,),
        in_specs=[
            pl.BlockSpec(smem_blk, lambda i: (0, 0, 0), memory_space=pltpu.SMEM),
            pl.BlockSpec(smem_blk, lambda i: (jnp.minimum(i + 1, n_tiles - 1), 0, 0), memory_space=pltpu.SMEM),
            pl.BlockSpec((COMBINE_TILE, d), lambda i: (i, 0)),
            pl.BlockSpec((COMBINE_TILE, ROUTER_COLS), lambda i: (i, 0)),
            pl.BlockSpec((None, 1, d), g_idx),
            pl.BlockSpec((1, d), lambda i: (0, 0)),
            pl.BlockSpec((1, d), lambda i: (0, 0)),
            pl.BlockSpec(memory_space=pl.ANY),
        ],
        out_specs=pl.BlockSpec((COMBINE_TILE, d), lambda i: (i, 0)),
        out_shape=jax.ShapeDtypeStruct((n_tok, d), F32),
        scratch_shapes=[pltpu.VMEM((2, TOP_K * COMBINE_TILE // SUBLANES, ROW_TILE_ROWS, SUBLANES, LANES), F32),
                        pltpu.SemaphoreType.DMA((2,))],
        compiler_params=_params("arbitrary"),
        name="moe_combine",
    )(pos, pos, x1, route, gvecs, lng, lnb, ys)


def _slots(route_t, tile_counts):
    n_tiles = tile_counts.shape[0]
    n = n_tiles * route_t.shape[2]
    tc = tile_counts.astype(jnp.int32)
    counts = jnp.sum(tc, axis=0)
    padded = (counts + MOE_BLOCK - 1) // MOE_BLOCK * MOE_BLOCK
    pends = jnp.cumsum(padded)
    base = (pends - padded)[None, :] + jnp.cumsum(tc, axis=0) - tc
    n_blocks = -(-n * TOP_K // MOE_BLOCK) + N_EXPERTS
    block_start = jnp.arange(n_blocks, dtype=jnp.int32) * MOE_BLOCK
    block_e = jnp.minimum(jnp.sum(pends[None, 0:N_EXPERTS] <= block_start[:, None], axis=1),
                          N_EXPERTS - 1).astype(jnp.int32)
    ids = jnp.arange(N_EXPERTS, dtype=jnp.int32)
    later = (ids[None, :] > ids[:, None]) & (padded[None, 0:N_EXPERTS] > 0)
    next_nonempty = jnp.min(jnp.where(later, ids[None, :], N_EXPERTS), axis=1)
    next_of = jnp.where(next_nonempty < N_EXPERTS, next_nonempty, ids)
    next_e = jnp.sum(jnp.where(block_e[:, None] == ids[None, :], next_of[None, :], 0), axis=1).astype(jnp.int32)
    n_used = (pends[N_EXPERTS - 1] // MOE_BLOCK).astype(jnp.int32).reshape(1)
    seg = jnp.stack([pends[0:N_EXPERTS], padded[0:N_EXPERTS]]).astype(jnp.int32)
    pos = []
    for k in range(TOP_K):
        e = route_t[:, ROUTE_EXPERT + k, :].astype(jnp.int32)
        hit = e[:, None, :] == jnp.arange(N_EXPERTS, dtype=jnp.int32)[None, :, None]
        b = jnp.sum(jnp.where(hit, base[:, 0:N_EXPERTS, None], 0), axis=1)
        pos.append((b + route_t[:, ROUTE_RANK + k, :].astype(jnp.int32)).reshape(n))
    return block_e, next_e, n_used, seg, pos


def _rope_tables(seq_len):
    m = HEAD_DIM // 4
    freqs = ROPE_BASE ** (-jnp.arange(m, dtype=F32) / m)
    t = jnp.arange(seq_len)
    row = (t // GRID_W).astype(F32)[:, None] * freqs[None, :]
    col = (t % GRID_W).astype(F32)[:, None] * freqs[None, :]
    cos = jnp.concatenate([jnp.cos(row), jnp.cos(row), jnp.cos(col), jnp.cos(col)], axis=-1)
    zero = jnp.zeros_like(row)
    sin_a = jnp.concatenate([zero, jnp.sin(row), zero, jnp.sin(col)], axis=-1)
    sin_b = jnp.concatenate([-jnp.sin(row), zero, -jnp.sin(col), zero], axis=-1)
    rep = LANES // HEAD_DIM
    return jnp.tile(cos, (1, rep)), jnp.tile(sin_a, (1, rep)), jnp.tile(sin_b, (1, rep))


def kernel(x, c, ctx, c_ctx, w_ada, b_ada, w_in, conv_w, attn_sink, gm_ws, gm_bs, w_out, ln1_g, ln1_b,
           w_rg, b_rg, w_re, b_re, w1, w3, w2, ln2_g, ln2_b):
    b_, s_, d_ = x.shape
    c_len = ctx.shape[1]
    depth = w_ada.shape[0]
    alpha = (2 * depth) ** 0.25
    n_lat = b_ * s_
    n_ctx = b_ * c_len
    ts = 1024

    cin = jnp.zeros((ADA_ROWS, d_), F32).at[0:b_].set(c).at[b_].set(c_ctx)
    mod = _ada_call(cin, w_ada, b_ada)

    cos, sin_a, sin_b = _rope_tables(s_)
    ones_c = jnp.ones((c_len, LANES), F32)
    zeros_c = jnp.zeros((c_len, LANES), F32)

    x_flat = x.reshape(n_lat, d_)
    c_flat = ctx.reshape(n_ctx, d_)
    x_off, c_off = 0, 0
    x_arr, c_arr = x_flat, c_flat

    for l in range(depth):
        last = l == depth - 1
        mx = mod[l, 0:b_].reshape(b_, 6, 1, d_)
        sh1, sc1, g1, sh2, sc2, g2 = (mx[:, i] for i in range(6))
        mc = jnp.broadcast_to(mod[l, b_].reshape(1, 6, 1, d_), (b_, 6, 1, d_))
        gm_w = gm_ws[l].astype(BF16)
        gm_b = jnp.repeat(gm_bs[l].T, GM_HEAD, axis=1)
        sink = attn_sink[l]

        qx, kvx, yx = _inproj_call(x_arr, x_off, b_, s_, ts, sc1, sh1, w_in, l, cos, sin_a, sin_b,
                                   conv_w[l], gm_w, gm_b)
        qc, kvc, yc = _inproj_call(c_arr, c_off, b_, c_len, c_len, mc[:, 1], mc[:, 0], w_in, l,
                                   ones_c, zeros_c, zeros_c, conv_w[l], gm_w, gm_b)
        att_x = _win_attn_call(sink, qx, kvx, kvc, b_, s_, c_len)

        w_r = jnp.zeros((d_, ROUTER_COLS), F32).at[:, 0:N_GROUPS].set(w_rg[l]) \
            .at[:, N_GROUPS:N_GROUPS + N_EXPERTS].set(w_re[l])
        w_r_hi = w_r.astype(BF16)
        w_r_lo = (w_r - w_r_hi.astype(F32)).astype(BF16)
        wr = jnp.concatenate([w_r_hi, w_r_lo], axis=1)
        br = jnp.zeros((1, ROUTER_COLS), F32).at[0, 0:N_GROUPS].set(b_rg[l]) \
            .at[0, N_GROUPS:N_GROUPS + N_EXPERTS].set(b_re[l])
        lng1, lnb1 = ln1_g[l].reshape(1, d_), ln1_b[l].reshape(1, d_)
        lng2, lnb2 = ln2_g[l].reshape(1, d_), ln2_b[l].reshape(1, d_)

        n_tot = n_lat if last else n_lat + n_ctx
        gvecs1 = jnp.concatenate([g1, mc[0:1, 2]], axis=0)
        scvecs2 = jnp.concatenate([sc2, mc[0:1, 4]], axis=0)
        shvecs2 = jnp.concatenate([sh2, mc[0:1, 3]], axis=0)
        if last:
            x1, h2t, route, route_t, tcnt = _outproj_call(att_x, att_x, yx, yx, x_arr, x_off, x_arr, x_off, n_lat, 0, s_,
                                                 w_out, l, gvecs1, lng1, lnb1, scvecs2, shvecs2, wr, br, alpha)
        else:
            att_c = _ctx_attn_call(sink, qc, kvc, b_, c_len)
            x1, h2t, route, route_t, tcnt = _outproj_call(att_x, att_c, yx, yc, x_arr, x_off, c_arr, c_off, n_lat, n_ctx,
                                                 s_, w_out, l, gvecs1, lng1, lnb1, scvecs2, shvecs2, wr, br, alpha)

        block_e, next_e, n_used, seg, pos = _slots(route_t, tcnt[:, 0, :])
        n_slots = block_e.shape[0] * MOE_BLOCK

        def per_tile(tile):
            return jnp.concatenate([p.reshape(n_tot // tile, 1, tile) for p in pos], axis=2)

        xs = _dispatch_call(seg, per_tile(DISPATCH_TILE), h2t, n_slots)
        ys = _moe_call(block_e, next_e, n_used, xs, w1, w3, w2, l)

        gvecs = jnp.concatenate([g2, mc[0:1, 5]], axis=0)
        out = _combine_call(per_tile(COMBINE_TILE), x1, route, gvecs, lng2, lnb2, ys, n_tot, s_ // COMBINE_TILE,
                            b_, alpha)
        x_arr, x_off = out, 0
        c_arr, c_off = out, n_lat

    return x_arr.reshape(b_, s_, d_)
```

```python
import functools

import jax
import jax.numpy as jnp
from jax import lax
from jax.experimental import pallas as pl
from jax.experimental.pallas import tpu as pltpu

F32 = jnp.float32
BF16 = jnp.bfloat16

D_MODEL = 1024
GRID_W = 64
HEAD_DIM = 64
N_HEADS = 8
N_KV_HEADS = 2
ATTN_WIDTH = N_HEADS * HEAD_DIM
KV_WIDTH = N_KV_HEADS * HEAD_DIM
WINDOW = 128
ATTN_SCALE = HEAD_DIM ** -0.5
LOG2E = 1.4426950408889634
ROPE_BASE = 10000.0
CONV_WIDTH = D_MODEL // 4
GM_WIDTH = D_MODEL // 4
GM_GROUPS = 4
GM_HEAD = GM_WIDTH // GM_GROUPS
CHUNK = 128
LOCAL_WIDTH = CONV_WIDTH + GM_WIDTH
IN_WIDTH = ATTN_WIDTH + 2 * KV_WIDTH + 3 * CONV_WIDTH + 2 * GM_WIDTH
QKV_WIDTH = ATTN_WIDTH + 2 * KV_WIDTH
KV_PACK_WIDTH = 4 * KV_WIDTH
N_GROUPS = 4
EXP_PER_GROUP = 8
N_EXPERTS = N_GROUPS * EXP_PER_GROUP
TOP_K = 2
D_EXPERT = D_MODEL // 2
MOE_BLOCK = 512
LN_EPS = 1e-6
NEG_INF = -1e30

LANES = 128
SUBLANES = 8
VMEM_LIMIT_BYTES = 48 * 1024 * 1024
N_DMA_PRIORITIES = 2

ROW_TILE_ROWS = D_MODEL // LANES
assert ROW_TILE_ROWS == SUBLANES
PACKED_TILE_ROWS = ROW_TILE_ROWS // 2
TOKENS_PER_PACKED_TILE = SUBLANES // PACKED_TILE_ROWS
assert TOKENS_PER_PACKED_TILE == 2

ROUTER_COLS = LANES
ROUTE_EXPERT = 0
ROUTE_GATE = TOP_K
ROUTE_RANK = 2 * TOP_K
ROPE_HALF_PAIR = HEAD_DIM // 4


def _bdot(a, b):
    return jnp.dot(a, b, preferred_element_type=F32)


def _split_bf16(a):
    hi = a.astype(BF16)
    lo = (a - hi.astype(F32)).astype(BF16)
    return hi, lo


def _layer_norm(r):
    mu = jnp.mean(r, axis=-1, keepdims=True)
    d = r - mu
    var = jnp.mean(d * d, axis=-1, keepdims=True)
    return d * lax.rsqrt(var + LN_EPS)


def _params(*sem, flags=None):
    return pltpu.CompilerParams(dimension_semantics=sem, vmem_limit_bytes=VMEM_LIMIT_BYTES, flags=flags)


def _start_all(copies):
    for cp in copies:
        cp.start()


def _wait_all(copies):
    for cp in copies:
        cp.wait()


ADA_ROWS = 16
ADA_TILE = 1536


def _ada_kernel(c_ref, w_ref, b_ref, o_ref):
    act = jax.nn.silu(c_ref[...])
    ah, al = _split_bf16(act)
    wh, wl = _split_bf16(w_ref[...])
    o_ref[...] = _bdot(ah, wh) + _bdot(ah, wl) + _bdot(al, wh) + b_ref[...]


def _ada_call(cin, w_ada, b_ada):
    depth, d, n = w_ada.shape
    return pl.pallas_call(
        _ada_kernel,
        grid=(depth, n // ADA_TILE),
        in_specs=[
            pl.BlockSpec((ADA_ROWS, d), lambda l, j: (0, 0)),
            pl.BlockSpec((None, d, ADA_TILE), lambda l, j: (l, 0, j)),
            pl.BlockSpec((None, 1, ADA_TILE), lambda l, j: (l, 0, j)),
        ],
        out_specs=pl.BlockSpec((None, ADA_ROWS, ADA_TILE), lambda l, j: (l, 0, j)),
        out_shape=jax.ShapeDtypeStruct((depth, ADA_ROWS, n), F32),
        compiler_params=_params("parallel", "parallel"),
        name="ada_modulation",
    )(cin, w_ada, b_ada.reshape(depth, 1, n))


def _inproj_kernel(x_ref, xp_ref, xn_ref, sc_ref, sh_ref, wf_ref, cos_ref, sa_ref, sb_ref,
                   cw_ref, gw_ref, gb_ref,
                   q_ref, kv_ref, y_ref, w_ref, *, ts):
    t = pl.program_id(1)
    nt = pl.num_programs(1)

    @pl.when((pl.program_id(0) == 0) & (t == 0))
    def _():
        w_ref[...] = wf_ref[...].astype(BF16)

    sc = 1.0 + sc_ref[...]
    sh = sh_ref[...]
    hx = (x_ref[...] * sc + sh).astype(BF16)

    pq = _bdot(hx, w_ref[:, 0:QKV_WIDTH])
    cos = cos_ref[...]
    sa = sa_ref[...]
    sb = sb_ref[...]

    def rope(z):
        return (z * cos + pltpu.roll(z, ROPE_HALF_PAIR, 1) * sa
                + pltpu.roll(z, LANES - ROPE_HALF_PAIR, 1) * sb)

    for j in range(ATTN_WIDTH // LANES):
        sl = slice(j * LANES, (j + 1) * LANES)
        q_ref[:, sl] = (rope(pq[:, sl]) * (ATTN_SCALE * LOG2E)).astype(BF16)
    kr = rope(pq[:, ATTN_WIDTH:ATTN_WIDTH + KV_WIDTH])
    vv = pq[:, ATTN_WIDTH + KV_WIDTH:QKV_WIDTH]
    for j, part in enumerate((kr, pltpu.roll(kr, HEAD_DIM, 1), vv, pltpu.roll(vv, HEAD_DIM, 1))):
        kv_ref[:, j * KV_WIDTH:(j + 1) * KV_WIDTH] = part.astype(BF16)

    pm = _bdot(hx, w_ref[:, QKV_WIDTH:IN_WIDTH])
    cb = pm[:, 0:CONV_WIDTH]
    u = pm[:, CONV_WIDTH:2 * CONV_WIDTH] * pm[:, 2 * CONV_WIDTH:3 * CONV_WIDTH]

    halo = jnp.concatenate([xp_ref[...], xn_ref[...]], axis=0)
    hh = (halo * sc + sh).astype(BF16)
    ph = _bdot(hh, w_ref[:, QKV_WIDTH + CONV_WIDTH:QKV_WIDTH + 3 * CONV_WIDTH])
    uh = ph[:, 0:CONV_WIDTH] * ph[:, CONV_WIDTH:2 * CONV_WIDTH]
    up_row = jnp.where(t > 0, uh[SUBLANES - 1:SUBLANES, :], 0.0)
    dn_row = jnp.where(t < nt - 1, uh[SUBLANES:SUBLANES + 1, :], 0.0)
    ridx = lax.broadcasted_iota(jnp.int32, (ts, CONV_WIDTH), 0)
    u_up = jnp.where(ridx == 0, up_row, pltpu.roll(u, 1, 0))
    u_dn = jnp.where(ridx == ts - 1, dn_row, pltpu.roll(u, ts - 1, 0))
    cw = cw_ref[...]
    y_conv = cb * (u_up * cw[0:1, :] + u * cw[1:2, :] + u_dn * cw[2:3, :])
    y_ref[:, 0:CONV_WIDTH] = y_conv.astype(BF16)

    gu = jax.nn.gelu(pm[:, 3 * CONV_WIDTH:3 * CONV_WIDTH + GM_WIDTH])
    gv = _layer_norm(jax.nn.gelu(pm[:, 3 * CONV_WIDTH + GM_WIDTH:3 * CONV_WIDTH + 2 * GM_WIDTH])).astype(BF16)
    lane = lax.broadcasted_iota(jnp.int32, (CHUNK, LANES), 1)
    zero = jnp.zeros((CHUNK, LANES), BF16)
    for c in range(ts // CHUNK):
        rows = slice(c * CHUNK, (c + 1) * CHUNK)
        for j in range(GM_WIDTH // LANES):
            cols = slice(j * LANES, (j + 1) * LANES)
            vp = gv[rows, cols]
            s = (_bdot(gw_ref[2 * j], jnp.where(lane < GM_HEAD, vp, zero))
                 + _bdot(gw_ref[2 * j + 1], jnp.where(lane >= GM_HEAD, vp, zero))
                 + gb_ref[:, cols])
            y_ref[rows, CONV_WIDTH + j * LANES:CONV_WIDTH + (j + 1) * LANES] = (gu[rows, cols] * s).astype(BF16)


def _inproj_call(x2d, row_off, n_seq, seq_len, ts, sc, sh, w_in, layer, cos, sa, sb, conv_w, gm_w, gm_b):
    nt = seq_len // ts
    off_t = row_off // ts
    off_8 = row_off // SUBLANES
    last_8 = x2d.shape[0] // SUBLANES - 1
    per_tile_8 = ts // SUBLANES
    n_out = n_seq * seq_len
    d = D_MODEL

    def tile_idx(b, t):
        return (off_t + b * nt + t, 0)

    def prev_idx(b, t):
        return (jnp.maximum(off_8 + (b * nt + t) * per_tile_8 - 1, 0), 0)

    def next_idx(b, t):
        return (jnp.minimum(off_8 + (b * nt + t + 1) * per_tile_8, last_8), 0)

    def out_idx(b, t):
        return (b * nt + t, 0)

    vec = pl.BlockSpec((None, 1, d), lambda b, t: (b, 0, 0))
    rope_spec = pl.BlockSpec((ts, LANES), lambda b, t: (t, 0))
    return pl.pallas_call(
        functools.partial(_inproj_kernel, ts=ts),
        grid=(n_seq, nt),
        in_specs=[
            pl.BlockSpec((ts, d), tile_idx),
            pl.BlockSpec((SUBLANES, d), prev_idx),
            pl.BlockSpec((SUBLANES, d), next_idx),
            vec, vec,
            pl.BlockSpec((None, d, IN_WIDTH), lambda b, t: (layer, 0, 0)),
            rope_spec, rope_spec, rope_spec,
            pl.BlockSpec((3, CONV_WIDTH), lambda b, t: (0, 0)),
            pl.BlockSpec((GM_GROUPS, CHUNK, CHUNK), lambda b, t: (0, 0, 0)),
            pl.BlockSpec((CHUNK, GM_WIDTH), lambda b, t: (0, 0)),
        ],
        out_specs=[
            pl.BlockSpec((ts, ATTN_WIDTH), out_idx),
            pl.BlockSpec((ts, KV_PACK_WIDTH), out_idx),
            pl.BlockSpec((ts, LOCAL_WIDTH), out_idx),
        ],
        out_shape=[
            jax.ShapeDtypeStruct((n_out, ATTN_WIDTH), BF16),
            jax.ShapeDtypeStruct((n_out, KV_PACK_WIDTH), BF16),
            jax.ShapeDtypeStruct((n_out, LOCAL_WIDTH), BF16),
        ],
        scratch_shapes=[pltpu.VMEM((d, IN_WIDTH), BF16)],
        compiler_params=_params("arbitrary", "arbitrary"),
        name="in_projection",
    )(x2d, x2d, x2d, sc, sh, w_in, cos, sa, sb, conv_w, gm_w, gm_b)


def _attn_body(q, keys, keys_sw, vals, vals_sw, halo_bias, sink_ref, o_ref, tq):
    nk = keys.shape[0]
    lane = lax.broadcasted_iota(jnp.int32, (nk, LANES), 1)
    lo = lane < HEAD_DIM
    zero = jnp.zeros((nk, LANES), BF16)
    rid = lax.broadcasted_iota(jnp.int32, (2 * tq, 1), 0)
    out_lo = lax.broadcasted_iota(jnp.int32, (2 * tq, LANES), 1) < HEAD_DIM
    nt_dims = (((1,), (1,)), ((), ()))
    gqa = N_HEADS // N_KV_HEADS
    biases = [None] * (nk // LANES)
    if halo_bias is not None:
        assert tq == LANES
        biases[0], biases[2] = halo_bias
    for h in range(N_KV_HEADS):
        k_own, k_oth = (keys, keys_sw) if h == 0 else (keys_sw, keys)
        v_own, v_oth = (vals, vals_sw) if h == 0 else (vals_sw, vals)
        kz = jnp.concatenate([jnp.where(lo, k_own, zero), jnp.where(lo, zero, k_oth)], axis=0)
        vz = jnp.concatenate([jnp.where(lo, v_own, zero), jnp.where(lo, zero, v_oth)], axis=0)
        c0 = h * gqa * HEAD_DIM
        qs = jnp.concatenate([q[:, c0:c0 + LANES], q[:, c0 + LANES:c0 + 2 * LANES]], axis=0)
        s_all = lax.dot_general(qs, kz, nt_dims, preferred_element_type=F32)
        probs, inv_den = [], []
        for par in range(2):
            tiles = []
            for j, bias in enumerate(biases):
                tile = s_all[:, par * nk + j * LANES:par * nk + (j + 1) * LANES]
                tiles.append(tile if bias is None else tile + bias)
            sink = jnp.where(rid < tq, sink_ref[h * gqa + par], sink_ref[h * gqa + 2 + par]) * LOG2E
            tile_max = tiles[0]
            for tile in tiles[1:]:
                tile_max = jnp.maximum(tile_max, tile)
            m = jnp.maximum(jnp.max(tile_max, axis=-1, keepdims=True), sink)
            tile_sum = None
            for tile in tiles:
                p = jnp.exp2(tile - m)
                tile_sum = p if tile_sum is None else tile_sum + p
                probs.append(p.astype(BF16))
            inv_den.append(1.0 / (jnp.sum(tile_sum, axis=-1, keepdims=True) + jnp.exp2(sink - m)))
        o = _bdot(jnp.concatenate(probs, axis=1), vz) * jnp.where(out_lo, inv_den[0], inv_den[1])
        o_ref[:, c0:c0 + LANES] = o[0:tq].astype(BF16)
        o_ref[:, c0 + LANES:c0 + 2 * LANES] = o[tq:2 * tq].astype(BF16)


def _split_kv(kv):
    return tuple(kv[:, j * KV_WIDTH:(j + 1) * KV_WIDTH] for j in range(4))


def _win_attn_kernel(sink_ref, q_ref, kvp_ref, kvc_ref, kvn_ref, kvx_ref, o_ref, *, tq):
    n = pl.program_id(1)
    nb = pl.num_programs(1)
    row = lax.broadcasted_iota(jnp.int32, (tq, tq), 0)
    col = lax.broadcasted_iota(jnp.int32, (tq, tq), 1)
    b_prev = jnp.where((col >= row) & (n > 0), 0.0, NEG_INF)
    b_next = jnp.where((col <= row) & (n < nb - 1), 0.0, NEG_INF)
    halo_bias = (jnp.concatenate([b_prev, b_prev], axis=0), jnp.concatenate([b_next, b_next], axis=0))

    kv = jnp.concatenate([kvp_ref[...], kvc_ref[...], kvn_ref[...], kvx_ref[...]], axis=0)
    _attn_body(q_ref[...], *_split_kv(kv), halo_bias, sink_ref, o_ref, tq)


def _win_attn_call(sink, q, kv, kv_ctx, n_seq, seq_len, ctx_len):
    tq = WINDOW
    nb = seq_len // tq

    def cur(b, n):
        return (b * nb + n, 0)

    def prev(b, n):
        return (b * nb + jnp.maximum(n - 1, 0), 0)

    def nxt(b, n):
        return (b * nb + jnp.minimum(n + 1, nb - 1), 0)

    return pl.pallas_call(
        functools.partial(_win_attn_kernel, tq=tq),
        grid=(n_seq, nb),
        in_specs=[pl.BlockSpec(memory_space=pltpu.SMEM), pl.BlockSpec((tq, ATTN_WIDTH), cur),
                  pl.BlockSpec((tq, KV_PACK_WIDTH), prev), pl.BlockSpec((tq, KV_PACK_WIDTH), cur),
                  pl.BlockSpec((tq, KV_PACK_WIDTH), nxt),
                  pl.BlockSpec((ctx_len, KV_PACK_WIDTH), lambda b, n: (b, 0))],
        out_specs=pl.BlockSpec((tq, ATTN_WIDTH), cur),
        out_shape=jax.ShapeDtypeStruct((n_seq * seq_len, ATTN_WIDTH), BF16),
        compiler_params=_params("parallel", "parallel"),
        name="window_attention",
    )(sink, q, kv, kv, kv, kv_ctx)


def _ctx_attn_kernel(sink_ref, q_ref, kv_ref, o_ref, *, tq):
    _attn_body(q_ref[...], *_split_kv(kv_ref[...]), None, sink_ref, o_ref, tq)


def _ctx_attn_call(sink, q, kv, n_seq, ctx_len):
    def blk(w):
        return pl.BlockSpec((ctx_len, w), lambda b: (b, 0))

    return pl.pallas_call(
        functools.partial(_ctx_attn_kernel, tq=ctx_len),
        grid=(n_seq,),
        in_specs=[pl.BlockSpec(memory_space=pltpu.SMEM), blk(ATTN_WIDTH), blk(KV_PACK_WIDTH)],
        out_specs=blk(ATTN_WIDTH),
        out_shape=jax.ShapeDtypeStruct((n_seq * ctx_len, ATTN_WIDTH), BF16),
        compiler_params=_params("parallel"),
        name="context_attention",
    )(sink, q, kv)


OUT_TILE = 512


def _outproj_kernel(attx_ref, attc_ref, yx_ref, yc_ref, xx_ref, xc_ref, wof_ref, g_ref, lng_ref, lnb_ref,
                    sc_ref, sh_ref, wr_ref, br_ref, before_ref,
                    x1_ref, h2t_hbm, route_ref, routet_ref, cnt_ref, wo_ref, hbuf, hsem,
                    *, alpha, n_lat_tiles, has_ctx):
    @pl.when(pl.program_id(0) == 0)
    def _():
        wo_ref[...] = wof_ref[...].astype(BF16)

    if has_ctx:
        is_lat = pl.program_id(0) < n_lat_tiles
        att = jnp.where(is_lat, attx_ref[...], attc_ref[...])
        yloc = jnp.where(is_lat, yx_ref[...], yc_ref[...])
        xin = jnp.where(is_lat, xx_ref[...], xc_ref[...])
    else:
        att, yloc, xin = attx_ref[...], yx_ref[...], xx_ref[...]
    o = _bdot(att, wo_ref[0:ATTN_WIDTH, :]) + _bdot(yloc, wo_ref[ATTN_WIDTH:, :])
    x1 = _layer_norm(alpha * xin + g_ref[...] * o) * lng_ref[...] + lnb_ref[...]
    x1_ref[...] = x1
    h2 = x1 * (1.0 + sc_ref[...]) + sh_ref[...]
    half = D_MODEL // 2
    hi_bits = lax.bitcast_convert_type(h2[:, 0:half].astype(BF16).astype(F32), jnp.uint32)
    lo_bits = lax.bitcast_convert_type(h2[:, half:].astype(BF16).astype(F32), jnp.uint32)
    packed = hi_bits | (lo_bits >> 16)
    step = pl.program_id(0)
    slot = step % 2
    groups = h2.shape[0] // SUBLANES

    def h2_copies(at_step, at_slot):
        return [pltpu.make_async_copy(
            hbuf.at[at_slot, :, :, s, :],
            h2t_hbm.at[pl.ds(at_step * groups, groups), s // TOKENS_PER_PACKED_TILE,
                       pl.ds(PACKED_TILE_ROWS * (s % TOKENS_PER_PACKED_TILE), PACKED_TILE_ROWS), :],
            hsem.at[at_slot]) for s in range(SUBLANES)]

    for c in range(PACKED_TILE_ROWS):
        hbuf[slot, :, c, :, :] = packed[:, c * LANES:(c + 1) * LANES].reshape(groups, SUBLANES, LANES)
    _start_all(h2_copies(step, slot))

    @pl.when(step >= 1)
    def _():
        _wait_all(h2_copies(step - 1, 1 - slot))

    @pl.when(step == pl.num_programs(0) - 1)
    def _():
        _wait_all(h2_copies(step, slot))

    hh, hl = _split_bf16(h2)
    t1 = _bdot(hh, wr_ref[...])
    lg = t1[:, 0:ROUTER_COLS] + t1[:, ROUTER_COLS:] + _bdot(hl, wr_ref[:, 0:ROUTER_COLS]) + br_ref[...]

    ts = lg.shape[0]
    lane = lax.broadcasted_iota(jnp.int32, (ts, ROUTER_COLS), 1).astype(F32)
    big = jnp.float32(ROUTER_COLS)

    def top1(v):
        m = jnp.max(v, axis=-1, keepdims=True)
        return m, jnp.min(jnp.where(v == m, lane, big), axis=-1, keepdims=True)

    gl = jnp.where(lane < N_GROUPS, lg, NEG_INF)
    g_val, g_idx = top1(gl)
    lse = g_val + jnp.log(jnp.sum(jnp.exp(gl - g_val), axis=-1, keepdims=True))
    p_group = jnp.exp(g_val - lse)
    e_lo = N_GROUPS + EXP_PER_GROUP * g_idx
    el = jnp.where((lane >= e_lo) & (lane < e_lo + EXP_PER_GROUP), lg, NEG_INF)
    e1, l1 = top1(el)
    e2, l2 = top1(jnp.where(lane == l1, NEG_INF, el))
    z = jnp.exp(e2 - e1)
    gate1 = p_group / (1.0 + z)
    gate2 = p_group * z / (1.0 + z)
    x1id = l1 - N_GROUPS
    x2id = l2 - N_GROUPS

    sel1 = lane == x1id
    sel2 = lane == x2id
    onehot = jnp.where(sel1 | sel2, 1.0, 0.0)
    prefix = _bdot(before_ref[...], onehot.astype(BF16))
    rank1 = jnp.sum(jnp.where(sel1, prefix, 0.0), axis=-1, keepdims=True)
    rank2 = jnp.sum(jnp.where(sel2, prefix, 0.0), axis=-1, keepdims=True)
    route = jnp.zeros((ts, ROUTER_COLS), F32)
    for col, val in enumerate((x1id, x2id, gate1, gate2, rank1, rank2)):
        route = jnp.where(lane == col, val, route)
    route_ref[...] = route
    routet_ref[...] = route.T[0:SUBLANES, :]
    cnt_ref[...] =jnp.broadcast_to(jnp.sum(onehot, axis=0, keepdims=True), (SUBLANES, ROUTER_COLS))


def _outproj_call(att_x, att_c, y_x, y_c, x_arr, x_off, c_arr, c_off, n_lat, n_ctx, seq_len,
                  w_out, layer, gvecs, lng, lnb, scvecs, shvecs, wr, br, alpha):
    ts = OUT_TILE
    d = D_MODEL
    n_lat_tiles = n_lat // ts
    n_tiles = (n_lat + n_ctx) // ts
    tiles_per_seq = seq_len // ts
    n_seq = n_lat // seq_len
    xo, co = x_off // ts, c_off // ts

    def lat_loc(i):
        return (jnp.minimum(i, n_lat_tiles - 1), 0)

    def ctx_loc(i):
        return (jnp.maximum(i - n_lat_tiles, 0), 0)

    def lat_in(i):
        return (xo + jnp.minimum(i, n_lat_tiles - 1), 0)

    def ctx_in(i):
        return (co + jnp.maximum(i - n_lat_tiles, 0), 0)

    def vec_idx(i):
        return (jnp.where(i < n_lat_tiles, i // tiles_per_seq, n_seq), 0, 0)

    vecb = pl.BlockSpec((None, 1, d), vec_idx)
    vec0 = pl.BlockSpec((1, d), lambda i: (0, 0))
    return pl.pallas_call(
        functools.partial(_outproj_kernel, alpha=alpha, n_lat_tiles=n_lat_tiles, has_ctx=n_ctx > 0),
        grid=(n_tiles,),
        in_specs=[
            pl.BlockSpec((ts, ATTN_WIDTH), lat_loc), pl.BlockSpec((ts, ATTN_WIDTH), ctx_loc),
            pl.BlockSpec((ts, LOCAL_WIDTH), lat_loc), pl.BlockSpec((ts, LOCAL_WIDTH), ctx_loc),
            pl.BlockSpec((ts, d), lat_in), pl.BlockSpec((ts, d), ctx_in),
            pl.BlockSpec((None, d, d), lambda i: (layer, 0, 0)),
            vecb, vec0, vec0, vecb, vecb,
            pl.BlockSpec((d, 2 * ROUTER_COLS), lambda i: (0, 0)),
            pl.BlockSpec((1, ROUTER_COLS), lambda i: (0, 0)),
            pl.BlockSpec((ts, ts), lambda i: (0, 0)),
        ],
        out_specs=[pl.BlockSpec((ts, d), lambda i: (i, 0)),
                   pl.BlockSpec(memory_space=pl.ANY),
                   pl.BlockSpec((ts, ROUTER_COLS), lambda i: (i, 0)),
                   pl.BlockSpec((None, SUBLANES, ts), lambda i: (i, 0, 0)),
                   pl.BlockSpec((None, SUBLANES, ROUTER_COLS), lambda i: (i, 0, 0))],
        out_shape=[jax.ShapeDtypeStruct((n_lat + n_ctx, d), F32),
                   jax.ShapeDtypeStruct(((n_lat + n_ctx) // SUBLANES, PACKED_TILE_ROWS, SUBLANES, LANES),
                                        jnp.uint32),
                   jax.ShapeDtypeStruct((n_lat + n_ctx, ROUTER_COLS), F32),
                   jax.ShapeDtypeStruct((n_tiles, SUBLANES, ts), F32),
                   jax.ShapeDtypeStruct((n_tiles, SUBLANES, ROUTER_COLS), F32)],
        scratch_shapes=[pltpu.VMEM((d, d), BF16),
                        pltpu.VMEM((2, ts // SUBLANES, PACKED_TILE_ROWS, SUBLANES, LANES), jnp.uint32),
                        pltpu.SemaphoreType.DMA((2,))],
        compiler_params=_params("arbitrary"),
        name="out_projection",
    )(att_x, att_c, y_x, y_c, x_arr, c_arr, w_out, gvecs, lng, lnb, scvecs, shvecs, wr, br,
      jnp.tril(jnp.ones((ts, ts), BF16), -1))


DISPATCH_TILE = 512


def _dispatch_kernel(seg_ref, pos_ref, h_hbm, xs_out, zbuf, zsem, sem):
    block_tiles = MOE_BLOCK // TOKENS_PER_PACKED_TILE

    @pl.when(pl.program_id(0) == 0)
    def _():
        zbuf[...] = jnp.zeros_like(zbuf)

        def zero_copy(e):
            first_tile = (seg_ref[0, e] - MOE_BLOCK) // TOKENS_PER_PACKED_TILE
            return pltpu.make_async_copy(zbuf, xs_out.at[pl.ds(first_tile, block_tiles)], zsem)

        for e in range(N_EXPERTS):
            @pl.when(seg_ref[1, e] > 0)
            def _():
                zero_copy(e).start()
        for e in range(N_EXPERTS):
            @pl.when(seg_ref[1, e] > 0)
            def _():
                zero_copy(e).wait()

        def tail_copy(b):
            return pltpu.make_async_copy(zbuf, xs_out.at[pl.ds(b * block_tiles, block_tiles)], zsem)

        n_blocks = xs_out.shape[0] // block_tiles
        first_unused = seg_ref[0, N_EXPERTS - 1] // MOE_BLOCK
        lax.fori_loop(first_unused, n_blocks, lambda b, c: (tail_copy(b).start(), c)[1], 0)
        lax.fori_loop(first_unused, n_blocks, lambda b, c: (tail_copy(b).wait(), c)[1], 0)

    tile_tiles = DISPATCH_TILE // TOKENS_PER_PACKED_TILE
    first_tile = pl.program_id(0) * tile_tiles

    def body(j, carry):
        for s in range(SUBLANES):
            t = SUBLANES * j + s
            src = h_hbm.at[first_tile + (SUBLANES // TOKENS_PER_PACKED_TILE) * j + s // TOKENS_PER_PACKED_TILE,
                           pl.ds(PACKED_TILE_ROWS * (s % TOKENS_PER_PACKED_TILE), PACKED_TILE_ROWS), :]
            for k in range(TOP_K):
                p = pos_ref[0, 0, k * DISPATCH_TILE + t]
                dst = xs_out.at[p >> 1, pl.ds(PACKED_TILE_ROWS * (p & 1), PACKED_TILE_ROWS), :]
                pltpu.make_async_copy(src, dst, sem).start(priority=k)
        return carry

    lax.fori_loop(0, DISPATCH_TILE // SUBLANES, body, 0)
    this_tile = h_hbm.at[pl.ds(first_tile, tile_tiles)]
    for k in range(TOP_K):
        pltpu.make_async_copy(this_tile, this_tile, sem).wait()


def _dispatch_call(seg, pos, h2t, n_slots):
    n_tok = h2t.shape[0] * SUBLANES
    n_tiles = n_tok // DISPATCH_TILE
    grid_spec = pltpu.PrefetchScalarGridSpec(
        num_scalar_prefetch=1,
        grid=(n_tiles,),
        in_specs=[
            pl.BlockSpec((1, 1, TOP_K * DISPATCH_TILE), lambda i, seg: (i, 0, 0), memory_space=pltpu.SMEM),
            pl.BlockSpec(memory_space=pl.ANY),
        ],
        out_specs=pl.BlockSpec(memory_space=pl.ANY),
        scratch_shapes=[pltpu.VMEM((MOE_BLOCK // TOKENS_PER_PACKED_TILE, SUBLANES, LANES), jnp.uint32),
                        pltpu.SemaphoreType.DMA(()), pltpu.SemaphoreType.DMA(())],
    )
    return pl.pallas_call(
        _dispatch_kernel,
        grid_spec=grid_spec,
        out_shape=jax.ShapeDtypeStruct((n_slots // TOKENS_PER_PACKED_TILE, SUBLANES, LANES), jnp.uint32),
        compiler_params=_params("arbitrary"),
        name="moe_dispatch",
    )(seg, pos, h2t.reshape(n_tok // TOKENS_PER_PACKED_TILE, SUBLANES, LANES))


def _moe_kernel(be_ref, ne_ref, nu_ref, xs_hbm, w1_hbm, w3_hbm, w2_hbm, ys_hbm, w1b, w3b, w2b, w1f, w3f, w2f,
                xbuf, obuf, isem, osem, wsem, *, layer):
    i = pl.program_id(0)
    n_used = nu_ref[0]
    slot = i % 2
    groups = MOE_BLOCK // SUBLANES
    half = D_MODEL // 2

    def in_copies(blk, at_slot):
        return [pltpu.make_async_copy(
            xs_hbm.at[pl.ds(blk * groups, groups), s // TOKENS_PER_PACKED_TILE,
                      pl.ds(PACKED_TILE_ROWS * (s % TOKENS_PER_PACKED_TILE), PACKED_TILE_ROWS), :],
            xbuf.at[at_slot, :, :, s, :], isem.at[at_slot]) for s in range(SUBLANES)]

    def out_copies(blk, at_slot):
        return [pltpu.make_async_copy(obuf.at[at_slot, :, :, s, :], ys_hbm.at[pl.ds(blk * groups, groups), s],
                                      osem.at[at_slot]) for s in range(SUBLANES)]

    def write_rows(at_slot, y):
        for c in range(ROW_TILE_ROWS):
            obuf[at_slot, :, c, :, :] = y[:, c * LANES:(c + 1) * LANES].reshape(groups, SUBLANES, LANES)

    @pl.when(i == 0)
    def _():
        _start_all(in_copies(0, 0))

    @pl.when(i + 1 < n_used)
    def _():
        _start_all(in_copies(i + 1, 1 - slot))

    def weight_copies(e):
        return [pltpu.make_async_copy(src.at[layer, e], dst, wsem)
                for src, dst in ((w1_hbm, w1f), (w3_hbm, w3f), (w2_hbm, w2f))]

    @pl.when(i == 0)
    def _():
        _start_all(weight_copies(be_ref[0]))

    @pl.when((i < n_used) & ((i == 0) | (be_ref[i] != be_ref[jnp.maximum(i - 1, 0)])))
    def _():
        _wait_all(weight_copies(be_ref[i]))
        w1b[...] = w1f[...].astype(BF16)
        w3b[...] = w3f[...].astype(BF16)
        w2b[...] = w2f[...].astype(BF16)

        @pl.when(ne_ref[i] != be_ref[i])
        def _():
            _start_all(weight_copies(ne_ref[i]))

    @pl.when(i < n_used)
    def _():
        _wait_all(in_copies(i, slot))
        p = jnp.concatenate([xbuf[slot, :, c, :, :].reshape(MOE_BLOCK, LANES) for c in range(PACKED_TILE_ROWS)],
                            axis=1)
        xa = lax.bitcast_convert_type(p & jnp.uint32(0xFFFF0000), F32).astype(BF16)
        xb = lax.bitcast_convert_type(p << 16, F32).astype(BF16)
        h1 = _bdot(xa, w1b[0:half, :]) + _bdot(xb, w1b[half:, :])
        h3 = _bdot(xa, w3b[0:half, :]) + _bdot(xb, w3b[half:, :])
        write_rows(slot, _bdot((jax.nn.silu(h1) * h3).astype(BF16), w2b[...]))
        _start_all(out_copies(i, slot))

        @pl.when(i >= 1)
        def _():
            _wait_all(out_copies(i - 1, 1 - slot))

        @pl.when(i == n_used - 1)
        def _():
            _wait_all(out_copies(i, slot))

    @pl.when(i >= n_used)
    def _():
        obuf[slot] = jnp.zeros(obuf.shape[1:], F32)
        _start_all(out_copies(i, slot))
        _wait_all(out_copies(i, slot))


def _moe_call(block_e, next_e, n_used, xs, w1, w3, w2, layer):
    n_blocks = block_e.shape[0]
    n_slots = n_blocks * MOE_BLOCK
    d = D_MODEL
    groups = MOE_BLOCK // SUBLANES
    grid_spec = pltpu.PrefetchScalarGridSpec(
        num_scalar_prefetch=3,
        grid=(n_blocks,),
        in_specs=[pl.BlockSpec(memory_space=pl.ANY)] * 4,
        out_specs=pl.BlockSpec(memory_space=pl.ANY),
        scratch_shapes=[pltpu.VMEM((d, D_EXPERT), BF16), pltpu.VMEM((d, D_EXPERT), BF16),
                        pltpu.VMEM((D_EXPERT, d), BF16),
                        pltpu.VMEM((d, D_EXPERT), F32), pltpu.VMEM((d, D_EXPERT), F32),
                        pltpu.VMEM((D_EXPERT, d), F32),
                        pltpu.VMEM((2, groups, PACKED_TILE_ROWS, SUBLANES, LANES), jnp.uint32),
                        pltpu.VMEM((2, groups, ROW_TILE_ROWS, SUBLANES, LANES), F32),
                        pltpu.SemaphoreType.DMA((2,)), pltpu.SemaphoreType.DMA((2,)),
                        pltpu.SemaphoreType.DMA(())],
    )
    xs_view = (n_slots // SUBLANES, SUBLANES // TOKENS_PER_PACKED_TILE, SUBLANES, LANES)
    ys_view = (n_slots // SUBLANES, SUBLANES, ROW_TILE_ROWS, LANES)
    ys = pl.pallas_call(
        functools.partial(_moe_kernel, layer=layer),
        grid_spec=grid_spec,
        out_shape=jax.ShapeDtypeStruct(ys_view, F32),
        compiler_params=_params("arbitrary"),
        name="moe_experts",
    )(block_e, next_e, n_used, xs.reshape(xs_view), w1, w3, w2)
    return ys.reshape(n_slots, ROW_TILE_ROWS, LANES)


COMBINE_TILE = 256


def _combine_kernel(pos0_ref, posn_ref, x1_ref, route_ref, g_ref, lng_ref, lnb_ref, ys_hbm, o_ref, ybuf, sem,
                    *, alpha):
    i = pl.program_id(0)
    nb = pl.num_programs(0)
    n_rows = TOP_K * COMBINE_TILE

    def issue(pos_ref, slot):
        def body(j, carry):
            for s in range(SUBLANES):
                pltpu.make_async_copy(ys_hbm.at[pos_ref[0, 0, SUBLANES * j + s]], ybuf.at[slot, j, :, s, :],
                                      sem.at[slot]).start(priority=s % N_DMA_PRIORITIES)
            return carry
        lax.fori_loop(0, n_rows // SUBLANES, body, 0)

    @pl.when(i == 0)
    def _():
        issue(pos0_ref, 0)

    @pl.when(i + 1 < nb)
    def _():
        issue(posn_ref, (i + 1) % 2)

    slot = i % 2
    pltpu.make_async_copy(ybuf.at[slot], ybuf.at[slot], sem.at[slot]).wait()
    yrows = jnp.concatenate([ybuf[slot, :, c, :, :].reshape(n_rows, LANES) for c in range(ROW_TILE_ROWS)], axis=1)
    route = route_ref[...]
    y = (yrows[0:COMBINE_TILE, :] * route[:, ROUTE_GATE:ROUTE_GATE + 1]
         + yrows[COMBINE_TILE:, :] * route[:, ROUTE_GATE + 1:ROUTE_GATE + 2])
    o_ref[...] = _layer_norm(alpha * x1_ref[...] + g_ref[...] * y) * lng_ref[...] + lnb_ref[...]


def _combine_call(pos, x1, route, gvecs, lng, lnb, ys, n_tok, tiles_per_seq, n_seq, alpha):
    d = D_MODEL
    n_tiles = n_tok // COMBINE_TILE
    n_lat_tiles = tiles_per_seq * n_seq

    def g_idx(i):
        return (jnp.where(i < n_lat_tiles, i // tiles_per_seq, n_seq), 0, 0)

    smem_blk = (1, 1, TOP_K * COMBINE_TILE)
    return pl.pallas_call(
        functools.partial(_combine_kernel, alpha=alpha),
        grid=(n_tiles,),
        in_specs=[
            pl.BlockSpec(smem_blk, lambda i: (0, 0, 0), memory_space=pltpu.SMEM),
            pl.BlockSpec(smem_blk, lambda i: (jnp.minimum(i + 1, n_tiles - 1), 0, 0), memory_space=pltpu.SMEM),
            pl.BlockSpec((COMBINE_TILE, d), lambda i: (i, 0)),
            pl.BlockSpec((COMBINE_TILE, ROUTER_COLS), lambda i: (i, 0)),
            pl.BlockSpec((None, 1, d), g_idx),
            pl.BlockSpec((1, d), lambda i: (0, 0)),
            pl.BlockSpec((1, d), lambda i: (0, 0)),
            pl.BlockSpec(memory_space=pl.ANY),
        ],
        out_specs=pl.BlockSpec((COMBINE_TILE, d), lambda i: (i, 0)),
        out_shape=jax.ShapeDtypeStruct((n_tok, d), F32),
        scratch_shapes=[pltpu.VMEM((2, TOP_K * COMBINE_TILE // SUBLANES, ROW_TILE_ROWS, SUBLANES, LANES), F32),
                        pltpu.SemaphoreType.DMA((2,))],
        compiler_params=_params("arbitrary"),
        name="moe_combine",
    )(pos, pos, x1, route, gvecs, lng, lnb, ys)


def _slots(route_t, tile_counts):
    n_tiles = tile_counts.shape[0]
    n = n_tiles * route_t.shape[2]
    tc = tile_counts.astype(jnp.int32)
    counts = jnp.sum(tc, axis=0)
    padded = (counts + MOE_BLOCK - 1) // MOE_BLOCK * MOE_BLOCK
    pends = jnp.cumsum(padded)
    base = (pends - padded)[None, :] + jnp.cumsum(tc, axis=0) - tc
    n_blocks = -(-n * TOP_K // MOE_BLOCK) + N_EXPERTS
    block_start = jnp.arange(n_blocks, dtype=jnp.int32) * MOE_BLOCK
    block_e = jnp.minimum(jnp.sum(pends[None, 0:N_EXPERTS] <= block_start[:, None], axis=1),
                          N_EXPERTS - 1).astype(jnp.int32)
    ids = jnp.arange(N_EXPERTS, dtype=jnp.int32)
    later = (ids[None, :] > ids[:, None]) & (padded[None, 0:N_EXPERTS] > 0)
    next_nonempty = jnp.min(jnp.where(later, ids[None, :], N_EXPERTS), axis=1)
    next_of = jnp.where(next_nonempty < N_EXPERTS, next_nonempty, ids)
    next_e = jnp.sum(jnp.where(block_e[:, None] == ids[None, :], next_of[None, :], 0), axis=1).astype(jnp.int32)
    n_used = (pends[N_EXPERTS - 1] // MOE_BLOCK).astype(jnp.int32).reshape(1)
    seg = jnp.stack([pends[0:N_EXPERTS], padded[0:N_EXPERTS]]).astype(jnp.int32)
    pos = []
    for k in range(TOP_K):
        e = route_t[:, ROUTE_EXPERT + k, :].astype(jnp.int32)
        hit = e[:, None, :] == jnp.arange(N_EXPERTS, dtype=jnp.int32)[None, :, None]
        b = jnp.sum(jnp.where(hit, base[:, 0:N_EXPERTS, None], 0), axis=1)
        pos.append((b + route_t[:, ROUTE_RANK + k, :].astype(jnp.int32)).reshape(n))
    return block_e, next_e, n_used, seg, pos


def _rope_tables(seq_len):
    m = HEAD_DIM // 4
    freqs = ROPE_BASE ** (-jnp.arange(m, dtype=F32) / m)
    t = jnp.arange(seq_len)
    row = (t // GRID_W).astype(F32)[:, None] * freqs[None, :]
    col = (t % GRID_W).astype(F32)[:, None] * freqs[None, :]
    cos = jnp.concatenate([jnp.cos(row), jnp.cos(row), jnp.cos(col), jnp.cos(col)], axis=-1)
    zero = jnp.zeros_like(row)
    sin_a = jnp.concatenate([zero, jnp.sin(row), zero, jnp.sin(col)], axis=-1)
    sin_b = jnp.concatenate([-jnp.sin(row), zero, -jnp.sin(col), zero], axis=-1)
    rep = LANES // HEAD_DIM
    return jnp.tile(cos, (1, rep)), jnp.tile(sin_a, (1, rep)), jnp.tile(sin_b, (1, rep))


def kernel(x, c, ctx, c_ctx, w_ada, b_ada, w_in, conv_w, attn_sink, gm_ws, gm_bs, w_out, ln1_g, ln1_b,
           w_rg, b_rg, w_re, b_re, w1, w3, w2, ln2_g, ln2_b):
    b_, s_, d_ = x.shape
    c_len = ctx.shape[1]
    depth = w_ada.shape[0]
    alpha = (2 * depth) ** 0.25
    n_lat = b_ * s_
    n_ctx = b_ * c_len
    ts = 1024

    cin = jnp.zeros((ADA_ROWS, d_), F32).at[0:b_].set(c).at[b_].set(c_ctx)
    mod = _ada_call(cin, w_ada, b_ada)

    cos, sin_a, sin_b = _rope_tables(s_)
    ones_c = jnp.ones((c_len, LANES), F32)
    zeros_c = jnp.zeros((c_len, LANES), F32)

    x_flat = x.reshape(n_lat, d_)
    c_flat = ctx.reshape(n_ctx, d_)
    x_off, c_off = 0, 0
    x_arr, c_arr = x_flat, c_flat

    for l in range(depth):
        last = l == depth - 1
        mx = mod[l, 0:b_].reshape(b_, 6, 1, d_)
        sh1, sc1, g1, sh2, sc2, g2 = (mx[:, i] for i in range(6))
        mc = jnp.broadcast_to(mod[l, b_].reshape(1, 6, 1, d_), (b_, 6, 1, d_))
        gm_w = gm_ws[l].astype(BF16)
        gm_b = jnp.repeat(gm_bs[l].T, GM_HEAD, axis=1)
        sink = attn_sink[l]

        qx, kvx, yx = _inproj_call(x_arr, x_off, b_, s_, ts, sc1, sh1, w_in, l, cos, sin_a, sin_b,
                                   conv_w[l], gm_w, gm_b)
        qc, kvc, yc = _inproj_call(c_arr, c_off, b_, c_len, c_len, mc[:, 1], mc[:, 0], w_in, l,
                                   ones_c, zeros_c, zeros_c, conv_w[l], gm_w, gm_b)
        att_x = _win_attn_call(sink, qx, kvx, kvc, b_, s_, c_len)

        w_r = jnp.zeros((d_, ROUTER_COLS), F32).at[:, 0:N_GROUPS].set(w_rg[l]) \
            .at[:, N_GROUPS:N_GROUPS + N_EXPERTS].set(w_re[l])
        w_r_hi = w_r.astype(BF16)
        w_r_lo = (w_r - w_r_hi.astype(F32)).astype(BF16)
        wr = jnp.concatenate([w_r_hi, w_r_lo], axis=1)
        br = jnp.zeros((1, ROUTER_COLS), F32).at[0, 0:N_GROUPS].set(b_rg[l]) \
            .at[0, N_GROUPS:N_GROUPS + N_EXPERTS].set(b_re[l])
        lng1, lnb1 = ln1_g[l].reshape(1, d_), ln1_b[l].reshape(1, d_)
        lng2, lnb2 = ln2_g[l].reshape(1, d_), ln2_b[l].reshape(1, d_)

        n_tot = n_lat if last else n_lat + n_ctx
        gvecs1 = jnp.concatenate([g1, mc[0:1, 2]], axis=0)
        scvecs2 = jnp.concatenate([sc2, mc[0:1, 4]], axis=0)
        shvecs2 = jnp.concatenate([sh2, mc[0:1, 3]], axis=0)
        if last:
            x1, h2t, route, route_t, tcnt = _outproj_call(att_x, att_x, yx, yx, x_arr, x_off, x_arr, x_off, n_lat, 0, s_,
                                                 w_out, l, gvecs1, lng1, lnb1, scvecs2, shvecs2, wr, br, alpha)
        else:
            att_c = _ctx_attn_call(sink, qc, kvc, b_, c_len)
            x1, h2t, route, route_t, tcnt = _outproj_call(att_x, att_c, yx, yc, x_arr, x_off, c_arr, c_off, n_lat, n_ctx,
                                                 s_, w_out, l, gvecs1, lng1, lnb1, scvecs2, shvecs2, wr, br, alpha)

        block_e, next_e, n_used, seg, pos = _slots(route_t, tcnt[:, 0, :])
        n_slots = block_e.shape[0] * MOE_BLOCK

        def per_tile(tile):
            return jnp.concatenate([p.reshape(n_tot // tile, 1, tile) for p in pos], axis=2)

        xs = _dispatch_call(seg, per_tile(DISPATCH_TILE), h2t, n_slots)
        ys = _moe_call(block_e, next_e, n_used, xs, w1, w3, w2, l)

        gvecs = jnp.concatenate([g2, mc[0:1, 5]], axis=0)
        out = _combine_call(per_tile(COMBINE_TILE), x1, route, gvecs, lng2, lnb2, ys, n_tot, s_ // COMBINE_TILE,
                            b_, alpha)
        x_arr, x_off = out, 0
        c_arr, c_off = out, n_lat

    return x_arr.reshape(b_, s_, d_)
```

```python
import functools

import jax
import jax.numpy as jnp
from jax import lax
from jax.experimental import pallas as pl
from jax.experimental.pallas import tpu as pltpu

F32 = jnp.float32
BF16 = jnp.bfloat16

D_MODEL = 1024
GRID_W = 64
HEAD_DIM = 64
N_HEADS = 8
N_KV_HEADS = 2
ATTN_WIDTH = N_HEADS * HEAD_DIM
KV_WIDTH = N_KV_HEADS * HEAD_DIM
WINDOW = 128
ATTN_SCALE = HEAD_DIM ** -0.5
LOG2E = 1.4426950408889634
ROPE_BASE = 10000.0
CONV_WIDTH = D_MODEL // 4
GM_WIDTH = D_MODEL // 4
GM_GROUPS = 4
GM_HEAD = GM_WIDTH // GM_GROUPS
CHUNK = 128
LOCAL_WIDTH = CONV_WIDTH + GM_WIDTH
IN_WIDTH = ATTN_WIDTH + 2 * KV_WIDTH + 3 * CONV_WIDTH + 2 * GM_WIDTH
QKV_WIDTH = ATTN_WIDTH + 2 * KV_WIDTH
KV_PACK_WIDTH = 4 * KV_WIDTH
N_GROUPS = 4
EXP_PER_GROUP = 8
N_EXPERTS = N_GROUPS * EXP_PER_GROUP
TOP_K = 2
D_EXPERT = D_MODEL // 2
MOE_BLOCK = 512
LN_EPS = 1e-6
NEG_INF = -1e30

LANES = 128
SUBLANES = 8
VMEM_LIMIT_BYTES = 48 * 1024 * 1024
N_DMA_PRIORITIES = 2

ROW_TILE_ROWS = D_MODEL // LANES
assert ROW_TILE_ROWS == SUBLANES
PACKED_TILE_ROWS = ROW_TILE_ROWS // 2
TOKENS_PER_PACKED_TILE = SUBLANES // PACKED_TILE_ROWS
assert TOKENS_PER_PACKED_TILE == 2

ROUTER_COLS = LANES
ROUTE_EXPERT = 0
ROUTE_GATE = TOP_K
ROUTE_RANK = 2 * TOP_K
ROPE_HALF_PAIR = HEAD_DIM // 4


def _bdot(a, b):
    return jnp.dot(a, b, preferred_element_type=F32)


def _split_bf16(a):
    hi = a.astype(BF16)
    lo = (a - hi.astype(F32)).astype(BF16)
    return hi, lo


def _layer_norm(r):
    mu = jnp.mean(r, axis=-1, keepdims=True)
    d = r - mu
    var = jnp.mean(d * d, axis=-1, keepdims=True)
    return d * lax.rsqrt(var + LN_EPS)


def _params(*sem, flags=None):
    return pltpu.CompilerParams(dimension_semantics=sem, vmem_limit_bytes=VMEM_LIMIT_BYTES, flags=flags)


def _start_all(copies):
    for cp in copies:
        cp.start()


def _wait_all(copies):
    for cp in copies:
        cp.wait()


ADA_ROWS = 16
ADA_TILE = 1536


def _ada_kernel(c_ref, w_ref, b_ref, o_ref):
    act = jax.nn.silu(c_ref[...])
    ah, al = _split_bf16(act)
    wh, wl = _split_bf16(w_ref[...])
    o_ref[...] = _bdot(ah, wh) + _bdot(ah, wl) + _bdot(al, wh) + b_ref[...]


def _ada_call(cin, w_ada, b_ada):
    depth, d, n = w_ada.shape
    return pl.pallas_call(
        _ada_kernel,
        grid=(depth, n // ADA_TILE),
        in_specs=[
            pl.BlockSpec((ADA_ROWS, d), lambda l, j: (0, 0)),
            pl.BlockSpec((None, d, ADA_TILE), lambda l, j: (l, 0, j)),
            pl.BlockSpec((None, 1, ADA_TILE), lambda l, j: (l, 0, j)),
        ],
        out_specs=pl.BlockSpec((None, ADA_ROWS, ADA_TILE), lambda l, j: (l, 0, j)),
        out_shape=jax.ShapeDtypeStruct((depth, ADA_ROWS, n), F32),
        compiler_params=_params("parallel", "parallel"),
        name="ada_modulation",
    )(cin, w_ada, b_ada.reshape(depth, 1, n))


def _inproj_kernel(x_ref, xp_ref, xn_ref, sc_ref, sh_ref, wf_ref, cos_ref, sa_ref, sb_ref,
                   cw_ref, gw_ref, gb_ref,
                   q_ref, kv_ref, y_ref, w_ref, *, ts):
    t = pl.program_id(1)
    nt = pl.num_programs(1)

    @pl.when((pl.program_id(0) == 0) & (t == 0))
    def _():
        w_ref[...] = wf_ref[...].astype(BF16)

    sc = 1.0 + sc_ref[...]
    sh = sh_ref[...]
    hx = (x_ref[...] * sc + sh).astype(BF16)

    pq = _bdot(hx, w_ref[:, 0:QKV_WIDTH])
    cos = cos_ref[...]
    sa = sa_ref[...]
    sb = sb_ref[...]

    def rope(z):
        return (z * cos + pltpu.roll(z, ROPE_HALF_PAIR, 1) * sa
                + pltpu.roll(z, LANES - ROPE_HALF_PAIR, 1) * sb)

    for j in range(ATTN_WIDTH // LANES):
        sl = slice(j * LANES, (j + 1) * LANES)
        q_ref[:, sl] = (rope(pq[:, sl]) * (ATTN_SCALE * LOG2E)).astype(BF16)
    kr = rope(pq[:, ATTN_WIDTH:ATTN_WIDTH + KV_WIDTH])
    vv = pq[:, ATTN_WIDTH + KV_WIDTH:QKV_WIDTH]
    for j, part in enumerate((kr, pltpu.roll(kr, HEAD_DIM, 1), vv, pltpu.roll(vv, HEAD_DIM, 1))):
        kv_ref[:, j * KV_WIDTH:(j + 1) * KV_WIDTH] = part.astype(BF16)

    pm = _bdot(hx, w_ref[:, QKV_WIDTH:IN_WIDTH])
    cb = pm[:, 0:CONV_WIDTH]
    u = pm[:, CONV_WIDTH:2 * CONV_WIDTH] * pm[:, 2 * CONV_WIDTH:3 * CONV_WIDTH]

    halo = jnp.concatenate([xp_ref[...], xn_ref[...]], axis=0)
    hh = (halo * sc + sh).astype(BF16)
    ph = _bdot(hh, w_ref[:, QKV_WIDTH + CONV_WIDTH:QKV_WIDTH + 3 * CONV_WIDTH])
    uh = ph[:, 0:CONV_WIDTH] * ph[:, CONV_WIDTH:2 * CONV_WIDTH]
    up_row = jnp.where(t > 0, uh[SUBLANES - 1:SUBLANES, :], 0.0)
    dn_row = jnp.where(t < nt - 1, uh[SUBLANES:SUBLANES + 1, :], 0.0)
    ridx = lax.broadcasted_iota(jnp.int32, (ts, CONV_WIDTH), 0)
    u_up = jnp.where(ridx == 0, up_row, pltpu.roll(u, 1, 0))
    u_dn = jnp.where(ridx == ts - 1, dn_row, pltpu.roll(u, ts - 1, 0))
    cw = cw_ref[...]
    y_conv = cb * (u_up * cw[0:1, :] + u * cw[1:2, :] + u_dn * cw[2:3, :])
    y_ref[:, 0:CONV_WIDTH] = y_conv.astype(BF16)

    gu = jax.nn.gelu(pm[:, 3 * CONV_WIDTH:3 * CONV_WIDTH + GM_WIDTH])
    gv = _layer_norm(jax.nn.gelu(pm[:, 3 * CONV_WIDTH + GM_WIDTH:3 * CONV_WIDTH + 2 * GM_WIDTH])).astype(BF16)
    lane = lax.broadcasted_iota(jnp.int32, (CHUNK, LANES), 1)
    zero = jnp.zeros((CHUNK, LANES), BF16)
    for c in range(ts // CHUNK):
        rows = slice(c * CHUNK, (c + 1) * CHUNK)
        for j in range(GM_WIDTH // LANES):
            cols = slice(j * LANES, (j + 1) * LANES)
            vp = gv[rows, cols]
            s = (_bdot(gw_ref[2 * j], jnp.where(lane < GM_HEAD, vp, zero))
                 + _bdot(gw_ref[2 * j + 1], jnp.where(lane >= GM_HEAD, vp, zero))
                 + gb_ref[:, cols])
            y_ref[rows, CONV_WIDTH + j * LANES:CONV_WIDTH + (j + 1) * LANES] = (gu[rows, cols] * s).astype(BF16)


def _inproj_call(x2d, row_off, n_seq, seq_len, ts, sc, sh, w_in, layer, cos, sa, sb, conv_w, gm_w, gm_b):
    nt = seq_len // ts
    off_t = row_off // ts
    off_8 = row_off // SUBLANES
    last_8 = x2d.shape[0] // SUBLANES - 1
    per_tile_8 = ts // SUBLANES
    n_out = n_seq * seq_len
    d = D_MODEL

    def tile_idx(b, t):
        return (off_t + b * nt + t, 0)

    def prev_idx(b, t):
        return (jnp.maximum(off_8 + (b * nt + t) * per_tile_8 - 1, 0), 0)

    def next_idx(b, t):
        return (jnp.minimum(off_8 + (b * nt + t + 1) * per_tile_8, last_8), 0)

    def out_idx(b, t):
        return (b * nt + t, 0)

    vec = pl.BlockSpec((None, 1, d), lambda b, t: (b, 0, 0))
    rope_spec = pl.BlockSpec((ts, LANES), lambda b, t: (t, 0))
    return pl.pallas_call(
        functools.partial(_inproj_kernel, ts=ts),
        grid=(n_seq, nt),
        in_specs=[
            pl.BlockSpec((ts, d), tile_idx),
            pl.BlockSpec((SUBLANES, d), prev_idx),
            pl.BlockSpec((SUBLANES, d), next_idx),
            vec, vec,
            pl.BlockSpec((None, d, IN_WIDTH), lambda b, t: (layer, 0, 0)),
            rope_spec, rope_spec, rope_spec,
            pl.BlockSpec((3, CONV_WIDTH), lambda b, t: (0, 0)),
            pl.BlockSpec((GM_GROUPS, CHUNK, CHUNK), lambda b, t: (0, 0, 0)),
            pl.BlockSpec((CHUNK, GM_WIDTH), lambda b, t: (0, 0)),
        ],
        out_specs=[
            pl.BlockSpec((ts, ATTN_WIDTH), out_idx),
            pl.BlockSpec((ts, KV_PACK_WIDTH), out_idx),
            pl.BlockSpec((ts, LOCAL_WIDTH), out_idx),
        ],
        out_shape=[
            jax.ShapeDtypeStruct((n_out, ATTN_WIDTH), BF16),
            jax.ShapeDtypeStruct((n_out, KV_PACK_WIDTH), BF16),
            jax.ShapeDtypeStruct((n_out, LOCAL_WIDTH), BF16),
        ],
        scratch_shapes=[pltpu.VMEM((d, IN_WIDTH), BF16)],
        compiler_params=_params("arbitrary", "arbitrary"),
        name="in_projection",
    )(x2d, x2d, x2d, sc, sh, w_in, cos, sa, sb, conv_w, gm_w, gm_b)


def _attn_body(q, keys, keys_sw, vals, vals_sw, halo_bias, sink_ref, o_ref, tq):
    nk = keys.shape[0]
    lane = lax.broadcasted_iota(jnp.int32, (nk, LANES), 1)
    lo = lane < HEAD_DIM
    zero = jnp.zeros((nk, LANES), BF16)
    rid = lax.broadcasted_iota(jnp.int32, (2 * tq, 1), 0)
    out_lo = lax.broadcasted_iota(jnp.int32, (2 * tq, LANES), 1) < HEAD_DIM
    nt_dims = (((1,), (1,)), ((), ()))
    gqa = N_HEADS // N_KV_HEADS
    biases = [None] * (nk // LANES)
    if halo_bias is not None:
        assert tq == LANES
        biases[0], biases[2] = halo_bias
    for h in range(N_KV_HEADS):
        k_own, k_oth = (keys, keys_sw) if h == 0 else (keys_sw, keys)
        v_own, v_oth = (vals, vals_sw) if h == 0 else (vals_sw, vals)
        kz = jnp.concatenate([jnp.where(lo, k_own, zero), jnp.where(lo, zero, k_oth)], axis=0)
        vz = jnp.concatenate([jnp.where(lo, v_own, zero), jnp.where(lo, zero, v_oth)], axis=0)
        c0 = h * gqa * HEAD_DIM
        qs = jnp.concatenate([q[:, c0:c0 + LANES], q[:, c0 + LANES:c0 + 2 * LANES]], axis=0)
        s_all = lax.dot_general(qs, kz, nt_dims, preferred_element_type=F32)
        probs, inv_den = [], []
        for par in range(2):
            tiles = []
            for j, bias in enumerate(biases):
                tile = s_all[:, par * nk + j * LANES:par * nk + (j + 1) * LANES]
                tiles.append(tile if bias is None else tile + bias)
            sink = jnp.where(rid < tq, sink_ref[h * gqa + par], sink_ref[h * gqa + 2 + par]) * LOG2E
            tile_max = tiles[0]
            for tile in tiles[1:]:
                tile_max = jnp.maximum(tile_max, tile)
            m = jnp.maximum(jnp.max(tile_max, axis=-1, keepdims=True), sink)
            tile_sum = None
            for tile in tiles:
                p = jnp.exp2(tile - m)
                tile_sum = p if tile_sum is None else tile_sum + p
                probs.append(p.astype(BF16))
            inv_den.append(1.0 / (jnp.sum(tile_sum, axis=-1, keepdims=True) + jnp.exp2(sink - m)))
        o = _bdot(jnp.concatenate(probs, axis=1), vz) * jnp.where(out_lo, inv_den[0], inv_den[1])
        o_ref[:, c0:c0 + LANES] = o[0:tq].astype(BF16)
        o_ref[:, c0 + LANES:c0 + 2 * LANES] = o[tq:2 * tq].astype(BF16)


def _split_kv(kv):
    return tuple(kv[:, j * KV_WIDTH:(j + 1) * KV_WIDTH] for j in range(4))


def _win_attn_kernel(sink_ref, q_ref, kvp_ref, kvc_ref, kvn_ref, kvx_ref, o_ref, *, tq):
    n = pl.program_id(1)
    nb = pl.num_programs(1)
    row = lax.broadcasted_iota(jnp.int32, (tq, tq), 0)
    col = lax.broadcasted_iota(jnp.int32, (tq, tq), 1)
    b_prev = jnp.where((col >= row) & (n > 0), 0.0, NEG_INF)
    b_next = jnp.where((col <= row) & (n < nb - 1), 0.0, NEG_INF)
    halo_bias = (jnp.concatenate([b_prev, b_prev], axis=0), jnp.concatenate([b_next, b_next], axis=0))

    kv = jnp.concatenate([kvp_ref[...], kvc_ref[...], kvn_ref[...], kvx_ref[...]], axis=0)
    _attn_body(q_ref[...], *_split_kv(kv), halo_bias, sink_ref, o_ref, tq)


def _win_attn_call(sink, q, kv, kv_ctx, n_seq, seq_len, ctx_len):
    tq = WINDOW
    nb = seq_len // tq

    def cur(b, n):
        return (b * nb + n, 0)

    def prev(b, n):
        return (b * nb + jnp.maximum(n - 1, 0), 0)

    def nxt(b, n):
        return (b * nb + jnp.minimum(n + 1, nb - 1), 0)

    return pl.pallas_call(
        functools.partial(_win_attn_kernel, tq=tq),
        grid=(n_seq, nb),
        in_specs=[pl.BlockSpec(memory_space=pltpu.SMEM), pl.BlockSpec((tq, ATTN_WIDTH), cur),
                  pl.BlockSpec((tq, KV_PACK_WIDTH), prev), pl.BlockSpec((tq, KV_PACK_WIDTH), cur),
                  pl.BlockSpec((tq, KV_PACK_WIDTH), nxt),
                  pl.BlockSpec((ctx_len, KV_PACK_WIDTH), lambda b, n: (b, 0))],
        out_specs=pl.BlockSpec((tq, ATTN_WIDTH), cur),
        out_shape=jax.ShapeDtypeStruct((n_seq * seq_len, ATTN_WIDTH), BF16),
        compiler_params=_params("parallel", "parallel"),
        name="window_attention",
    )(sink, q, kv, kv, kv, kv_ctx)


def _ctx_attn_kernel(sink_ref, q_ref, kv_ref, o_ref, *, tq):
    _attn_body(q_ref[...], *_split_kv(kv_ref[...]), None, sink_ref, o_ref, tq)


def _ctx_attn_call(sink, q, kv, n_seq, ctx_len):
    def blk(w):
        return pl.BlockSpec((ctx_len, w), lambda b: (b, 0))

    return pl.pallas_call(
        functools.partial(_ctx_attn_kernel, tq=ctx_len),
        grid=(n_seq,),
        in_specs=[pl.BlockSpec(memory_space=pltpu.SMEM), blk(ATTN_WIDTH), blk(KV_PACK_WIDTH)],
        out_specs=blk(ATTN_WIDTH),
        out_shape=jax.ShapeDtypeStruct((n_seq * ctx_len, ATTN_WIDTH), BF16),
        compiler_params=_params("parallel"),
        name="context_attention",
    )(sink, q, kv)


OUT_TILE = 512


def _outproj_kernel(attx_ref, attc_ref, yx_ref, yc_ref, xx_ref, xc_ref, wof_ref, g_ref, lng_ref, lnb_ref,
                    sc_ref, sh_ref, wr_ref, br_ref, before_ref,
                    x1_ref, h2v_ref, route_ref, routet_ref, cnt_ref, wo_ref, o_buf, lg_buf,
                    *, alpha, n_lat_tiles, n_tiles, has_ctx):
    step = pl.program_id(0)

    @pl.when(step == 0)
    def _():
        wo_ref[...] = wof_ref[...].astype(BF16)
        o_buf[1] = jnp.zeros(o_buf.shape[1:], F32)
        lg_buf[1] = jnp.zeros(lg_buf.shape[1:], F32)

    o = o_buf[(step + 1) % 2]
    lg = lg_buf[(step + 1) % 2]

    if has_ctx:
        att = jnp.where(jnp.minimum(step, n_tiles - 1) < n_lat_tiles, attx_ref[...], attc_ref[...])
        yloc = jnp.where(jnp.minimum(step, n_tiles - 1) < n_lat_tiles, yx_ref[...], yc_ref[...])
        xin = jnp.where(jnp.clip(step - 1, 0, n_tiles - 1) < n_lat_tiles, xx_ref[...], xc_ref[...])
    else:
        att, yloc, xin = attx_ref[...], yx_ref[...], xx_ref[...]
    o_buf[step % 2] = _bdot(att, wo_ref[0:ATTN_WIDTH, :]) + _bdot(yloc, wo_ref[ATTN_WIDTH:, :])
    x1 = _layer_norm(alpha * xin + g_ref[...] * o) * lng_ref[...] + lnb_ref[...]
    x1_ref[...] = x1
    h2 = x1 * (1.0 + sc_ref[...]) + sh_ref[...]
    half = D_MODEL // 2
    hi_bits = lax.bitcast_convert_type(h2[:, 0:half].astype(BF16).astype(F32), jnp.uint32)
    lo_bits = lax.bitcast_convert_type(h2[:, half:].astype(BF16).astype(F32), jnp.uint32)
    packed = hi_bits | (lo_bits >> 16)
    for c in range(PACKED_TILE_ROWS):
        h2v_ref[:, c, :, :] = packed[:, c * LANES:(c + 1) * LANES].reshape(h2.shape[0] // SUBLANES, SUBLANES, LANES)

    hh, hl = _split_bf16(h2)
    t1 = _bdot(hh, wr_ref[...])
    lg_buf[step % 2] = (t1[:, 0:ROUTER_COLS] + t1[:, ROUTER_COLS:] + _bdot(hl, wr_ref[:, 0:ROUTER_COLS])
                        + br_ref[...])

    ts = lg.shape[0]
    lane = lax.broadcasted_iota(jnp.int32, (ts, ROUTER_COLS), 1).astype(F32)
    big = jnp.float32(ROUTER_COLS)

    def top1(v):
        m = jnp.max(v, axis=-1, keepdims=True)
        return m, jnp.min(jnp.where(v == m, lane, big), axis=-1, keepdims=True)

    gl = jnp.where(lane < N_GROUPS, lg, NEG_INF)
    g_val, g_idx = top1(gl)
    lse = g_val + jnp.log(jnp.sum(jnp.exp(gl - g_val), axis=-1, keepdims=True))
    p_group = jnp.exp(g_val - lse)
    e_lo = N_GROUPS + EXP_PER_GROUP * g_idx
    el = jnp.where((lane >= e_lo) & (lane < e_lo + EXP_PER_GROUP), lg, NEG_INF)
    e1, l1 = top1(el)
    e2, l2 = top1(jnp.where(lane == l1, NEG_INF, el))
    z = jnp.exp(e2 - e1)
    gate1 = p_group / (1.0 + z)
    gate2 = p_group * z / (1.0 + z)
    x1id = l1 - N_GROUPS
    x2id = l2 - N_GROUPS

    sel1 = lane == x1id
    sel2 = lane == x2id
    onehot = jnp.where(sel1 | sel2, 1.0, 0.0)
    prefix = _bdot(before_ref[...], onehot.astype(BF16))
    rank1 = jnp.sum(jnp.where(sel1, prefix, 0.0), axis=-1, keepdims=True)
    rank2 = jnp.sum(jnp.where(sel2, prefix, 0.0), axis=-1, keepdims=True)
    route = jnp.zeros((ts, ROUTER_COLS), F32)
    for col, val in enumerate((x1id, x2id, gate1, gate2, rank1, rank2)):
        route = jnp.where(lane == col, val, route)
    route_ref[...] = route
    routet_ref[...] = route.T[0:SUBLANES, :]
    cnt_ref[...] =jnp.broadcast_to(jnp.sum(onehot, axis=0, keepdims=True), (SUBLANES, ROUTER_COLS))


def _outproj_call(att_x, att_c, y_x, y_c, x_arr, x_off, c_arr, c_off, n_lat, n_ctx, seq_len,
                  w_out, layer, gvecs, lng, lnb, scvecs, shvecs, wr, br, alpha):
    ts = OUT_TILE
    d = D_MODEL
    n_lat_tiles = n_lat // ts
    n_tiles = (n_lat + n_ctx) // ts
    tiles_per_seq = seq_len // ts
    n_seq = n_lat // seq_len
    xo, co = x_off // ts, c_off // ts

    def mm_tile(s):
        return jnp.minimum(s, n_tiles - 1)

    def ln_tile(s):
        return jnp.clip(s - 1, 0, n_tiles - 1)

    def route_tile(s):
        return jnp.clip(s - 2, 0, n_tiles - 1)

    def lat_loc(s):
        return (jnp.minimum(mm_tile(s), n_lat_tiles - 1), 0)

    def ctx_loc(s):
        return (jnp.maximum(mm_tile(s) - n_lat_tiles, 0), 0)

    def lat_in(s):
        return (xo + jnp.minimum(ln_tile(s), n_lat_tiles - 1), 0)

    def ctx_in(s):
        return (co + jnp.maximum(ln_tile(s) - n_lat_tiles, 0), 0)

    def vec_idx(s):
        i = ln_tile(s)
        return (jnp.where(i < n_lat_tiles, i // tiles_per_seq, n_seq), 0, 0)

    vecb = pl.BlockSpec((None, 1, d), vec_idx)
    vec0 = pl.BlockSpec((1, d), lambda i: (0, 0))
    return pl.pallas_call(
        functools.partial(_outproj_kernel, alpha=alpha, n_lat_tiles=n_lat_tiles, n_tiles=n_tiles,
                          has_ctx=n_ctx > 0),
        grid=(n_tiles + 2,),
        in_specs=[
            pl.BlockSpec((ts, ATTN_WIDTH), lat_loc), pl.BlockSpec((ts, ATTN_WIDTH), ctx_loc),
            pl.BlockSpec((ts, LOCAL_WIDTH), lat_loc), pl.BlockSpec((ts, LOCAL_WIDTH), ctx_loc),
            pl.BlockSpec((ts, d), lat_in), pl.BlockSpec((ts, d), ctx_in),
            pl.BlockSpec((None, d, d), lambda i: (layer, 0, 0)),
            vecb, vec0, vec0, vecb, vecb,
            pl.BlockSpec((d, 2 * ROUTER_COLS), lambda i: (0, 0)),
            pl.BlockSpec((1, ROUTER_COLS), lambda i: (0, 0)),
            pl.BlockSpec((ts, ts), lambda i: (0, 0)),
        ],
        out_specs=[pl.BlockSpec((ts, d), lambda s: (ln_tile(s), 0)),
                   pl.BlockSpec((ts // SUBLANES, PACKED_TILE_ROWS, SUBLANES, LANES),
                                lambda s: (ln_tile(s), 0, 0, 0)),
                   pl.BlockSpec((ts, ROUTER_COLS), lambda s: (route_tile(s), 0)),
                   pl.BlockSpec((None, SUBLANES, ts), lambda s: (route_tile(s), 0, 0)),
                   pl.BlockSpec((None, SUBLANES, ROUTER_COLS), lambda s: (route_tile(s), 0, 0))],
        out_shape=[jax.ShapeDtypeStruct((n_lat + n_ctx, d), F32),
                   jax.ShapeDtypeStruct(((n_lat + n_ctx) // SUBLANES, PACKED_TILE_ROWS, SUBLANES, LANES),
                                        jnp.uint32),
                   jax.ShapeDtypeStruct((n_lat + n_ctx, ROUTER_COLS), F32),
                   jax.ShapeDtypeStruct((n_tiles, SUBLANES, ts), F32),
                   jax.ShapeDtypeStruct((n_tiles, SUBLANES, ROUTER_COLS), F32)],
        scratch_shapes=[pltpu.VMEM((d, d), BF16), pltpu.VMEM((2, ts, d), F32),
                        pltpu.VMEM((2, ts, ROUTER_COLS), F32)],
        compiler_params=_params("arbitrary"),
        name="out_projection",
    )(att_x, att_c, y_x, y_c, x_arr, c_arr, w_out, gvecs, lng, lnb, scvecs, shvecs, wr, br,
      jnp.tril(jnp.ones((ts, ts), BF16), -1))


DISPATCH_TILE = 512


def _dispatch_kernel(seg_ref, pos_ref, h_ref, xs_out, zbuf, zsem, sem):
    block_tiles = MOE_BLOCK // TOKENS_PER_PACKED_TILE

    @pl.when(pl.program_id(0) == 0)
    def _():
        zbuf[...] = jnp.zeros_like(zbuf)

        def zero_copy(e):
            first_tile = (seg_ref[0, e] - MOE_BLOCK) // TOKENS_PER_PACKED_TILE
            return pltpu.make_async_copy(zbuf, xs_out.at[pl.ds(first_tile, block_tiles)], zsem)

        for e in range(N_EXPERTS):
            @pl.when(seg_ref[1, e] > 0)
            def _():
                zero_copy(e).start()
        for e in range(N_EXPERTS):
            @pl.when(seg_ref[1, e] > 0)
            def _():
                zero_copy(e).wait()

        def tail_copy(b):
            return pltpu.make_async_copy(zbuf, xs_out.at[pl.ds(b * block_tiles, block_tiles)], zsem)

        n_blocks = xs_out.shape[0] // block_tiles
        first_unused = seg_ref[0, N_EXPERTS - 1] // MOE_BLOCK
        lax.fori_loop(first_unused, n_blocks, lambda b, c: (tail_copy(b).start(), c)[1], 0)
        lax.fori_loop(first_unused, n_blocks, lambda b, c: (tail_copy(b).wait(), c)[1], 0)

    def body(j, carry):
        for s in range(SUBLANES):
            t = SUBLANES * j + s
            for k in range(TOP_K):
                p = pos_ref[0, 0, k * DISPATCH_TILE + t]
                dst = xs_out.at[p >> 1, pl.ds(PACKED_TILE_ROWS * (p & 1), PACKED_TILE_ROWS), :]
                pltpu.make_async_copy(h_ref.at[j, :, s, :], dst, sem).start(priority=k)
        return carry

    lax.fori_loop(0, DISPATCH_TILE // SUBLANES, body, 0)
    for k in range(TOP_K):
        pltpu.make_async_copy(h_ref, h_ref, sem).wait()


def _dispatch_call(seg, pos, h2t, n_slots):
    n_tok = h2t.shape[0] * SUBLANES
    n_tiles = n_tok // DISPATCH_TILE
    grid_spec = pltpu.PrefetchScalarGridSpec(
        num_scalar_prefetch=1,
        grid=(n_tiles,),
        in_specs=[
            pl.BlockSpec((1, 1, TOP_K * DISPATCH_TILE), lambda i, seg: (i, 0, 0), memory_space=pltpu.SMEM),
            pl.BlockSpec((DISPATCH_TILE // SUBLANES, PACKED_TILE_ROWS, SUBLANES, LANES),
                         lambda i, seg: (i, 0, 0, 0)),
        ],
        out_specs=pl.BlockSpec(memory_space=pl.ANY),
        scratch_shapes=[pltpu.VMEM((MOE_BLOCK // TOKENS_PER_PACKED_TILE, SUBLANES, LANES), jnp.uint32),
                        pltpu.SemaphoreType.DMA(()), pltpu.SemaphoreType.DMA(())],
    )
    return pl.pallas_call(
        _dispatch_kernel,
        grid_spec=grid_spec,
        out_shape=jax.ShapeDtypeStruct((n_slots // TOKENS_PER_PACKED_TILE, SUBLANES, LANES), jnp.uint32),
        compiler_params=_params("arbitrary"),
        name="moe_dispatch",
    )(seg, pos, h2t)


def _moe_kernel(be_ref, ne_ref, nu_ref, xs_hbm, w1_hbm, w3_hbm, w2_hbm, ys_hbm, w1b, w3b, w2b, w1f, w3f, w2f,
                xbuf, obuf, isem, osem, wsem, *, layer):
    i = pl.program_id(0)
    n_used = nu_ref[0]
    slot = i % 2
    groups = MOE_BLOCK // SUBLANES
    half = D_MODEL // 2

    def in_copies(blk, at_slot):
        return [pltpu.make_async_copy(
            xs_hbm.at[pl.ds(blk * groups, groups), s // TOKENS_PER_PACKED_TILE,
                      pl.ds(PACKED_TILE_ROWS * (s % TOKENS_PER_PACKED_TILE), PACKED_TILE_ROWS), :],
            xbuf.at[at_slot, :, :, s, :], isem.at[at_slot]) for s in range(SUBLANES)]

    def out_copies(blk, at_slot):
        return [pltpu.make_async_copy(obuf.at[at_slot, :, :, s, :], ys_hbm.at[pl.ds(blk * groups, groups), s],
                                      osem.at[at_slot]) for s in range(SUBLANES)]

    def write_rows(at_slot, y):
        for c in range(ROW_TILE_ROWS):
            obuf[at_slot, :, c, :, :] = y[:, c * LANES:(c + 1) * LANES].reshape(groups, SUBLANES, LANES)

    @pl.when(i == 0)
    def _():
        _start_all(in_copies(0, 0))

    @pl.when(i + 1 < n_used)
    def _():
        _start_all(in_copies(i + 1, 1 - slot))

    def weight_copies(e):
        return [pltpu.make_async_copy(src.at[layer, e], dst, wsem)
                for src, dst in ((w1_hbm, w1f), (w3_hbm, w3f), (w2_hbm, w2f))]

    @pl.when(i == 0)
    def _():
        _start_all(weight_copies(be_ref[0]))

    @pl.when((i < n_used) & ((i == 0) | (be_ref[i] != be_ref[jnp.maximum(i - 1, 0)])))
    def _():
        _wait_all(weight_copies(be_ref[i]))
        w1b[...] = w1f[...].astype(BF16)
        w3b[...] = w3f[...].astype(BF16)
        w2b[...] = w2f[...].astype(BF16)

        @pl.when(ne_ref[i] != be_ref[i])
        def _():
            _start_all(weight_copies(ne_ref[i]))

    @pl.when(i < n_used)
    def _():
        _wait_all(in_copies(i, slot))
        p = jnp.concatenate([xbuf[slot, :, c, :, :].reshape(MOE_BLOCK, LANES) for c in range(PACKED_TILE_ROWS)],
                            axis=1)
        xa = lax.bitcast_convert_type(p & jnp.uint32(0xFFFF0000), F32).astype(BF16)
        xb = lax.bitcast_convert_type(p << 16, F32).astype(BF16)
        h1 = _bdot(xa, w1b[0:half, :]) + _bdot(xb, w1b[half:, :])
        h3 = _bdot(xa, w3b[0:half, :]) + _bdot(xb, w3b[half:, :])
        write_rows(slot, _bdot((jax.nn.silu(h1) * h3).astype(BF16), w2b[...]))
        _start_all(out_copies(i, slot))

        @pl.when(i >= 1)
        def _():
            _wait_all(out_copies(i - 1, 1 - slot))

        @pl.when(i == n_used - 1)
        def _():
            _wait_all(out_copies(i, slot))

    @pl.when(i >= n_used)
    def _():
        obuf[slot] = jnp.zeros(obuf.shape[1:], F32)
        _start_all(out_copies(i, slot))
        _wait_all(out_copies(i, slot))


def _moe_call(block_e, next_e, n_used, xs, w1, w3, w2, layer):
    n_blocks = block_e.shape[0]
    n_slots = n_blocks * MOE_BLOCK
    d = D_MODEL
    groups = MOE_BLOCK // SUBLANES
    grid_spec = pltpu.PrefetchScalarGridSpec(
        num_scalar_prefetch=3,
        grid=(n_blocks,),
        in_specs=[pl.BlockSpec(memory_space=pl.ANY)] * 4,
        out_specs=pl.BlockSpec(memory_space=pl.ANY),
        scratch_shapes=[pltpu.VMEM((d, D_EXPERT), BF16), pltpu.VMEM((d, D_EXPERT), BF16),
                        pltpu.VMEM((D_EXPERT, d), BF16),
                        pltpu.VMEM((d, D_EXPERT), F32), pltpu.VMEM((d, D_EXPERT), F32),
                        pltpu.VMEM((D_EXPERT, d), F32),
                        pltpu.VMEM((2, groups, PACKED_TILE_ROWS, SUBLANES, LANES), jnp.uint32),
                        pltpu.VMEM((2, groups, ROW_TILE_ROWS, SUBLANES, LANES), F32),
                        pltpu.SemaphoreType.DMA((2,)), pltpu.SemaphoreType.DMA((2,)),
                        pltpu.SemaphoreType.DMA(())],
    )
    xs_view = (n_slots // SUBLANES, SUBLANES // TOKENS_PER_PACKED_TILE, SUBLANES, LANES)
    ys_view = (n_slots // SUBLANES, SUBLANES, ROW_TILE_ROWS, LANES)
    ys = pl.pallas_call(
        functools.partial(_moe_kernel, layer=layer),
        grid_spec=grid_spec,
        out_shape=jax.ShapeDtypeStruct(ys_view, F32),
        compiler_params=_params("arbitrary"),
        name="moe_experts",
    )(block_e, next_e, n_used, xs.reshape(xs_view), w1, w3, w2)
    return ys.reshape(n_slots, ROW_TILE_ROWS, LANES)


COMBINE_TILE = 256


def _combine_kernel(pos0_ref, posn_ref, x1_ref, route_ref, g_ref, lng_ref, lnb_ref, ys_hbm, o_ref, ybuf, sem,
                    *, alpha):
    i = pl.program_id(0)
    nb = pl.num_programs(0)
    n_rows = TOP_K * COMBINE_TILE

    def issue(pos_ref, slot):
        def body(j, carry):
            for s in range(SUBLANES):
                pltpu.make_async_copy(ys_hbm.at[pos_ref[0, 0, SUBLANES * j + s]], ybuf.at[slot, j, :, s, :],
                                      sem.at[slot]).start(priority=s % N_DMA_PRIORITIES)
            return carry
        lax.fori_loop(0, n_rows // SUBLANES, body, 0)

    @pl.when(i == 0)
    def _():
        issue(pos0_ref, 0)

    @pl.when(i + 1 < nb)
    def _():
        issue(posn_ref, (i + 1) % 2)

    slot = i % 2
    pltpu.make_async_copy(ybuf.at[slot], ybuf.at[slot], sem.at[slot]).wait()
    yrows = jnp.concatenate([ybuf[slot, :, c, :, :].reshape(n_rows, LANES) for c in range(ROW_TILE_ROWS)], axis=1)
    route = route_ref[...]
    y = (yrows[0:COMBINE_TILE, :] * route[:, ROUTE_GATE:ROUTE_GATE + 1]
         + yrows[COMBINE_TILE:, :] * route[:, ROUTE_GATE + 1:ROUTE_GATE + 2])
    o_ref[...] = _layer_norm(alpha * x1_ref[...] + g_ref[...] * y) * lng_ref[...] + lnb_ref[...]


def _combine_call(pos, x1, route, gvecs, lng, lnb, ys, n_tok, tiles_per_seq, n_seq, alpha):
    d = D_MODEL
    n_tiles = n_tok // COMBINE_TILE
    n_lat_tiles = tiles_per_seq * n_seq

    def g_idx(i):
        return (jnp.where(i < n_lat_tiles, i // tiles_per_seq, n_seq), 0, 0)

    smem_blk = (1, 1, TOP_K * COMBINE_TILE)
    return pl.pallas_call(
        functools.partial(_combine_kernel, alpha=alpha),
        grid=(n_tiles,),
        in_specs=[
            pl.BlockSpec(smem_blk, lambda i: (0, 0, 0), memory_space=pltpu.SMEM),
            pl.BlockSpec(smem_blk, lambda i: (jnp.minimum(i + 1, n_tiles - 1), 0, 0), memory_space=pltpu.SMEM),
            pl.BlockSpec((COMBINE_TILE, d), lambda i: (i, 0)),
            pl.BlockSpec((COMBINE_TILE, ROUTER_COLS), lambda i: (i, 0)),
            pl.BlockSpec((None, 1, d), g_idx),
            pl.BlockSpec((1, d), lambda i: (0, 0)),
            pl.BlockSpec((1, d), lambda i: (0, 0)),
            pl.BlockSpec(memory_space=pl.ANY),
        ],
        out_specs=pl.BlockSpec((COMBINE_TILE, d), lambda i: (i, 0)),
        out_shape=jax.ShapeDtypeStruct((n_tok, d), F32),
        scratch_shapes=[pltpu.VMEM((2, TOP_K * COMBINE_TILE // SUBLANES, ROW_TILE_ROWS, SUBLANES, LANES), F32),
                        pltpu.SemaphoreType.DMA((2,))],
        compiler_params=_params("arbitrary"),
        name="moe_combine",
    )(pos, pos, x1, route, gvecs, lng, lnb, ys)


def _slots(route_t, tile_counts):
    n_tiles = tile_counts.shape[0]
    n = n_tiles * route_t.shape[2]
    tc = tile_counts.astype(jnp.int32)
    counts = jnp.sum(tc, axis=0)
    padded = (counts + MOE_BLOCK - 1) // MOE_BLOCK * MOE_BLOCK
    pends = jnp.cumsum(padded)
    base = (pends - padded)[None, :] + jnp.cumsum(tc, axis=0) - tc
    n_blocks = -(-n * TOP_K // MOE_BLOCK) + N_EXPERTS
    block_start = jnp.arange(n_blocks, dtype=jnp.int32) * MOE_BLOCK
    block_e = jnp.minimum(jnp.sum(pends[None, 0:N_EXPERTS] <= block_start[:, None], axis=1),
                          N_EXPERTS - 1).astype(jnp.int32)
    ids = jnp.arange(N_EXPERTS, dtype=jnp.int32)
    later = (ids[None, :] > ids[:, None]) & (padded[None, 0:N_EXPERTS] > 0)
    next_nonempty = jnp.min(jnp.where(later, ids[None, :], N_EXPERTS), axis=1)
    next_of = jnp.where(next_nonempty < N_EXPERTS, next_nonempty, ids)
    next_e = jnp.sum(jnp.where(block_e[:, None] == ids[None, :], next_of[None, :], 0), axis=1).astype(jnp.int32)
    n_used = (pends[N_EXPERTS - 1] // MOE_BLOCK).astype(jnp.int32).reshape(1)
    seg = jnp.stack([pends[0:N_EXPERTS], padded[0:N_EXPERTS]]).astype(jnp.int32)
    pos = []
    for k in range(TOP_K):
        e = route_t[:, ROUTE_EXPERT + k, :].astype(jnp.int32)
        hit = e[:, None, :] == jnp.arange(N_EXPERTS, dtype=jnp.int32)[None, :, None]
        b = jnp.sum(jnp.where(hit, base[:, 0:N_EXPERTS, None], 0), axis=1)
        pos.append((b + route_t[:, ROUTE_RANK + k, :].astype(jnp.int32)).reshape(n))
    return block_e, next_e, n_used, seg, pos


def _rope_tables(seq_len):
    m = HEAD_DIM // 4
    freqs = ROPE_BASE ** (-jnp.arange(m, dtype=F32) / m)
    t = jnp.arange(seq_len)
    row = (t // GRID_W).astype(F32)[:, None] * freqs[None, :]
    col = (t % GRID_W).astype(F32)[:, None] * freqs[None, :]
    cos = jnp.concatenate([jnp.cos(row), jnp.cos(row), jnp.cos(col), jnp.cos(col)], axis=-1)
    zero = jnp.zeros_like(row)
    sin_a = jnp.concatenate([zero, jnp.sin(row), zero, jnp.sin(col)], axis=-1)
    sin_b = jnp.concatenate([-jnp.sin(row), zero, -jnp.sin(col), zero], axis=-1)
    rep = LANES // HEAD_DIM
    return jnp.tile(cos, (1, rep)), jnp.tile(sin_a, (1, rep)), jnp.tile(sin_b, (1, rep))


def kernel(x, c, ctx, c_ctx, w_ada, b_ada, w_in, conv_w, attn_sink, gm_ws, gm_bs, w_out, ln1_g, ln1_b,
           w_rg, b_rg, w_re, b_re, w1, w3, w2, ln2_g, ln2_b):
    b_, s_, d_ = x.shape
    c_len = ctx.shape[1]
    depth = w_ada.shape[0]
    alpha = (2 * depth) ** 0.25
    n_lat = b_ * s_
    n_ctx = b_ * c_len
    ts = 1024

    cin = jnp.zeros((ADA_ROWS, d_), F32).at[0:b_].set(c).at[b_].set(c_ctx)
    mod = _ada_call(cin, w_ada, b_ada)

    cos, sin_a, sin_b = _rope_tables(s_)
    ones_c = jnp.ones((c_len, LANES), F32)
    zeros_c = jnp.zeros((c_len, LANES), F32)

    x_flat = x.reshape(n_lat, d_)
    c_flat = ctx.reshape(n_ctx, d_)
    x_off, c_off = 0, 0
    x_arr, c_arr = x_flat, c_flat

    for l in range(depth):
        last = l == depth - 1
        mx = mod[l, 0:b_].reshape(b_, 6, 1, d_)
        sh1, sc1, g1, sh2, sc2, g2 = (mx[:, i] for i in range(6))
        mc = jnp.broadcast_to(mod[l, b_].reshape(1, 6, 1, d_), (b_, 6, 1, d_))
        gm_w = gm_ws[l].astype(BF16)
        gm_b = jnp.repeat(gm_bs[l].T, GM_HEAD, axis=1)
        sink = attn_sink[l]

        qx, kvx, yx = _inproj_call(x_arr, x_off, b_, s_, ts, sc1, sh1, w_in, l, cos, sin_a, sin_b,
                                   conv_w[l], gm_w, gm_b)
        qc, kvc, yc = _inproj_call(c_arr, c_off, b_, c_len, c_len, mc[:, 1], mc[:, 0], w_in, l,
                                   ones_c, zeros_c, zeros_c, conv_w[l], gm_w, gm_b)
        att_x = _win_attn_call(sink, qx, kvx, kvc, b_, s_, c_len)

        w_r = jnp.zeros((d_, ROUTER_COLS), F32).at[:, 0:N_GROUPS].set(w_rg[l]) \
            .at[:, N_GROUPS:N_GROUPS + N_EXPERTS].set(w_re[l])
        w_r_hi = w_r.astype(BF16)
        w_r_lo = (w_r - w_r_hi.astype(F32)).astype(BF16)
        wr = jnp.concatenate([w_r_hi, w_r_lo], axis=1)
        br = jnp.zeros((1, ROUTER_COLS), F32).at[0, 0:N_GROUPS].set(b_rg[l]) \
            .at[0, N_GROUPS:N_GROUPS + N_EXPERTS].set(b_re[l])
        lng1, lnb1 = ln1_g[l].reshape(1, d_), ln1_b[l].reshape(1, d_)
        lng2, lnb2 = ln2_g[l].reshape(1, d_), ln2_b[l].reshape(1, d_)

        n_tot = n_lat if last else n_lat + n_ctx
        gvecs1 = jnp.concatenate([g1, mc[0:1, 2]], axis=0)
        scvecs2 = jnp.concatenate([sc2, mc[0:1, 4]], axis=0)
        shvecs2 = jnp.concatenate([sh2, mc[0:1, 3]], axis=0)
        if last:
            x1, h2t, route, route_t, tcnt = _outproj_call(att_x, att_x, yx, yx, x_arr, x_off, x_arr, x_off, n_lat, 0, s_,
                                                 w_out, l, gvecs1, lng1, lnb1, scvecs2, shvecs2, wr, br, alpha)
        else:
            att_c = _ctx_attn_call(sink, qc, kvc, b_, c_len)
            x1, h2t, route, route_t, tcnt = _outproj_call(att_x, att_c, yx, yc, x_arr, x_off, c_arr, c_off, n_lat, n_ctx,
                                                 s_, w_out, l, gvecs1, lng1, lnb1, scvecs2, shvecs2, wr, br, alpha)

        block_e, next_e, n_used, seg, pos = _slots(route_t, tcnt[:, 0, :])
        n_slots = block_e.shape[0] * MOE_BLOCK

        def per_tile(tile):
            return jnp.concatenate([p.reshape(n_tot // tile, 1, tile) for p in pos], axis=2)

        xs = _dispatch_call(seg, per_tile(DISPATCH_TILE), h2t, n_slots)
        ys = _moe_call(block_e, next_e, n_used, xs, w1, w3, w2, l)

        gvecs = jnp.concatenate([g2, mc[0:1, 5]], axis=0)
        out = _combine_call(per_tile(COMBINE_TILE), x1, route, gvecs, lng2, lnb2, ys, n_tot, s_ // COMBINE_TILE,
                            b_, alpha)
        x_arr, x_off = out, 0
        c_arr, c_off = out, n_lat

    return x_arr.reshape(b_, s_, d_)
```

```python
import functools

import jax
import jax.numpy as jnp
from jax import lax
from jax.experimental import pallas as pl
from jax.experimental.pallas import tpu as pltpu

F32 = jnp.float32
BF16 = jnp.bfloat16

D_MODEL = 1024
GRID_W = 64
HEAD_DIM = 64
N_HEADS = 8
N_KV_HEADS = 2
ATTN_WIDTH = N_HEADS * HEAD_DIM
KV_WIDTH = N_KV_HEADS * HEAD_DIM
WINDOW = 128
ATTN_SCALE = HEAD_DIM ** -0.5
LOG2E = 1.4426950408889634
ROPE_BASE = 10000.0
CONV_WIDTH = D_MODEL // 4
GM_WIDTH = D_MODEL // 4
GM_GROUPS = 4
GM_HEAD = GM_WIDTH // GM_GROUPS
CHUNK = 128
LOCAL_WIDTH = CONV_WIDTH + GM_WIDTH
IN_WIDTH = ATTN_WIDTH + 2 * KV_WIDTH + 3 * CONV_WIDTH + 2 * GM_WIDTH
QKV_WIDTH = ATTN_WIDTH + 2 * KV_WIDTH
KV_PACK_WIDTH = 4 * KV_WIDTH
N_GROUPS = 4
EXP_PER_GROUP = 8
N_EXPERTS = N_GROUPS * EXP_PER_GROUP
TOP_K = 2
D_EXPERT = D_MODEL // 2
MOE_BLOCK = 512
LN_EPS = 1e-6
NEG_INF = -1e30

LANES = 128
SUBLANES = 8
VMEM_LIMIT_BYTES = 48 * 1024 * 1024
N_DMA_PRIORITIES = 2

ROW_TILE_ROWS = D_MODEL // LANES
assert ROW_TILE_ROWS == SUBLANES
PACKED_TILE_ROWS = ROW_TILE_ROWS // 2
TOKENS_PER_PACKED_TILE = SUBLANES // PACKED_TILE_ROWS
assert TOKENS_PER_PACKED_TILE == 2

ROUTER_COLS = LANES
ROUTE_EXPERT = 0
ROUTE_GATE = TOP_K
ROUTE_RANK = 2 * TOP_K
ROPE_HALF_PAIR = HEAD_DIM // 4


def _bdot(a, b):
    return jnp.dot(a, b, preferred_element_type=F32)


def _split_bf16(a):
    hi = a.astype(BF16)
    lo = (a - hi.astype(F32)).astype(BF16)
    return hi, lo


def _layer_norm(r):
    mu = jnp.mean(r, axis=-1, keepdims=True)
    d = r - mu
    var = jnp.mean(d * d, axis=-1, keepdims=True)
    return d * lax.rsqrt(var + LN_EPS)


def _params(*sem, flags=None):
    return pltpu.CompilerParams(dimension_semantics=sem, vmem_limit_bytes=VMEM_LIMIT_BYTES, flags=flags)


def _start_all(copies):
    for cp in copies:
        cp.start()


def _wait_all(copies):
    for cp in copies:
        cp.wait()


ADA_ROWS = 16
ADA_TILE = 1536


def _ada_kernel(c_ref, w_ref, b_ref, o_ref):
    act = jax.nn.silu(c_ref[...])
    ah, al = _split_bf16(act)
    wh, wl = _split_bf16(w_ref[...])
    o_ref[...] = _bdot(ah, wh) + _bdot(ah, wl) + _bdot(al, wh) + b_ref[...]


def _ada_call(cin, w_ada, b_ada):
    depth, d, n = w_ada.shape
    return pl.pallas_call(
        _ada_kernel,
        grid=(depth, n // ADA_TILE),
        in_specs=[
            pl.BlockSpec((ADA_ROWS, d), lambda l, j: (0, 0)),
            pl.BlockSpec((None, d, ADA_TILE), lambda l, j: (l, 0, j)),
            pl.BlockSpec((None, 1, ADA_TILE), lambda l, j: (l, 0, j)),
        ],
        out_specs=pl.BlockSpec((None, ADA_ROWS, ADA_TILE), lambda l, j: (l, 0, j)),
        out_shape=jax.ShapeDtypeStruct((depth, ADA_ROWS, n), F32),
        compiler_params=_params("parallel", "parallel"),
        name="ada_modulation",
    )(cin, w_ada, b_ada.reshape(depth, 1, n))


def _inproj_kernel(x_ref, xp_ref, xn_ref, sc_ref, sh_ref, wf_ref, cos_ref, sa_ref, sb_ref,
                   cw_ref, gw_ref, gb_ref,
                   q_ref, kv_ref, y_ref, w_ref, *, ts):
    t = pl.program_id(1)
    nt = pl.num_programs(1)

    @pl.when((pl.program_id(0) == 0) & (t == 0))
    def _():
        w_ref[...] = wf_ref[...].astype(BF16)

    sc = 1.0 + sc_ref[...]
    sh = sh_ref[...]
    hx = (x_ref[...] * sc + sh).astype(BF16)

    pq = _bdot(hx, w_ref[:, 0:QKV_WIDTH])
    cos = cos_ref[...]
    sa = sa_ref[...]
    sb = sb_ref[...]

    def rope(z):
        return (z * cos + pltpu.roll(z, ROPE_HALF_PAIR, 1) * sa
                + pltpu.roll(z, LANES - ROPE_HALF_PAIR, 1) * sb)

    for j in range(ATTN_WIDTH // LANES):
        sl = slice(j * LANES, (j + 1) * LANES)
        q_ref[:, sl] = (rope(pq[:, sl]) * (ATTN_SCALE * LOG2E)).astype(BF16)
    kr = rope(pq[:, ATTN_WIDTH:ATTN_WIDTH + KV_WIDTH])
    vv = pq[:, ATTN_WIDTH + KV_WIDTH:QKV_WIDTH]
    for j, part in enumerate((kr, pltpu.roll(kr, HEAD_DIM, 1), vv, pltpu.roll(vv, HEAD_DIM, 1))):
        kv_ref[:, j * KV_WIDTH:(j + 1) * KV_WIDTH] = part.astype(BF16)

    pm = _bdot(hx, w_ref[:, QKV_WIDTH:IN_WIDTH])
    cb = pm[:, 0:CONV_WIDTH]
    u = pm[:, CONV_WIDTH:2 * CONV_WIDTH] * pm[:, 2 * CONV_WIDTH:3 * CONV_WIDTH]

    halo = jnp.concatenate([xp_ref[...], xn_ref[...]], axis=0)
    hh = (halo * sc + sh).astype(BF16)
    ph = _bdot(hh, w_ref[:, QKV_WIDTH + CONV_WIDTH:QKV_WIDTH + 3 * CONV_WIDTH])
    uh = ph[:, 0:CONV_WIDTH] * ph[:, CONV_WIDTH:2 * CONV_WIDTH]
    up_row = jnp.where(t > 0, uh[SUBLANES - 1:SUBLANES, :], 0.0)
    dn_row = jnp.where(t < nt - 1, uh[SUBLANES:SUBLANES + 1, :], 0.0)
    ridx = lax.broadcasted_iota(jnp.int32, (ts, CONV_WIDTH), 0)
    u_up = jnp.where(ridx == 0, up_row, pltpu.roll(u, 1, 0))
    u_dn = jnp.where(ridx == ts - 1, dn_row, pltpu.roll(u, ts - 1, 0))
    cw = cw_ref[...]
    y_conv = cb * (u_up * cw[0:1, :] + u * cw[1:2, :] + u_dn * cw[2:3, :])
    y_ref[:, 0:CONV_WIDTH] = y_conv.astype(BF16)

    gu = jax.nn.gelu(pm[:, 3 * CONV_WIDTH:3 * CONV_WIDTH + GM_WIDTH])
    gv = _layer_norm(jax.nn.gelu(pm[:, 3 * CONV_WIDTH + GM_WIDTH:3 * CONV_WIDTH + 2 * GM_WIDTH])).astype(BF16)
    lane = lax.broadcasted_iota(jnp.int32, (CHUNK, LANES), 1)
    zero = jnp.zeros((CHUNK, LANES), BF16)
    for c in range(ts // CHUNK):
        rows = slice(c * CHUNK, (c + 1) * CHUNK)
        for j in range(GM_WIDTH // LANES):
            cols = slice(j * LANES, (j + 1) * LANES)
            vp = gv[rows, cols]
            s = (_bdot(gw_ref[2 * j], jnp.where(lane < GM_HEAD, vp, zero))
                 + _bdot(gw_ref[2 * j + 1], jnp.where(lane >= GM_HEAD, vp, zero))
                 + gb_ref[:, cols])
            y_ref[rows, CONV_WIDTH + j * LANES:CONV_WIDTH + (j + 1) * LANES] = (gu[rows, cols] * s).astype(BF16)


def _inproj_call(x2d, row_off, n_seq, seq_len, ts, sc, sh, w_in, layer, cos, sa, sb, conv_w, gm_w, gm_b):
    nt = seq_len // ts
    off_t = row_off // ts
    off_8 = row_off // SUBLANES
    last_8 = x2d.shape[0] // SUBLANES - 1
    per_tile_8 = ts // SUBLANES
    n_out = n_seq * seq_len
    d = D_MODEL

    def tile_idx(b, t):
        return (off_t + b * nt + t, 0)

    def prev_idx(b, t):
        return (jnp.maximum(off_8 + (b * nt + t) * per_tile_8 - 1, 0), 0)

    def next_idx(b, t):
        return (jnp.minimum(off_8 + (b * nt + t + 1) * per_tile_8, last_8), 0)

    def out_idx(b, t):
        return (b * nt + t, 0)

    vec = pl.BlockSpec((None, 1, d), lambda b, t: (b, 0, 0))
    rope_spec = pl.BlockSpec((ts, LANES), lambda b, t: (t, 0))
    return pl.pallas_call(
        functools.partial(_inproj_kernel, ts=ts),
        grid=(n_seq, nt),
        in_specs=[
            pl.BlockSpec((ts, d), tile_idx),
            pl.BlockSpec((SUBLANES, d), prev_idx),
            pl.BlockSpec((SUBLANES, d), next_idx),
            vec, vec,
            pl.BlockSpec((None, d, IN_WIDTH), lambda b, t: (layer, 0, 0)),
            rope_spec, rope_spec, rope_spec,
            pl.BlockSpec((3, CONV_WIDTH), lambda b, t: (0, 0)),
            pl.BlockSpec((GM_GROUPS, CHUNK, CHUNK), lambda b, t: (0, 0, 0)),
            pl.BlockSpec((CHUNK, GM_WIDTH), lambda b, t: (0, 0)),
        ],
        out_specs=[
            pl.BlockSpec((ts, ATTN_WIDTH), out_idx),
            pl.BlockSpec((ts, KV_PACK_WIDTH), out_idx),
            pl.BlockSpec((ts, LOCAL_WIDTH), out_idx),
        ],
        out_shape=[
            jax.ShapeDtypeStruct((n_out, ATTN_WIDTH), BF16),
            jax.ShapeDtypeStruct((n_out, KV_PACK_WIDTH), BF16),
            jax.ShapeDtypeStruct((n_out, LOCAL_WIDTH), BF16),
        ],
        scratch_shapes=[pltpu.VMEM((d, IN_WIDTH), BF16)],
        compiler_params=_params("arbitrary", "arbitrary"),
        name="in_projection",
    )(x2d, x2d, x2d, sc, sh, w_in, cos, sa, sb, conv_w, gm_w, gm_b)


def _attn_body(q, keys, keys_sw, vals, vals_sw, halo_bias, sink_ref, o_ref, tq):
    nk = keys.shape[0]
    lane = lax.broadcasted_iota(jnp.int32, (nk, LANES), 1)
    lo = lane < HEAD_DIM
    zero = jnp.zeros((nk, LANES), BF16)
    rid = lax.broadcasted_iota(jnp.int32, (2 * tq, 1), 0)
    out_lo = lax.broadcasted_iota(jnp.int32, (2 * tq, LANES), 1) < HEAD_DIM
    nt_dims = (((1,), (1,)), ((), ()))
    gqa = N_HEADS // N_KV_HEADS
    biases = [None] * (nk // LANES)
    if halo_bias is not None:
        assert tq == LANES
        biases[0], biases[2] = halo_bias
    for h in range(N_KV_HEADS):
        k_own, k_oth = (keys, keys_sw) if h == 0 else (keys_sw, keys)
        v_own, v_oth = (vals, vals_sw) if h == 0 else (vals_sw, vals)
        kz = jnp.concatenate([jnp.where(lo, k_own, zero), jnp.where(lo, zero, k_oth)], axis=0)
        vz = jnp.concatenate([jnp.where(lo, v_own, zero), jnp.where(lo, zero, v_oth)], axis=0)
        c0 = h * gqa * HEAD_DIM
        qs = jnp.concatenate([q[:, c0:c0 + LANES], q[:, c0 + LANES:c0 + 2 * LANES]], axis=0)
        s_all = lax.dot_general(qs, kz, nt_dims, preferred_element_type=F32)
        probs, inv_den = [], []
        for par in range(2):
            tiles = []
            for j, bias in enumerate(biases):
                tile = s_all[:, par * nk + j * LANES:par * nk + (j + 1) * LANES]
                tiles.append(tile if bias is None else tile + bias)
            sink = jnp.where(rid < tq, sink_ref[h * gqa + par], sink_ref[h * gqa + 2 + par]) * LOG2E
            tile_max = tiles[0]
            for tile in tiles[1:]:
                tile_max = jnp.maximum(tile_max, tile)
            m = jnp.maximum(jnp.max(tile_max, axis=-1, keepdims=True), sink)
            tile_sum = None
            for tile in tiles:
                p = jnp.exp2(tile - m)
                tile_sum = p if tile_sum is None else tile_sum + p
                probs.append(p.astype(BF16))
            inv_den.append(1.0 / (jnp.sum(tile_sum, axis=-1, keepdims=True) + jnp.exp2(sink - m)))
        o = _bdot(jnp.concatenate(probs, axis=1), vz) * jnp.where(out_lo, inv_den[0], inv_den[1])
        o_ref[:, c0:c0 + LANES] = o[0:tq].astype(BF16)
        o_ref[:, c0 + LANES:c0 + 2 * LANES] = o[tq:2 * tq].astype(BF16)


def _split_kv(kv):
    return tuple(kv[:, j * KV_WIDTH:(j + 1) * KV_WIDTH] for j in range(4))


def _win_attn_kernel(sink_ref, q_ref, kvp_ref, kvc_ref, kvn_ref, kvx_ref, o_ref, *, tq):
    n = pl.program_id(1)
    nb = pl.num_programs(1)
    row = lax.broadcasted_iota(jnp.int32, (tq, tq), 0)
    col = lax.broadcasted_iota(jnp.int32, (tq, tq), 1)
    b_prev = jnp.where((col >= row) & (n > 0), 0.0, NEG_INF)
    b_next = jnp.where((col <= row) & (n < nb - 1), 0.0, NEG_INF)
    halo_bias = (jnp.concatenate([b_prev, b_prev], axis=0), jnp.concatenate([b_next, b_next], axis=0))

    kv = jnp.concatenate([kvp_ref[...], kvc_ref[...], kvn_ref[...], kvx_ref[...]], axis=0)
    _attn_body(q_ref[...], *_split_kv(kv), halo_bias, sink_ref, o_ref, tq)


def _win_attn_call(sink, q, kv, kv_ctx, n_seq, seq_len, ctx_len):
    tq = WINDOW
    nb = seq_len // tq

    def cur(b, n):
        return (b * nb + n, 0)

    def prev(b, n):
        return (b * nb + jnp.maximum(n - 1, 0), 0)

    def nxt(b, n):
        return (b * nb + jnp.minimum(n + 1, nb - 1), 0)

    return pl.pallas_call(
        functools.partial(_win_attn_kernel, tq=tq),
        grid=(n_seq, nb),
        in_specs=[pl.BlockSpec(memory_space=pltpu.SMEM), pl.BlockSpec((tq, ATTN_WIDTH), cur),
                  pl.BlockSpec((tq, KV_PACK_WIDTH), prev), pl.BlockSpec((tq, KV_PACK_WIDTH), cur),
                  pl.BlockSpec((tq, KV_PACK_WIDTH), nxt),
                  pl.BlockSpec((ctx_len, KV_PACK_WIDTH), lambda b, n: (b, 0))],
        out_specs=pl.BlockSpec((tq, ATTN_WIDTH), cur),
        out_shape=jax.ShapeDtypeStruct((n_seq * seq_len, ATTN_WIDTH), BF16),
        compiler_params=_params("parallel", "parallel"),
        name="window_attention",
    )(sink, q, kv, kv, kv, kv_ctx)


def _ctx_attn_kernel(sink_ref, q_ref, kv_ref, o_ref, *, tq):
    _attn_body(q_ref[...], *_split_kv(kv_ref[...]), None, sink_ref, o_ref, tq)


def _ctx_attn_call(sink, q, kv, n_seq, ctx_len):
    def blk(w):
        return pl.BlockSpec((ctx_len, w), lambda b: (b, 0))

    return pl.pallas_call(
        functools.partial(_ctx_attn_kernel, tq=ctx_len),
        grid=(n_seq,),
        in_specs=[pl.BlockSpec(memory_space=pltpu.SMEM), blk(ATTN_WIDTH), blk(KV_PACK_WIDTH)],
        out_specs=blk(ATTN_WIDTH),
        out_shape=jax.ShapeDtypeStruct((n_seq * ctx_len, ATTN_WIDTH), BF16),
        compiler_params=_params("parallel"),
        name="context_attention",
    )(sink, q, kv)


OUT_TILE = 512


def _outproj_kernel(attx_ref, attc_ref, yx_ref, yc_ref, xx_ref, xc_ref, wof_ref, g_ref, lng_ref, lnb_ref,
                    sc_ref, sh_ref, wr_ref, br_ref, before_ref,
                    x1_ref, h2v_ref, route_ref, routet_ref, cnt_ref, wo_ref, o_buf, lg_buf,
                    *, alpha, n_lat_tiles, n_tiles, has_ctx):
    step = pl.program_id(0)

    @pl.when(step == 0)
    def _():
        wo_ref[...] = wof_ref[...].astype(BF16)
        o_buf[1] = jnp.zeros(o_buf.shape[1:], F32)
        lg_buf[1] = jnp.zeros(lg_buf.shape[1:], F32)

    o = o_buf[(step + 1) % 2]
    lg = lg_buf[(step + 1) % 2]

    if has_ctx:
        att = jnp.where(jnp.minimum(step, n_tiles - 1) < n_lat_tiles, attx_ref[...], attc_ref[...])
        yloc = jnp.where(jnp.minimum(step, n_tiles - 1) < n_lat_tiles, yx_ref[...], yc_ref[...])
        xin = jnp.where(jnp.clip(step - 1, 0, n_tiles - 1) < n_lat_tiles, xx_ref[...], xc_ref[...])
    else:
        att, yloc, xin = attx_ref[...], yx_ref[...], xx_ref[...]
    o_buf[step % 2] = _bdot(att, wo_ref[0:ATTN_WIDTH, :]) + _bdot(yloc, wo_ref[ATTN_WIDTH:, :])
    x1 = _layer_norm(alpha * xin + g_ref[...] * o) * lng_ref[...] + lnb_ref[...]
    x1_ref[...] = x1
    h2 = x1 * (1.0 + sc_ref[...]) + sh_ref[...]
    half = D_MODEL // 2
    hi_bits = lax.bitcast_convert_type(h2[:, 0:half].astype(BF16).astype(F32), jnp.uint32)
    lo_bits = lax.bitcast_convert_type(h2[:, half:].astype(BF16).astype(F32), jnp.uint32)
    packed = hi_bits | (lo_bits >> 16)
    for c in range(PACKED_TILE_ROWS):
        h2v_ref[:, c, :, :] = packed[:, c * LANES:(c + 1) * LANES].reshape(h2.shape[0] // SUBLANES, SUBLANES, LANES)

    hh, hl = _split_bf16(h2)
    t1 = _bdot(hh, wr_ref[...])
    lg_buf[step % 2] = (t1[:, 0:ROUTER_COLS] + t1[:, ROUTER_COLS:] + _bdot(hl, wr_ref[:, 0:ROUTER_COLS])
                        + br_ref[...])

    ts = lg.shape[0]
    lane = lax.broadcasted_iota(jnp.int32, (ts, ROUTER_COLS), 1).astype(F32)
    big = jnp.float32(ROUTER_COLS)

    def top1(v):
        m = jnp.max(v, axis=-1, keepdims=True)
        return m, jnp.min(jnp.where(v == m, lane, big), axis=-1, keepdims=True)

    gl = jnp.where(lane < N_GROUPS, lg, NEG_INF)
    g_val, g_idx = top1(gl)
    lse = g_val + jnp.log(jnp.sum(jnp.exp(gl - g_val), axis=-1, keepdims=True))
    p_group = jnp.exp(g_val - lse)
    e_lo = N_GROUPS + EXP_PER_GROUP * g_idx
    el = jnp.where((lane >= e_lo) & (lane < e_lo + EXP_PER_GROUP), lg, NEG_INF)
    e1, l1 = top1(el)
    e2, l2 = top1(jnp.where(lane == l1, NEG_INF, el))
    z = jnp.exp(e2 - e1)
    gate1 = p_group / (1.0 + z)
    gate2 = p_group * z / (1.0 + z)
    x1id = l1 - N_GROUPS
    x2id = l2 - N_GROUPS

    sel1 = lane == x1id
    sel2 = lane == x2id
    onehot = jnp.where(sel1 | sel2, 1.0, 0.0)
    prefix = _bdot(before_ref[...], onehot.astype(BF16))
    rank1 = jnp.sum(jnp.where(sel1, prefix, 0.0), axis=-1, keepdims=True)
    rank2 = jnp.sum(jnp.where(sel2, prefix, 0.0), axis=-1, keepdims=True)
    route = jnp.zeros((ts, ROUTER_COLS), F32)
    for col, val in enumerate((x1id, x2id, gate1, gate2, rank1, rank2)):
        route = jnp.where(lane == col, val, route)
    route_ref[...] = route
    routet_ref[...] = route.T[0:SUBLANES, :]
    cnt_ref[...] =jnp.broadcast_to(jnp.sum(onehot, axis=0, keepdims=True), (SUBLANES, ROUTER_COLS))


def _outproj_call(att_x, att_c, y_x, y_c, x_arr, x_off, c_arr, c_off, n_lat, n_ctx, seq_len,
                  w_out, layer, gvecs, lng, lnb, scvecs, shvecs, wr, br, alpha):
    ts = OUT_TILE
    d = D_MODEL
    n_lat_tiles = n_lat // ts
    n_tiles = (n_lat + n_ctx) // ts
    tiles_per_seq = seq_len // ts
    n_seq = n_lat // seq_len
    xo, co = x_off // ts, c_off // ts

    def mm_tile(s):
        return jnp.minimum(s, n_tiles - 1)

    def ln_tile(s):
        return jnp.clip(s - 1, 0, n_tiles - 1)

    def route_tile(s):
        return jnp.clip(s - 2, 0, n_tiles - 1)

    def lat_loc(s):
        return (jnp.minimum(mm_tile(s), n_lat_tiles - 1), 0)

    def ctx_loc(s):
        return (jnp.maximum(mm_tile(s) - n_lat_tiles, 0), 0)

    def lat_in(s):
        return (xo + jnp.minimum(ln_tile(s), n_lat_tiles - 1), 0)

    def ctx_in(s):
        return (co + jnp.maximum(ln_tile(s) - n_lat_tiles, 0), 0)

    def vec_idx(s):
        i = ln_tile(s)
        return (jnp.where(i < n_lat_tiles, i // tiles_per_seq, n_seq), 0, 0)

    vecb = pl.BlockSpec((None, 1, d), vec_idx)
    vec0 = pl.BlockSpec((1, d), lambda i: (0, 0))
    return pl.pallas_call(
        functools.partial(_outproj_kernel, alpha=alpha, n_lat_tiles=n_lat_tiles, n_tiles=n_tiles,
                          has_ctx=n_ctx > 0),
        grid=(n_tiles + 2,),
        in_specs=[
            pl.BlockSpec((ts, ATTN_WIDTH), lat_loc), pl.BlockSpec((ts, ATTN_WIDTH), ctx_loc),
            pl.BlockSpec((ts, LOCAL_WIDTH), lat_loc), pl.BlockSpec((ts, LOCAL_WIDTH), ctx_loc),
            pl.BlockSpec((ts, d), lat_in), pl.BlockSpec((ts, d), ctx_in),
            pl.BlockSpec((None, d, d), lambda i: (layer, 0, 0)),
            vecb, vec0, vec0, vecb, vecb,
            pl.BlockSpec((d, 2 * ROUTER_COLS), lambda i: (0, 0)),
            pl.BlockSpec((1, ROUTER_COLS), lambda i: (0, 0)),
            pl.BlockSpec((ts, ts), lambda i: (0, 0)),
        ],
        out_specs=[pl.BlockSpec((ts, d), lambda s: (ln_tile(s), 0)),
                   pl.BlockSpec((ts // SUBLANES, PACKED_TILE_ROWS, SUBLANES, LANES),
                                lambda s: (ln_tile(s), 0, 0, 0)),
                   pl.BlockSpec((ts, ROUTER_COLS), lambda s: (route_tile(s), 0)),
                   pl.BlockSpec((None, SUBLANES, ts), lambda s: (route_tile(s), 0, 0)),
                   pl.BlockSpec((None, SUBLANES, ROUTER_COLS), lambda s: (route_tile(s), 0, 0))],
        out_shape=[jax.ShapeDtypeStruct((n_lat + n_ctx, d), F32),
                   jax.ShapeDtypeStruct(((n_lat + n_ctx) // SUBLANES, PACKED_TILE_ROWS, SUBLANES, LANES),
                                        jnp.uint32),
                   jax.ShapeDtypeStruct((n_lat + n_ctx, ROUTER_COLS), F32),
                   jax.ShapeDtypeStruct((n_tiles, SUBLANES, ts), F32),
                   jax.ShapeDtypeStruct((n_tiles, SUBLANES, ROUTER_COLS), F32)],
        scratch_shapes=[pltpu.VMEM((d, d), BF16), pltpu.VMEM((2, ts, d), F32),
                        pltpu.VMEM((2, ts, ROUTER_COLS), F32)],
        compiler_params=_params("arbitrary"),
        name="out_projection",
    )(att_x, att_c, y_x, y_c, x_arr, c_arr, w_out, gvecs, lng, lnb, scvecs, shvecs, wr, br,
      jnp.tril(jnp.ones((ts, ts), BF16), -1))


DISPATCH_TILE = 1024


def _dispatch_kernel(seg_ref, pos_ref, h_ref, xs_out, zbuf, zsem, sem):
    block_tiles = MOE_BLOCK // TOKENS_PER_PACKED_TILE

    @pl.when(pl.program_id(0) == 0)
    def _():
        zbuf[...] = jnp.zeros_like(zbuf)

        def zero_copy(e):
            first_tile = (seg_ref[0, e] - MOE_BLOCK) // TOKENS_PER_PACKED_TILE
            return pltpu.make_async_copy(zbuf, xs_out.at[pl.ds(first_tile, block_tiles)], zsem)

        for e in range(N_EXPERTS):
            @pl.when(seg_ref[1, e] > 0)
            def _():
                zero_copy(e).start()
        for e in range(N_EXPERTS):
            @pl.when(seg_ref[1, e] > 0)
            def _():
                zero_copy(e).wait()

        def tail_copy(b):
            return pltpu.make_async_copy(zbuf, xs_out.at[pl.ds(b * block_tiles, block_tiles)], zsem)

        n_blocks = xs_out.shape[0] // block_tiles
        first_unused = seg_ref[0, N_EXPERTS - 1] // MOE_BLOCK
        lax.fori_loop(first_unused, n_blocks, lambda b, c: (tail_copy(b).start(), c)[1], 0)
        lax.fori_loop(first_unused, n_blocks, lambda b, c: (tail_copy(b).wait(), c)[1], 0)

    def body(j, carry):
        for s in range(SUBLANES):
            t = SUBLANES * j + s
            for k in range(TOP_K):
                p = pos_ref[0, 0, k * DISPATCH_TILE + t]
                dst = xs_out.at[p >> 1, pl.ds(PACKED_TILE_ROWS * (p & 1), PACKED_TILE_ROWS), :]
                pltpu.make_async_copy(h_ref.at[j, :, s, :], dst, sem).start(priority=k)
        return carry

    lax.fori_loop(0, DISPATCH_TILE // SUBLANES, body, 0)
    for k in range(TOP_K):
        pltpu.make_async_copy(h_ref, h_ref, sem).wait()


def _dispatch_call(seg, pos, h2t, n_slots):
    n_tok = h2t.shape[0] * SUBLANES
    n_tiles = n_tok // DISPATCH_TILE
    grid_spec = pltpu.PrefetchScalarGridSpec(
        num_scalar_prefetch=1,
        grid=(n_tiles,),
        in_specs=[
            pl.BlockSpec((1, 1, TOP_K * DISPATCH_TILE), lambda i, seg: (i, 0, 0), memory_space=pltpu.SMEM),
            pl.BlockSpec((DISPATCH_TILE // SUBLANES, PACKED_TILE_ROWS, SUBLANES, LANES),
                         lambda i, seg: (i, 0, 0, 0)),
        ],
        out_specs=pl.BlockSpec(memory_space=pl.ANY),
        scratch_shapes=[pltpu.VMEM((MOE_BLOCK // TOKENS_PER_PACKED_TILE, SUBLANES, LANES), jnp.uint32),
                        pltpu.SemaphoreType.DMA(()), pltpu.SemaphoreType.DMA(())],
    )
    return pl.pallas_call(
        _dispatch_kernel,
        grid_spec=grid_spec,
        out_shape=jax.ShapeDtypeStruct((n_slots // TOKENS_PER_PACKED_TILE, SUBLANES, LANES), jnp.uint32),
        compiler_params=_params("arbitrary"),
        name="moe_dispatch",
    )(seg, pos, h2t)


def _moe_kernel(be_ref, ne_ref, nu_ref, xs_hbm, w1_hbm, w3_hbm, w2_hbm, ys_hbm, w1b, w3b, w2b, w1f, w3f, w2f,
                xbuf, obuf, isem, osem, wsem, *, layer):
    i = pl.program_id(0)
    n_used = nu_ref[0]
    slot = i % 2
    groups = MOE_BLOCK // SUBLANES
    half = D_MODEL // 2

    def in_copies(blk, at_slot):
        return [pltpu.make_async_copy(
            xs_hbm.at[pl.ds(blk * groups, groups), s // TOKENS_PER_PACKED_TILE,
                      pl.ds(PACKED_TILE_ROWS * (s % TOKENS_PER_PACKED_TILE), PACKED_TILE_ROWS), :],
            xbuf.at[at_slot, :, :, s, :], isem.at[at_slot]) for s in range(SUBLANES)]

    def out_copies(blk, at_slot):
        return [pltpu.make_async_copy(obuf.at[at_slot, :, :, s, :], ys_hbm.at[pl.ds(blk * groups, groups), s],
                                      osem.at[at_slot]) for s in range(SUBLANES)]

    def write_rows(at_slot, y):
        for c in range(ROW_TILE_ROWS):
            obuf[at_slot, :, c, :, :] = y[:, c * LANES:(c + 1) * LANES].reshape(groups, SUBLANES, LANES)

    @pl.when(i == 0)
    def _():
        _start_all(in_copies(0, 0))

    @pl.when(i + 1 < n_used)
    def _():
        _start_all(in_copies(i + 1, 1 - slot))

    def weight_copies(e):
        return [pltpu.make_async_copy(src.at[layer, e], dst, wsem)
                for src, dst in ((w1_hbm, w1f), (w3_hbm, w3f), (w2_hbm, w2f))]

    @pl.when(i == 0)
    def _():
        _start_all(weight_copies(be_ref[0]))

    @pl.when((i < n_used) & ((i == 0) | (be_ref[i] != be_ref[jnp.maximum(i - 1, 0)])))
    def _():
        _wait_all(weight_copies(be_ref[i]))
        w1b[...] = w1f[...].astype(BF16)
        w3b[...] = w3f[...].astype(BF16)
        w2b[...] = w2f[...].astype(BF16)

        @pl.when(ne_ref[i] != be_ref[i])
        def _():
            _start_all(weight_copies(ne_ref[i]))

    @pl.when(i < n_used)
    def _():
        _wait_all(in_copies(i, slot))
        p = jnp.concatenate([xbuf[slot, :, c, :, :].reshape(MOE_BLOCK, LANES) for c in range(PACKED_TILE_ROWS)],
                            axis=1)
        xa = lax.bitcast_convert_type(p & jnp.uint32(0xFFFF0000), F32).astype(BF16)
        xb = lax.bitcast_convert_type(p << 16, F32).astype(BF16)
        h1 = _bdot(xa, w1b[0:half, :]) + _bdot(xb, w1b[half:, :])
        h3 = _bdot(xa, w3b[0:half, :]) + _bdot(xb, w3b[half:, :])
        write_rows(slot, _bdot((jax.nn.silu(h1) * h3).astype(BF16), w2b[...]))
        _start_all(out_copies(i, slot))

        @pl.when(i >= 1)
        def _():
            _wait_all(out_copies(i - 1, 1 - slot))

        @pl.when(i == n_used - 1)
        def _():
            _wait_all(out_copies(i, slot))

    @pl.when(i >= n_used)
    def _():
        obuf[slot] = jnp.zeros(obuf.shape[1:], F32)
        _start_all(out_copies(i, slot))
        _wait_all(out_copies(i, slot))


def _moe_call(block_e, next_e, n_used, xs, w1, w3, w2, layer):
    n_blocks = block_e.shape[0]
    n_slots = n_blocks * MOE_BLOCK
    d = D_MODEL
    groups = MOE_BLOCK // SUBLANES
    grid_spec = pltpu.PrefetchScalarGridSpec(
        num_scalar_prefetch=3,
        grid=(n_blocks,),
        in_specs=[pl.BlockSpec(memory_space=pl.ANY)] * 4,
        out_specs=pl.BlockSpec(memory_space=pl.ANY),
        scratch_shapes=[pltpu.VMEM((d, D_EXPERT), BF16), pltpu.VMEM((d, D_EXPERT), BF16),
                        pltpu.VMEM((D_EXPERT, d), BF16),
                        pltpu.VMEM((d, D_EXPERT), F32), pltpu.VMEM((d, D_EXPERT), F32),
                        pltpu.VMEM((D_EXPERT, d), F32),
                        pltpu.VMEM((2, groups, PACKED_TILE_ROWS, SUBLANES, LANES), jnp.uint32),
                        pltpu.VMEM((2, groups, ROW_TILE_ROWS, SUBLANES, LANES), F32),
                        pltpu.SemaphoreType.DMA((2,)), pltpu.SemaphoreType.DMA((2,)),
                        pltpu.SemaphoreType.DMA(())],
    )
    xs_view = (n_slots // SUBLANES, SUBLANES // TOKENS_PER_PACKED_TILE, SUBLANES, LANES)
    ys_view = (n_slots // SUBLANES, SUBLANES, ROW_TILE_ROWS, LANES)
    ys = pl.pallas_call(
        functools.partial(_moe_kernel, layer=layer),
        grid_spec=grid_spec,
        out_shape=jax.ShapeDtypeStruct(ys_view, F32),
        compiler_params=_params("arbitrary"),
        name="moe_experts",
    )(block_e, next_e, n_used, xs.reshape(xs_view), w1, w3, w2)
    return ys.reshape(n_slots, ROW_TILE_ROWS, LANES)


COMBINE_TILE = 512


def _combine_kernel(pos0_ref, posn_ref, x1_ref, route_ref, g_ref, lng_ref, lnb_ref, ys_hbm, o_ref, ybuf, sem,
                    *, alpha):
    i = pl.program_id(0)
    nb = pl.num_programs(0)
    n_rows = TOP_K * COMBINE_TILE

    def issue(pos_ref, slot):
        def body(j, carry):
            for s in range(SUBLANES):
                pltpu.make_async_copy(ys_hbm.at[pos_ref[0, 0, SUBLANES * j + s]], ybuf.at[slot, j, :, s, :],
                                      sem.at[slot]).start(priority=s % N_DMA_PRIORITIES)
            return carry
        lax.fori_loop(0, n_rows // SUBLANES, body, 0)

    @pl.when(i == 0)
    def _():
        issue(pos0_ref, 0)

    @pl.when(i + 1 < nb)
    def _():
        issue(posn_ref, (i + 1) % 2)

    slot = i % 2
    pltpu.make_async_copy(ybuf.at[slot], ybuf.at[slot], sem.at[slot]).wait()
    yrows = jnp.concatenate([ybuf[slot, :, c, :, :].reshape(n_rows, LANES) for c in range(ROW_TILE_ROWS)], axis=1)
    route = route_ref[...]
    y = (yrows[0:COMBINE_TILE, :] * route[:, ROUTE_GATE:ROUTE_GATE + 1]
         + yrows[COMBINE_TILE:, :] * route[:, ROUTE_GATE + 1:ROUTE_GATE + 2])
    o_ref[...] = _layer_norm(alpha * x1_ref[...] + g_ref[...] * y) * lng_ref[...] + lnb_ref[...]


def _combine_call(pos, x1, route, gvecs, lng, lnb, ys, n_tok, tiles_per_seq, n_seq, alpha):
    d = D_MODEL
    n_tiles = n_tok // COMBINE_TILE
    n_lat_tiles = tiles_per_seq * n_seq

    def g_idx(i):
        return (jnp.where(i < n_lat_tiles, i // tiles_per_seq, n_seq), 0, 0)

    smem_blk = (1, 1, TOP_K * COMBINE_TILE)
    return pl.pallas_call(
        functools.partial(_combine_kernel, alpha=alpha),
        grid=(n_tiles,),
        in_specs=[
            pl.BlockSpec(smem_blk, lambda i: (0, 0, 0), memory_space=pltpu.SMEM),
            pl.BlockSpec(smem_blk, lambda i: (jnp.minimum(i + 1, n_tiles - 1), 0, 0), memory_space=pltpu.SMEM),
            pl.BlockSpec((COMBINE_TILE, d), lambda i: (i, 0)),
            pl.BlockSpec((COMBINE_TILE, ROUTER_COLS), lambda i: (i, 0)),
            pl.BlockSpec((None, 1, d), g_idx),
            pl.BlockSpec((1, d), lambda i: (0, 0)),
            pl.BlockSpec((1, d), lambda i: (0, 0)),
            pl.BlockSpec(memory_space=pl.ANY),
        ],
        out_specs=pl.BlockSpec((COMBINE_TILE, d), lambda i: (i, 0)),
        out_shape=jax.ShapeDtypeStruct((n_tok, d), F32),
        scratch_shapes=[pltpu.VMEM((2, TOP_K * COMBINE_TILE // SUBLANES, ROW_TILE_ROWS, SUBLANES, LANES), F32),
                        pltpu.SemaphoreType.DMA((2,))],
        compiler_params=_params("arbitrary"),
        name="moe_combine",
    )(pos, pos, x1, route, gvecs, lng, lnb, ys)


def _slots(route_t, tile_counts):
    n_tiles = tile_counts.shape[0]
    n = n_tiles * route_t.shape[2]
    tc = tile_counts.astype(jnp.int32)
    counts = jnp.sum(tc, axis=0)
    padded = (counts + MOE_BLOCK - 1) // MOE_BLOCK * MOE_BLOCK
    pends = jnp.cumsum(padded)
    base = (pends - padded)[None, :] + jnp.cumsum(tc, axis=0) - tc
    n_blocks = -(-n * TOP_K // MOE_BLOCK) + N_EXPERTS
    block_start = jnp.arange(n_blocks, dtype=jnp.int32) * MOE_BLOCK
    block_e = jnp.minimum(jnp.sum(pends[None, 0:N_EXPERTS] <= block_start[:, None], axis=1),
                          N_EXPERTS - 1).astype(jnp.int32)
    ids = jnp.arange(N_EXPERTS, dtype=jnp.int32)
    later = (ids[None, :] > ids[:, None]) & (padded[None, 0:N_EXPERTS] > 0)
    next_nonempty = jnp.min(jnp.where(later, ids[None, :], N_EXPERTS), axis=1)
    next_of = jnp.where(next_nonempty < N_EXPERTS, next_nonempty, ids)
    next_e = jnp.sum(jnp.where(block_e[:, None] == ids[None, :], next_of[None, :], 0), axis=1).astype(jnp.int32)
    n_used = (pends[N_EXPERTS - 1] // MOE_BLOCK).astype(jnp.int32).reshape(1)
    seg = jnp.stack([pends[0:N_EXPERTS], padded[0:N_EXPERTS]]).astype(jnp.int32)
    pos = []
    for k in range(TOP_K):
        e = route_t[:, ROUTE_EXPERT + k, :].astype(jnp.int32)
        hit = e[:, None, :] == jnp.arange(N_EXPERTS, dtype=jnp.int32)[None, :, None]
        b = jnp.sum(jnp.where(hit, base[:, 0:N_EXPERTS, None], 0), axis=1)
        pos.append((b + route_t[:, ROUTE_RANK + k, :].astype(jnp.int32)).reshape(n))
    return block_e, next_e, n_used, seg, pos


def _rope_tables(seq_len):
    m = HEAD_DIM // 4
    freqs = ROPE_BASE ** (-jnp.arange(m, dtype=F32) / m)
    t = jnp.arange(seq_len)
    row = (t // GRID_W).astype(F32)[:, None] * freqs[None, :]
    col = (t % GRID_W).astype(F32)[:, None] * freqs[None, :]
    cos = jnp.concatenate([jnp.cos(row), jnp.cos(row), jnp.cos(col), jnp.cos(col)], axis=-1)
    zero = jnp.zeros_like(row)
    sin_a = jnp.concatenate([zero, jnp.sin(row), zero, jnp.sin(col)], axis=-1)
    sin_b = jnp.concatenate([-jnp.sin(row), zero, -jnp.sin(col), zero], axis=-1)
    rep = LANES // HEAD_DIM
    return jnp.tile(cos, (1, rep)), jnp.tile(sin_a, (1, rep)), jnp.tile(sin_b, (1, rep))


def kernel(x, c, ctx, c_ctx, w_ada, b_ada, w_in, conv_w, attn_sink, gm_ws, gm_bs, w_out, ln1_g, ln1_b,
           w_rg, b_rg, w_re, b_re, w1, w3, w2, ln2_g, ln2_b):
    b_, s_, d_ = x.shape
    c_len = ctx.shape[1]
    depth = w_ada.shape[0]
    alpha = (2 * depth) ** 0.25
    n_lat = b_ * s_
    n_ctx = b_ * c_len
    ts = 1024

    cin = jnp.zeros((ADA_ROWS, d_), F32).at[0:b_].set(c).at[b_].set(c_ctx)
    mod = _ada_call(cin, w_ada, b_ada)

    cos, sin_a, sin_b = _rope_tables(s_)
    ones_c = jnp.ones((c_len, LANES), F32)
    zeros_c = jnp.zeros((c_len, LANES), F32)

    x_flat = x.reshape(n_lat, d_)
    c_flat = ctx.reshape(n_ctx, d_)
    x_off, c_off = 0, 0
    x_arr, c_arr = x_flat, c_flat

    for l in range(depth):
        last = l == depth - 1
        mx = mod[l, 0:b_].reshape(b_, 6, 1, d_)
        sh1, sc1, g1, sh2, sc2, g2 = (mx[:, i] for i in range(6))
        mc = jnp.broadcast_to(mod[l, b_].reshape(1, 6, 1, d_), (b_, 6, 1, d_))
        gm_w = gm_ws[l].astype(BF16)
        gm_b = jnp.repeat(gm_bs[l].T, GM_HEAD, axis=1)
        sink = attn_sink[l]

        qx, kvx, yx = _inproj_call(x_arr, x_off, b_, s_, ts, sc1, sh1, w_in, l, cos, sin_a, sin_b,
                                   conv_w[l], gm_w, gm_b)
        qc, kvc, yc = _inproj_call(c_arr, c_off, b_, c_len, c_len, mc[:, 1], mc[:, 0], w_in, l,
                                   ones_c, zeros_c, zeros_c, conv_w[l], gm_w, gm_b)
        att_x = _win_attn_call(sink, qx, kvx, kvc, b_, s_, c_len)

        w_r = jnp.zeros((d_, ROUTER_COLS), F32).at[:, 0:N_GROUPS].set(w_rg[l]) \
            .at[:, N_GROUPS:N_GROUPS + N_EXPERTS].set(w_re[l])
        w_r_hi = w_r.astype(BF16)
        w_r_lo = (w_r - w_r_hi.astype(F32)).astype(BF16)
        wr = jnp.concatenate([w_r_hi, w_r_lo], axis=1)
        br = jnp.zeros((1, ROUTER_COLS), F32).at[0, 0:N_GROUPS].set(b_rg[l]) \
            .at[0, N_GROUPS:N_GROUPS + N_EXPERTS].set(b_re[l])
        lng1, lnb1 = ln1_g[l].reshape(1, d_), ln1_b[l].reshape(1, d_)
        lng2, lnb2 = ln2_g[l].reshape(1, d_), ln2_b[l].reshape(1, d_)

        n_tot = n_lat if last else n_lat + n_ctx
        gvecs1 = jnp.concatenate([g1, mc[0:1, 2]], axis=0)
        scvecs2 = jnp.concatenate([sc2, mc[0:1, 4]], axis=0)
        shvecs2 = jnp.concatenate([sh2, mc[0:1, 3]], axis=0)
        if last:
            x1, h2t, route, route_t, tcnt = _outproj_call(att_x, att_x, yx, yx, x_arr, x_off, x_arr, x_off, n_lat, 0, s_,
                                                 w_out, l, gvecs1, lng1, lnb1, scvecs2, shvecs2, wr, br, alpha)
        else:
            att_c = _ctx_attn_call(sink, qc, kvc, b_, c_len)
            x1, h2t, route, route_t, tcnt = _outproj_call(att_x, att_c, yx, yc, x_arr, x_off, c_arr, c_off, n_lat, n_ctx,
                                                 s_, w_out, l, gvecs1, lng1, lnb1, scvecs2, shvecs2, wr, br, alpha)

        block_e, next_e, n_used, seg, pos = _slots(route_t, tcnt[:, 0, :])
        n_slots = block_e.shape[0] * MOE_BLOCK

        def per_tile(tile):
            return jnp.concatenate([p.reshape(n_tot // tile, 1, tile) for p in pos], axis=2)

        xs = _dispatch_call(seg, per_tile(DISPATCH_TILE), h2t, n_slots)
        ys = _moe_call(block_e, next_e, n_used, xs, w1, w3, w2, l)

        gvecs = jnp.concatenate([g2, mc[0:1, 5]], axis=0)
        out = _combine_call(per_tile(COMBINE_TILE), x1, route, gvecs, lng2, lnb2, ys, n_tot, s_ // COMBINE_TILE,
                            b_, alpha)
        x_arr, x_off = out, 0
        c_arr, c_off = out, n_lat

    return x_arr.reshape(b_, s_, d_)
```

```python
import functools

import jax
import jax.numpy as jnp
from jax import lax
from jax.experimental import pallas as pl
from jax.experimental.pallas import tpu as pltpu

F32 = jnp.float32
BF16 = jnp.bfloat16

D_MODEL = 1024
GRID_W = 64
HEAD_DIM = 64
N_HEADS = 8
N_KV_HEADS = 2
ATTN_WIDTH = N_HEADS * HEAD_DIM
KV_WIDTH = N_KV_HEADS * HEAD_DIM
WINDOW = 128
ATTN_SCALE = HEAD_DIM ** -0.5
LOG2E = 1.4426950408889634
ROPE_BASE = 10000.0
CONV_WIDTH = D_MODEL // 4
GM_WIDTH = D_MODEL // 4
GM_GROUPS = 4
GM_HEAD = GM_WIDTH // GM_GROUPS
CHUNK = 128
LOCAL_WIDTH = CONV_WIDTH + GM_WIDTH
IN_WIDTH = ATTN_WIDTH + 2 * KV_WIDTH + 3 * CONV_WIDTH + 2 * GM_WIDTH
QKV_WIDTH = ATTN_WIDTH + 2 * KV_WIDTH
KV_PACK_WIDTH = 4 * KV_WIDTH
N_GROUPS = 4
EXP_PER_GROUP = 8
N_EXPERTS = N_GROUPS * EXP_PER_GROUP
TOP_K = 2
D_EXPERT = D_MODEL // 2
MOE_BLOCK = 512
LN_EPS = 1e-6
NEG_INF = -1e30

LANES = 128
SUBLANES = 8
VMEM_LIMIT_BYTES = 48 * 1024 * 1024
N_DMA_PRIORITIES = 2

ROW_TILE_ROWS = D_MODEL // LANES
assert ROW_TILE_ROWS == SUBLANES
PACKED_TILE_ROWS = ROW_TILE_ROWS // 2
TOKENS_PER_PACKED_TILE = SUBLANES // PACKED_TILE_ROWS
assert TOKENS_PER_PACKED_TILE == 2

ROUTER_COLS = LANES
ROUTE_EXPERT = 0
ROUTE_GATE = TOP_K
ROUTE_RANK = 2 * TOP_K
ROPE_HALF_PAIR = HEAD_DIM // 4


def _bdot(a, b):
    return jnp.dot(a, b, preferred_element_type=F32)


def _split_bf16(a):
    hi = a.astype(BF16)
    lo = (a - hi.astype(F32)).astype(BF16)
    return hi, lo


def _layer_norm(r):
    mu = jnp.mean(r, axis=-1, keepdims=True)
    d = r - mu
    var = jnp.mean(d * d, axis=-1, keepdims=True)
    return d * lax.rsqrt(var + LN_EPS)


def _params(*sem, flags=None):
    return pltpu.CompilerParams(dimension_semantics=sem, vmem_limit_bytes=VMEM_LIMIT_BYTES, flags=flags)


def _start_all(copies):
    for cp in copies:
        cp.start()


def _wait_all(copies):
    for cp in copies:
        cp.wait()


ADA_ROWS = 16
ADA_TILE = 1536


def _ada_kernel(c_ref, w_ref, b_ref, o_ref):
    act = jax.nn.silu(c_ref[...])
    ah, al = _split_bf16(act)
    wh, wl = _split_bf16(w_ref[...])
    o_ref[...] = _bdot(ah, wh) + _bdot(ah, wl) + _bdot(al, wh) + b_ref[...]


def _ada_call(cin, w_ada, b_ada):
    depth, d, n = w_ada.shape
    return pl.pallas_call(
        _ada_kernel,
        grid=(depth, n // ADA_TILE),
        in_specs=[
            pl.BlockSpec((ADA_ROWS, d), lambda l, j: (0, 0)),
            pl.BlockSpec((None, d, ADA_TILE), lambda l, j: (l, 0, j)),
            pl.BlockSpec((None, 1, ADA_TILE), lambda l, j: (l, 0, j)),
        ],
        out_specs=pl.BlockSpec((None, ADA_ROWS, ADA_TILE), lambda l, j: (l, 0, j)),
        out_shape=jax.ShapeDtypeStruct((depth, ADA_ROWS, n), F32),
        compiler_params=_params("parallel", "parallel"),
        name="ada_modulation",
    )(cin, w_ada, b_ada.reshape(depth, 1, n))


def _inproj_kernel(x_ref, xp_ref, xn_ref, sc_ref, sh_ref, wf_ref, cos_ref, sa_ref, sb_ref,
                   cw_ref, gw_ref, gb_ref,
                   q_ref, kv_ref, y_ref, w_ref, *, ts):
    t = pl.program_id(1)
    nt = pl.num_programs(1)

    @pl.when((pl.program_id(0) == 0) & (t == 0))
    def _():
        w_ref[...] = wf_ref[...].astype(BF16)

    sc = 1.0 + sc_ref[...]
    sh = sh_ref[...]
    hx = (x_ref[...] * sc + sh).astype(BF16)

    pq = _bdot(hx, w_ref[:, 0:QKV_WIDTH])
    cos = cos_ref[...]
    sa = sa_ref[...]
    sb = sb_ref[...]

    def rope(z):
        return (z * cos + pltpu.roll(z, ROPE_HALF_PAIR, 1) * sa
                + pltpu.roll(z, LANES - ROPE_HALF_PAIR, 1) * sb)

    for j in range(ATTN_WIDTH // LANES):
        sl = slice(j * LANES, (j + 1) * LANES)
        q_ref[:, sl] = (rope(pq[:, sl]) * (ATTN_SCALE * LOG2E)).astype(BF16)
    kr = rope(pq[:, ATTN_WIDTH:ATTN_WIDTH + KV_WIDTH])
    vv = pq[:, ATTN_WIDTH + KV_WIDTH:QKV_WIDTH]
    for j, part in enumerate((kr, pltpu.roll(kr, HEAD_DIM, 1), vv, pltpu.roll(vv, HEAD_DIM, 1))):
        kv_ref[:, j * KV_WIDTH:(j + 1) * KV_WIDTH] = part.astype(BF16)

    pm = _bdot(hx, w_ref[:, QKV_WIDTH:IN_WIDTH])
    cb = pm[:, 0:CONV_WIDTH]
    u = pm[:, CONV_WIDTH:2 * CONV_WIDTH] * pm[:, 2 * CONV_WIDTH:3 * CONV_WIDTH]

    halo = jnp.concatenate([xp_ref[...], xn_ref[...]], axis=0)
    hh = (halo * sc + sh).astype(BF16)
    ph = _bdot(hh, w_ref[:, QKV_WIDTH + CONV_WIDTH:QKV_WIDTH + 3 * CONV_WIDTH])
    uh = ph[:, 0:CONV_WIDTH] * ph[:, CONV_WIDTH:2 * CONV_WIDTH]
    up_row = jnp.where(t > 0, uh[SUBLANES - 1:SUBLANES, :], 0.0)
    dn_row = jnp.where(t < nt - 1, uh[SUBLANES:SUBLANES + 1, :], 0.0)
    ridx = lax.broadcasted_iota(jnp.int32, (ts, CONV_WIDTH), 0)
    u_up = jnp.where(ridx == 0, up_row, pltpu.roll(u, 1, 0))
    u_dn = jnp.where(ridx == ts - 1, dn_row, pltpu.roll(u, ts - 1, 0))
    cw = cw_ref[...]
    y_conv = cb * (u_up * cw[0:1, :] + u * cw[1:2, :] + u_dn * cw[2:3, :])
    y_ref[:, 0:CONV_WIDTH] = y_conv.astype(BF16)

    gu = jax.nn.gelu(pm[:, 3 * CONV_WIDTH:3 * CONV_WIDTH + GM_WIDTH])
    gv = _layer_norm(jax.nn.gelu(pm[:, 3 * CONV_WIDTH + GM_WIDTH:3 * CONV_WIDTH + 2 * GM_WIDTH])).astype(BF16)
    lane = lax.broadcasted_iota(jnp.int32, (CHUNK, LANES), 1)
    zero = jnp.zeros((CHUNK, LANES), BF16)
    for c in range(ts // CHUNK):
        rows = slice(c * CHUNK, (c + 1) * CHUNK)
        for j in range(GM_WIDTH // LANES):
            cols = slice(j * LANES, (j + 1) * LANES)
            vp = gv[rows, cols]
            s = (_bdot(gw_ref[2 * j], jnp.where(lane < GM_HEAD, vp, zero))
                 + _bdot(gw_ref[2 * j + 1], jnp.where(lane >= GM_HEAD, vp, zero))
                 + gb_ref[:, cols])
            y_ref[rows, CONV_WIDTH + j * LANES:CONV_WIDTH + (j + 1) * LANES] = (gu[rows, cols] * s).astype(BF16)


def _inproj_call(x2d, row_off, n_seq, seq_len, ts, sc, sh, w_in, layer, cos, sa, sb, conv_w, gm_w, gm_b):
    nt = seq_len // ts
    off_t = row_off // ts
    off_8 = row_off // SUBLANES
    last_8 = x2d.shape[0] // SUBLANES - 1
    per_tile_8 = ts // SUBLANES
    n_out = n_seq * seq_len
    d = D_MODEL

    def tile_idx(b, t):
        return (off_t + b * nt + t, 0)

    def prev_idx(b, t):
        return (jnp.maximum(off_8 + (b * nt + t) * per_tile_8 - 1, 0), 0)

    def next_idx(b, t):
        return (jnp.minimum(off_8 + (b * nt + t + 1) * per_tile_8, last_8), 0)

    def out_idx(b, t):
        return (b * nt + t, 0)

    vec = pl.BlockSpec((None, 1, d), lambda b, t: (b, 0, 0))
    rope_spec = pl.BlockSpec((ts, LANES), lambda b, t: (t, 0))
    return pl.pallas_call(
        functools.partial(_inproj_kernel, ts=ts),
        grid=(n_seq, nt),
        in_specs=[
            pl.BlockSpec((ts, d), tile_idx),
            pl.BlockSpec((SUBLANES, d), prev_idx),
            pl.BlockSpec((SUBLANES, d), next_idx),
            vec, vec,
            pl.BlockSpec((None, d, IN_WIDTH), lambda b, t: (layer, 0, 0)),
            rope_spec, rope_spec, rope_spec,
            pl.BlockSpec((3, CONV_WIDTH), lambda b, t: (0, 0)),
            pl.BlockSpec((GM_GROUPS, CHUNK, CHUNK), lambda b, t: (0, 0, 0)),
            pl.BlockSpec((CHUNK, GM_WIDTH), lambda b, t: (0, 0)),
        ],
        out_specs=[
            pl.BlockSpec((ts, ATTN_WIDTH), out_idx),
            pl.BlockSpec((ts, KV_PACK_WIDTH), out_idx),
            pl.BlockSpec((ts, LOCAL_WIDTH), out_idx),
        ],
        out_shape=[
            jax.ShapeDtypeStruct((n_out, ATTN_WIDTH), BF16),
            jax.ShapeDtypeStruct((n_out, KV_PACK_WIDTH), BF16),
            jax.ShapeDtypeStruct((n_out, LOCAL_WIDTH), BF16),
        ],
        scratch_shapes=[pltpu.VMEM((d, IN_WIDTH), BF16)],
        compiler_params=_params("arbitrary", "arbitrary"),
        name="in_projection",
    )(x2d, x2d, x2d, sc, sh, w_in, cos, sa, sb, conv_w, gm_w, gm_b)


def _attn_body(q, keys, keys_sw, vals, vals_sw, halo_bias, sink_ref, o_ref, tq):
    nk = keys.shape[0]
    lane = lax.broadcasted_iota(jnp.int32, (nk, LANES), 1)
    lo = lane < HEAD_DIM
    zero = jnp.zeros((nk, LANES), BF16)
    rid = lax.broadcasted_iota(jnp.int32, (2 * tq, 1), 0)
    out_lo = lax.broadcasted_iota(jnp.int32, (2 * tq, LANES), 1) < HEAD_DIM
    nt_dims = (((1,), (1,)), ((), ()))
    gqa = N_HEADS // N_KV_HEADS
    biases = [None] * (nk // LANES)
    if halo_bias is not None:
        assert tq == LANES
        biases[0], biases[2] = halo_bias
    for h in range(N_KV_HEADS):
        k_own, k_oth = (keys, keys_sw) if h == 0 else (keys_sw, keys)
        v_own, v_oth = (vals, vals_sw) if h == 0 else (vals_sw, vals)
        kz = jnp.concatenate([jnp.where(lo, k_own, zero), jnp.where(lo, zero, k_oth)], axis=0)
        vz = jnp.concatenate([jnp.where(lo, v_own, zero), jnp.where(lo, zero, v_oth)], axis=0)
        c0 = h * gqa * HEAD_DIM
        qs = jnp.concatenate([q[:, c0:c0 + LANES], q[:, c0 + LANES:c0 + 2 * LANES]], axis=0)
        s_all = lax.dot_general(qs, kz, nt_dims, preferred_element_type=F32)
        probs, inv_den = [], []
        for par in range(2):
            tiles = []
            for j, bias in enumerate(biases):
                tile = s_all[:, par * nk + j * LANES:par * nk + (j + 1) * LANES]
                tiles.append(tile if bias is None else tile + bias)
            sink = jnp.where(rid < tq, sink_ref[h * gqa + par], sink_ref[h * gqa + 2 + par]) * LOG2E
            tile_max = tiles[0]
            for tile in tiles[1:]:
                tile_max = jnp.maximum(tile_max, tile)
            m = jnp.maximum(jnp.max(tile_max, axis=-1, keepdims=True), sink)
            tile_sum = None
            for tile in tiles:
                p = jnp.exp2(tile - m)
                tile_sum = p if tile_sum is None else tile_sum + p
                probs.append(p.astype(BF16))
            inv_den.append(1.0 / (jnp.sum(tile_sum, axis=-1, keepdims=True) + jnp.exp2(sink - m)))
        o = _bdot(jnp.concatenate(probs, axis=1), vz) * jnp.where(out_lo, inv_den[0], inv_den[1])
        o_ref[:, c0:c0 + LANES] = o[0:tq].astype(BF16)
        o_ref[:, c0 + LANES:c0 + 2 * LANES] = o[tq:2 * tq].astype(BF16)


def _split_kv(kv):
    return tuple(kv[:, j * KV_WIDTH:(j + 1) * KV_WIDTH] for j in range(4))


def _win_attn_kernel(sink_ref, q_ref, kvp_ref, kvc_ref, kvn_ref, kvx_ref, o_ref, *, tq):
    n = pl.program_id(1)
    nb = pl.num_programs(1)
    row = lax.broadcasted_iota(jnp.int32, (tq, tq), 0)
    col = lax.broadcasted_iota(jnp.int32, (tq, tq), 1)
    b_prev = jnp.where((col >= row) & (n > 0), 0.0, NEG_INF)
    b_next = jnp.where((col <= row) & (n < nb - 1), 0.0, NEG_INF)
    halo_bias = (jnp.concatenate([b_prev, b_prev], axis=0), jnp.concatenate([b_next, b_next], axis=0))

    kv = jnp.concatenate([kvp_ref[...], kvc_ref[...], kvn_ref[...], kvx_ref[...]], axis=0)
    _attn_body(q_ref[...], *_split_kv(kv), halo_bias, sink_ref, o_ref, tq)


def _win_attn_call(sink, q, kv, kv_ctx, n_seq, seq_len, ctx_len):
    tq = WINDOW
    nb = seq_len // tq

    def cur(b, n):
        return (b * nb + n, 0)

    def prev(b, n):
        return (b * nb + jnp.maximum(n - 1, 0), 0)

    def nxt(b, n):
        return (b * nb + jnp.minimum(n + 1, nb - 1), 0)

    return pl.pallas_call(
        functools.partial(_win_attn_kernel, tq=tq),
        grid=(n_seq, nb),
        in_specs=[pl.BlockSpec(memory_space=pltpu.SMEM), pl.BlockSpec((tq, ATTN_WIDTH), cur),
                  pl.BlockSpec((tq, KV_PACK_WIDTH), prev), pl.BlockSpec((tq, KV_PACK_WIDTH), cur),
                  pl.BlockSpec((tq, KV_PACK_WIDTH), nxt),
                  pl.BlockSpec((ctx_len, KV_PACK_WIDTH), lambda b, n: (b, 0))],
        out_specs=pl.BlockSpec((tq, ATTN_WIDTH), cur),
        out_shape=jax.ShapeDtypeStruct((n_seq * seq_len, ATTN_WIDTH), BF16),
        compiler_params=_params("parallel", "parallel"),
        name="window_attention",
    )(sink, q, kv, kv, kv, kv_ctx)


def _ctx_attn_kernel(sink_ref, q_ref, kv_ref, o_ref, *, tq):
    _attn_body(q_ref[...], *_split_kv(kv_ref[...]), None, sink_ref, o_ref, tq)


def _ctx_attn_call(sink, q, kv, n_seq, ctx_len):
    def blk(w):
        return pl.BlockSpec((ctx_len, w), lambda b: (b, 0))

    return pl.pallas_call(
        functools.partial(_ctx_attn_kernel, tq=ctx_len),
        grid=(n_seq,),
        in_specs=[pl.BlockSpec(memory_space=pltpu.SMEM), blk(ATTN_WIDTH), blk(KV_PACK_WIDTH)],
        out_specs=blk(ATTN_WIDTH),
        out_shape=jax.ShapeDtypeStruct((n_seq * ctx_len, ATTN_WIDTH), BF16),
        compiler_params=_params("parallel"),
        name="context_attention",
    )(sink, q, kv)


OUT_TILE = 512


def _outproj_kernel(attx_ref, attc_ref, yx_ref, yc_ref, xx_ref, xc_ref, wof_ref, g_ref, lng_ref, lnb_ref,
                    sc_ref, sh_ref, wr_ref, br_ref, before_ref,
                    x1_ref, h2v_ref, route_ref, routet_ref, cnt_ref, wo_ref, o_buf, lg_buf,
                    *, alpha, n_lat_tiles, n_tiles, has_ctx):
    step = pl.program_id(0)

    @pl.when(step == 0)
    def _():
        wo_ref[...] = wof_ref[...].astype(BF16)
        o_buf[1] = jnp.zeros(o_buf.shape[1:], F32)
        lg_buf[1] = jnp.zeros(lg_buf.shape[1:], F32)

    o = o_buf[(step + 1) % 2]
    lg = lg_buf[(step + 1) % 2]

    if has_ctx:
        att = jnp.where(jnp.minimum(step, n_tiles - 1) < n_lat_tiles, attx_ref[...], attc_ref[...])
        yloc = jnp.where(jnp.minimum(step, n_tiles - 1) < n_lat_tiles, yx_ref[...], yc_ref[...])
        xin = jnp.where(jnp.clip(step - 1, 0, n_tiles - 1) < n_lat_tiles, xx_ref[...], xc_ref[...])
    else:
        att, yloc, xin = attx_ref[...], yx_ref[...], xx_ref[...]
    o_buf[step % 2] = _bdot(att, wo_ref[0:ATTN_WIDTH, :]) + _bdot(yloc, wo_ref[ATTN_WIDTH:, :])
    x1 = _layer_norm(alpha * xin + g_ref[...] * o) * lng_ref[...] + lnb_ref[...]
    x1_ref[...] = x1
    h2 = x1 * (1.0 + sc_ref[...]) + sh_ref[...]
    half = D_MODEL // 2
    hi_bits = lax.bitcast_convert_type(h2[:, 0:half].astype(BF16).astype(F32), jnp.uint32)
    lo_bits = lax.bitcast_convert_type(h2[:, half:].astype(BF16).astype(F32), jnp.uint32)
    packed = hi_bits | (lo_bits >> 16)
    for c in range(PACKED_TILE_ROWS):
        h2v_ref[:, c, :, :] = packed[:, c * LANES:(c + 1) * LANES].reshape(h2.shape[0] // SUBLANES, SUBLANES, LANES)

    hh, hl = _split_bf16(h2)
    t1 = _bdot(hh, wr_ref[...])
    lg_buf[step % 2] = (t1[:, 0:ROUTER_COLS] + t1[:, ROUTER_COLS:] + _bdot(hl, wr_ref[:, 0:ROUTER_COLS])
                        + br_ref[...])

    ts = lg.shape[0]
    lane = lax.broadcasted_iota(jnp.int32, (ts, ROUTER_COLS), 1).astype(F32)
    big = jnp.float32(ROUTER_COLS)

    def top1(v):
        m = jnp.max(v, axis=-1, keepdims=True)
        return m, jnp.min(jnp.where(v == m, lane, big), axis=-1, keepdims=True)

    gl = jnp.where(lane < N_GROUPS, lg, NEG_INF)
    g_val, g_idx = top1(gl)
    lse = g_val + jnp.log(jnp.sum(jnp.exp(gl - g_val), axis=-1, keepdims=True))
    p_group = jnp.exp(g_val - lse)
    e_lo = N_GROUPS + EXP_PER_GROUP * g_idx
    el = jnp.where((lane >= e_lo) & (lane < e_lo + EXP_PER_GROUP), lg, NEG_INF)
    e1, l1 = top1(el)
    e2, l2 = top1(jnp.where(lane == l1, NEG_INF, el))
    z = jnp.exp(e2 - e1)
    gate1 = p_group / (1.0 + z)
    gate2 = p_group * z / (1.0 + z)
    x1id = l1 - N_GROUPS
    x2id = l2 - N_GROUPS

    sel1 = lane == x1id
    sel2 = lane == x2id
    onehot = jnp.where(sel1 | sel2, 1.0, 0.0)
    prefix = _bdot(before_ref[...], onehot.astype(BF16))
    rank1 = jnp.sum(jnp.where(sel1, prefix, 0.0), axis=-1, keepdims=True)
    rank2 = jnp.sum(jnp.where(sel2, prefix, 0.0), axis=-1, keepdims=True)
    route = jnp.zeros((ts, ROUTER_COLS), F32)
    for col, val in enumerate((x1id, x2id, gate1, gate2, rank1, rank2)):
        route = jnp.where(lane == col, val, route)
    route_ref[...] = route
    routet_ref[...] = route.T[0:SUBLANES, :]
    cnt_ref[...] =jnp.broadcast_to(jnp.sum(onehot, axis=0, keepdims=True), (SUBLANES, ROUTER_COLS))


def _outproj_call(att_x, att_c, y_x, y_c, x_arr, x_off, c_arr, c_off, n_lat, n_ctx, seq_len,
                  w_out, layer, gvecs, lng, lnb, scvecs, shvecs, wr, br, alpha):
    ts = OUT_TILE
    d = D_MODEL
    n_lat_tiles = n_lat // ts
    n_tiles = (n_lat + n_ctx) // ts
    tiles_per_seq = seq_len // ts
    n_seq = n_lat // seq_len
    xo, co = x_off // ts, c_off // ts

    def mm_tile(s):
        return jnp.minimum(s, n_tiles - 1)

    def ln_tile(s):
        return jnp.clip(s - 1, 0, n_tiles - 1)

    def route_tile(s):
        return jnp.clip(s - 2, 0, n_tiles - 1)

    def lat_loc(s):
        return (jnp.minimum(mm_tile(s), n_lat_tiles - 1), 0)

    def ctx_loc(s):
        return (jnp.maximum(mm_tile(s) - n_lat_tiles, 0), 0)

    def lat_in(s):
        return (xo + jnp.minimum(ln_tile(s), n_lat_tiles - 1), 0)

    def ctx_in(s):
        return (co + jnp.maximum(ln_tile(s) - n_lat_tiles, 0), 0)

    def vec_idx(s):
        i = ln_tile(s)
        return (jnp.where(i < n_lat_tiles, i // tiles_per_seq, n_seq), 0, 0)

    vecb = pl.BlockSpec((None, 1, d), vec_idx)
    vec0 = pl.BlockSpec((1, d), lambda i: (0, 0))
    return pl.pallas_call(
        functools.partial(_outproj_kernel, alpha=alpha, n_lat_tiles=n_lat_tiles, n_tiles=n_tiles,
                          has_ctx=n_ctx > 0),
        grid=(n_tiles + 2,),
        in_specs=[
            pl.BlockSpec((ts, ATTN_WIDTH), lat_loc), pl.BlockSpec((ts, ATTN_WIDTH), ctx_loc),
            pl.BlockSpec((ts, LOCAL_WIDTH), lat_loc), pl.BlockSpec((ts, LOCAL_WIDTH), ctx_loc),
            pl.BlockSpec((ts, d), lat_in), pl.BlockSpec((ts, d), ctx_in),
            pl.BlockSpec((None, d, d), lambda i: (layer, 0, 0)),
            vecb, vec0, vec0, vecb, vecb,
            pl.BlockSpec((d, 2 * ROUTER_COLS), lambda i: (0, 0)),
            pl.BlockSpec((1, ROUTER_COLS), lambda i: (0, 0)),
            pl.BlockSpec((ts, ts), lambda i: (0, 0)),
        ],
        out_specs=[pl.BlockSpec((ts, d), lambda s: (ln_tile(s), 0)),
                   pl.BlockSpec((ts // SUBLANES, PACKED_TILE_ROWS, SUBLANES, LANES),
                                lambda s: (ln_tile(s), 0, 0, 0)),
                   pl.BlockSpec((ts, ROUTER_COLS), lambda s: (route_tile(s), 0)),
                   pl.BlockSpec((None, SUBLANES, ts), lambda s: (route_tile(s), 0, 0)),
                   pl.BlockSpec((None, SUBLANES, ROUTER_COLS), lambda s: (route_tile(s), 0, 0))],
        out_shape=[jax.ShapeDtypeStruct((n_lat + n_ctx, d), F32),
                   jax.ShapeDtypeStruct(((n_lat + n_ctx) // SUBLANES, PACKED_TILE_ROWS, SUBLANES, LANES),
                                        jnp.uint32),
                   jax.ShapeDtypeStruct((n_lat + n_ctx, ROUTER_COLS), F32),
                   jax.ShapeDtypeStruct((n_tiles, SUBLANES, ts), F32),
                   jax.ShapeDtypeStruct((n_tiles, SUBLANES, ROUTER_COLS), F32)],
        scratch_shapes=[pltpu.VMEM((d, d), BF16), pltpu.VMEM((2, ts, d), F32),
                        pltpu.VMEM((2, ts, ROUTER_COLS), F32)],
        compiler_params=_params("arbitrary"),
        name="out_projection",
    )(att_x, att_c, y_x, y_c, x_arr, c_arr, w_out, gvecs, lng, lnb, scvecs, shvecs, wr, br,
      jnp.tril(jnp.ones((ts, ts), BF16), -1))


DISPATCH_TILE = 2048


def _dispatch_kernel(seg_ref, pos_ref, h_ref, xs_out, zbuf, zsem, sem):
    block_tiles = MOE_BLOCK // TOKENS_PER_PACKED_TILE

    @pl.when(pl.program_id(0) == 0)
    def _():
        zbuf[...] = jnp.zeros_like(zbuf)

        def zero_copy(e):
            first_tile = (seg_ref[0, e] - MOE_BLOCK) // TOKENS_PER_PACKED_TILE
            return pltpu.make_async_copy(zbuf, xs_out.at[pl.ds(first_tile, block_tiles)], zsem)

        for e in range(N_EXPERTS):
            @pl.when(seg_ref[1, e] > 0)
            def _():
                zero_copy(e).start()
        for e in range(N_EXPERTS):
            @pl.when(seg_ref[1, e] > 0)
            def _():
                zero_copy(e).wait()

        def tail_copy(b):
            return pltpu.make_async_copy(zbuf, xs_out.at[pl.ds(b * block_tiles, block_tiles)], zsem)

        n_blocks = xs_out.shape[0] // block_tiles
        first_unused = seg_ref[0, N_EXPERTS - 1] // MOE_BLOCK
        lax.fori_loop(first_unused, n_blocks, lambda b, c: (tail_copy(b).start(), c)[1], 0)
        lax.fori_loop(first_unused, n_blocks, lambda b, c: (tail_copy(b).wait(), c)[1], 0)

    def body(j, carry):
        for s in range(SUBLANES):
            t = SUBLANES * j + s
            for k in range(TOP_K):
                p = pos_ref[0, 0, k * DISPATCH_TILE + t]
                dst = xs_out.at[p >> 1, pl.ds(PACKED_TILE_ROWS * (p & 1), PACKED_TILE_ROWS), :]
                pltpu.make_async_copy(h_ref.at[j, :, s, :], dst, sem).start(priority=k)
        return carry

    lax.fori_loop(0, DISPATCH_TILE // SUBLANES, body, 0)
    for k in range(TOP_K):
        pltpu.make_async_copy(h_ref, h_ref, sem).wait()


def _dispatch_call(seg, pos, h2t, n_slots):
    n_tok = h2t.shape[0] * SUBLANES
    n_tiles = n_tok // DISPATCH_TILE
    grid_spec = pltpu.PrefetchScalarGridSpec(
        num_scalar_prefetch=1,
        grid=(n_tiles,),
        in_specs=[
            pl.BlockSpec((1, 1, TOP_K * DISPATCH_TILE), lambda i, seg: (i, 0, 0), memory_space=pltpu.SMEM),
            pl.BlockSpec((DISPATCH_TILE // SUBLANES, PACKED_TILE_ROWS, SUBLANES, LANES),
                         lambda i, seg: (i, 0, 0, 0)),
        ],
        out_specs=pl.BlockSpec(memory_space=pl.ANY),
        scratch_shapes=[pltpu.VMEM((MOE_BLOCK // TOKENS_PER_PACKED_TILE, SUBLANES, LANES), jnp.uint32),
                        pltpu.SemaphoreType.DMA(()), pltpu.SemaphoreType.DMA(())],
    )
    return pl.pallas_call(
        _dispatch_kernel,
        grid_spec=grid_spec,
        out_shape=jax.ShapeDtypeStruct((n_slots // TOKENS_PER_PACKED_TILE, SUBLANES, LANES), jnp.uint32),
        compiler_params=_params("arbitrary"),
        name="moe_dispatch",
    )(seg, pos, h2t)


def _moe_kernel(be_ref, ne_ref, nu_ref, xs_hbm, w1_hbm, w3_hbm, w2_hbm, ys_hbm, w1b, w3b, w2b, w1f, w3f, w2f,
                xbuf, obuf, isem, osem, wsem, *, layer):
    i = pl.program_id(0)
    n_used = nu_ref[0]
    slot = i % 2
    groups = MOE_BLOCK // SUBLANES
    half = D_MODEL // 2

    def in_copies(blk, at_slot):
        return [pltpu.make_async_copy(
            xs_hbm.at[pl.ds(blk * groups, groups), s // TOKENS_PER_PACKED_TILE,
                      pl.ds(PACKED_TILE_ROWS * (s % TOKENS_PER_PACKED_TILE), PACKED_TILE_ROWS), :],
            xbuf.at[at_slot, :, :, s, :], isem.at[at_slot]) for s in range(SUBLANES)]

    def out_copies(blk, at_slot):
        return [pltpu.make_async_copy(obuf.at[at_slot, :, :, s, :], ys_hbm.at[pl.ds(blk * groups, groups), s],
                                      osem.at[at_slot]) for s in range(SUBLANES)]

    def write_rows(at_slot, y):
        for c in range(ROW_TILE_ROWS):
            obuf[at_slot, :, c, :, :] = y[:, c * LANES:(c + 1) * LANES].reshape(groups, SUBLANES, LANES)

    @pl.when(i == 0)
    def _():
        _start_all(in_copies(0, 0))

    @pl.when(i + 1 < n_used)
    def _():
        _start_all(in_copies(i + 1, 1 - slot))

    def weight_copies(e):
        return [pltpu.make_async_copy(src.at[layer, e], dst, wsem)
                for src, dst in ((w1_hbm, w1f), (w3_hbm, w3f), (w2_hbm, w2f))]

    @pl.when(i == 0)
    def _():
        _start_all(weight_copies(be_ref[0]))

    @pl.when((i < n_used) & ((i == 0) | (be_ref[i] != be_ref[jnp.maximum(i - 1, 0)])))
    def _():
        _wait_all(weight_copies(be_ref[i]))
        w1b[...] = w1f[...].astype(BF16)
        w3b[...] = w3f[...].astype(BF16)
        w2b[...] = w2f[...].astype(BF16)

        @pl.when(ne_ref[i] != be_ref[i])
        def _():
            _start_all(weight_copies(ne_ref[i]))

    @pl.when(i < n_used)
    def _():
        _wait_all(in_copies(i, slot))
        p = jnp.concatenate([xbuf[slot, :, c, :, :].reshape(MOE_BLOCK, LANES) for c in range(PACKED_TILE_ROWS)],
                            axis=1)
        xa = lax.bitcast_convert_type(p & jnp.uint32(0xFFFF0000), F32).astype(BF16)
        xb = lax.bitcast_convert_type(p << 16, F32).astype(BF16)
        h1 = _bdot(xa, w1b[0:half, :]) + _bdot(xb, w1b[half:, :])
        h3 = _bdot(xa, w3b[0:half, :]) + _bdot(xb, w3b[half:, :])
        write_rows(slot, _bdot((jax.nn.silu(h1) * h3).astype(BF16), w2b[...]))
        _start_all(out_copies(i, slot))

        @pl.when(i >= 1)
        def _():
            _wait_all(out_copies(i - 1, 1 - slot))

        @pl.when(i == n_used - 1)
        def _():
            _wait_all(out_copies(i, slot))

    @pl.when(i >= n_used)
    def _():
        obuf[slot] = jnp.zeros(obuf.shape[1:], F32)
        _start_all(out_copies(i, slot))
        _wait_all(out_copies(i, slot))


def _moe_call(block_e, next_e, n_used, xs, w1, w3, w2, layer):
    n_blocks = block_e.shape[0]
    n_slots = n_blocks * MOE_BLOCK
    d = D_MODEL
    groups = MOE_BLOCK // SUBLANES
    grid_spec = pltpu.PrefetchScalarGridSpec(
        num_scalar_prefetch=3,
        grid=(n_blocks,),
        in_specs=[pl.BlockSpec(memory_space=pl.ANY)] * 4,
        out_specs=pl.BlockSpec(memory_space=pl.ANY),
        scratch_shapes=[pltpu.VMEM((d, D_EXPERT), BF16), pltpu.VMEM((d, D_EXPERT), BF16),
                        pltpu.VMEM((D_EXPERT, d), BF16),
                        pltpu.VMEM((d, D_EXPERT), F32), pltpu.VMEM((d, D_EXPERT), F32),
                        pltpu.VMEM((D_EXPERT, d), F32),
                        pltpu.VMEM((2, groups, PACKED_TILE_ROWS, SUBLANES, LANES), jnp.uint32),
                        pltpu.VMEM((2, groups, ROW_TILE_ROWS, SUBLANES, LANES), F32),
                        pltpu.SemaphoreType.DMA((2,)), pltpu.SemaphoreType.DMA((2,)),
                        pltpu.SemaphoreType.DMA(())],
    )
    xs_view = (n_slots // SUBLANES, SUBLANES // TOKENS_PER_PACKED_TILE, SUBLANES, LANES)
    ys_view = (n_slots // SUBLANES, SUBLANES, ROW_TILE_ROWS, LANES)
    ys = pl.pallas_call(
        functools.partial(_moe_kernel, layer=layer),
        grid_spec=grid_spec,
        out_shape=jax.ShapeDtypeStruct(ys_view, F32),
        compiler_params=_params("arbitrary"),
        name="moe_experts",
    )(block_e, next_e, n_used, xs.reshape(xs_view), w1, w3, w2)
    return ys.reshape(n_slots, ROW_TILE_ROWS, LANES)


COMBINE_TILE = 256


def _combine_kernel(pos0_ref, posn_ref, x1_ref, route_ref, g_ref, lng_ref, lnb_ref, ys_hbm, o_ref, ybuf, sem,
                    *, alpha):
    i = pl.program_id(0)
    nb = pl.num_programs(0)
    n_rows = TOP_K * COMBINE_TILE

    def issue(pos_ref, slot):
        def body(j, carry):
            for s in range(SUBLANES):
                pltpu.make_async_copy(ys_hbm.at[pos_ref[0, 0, SUBLANES * j + s]], ybuf.at[slot, j, :, s, :],
                                      sem.at[slot]).start(priority=s % N_DMA_PRIORITIES)
            return carry
        lax.fori_loop(0, n_rows // SUBLANES, body, 0)

    @pl.when(i == 0)
    def _():
        issue(pos0_ref, 0)

    @pl.when(i + 1 < nb)
    def _():
        issue(posn_ref, (i + 1) % 2)

    slot = i % 2
    pltpu.make_async_copy(ybuf.at[slot], ybuf.at[slot], sem.at[slot]).wait()
    yrows = jnp.concatenate([ybuf[slot, :, c, :, :].reshape(n_rows, LANES) for c in range(ROW_TILE_ROWS)], axis=1)
    route = route_ref[...]
    y = (yrows[0:COMBINE_TILE, :] * route[:, ROUTE_GATE:ROUTE_GATE + 1]
         + yrows[COMBINE_TILE:, :] * route[:, ROUTE_GATE + 1:ROUTE_GATE + 2])
    o_ref[...] = _layer_norm(alpha * x1_ref[...] + g_ref[...] * y) * lng_ref[...] + lnb_ref[...]


def _combine_call(pos, x1, route, gvecs, lng, lnb, ys, n_tok, tiles_per_seq, n_seq, alpha):
    d = D_MODEL
    n_tiles = n_tok // COMBINE_TILE
    n_lat_tiles = tiles_per_seq * n_seq

    def g_idx(i):
        return (jnp.where(i < n_lat_tiles, i // tiles_per_seq, n_seq), 0, 0)

    smem_blk = (1, 1, TOP_K * COMBINE_TILE)
    return pl.pallas_call(
        functools.partial(_combine_kernel, alpha=alpha),
        grid=(n_tiles,),
        in_specs=[
            pl.BlockSpec(smem_blk, lambda i: (0, 0, 0), memory_space=pltpu.SMEM),
            pl.BlockSpec(smem_blk, lambda i: (jnp.minimum(i + 1, n_tiles - 1), 0, 0), memory_space=pltpu.SMEM),
            pl.BlockSpec((COMBINE_TILE, d), lambda i: (i, 0)),
            pl.BlockSpec((COMBINE_TILE, ROUTER_COLS), lambda i: (i, 0)),
            pl.BlockSpec((None, 1, d), g_idx),
            pl.BlockSpec((1, d), lambda i: (0, 0)),
            pl.BlockSpec((1, d), lambda i: (0, 0)),
            pl.BlockSpec(memory_space=pl.ANY),
        ],
        out_specs=pl.BlockSpec((COMBINE_TILE, d), lambda i: (i, 0)),
        out_shape=jax.ShapeDtypeStruct((n_tok, d), F32),
        scratch_shapes=[pltpu.VMEM((2, TOP_K * COMBINE_TILE // SUBLANES, ROW_TILE_ROWS, SUBLANES, LANES), F32),
                        pltpu.SemaphoreType.DMA((2,))],
        compiler_params=_params("arbitrary"),
        name="moe_combine",
    )(pos, pos, x1, route, gvecs, lng, lnb, ys)


def _slots(route_t, tile_counts):
    n_tiles = tile_counts.shape[0]
    n = n_tiles * route_t.shape[2]
    tc = tile_counts.astype(jnp.int32)
    counts = jnp.sum(tc, axis=0)
    padded = (counts + MOE_BLOCK - 1) // MOE_BLOCK * MOE_BLOCK
    pends = jnp.cumsum(padded)
    base = (pends - padded)[None, :] + jnp.cumsum(tc, axis=0) - tc
    n_blocks = -(-n * TOP_K // MOE_BLOCK) + N_EXPERTS
    block_start = jnp.arange(n_blocks, dtype=jnp.int32) * MOE_BLOCK
    block_e = jnp.minimum(jnp.sum(pends[None, 0:N_EXPERTS] <= block_start[:, None], axis=1),
                          N_EXPERTS - 1).astype(jnp.int32)
    ids = jnp.arange(N_EXPERTS, dtype=jnp.int32)
    later = (ids[None, :] > ids[:, None]) & (padded[None, 0:N_EXPERTS] > 0)
    next_nonempty = jnp.min(jnp.where(later, ids[None, :], N_EXPERTS), axis=1)
    next_of = jnp.where(next_nonempty < N_EXPERTS, next_nonempty, ids)
    next_e = jnp.sum(jnp.where(block_e[:, None] == ids[None, :], next_of[None, :], 0), axis=1).astype(jnp.int32)
    n_used = (pends[N_EXPERTS - 1] // MOE_BLOCK).astype(jnp.int32).reshape(1)
    seg = jnp.stack([pends[0:N_EXPERTS], padded[0:N_EXPERTS]]).astype(jnp.int32)
    pos = []
    for k in range(TOP_K):
        e = route_t[:, ROUTE_EXPERT + k, :].astype(jnp.int32)
        hit = e[:, None, :] == jnp.arange(N_EXPERTS, dtype=jnp.int32)[None, :, None]
        b = jnp.sum(jnp.where(hit, base[:, 0:N_EXPERTS, None], 0), axis=1)
        pos.append((b + route_t[:, ROUTE_RANK + k, :].astype(jnp.int32)).reshape(n))
    return block_e, next_e, n_used, seg, pos


def _rope_tables(seq_len):
    m = HEAD_DIM // 4
    freqs = ROPE_BASE ** (-jnp.arange(m, dtype=F32) / m)
    t = jnp.arange(seq_len)
    row = (t // GRID_W).astype(F32)[:, None] * freqs[None, :]
    col = (t % GRID_W).astype(F32)[:, None] * freqs[None, :]
    cos = jnp.concatenate([jnp.cos(row), jnp.cos(row), jnp.cos(col), jnp.cos(col)], axis=-1)
    zero = jnp.zeros_like(row)
    sin_a = jnp.concatenate([zero, jnp.sin(row), zero, jnp.sin(col)], axis=-1)
    sin_b = jnp.concatenate([-jnp.sin(row), zero, -jnp.sin(col), zero], axis=-1)
    rep = LANES // HEAD_DIM
    return jnp.tile(cos, (1, rep)), jnp.tile(sin_a, (1, rep)), jnp.tile(sin_b, (1, rep))


def kernel(x, c, ctx, c_ctx, w_ada, b_ada, w_in, conv_w, attn_sink, gm_ws, gm_bs, w_out, ln1_g, ln1_b,
           w_rg, b_rg, w_re, b_re, w1, w3, w2, ln2_g, ln2_b):
    b_, s_, d_ = x.shape
    c_len = ctx.shape[1]
    depth = w_ada.shape[0]
    alpha = (2 * depth) ** 0.25
    n_lat = b_ * s_
    n_ctx = b_ * c_len
    ts = 1024

    cin = jnp.zeros((ADA_ROWS, d_), F32).at[0:b_].set(c).at[b_].set(c_ctx)
    mod = _ada_call(cin, w_ada, b_ada)

    cos, sin_a, sin_b = _rope_tables(s_)
    ones_c = jnp.ones((c_len, LANES), F32)
    zeros_c = jnp.zeros((c_len, LANES), F32)

    x_flat = x.reshape(n_lat, d_)
    c_flat = ctx.reshape(n_ctx, d_)
    x_off, c_off = 0, 0
    x_arr, c_arr = x_flat, c_flat

    for l in range(depth):
        last = l == depth - 1
        mx = mod[l, 0:b_].reshape(b_, 6, 1, d_)
        sh1, sc1, g1, sh2, sc2, g2 = (mx[:, i] for i in range(6))
        mc = jnp.broadcast_to(mod[l, b_].reshape(1, 6, 1, d_), (b_, 6, 1, d_))
        gm_w = gm_ws[l].astype(BF16)
        gm_b = jnp.repeat(gm_bs[l].T, GM_HEAD, axis=1)
        sink = attn_sink[l]

        qx, kvx, yx = _inproj_call(x_arr, x_off, b_, s_, ts, sc1, sh1, w_in, l, cos, sin_a, sin_b,
                                   conv_w[l], gm_w, gm_b)
        qc, kvc, yc = _inproj_call(c_arr, c_off, b_, c_len, c_len, mc[:, 1], mc[:, 0], w_in, l,
                                   ones_c, zeros_c, zeros_c, conv_w[l], gm_w, gm_b)
        att_x = _win_attn_call(sink, qx, kvx, kvc, b_, s_, c_len)

        w_r = jnp.zeros((d_, ROUTER_COLS), F32).at[:, 0:N_GROUPS].set(w_rg[l]) \
            .at[:, N_GROUPS:N_GROUPS + N_EXPERTS].set(w_re[l])
        w_r_hi = w_r.astype(BF16)
        w_r_lo = (w_r - w_r_hi.astype(F32)).astype(BF16)
        wr = jnp.concatenate([w_r_hi, w_r_lo], axis=1)
        br = jnp.zeros((1, ROUTER_COLS), F32).at[0, 0:N_GROUPS].set(b_rg[l]) \
            .at[0, N_GROUPS:N_GROUPS + N_EXPERTS].set(b_re[l])
        lng1, lnb1 = ln1_g[l].reshape(1, d_), ln1_b[l].reshape(1, d_)
        lng2, lnb2 = ln2_g[l].reshape(1, d_), ln2_b[l].reshape(1, d_)

        n_tot = n_lat if last else n_lat + n_ctx
        gvecs1 = jnp.concatenate([g1, mc[0:1, 2]], axis=0)
        scvecs2 = jnp.concatenate([sc2, mc[0:1, 4]], axis=0)
        shvecs2 = jnp.concatenate([sh2, mc[0:1, 3]], axis=0)
        if last:
            x1, h2t, route, route_t, tcnt = _outproj_call(att_x, att_x, yx, yx, x_arr, x_off, x_arr, x_off, n_lat, 0, s_,
                                                 w_out, l, gvecs1, lng1, lnb1, scvecs2, shvecs2, wr, br, alpha)
        else:
            att_c = _ctx_attn_call(sink, qc, kvc, b_, c_len)
            x1, h2t, route, route_t, tcnt = _outproj_call(att_x, att_c, yx, yc, x_arr, x_off, c_arr, c_off, n_lat, n_ctx,
                                                 s_, w_out, l, gvecs1, lng1, lnb1, scvecs2, shvecs2, wr, br, alpha)

        block_e, next_e, n_used, seg, pos = _slots(route_t, tcnt[:, 0, :])
        n_slots = block_e.shape[0] * MOE_BLOCK

        def per_tile(tile):
            return jnp.concatenate([p.reshape(n_tot // tile, 1, tile) for p in pos], axis=2)

        xs = _dispatch_call(seg, per_tile(DISPATCH_TILE), h2t, n_slots)
        ys = _moe_call(block_e, next_e, n_used, xs, w1, w3, w2, l)

        gvecs = jnp.concatenate([g2, mc[0:1, 5]], axis=0)
        out = _combine_call(per_tile(COMBINE_TILE), x1, route, gvecs, lng2, lnb2, ys, n_tot, s_ // COMBINE_TILE,
                            b_, alpha)
        x_arr, x_off = out, 0
        c_arr, c_off = out, n_lat

    return x_arr.reshape(b_, s_, d_)
```

```python
import functools

import jax
import jax.numpy as jnp
from jax import lax
from jax.experimental import pallas as pl
from jax.experimental.pallas import tpu as pltpu

F32 = jnp.float32
BF16 = jnp.bfloat16

D_MODEL = 1024
GRID_W = 64
HEAD_DIM = 64
N_HEADS = 8
N_KV_HEADS = 2
ATTN_WIDTH = N_HEADS * HEAD_DIM
KV_WIDTH = N_KV_HEADS * HEAD_DIM
WINDOW = 128
ATTN_SCALE = HEAD_DIM ** -0.5
LOG2E = 1.4426950408889634
ROPE_BASE = 10000.0
CONV_WIDTH = D_MODEL // 4
GM_WIDTH = D_MODEL // 4
GM_GROUPS = 4
GM_HEAD = GM_WIDTH // GM_GROUPS
CHUNK = 128
LOCAL_WIDTH = CONV_WIDTH + GM_WIDTH
IN_WIDTH = ATTN_WIDTH + 2 * KV_WIDTH + 3 * CONV_WIDTH + 2 * GM_WIDTH
QKV_WIDTH = ATTN_WIDTH + 2 * KV_WIDTH
KV_PACK_WIDTH = 4 * KV_WIDTH
N_GROUPS = 4
EXP_PER_GROUP = 8
N_EXPERTS = N_GROUPS * EXP_PER_GROUP
TOP_K = 2
D_EXPERT = D_MODEL // 2
MOE_BLOCK = 512
LN_EPS = 1e-6
NEG_INF = -1e30

LANES = 128
SUBLANES = 8
VMEM_LIMIT_BYTES = 48 * 1024 * 1024
N_DMA_PRIORITIES = 2

ROW_TILE_ROWS = D_MODEL // LANES
assert ROW_TILE_ROWS == SUBLANES
PACKED_TILE_ROWS = ROW_TILE_ROWS // 2
TOKENS_PER_PACKED_TILE = SUBLANES // PACKED_TILE_ROWS
assert TOKENS_PER_PACKED_TILE == 2

ROUTER_COLS = LANES
ROUTE_EXPERT = 0
ROUTE_GATE = TOP_K
ROUTE_RANK = 2 * TOP_K
ROPE_HALF_PAIR = HEAD_DIM // 4


def _bdot(a, b):
    return jnp.dot(a, b, preferred_element_type=F32)


def _split_bf16(a):
    hi = a.astype(BF16)
    lo = (a - hi.astype(F32)).astype(BF16)
    return hi, lo


def _layer_norm(r):
    mu = jnp.mean(r, axis=-1, keepdims=True)
    d = r - mu
    var = jnp.mean(d * d, axis=-1, keepdims=True)
    return d * lax.rsqrt(var + LN_EPS)


def _params(*sem, flags=None):
    return pltpu.CompilerParams(dimension_semantics=sem, vmem_limit_bytes=VMEM_LIMIT_BYTES, flags=flags)


def _start_all(copies):
    for cp in copies:
        cp.start()


def _wait_all(copies):
    for cp in copies:
        cp.wait()


ADA_ROWS = 16
ADA_TILE = 1536


def _ada_kernel(c_ref, w_ref, b_ref, o_ref):
    act = jax.nn.silu(c_ref[...])
    ah, al = _split_bf16(act)
    wh, wl = _split_bf16(w_ref[...])
    o_ref[...] = _bdot(ah, wh) + _bdot(ah, wl) + _bdot(al, wh) + b_ref[...]


def _ada_call(cin, w_ada, b_ada):
    depth, d, n = w_ada.shape
    return pl.pallas_call(
        _ada_kernel,
        grid=(depth, n // ADA_TILE),
        in_specs=[
            pl.BlockSpec((ADA_ROWS, d), lambda l, j: (0, 0)),
            pl.BlockSpec((None, d, ADA_TILE), lambda l, j: (l, 0, j)),
            pl.BlockSpec((None, 1, ADA_TILE), lambda l, j: (l, 0, j)),
        ],
        out_specs=pl.BlockSpec((None, ADA_ROWS, ADA_TILE), lambda l, j: (l, 0, j)),
        out_shape=jax.ShapeDtypeStruct((depth, ADA_ROWS, n), F32),
        compiler_params=_params("parallel", "parallel"),
        name="ada_modulation",
    )(cin, w_ada, b_ada.reshape(depth, 1, n))


def _inproj_kernel(x_ref, xp_ref, xn_ref, sc_ref, sh_ref, wf_ref, cos_ref, sa_ref, sb_ref,
                   cw_ref, gw_ref, gb_ref,
                   q_ref, kv_ref, y_ref, w_ref, *, ts):
    t = pl.program_id(1)
    nt = pl.num_programs(1)

    @pl.when((pl.program_id(0) == 0) & (t == 0))
    def _():
        w_ref[...] = wf_ref[...].astype(BF16)

    sc = 1.0 + sc_ref[...]
    sh = sh_ref[...]
    hx = (x_ref[...] * sc + sh).astype(BF16)

    pq = _bdot(hx, w_ref[:, 0:QKV_WIDTH])
    cos = cos_ref[...]
    sa = sa_ref[...]
    sb = sb_ref[...]

    def rope(z):
        return (z * cos + pltpu.roll(z, ROPE_HALF_PAIR, 1) * sa
                + pltpu.roll(z, LANES - ROPE_HALF_PAIR, 1) * sb)

    for j in range(ATTN_WIDTH // LANES):
        sl = slice(j * LANES, (j + 1) * LANES)
        q_ref[:, sl] = (rope(pq[:, sl]) * (ATTN_SCALE * LOG2E)).astype(BF16)
    kr = rope(pq[:, ATTN_WIDTH:ATTN_WIDTH + KV_WIDTH])
    vv = pq[:, ATTN_WIDTH + KV_WIDTH:QKV_WIDTH]
    for j, part in enumerate((kr, pltpu.roll(kr, HEAD_DIM, 1), vv, pltpu.roll(vv, HEAD_DIM, 1))):
        kv_ref[:, j * KV_WIDTH:(j + 1) * KV_WIDTH] = part.astype(BF16)

    pm = _bdot(hx, w_ref[:, QKV_WIDTH:IN_WIDTH])
    cb = pm[:, 0:CONV_WIDTH]
    u = pm[:, CONV_WIDTH:2 * CONV_WIDTH] * pm[:, 2 * CONV_WIDTH:3 * CONV_WIDTH]

    halo = jnp.concatenate([xp_ref[...], xn_ref[...]], axis=0)
    hh = (halo * sc + sh).astype(BF16)
    ph = _bdot(hh, w_ref[:, QKV_WIDTH + CONV_WIDTH:QKV_WIDTH + 3 * CONV_WIDTH])
    uh = ph[:, 0:CONV_WIDTH] * ph[:, CONV_WIDTH:2 * CONV_WIDTH]
    up_row = jnp.where(t > 0, uh[SUBLANES - 1:SUBLANES, :], 0.0)
    dn_row = jnp.where(t < nt - 1, uh[SUBLANES:SUBLANES + 1, :], 0.0)
    ridx = lax.broadcasted_iota(jnp.int32, (ts, CONV_WIDTH), 0)
    u_up = jnp.where(ridx == 0, up_row, pltpu.roll(u, 1, 0))
    u_dn = jnp.where(ridx == ts - 1, dn_row, pltpu.roll(u, ts - 1, 0))
    cw = cw_ref[...]
    y_conv = cb * (u_up * cw[0:1, :] + u * cw[1:2, :] + u_dn * cw[2:3, :])
    y_ref[:, 0:CONV_WIDTH] = y_conv.astype(BF16)

    gu = jax.nn.gelu(pm[:, 3 * CONV_WIDTH:3 * CONV_WIDTH + GM_WIDTH])
    gv = _layer_norm(jax.nn.gelu(pm[:, 3 * CONV_WIDTH + GM_WIDTH:3 * CONV_WIDTH + 2 * GM_WIDTH])).astype(BF16)
    lane = lax.broadcasted_iota(jnp.int32, (CHUNK, LANES), 1)
    zero = jnp.zeros((CHUNK, LANES), BF16)
    for c in range(ts // CHUNK):
        rows = slice(c * CHUNK, (c + 1) * CHUNK)
        for j in range(GM_WIDTH // LANES):
            cols = slice(j * LANES, (j + 1) * LANES)
            vp = gv[rows, cols]
            s = (_bdot(gw_ref[2 * j], jnp.where(lane < GM_HEAD, vp, zero))
                 + _bdot(gw_ref[2 * j + 1], jnp.where(lane >= GM_HEAD, vp, zero))
                 + gb_ref[:, cols])
            y_ref[rows, CONV_WIDTH + j * LANES:CONV_WIDTH + (j + 1) * LANES] = (gu[rows, cols] * s).astype(BF16)


def _inproj_call(x2d, row_off, n_seq, seq_len, ts, sc, sh, w_in, layer, cos, sa, sb, conv_w, gm_w, gm_b):
    nt = seq_len // ts
    off_t = row_off // ts
    off_8 = row_off // SUBLANES
    last_8 = x2d.shape[0] // SUBLANES - 1
    per_tile_8 = ts // SUBLANES
    n_out = n_seq * seq_len
    d = D_MODEL

    def tile_idx(b, t):
        return (off_t + b * nt + t, 0)

    def prev_idx(b, t):
        return (jnp.maximum(off_8 + (b * nt + t) * per_tile_8 - 1, 0), 0)

    def next_idx(b, t):
        return (jnp.minimum(off_8 + (b * nt + t + 1) * per_tile_8, last_8), 0)

    def out_idx(b, t):
        return (b * nt + t, 0)

    vec = pl.BlockSpec((None, 1, d), lambda b, t: (b, 0, 0))
    rope_spec = pl.BlockSpec((ts, LANES), lambda b, t: (t, 0))
    return pl.pallas_call(
        functools.partial(_inproj_kernel, ts=ts),
        grid=(n_seq, nt),
        in_specs=[
            pl.BlockSpec((ts, d), tile_idx),
            pl.BlockSpec((SUBLANES, d), prev_idx),
            pl.BlockSpec((SUBLANES, d), next_idx),
            vec, vec,
            pl.BlockSpec((None, d, IN_WIDTH), lambda b, t: (layer, 0, 0)),
            rope_spec, rope_spec, rope_spec,
            pl.BlockSpec((3, CONV_WIDTH), lambda b, t: (0, 0)),
            pl.BlockSpec((GM_GROUPS, CHUNK, CHUNK), lambda b, t: (0, 0, 0)),
            pl.BlockSpec((CHUNK, GM_WIDTH), lambda b, t: (0, 0)),
        ],
        out_specs=[
            pl.BlockSpec((ts, ATTN_WIDTH), out_idx),
            pl.BlockSpec((ts, KV_PACK_WIDTH), out_idx),
            pl.BlockSpec((ts, LOCAL_WIDTH), out_idx),
        ],
        out_shape=[
            jax.ShapeDtypeStruct((n_out, ATTN_WIDTH), BF16),
            jax.ShapeDtypeStruct((n_out, KV_PACK_WIDTH), BF16),
            jax.ShapeDtypeStruct((n_out, LOCAL_WIDTH), BF16),
        ],
        scratch_shapes=[pltpu.VMEM((d, IN_WIDTH), BF16)],
        compiler_params=_params("arbitrary", "arbitrary"),
        name="in_projection",
    )(x2d, x2d, x2d, sc, sh, w_in, cos, sa, sb, conv_w, gm_w, gm_b)


def _attn_body(q, keys, keys_sw, vals, vals_sw, halo_bias, sink_ref, o_ref, tq):
    nk = keys.shape[0]
    lane = lax.broadcasted_iota(jnp.int32, (nk, LANES), 1)
    lo = lane < HEAD_DIM
    zero = jnp.zeros((nk, LANES), BF16)
    rid = lax.broadcasted_iota(jnp.int32, (2 * tq, 1), 0)
    out_lo = lax.broadcasted_iota(jnp.int32, (2 * tq, LANES), 1) < HEAD_DIM
    nt_dims = (((1,), (1,)), ((), ()))
    gqa = N_HEADS // N_KV_HEADS
    biases = [None] * (nk // LANES)
    if halo_bias is not None:
        assert tq == LANES
        biases[0], biases[2] = halo_bias
    for h in range(N_KV_HEADS):
        k_own, k_oth = (keys, keys_sw) if h == 0 else (keys_sw, keys)
        v_own, v_oth = (vals, vals_sw) if h == 0 else (vals_sw, vals)
        kz = jnp.concatenate([jnp.where(lo, k_own, zero), jnp.where(lo, zero, k_oth)], axis=0)
        vz = jnp.concatenate([jnp.where(lo, v_own, zero), jnp.where(lo, zero, v_oth)], axis=0)
        c0 = h * gqa * HEAD_DIM
        qs = jnp.concatenate([q[:, c0:c0 + LANES], q[:, c0 + LANES:c0 + 2 * LANES]], axis=0)
        s_all = lax.dot_general(qs, kz, nt_dims, preferred_element_type=F32)
        probs, inv_den = [], []
        for par in range(2):
            tiles = []
            for j, bias in enumerate(biases):
                tile = s_all[:, par * nk + j * LANES:par * nk + (j + 1) * LANES]
                tiles.append(tile if bias is None else tile + bias)
            sink = jnp.where(rid < tq, sink_ref[h * gqa + par], sink_ref[h * gqa + 2 + par]) * LOG2E
            tile_max = tiles[0]
            for tile in tiles[1:]:
                tile_max = jnp.maximum(tile_max, tile)
            m = jnp.maximum(jnp.max(tile_max, axis=-1, keepdims=True), sink)
            tile_sum = None
            for tile in tiles:
                p = jnp.exp2(tile - m)
                tile_sum = p if tile_sum is None else tile_sum + p
                probs.append(p.astype(BF16))
            inv_den.append(1.0 / (jnp.sum(tile_sum, axis=-1, keepdims=True) + jnp.exp2(sink - m)))
        o = _bdot(jnp.concatenate(probs, axis=1), vz) * jnp.where(out_lo, inv_den[0], inv_den[1])
        o_ref[:, c0:c0 + LANES] = o[0:tq].astype(BF16)
        o_ref[:, c0 + LANES:c0 + 2 * LANES] = o[tq:2 * tq].astype(BF16)


def _split_kv(kv):
    return tuple(kv[:, j * KV_WIDTH:(j + 1) * KV_WIDTH] for j in range(4))


def _win_attn_kernel(sink_ref, q_ref, kvp_ref, kvc_ref, kvn_ref, kvx_ref, o_ref, *, tq):
    n = pl.program_id(1)
    nb = pl.num_programs(1)
    row = lax.broadcasted_iota(jnp.int32, (tq, tq), 0)
    col = lax.broadcasted_iota(jnp.int32, (tq, tq), 1)
    b_prev = jnp.where((col >= row) & (n > 0), 0.0, NEG_INF)
    b_next = jnp.where((col <= row) & (n < nb - 1), 0.0, NEG_INF)
    halo_bias = (jnp.concatenate([b_prev, b_prev], axis=0), jnp.concatenate([b_next, b_next], axis=0))

    kv = jnp.concatenate([kvp_ref[...], kvc_ref[...], kvn_ref[...], kvx_ref[...]], axis=0)
    _attn_body(q_ref[...], *_split_kv(kv), halo_bias, sink_ref, o_ref, tq)


def _win_attn_call(sink, q, kv, kv_ctx, n_seq, seq_len, ctx_len):
    tq = WINDOW
    nb = seq_len // tq

    def cur(b, n):
        return (b * nb + n, 0)

    def prev(b, n):
        return (b * nb + jnp.maximum(n - 1, 0), 0)

    def nxt(b, n):
        return (b * nb + jnp.minimum(n + 1, nb - 1), 0)

    return pl.pallas_call(
        functools.partial(_win_attn_kernel, tq=tq),
        grid=(n_seq, nb),
        in_specs=[pl.BlockSpec(memory_space=pltpu.SMEM), pl.BlockSpec((tq, ATTN_WIDTH), cur),
                  pl.BlockSpec((tq, KV_PACK_WIDTH), prev), pl.BlockSpec((tq, KV_PACK_WIDTH), cur),
                  pl.BlockSpec((tq, KV_PACK_WIDTH), nxt),
                  pl.BlockSpec((ctx_len, KV_PACK_WIDTH), lambda b, n: (b, 0))],
        out_specs=pl.BlockSpec((tq, ATTN_WIDTH), cur),
        out_shape=jax.ShapeDtypeStruct((n_seq * seq_len, ATTN_WIDTH), BF16),
        compiler_params=_params("parallel", "parallel"),
        name="window_attention",
    )(sink, q, kv, kv, kv, kv_ctx)


def _ctx_attn_kernel(sink_ref, q_ref, kv_ref, o_ref, *, tq):
    _attn_body(q_ref[...], *_split_kv(kv_ref[...]), None, sink_ref, o_ref, tq)


def _ctx_attn_call(sink, q, kv, n_seq, ctx_len):
    def blk(w):
        return pl.BlockSpec((ctx_len, w), lambda b: (b, 0))

    return pl.pallas_call(
        functools.partial(_ctx_attn_kernel, tq=ctx_len),
        grid=(n_seq,),
        in_specs=[pl.BlockSpec(memory_space=pltpu.SMEM), blk(ATTN_WIDTH), blk(KV_PACK_WIDTH)],
        out_specs=blk(ATTN_WIDTH),
        out_shape=jax.ShapeDtypeStruct((n_seq * ctx_len, ATTN_WIDTH), BF16),
        compiler_params=_params("parallel"),
        name="context_attention",
    )(sink, q, kv)


OUT_TILE = 512


def _outproj_kernel(attx_ref, attc_ref, yx_ref, yc_ref, xx_ref, xc_ref, wof_ref, g_ref, lng_ref, lnb_ref,
                    sc_ref, sh_ref, wr_ref, br_ref, before_ref,
                    x1_ref, h2v_ref, route_ref, routet_ref, cnt_ref, wo_ref, o_buf, lg_buf,
                    *, alpha, n_lat_tiles, n_tiles, has_ctx):
    step = pl.program_id(0)

    @pl.when(step == 0)
    def _():
        wo_ref[...] = wof_ref[...].astype(BF16)
        o_buf[1] = jnp.zeros(o_buf.shape[1:], F32)
        lg_buf[1] = jnp.zeros(lg_buf.shape[1:], F32)

    o = o_buf[(step + 1) % 2]
    lg = lg_buf[(step + 1) % 2]

    if has_ctx:
        att = jnp.where(jnp.minimum(step, n_tiles - 1) < n_lat_tiles, attx_ref[...], attc_ref[...])
        yloc = jnp.where(jnp.minimum(step, n_tiles - 1) < n_lat_tiles, yx_ref[...], yc_ref[...])
        xin = jnp.where(jnp.clip(step - 1, 0, n_tiles - 1) < n_lat_tiles, xx_ref[...], xc_ref[...])
    else:
        att, yloc, xin = attx_ref[...], yx_ref[...], xx_ref[...]
    o_buf[step % 2] = _bdot(att, wo_ref[0:ATTN_WIDTH, :]) + _bdot(yloc, wo_ref[ATTN_WIDTH:, :])
    x1 = _layer_norm(alpha * xin + g_ref[...] * o) * lng_ref[...] + lnb_ref[...]
    x1_ref[...] = x1
    h2 = x1 * (1.0 + sc_ref[...]) + sh_ref[...]
    half = D_MODEL // 2
    hi_bits = lax.bitcast_convert_type(h2[:, 0:half].astype(BF16).astype(F32), jnp.uint32)
    lo_bits = lax.bitcast_convert_type(h2[:, half:].astype(BF16).astype(F32), jnp.uint32)
    packed = hi_bits | (lo_bits >> 16)
    for c in range(PACKED_TILE_ROWS):
        h2v_ref[:, c, :, :] = packed[:, c * LANES:(c + 1) * LANES].reshape(h2.shape[0] // SUBLANES, SUBLANES, LANES)

    hh, hl = _split_bf16(h2)
    t1 = _bdot(hh, wr_ref[...])
    lg_buf[step % 2] = (t1[:, 0:ROUTER_COLS] + t1[:, ROUTER_COLS:] + _bdot(hl, wr_ref[:, 0:ROUTER_COLS])
                        + br_ref[...])

    ts = lg.shape[0]
    lane = lax.broadcasted_iota(jnp.int32, (ts, ROUTER_COLS), 1).astype(F32)
    big = jnp.float32(ROUTER_COLS)

    def top1(v):
        m = jnp.max(v, axis=-1, keepdims=True)
        return m, jnp.min(jnp.where(v == m, lane, big), axis=-1, keepdims=True)

    gl = jnp.where(lane < N_GROUPS, lg, NEG_INF)
    g_val, g_idx = top1(gl)
    lse = g_val + jnp.log(jnp.sum(jnp.exp(gl - g_val), axis=-1, keepdims=True))
    p_group = jnp.exp(g_val - lse)
    e_lo = N_GROUPS + EXP_PER_GROUP * g_idx
    el = jnp.where((lane >= e_lo) & (lane < e_lo + EXP_PER_GROUP), lg, NEG_INF)
    e1, l1 = top1(el)
    e2, l2 = top1(jnp.where(lane == l1, NEG_INF, el))
    z = jnp.exp(e2 - e1)
    gate1 = p_group / (1.0 + z)
    gate2 = p_group * z / (1.0 + z)
    x1id = l1 - N_GROUPS
    x2id = l2 - N_GROUPS

    sel1 = lane == x1id
    sel2 = lane == x2id
    onehot = jnp.where(sel1 | sel2, 1.0, 0.0)
    prefix = _bdot(before_ref[...], onehot.astype(BF16))
    rank1 = jnp.sum(jnp.where(sel1, prefix, 0.0), axis=-1, keepdims=True)
    rank2 = jnp.sum(jnp.where(sel2, prefix, 0.0), axis=-1, keepdims=True)
    route = jnp.zeros((ts, ROUTER_COLS), F32)
    for col, val in enumerate((x1id, x2id, gate1, gate2, rank1, rank2)):
        route = jnp.where(lane == col, val, route)
    route_ref[...] = route
    routet_ref[...] = route.T[0:SUBLANES, :]
    cnt_ref[...] =jnp.broadcast_to(jnp.sum(onehot, axis=0, keepdims=True), (SUBLANES, ROUTER_COLS))


def _outproj_call(att_x, att_c, y_x, y_c, x_arr, x_off, c_arr, c_off, n_lat, n_ctx, seq_len,
                  w_out, layer, gvecs, lng, lnb, scvecs, shvecs, wr, br, alpha):
    ts = OUT_TILE
    d = D_MODEL
    n_lat_tiles = n_lat // ts
    n_tiles = (n_lat + n_ctx) // ts
    tiles_per_seq = seq_len // ts
    n_seq = n_lat // seq_len
    xo, co = x_off // ts, c_off // ts

    def mm_tile(s):
        return jnp.minimum(s, n_tiles - 1)

    def ln_tile(s):
        return jnp.clip(s - 1, 0, n_tiles - 1)

    def route_tile(s):
        return jnp.clip(s - 2, 0, n_tiles - 1)

    def lat_loc(s):
        return (jnp.minimum(mm_tile(s), n_lat_tiles - 1), 0)

    def ctx_loc(s):
        return (jnp.maximum(mm_tile(s) - n_lat_tiles, 0), 0)

    def lat_in(s):
        return (xo + jnp.minimum(ln_tile(s), n_lat_tiles - 1), 0)

    def ctx_in(s):
        return (co + jnp.maximum(ln_tile(s) - n_lat_tiles, 0), 0)

    def vec_idx(s):
        i = ln_tile(s)
        return (jnp.where(i < n_lat_tiles, i // tiles_per_seq, n_seq), 0, 0)

    vecb = pl.BlockSpec((None, 1, d), vec_idx)
    vec0 = pl.BlockSpec((1, d), lambda i: (0, 0))
    return pl.pallas_call(
        functools.partial(_outproj_kernel, alpha=alpha, n_lat_tiles=n_lat_tiles, n_tiles=n_tiles,
                          has_ctx=n_ctx > 0),
        grid=(n_tiles + 2,),
        in_specs=[
            pl.BlockSpec((ts, ATTN_WIDTH), lat_loc), pl.BlockSpec((ts, ATTN_WIDTH), ctx_loc),
            pl.BlockSpec((ts, LOCAL_WIDTH), lat_loc), pl.BlockSpec((ts, LOCAL_WIDTH), ctx_loc),
            pl.BlockSpec((ts, d), lat_in), pl.BlockSpec((ts, d), ctx_in),
            pl.BlockSpec((None, d, d), lambda i: (layer, 0, 0)),
            vecb, vec0, vec0, vecb, vecb,
            pl.BlockSpec((d, 2 * ROUTER_COLS), lambda i: (0, 0)),
            pl.BlockSpec((1, ROUTER_COLS), lambda i: (0, 0)),
            pl.BlockSpec((ts, ts), lambda i: (0, 0)),
        ],
        out_specs=[pl.BlockSpec((ts, d), lambda s: (ln_tile(s), 0)),
                   pl.BlockSpec((ts // SUBLANES, PACKED_TILE_ROWS, SUBLANES, LANES),
                                lambda s: (ln_tile(s), 0, 0, 0)),
                   pl.BlockSpec((ts, ROUTER_COLS), lambda s: (route_tile(s), 0)),
                   pl.BlockSpec((None, SUBLANES, ts), lambda s: (route_tile(s), 0, 0)),
                   pl.BlockSpec((None, SUBLANES, ROUTER_COLS), lambda s: (route_tile(s), 0, 0))],
        out_shape=[jax.ShapeDtypeStruct((n_lat + n_ctx, d), F32),
                   jax.ShapeDtypeStruct(((n_lat + n_ctx) // SUBLANES, PACKED_TILE_ROWS, SUBLANES, LANES),
                                        jnp.uint32),
                   jax.ShapeDtypeStruct((n_lat + n_ctx, ROUTER_COLS), F32),
                   jax.ShapeDtypeStruct((n_tiles, SUBLANES, ts), F32),
                   jax.ShapeDtypeStruct((n_tiles, SUBLANES, ROUTER_COLS), F32)],
        scratch_shapes=[pltpu.VMEM((d, d), BF16), pltpu.VMEM((2, ts, d), F32),
                        pltpu.VMEM((2, ts, ROUTER_COLS), F32)],
        compiler_params=_params("arbitrary"),
        name="out_projection",
    )(att_x, att_c, y_x, y_c, x_arr, c_arr, w_out, gvecs, lng, lnb, scvecs, shvecs, wr, br,
      jnp.tril(jnp.ones((ts, ts), BF16), -1))


DISPATCH_TILE = 2048


def _dispatch_kernel(seg_ref, pos_ref, h_ref, xs_out, zbuf, zsem, sem):
    block_tiles = MOE_BLOCK // TOKENS_PER_PACKED_TILE

    @pl.when(pl.program_id(0) == 0)
    def _():
        zbuf[...] = jnp.zeros_like(zbuf)

        def zero_copy(e):
            first_tile = (seg_ref[0, e] - MOE_BLOCK) // TOKENS_PER_PACKED_TILE
            return pltpu.make_async_copy(zbuf, xs_out.at[pl.ds(first_tile, block_tiles)], zsem)

        for e in range(N_EXPERTS):
            @pl.when(seg_ref[1, e] > 0)
            def _():
                zero_copy(e).start()
        for e in range(N_EXPERTS):
            @pl.when(seg_ref[1, e] > 0)
            def _():
                zero_copy(e).wait()

        def tail_copy(b):
            return pltpu.make_async_copy(zbuf, xs_out.at[pl.ds(b * block_tiles, block_tiles)], zsem)

        n_blocks = xs_out.shape[0] // block_tiles
        first_unused = seg_ref[0, N_EXPERTS - 1] // MOE_BLOCK
        lax.fori_loop(first_unused, n_blocks, lambda b, c: (tail_copy(b).start(), c)[1], 0)
        lax.fori_loop(first_unused, n_blocks, lambda b, c: (tail_copy(b).wait(), c)[1], 0)

    def body(j, carry):
        for s in range(SUBLANES):
            t = SUBLANES * j + s
            for k in range(TOP_K):
                p = pos_ref[0, 0, k * DISPATCH_TILE + t]
                dst = xs_out.at[p >> 1, pl.ds(PACKED_TILE_ROWS * (p & 1), PACKED_TILE_ROWS), :]
                pltpu.make_async_copy(h_ref.at[j, :, s, :], dst, sem).start(priority=k)
        return carry

    lax.fori_loop(0, DISPATCH_TILE // SUBLANES, body, 0)
    for k in range(TOP_K):
        pltpu.make_async_copy(h_ref, h_ref, sem).wait()


def _dispatch_call(seg, pos, h2t, n_slots):
    n_tok = h2t.shape[0] * SUBLANES
    n_tiles = n_tok // DISPATCH_TILE
    grid_spec = pltpu.PrefetchScalarGridSpec(
        num_scalar_prefetch=1,
        grid=(n_tiles,),
        in_specs=[
            pl.BlockSpec((1, 1, TOP_K * DISPATCH_TILE), lambda i, seg: (i, 0, 0), memory_space=pltpu.SMEM),
            pl.BlockSpec((DISPATCH_TILE // SUBLANES, PACKED_TILE_ROWS, SUBLANES, LANES),
                         lambda i, seg: (i, 0, 0, 0)),
        ],
        out_specs=pl.BlockSpec(memory_space=pl.ANY),
        scratch_shapes=[pltpu.VMEM((MOE_BLOCK // TOKENS_PER_PACKED_TILE, SUBLANES, LANES), jnp.uint32),
                        pltpu.SemaphoreType.DMA(()), pltpu.SemaphoreType.DMA(())],
    )
    return pl.pallas_call(
        _dispatch_kernel,
        grid_spec=grid_spec,
        out_shape=jax.ShapeDtypeStruct((n_slots // TOKENS_PER_PACKED_TILE, SUBLANES, LANES), jnp.uint32),
        compiler_params=_params("arbitrary"),
        name="moe_dispatch",
    )(seg, pos, h2t)


def _moe_kernel(be_ref, ne_ref, nu_ref, xs_hbm, w1_hbm, w3_hbm, w2_hbm, ys_hbm, w1b, w3b, w2b, w1f, w3f, w2f,
                xbuf, obuf, isem, osem, wsem, *, layer):
    i = pl.program_id(0)
    n_used = nu_ref[0]
    slot = i % 2
    groups = MOE_BLOCK // SUBLANES
    half = D_MODEL // 2

    def in_copies(blk, at_slot):
        return [pltpu.make_async_copy(
            xs_hbm.at[pl.ds(blk * groups, groups), s // TOKENS_PER_PACKED_TILE,
                      pl.ds(PACKED_TILE_ROWS * (s % TOKENS_PER_PACKED_TILE), PACKED_TILE_ROWS), :],
            xbuf.at[at_slot, :, :, s, :], isem.at[at_slot]) for s in range(SUBLANES)]

    def out_copies(blk, at_slot):
        return [pltpu.make_async_copy(obuf.at[at_slot, :, :, s, :], ys_hbm.at[pl.ds(blk * groups, groups), s],
                                      osem.at[at_slot]) for s in range(SUBLANES)]

    def write_rows(at_slot, y):
        for c in range(ROW_TILE_ROWS):
            obuf[at_slot, :, c, :, :] = y[:, c * LANES:(c + 1) * LANES].reshape(groups, SUBLANES, LANES)

    @pl.when(i == 0)
    def _():
        _start_all(in_copies(0, 0))

    @pl.when(i + 1 < n_used)
    def _():
        _start_all(in_copies(i + 1, 1 - slot))

    def weight_copies(e):
        return [pltpu.make_async_copy(src.at[layer, e], dst, wsem)
                for src, dst in ((w1_hbm, w1f), (w3_hbm, w3f), (w2_hbm, w2f))]

    @pl.when(i == 0)
    def _():
        _start_all(weight_copies(be_ref[0]))

    @pl.when((i < n_used) & ((i == 0) | (be_ref[i] != be_ref[jnp.maximum(i - 1, 0)])))
    def _():
        _wait_all(weight_copies(be_ref[i]))
        w1b[...] = w1f[...].astype(BF16)
        w3b[...] = w3f[...].astype(BF16)
        w2b[...] = w2f[...].astype(BF16)

        @pl.when(ne_ref[i] != be_ref[i])
        def _():
            _start_all(weight_copies(ne_ref[i]))

    @pl.when(i < n_used)
    def _():
        _wait_all(in_copies(i, slot))
        p = jnp.concatenate([xbuf[slot, :, c, :, :].reshape(MOE_BLOCK, LANES) for c in range(PACKED_TILE_ROWS)],
                            axis=1)
        xa = lax.bitcast_convert_type(p & jnp.uint32(0xFFFF0000), F32).astype(BF16)
        xb = lax.bitcast_convert_type(p << 16, F32).astype(BF16)
        h1 = _bdot(xa, w1b[0:half, :]) + _bdot(xb, w1b[half:, :])
        h3 = _bdot(xa, w3b[0:half, :]) + _bdot(xb, w3b[half:, :])
        write_rows(slot, _bdot((jax.nn.silu(h1) * h3).astype(BF16), w2b[...]))
        _start_all(out_copies(i, slot))

        @pl.when(i >= 1)
        def _():
            _wait_all(out_copies(i - 1, 1 - slot))

        @pl.when(i == n_used - 1)
        def _():
            _wait_all(out_copies(i, slot))

    @pl.when(i == n_used)
    def _():
        obuf[0] = jnp.zeros(obuf.shape[1:], F32)

    @pl.when(i >= n_used)
    def _():
        _start_all(out_copies(i, 0))

    @pl.when(i == pl.num_programs(0) - 1)
    def _():
        lax.fori_loop(n_used, pl.num_programs(0), lambda b, c: (_wait_all(out_copies(b, 0)), c)[1], 0)


def _moe_call(block_e, next_e, n_used, xs, w1, w3, w2, layer):
    n_blocks = block_e.shape[0]
    n_slots = n_blocks * MOE_BLOCK
    d = D_MODEL
    groups = MOE_BLOCK // SUBLANES
    grid_spec = pltpu.PrefetchScalarGridSpec(
        num_scalar_prefetch=3,
        grid=(n_blocks,),
        in_specs=[pl.BlockSpec(memory_space=pl.ANY)] * 4,
        out_specs=pl.BlockSpec(memory_space=pl.ANY),
        scratch_shapes=[pltpu.VMEM((d, D_EXPERT), BF16), pltpu.VMEM((d, D_EXPERT), BF16),
                        pltpu.VMEM((D_EXPERT, d), BF16),
                        pltpu.VMEM((d, D_EXPERT), F32), pltpu.VMEM((d, D_EXPERT), F32),
                        pltpu.VMEM((D_EXPERT, d), F32),
                        pltpu.VMEM((2, groups, PACKED_TILE_ROWS, SUBLANES, LANES), jnp.uint32),
                        pltpu.VMEM((2, groups, ROW_TILE_ROWS, SUBLANES, LANES), F32),
                        pltpu.SemaphoreType.DMA((2,)), pltpu.SemaphoreType.DMA((2,)),
                        pltpu.SemaphoreType.DMA(())],
    )
    xs_view = (n_slots // SUBLANES, SUBLANES // TOKENS_PER_PACKED_TILE, SUBLANES, LANES)
    ys_view = (n_slots // SUBLANES, SUBLANES, ROW_TILE_ROWS, LANES)
    ys = pl.pallas_call(
        functools.partial(_moe_kernel, layer=layer),
        grid_spec=grid_spec,
        out_shape=jax.ShapeDtypeStruct(ys_view, F32),
        compiler_params=_params("arbitrary"),
        name="moe_experts",
    )(block_e, next_e, n_used, xs.reshape(xs_view), w1, w3, w2)
    return ys.reshape(n_slots, ROW_TILE_ROWS, LANES)


COMBINE_TILE = 256


def _combine_kernel(pos0_ref, posn_ref, x1_ref, route_ref, g_ref, lng_ref, lnb_ref, ys_hbm, o_ref, ybuf, sem,
                    *, alpha):
    i = pl.program_id(0)
    nb = pl.num_programs(0)
    n_rows = TOP_K * COMBINE_TILE

    def issue(pos_ref, slot):
        def body(j, carry):
            for s in range(SUBLANES):
                pltpu.make_async_copy(ys_hbm.at[pos_ref[0, 0, SUBLANES * j + s]], ybuf.at[slot, j, :, s, :],
                                      sem.at[slot]).start(priority=s % N_DMA_PRIORITIES)
            return carry
        lax.fori_loop(0, n_rows // SUBLANES, body, 0)

    @pl.when(i == 0)
    def _():
        issue(pos0_ref, 0)

    @pl.when(i + 1 < nb)
    def _():
        issue(posn_ref, (i + 1) % 2)

    slot = i % 2
    pltpu.make_async_copy(ybuf.at[slot], ybuf.at[slot], sem.at[slot]).wait()
    yrows = jnp.concatenate([ybuf[slot, :, c, :, :].reshape(n_rows, LANES) for c in range(ROW_TILE_ROWS)], axis=1)
    route = route_ref[...]
    y = (yrows[0:COMBINE_TILE, :] * route[:, ROUTE_GATE:ROUTE_GATE + 1]
         + yrows[COMBINE_TILE:, :] * route[:, ROUTE_GATE + 1:ROUTE_GATE + 2])
    o_ref[...] = _layer_norm(alpha * x1_ref[...] + g_ref[...] * y) * lng_ref[...] + lnb_ref[...]


def _combine_call(pos, x1, route, gvecs, lng, lnb, ys, n_tok, tiles_per_seq, n_seq, alpha):
    d = D_MODEL
    n_tiles = n_tok // COMBINE_TILE
    n_lat_tiles = tiles_per_seq * n_seq

    def g_idx(i):
        return (jnp.where(i < n_lat_tiles, i // tiles_per_seq, n_seq), 0, 0)

    smem_blk = (1, 1, TOP_K * COMBINE_TILE)
    return pl.pallas_call(
        functools.partial(_combine_kernel, alpha=alpha),
        grid=(n_tiles,),
        in_specs=[
            pl.BlockSpec(smem_blk, lambda i: (0, 0, 0), memory_space=pltpu.SMEM),
            pl.BlockSpec(smem_blk, lambda i: (jnp.minimum(i + 1, n_tiles - 1), 0, 0), memory_space=pltpu.SMEM),
            pl.BlockSpec((COMBINE_TILE, d), lambda i: (i, 0)),
            pl.BlockSpec((COMBINE_TILE, ROUTER_COLS), lambda i: (i, 0)),
            pl.BlockSpec((None, 1, d), g_idx),
            pl.BlockSpec((1, d), lambda i: (0, 0)),
            pl.BlockSpec((1, d), lambda i: (0, 0)),
            pl.BlockSpec(memory_space=pl.ANY),
        ],
        out_specs=pl.BlockSpec((COMBINE_TILE, d), lambda i: (i, 0)),
        out_shape=jax.ShapeDtypeStruct((n_tok, d), F32),
        scratch_shapes=[pltpu.VMEM((2, TOP_K * COMBINE_TILE // SUBLANES, ROW_TILE_ROWS, SUBLANES, LANES), F32),
                        pltpu.SemaphoreType.DMA((2,))],
        compiler_params=_params("arbitrary"),
        name="moe_combine",
    )(pos, pos, x1, route, gvecs, lng, lnb, ys)


def _slots(route_t, tile_counts):
    n_tiles = tile_counts.shape[0]
    n = n_tiles * route_t.shape[2]
    tc = tile_counts.astype(jnp.int32)
    counts = jnp.sum(tc, axis=0)
    padded = (counts + MOE_BLOCK - 1) // MOE_BLOCK * MOE_BLOCK
    pends = jnp.cumsum(padded)
    base = (pends - padded)[None, :] + jnp.cumsum(tc, axis=0) - tc
    n_blocks = -(-n * TOP_K // MOE_BLOCK) + N_EXPERTS
    block_start = jnp.arange(n_blocks, dtype=jnp.int32) * MOE_BLOCK
    block_e = jnp.minimum(jnp.sum(pends[None, 0:N_EXPERTS] <= block_start[:, None], axis=1),
                          N_EXPERTS - 1).astype(jnp.int32)
    ids = jnp.arange(N_EXPERTS, dtype=jnp.int32)
    later = (ids[None, :] > ids[:, None]) & (padded[None, 0:N_EXPERTS] > 0)
    next_nonempty = jnp.min(jnp.where(later, ids[None, :], N_EXPERTS), axis=1)
    next_of = jnp.where(next_nonempty < N_EXPERTS, next_nonempty, ids)
    next_e = jnp.sum(jnp.where(block_e[:, None] == ids[None, :], next_of[None, :], 0), axis=1).astype(jnp.int32)
    n_used = (pends[N_EXPERTS - 1] // MOE_BLOCK).astype(jnp.int32).reshape(1)
    seg = jnp.stack([pends[0:N_EXPERTS], padded[0:N_EXPERTS]]).astype(jnp.int32)
    pos = []
    for k in range(TOP_K):
        e = route_t[:, ROUTE_EXPERT + k, :].astype(jnp.int32)
        hit = e[:, None, :] == jnp.arange(N_EXPERTS, dtype=jnp.int32)[None, :, None]
        b = jnp.sum(jnp.where(hit, base[:, 0:N_EXPERTS, None], 0), axis=1)
        pos.append((b + route_t[:, ROUTE_RANK + k, :].astype(jnp.int32)).reshape(n))
    return block_e, next_e, n_used, seg, pos


def _rope_tables(seq_len):
    m = HEAD_DIM // 4
    freqs = ROPE_BASE ** (-jnp.arange(m, dtype=F32) / m)
    t = jnp.arange(seq_len)
    row = (t // GRID_W).astype(F32)[:, None] * freqs[None, :]
    col = (t % GRID_W).astype(F32)[:, None] * freqs[None, :]
    cos = jnp.concatenate([jnp.cos(row), jnp.cos(row), jnp.cos(col), jnp.cos(col)], axis=-1)
    zero = jnp.zeros_like(row)
    sin_a = jnp.concatenate([zero, jnp.sin(row), zero, jnp.sin(col)], axis=-1)
    sin_b = jnp.concatenate([-jnp.sin(row), zero, -jnp.sin(col), zero], axis=-1)
    rep = LANES // HEAD_DIM
    return jnp.tile(cos, (1, rep)), jnp.tile(sin_a, (1, rep)), jnp.tile(sin_b, (1, rep))


def kernel(x, c, ctx, c_ctx, w_ada, b_ada, w_in, conv_w, attn_sink, gm_ws, gm_bs, w_out, ln1_g, ln1_b,
           w_rg, b_rg, w_re, b_re, w1, w3, w2, ln2_g, ln2_b):
    b_, s_, d_ = x.shape
    c_len = ctx.shape[1]
    depth = w_ada.shape[0]
    alpha = (2 * depth) ** 0.25
    n_lat = b_ * s_
    n_ctx = b_ * c_len
    ts = 1024

    cin = jnp.zeros((ADA_ROWS, d_), F32).at[0:b_].set(c).at[b_].set(c_ctx)
    mod = _ada_call(cin, w_ada, b_ada)

    cos, sin_a, sin_b = _rope_tables(s_)
    ones_c = jnp.ones((c_len, LANES), F32)
    zeros_c = jnp.zeros((c_len, LANES), F32)

    x_flat = x.reshape(n_lat, d_)
    c_flat = ctx.reshape(n_ctx, d_)
    x_off, c_off = 0, 0
    x_arr, c_arr = x_flat, c_flat

    for l in range(depth):
        last = l == depth - 1
        mx = mod[l, 0:b_].reshape(b_, 6, 1, d_)
        sh1, sc1, g1, sh2, sc2, g2 = (mx[:, i] for i in range(6))
        mc = jnp.broadcast_to(mod[l, b_].reshape(1, 6, 1, d_), (b_, 6, 1, d_))
        gm_w = gm_ws[l].astype(BF16)
        gm_b = jnp.repeat(gm_bs[l].T, GM_HEAD, axis=1)
        sink = attn_sink[l]

        qx, kvx, yx = _inproj_call(x_arr, x_off, b_, s_, ts, sc1, sh1, w_in, l, cos, sin_a, sin_b,
                                   conv_w[l], gm_w, gm_b)
        qc, kvc, yc = _inproj_call(c_arr, c_off, b_, c_len, c_len, mc[:, 1], mc[:, 0], w_in, l,
                                   ones_c, zeros_c, zeros_c, conv_w[l], gm_w, gm_b)
        att_x = _win_attn_call(sink, qx, kvx, kvc, b_, s_, c_len)

        w_r = jnp.zeros((d_, ROUTER_COLS), F32).at[:, 0:N_GROUPS].set(w_rg[l]) \
            .at[:, N_GROUPS:N_GROUPS + N_EXPERTS].set(w_re[l])
        w_r_hi = w_r.astype(BF16)
        w_r_lo = (w_r - w_r_hi.astype(F32)).astype(BF16)
        wr = jnp.concatenate([w_r_hi, w_r_lo], axis=1)
        br = jnp.zeros((1, ROUTER_COLS), F32).at[0, 0:N_GROUPS].set(b_rg[l]) \
            .at[0, N_GROUPS:N_GROUPS + N_EXPERTS].set(b_re[l])
        lng1, lnb1 = ln1_g[l].reshape(1, d_), ln1_b[l].reshape(1, d_)
        lng2, lnb2 = ln2_g[l].reshape(1, d_), ln2_b[l].reshape(1, d_)

        n_tot = n_lat if last else n_lat + n_ctx
        gvecs1 = jnp.concatenate([g1, mc[0:1, 2]], axis=0)
        scvecs2 = jnp.concatenate([sc2, mc[0:1, 4]], axis=0)
        shvecs2 = jnp.concatenate([sh2, mc[0:1, 3]], axis=0)
        if last:
            x1, h2t, route, route_t, tcnt = _outproj_call(att_x, att_x, yx, yx, x_arr, x_off, x_arr, x_off, n_lat, 0, s_,
                                                 w_out, l, gvecs1, lng1, lnb1, scvecs2, shvecs2, wr, br, alpha)
        else:
            att_c = _ctx_attn_call(sink, qc, kvc, b_, c_len)
            x1, h2t, route, route_t, tcnt = _outproj_call(att_x, att_c, yx, yc, x_arr, x_off, c_arr, c_off, n_lat, n_ctx,
                                                 s_, w_out, l, gvecs1, lng1, lnb1, scvecs2, shvecs2, wr, br, alpha)

        block_e, next_e, n_used, seg, pos = _slots(route_t, tcnt[:, 0, :])
        n_slots = block_e.shape[0] * MOE_BLOCK

        def per_tile(tile):
            return jnp.concatenate([p.reshape(n_tot // tile, 1, tile) for p in pos], axis=2)

        xs = _dispatch_call(seg, per_tile(DISPATCH_TILE), h2t, n_slots)
        ys = _moe_call(block_e, next_e, n_used, xs, w1, w3, w2, l)

        gvecs = jnp.concatenate([g2, mc[0:1, 5]], axis=0)
        out = _combine_call(per_tile(COMBINE_TILE), x1, route, gvecs, lng2, lnb2, ys, n_tot, s_ // COMBINE_TILE,
                            b_, alpha)
        x_arr, x_off = out, 0
        c_arr, c_off = out, n_lat

    return x_arr.reshape(b_, s_, d_)
```

```python
import functools

import jax
import jax.numpy as jnp
from jax import lax
from jax.experimental import pallas as pl
from jax.experimental.pallas import tpu as pltpu

F32 = jnp.float32
BF16 = jnp.bfloat16

D_MODEL = 1024
GRID_W = 64
HEAD_DIM = 64
N_HEADS = 8
N_KV_HEADS = 2
ATTN_WIDTH = N_HEADS * HEAD_DIM
KV_WIDTH = N_KV_HEADS * HEAD_DIM
WINDOW = 128
ATTN_SCALE = HEAD_DIM ** -0.5
LOG2E = 1.4426950408889634
ROPE_BASE = 10000.0
CONV_WIDTH = D_MODEL // 4
GM_WIDTH = D_MODEL // 4
GM_GROUPS = 4
GM_HEAD = GM_WIDTH // GM_GROUPS
CHUNK = 128
LOCAL_WIDTH = CONV_WIDTH + GM_WIDTH
IN_WIDTH = ATTN_WIDTH + 2 * KV_WIDTH + 3 * CONV_WIDTH + 2 * GM_WIDTH
QKV_WIDTH = ATTN_WIDTH + 2 * KV_WIDTH
KV_PACK_WIDTH = 4 * KV_WIDTH
N_GROUPS = 4
EXP_PER_GROUP = 8
N_EXPERTS = N_GROUPS * EXP_PER_GROUP
TOP_K = 2
D_EXPERT = D_MODEL // 2
MOE_BLOCK = 512
LN_EPS = 1e-6
NEG_INF = -1e30

LANES = 128
SUBLANES = 8
VMEM_LIMIT_BYTES = 48 * 1024 * 1024
N_DMA_PRIORITIES = 2

ROW_TILE_ROWS = D_MODEL // LANES
assert ROW_TILE_ROWS == SUBLANES
PACKED_TILE_ROWS = ROW_TILE_ROWS // 2
TOKENS_PER_PACKED_TILE = SUBLANES // PACKED_TILE_ROWS
assert TOKENS_PER_PACKED_TILE == 2

ROUTER_COLS = LANES
ROUTE_EXPERT = 0
ROUTE_GATE = TOP_K
ROUTE_RANK = 2 * TOP_K
ROPE_HALF_PAIR = HEAD_DIM // 4


def _bdot(a, b):
    return jnp.dot(a, b, preferred_element_type=F32)


def _split_bf16(a):
    hi = a.astype(BF16)
    lo = (a - hi.astype(F32)).astype(BF16)
    return hi, lo


def _layer_norm(r):
    mu = jnp.mean(r, axis=-1, keepdims=True)
    d = r - mu
    var = jnp.mean(d * d, axis=-1, keepdims=True)
    return d * lax.rsqrt(var + LN_EPS)


def _params(*sem, flags=None):
    return pltpu.CompilerParams(dimension_semantics=sem, vmem_limit_bytes=VMEM_LIMIT_BYTES, flags=flags)


def _start_all(copies):
    for cp in copies:
        cp.start()


def _wait_all(copies):
    for cp in copies:
        cp.wait()


ADA_ROWS = 16
ADA_TILE = 1536


def _ada_kernel(c_ref, w_ref, b_ref, o_ref):
    act = jax.nn.silu(c_ref[...])
    ah, al = _split_bf16(act)
    wh, wl = _split_bf16(w_ref[...])
    o_ref[...] = _bdot(ah, wh) + _bdot(ah, wl) + _bdot(al, wh) + b_ref[...]


def _ada_call(cin, w_ada, b_ada):
    depth, d, n = w_ada.shape
    return pl.pallas_call(
        _ada_kernel,
        grid=(depth, n // ADA_TILE),
        in_specs=[
            pl.BlockSpec((ADA_ROWS, d), lambda l, j: (0, 0)),
            pl.BlockSpec((None, d, ADA_TILE), lambda l, j: (l, 0, j)),
            pl.BlockSpec((None, 1, ADA_TILE), lambda l, j: (l, 0, j)),
        ],
        out_specs=pl.BlockSpec((None, ADA_ROWS, ADA_TILE), lambda l, j: (l, 0, j)),
        out_shape=jax.ShapeDtypeStruct((depth, ADA_ROWS, n), F32),
        compiler_params=_params("parallel", "parallel"),
        name="ada_modulation",
    )(cin, w_ada, b_ada.reshape(depth, 1, n))


def _inproj_kernel(x_ref, xp_ref, xn_ref, sc_ref, sh_ref, wf_ref, cos_ref, sa_ref, sb_ref,
                   cw_ref, gw_ref, gb_ref,
                   q_ref, kv_ref, y_ref, w_ref, *, ts):
    t = pl.program_id(1)
    nt = pl.num_programs(1)

    @pl.when((pl.program_id(0) == 0) & (t == 0))
    def _():
        w_ref[...] = wf_ref[...].astype(BF16)

    sc = 1.0 + sc_ref[...]
    sh = sh_ref[...]
    hx = (x_ref[...] * sc + sh).astype(BF16)

    pq = _bdot(hx, w_ref[:, 0:QKV_WIDTH])
    cos = cos_ref[...]
    sa = sa_ref[...]
    sb = sb_ref[...]

    def rope(z):
        return (z * cos + pltpu.roll(z, ROPE_HALF_PAIR, 1) * sa
                + pltpu.roll(z, LANES - ROPE_HALF_PAIR, 1) * sb)

    for j in range(ATTN_WIDTH // LANES):
        sl = slice(j * LANES, (j + 1) * LANES)
        q_ref[:, sl] = (rope(pq[:, sl]) * (ATTN_SCALE * LOG2E)).astype(BF16)
    kr = rope(pq[:, ATTN_WIDTH:ATTN_WIDTH + KV_WIDTH])
    vv = pq[:, ATTN_WIDTH + KV_WIDTH:QKV_WIDTH]
    for j, part in enumerate((kr, pltpu.roll(kr, HEAD_DIM, 1), vv, pltpu.roll(vv, HEAD_DIM, 1))):
        kv_ref[:, j * KV_WIDTH:(j + 1) * KV_WIDTH] = part.astype(BF16)

    pm = _bdot(hx, w_ref[:, QKV_WIDTH:IN_WIDTH])
    cb = pm[:, 0:CONV_WIDTH]
    u = pm[:, CONV_WIDTH:2 * CONV_WIDTH] * pm[:, 2 * CONV_WIDTH:3 * CONV_WIDTH]

    halo = jnp.concatenate([xp_ref[...], xn_ref[...]], axis=0)
    hh = (halo * sc + sh).astype(BF16)
    ph = _bdot(hh, w_ref[:, QKV_WIDTH + CONV_WIDTH:QKV_WIDTH + 3 * CONV_WIDTH])
    uh = ph[:, 0:CONV_WIDTH] * ph[:, CONV_WIDTH:2 * CONV_WIDTH]
    up_row = jnp.where(t > 0, uh[SUBLANES - 1:SUBLANES, :], 0.0)
    dn_row = jnp.where(t < nt - 1, uh[SUBLANES:SUBLANES + 1, :], 0.0)
    ridx = lax.broadcasted_iota(jnp.int32, (ts, CONV_WIDTH), 0)
    u_up = jnp.where(ridx == 0, up_row, pltpu.roll(u, 1, 0))
    u_dn = jnp.where(ridx == ts - 1, dn_row, pltpu.roll(u, ts - 1, 0))
    cw = cw_ref[...]
    y_conv = cb * (u_up * cw[0:1, :] + u * cw[1:2, :] + u_dn * cw[2:3, :])
    y_ref[:, 0:CONV_WIDTH] = y_conv.astype(BF16)

    gu = jax.nn.gelu(pm[:, 3 * CONV_WIDTH:3 * CONV_WIDTH + GM_WIDTH])
    gv = _layer_norm(jax.nn.gelu(pm[:, 3 * CONV_WIDTH + GM_WIDTH:3 * CONV_WIDTH + 2 * GM_WIDTH])).astype(BF16)
    lane = lax.broadcasted_iota(jnp.int32, (CHUNK, LANES), 1)
    zero = jnp.zeros((CHUNK, LANES), BF16)
    for c in range(ts // CHUNK):
        rows = slice(c * CHUNK, (c + 1) * CHUNK)
        for j in range(GM_WIDTH // LANES):
            cols = slice(j * LANES, (j + 1) * LANES)
            vp = gv[rows, cols]
            s = (_bdot(gw_ref[2 * j], jnp.where(lane < GM_HEAD, vp, zero))
                 + _bdot(gw_ref[2 * j + 1], jnp.where(lane >= GM_HEAD, vp, zero))
                 + gb_ref[:, cols])
            y_ref[rows, CONV_WIDTH + j * LANES:CONV_WIDTH + (j + 1) * LANES] = (gu[rows, cols] * s).astype(BF16)


def _inproj_call(x2d, row_off, n_seq, seq_len, ts, sc, sh, w_in, layer, cos, sa, sb, conv_w, gm_w, gm_b):
    nt = seq_len // ts
    off_t = row_off // ts
    off_8 = row_off // SUBLANES
    last_8 = x2d.shape[0] // SUBLANES - 1
    per_tile_8 = ts // SUBLANES
    n_out = n_seq * seq_len
    d = D_MODEL

    def tile_idx(b, t):
        return (off_t + b * nt + t, 0)

    def prev_idx(b, t):
        return (jnp.maximum(off_8 + (b * nt + t) * per_tile_8 - 1, 0), 0)

    def next_idx(b, t):
        return (jnp.minimum(off_8 + (b * nt + t + 1) * per_tile_8, last_8), 0)

    def out_idx(b, t):
        return (b * nt + t, 0)

    vec = pl.BlockSpec((None, 1, d), lambda b, t: (b, 0, 0))
    rope_spec = pl.BlockSpec((ts, LANES), lambda b, t: (t, 0))
    return pl.pallas_call(
        functools.partial(_inproj_kernel, ts=ts),
        grid=(n_seq, nt),
        in_specs=[
            pl.BlockSpec((ts, d), tile_idx),
            pl.BlockSpec((SUBLANES, d), prev_idx),
            pl.BlockSpec((SUBLANES, d), next_idx),
            vec, vec,
            pl.BlockSpec((None, d, IN_WIDTH), lambda b, t: (layer, 0, 0)),
            rope_spec, rope_spec, rope_spec,
            pl.BlockSpec((3, CONV_WIDTH), lambda b, t: (0, 0)),
            pl.BlockSpec((GM_GROUPS, CHUNK, CHUNK), lambda b, t: (0, 0, 0)),
            pl.BlockSpec((CHUNK, GM_WIDTH), lambda b, t: (0, 0)),
        ],
        out_specs=[
            pl.BlockSpec((ts, ATTN_WIDTH), out_idx),
            pl.BlockSpec((ts, KV_PACK_WIDTH), out_idx),
            pl.BlockSpec((ts, LOCAL_WIDTH), out_idx),
        ],
        out_shape=[
            jax.ShapeDtypeStruct((n_out, ATTN_WIDTH), BF16),
            jax.ShapeDtypeStruct((n_out, KV_PACK_WIDTH), BF16),
            jax.ShapeDtypeStruct((n_out, LOCAL_WIDTH), BF16),
        ],
        scratch_shapes=[pltpu.VMEM((d, IN_WIDTH), BF16)],
        compiler_params=_params("arbitrary", "arbitrary"),
        name="in_projection",
    )(x2d, x2d, x2d, sc, sh, w_in, cos, sa, sb, conv_w, gm_w, gm_b)


def _attn_body(q, keys, keys_sw, vals, vals_sw, halo_bias, sink_ref, o_ref, tq):
    nk = keys.shape[0]
    lane = lax.broadcasted_iota(jnp.int32, (nk, LANES), 1)
    lo = lane < HEAD_DIM
    zero = jnp.zeros((nk, LANES), BF16)
    rid = lax.broadcasted_iota(jnp.int32, (2 * tq, 1), 0)
    out_lo = lax.broadcasted_iota(jnp.int32, (2 * tq, LANES), 1) < HEAD_DIM
    nt_dims = (((1,), (1,)), ((), ()))
    gqa = N_HEADS // N_KV_HEADS
    biases = [None] * (nk // LANES)
    if halo_bias is not None:
        assert tq == LANES
        biases[0], biases[2] = halo_bias
    for h in range(N_KV_HEADS):
        k_own, k_oth = (keys, keys_sw) if h == 0 else (keys_sw, keys)
        v_own, v_oth = (vals, vals_sw) if h == 0 else (vals_sw, vals)
        kz = jnp.concatenate([jnp.where(lo, k_own, zero), jnp.where(lo, zero, k_oth)], axis=0)
        vz = jnp.concatenate([jnp.where(lo, v_own, zero), jnp.where(lo, zero, v_oth)], axis=0)
        c0 = h * gqa * HEAD_DIM
        qs = jnp.concatenate([q[:, c0:c0 + LANES], q[:, c0 + LANES:c0 + 2 * LANES]], axis=0)
        s_all = lax.dot_general(qs, kz, nt_dims, preferred_element_type=F32)
        probs, inv_den = [], []
        for par in range(2):
            tiles = []
            for j, bias in enumerate(biases):
                tile = s_all[:, par * nk + j * LANES:par * nk + (j + 1) * LANES]
                tiles.append(tile if bias is None else tile + bias)
            sink = jnp.where(rid < tq, sink_ref[h * gqa + par], sink_ref[h * gqa + 2 + par]) * LOG2E
            tile_max = tiles[0]
            for tile in tiles[1:]:
                tile_max = jnp.maximum(tile_max, tile)
            m = jnp.maximum(jnp.max(tile_max, axis=-1, keepdims=True), sink)
            tile_sum = None
            for tile in tiles:
                p = jnp.exp2(tile - m)
                tile_sum = p if tile_sum is None else tile_sum + p
                probs.append(p.astype(BF16))
            inv_den.append(1.0 / (jnp.sum(tile_sum, axis=-1, keepdims=True) + jnp.exp2(sink - m)))
        o = _bdot(jnp.concatenate(probs, axis=1), vz) * jnp.where(out_lo, inv_den[0], inv_den[1])
        o_ref[:, c0:c0 + LANES] = o[0:tq].astype(BF16)
        o_ref[:, c0 + LANES:c0 + 2 * LANES] = o[tq:2 * tq].astype(BF16)


def _split_kv(kv):
    return tuple(kv[:, j * KV_WIDTH:(j + 1) * KV_WIDTH] for j in range(4))


def _win_attn_kernel(sink_ref, q_ref, kvp_ref, kvc_ref, kvn_ref, kvx_ref, o_ref, *, tq):
    n = pl.program_id(1)
    nb = pl.num_programs(1)
    row = lax.broadcasted_iota(jnp.int32, (tq, tq), 0)
    col = lax.broadcasted_iota(jnp.int32, (tq, tq), 1)
    b_prev = jnp.where((col >= row) & (n > 0), 0.0, NEG_INF)
    b_next = jnp.where((col <= row) & (n < nb - 1), 0.0, NEG_INF)
    halo_bias = (jnp.concatenate([b_prev, b_prev], axis=0), jnp.concatenate([b_next, b_next], axis=0))

    kv = jnp.concatenate([kvp_ref[...], kvc_ref[...], kvn_ref[...], kvx_ref[...]], axis=0)
    _attn_body(q_ref[...], *_split_kv(kv), halo_bias, sink_ref, o_ref, tq)


def _win_attn_call(sink, q, kv, kv_ctx, n_seq, seq_len, ctx_len):
    tq = WINDOW
    nb = seq_len // tq

    def cur(b, n):
        return (b * nb + n, 0)

    def prev(b, n):
        return (b * nb + jnp.maximum(n - 1, 0), 0)

    def nxt(b, n):
        return (b * nb + jnp.minimum(n + 1, nb - 1), 0)

    return pl.pallas_call(
        functools.partial(_win_attn_kernel, tq=tq),
        grid=(n_seq, nb),
        in_specs=[pl.BlockSpec(memory_space=pltpu.SMEM), pl.BlockSpec((tq, ATTN_WIDTH), cur),
                  pl.BlockSpec((tq, KV_PACK_WIDTH), prev), pl.BlockSpec((tq, KV_PACK_WIDTH), cur),
                  pl.BlockSpec((tq, KV_PACK_WIDTH), nxt),
                  pl.BlockSpec((ctx_len, KV_PACK_WIDTH), lambda b, n: (b, 0))],
        out_specs=pl.BlockSpec((tq, ATTN_WIDTH), cur),
        out_shape=jax.ShapeDtypeStruct((n_seq * seq_len, ATTN_WIDTH), BF16),
        compiler_params=_params("parallel", "parallel"),
        name="window_attention",
    )(sink, q, kv, kv, kv, kv_ctx)


def _ctx_attn_kernel(sink_ref, q_ref, kv_ref, o_ref, *, tq):
    _attn_body(q_ref[...], *_split_kv(kv_ref[...]), None, sink_ref, o_ref, tq)


def _ctx_attn_call(sink, q, kv, n_seq, ctx_len):
    def blk(w):
        return pl.BlockSpec((ctx_len, w), lambda b: (b, 0))

    return pl.pallas_call(
        functools.partial(_ctx_attn_kernel, tq=ctx_len),
        grid=(n_seq,),
        in_specs=[pl.BlockSpec(memory_space=pltpu.SMEM), blk(ATTN_WIDTH), blk(KV_PACK_WIDTH)],
        out_specs=blk(ATTN_WIDTH),
        out_shape=jax.ShapeDtypeStruct((n_seq * ctx_len, ATTN_WIDTH), BF16),
        compiler_params=_params("parallel"),
        name="context_attention",
    )(sink, q, kv)


OUT_TILE = 512


def _outproj_kernel(attx_ref, attc_ref, yx_ref, yc_ref, xx_ref, xc_ref, wof_ref, g_ref, lng_ref, lnb_ref,
                    sc_ref, sh_ref, wr_ref, br_ref, before_ref,
                    x1_ref, h2v_ref, route_ref, routet_ref, cnt_ref, wo_ref, o_buf, lg_buf,
                    *, alpha, n_lat_tiles, n_tiles, has_ctx):
    step = pl.program_id(0)

    @pl.when(step == 0)
    def _():
        wo_ref[...] = wof_ref[...].astype(BF16)
        o_buf[1] = jnp.zeros(o_buf.shape[1:], F32)
        lg_buf[1] = jnp.zeros(lg_buf.shape[1:], F32)

    refs = (attx_ref, attc_ref, yx_ref, yc_ref, xx_ref, xc_ref, g_ref, lng_ref, lnb_ref, sc_ref, sh_ref, wr_ref,
            br_ref, before_ref, x1_ref, h2v_ref, route_ref, routet_ref, cnt_ref, wo_ref, o_buf, lg_buf)
    for parity in range(2):
        @pl.when(step % 2 == parity)
        def _():
            _outproj_stages(parity, 1 - parity, step, *refs, alpha=alpha, n_lat_tiles=n_lat_tiles,
                            n_tiles=n_tiles, has_ctx=has_ctx)


def _outproj_stages(cur, prev, step, attx_ref, attc_ref, yx_ref, yc_ref, xx_ref, xc_ref, g_ref, lng_ref, lnb_ref,
                    sc_ref, sh_ref, wr_ref, br_ref, before_ref, x1_ref, h2v_ref, route_ref, routet_ref, cnt_ref,
                    wo_ref, o_buf, lg_buf, *, alpha, n_lat_tiles, n_tiles, has_ctx):
    o = o_buf[prev]
    lg = lg_buf[prev]

    if has_ctx:
        att = jnp.where(jnp.minimum(step, n_tiles - 1) < n_lat_tiles, attx_ref[...], attc_ref[...])
        yloc = jnp.where(jnp.minimum(step, n_tiles - 1) < n_lat_tiles, yx_ref[...], yc_ref[...])
        xin = jnp.where(jnp.clip(step - 1, 0, n_tiles - 1) < n_lat_tiles, xx_ref[...], xc_ref[...])
    else:
        att, yloc, xin = attx_ref[...], yx_ref[...], xx_ref[...]
    o_buf[cur] = _bdot(att, wo_ref[0:ATTN_WIDTH, :]) + _bdot(yloc, wo_ref[ATTN_WIDTH:, :])
    x1 = _layer_norm(alpha * xin + g_ref[...] * o) * lng_ref[...] + lnb_ref[...]
    x1_ref[...] = x1
    h2 = x1 * (1.0 + sc_ref[...]) + sh_ref[...]
    half = D_MODEL // 2
    hi_bits = lax.bitcast_convert_type(h2[:, 0:half].astype(BF16).astype(F32), jnp.uint32)
    lo_bits = lax.bitcast_convert_type(h2[:, half:].astype(BF16).astype(F32), jnp.uint32)
    packed = hi_bits | (lo_bits >> 16)
    for c in range(PACKED_TILE_ROWS):
        h2v_ref[:, c, :, :] = packed[:, c * LANES:(c + 1) * LANES].reshape(h2.shape[0] // SUBLANES, SUBLANES, LANES)

    hh, hl = _split_bf16(h2)
    t1 = _bdot(hh, wr_ref[...])
    lg_buf[cur] = (t1[:, 0:ROUTER_COLS] + t1[:, ROUTER_COLS:] + _bdot(hl, wr_ref[:, 0:ROUTER_COLS])
                        + br_ref[...])

    ts = lg.shape[0]
    lane = lax.broadcasted_iota(jnp.int32, (ts, ROUTER_COLS), 1).astype(F32)
    big = jnp.float32(ROUTER_COLS)

    def top1(v):
        m = jnp.max(v, axis=-1, keepdims=True)
        return m, jnp.min(jnp.where(v == m, lane, big), axis=-1, keepdims=True)

    gl = jnp.where(lane < N_GROUPS, lg, NEG_INF)
    g_val, g_idx = top1(gl)
    lse = g_val + jnp.log(jnp.sum(jnp.exp(gl - g_val), axis=-1, keepdims=True))
    p_group = jnp.exp(g_val - lse)
    e_lo = N_GROUPS + EXP_PER_GROUP * g_idx
    el = jnp.where((lane >= e_lo) & (lane < e_lo + EXP_PER_GROUP), lg, NEG_INF)
    e1, l1 = top1(el)
    e2, l2 = top1(jnp.where(lane == l1, NEG_INF, el))
    z = jnp.exp(e2 - e1)
    gate1 = p_group / (1.0 + z)
    gate2 = p_group * z / (1.0 + z)
    x1id = l1 - N_GROUPS
    x2id = l2 - N_GROUPS

    sel1 = lane == x1id
    sel2 = lane == x2id
    onehot = jnp.where(sel1 | sel2, 1.0, 0.0)
    prefix = _bdot(before_ref[...], onehot.astype(BF16))
    rank1 = jnp.sum(jnp.where(sel1, prefix, 0.0), axis=-1, keepdims=True)
    rank2 = jnp.sum(jnp.where(sel2, prefix, 0.0), axis=-1, keepdims=True)
    route = jnp.zeros((ts, ROUTER_COLS), F32)
    for col, val in enumerate((x1id, x2id, gate1, gate2, rank1, rank2)):
        route = jnp.where(lane == col, val, route)
    route_ref[...] = route
    routet_ref[...] = route.T[0:SUBLANES, :]
    cnt_ref[...] =jnp.broadcast_to(jnp.sum(onehot, axis=0, keepdims=True), (SUBLANES, ROUTER_COLS))


def _outproj_call(att_x, att_c, y_x, y_c, x_arr, x_off, c_arr, c_off, n_lat, n_ctx, seq_len,
                  w_out, layer, gvecs, lng, lnb, scvecs, shvecs, wr, br, alpha):
    ts = OUT_TILE
    d = D_MODEL
    n_lat_tiles = n_lat // ts
    n_tiles = (n_lat + n_ctx) // ts
    tiles_per_seq = seq_len // ts
    n_seq = n_lat // seq_len
    xo, co = x_off // ts, c_off // ts

    def mm_tile(s):
        return jnp.minimum(s, n_tiles - 1)

    def ln_tile(s):
        return jnp.clip(s - 1, 0, n_tiles - 1)

    def route_tile(s):
        return jnp.clip(s - 2, 0, n_tiles - 1)

    def lat_loc(s):
        return (jnp.minimum(mm_tile(s), n_lat_tiles - 1), 0)

    def ctx_loc(s):
        return (jnp.maximum(mm_tile(s) - n_lat_tiles, 0), 0)

    def lat_in(s):
        return (xo + jnp.minimum(ln_tile(s), n_lat_tiles - 1), 0)

    def ctx_in(s):
        return (co + jnp.maximum(ln_tile(s) - n_lat_tiles, 0), 0)

    def vec_idx(s):
        i = ln_tile(s)
        return (jnp.where(i < n_lat_tiles, i // tiles_per_seq, n_seq), 0, 0)

    vecb = pl.BlockSpec((None, 1, d), vec_idx)
    vec0 = pl.BlockSpec((1, d), lambda i: (0, 0))
    return pl.pallas_call(
        functools.partial(_outproj_kernel, alpha=alpha, n_lat_tiles=n_lat_tiles, n_tiles=n_tiles,
                          has_ctx=n_ctx > 0),
        grid=(n_tiles + 2,),
        in_specs=[
            pl.BlockSpec((ts, ATTN_WIDTH), lat_loc), pl.BlockSpec((ts, ATTN_WIDTH), ctx_loc),
            pl.BlockSpec((ts, LOCAL_WIDTH), lat_loc), pl.BlockSpec((ts, LOCAL_WIDTH), ctx_loc),
            pl.BlockSpec((ts, d), lat_in), pl.BlockSpec((ts, d), ctx_in),
            pl.BlockSpec((None, d, d), lambda i: (layer, 0, 0)),
            vecb, vec0, vec0, vecb, vecb,
            pl.BlockSpec((d, 2 * ROUTER_COLS), lambda i: (0, 0)),
            pl.BlockSpec((1, ROUTER_COLS), lambda i: (0, 0)),
            pl.BlockSpec((ts, ts), lambda i: (0, 0)),
        ],
        out_specs=[pl.BlockSpec((ts, d), lambda s: (ln_tile(s), 0)),
                   pl.BlockSpec((ts // SUBLANES, PACKED_TILE_ROWS, SUBLANES, LANES),
                                lambda s: (ln_tile(s), 0, 0, 0)),
                   pl.BlockSpec((ts, ROUTER_COLS), lambda s: (route_tile(s), 0)),
                   pl.BlockSpec((None, SUBLANES, ts), lambda s: (route_tile(s), 0, 0)),
                   pl.BlockSpec((None, SUBLANES, ROUTER_COLS), lambda s: (route_tile(s), 0, 0))],
        out_shape=[jax.ShapeDtypeStruct((n_lat + n_ctx, d), F32),
                   jax.ShapeDtypeStruct(((n_lat + n_ctx) // SUBLANES, PACKED_TILE_ROWS, SUBLANES, LANES),
                                        jnp.uint32),
                   jax.ShapeDtypeStruct((n_lat + n_ctx, ROUTER_COLS), F32),
                   jax.ShapeDtypeStruct((n_tiles, SUBLANES, ts), F32),
                   jax.ShapeDtypeStruct((n_tiles, SUBLANES, ROUTER_COLS), F32)],
        scratch_shapes=[pltpu.VMEM((d, d), BF16), pltpu.VMEM((2, ts, d), F32),
                        pltpu.VMEM((2, ts, ROUTER_COLS), F32)],
        compiler_params=_params("arbitrary"),
        name="out_projection",
    )(att_x, att_c, y_x, y_c, x_arr, c_arr, w_out, gvecs, lng, lnb, scvecs, shvecs, wr, br,
      jnp.tril(jnp.ones((ts, ts), BF16), -1))


DISPATCH_TILE = 2048


def _dispatch_kernel(seg_ref, pos_ref, h_ref, xs_out, zbuf, zsem, sem):
    block_tiles = MOE_BLOCK // TOKENS_PER_PACKED_TILE

    @pl.when(pl.program_id(0) == 0)
    def _():
        zbuf[...] = jnp.zeros_like(zbuf)

        def zero_copy(e):
            first_tile = (seg_ref[0, e] - MOE_BLOCK) // TOKENS_PER_PACKED_TILE
            return pltpu.make_async_copy(zbuf, xs_out.at[pl.ds(first_tile, block_tiles)], zsem)

        for e in range(N_EXPERTS):
            @pl.when(seg_ref[1, e] > 0)
            def _():
                zero_copy(e).start()
        for e in range(N_EXPERTS):
            @pl.when(seg_ref[1, e] > 0)
            def _():
                zero_copy(e).wait()

        def tail_copy(b):
            return pltpu.make_async_copy(zbuf, xs_out.at[pl.ds(b * block_tiles, block_tiles)], zsem)

        n_blocks = xs_out.shape[0] // block_tiles
        first_unused = seg_ref[0, N_EXPERTS - 1] // MOE_BLOCK
        lax.fori_loop(first_unused, n_blocks, lambda b, c: (tail_copy(b).start(), c)[1], 0)
        lax.fori_loop(first_unused, n_blocks, lambda b, c: (tail_copy(b).wait(), c)[1], 0)

    def body(j, carry):
        for s in range(SUBLANES):
            t = SUBLANES * j + s
            for k in range(TOP_K):
                p = pos_ref[0, 0, k * DISPATCH_TILE + t]
                dst = xs_out.at[p >> 1, pl.ds(PACKED_TILE_ROWS * (p & 1), PACKED_TILE_ROWS), :]
                pltpu.make_async_copy(h_ref.at[j, :, s, :], dst, sem).start(priority=k)
        return carry

    lax.fori_loop(0, DISPATCH_TILE // SUBLANES, body, 0)
    for k in range(TOP_K):
        pltpu.make_async_copy(h_ref, h_ref, sem).wait()


def _dispatch_call(seg, pos, h2t, n_slots):
    n_tok = h2t.shape[0] * SUBLANES
    n_tiles = n_tok // DISPATCH_TILE
    grid_spec = pltpu.PrefetchScalarGridSpec(
        num_scalar_prefetch=1,
        grid=(n_tiles,),
        in_specs=[
            pl.BlockSpec((1, 1, TOP_K * DISPATCH_TILE), lambda i, seg: (i, 0, 0), memory_space=pltpu.SMEM),
            pl.BlockSpec((DISPATCH_TILE // SUBLANES, PACKED_TILE_ROWS, SUBLANES, LANES),
                         lambda i, seg: (i, 0, 0, 0)),
        ],
        out_specs=pl.BlockSpec(memory_space=pl.ANY),
        scratch_shapes=[pltpu.VMEM((MOE_BLOCK // TOKENS_PER_PACKED_TILE, SUBLANES, LANES), jnp.uint32),
                        pltpu.SemaphoreType.DMA(()), pltpu.SemaphoreType.DMA(())],
    )
    return pl.pallas_call(
        _dispatch_kernel,
        grid_spec=grid_spec,
        out_shape=jax.ShapeDtypeStruct((n_slots // TOKENS_PER_PACKED_TILE, SUBLANES, LANES), jnp.uint32),
        compiler_params=_params("arbitrary"),
        name="moe_dispatch",
    )(seg, pos, h2t)


def _moe_kernel(be_ref, ne_ref, nu_ref, xs_hbm, w1_hbm, w3_hbm, w2_hbm, ys_hbm, w1b, w3b, w2b, w1f, w3f, w2f,
                xbuf, obuf, isem, osem, wsem, *, layer):
    i = pl.program_id(0)
    n_used = nu_ref[0]
    slot = i % 2
    groups = MOE_BLOCK // SUBLANES
    half = D_MODEL // 2

    def in_copies(blk, at_slot):
        return [pltpu.make_async_copy(
            xs_hbm.at[pl.ds(blk * groups, groups), s // TOKENS_PER_PACKED_TILE,
                      pl.ds(PACKED_TILE_ROWS * (s % TOKENS_PER_PACKED_TILE), PACKED_TILE_ROWS), :],
            xbuf.at[at_slot, :, :, s, :], isem.at[at_slot]) for s in range(SUBLANES)]

    def out_copies(blk, at_slot):
        return [pltpu.make_async_copy(obuf.at[at_slot, :, :, s, :], ys_hbm.at[pl.ds(blk * groups, groups), s],
                                      osem.at[at_slot]) for s in range(SUBLANES)]

    def write_rows(at_slot, y):
        for c in range(ROW_TILE_ROWS):
            obuf[at_slot, :, c, :, :] = y[:, c * LANES:(c + 1) * LANES].reshape(groups, SUBLANES, LANES)

    @pl.when(i == 0)
    def _():
        _start_all(in_copies(0, 0))

    @pl.when(i + 1 < n_used)
    def _():
        _start_all(in_copies(i + 1, 1 - slot))

    def weight_copies(e):
        return [pltpu.make_async_copy(src.at[layer, e], dst, wsem)
                for src, dst in ((w1_hbm, w1f), (w3_hbm, w3f), (w2_hbm, w2f))]

    @pl.when(i == 0)
    def _():
        _start_all(weight_copies(be_ref[0]))

    @pl.when((i < n_used) & ((i == 0) | (be_ref[i] != be_ref[jnp.maximum(i - 1, 0)])))
    def _():
        _wait_all(weight_copies(be_ref[i]))
        w1b[...] = w1f[...].astype(BF16)
        w3b[...] = w3f[...].astype(BF16)
        w2b[...] = w2f[...].astype(BF16)

        @pl.when(ne_ref[i] != be_ref[i])
        def _():
            _start_all(weight_copies(ne_ref[i]))

    @pl.when(i < n_used)
    def _():
        _wait_all(in_copies(i, slot))
        p = jnp.concatenate([xbuf[slot, :, c, :, :].reshape(MOE_BLOCK, LANES) for c in range(PACKED_TILE_ROWS)],
                            axis=1)
        xa = lax.bitcast_convert_type(p & jnp.uint32(0xFFFF0000), F32).astype(BF16)
        xb = lax.bitcast_convert_type(p << 16, F32).astype(BF16)
        h1 = _bdot(xa, w1b[0:half, :]) + _bdot(xb, w1b[half:, :])
        h3 = _bdot(xa, w3b[0:half, :]) + _bdot(xb, w3b[half:, :])
        write_rows(slot, _bdot((jax.nn.silu(h1) * h3).astype(BF16), w2b[...]))
        _start_all(out_copies(i, slot))

        @pl.when(i >= 1)
        def _():
            _wait_all(out_copies(i - 1, 1 - slot))

        @pl.when(i == n_used - 1)
        def _():
            _wait_all(out_copies(i, slot))

    @pl.when(i == n_used)
    def _():
        obuf[0] = jnp.zeros(obuf.shape[1:], F32)

    @pl.when(i >= n_used)
    def _():
        _start_all(out_copies(i, 0))

    @pl.when(i == pl.num_programs(0) - 1)
    def _():
        lax.fori_loop(n_used, pl.num_programs(0), lambda b, c: (_wait_all(out_copies(b, 0)), c)[1], 0)


def _moe_call(block_e, next_e, n_used, xs, w1, w3, w2, layer):
    n_blocks = block_e.shape[0]
    n_slots = n_blocks * MOE_BLOCK
    d = D_MODEL
    groups = MOE_BLOCK // SUBLANES
    grid_spec = pltpu.PrefetchScalarGridSpec(
        num_scalar_prefetch=3,
        grid=(n_blocks,),
        in_specs=[pl.BlockSpec(memory_space=pl.ANY)] * 4,
        out_specs=pl.BlockSpec(memory_space=pl.ANY),
        scratch_shapes=[pltpu.VMEM((d, D_EXPERT), BF16), pltpu.VMEM((d, D_EXPERT), BF16),
                        pltpu.VMEM((D_EXPERT, d), BF16),
                        pltpu.VMEM((d, D_EXPERT), F32), pltpu.VMEM((d, D_EXPERT), F32),
                        pltpu.VMEM((D_EXPERT, d), F32),
                        pltpu.VMEM((2, groups, PACKED_TILE_ROWS, SUBLANES, LANES), jnp.uint32),
                        pltpu.VMEM((2, groups, ROW_TILE_ROWS, SUBLANES, LANES), F32),
                        pltpu.SemaphoreType.DMA((2,)), pltpu.SemaphoreType.DMA((2,)),
                        pltpu.SemaphoreType.DMA(())],
    )
    xs_view = (n_slots // SUBLANES, SUBLANES // TOKENS_PER_PACKED_TILE, SUBLANES, LANES)
    ys_view = (n_slots // SUBLANES, SUBLANES, ROW_TILE_ROWS, LANES)
    ys = pl.pallas_call(
        functools.partial(_moe_kernel, layer=layer),
        grid_spec=grid_spec,
        out_shape=jax.ShapeDtypeStruct(ys_view, F32),
        compiler_params=_params("arbitrary"),
        name="moe_experts",
    )(block_e, next_e, n_used, xs.reshape(xs_view), w1, w3, w2)
    return ys.reshape(n_slots, ROW_TILE_ROWS, LANES)


COMBINE_TILE = 256


def _combine_kernel(pos0_ref, posn_ref, x1_ref, route_ref, g_ref, lng_ref, lnb_ref, ys_hbm, o_ref, ybuf, sem,
                    *, alpha):
    i = pl.program_id(0)
    nb = pl.num_programs(0)
    n_rows = TOP_K * COMBINE_TILE

    def issue(pos_ref, slot):
        def body(j, carry):
            for s in range(SUBLANES):
                pltpu.make_async_copy(ys_hbm.at[pos_ref[0, 0, SUBLANES * j + s]], ybuf.at[slot, j, :, s, :],
                                      sem.at[slot]).start(priority=s % N_DMA_PRIORITIES)
            return carry
        lax.fori_loop(0, n_rows // SUBLANES, body, 0)

    @pl.when(i == 0)
    def _():
        issue(pos0_ref, 0)

    @pl.when(i + 1 < nb)
    def _():
        issue(posn_ref, (i + 1) % 2)

    slot = i % 2
    pltpu.make_async_copy(ybuf.at[slot], ybuf.at[slot], sem.at[slot]).wait()
    yrows = jnp.concatenate([ybuf[slot, :, c, :, :].reshape(n_rows, LANES) for c in range(ROW_TILE_ROWS)], axis=1)
    route = route_ref[...]
    y = (yrows[0:COMBINE_TILE, :] * route[:, ROUTE_GATE:ROUTE_GATE + 1]
         + yrows[COMBINE_TILE:, :] * route[:, ROUTE_GATE + 1:ROUTE_GATE + 2])
    o_ref[...] = _layer_norm(alpha * x1_ref[...] + g_ref[...] * y) * lng_ref[...] + lnb_ref[...]


def _combine_call(pos, x1, route, gvecs, lng, lnb, ys, n_tok, tiles_per_seq, n_seq, alpha):
    d = D_MODEL
    n_tiles = n_tok // COMBINE_TILE
    n_lat_tiles = tiles_per_seq * n_seq

    def g_idx(i):
        return (jnp.where(i < n_lat_tiles, i // tiles_per_seq, n_seq), 0, 0)

    smem_blk = (1, 1, TOP_K * COMBINE_TILE)
    return pl.pallas_call(
        functools.partial(_combine_kernel, alpha=alpha),
        grid=(n_tiles,),
        in_specs=[
            pl.BlockSpec(smem_blk, lambda i: (0, 0, 0), memory_space=pltpu.SMEM),
            pl.BlockSpec(smem_blk, lambda i: (jnp.minimum(i + 1, n_tiles - 1), 0, 0), memory_space=pltpu.SMEM),
            pl.BlockSpec((COMBINE_TILE, d), lambda i: (i, 0)),
            pl.BlockSpec((COMBINE_TILE, ROUTER_COLS), lambda i: (i, 0)),
            pl.BlockSpec((None, 1, d), g_idx),
            pl.BlockSpec((1, d), lambda i: (0, 0)),
            pl.BlockSpec((1, d), lambda i: (0, 0)),
            pl.BlockSpec(memory_space=pl.ANY),
        ],
        out_specs=pl.BlockSpec((COMBINE_TILE, d), lambda i: (i, 0)),
        out_shape=jax.ShapeDtypeStruct((n_tok, d), F32),
        scratch_shapes=[pltpu.VMEM((2, TOP_K * COMBINE_TILE // SUBLANES, ROW_TILE_ROWS, SUBLANES, LANES), F32),
                        pltpu.SemaphoreType.DMA((2,))],
        compiler_params=_params("arbitrary"),
        name="moe_combine",
    )(pos, pos, x1, route, gvecs, lng, lnb, ys)


def _slots(route_t, tile_counts):
    n_tiles = tile_counts.shape[0]
    n = n_tiles * route_t.shape[2]
    tc = tile_counts.astype(jnp.int32)
    counts = jnp.sum(tc, axis=0)
    padded = (counts + MOE_BLOCK - 1) // MOE_BLOCK * MOE_BLOCK
    pends = jnp.cumsum(padded)
    base = (pends - padded)[None, :] + jnp.cumsum(tc, axis=0) - tc
    n_blocks = -(-n * TOP_K // MOE_BLOCK) + N_EXPERTS
    block_start = jnp.arange(n_blocks, dtype=jnp.int32) * MOE_BLOCK
    block_e = jnp.minimum(jnp.sum(pends[None, 0:N_EXPERTS] <= block_start[:, None], axis=1),
                          N_EXPERTS - 1).astype(jnp.int32)
    ids = jnp.arange(N_EXPERTS, dtype=jnp.int32)
    later = (ids[None, :] > ids[:, None]) & (padded[None, 0:N_EXPERTS] > 0)
    next_nonempty = jnp.min(jnp.where(later, ids[None, :], N_EXPERTS), axis=1)
    next_of = jnp.where(next_nonempty < N_EXPERTS, next_nonempty, ids)
    next_e = jnp.sum(jnp.where(block_e[:, None] == ids[None, :], next_of[None, :], 0), axis=1).astype(jnp.int32)
    n_used = (pends[N_EXPERTS - 1] // MOE_BLOCK).astype(jnp.int32).reshape(1)
    seg = jnp.stack([pends[0:N_EXPERTS], padded[0:N_EXPERTS]]).astype(jnp.int32)
    pos = []
    for k in range(TOP_K):
        e = route_t[:, ROUTE_EXPERT + k, :].astype(jnp.int32)
        hit = e[:, None, :] == jnp.arange(N_EXPERTS, dtype=jnp.int32)[None, :, None]
        b = jnp.sum(jnp.where(hit, base[:, 0:N_EXPERTS, None], 0), axis=1)
        pos.append((b + route_t[:, ROUTE_RANK + k, :].astype(jnp.int32)).reshape(n))
    return block_e, next_e, n_used, seg, pos


def _rope_tables(seq_len):
    m = HEAD_DIM // 4
    freqs = ROPE_BASE ** (-jnp.arange(m, dtype=F32) / m)
    t = jnp.arange(seq_len)
    row = (t // GRID_W).astype(F32)[:, None] * freqs[None, :]
    col = (t % GRID_W).astype(F32)[:, None] * freqs[None, :]
    cos = jnp.concatenate([jnp.cos(row), jnp.cos(row), jnp.cos(col), jnp.cos(col)], axis=-1)
    zero = jnp.zeros_like(row)
    sin_a = jnp.concatenate([zero, jnp.sin(row), zero, jnp.sin(col)], axis=-1)
    sin_b = jnp.concatenate([-jnp.sin(row), zero, -jnp.sin(col), zero], axis=-1)
    rep = LANES // HEAD_DIM
    return jnp.tile(cos, (1, rep)), jnp.tile(sin_a, (1, rep)), jnp.tile(sin_b, (1, rep))


def kernel(x, c, ctx, c_ctx, w_ada, b_ada, w_in, conv_w, attn_sink, gm_ws, gm_bs, w_out, ln1_g, ln1_b,
           w_rg, b_rg, w_re, b_re, w1, w3, w2, ln2_g, ln2_b):
    b_, s_, d_ = x.shape
    c_len = ctx.shape[1]
    depth = w_ada.shape[0]
    alpha = (2 * depth) ** 0.25
    n_lat = b_ * s_
    n_ctx = b_ * c_len
    ts = 1024

    cin = jnp.zeros((ADA_ROWS, d_), F32).at[0:b_].set(c).at[b_].set(c_ctx)
    mod = _ada_call(cin, w_ada, b_ada)

    cos, sin_a, sin_b = _rope_tables(s_)
    ones_c = jnp.ones((c_len, LANES), F32)
    zeros_c = jnp.zeros((c_len, LANES), F32)

    x_flat = x.reshape(n_lat, d_)
    c_flat = ctx.reshape(n_ctx, d_)
    x_off, c_off = 0, 0
    x_arr, c_arr = x_flat, c_flat

    for l in range(depth):
        last = l == depth - 1
        mx = mod[l, 0:b_].reshape(b_, 6, 1, d_)
        sh1, sc1, g1, sh2, sc2, g2 = (mx[:, i] for i in range(6))
        mc = jnp.broadcast_to(mod[l, b_].reshape(1, 6, 1, d_), (b_, 6, 1, d_))
        gm_w = gm_ws[l].astype(BF16)
        gm_b = jnp.repeat(gm_bs[l].T, GM_HEAD, axis=1)
        sink = attn_sink[l]

        qx, kvx, yx = _inproj_call(x_arr, x_off, b_, s_, ts, sc1, sh1, w_in, l, cos, sin_a, sin_b,
                                   conv_w[l], gm_w, gm_b)
        qc, kvc, yc = _inproj_call(c_arr, c_off, b_, c_len, c_len, mc[:, 1], mc[:, 0], w_in, l,
                                   ones_c, zeros_c, zeros_c, conv_w[l], gm_w, gm_b)
        att_x = _win_attn_call(sink, qx, kvx, kvc, b_, s_, c_len)

        w_r = jnp.zeros((d_, ROUTER_COLS), F32).at[:, 0:N_GROUPS].set(w_rg[l]) \
            .at[:, N_GROUPS:N_GROUPS + N_EXPERTS].set(w_re[l])
        w_r_hi = w_r.astype(BF16)
        w_r_lo = (w_r - w_r_hi.astype(F32)).astype(BF16)
        wr = jnp.concatenate([w_r_hi, w_r_lo], axis=1)
        br = jnp.zeros((1, ROUTER_COLS), F32).at[0, 0:N_GROUPS].set(b_rg[l]) \
            .at[0, N_GROUPS:N_GROUPS + N_EXPERTS].set(b_re[l])
        lng1, lnb1 = ln1_g[l].reshape(1, d_), ln1_b[l].reshape(1, d_)
        lng2, lnb2 = ln2_g[l].reshape(1, d_), ln2_b[l].reshape(1, d_)

        n_tot = n_lat if last else n_lat + n_ctx
        gvecs1 = jnp.concatenate([g1, mc[0:1, 2]], axis=0)
        scvecs2 = jnp.concatenate([sc2, mc[0:1, 4]], axis=0)
        shvecs2 = jnp.concatenate([sh2, mc[0:1, 3]], axis=0)
        if last:
            x1, h2t, route, route_t, tcnt = _outproj_call(att_x, att_x, yx, yx, x_arr, x_off, x_arr, x_off, n_lat, 0, s_,
                                                 w_out, l, gvecs1, lng1, lnb1, scvecs2, shvecs2, wr, br, alpha)
        else:
            att_c = _ctx_attn_call(sink, qc, kvc, b_, c_len)
            x1, h2t, route, route_t, tcnt = _outproj_call(att_x, att_c, yx, yc, x_arr, x_off, c_arr, c_off, n_lat, n_ctx,
                                                 s_, w_out, l, gvecs1, lng1, lnb1, scvecs2, shvecs2, wr, br, alpha)

        block_e, next_e, n_used, seg, pos = _slots(route_t, tcnt[:, 0, :])
        n_slots = block_e.shape[0] * MOE_BLOCK

        def per_tile(tile):
            return jnp.concatenate([p.reshape(n_tot // tile, 1, tile) for p in pos], axis=2)

        xs = _dispatch_call(seg, per_tile(DISPATCH_TILE), h2t, n_slots)
        ys = _moe_call(block_e, next_e, n_used, xs, w1, w3, w2, l)

        gvecs = jnp.concatenate([g2, mc[0:1, 5]], axis=0)
        out = _combine_call(per_tile(COMBINE_TILE), x1, route, gvecs, lng2, lnb2, ys, n_tot, s_ // COMBINE_TILE,
                            b_, alpha)
        x_arr, x_off = out, 0
        c_arr, c_off = out, n_lat

    return x_arr.reshape(b_, s_, d_)
```

```python
import functools

import jax
import jax.numpy as jnp
from jax import lax
from jax.experimental import pallas as pl
from jax.experimental.pallas import tpu as pltpu

F32 = jnp.float32
BF16 = jnp.bfloat16

D_MODEL = 1024
GRID_W = 64
HEAD_DIM = 64
N_HEADS = 8
N_KV_HEADS = 2
ATTN_WIDTH = N_HEADS * HEAD_DIM
KV_WIDTH = N_KV_HEADS * HEAD_DIM
WINDOW = 128
ATTN_SCALE = HEAD_DIM ** -0.5
LOG2E = 1.4426950408889634
ROPE_BASE = 10000.0
CONV_WIDTH = D_MODEL // 4
GM_WIDTH = D_MODEL // 4
GM_GROUPS = 4
GM_HEAD = GM_WIDTH // GM_GROUPS
CHUNK = 128
LOCAL_WIDTH = CONV_WIDTH + GM_WIDTH
IN_WIDTH = ATTN_WIDTH + 2 * KV_WIDTH + 3 * CONV_WIDTH + 2 * GM_WIDTH
QKV_WIDTH = ATTN_WIDTH + 2 * KV_WIDTH
KV_PACK_WIDTH = 4 * KV_WIDTH
N_GROUPS = 4
EXP_PER_GROUP = 8
N_EXPERTS = N_GROUPS * EXP_PER_GROUP
TOP_K = 2
D_EXPERT = D_MODEL // 2
MOE_BLOCK = 512
LN_EPS = 1e-6
NEG_INF = -1e30

LANES = 128
SUBLANES = 8
VMEM_LIMIT_BYTES = 48 * 1024 * 1024
N_DMA_PRIORITIES = 2

ROW_TILE_ROWS = D_MODEL // LANES
assert ROW_TILE_ROWS == SUBLANES
PACKED_TILE_ROWS = ROW_TILE_ROWS // 2
TOKENS_PER_PACKED_TILE = SUBLANES // PACKED_TILE_ROWS
assert TOKENS_PER_PACKED_TILE == 2

ROUTER_COLS = LANES
ROUTE_EXPERT = 0
ROUTE_GATE = TOP_K
ROUTE_RANK = 2 * TOP_K
ROPE_HALF_PAIR = HEAD_DIM // 4


def _bdot(a, b):
    return jnp.dot(a, b, preferred_element_type=F32)


def _split_bf16(a):
    hi = a.astype(BF16)
    lo = (a - hi.astype(F32)).astype(BF16)
    return hi, lo


def _layer_norm(r):
    mu = jnp.mean(r, axis=-1, keepdims=True)
    d = r - mu
    var = jnp.mean(d * d, axis=-1, keepdims=True)
    return d * lax.rsqrt(var + LN_EPS)


def _params(*sem):
    return pltpu.CompilerParams(dimension_semantics=sem, vmem_limit_bytes=VMEM_LIMIT_BYTES)


def _start_all(copies):
    for cp in copies:
        cp.start()


def _wait_all(copies):
    for cp in copies:
        cp.wait()


ADA_ROWS = 16
ADA_TILE = 1536


def _ada_kernel(c_ref, w_ref, b_ref, o_ref):
    act = jax.nn.silu(c_ref[...])
    ah, al = _split_bf16(act)
    wh, wl = _split_bf16(w_ref[...])
    o_ref[...] = _bdot(ah, wh) + _bdot(ah, wl) + _bdot(al, wh) + b_ref[...]


def _ada_call(cin, w_ada, b_ada):
    depth, d, n = w_ada.shape
    return pl.pallas_call(
        _ada_kernel,
        grid=(depth, n // ADA_TILE),
        in_specs=[
            pl.BlockSpec((ADA_ROWS, d), lambda l, j: (0, 0)),
            pl.BlockSpec((None, d, ADA_TILE), lambda l, j: (l, 0, j)),
            pl.BlockSpec((None, 1, ADA_TILE), lambda l, j: (l, 0, j)),
        ],
        out_specs=pl.BlockSpec((None, ADA_ROWS, ADA_TILE), lambda l, j: (l, 0, j)),
        out_shape=jax.ShapeDtypeStruct((depth, ADA_ROWS, n), F32),
        compiler_params=_params("parallel", "parallel"),
        name="ada_modulation",
    )(cin, w_ada, b_ada.reshape(depth, 1, n))


IN_TILE = 1024


def _inproj_kernel(x_ref, xp_ref, xn_ref, sc_ref, sh_ref, wf_ref, cos_ref, sa_ref, sb_ref,
                   cw_ref, gw_ref, gb_ref,
                   q_ref, kv_ref, y_ref, w_ref, *, ts):
    t = pl.program_id(1)
    nt = pl.num_programs(1)

    @pl.when((pl.program_id(0) == 0) & (t == 0))
    def _():
        w_ref[...] = wf_ref[...].astype(BF16)

    sc = 1.0 + sc_ref[...]
    sh = sh_ref[...]
    hx = (x_ref[...] * sc + sh).astype(BF16)

    pq = _bdot(hx, w_ref[:, 0:QKV_WIDTH])
    cos = cos_ref[...]
    sa = sa_ref[...]
    sb = sb_ref[...]

    def rope(z):
        return (z * cos + pltpu.roll(z, ROPE_HALF_PAIR, 1) * sa
                + pltpu.roll(z, LANES - ROPE_HALF_PAIR, 1) * sb)

    for j in range(ATTN_WIDTH // LANES):
        sl = slice(j * LANES, (j + 1) * LANES)
        q_ref[:, sl] = (rope(pq[:, sl]) * (ATTN_SCALE * LOG2E)).astype(BF16)
    kr = rope(pq[:, ATTN_WIDTH:ATTN_WIDTH + KV_WIDTH])
    vv = pq[:, ATTN_WIDTH + KV_WIDTH:QKV_WIDTH]
    for j, part in enumerate((kr, pltpu.roll(kr, HEAD_DIM, 1), vv, pltpu.roll(vv, HEAD_DIM, 1))):
        kv_ref[:, j * KV_WIDTH:(j + 1) * KV_WIDTH] = part.astype(BF16)

    pm = _bdot(hx, w_ref[:, QKV_WIDTH:IN_WIDTH])
    cb = pm[:, 0:CONV_WIDTH]
    u = pm[:, CONV_WIDTH:2 * CONV_WIDTH] * pm[:, 2 * CONV_WIDTH:3 * CONV_WIDTH]

    halo = jnp.concatenate([xp_ref[...], xn_ref[...]], axis=0)
    hh = (halo * sc + sh).astype(BF16)
    ph = _bdot(hh, w_ref[:, QKV_WIDTH + CONV_WIDTH:QKV_WIDTH + 3 * CONV_WIDTH])
    uh = ph[:, 0:CONV_WIDTH] * ph[:, CONV_WIDTH:2 * CONV_WIDTH]
    up_row = jnp.where(t > 0, uh[SUBLANES - 1:SUBLANES, :], 0.0)
    dn_row = jnp.where(t < nt - 1, uh[SUBLANES:SUBLANES + 1, :], 0.0)
    ridx = lax.broadcasted_iota(jnp.int32, (ts, CONV_WIDTH), 0)
    u_up = jnp.where(ridx == 0, up_row, pltpu.roll(u, 1, 0))
    u_dn = jnp.where(ridx == ts - 1, dn_row, pltpu.roll(u, ts - 1, 0))
    cw = cw_ref[...]
    y_conv = cb * (u_up * cw[0:1, :] + u * cw[1:2, :] + u_dn * cw[2:3, :])
    y_ref[:, 0:CONV_WIDTH] = y_conv.astype(BF16)

    gu = jax.nn.gelu(pm[:, 3 * CONV_WIDTH:3 * CONV_WIDTH + GM_WIDTH])
    gv = _layer_norm(jax.nn.gelu(pm[:, 3 * CONV_WIDTH + GM_WIDTH:3 * CONV_WIDTH + 2 * GM_WIDTH])).astype(BF16)
    lane = lax.broadcasted_iota(jnp.int32, (CHUNK, LANES), 1)
    zero = jnp.zeros((CHUNK, LANES), BF16)
    for c in range(ts // CHUNK):
        rows = slice(c * CHUNK, (c + 1) * CHUNK)
        for j in range(GM_WIDTH // LANES):
            cols = slice(j * LANES, (j + 1) * LANES)
            vp = gv[rows, cols]
            s = (_bdot(gw_ref[2 * j], jnp.where(lane < GM_HEAD, vp, zero))
                 + _bdot(gw_ref[2 * j + 1], jnp.where(lane >= GM_HEAD, vp, zero))
                 + gb_ref[:, cols])
            y_ref[rows, CONV_WIDTH + j * LANES:CONV_WIDTH + (j + 1) * LANES] = (gu[rows, cols] * s).astype(BF16)


def _inproj_call(x2d, row_off, n_seq, seq_len, ts, sc, sh, w_in, layer, cos, sa, sb, conv_w, gm_w, gm_b):
    nt = seq_len // ts
    off_t = row_off // ts
    off_8 = row_off // SUBLANES
    last_8 = x2d.shape[0] // SUBLANES - 1
    per_tile_8 = ts // SUBLANES
    n_out = n_seq * seq_len
    d = D_MODEL

    def tile_idx(b, t):
        return (off_t + b * nt + t, 0)

    def prev_idx(b, t):
        return (jnp.maximum(off_8 + (b * nt + t) * per_tile_8 - 1, 0), 0)

    def next_idx(b, t):
        return (jnp.minimum(off_8 + (b * nt + t + 1) * per_tile_8, last_8), 0)

    def out_idx(b, t):
        return (b * nt + t, 0)

    vec = pl.BlockSpec((None, 1, d), lambda b, t: (b, 0, 0))
    rope_spec = pl.BlockSpec((ts, LANES), lambda b, t: (t, 0))
    return pl.pallas_call(
        functools.partial(_inproj_kernel, ts=ts),
        grid=(n_seq, nt),
        in_specs=[
            pl.BlockSpec((ts, d), tile_idx),
            pl.BlockSpec((SUBLANES, d), prev_idx),
            pl.BlockSpec((SUBLANES, d), next_idx),
            vec, vec,
            pl.BlockSpec((None, d, IN_WIDTH), lambda b, t: (layer, 0, 0)),
            rope_spec, rope_spec, rope_spec,
            pl.BlockSpec((3, CONV_WIDTH), lambda b, t: (0, 0)),
            pl.BlockSpec((GM_GROUPS, CHUNK, CHUNK), lambda b, t: (0, 0, 0)),
            pl.BlockSpec((CHUNK, GM_WIDTH), lambda b, t: (0, 0)),
        ],
        out_specs=[
            pl.BlockSpec((ts, ATTN_WIDTH), out_idx),
            pl.BlockSpec((ts, KV_PACK_WIDTH), out_idx),
            pl.BlockSpec((ts, LOCAL_WIDTH), out_idx),
        ],
        out_shape=[
            jax.ShapeDtypeStruct((n_out, ATTN_WIDTH), BF16),
            jax.ShapeDtypeStruct((n_out, KV_PACK_WIDTH), BF16),
            jax.ShapeDtypeStruct((n_out, LOCAL_WIDTH), BF16),
        ],
        scratch_shapes=[pltpu.VMEM((d, IN_WIDTH), BF16)],
        compiler_params=_params("arbitrary", "arbitrary"),
        name="in_projection",
    )(x2d, x2d, x2d, sc, sh, w_in, cos, sa, sb, conv_w, gm_w, gm_b)


def _attn_body(q, keys, keys_sw, vals, vals_sw, halo_bias, sink_ref, o_ref, tq):
    nk = keys.shape[0]
    lane = lax.broadcasted_iota(jnp.int32, (nk, LANES), 1)
    lo = lane < HEAD_DIM
    zero = jnp.zeros((nk, LANES), BF16)
    rid = lax.broadcasted_iota(jnp.int32, (2 * tq, 1), 0)
    out_lo = lax.broadcasted_iota(jnp.int32, (2 * tq, LANES), 1) < HEAD_DIM
    nt_dims = (((1,), (1,)), ((), ()))
    gqa = N_HEADS // N_KV_HEADS
    biases = [None] * (nk // LANES)
    if halo_bias is not None:
        assert tq == LANES
        biases[0], biases[2] = halo_bias
    for h in range(N_KV_HEADS):
        k_own, k_oth = (keys, keys_sw) if h == 0 else (keys_sw, keys)
        v_own, v_oth = (vals, vals_sw) if h == 0 else (vals_sw, vals)
        kz = jnp.concatenate([jnp.where(lo, k_own, zero), jnp.where(lo, zero, k_oth)], axis=0)
        vz = jnp.concatenate([jnp.where(lo, v_own, zero), jnp.where(lo, zero, v_oth)], axis=0)
        c0 = h * gqa * HEAD_DIM
        qs = jnp.concatenate([q[:, c0:c0 + LANES], q[:, c0 + LANES:c0 + 2 * LANES]], axis=0)
        s_all = lax.dot_general(qs, kz, nt_dims, preferred_element_type=F32)
        probs, inv_den = [], []
        for par in range(2):
            tiles = []
            for j, bias in enumerate(biases):
                tile = s_all[:, par * nk + j * LANES:par * nk + (j + 1) * LANES]
                tiles.append(tile if bias is None else tile + bias)
            sink = jnp.where(rid < tq, sink_ref[h * gqa + par], sink_ref[h * gqa + 2 + par]) * LOG2E
            tile_max = tiles[0]
            for tile in tiles[1:]:
                tile_max = jnp.maximum(tile_max, tile)
            m = jnp.maximum(jnp.max(tile_max, axis=-1, keepdims=True), sink)
            tile_sum = None
            for tile in tiles:
                p = jnp.exp2(tile - m)
                tile_sum = p if tile_sum is None else tile_sum + p
                probs.append(p.astype(BF16))
            inv_den.append(1.0 / (jnp.sum(tile_sum, axis=-1, keepdims=True) + jnp.exp2(sink - m)))
        o = _bdot(jnp.concatenate(probs, axis=1), vz) * jnp.where(out_lo, inv_den[0], inv_den[1])
        o_ref[:, c0:c0 + LANES] = o[0:tq].astype(BF16)
        o_ref[:, c0 + LANES:c0 + 2 * LANES] = o[tq:2 * tq].astype(BF16)


def _split_kv(kv):
    return tuple(kv[:, j * KV_WIDTH:(j + 1) * KV_WIDTH] for j in range(4))


def _win_attn_kernel(sink_ref, q_ref, kvp_ref, kvc_ref, kvn_ref, kvx_ref, o_ref, *, tq):
    n = pl.program_id(1)
    nb = pl.num_programs(1)
    row = lax.broadcasted_iota(jnp.int32, (tq, tq), 0)
    col = lax.broadcasted_iota(jnp.int32, (tq, tq), 1)
    b_prev = jnp.where((col >= row) & (n > 0), 0.0, NEG_INF)
    b_next = jnp.where((col <= row) & (n < nb - 1), 0.0, NEG_INF)
    halo_bias = (jnp.concatenate([b_prev, b_prev], axis=0), jnp.concatenate([b_next, b_next], axis=0))

    kv = jnp.concatenate([kvp_ref[...], kvc_ref[...], kvn_ref[...], kvx_ref[...]], axis=0)
    _attn_body(q_ref[...], *_split_kv(kv), halo_bias, sink_ref, o_ref, tq)


def _win_attn_call(sink, q, kv, kv_ctx, n_seq, seq_len, ctx_len):
    tq = WINDOW
    nb = seq_len // tq

    def cur(b, n):
        return (b * nb + n, 0)

    def prev(b, n):
        return (b * nb + jnp.maximum(n - 1, 0), 0)

    def nxt(b, n):
        return (b * nb + jnp.minimum(n + 1, nb - 1), 0)

    return pl.pallas_call(
        functools.partial(_win_attn_kernel, tq=tq),
        grid=(n_seq, nb),
        in_specs=[pl.BlockSpec(memory_space=pltpu.SMEM), pl.BlockSpec((tq, ATTN_WIDTH), cur),
                  pl.BlockSpec((tq, KV_PACK_WIDTH), prev), pl.BlockSpec((tq, KV_PACK_WIDTH), cur),
                  pl.BlockSpec((tq, KV_PACK_WIDTH), nxt),
                  pl.BlockSpec((ctx_len, KV_PACK_WIDTH), lambda b, n: (b, 0))],
        out_specs=pl.BlockSpec((tq, ATTN_WIDTH), cur),
        out_shape=jax.ShapeDtypeStruct((n_seq * seq_len, ATTN_WIDTH), BF16),
        compiler_params=_params("parallel", "parallel"),
        name="window_attention",
    )(sink, q, kv, kv, kv, kv_ctx)


def _ctx_attn_kernel(sink_ref, q_ref, kv_ref, o_ref, *, tq):
    _attn_body(q_ref[...], *_split_kv(kv_ref[...]), None, sink_ref, o_ref, tq)


def _ctx_attn_call(sink, q, kv, n_seq, ctx_len):
    def blk(w):
        return pl.BlockSpec((ctx_len, w), lambda b: (b, 0))

    return pl.pallas_call(
        functools.partial(_ctx_attn_kernel, tq=ctx_len),
        grid=(n_seq,),
        in_specs=[pl.BlockSpec(memory_space=pltpu.SMEM), blk(ATTN_WIDTH), blk(KV_PACK_WIDTH)],
        out_specs=blk(ATTN_WIDTH),
        out_shape=jax.ShapeDtypeStruct((n_seq * ctx_len, ATTN_WIDTH), BF16),
        compiler_params=_params("parallel"),
        name="context_attention",
    )(sink, q, kv)


OUT_TILE = 512


def _outproj_kernel(attx_ref, attc_ref, yx_ref, yc_ref, xx_ref, xc_ref, wof_ref, g_ref, lng_ref, lnb_ref,
                    sc_ref, sh_ref, wr_ref, br_ref, before_ref,
                    x1_ref, h2v_ref, route_ref, routet_ref, cnt_ref, wo_ref, o_buf, lg_buf,
                    *, alpha, n_lat_tiles, n_tiles, has_ctx):
    step = pl.program_id(0)

    @pl.when(step == 0)
    def _():
        wo_ref[...] = wof_ref[...].astype(BF16)
        o_buf[1] = jnp.zeros(o_buf.shape[1:], F32)
        lg_buf[1] = jnp.zeros(lg_buf.shape[1:], F32)

    o = o_buf[(step + 1) % 2]
    lg = lg_buf[(step + 1) % 2]

    if has_ctx:
        att = jnp.where(jnp.minimum(step, n_tiles - 1) < n_lat_tiles, attx_ref[...], attc_ref[...])
        yloc = jnp.where(jnp.minimum(step, n_tiles - 1) < n_lat_tiles, yx_ref[...], yc_ref[...])
        xin = jnp.where(jnp.clip(step - 1, 0, n_tiles - 1) < n_lat_tiles, xx_ref[...], xc_ref[...])
    else:
        att, yloc, xin = attx_ref[...], yx_ref[...], xx_ref[...]
    o_buf[step % 2] = _bdot(att, wo_ref[0:ATTN_WIDTH, :]) + _bdot(yloc, wo_ref[ATTN_WIDTH:, :])
    x1 = _layer_norm(alpha * xin + g_ref[...] * o) * lng_ref[...] + lnb_ref[...]
    x1_ref[...] = x1
    h2 = x1 * (1.0 + sc_ref[...]) + sh_ref[...]
    half = D_MODEL // 2
    hi_bits = lax.bitcast_convert_type(h2[:, 0:half].astype(BF16).astype(F32), jnp.uint32)
    lo_bits = lax.bitcast_convert_type(h2[:, half:].astype(BF16).astype(F32), jnp.uint32)
    packed = hi_bits | (lo_bits >> 16)
    for c in range(PACKED_TILE_ROWS):
        h2v_ref[:, c, :, :] = packed[:, c * LANES:(c + 1) * LANES].reshape(h2.shape[0] // SUBLANES, SUBLANES, LANES)

    hh, hl = _split_bf16(h2)
    t1 = _bdot(hh, wr_ref[...])
    lg_buf[step % 2] = (t1[:, 0:ROUTER_COLS] + t1[:, ROUTER_COLS:] + _bdot(hl, wr_ref[:, 0:ROUTER_COLS])
                        + br_ref[...])

    ts = lg.shape[0]
    lane = lax.broadcasted_iota(jnp.int32, (ts, ROUTER_COLS), 1).astype(F32)
    big = jnp.float32(ROUTER_COLS)

    def top1(v):
        m = jnp.max(v, axis=-1, keepdims=True)
        return m, jnp.min(jnp.where(v == m, lane, big), axis=-1, keepdims=True)

    gl = jnp.where(lane < N_GROUPS, lg, NEG_INF)
    g_val, g_idx = top1(gl)
    lse = g_val + jnp.log(jnp.sum(jnp.exp(gl - g_val), axis=-1, keepdims=True))
    p_group = jnp.exp(g_val - lse)
    e_lo = N_GROUPS + EXP_PER_GROUP * g_idx
    el = jnp.where((lane >= e_lo) & (lane < e_lo + EXP_PER_GROUP), lg, NEG_INF)
    e1, l1 = top1(el)
    e2, l2 = top1(jnp.where(lane == l1, NEG_INF, el))
    z = jnp.exp(e2 - e1)
    gate1 = p_group / (1.0 + z)
    gate2 = p_group * z / (1.0 + z)
    x1id = l1 - N_GROUPS
    x2id = l2 - N_GROUPS

    sel1 = lane == x1id
    sel2 = lane == x2id
    onehot = jnp.where(sel1 | sel2, 1.0, 0.0)
    prefix = _bdot(before_ref[...], onehot.astype(BF16))
    rank1 = jnp.sum(jnp.where(sel1, prefix, 0.0), axis=-1, keepdims=True)
    rank2 = jnp.sum(jnp.where(sel2, prefix, 0.0), axis=-1, keepdims=True)
    route = jnp.zeros((ts, ROUTER_COLS), F32)
    for col, val in enumerate((x1id, x2id, gate1, gate2, rank1, rank2)):
        route = jnp.where(lane == col, val, route)
    route_ref[...] = route
    routet_ref[...] = route.T[0:SUBLANES, :]
    cnt_ref[...] = jnp.broadcast_to(jnp.sum(onehot, axis=0, keepdims=True), (SUBLANES, ROUTER_COLS))


def _outproj_call(att_x, att_c, y_x, y_c, x_arr, x_off, c_arr, c_off, n_lat, n_ctx, seq_len,
                  w_out, layer, gvecs, lng, lnb, scvecs, shvecs, wr, br, alpha):
    ts = OUT_TILE
    d = D_MODEL
    n_lat_tiles = n_lat // ts
    n_tiles = (n_lat + n_ctx) // ts
    tiles_per_seq = seq_len // ts
    n_seq = n_lat // seq_len
    xo, co = x_off // ts, c_off // ts

    def mm_tile(s):
        return jnp.minimum(s, n_tiles - 1)

    def ln_tile(s):
        return jnp.clip(s - 1, 0, n_tiles - 1)

    def route_tile(s):
        return jnp.clip(s - 2, 0, n_tiles - 1)

    def lat_loc(s):
        return (jnp.minimum(mm_tile(s), n_lat_tiles - 1), 0)

    def ctx_loc(s):
        return (jnp.maximum(mm_tile(s) - n_lat_tiles, 0), 0)

    def lat_in(s):
        return (xo + jnp.minimum(ln_tile(s), n_lat_tiles - 1), 0)

    def ctx_in(s):
        return (co + jnp.maximum(ln_tile(s) - n_lat_tiles, 0), 0)

    def vec_idx(s):
        i = ln_tile(s)
        return (jnp.where(i < n_lat_tiles, i // tiles_per_seq, n_seq), 0, 0)

    vecb = pl.BlockSpec((None, 1, d), vec_idx)
    vec0 = pl.BlockSpec((1, d), lambda i: (0, 0))
    return pl.pallas_call(
        functools.partial(_outproj_kernel, alpha=alpha, n_lat_tiles=n_lat_tiles, n_tiles=n_tiles,
                          has_ctx=n_ctx > 0),
        grid=(n_tiles + 2,),
        in_specs=[
            pl.BlockSpec((ts, ATTN_WIDTH), lat_loc), pl.BlockSpec((ts, ATTN_WIDTH), ctx_loc),
            pl.BlockSpec((ts, LOCAL_WIDTH), lat_loc), pl.BlockSpec((ts, LOCAL_WIDTH), ctx_loc),
            pl.BlockSpec((ts, d), lat_in), pl.BlockSpec((ts, d), ctx_in),
            pl.BlockSpec((None, d, d), lambda i: (layer, 0, 0)),
            vecb, vec0, vec0, vecb, vecb,
            pl.BlockSpec((d, 2 * ROUTER_COLS), lambda i: (0, 0)),
            pl.BlockSpec((1, ROUTER_COLS), lambda i: (0, 0)),
            pl.BlockSpec((ts, ts), lambda i: (0, 0)),
        ],
        out_specs=[pl.BlockSpec((ts, d), lambda s: (ln_tile(s), 0)),
                   pl.BlockSpec((ts // SUBLANES, PACKED_TILE_ROWS, SUBLANES, LANES),
                                lambda s: (ln_tile(s), 0, 0, 0)),
                   pl.BlockSpec((ts, ROUTER_COLS), lambda s: (route_tile(s), 0)),
                   pl.BlockSpec((None, SUBLANES, ts), lambda s: (route_tile(s), 0, 0)),
                   pl.BlockSpec((None, SUBLANES, ROUTER_COLS), lambda s: (route_tile(s), 0, 0))],
        out_shape=[jax.ShapeDtypeStruct((n_lat + n_ctx, d), F32),
                   jax.ShapeDtypeStruct(((n_lat + n_ctx) // SUBLANES, PACKED_TILE_ROWS, SUBLANES, LANES),
                                        jnp.uint32),
                   jax.ShapeDtypeStruct((n_lat + n_ctx, ROUTER_COLS), F32),
                   jax.ShapeDtypeStruct((n_tiles, SUBLANES, ts), F32),
                   jax.ShapeDtypeStruct((n_tiles, SUBLANES, ROUTER_COLS), F32)],
        scratch_shapes=[pltpu.VMEM((d, d), BF16), pltpu.VMEM((2, ts, d), F32),
                        pltpu.VMEM((2, ts, ROUTER_COLS), F32)],
        compiler_params=_params("arbitrary"),
        name="out_projection",
    )(att_x, att_c, y_x, y_c, x_arr, c_arr, w_out, gvecs, lng, lnb, scvecs, shvecs, wr, br,
      jnp.tril(jnp.ones((ts, ts), BF16), -1))


DISPATCH_TILE = 2048


def _dispatch_kernel(seg_ref, pos_ref, h_ref, xs_out, zbuf, zsem, sem):
    block_tiles = MOE_BLOCK // TOKENS_PER_PACKED_TILE

    @pl.when(pl.program_id(0) == 0)
    def _():
        zbuf[...] = jnp.zeros_like(zbuf)

        def zero_copy(e):
            first_tile = (seg_ref[0, e] - MOE_BLOCK) // TOKENS_PER_PACKED_TILE
            return pltpu.make_async_copy(zbuf, xs_out.at[pl.ds(first_tile, block_tiles)], zsem)

        for e in range(N_EXPERTS):
            @pl.when(seg_ref[1, e] > 0)
            def _():
                zero_copy(e).start()
        for e in range(N_EXPERTS):
            @pl.when(seg_ref[1, e] > 0)
            def _():
                zero_copy(e).wait()

        def tail_copy(b):
            return pltpu.make_async_copy(zbuf, xs_out.at[pl.ds(b * block_tiles, block_tiles)], zsem)

        n_blocks = xs_out.shape[0] // block_tiles
        first_unused = seg_ref[0, N_EXPERTS - 1] // MOE_BLOCK
        lax.fori_loop(first_unused, n_blocks, lambda b, c: (tail_copy(b).start(), c)[1], 0)
        lax.fori_loop(first_unused, n_blocks, lambda b, c: (tail_copy(b).wait(), c)[1], 0)

    def body(j, carry):
        for s in range(SUBLANES):
            t = SUBLANES * j + s
            for k in range(TOP_K):
                p = pos_ref[0, 0, k * DISPATCH_TILE + t]
                dst = xs_out.at[p >> 1, pl.ds(PACKED_TILE_ROWS * (p & 1), PACKED_TILE_ROWS), :]
                pltpu.make_async_copy(h_ref.at[j, :, s, :], dst, sem).start(priority=k)
        return carry

    lax.fori_loop(0, DISPATCH_TILE // SUBLANES, body, 0)
    for k in range(TOP_K):
        pltpu.make_async_copy(h_ref, h_ref, sem).wait()


def _dispatch_call(seg, pos, h2t, n_slots):
    n_tok = h2t.shape[0] * SUBLANES
    n_tiles = n_tok // DISPATCH_TILE
    grid_spec = pltpu.PrefetchScalarGridSpec(
        num_scalar_prefetch=1,
        grid=(n_tiles,),
        in_specs=[
            pl.BlockSpec((1, 1, TOP_K * DISPATCH_TILE), lambda i, seg: (i, 0, 0), memory_space=pltpu.SMEM),
            pl.BlockSpec((DISPATCH_TILE // SUBLANES, PACKED_TILE_ROWS, SUBLANES, LANES),
                         lambda i, seg: (i, 0, 0, 0)),
        ],
        out_specs=pl.BlockSpec(memory_space=pl.ANY),
        scratch_shapes=[pltpu.VMEM((MOE_BLOCK // TOKENS_PER_PACKED_TILE, SUBLANES, LANES), jnp.uint32),
                        pltpu.SemaphoreType.DMA(()), pltpu.SemaphoreType.DMA(())],
    )
    return pl.pallas_call(
        _dispatch_kernel,
        grid_spec=grid_spec,
        out_shape=jax.ShapeDtypeStruct((n_slots // TOKENS_PER_PACKED_TILE, SUBLANES, LANES), jnp.uint32),
        compiler_params=_params("arbitrary"),
        name="moe_dispatch",
    )(seg, pos, h2t)


def _moe_kernel(be_ref, ne_ref, nu_ref, xs_hbm, w1_hbm, w3_hbm, w2_hbm, ys_hbm, w1b, w3b, w2b, w1f, w3f, w2f,
                xbuf, obuf, isem, osem, wsem, *, layer):
    i = pl.program_id(0)
    n_used = nu_ref[0]
    slot = i % 2
    groups = MOE_BLOCK // SUBLANES
    half = D_MODEL // 2

    def in_copies(blk, at_slot):
        return [pltpu.make_async_copy(
            xs_hbm.at[pl.ds(blk * groups, groups), s // TOKENS_PER_PACKED_TILE,
                      pl.ds(PACKED_TILE_ROWS * (s % TOKENS_PER_PACKED_TILE), PACKED_TILE_ROWS), :],
            xbuf.at[at_slot, :, :, s, :], isem.at[at_slot]) for s in range(SUBLANES)]

    def out_copies(blk, at_slot):
        return [pltpu.make_async_copy(obuf.at[at_slot, :, :, s, :], ys_hbm.at[pl.ds(blk * groups, groups), s],
                                      osem.at[at_slot]) for s in range(SUBLANES)]

    def write_rows(at_slot, y):
        for c in range(ROW_TILE_ROWS):
            obuf[at_slot, :, c, :, :] = y[:, c * LANES:(c + 1) * LANES].reshape(groups, SUBLANES, LANES)

    @pl.when(i == 0)
    def _():
        _start_all(in_copies(0, 0))

    @pl.when(i + 1 < n_used)
    def _():
        _start_all(in_copies(i + 1, 1 - slot))

    def weight_copies(e):
        return [pltpu.make_async_copy(src.at[layer, e], dst, wsem)
                for src, dst in ((w1_hbm, w1f), (w3_hbm, w3f), (w2_hbm, w2f))]

    @pl.when(i == 0)
    def _():
        _start_all(weight_copies(be_ref[0]))

    @pl.when((i < n_used) & ((i == 0) | (be_ref[i] != be_ref[jnp.maximum(i - 1, 0)])))
    def _():
        _wait_all(weight_copies(be_ref[i]))
        w1b[...] = w1f[...].astype(BF16)
        w3b[...] = w3f[...].astype(BF16)
        w2b[...] = w2f[...].astype(BF16)

        @pl.when(ne_ref[i] != be_ref[i])
        def _():
            _start_all(weight_copies(ne_ref[i]))

    @pl.when(i < n_used)
    def _():
        _wait_all(in_copies(i, slot))
        p = jnp.concatenate([xbuf[slot, :, c, :, :].reshape(MOE_BLOCK, LANES) for c in range(PACKED_TILE_ROWS)],
                            axis=1)
        xa = lax.bitcast_convert_type(p & jnp.uint32(0xFFFF0000), F32).astype(BF16)
        xb = lax.bitcast_convert_type(p << 16, F32).astype(BF16)
        h1 = _bdot(xa, w1b[0:half, :]) + _bdot(xb, w1b[half:, :])
        h3 = _bdot(xa, w3b[0:half, :]) + _bdot(xb, w3b[half:, :])
        write_rows(slot, _bdot((jax.nn.silu(h1) * h3).astype(BF16), w2b[...]))
        _start_all(out_copies(i, slot))

        @pl.when(i >= 1)
        def _():
            _wait_all(out_copies(i - 1, 1 - slot))

        @pl.when(i == n_used - 1)
        def _():
            _wait_all(out_copies(i, slot))

    @pl.when(i == n_used)
    def _():
        obuf[0] = jnp.zeros(obuf.shape[1:], F32)

    @pl.when(i >= n_used)
    def _():
        _start_all(out_copies(i, 0))

    @pl.when(i == pl.num_programs(0) - 1)
    def _():
        lax.fori_loop(n_used, pl.num_programs(0), lambda b, c: (_wait_all(out_copies(b, 0)), c)[1], 0)


def _moe_call(block_e, next_e, n_used, xs, w1, w3, w2, layer):
    n_blocks = block_e.shape[0]
    n_slots = n_blocks * MOE_BLOCK
    d = D_MODEL
    groups = MOE_BLOCK // SUBLANES
    grid_spec = pltpu.PrefetchScalarGridSpec(
        num_scalar_prefetch=3,
        grid=(n_blocks,),
        in_specs=[pl.BlockSpec(memory_space=pl.ANY)] * 4,
        out_specs=pl.BlockSpec(memory_space=pl.ANY),
        scratch_shapes=[pltpu.VMEM((d, D_EXPERT), BF16), pltpu.VMEM((d, D_EXPERT), BF16),
                        pltpu.VMEM((D_EXPERT, d), BF16),
                        pltpu.VMEM((d, D_EXPERT), F32), pltpu.VMEM((d, D_EXPERT), F32),
                        pltpu.VMEM((D_EXPERT, d), F32),
                        pltpu.VMEM((2, groups, PACKED_TILE_ROWS, SUBLANES, LANES), jnp.uint32),
                        pltpu.VMEM((2, groups, ROW_TILE_ROWS, SUBLANES, LANES), F32),
                        pltpu.SemaphoreType.DMA((2,)), pltpu.SemaphoreType.DMA((2,)),
                        pltpu.SemaphoreType.DMA(())],
    )
    xs_view = (n_slots // SUBLANES, SUBLANES // TOKENS_PER_PACKED_TILE, SUBLANES, LANES)
    ys_view = (n_slots // SUBLANES, SUBLANES, ROW_TILE_ROWS, LANES)
    ys = pl.pallas_call(
        functools.partial(_moe_kernel, layer=layer),
        grid_spec=grid_spec,
        out_shape=jax.ShapeDtypeStruct(ys_view, F32),
        compiler_params=_params("arbitrary"),
        name="moe_experts",
    )(block_e, next_e, n_used, xs.reshape(xs_view), w1, w3, w2)
    return ys.reshape(n_slots, ROW_TILE_ROWS, LANES)


COMBINE_TILE = 256


def _combine_kernel(pos0_ref, posn_ref, x1_ref, route_ref, g_ref, lng_ref, lnb_ref, ys_hbm, o_ref, ybuf, sem,
                    *, alpha):
    i = pl.program_id(0)
    nb = pl.num_programs(0)
    n_rows = TOP_K * COMBINE_TILE

    def issue(pos_ref, slot):
        def body(j, carry):
            for s in range(SUBLANES):
                pltpu.make_async_copy(ys_hbm.at[pos_ref[0, 0, SUBLANES * j + s]], ybuf.at[slot, j, :, s, :],
                                      sem.at[slot]).start(priority=s % N_DMA_PRIORITIES)
            return carry
        lax.fori_loop(0, n_rows // SUBLANES, body, 0)

    @pl.when(i == 0)
    def _():
        issue(pos0_ref, 0)

    @pl.when(i + 1 < nb)
    def _():
        issue(posn_ref, (i + 1) % 2)

    slot = i % 2
    pltpu.make_async_copy(ybuf.at[slot], ybuf.at[slot], sem.at[slot]).wait()
    yrows = jnp.concatenate([ybuf[slot, :, c, :, :].reshape(n_rows, LANES) for c in range(ROW_TILE_ROWS)], axis=1)
    route = route_ref[...]
    y = (yrows[0:COMBINE_TILE, :] * route[:, ROUTE_GATE:ROUTE_GATE + 1]
         + yrows[COMBINE_TILE:, :] * route[:, ROUTE_GATE + 1:ROUTE_GATE + 2])
    o_ref[...] = _layer_norm(alpha * x1_ref[...] + g_ref[...] * y) * lng_ref[...] + lnb_ref[...]


def _combine_call(pos, x1, route, gvecs, lng, lnb, ys, n_tok, tiles_per_seq, n_seq, alpha):
    d = D_MODEL
    n_tiles = n_tok // COMBINE_TILE
    n_lat_tiles = tiles_per_seq * n_seq

    def g_idx(i):
        return (jnp.where(i < n_lat_tiles, i // tiles_per_seq, n_seq), 0, 0)

    smem_blk = (1, 1, TOP_K * COMBINE_TILE)
    return pl.pallas_call(
        functools.partial(_combine_kernel, alpha=alpha),
        grid=(n_tiles,),
        in_specs=[
            pl.BlockSpec(smem_blk, lambda i: (0, 0, 0), memory_space=pltpu.SMEM),
            pl.BlockSpec(smem_blk, lambda i: (jnp.minimum(i + 1, n_tiles - 1), 0, 0), memory_space=pltpu.SMEM),
            pl.BlockSpec((COMBINE_TILE, d), lambda i: (i, 0)),
            pl.BlockSpec((COMBINE_TILE, ROUTER_COLS), lambda i: (i, 0)),
            pl.BlockSpec((None, 1, d), g_idx),
            pl.BlockSpec((1, d), lambda i: (0, 0)),
            pl.BlockSpec((1, d), lambda i: (0, 0)),
            pl.BlockSpec(memory_space=pl.ANY),
        ],
        out_specs=pl.BlockSpec((COMBINE_TILE, d), lambda i: (i, 0)),
        out_shape=jax.ShapeDtypeStruct((n_tok, d), F32),
        scratch_shapes=[pltpu.VMEM((2, TOP_K * COMBINE_TILE // SUBLANES, ROW_TILE_ROWS, SUBLANES, LANES), F32),
                        pltpu.SemaphoreType.DMA((2,))],
        compiler_params=_params("arbitrary"),
        name="moe_combine",
    )(pos, pos, x1, route, gvecs, lng, lnb, ys)


def _slots(route_t, tile_counts):
    n_tiles = tile_counts.shape[0]
    n = n_tiles * route_t.shape[2]
    tc = tile_counts.astype(jnp.int32)
    counts = jnp.sum(tc, axis=0)
    padded = (counts + MOE_BLOCK - 1) // MOE_BLOCK * MOE_BLOCK
    pends = jnp.cumsum(padded)
    base = (pends - padded)[None, :] + jnp.cumsum(tc, axis=0) - tc
    n_blocks = -(-n * TOP_K // MOE_BLOCK) + N_EXPERTS
    block_start = jnp.arange(n_blocks, dtype=jnp.int32) * MOE_BLOCK
    block_e = jnp.minimum(jnp.sum(pends[None, 0:N_EXPERTS] <= block_start[:, None], axis=1),
                          N_EXPERTS - 1).astype(jnp.int32)
    ids = jnp.arange(N_EXPERTS, dtype=jnp.int32)
    later = (ids[None, :] > ids[:, None]) & (padded[None, 0:N_EXPERTS] > 0)
    next_nonempty = jnp.min(jnp.where(later, ids[None, :], N_EXPERTS), axis=1)
    next_of = jnp.where(next_nonempty < N_EXPERTS, next_nonempty, ids)
    next_e = jnp.sum(jnp.where(block_e[:, None] == ids[None, :], next_of[None, :], 0), axis=1).astype(jnp.int32)
    n_used = (pends[N_EXPERTS - 1] // MOE_BLOCK).astype(jnp.int32).reshape(1)
    seg = jnp.stack([pends[0:N_EXPERTS], padded[0:N_EXPERTS]]).astype(jnp.int32)
    pos = []
    for k in range(TOP_K):
        e = route_t[:, ROUTE_EXPERT + k, :].astype(jnp.int32)
        hit = e[:, None, :] == jnp.arange(N_EXPERTS, dtype=jnp.int32)[None, :, None]
        b = jnp.sum(jnp.where(hit, base[:, 0:N_EXPERTS, None], 0), axis=1)
        pos.append((b + route_t[:, ROUTE_RANK + k, :].astype(jnp.int32)).reshape(n))
    return block_e, next_e, n_used, seg, pos


def _rope_tables(seq_len):
    m = HEAD_DIM // 4
    freqs = ROPE_BASE ** (-jnp.arange(m, dtype=F32) / m)
    t = jnp.arange(seq_len)
    row = (t // GRID_W).astype(F32)[:, None] * freqs[None, :]
    col = (t % GRID_W).astype(F32)[:, None] * freqs[None, :]
    cos = jnp.concatenate([jnp.cos(row), jnp.cos(row), jnp.cos(col), jnp.cos(col)], axis=-1)
    zero = jnp.zeros_like(row)
    sin_a = jnp.concatenate([zero, jnp.sin(row), zero, jnp.sin(col)], axis=-1)
    sin_b = jnp.concatenate([-jnp.sin(row), zero, -jnp.sin(col), zero], axis=-1)
    rep = LANES // HEAD_DIM
    return jnp.tile(cos, (1, rep)), jnp.tile(sin_a, (1, rep)), jnp.tile(sin_b, (1, rep))


def kernel(x, c, ctx, c_ctx, w_ada, b_ada, w_in, conv_w, attn_sink, gm_ws, gm_bs, w_out, ln1_g, ln1_b,
           w_rg, b_rg, w_re, b_re, w1, w3, w2, ln2_g, ln2_b):
    b_, s_, d_ = x.shape
    c_len = ctx.shape[1]
    depth = w_ada.shape[0]
    alpha = (2 * depth) ** 0.25
    n_lat = b_ * s_
    n_ctx = b_ * c_len

    cin = jnp.zeros((ADA_ROWS, d_), F32).at[0:b_].set(c).at[b_].set(c_ctx)
    mod = _ada_call(cin, w_ada, b_ada)

    cos, sin_a, sin_b = _rope_tables(s_)
    ones_c = jnp.ones((c_len, LANES), F32)
    zeros_c = jnp.zeros((c_len, LANES), F32)

    x_flat = x.reshape(n_lat, d_)
    c_flat = ctx.reshape(n_ctx, d_)
    x_off, c_off = 0, 0
    x_arr, c_arr = x_flat, c_flat

    for l in range(depth):
        last = l == depth - 1
        mx = mod[l, 0:b_].reshape(b_, 6, 1, d_)
        sh1, sc1, g1, sh2, sc2, g2 = (mx[:, i] for i in range(6))
        mc = jnp.broadcast_to(mod[l, b_].reshape(1, 6, 1, d_), (b_, 6, 1, d_))
        gm_w = gm_ws[l].astype(BF16)
        gm_b = jnp.repeat(gm_bs[l].T, GM_HEAD, axis=1)
        sink = attn_sink[l]

        qx, kvx, yx = _inproj_call(x_arr, x_off, b_, s_, IN_TILE, sc1, sh1, w_in, l, cos, sin_a, sin_b,
                                   conv_w[l], gm_w, gm_b)
        qc, kvc, yc = _inproj_call(c_arr, c_off, b_, c_len, c_len, mc[:, 1], mc[:, 0], w_in, l,
                                   ones_c, zeros_c, zeros_c, conv_w[l], gm_w, gm_b)
        att_x = _win_attn_call(sink, qx, kvx, kvc, b_, s_, c_len)

        w_r = jnp.zeros((d_, ROUTER_COLS), F32).at[:, 0:N_GROUPS].set(w_rg[l]) \
            .at[:, N_GROUPS:N_GROUPS + N_EXPERTS].set(w_re[l])
        w_r_hi = w_r.astype(BF16)
        w_r_lo = (w_r - w_r_hi.astype(F32)).astype(BF16)
        wr = jnp.concatenate([w_r_hi, w_r_lo], axis=1)
        br = jnp.zeros((1, ROUTER_COLS), F32).at[0, 0:N_GROUPS].set(b_rg[l]) \
            .at[0, N_GROUPS:N_GROUPS + N_EXPERTS].set(b_re[l])
        lng1, lnb1 = ln1_g[l].reshape(1, d_), ln1_b[l].reshape(1, d_)
        lng2, lnb2 = ln2_g[l].reshape(1, d_), ln2_b[l].reshape(1, d_)

        n_tot = n_lat if last else n_lat + n_ctx
        gvecs1 = jnp.concatenate([g1, mc[0:1, 2]], axis=0)
        scvecs2 = jnp.concatenate([sc2, mc[0:1, 4]], axis=0)
        shvecs2 = jnp.concatenate([sh2, mc[0:1, 3]], axis=0)
        if last:
            x1, h2t, route, route_t, tcnt = _outproj_call(att_x, att_x, yx, yx, x_arr, x_off, x_arr, x_off, n_lat, 0, s_,
                                                 w_out, l, gvecs1, lng1, lnb1, scvecs2, shvecs2, wr, br, alpha)
        else:
            att_c = _ctx_attn_call(sink, qc, kvc, b_, c_len)
            x1, h2t, route, route_t, tcnt = _outproj_call(att_x, att_c, yx, yc, x_arr, x_off, c_arr, c_off, n_lat, n_ctx,
                                                 s_, w_out, l, gvecs1, lng1, lnb1, scvecs2, shvecs2, wr, br, alpha)

        block_e, next_e, n_used, seg, pos = _slots(route_t, tcnt[:, 0, :])
        n_slots = block_e.shape[0] * MOE_BLOCK

        def per_tile(tile):
            return jnp.concatenate([p.reshape(n_tot // tile, 1, tile) for p in pos], axis=2)

        xs = _dispatch_call(seg, per_tile(DISPATCH_TILE), h2t, n_slots)
        ys = _moe_call(block_e, next_e, n_used, xs, w1, w3, w2, l)

        gvecs = jnp.concatenate([g2, mc[0:1, 5]], axis=0)
        out = _combine_call(per_tile(COMBINE_TILE), x1, route, gvecs, lng2, lnb2, ys, n_tot, s_ // COMBINE_TILE,
                            b_, alpha)
        x_arr, x_off = out, 0
        c_arr, c_off = out, n_lat

    return x_arr.reshape(b_, s_, d_)
```

```python
import functools

import jax
import jax.numpy as jnp
from jax import lax
from jax.experimental import pallas as pl
from jax.experimental.pallas import tpu as pltpu

F32 = jnp.float32
BF16 = jnp.bfloat16

D_MODEL = 1024
GRID_W = 64
HEAD_DIM = 64
N_HEADS = 8
N_KV_HEADS = 2
ATTN_WIDTH = N_HEADS * HEAD_DIM
KV_WIDTH = N_KV_HEADS * HEAD_DIM
WINDOW = 128
ATTN_SCALE = HEAD_DIM ** -0.5
LOG2E = 1.4426950408889634
ROPE_BASE = 10000.0
CONV_WIDTH = D_MODEL // 4
GM_WIDTH = D_MODEL // 4
GM_GROUPS = 4
GM_HEAD = GM_WIDTH // GM_GROUPS
CHUNK = 128
LOCAL_WIDTH = CONV_WIDTH + GM_WIDTH
IN_WIDTH = ATTN_WIDTH + 2 * KV_WIDTH + 3 * CONV_WIDTH + 2 * GM_WIDTH
QKV_WIDTH = ATTN_WIDTH + 2 * KV_WIDTH
KV_PACK_WIDTH = 4 * KV_WIDTH
N_GROUPS = 4
EXP_PER_GROUP = 8
N_EXPERTS = N_GROUPS * EXP_PER_GROUP
TOP_K = 2
D_EXPERT = D_MODEL // 2
MOE_BLOCK = 512
LN_EPS = 1e-6
NEG_INF = -1e30

LANES = 128
SUBLANES = 8
VMEM_LIMIT_BYTES = 48 * 1024 * 1024
N_DMA_PRIORITIES = 2

ROW_TILE_ROWS = D_MODEL // LANES
assert ROW_TILE_ROWS == SUBLANES
PACKED_TILE_ROWS = ROW_TILE_ROWS // 2
TOKENS_PER_PACKED_TILE = SUBLANES // PACKED_TILE_ROWS
assert TOKENS_PER_PACKED_TILE == 2

ROUTER_COLS = LANES
ROUTE_EXPERT = 0
ROUTE_GATE = TOP_K
ROUTE_RANK = 2 * TOP_K
ROPE_HALF_PAIR = HEAD_DIM // 4


def _bdot(a, b):
    return jnp.dot(a, b, preferred_element_type=F32)


def _split_bf16(a):
    hi = a.astype(BF16)
    lo = (a - hi.astype(F32)).astype(BF16)
    return hi, lo


def _layer_norm(r):
    mu = jnp.mean(r, axis=-1, keepdims=True)
    d = r - mu
    var = jnp.mean(d * d, axis=-1, keepdims=True)
    return d * lax.rsqrt(var + LN_EPS)


def _params(*sem, flags=None):
    return pltpu.CompilerParams(dimension_semantics=sem, vmem_limit_bytes=VMEM_LIMIT_BYTES, flags=flags)


def _start_all(copies):
    for cp in copies:
        cp.start()


def _wait_all(copies):
    for cp in copies:
        cp.wait()


ADA_ROWS = 16
ADA_TILE = 1536


def _ada_kernel(c_ref, w_ref, b_ref, o_ref):
    act = jax.nn.silu(c_ref[...])
    ah, al = _split_bf16(act)
    wh, wl = _split_bf16(w_ref[...])
    o_ref[...] = _bdot(ah, wh) + _bdot(ah, wl) + _bdot(al, wh) + b_ref[...]


def _ada_call(cin, w_ada, b_ada):
    depth, d, n = w_ada.shape
    return pl.pallas_call(
        _ada_kernel,
        grid=(depth, n // ADA_TILE),
        in_specs=[
            pl.BlockSpec((ADA_ROWS, d), lambda l, j: (0, 0)),
            pl.BlockSpec((None, d, ADA_TILE), lambda l, j: (l, 0, j)),
            pl.BlockSpec((None, 1, ADA_TILE), lambda l, j: (l, 0, j)),
        ],
        out_specs=pl.BlockSpec((None, ADA_ROWS, ADA_TILE), lambda l, j: (l, 0, j)),
        out_shape=jax.ShapeDtypeStruct((depth, ADA_ROWS, n), F32),
        compiler_params=_params("parallel", "parallel"),
        name="ada_modulation",
    )(cin, w_ada, b_ada.reshape(depth, 1, n))


def _inproj_kernel(x_ref, xp_ref, xn_ref, sc_ref, sh_ref, wf_ref, cos_ref, sa_ref, sb_ref,
                   cw_ref, gw_ref, gb_ref,
                   q_ref, kv_ref, y_ref, w_ref, *, ts):
    t = pl.program_id(1)
    nt = pl.num_programs(1)

    @pl.when((pl.program_id(0) == 0) & (t == 0))
    def _():
        w_ref[...] = wf_ref[...].astype(BF16)

    sc = 1.0 + sc_ref[...]
    sh = sh_ref[...]
    hx = (x_ref[...] * sc + sh).astype(BF16)

    pq = _bdot(hx, w_ref[:, 0:QKV_WIDTH])
    cos = cos_ref[...]
    sa = sa_ref[...]
    sb = sb_ref[...]

    def rope(z):
        return (z * cos + pltpu.roll(z, ROPE_HALF_PAIR, 1) * sa
                + pltpu.roll(z, LANES - ROPE_HALF_PAIR, 1) * sb)

    for j in range(ATTN_WIDTH // LANES):
        sl = slice(j * LANES, (j + 1) * LANES)
        q_ref[:, sl] = (rope(pq[:, sl]) * (ATTN_SCALE * LOG2E)).astype(BF16)
    kr = rope(pq[:, ATTN_WIDTH:ATTN_WIDTH + KV_WIDTH])
    vv = pq[:, ATTN_WIDTH + KV_WIDTH:QKV_WIDTH]
    for j, part in enumerate((kr, pltpu.roll(kr, HEAD_DIM, 1), vv, pltpu.roll(vv, HEAD_DIM, 1))):
        kv_ref[:, j * KV_WIDTH:(j + 1) * KV_WIDTH] = part.astype(BF16)

    pm = _bdot(hx, w_ref[:, QKV_WIDTH:IN_WIDTH])
    cb = pm[:, 0:CONV_WIDTH]
    u = pm[:, CONV_WIDTH:2 * CONV_WIDTH] * pm[:, 2 * CONV_WIDTH:3 * CONV_WIDTH]

    halo = jnp.concatenate([xp_ref[...], xn_ref[...]], axis=0)
    hh = (halo * sc + sh).astype(BF16)
    ph = _bdot(hh, w_ref[:, QKV_WIDTH + CONV_WIDTH:QKV_WIDTH + 3 * CONV_WIDTH])
    uh = ph[:, 0:CONV_WIDTH] * ph[:, CONV_WIDTH:2 * CONV_WIDTH]
    up_row = jnp.where(t > 0, uh[SUBLANES - 1:SUBLANES, :], 0.0)
    dn_row = jnp.where(t < nt - 1, uh[SUBLANES:SUBLANES + 1, :], 0.0)
    ridx = lax.broadcasted_iota(jnp.int32, (ts, CONV_WIDTH), 0)
    u_up = jnp.where(ridx == 0, up_row, pltpu.roll(u, 1, 0))
    u_dn = jnp.where(ridx == ts - 1, dn_row, pltpu.roll(u, ts - 1, 0))
    cw = cw_ref[...]
    y_conv = cb * (u_up * cw[0:1, :] + u * cw[1:2, :] + u_dn * cw[2:3, :])
    y_ref[:, 0:CONV_WIDTH] = y_conv.astype(BF16)

    gu = jax.nn.gelu(pm[:, 3 * CONV_WIDTH:3 * CONV_WIDTH + GM_WIDTH])
    gv = _layer_norm(jax.nn.gelu(pm[:, 3 * CONV_WIDTH + GM_WIDTH:3 * CONV_WIDTH + 2 * GM_WIDTH])).astype(BF16)
    lane = lax.broadcasted_iota(jnp.int32, (CHUNK, LANES), 1)
    zero = jnp.zeros((CHUNK, LANES), BF16)
    for c in range(ts // CHUNK):
        rows = slice(c * CHUNK, (c + 1) * CHUNK)
        for j in range(GM_WIDTH // LANES):
            cols = slice(j * LANES, (j + 1) * LANES)
            vp = gv[rows, cols]
            s = (_bdot(gw_ref[2 * j], jnp.where(lane < GM_HEAD, vp, zero))
                 + _bdot(gw_ref[2 * j + 1], jnp.where(lane >= GM_HEAD, vp, zero))
                 + gb_ref[:, cols])
            y_ref[rows, CONV_WIDTH + j * LANES:CONV_WIDTH + (j + 1) * LANES] = (gu[rows, cols] * s).astype(BF16)


def _inproj_call(x2d, row_off, n_seq, seq_len, ts, sc, sh, w_in, layer, cos, sa, sb, conv_w, gm_w, gm_b):
    nt = seq_len // ts
    off_t = row_off // ts
    off_8 = row_off // SUBLANES
    last_8 = x2d.shape[0] // SUBLANES - 1
    per_tile_8 = ts // SUBLANES
    n_out = n_seq * seq_len
    d = D_MODEL

    def tile_idx(b, t):
        return (off_t + b * nt + t, 0)

    def prev_idx(b, t):
        return (jnp.maximum(off_8 + (b * nt + t) * per_tile_8 - 1, 0), 0)

    def next_idx(b, t):
        return (jnp.minimum(off_8 + (b * nt + t + 1) * per_tile_8, last_8), 0)

    def out_idx(b, t):
        return (b * nt + t, 0)

    vec = pl.BlockSpec((None, 1, d), lambda b, t: (b, 0, 0))
    rope_spec = pl.BlockSpec((ts, LANES), lambda b, t: (t, 0))
    return pl.pallas_call(
        functools.partial(_inproj_kernel, ts=ts),
        grid=(n_seq, nt),
        in_specs=[
            pl.BlockSpec((ts, d), tile_idx),
            pl.BlockSpec((SUBLANES, d), prev_idx),
            pl.BlockSpec((SUBLANES, d), next_idx),
            vec, vec,
            pl.BlockSpec((None, d, IN_WIDTH), lambda b, t: (layer, 0, 0)),
            rope_spec, rope_spec, rope_spec,
            pl.BlockSpec((3, CONV_WIDTH), lambda b, t: (0, 0)),
            pl.BlockSpec((GM_GROUPS, CHUNK, CHUNK), lambda b, t: (0, 0, 0)),
            pl.BlockSpec((CHUNK, GM_WIDTH), lambda b, t: (0, 0)),
        ],
        out_specs=[
            pl.BlockSpec((ts, ATTN_WIDTH), out_idx),
            pl.BlockSpec((ts, KV_PACK_WIDTH), out_idx),
            pl.BlockSpec((ts, LOCAL_WIDTH), out_idx),
        ],
        out_shape=[
            jax.ShapeDtypeStruct((n_out, ATTN_WIDTH), BF16),
            jax.ShapeDtypeStruct((n_out, KV_PACK_WIDTH), BF16),
            jax.ShapeDtypeStruct((n_out, LOCAL_WIDTH), BF16),
        ],
        scratch_shapes=[pltpu.VMEM((d, IN_WIDTH), BF16)],
        compiler_params=_params("arbitrary", "arbitrary"),
        name="in_projection",
    )(x2d, x2d, x2d, sc, sh, w_in, cos, sa, sb, conv_w, gm_w, gm_b)


def _attn_body(q, keys, keys_sw, vals, vals_sw, halo_bias, sink_ref, o_ref, tq):
    nk = keys.shape[0]
    lane = lax.broadcasted_iota(jnp.int32, (nk, LANES), 1)
    lo = lane < HEAD_DIM
    zero = jnp.zeros((nk, LANES), BF16)
    rid = lax.broadcasted_iota(jnp.int32, (2 * tq, 1), 0)
    out_lo = lax.broadcasted_iota(jnp.int32, (2 * tq, LANES), 1) < HEAD_DIM
    nt_dims = (((1,), (1,)), ((), ()))
    gqa = N_HEADS // N_KV_HEADS
    biases = [None] * (nk // LANES)
    if halo_bias is not None:
        assert tq == LANES
        biases[0], biases[2] = halo_bias
    for h in range(N_KV_HEADS):
        k_own, k_oth = (keys, keys_sw) if h == 0 else (keys_sw, keys)
        v_own, v_oth = (vals, vals_sw) if h == 0 else (vals_sw, vals)
        kz = jnp.concatenate([jnp.where(lo, k_own, zero), jnp.where(lo, zero, k_oth)], axis=0)
        vz = jnp.concatenate([jnp.where(lo, v_own, zero), jnp.where(lo, zero, v_oth)], axis=0)
        c0 = h * gqa * HEAD_DIM
        qs = jnp.concatenate([q[:, c0:c0 + LANES], q[:, c0 + LANES:c0 + 2 * LANES]], axis=0)
        s_all = lax.dot_general(qs, kz, nt_dims, preferred_element_type=F32)
        probs, inv_den = [], []
        for par in range(2):
            tiles = []
            for j, bias in enumerate(biases):
                tile = s_all[:, par * nk + j * LANES:par * nk + (j + 1) * LANES]
                tiles.append(tile if bias is None else tile + bias)
            sink = jnp.where(rid < tq, sink_ref[h * gqa + par], sink_ref[h * gqa + 2 + par]) * LOG2E
            tile_max = tiles[0]
            for tile in tiles[1:]:
                tile_max = jnp.maximum(tile_max, tile)
            m = jnp.maximum(jnp.max(tile_max, axis=-1, keepdims=True), sink)
            tile_sum = None
            for tile in tiles:
                p = jnp.exp2(tile - m)
                tile_sum = p if tile_sum is None else tile_sum + p
                probs.append(p.astype(BF16))
            inv_den.append(1.0 / (jnp.sum(tile_sum, axis=-1, keepdims=True) + jnp.exp2(sink - m)))
        o = _bdot(jnp.concatenate(probs, axis=1), vz) * jnp.where(out_lo, inv_den[0], inv_den[1])
        o_ref[:, c0:c0 + LANES] = o[0:tq].astype(BF16)
        o_ref[:, c0 + LANES:c0 + 2 * LANES] = o[tq:2 * tq].astype(BF16)


def _split_kv(kv):
    return tuple(kv[:, j * KV_WIDTH:(j + 1) * KV_WIDTH] for j in range(4))


def _win_attn_kernel(sink_ref, q_ref, kvp_ref, kvc_ref, kvn_ref, kvx_ref, o_ref, *, tq):
    n = pl.program_id(1)
    nb = pl.num_programs(1)
    row = lax.broadcasted_iota(jnp.int32, (tq, tq), 0)
    col = lax.broadcasted_iota(jnp.int32, (tq, tq), 1)
    b_prev = jnp.where((col >= row) & (n > 0), 0.0, NEG_INF)
    b_next = jnp.where((col <= row) & (n < nb - 1), 0.0, NEG_INF)
    halo_bias = (jnp.concatenate([b_prev, b_prev], axis=0), jnp.concatenate([b_next, b_next], axis=0))

    kv = jnp.concatenate([kvp_ref[...], kvc_ref[...], kvn_ref[...], kvx_ref[...]], axis=0)
    _attn_body(q_ref[...], *_split_kv(kv), halo_bias, sink_ref, o_ref, tq)


def _win_attn_call(sink, q, kv, kv_ctx, n_seq, seq_len, ctx_len):
    tq = WINDOW
    nb = seq_len // tq

    def cur(b, n):
        return (b * nb + n, 0)

    def prev(b, n):
        return (b * nb + jnp.maximum(n - 1, 0), 0)

    def nxt(b, n):
        return (b * nb + jnp.minimum(n + 1, nb - 1), 0)

    return pl.pallas_call(
        functools.partial(_win_attn_kernel, tq=tq),
        grid=(n_seq, nb),
        in_specs=[pl.BlockSpec(memory_space=pltpu.SMEM), pl.BlockSpec((tq, ATTN_WIDTH), cur),
                  pl.BlockSpec((tq, KV_PACK_WIDTH), prev), pl.BlockSpec((tq, KV_PACK_WIDTH), cur),
                  pl.BlockSpec((tq, KV_PACK_WIDTH), nxt),
                  pl.BlockSpec((ctx_len, KV_PACK_WIDTH), lambda b, n: (b, 0))],
        out_specs=pl.BlockSpec((tq, ATTN_WIDTH), cur),
        out_shape=jax.ShapeDtypeStruct((n_seq * seq_len, ATTN_WIDTH), BF16),
        compiler_params=_params("parallel", "parallel"),
        name="window_attention",
    )(sink, q, kv, kv, kv, kv_ctx)


def _ctx_attn_kernel(sink_ref, q_ref, kv_ref, o_ref, *, tq):
    _attn_body(q_ref[...], *_split_kv(kv_ref[...]), None, sink_ref, o_ref, tq)


def _ctx_attn_call(sink, q, kv, n_seq, ctx_len):
    def blk(w):
        return pl.BlockSpec((ctx_len, w), lambda b: (b, 0))

    return pl.pallas_call(
        functools.partial(_ctx_attn_kernel, tq=ctx_len),
        grid=(n_seq,),
        in_specs=[pl.BlockSpec(memory_space=pltpu.SMEM), blk(ATTN_WIDTH), blk(KV_PACK_WIDTH)],
        out_specs=blk(ATTN_WIDTH),
        out_shape=jax.ShapeDtypeStruct((n_seq * ctx_len, ATTN_WIDTH), BF16),
        compiler_params=_params("parallel"),
        name="context_attention",
    )(sink, q, kv)


OUT_TILE = 512


def _outproj_kernel(attx_ref, attc_ref, yx_ref, yc_ref, xx_ref, xc_ref, wof_ref, g_ref, lng_ref, lnb_ref,
                    sc_ref, sh_ref, wr_ref, br_ref, before_ref,
                    x1_ref, h2v_ref, route_ref, routet_ref, cnt_ref, wo_ref, o_buf, lg_buf,
                    *, alpha, n_lat_tiles, n_tiles, has_ctx):
    step = pl.program_id(0)

    @pl.when(step == 0)
    def _():
        wo_ref[...] = wof_ref[...].astype(BF16)
        o_buf[1] = jnp.zeros(o_buf.shape[1:], F32)
        lg_buf[1] = jnp.zeros(lg_buf.shape[1:], F32)

    o = o_buf[(step + 1) % 2]
    lg = lg_buf[(step + 1) % 2]

    if has_ctx:
        att = jnp.where(jnp.minimum(step, n_tiles - 1) < n_lat_tiles, attx_ref[...], attc_ref[...])
        yloc = jnp.where(jnp.minimum(step, n_tiles - 1) < n_lat_tiles, yx_ref[...], yc_ref[...])
        xin = jnp.where(jnp.clip(step - 1, 0, n_tiles - 1) < n_lat_tiles, xx_ref[...], xc_ref[...])
    else:
        att, yloc, xin = attx_ref[...], yx_ref[...], xx_ref[...]
    o_buf[step % 2] = _bdot(att, wo_ref[0:ATTN_WIDTH, :]) + _bdot(yloc, wo_ref[ATTN_WIDTH:, :])
    x1 = _layer_norm(alpha * xin + g_ref[...] * o) * lng_ref[...] + lnb_ref[...]
    x1_ref[...] = x1
    h2 = x1 * (1.0 + sc_ref[...]) + sh_ref[...]
    half = D_MODEL // 2
    hi_bits = lax.bitcast_convert_type(h2[:, 0:half].astype(BF16).astype(F32), jnp.uint32)
    lo_bits = lax.bitcast_convert_type(h2[:, half:].astype(BF16).astype(F32), jnp.uint32)
    packed = hi_bits | (lo_bits >> 16)
    for c in range(PACKED_TILE_ROWS):
        h2v_ref[:, c, :, :] = packed[:, c * LANES:(c + 1) * LANES].reshape(h2.shape[0] // SUBLANES, SUBLANES, LANES)

    hh, hl = _split_bf16(h2)
    t1 = _bdot(hh, wr_ref[...])
    lg_buf[step % 2] = (t1[:, 0:ROUTER_COLS] + t1[:, ROUTER_COLS:] + _bdot(hl, wr_ref[:, 0:ROUTER_COLS])
                        + br_ref[...])

    ts = lg.shape[0]
    lane = lax.broadcasted_iota(jnp.int32, (ts, ROUTER_COLS), 1).astype(F32)
    big = jnp.float32(ROUTER_COLS)

    def top1(v):
        m = jnp.max(v, axis=-1, keepdims=True)
        return m, jnp.min(jnp.where(v == m, lane, big), axis=-1, keepdims=True)

    gl = jnp.where(lane < N_GROUPS, lg, NEG_INF)
    g_val, g_idx = top1(gl)
    lse = g_val + jnp.log(jnp.sum(jnp.exp(gl - g_val), axis=-1, keepdims=True))
    p_group = jnp.exp(g_val - lse)
    e_lo = N_GROUPS + EXP_PER_GROUP * g_idx
    el = jnp.where((lane >= e_lo) & (lane < e_lo + EXP_PER_GROUP), lg, NEG_INF)
    e1, l1 = top1(el)
    e2, l2 = top1(jnp.where(lane == l1, NEG_INF, el))
    z = jnp.exp(e2 - e1)
    gate1 = p_group / (1.0 + z)
    gate2 = p_group * z / (1.0 + z)
    x1id = l1 - N_GROUPS
    x2id = l2 - N_GROUPS

    sel1 = lane == x1id
    sel2 = lane == x2id
    onehot = jnp.where(sel1 | sel2, 1.0, 0.0)
    prefix = _bdot(before_ref[...], onehot.astype(BF16))
    rank1 = jnp.sum(jnp.where(sel1, prefix, 0.0), axis=-1, keepdims=True)
    rank2 = jnp.sum(jnp.where(sel2, prefix, 0.0), axis=-1, keepdims=True)
    route = jnp.zeros((ts, ROUTER_COLS), F32)
    for col, val in enumerate((x1id, x2id, gate1, gate2, rank1, rank2)):
        route = jnp.where(lane == col, val, route)
    route_ref[...] = route
    routet_ref[...] = route.T[0:SUBLANES, :]
    cnt_ref[...] =jnp.broadcast_to(jnp.sum(onehot, axis=0, keepdims=True), (SUBLANES, ROUTER_COLS))


def _outproj_call(att_x, att_c, y_x, y_c, x_arr, x_off, c_arr, c_off, n_lat, n_ctx, seq_len,
                  w_out, layer, gvecs, lng, lnb, scvecs, shvecs, wr, br, alpha):
    ts = OUT_TILE
    d = D_MODEL
    n_lat_tiles = n_lat // ts
    n_tiles = (n_lat + n_ctx) // ts
    tiles_per_seq = seq_len // ts
    n_seq = n_lat // seq_len
    xo, co = x_off // ts, c_off // ts

    def mm_tile(s):
        return jnp.minimum(s, n_tiles - 1)

    def ln_tile(s):
        return jnp.clip(s - 1, 0, n_tiles - 1)

    def route_tile(s):
        return jnp.clip(s - 2, 0, n_tiles - 1)

    def lat_loc(s):
        return (jnp.minimum(mm_tile(s), n_lat_tiles - 1), 0)

    def ctx_loc(s):
        return (jnp.maximum(mm_tile(s) - n_lat_tiles, 0), 0)

    def lat_in(s):
        return (xo + jnp.minimum(ln_tile(s), n_lat_tiles - 1), 0)

    def ctx_in(s):
        return (co + jnp.maximum(ln_tile(s) - n_lat_tiles, 0), 0)

    def vec_idx(s):
        i = ln_tile(s)
        return (jnp.where(i < n_lat_tiles, i // tiles_per_seq, n_seq), 0, 0)

    vecb = pl.BlockSpec((None, 1, d), vec_idx)
    vec0 = pl.BlockSpec((1, d), lambda i: (0, 0))
    return pl.pallas_call(
        functools.partial(_outproj_kernel, alpha=alpha, n_lat_tiles=n_lat_tiles, n_tiles=n_tiles,
                          has_ctx=n_ctx > 0),
        grid=(n_tiles + 2,),
        in_specs=[
            pl.BlockSpec((ts, ATTN_WIDTH), lat_loc), pl.BlockSpec((ts, ATTN_WIDTH), ctx_loc),
            pl.BlockSpec((ts, LOCAL_WIDTH), lat_loc), pl.BlockSpec((ts, LOCAL_WIDTH), ctx_loc),
            pl.BlockSpec((ts, d), lat_in), pl.BlockSpec((ts, d), ctx_in),
            pl.BlockSpec((None, d, d), lambda i: (layer, 0, 0)),
            vecb, vec0, vec0, vecb, vecb,
            pl.BlockSpec((d, 2 * ROUTER_COLS), lambda i: (0, 0)),
            pl.BlockSpec((1, ROUTER_COLS), lambda i: (0, 0)),
            pl.BlockSpec((ts, ts), lambda i: (0, 0)),
        ],
        out_specs=[pl.BlockSpec((ts, d), lambda s: (ln_tile(s), 0)),
                   pl.BlockSpec((ts // SUBLANES, PACKED_TILE_ROWS, SUBLANES, LANES),
                                lambda s: (ln_tile(s), 0, 0, 0)),
                   pl.BlockSpec((ts, ROUTER_COLS), lambda s: (route_tile(s), 0)),
                   pl.BlockSpec((None, SUBLANES, ts), lambda s: (route_tile(s), 0, 0)),
                   pl.BlockSpec((None, SUBLANES, ROUTER_COLS), lambda s: (route_tile(s), 0, 0))],
        out_shape=[jax.ShapeDtypeStruct((n_lat + n_ctx, d), F32),
                   jax.ShapeDtypeStruct(((n_lat + n_ctx) // SUBLANES, PACKED_TILE_ROWS, SUBLANES, LANES),
                                        jnp.uint32),
                   jax.ShapeDtypeStruct((n_lat + n_ctx, ROUTER_COLS), F32),
                   jax.ShapeDtypeStruct((n_tiles, SUBLANES, ts), F32),
                   jax.ShapeDtypeStruct((n_tiles, SUBLANES, ROUTER_COLS), F32)],
        scratch_shapes=[pltpu.VMEM((d, d), BF16), pltpu.VMEM((2, ts, d), F32),
                        pltpu.VMEM((2, ts, ROUTER_COLS), F32)],
        compiler_params=_params("arbitrary"),
        name="out_projection",
    )(att_x, att_c, y_x, y_c, x_arr, c_arr, w_out, gvecs, lng, lnb, scvecs, shvecs, wr, br,
      jnp.tril(jnp.ones((ts, ts), BF16), -1))


DISPATCH_TILE = 2048


def _dispatch_kernel(seg_ref, pos_ref, h_hbm, xs_out, zbuf, zsem, sem, hbuf, isem):
    block_tiles = MOE_BLOCK // TOKENS_PER_PACKED_TILE

    @pl.when(pl.program_id(0) == 0)
    def _():
        zbuf[...] = jnp.zeros_like(zbuf)

        def zero_copy(e):
            first_tile = (seg_ref[0, e] - MOE_BLOCK) // TOKENS_PER_PACKED_TILE
            return pltpu.make_async_copy(zbuf, xs_out.at[pl.ds(first_tile, block_tiles)], zsem)

        for e in range(N_EXPERTS):
            @pl.when(seg_ref[1, e] > 0)
            def _():
                zero_copy(e).start()
        for e in range(N_EXPERTS):
            @pl.when(seg_ref[1, e] > 0)
            def _():
                zero_copy(e).wait()

        def tail_copy(b):
            return pltpu.make_async_copy(zbuf, xs_out.at[pl.ds(b * block_tiles, block_tiles)], zsem)

        n_blocks = xs_out.shape[0] // block_tiles
        first_unused = seg_ref[0, N_EXPERTS - 1] // MOE_BLOCK
        lax.fori_loop(first_unused, n_blocks, lambda b, c: (tail_copy(b).start(), c)[1], 0)
        lax.fori_loop(first_unused, n_blocks, lambda b, c: (tail_copy(b).wait(), c)[1], 0)

    i = pl.program_id(0)
    n = pl.num_programs(0)
    groups = DISPATCH_TILE // SUBLANES

    def load(tile, slot):
        return pltpu.make_async_copy(h_hbm.at[pl.ds(tile * groups, groups)], hbuf.at[slot], isem.at[slot])

    def rows_done(k):
        for _ in range(TOP_K):
            pltpu.make_async_copy(hbuf.at[0], hbuf.at[0], sem.at[k]).wait()

    @pl.when(i == 0)
    def _():
        load(0, 0).start()

    @pl.when(i + 1 < n)
    def _():
        load(i + 1, (i + 1) % 3).start()

    slot = i % 3
    load(i, slot).wait()

    def body(j, carry):
        for s in range(SUBLANES):
            t = SUBLANES * j + s
            for k in range(TOP_K):
                p = pos_ref[0, 0, k * DISPATCH_TILE + t]
                dst = xs_out.at[p >> 1, pl.ds(PACKED_TILE_ROWS * (p & 1), PACKED_TILE_ROWS), :]
                pltpu.make_async_copy(hbuf.at[slot, j, :, s, :], dst, sem.at[i % 2]).start(priority=k)
        return carry

    lax.fori_loop(0, groups, body, 0)

    @pl.when(i >= 1)
    def _():
        rows_done((i + 1) % 2)

    @pl.when(i == n - 1)
    def _():
        rows_done(i % 2)


def _dispatch_call(seg, pos, h2t, n_slots):
    n_tok = h2t.shape[0] * SUBLANES
    n_tiles = n_tok // DISPATCH_TILE
    grid_spec = pltpu.PrefetchScalarGridSpec(
        num_scalar_prefetch=1,
        grid=(n_tiles,),
        in_specs=[
            pl.BlockSpec((1, 1, TOP_K * DISPATCH_TILE), lambda i, seg: (i, 0, 0), memory_space=pltpu.SMEM),
            pl.BlockSpec(memory_space=pl.ANY),
        ],
        out_specs=pl.BlockSpec(memory_space=pl.ANY),
        scratch_shapes=[pltpu.VMEM((MOE_BLOCK // TOKENS_PER_PACKED_TILE, SUBLANES, LANES), jnp.uint32),
                        pltpu.SemaphoreType.DMA(()), pltpu.SemaphoreType.DMA((2,)),
                        pltpu.VMEM((3, DISPATCH_TILE // SUBLANES, PACKED_TILE_ROWS, SUBLANES, LANES), jnp.uint32),
                        pltpu.SemaphoreType.DMA((3,))],
    )
    return pl.pallas_call(
        _dispatch_kernel,
        grid_spec=grid_spec,
        out_shape=jax.ShapeDtypeStruct((n_slots // TOKENS_PER_PACKED_TILE, SUBLANES, LANES), jnp.uint32),
        compiler_params=_params("arbitrary"),
        name="moe_dispatch",
    )(seg, pos, h2t)


def _moe_kernel(be_ref, ne_ref, nu_ref, xs_hbm, w1_hbm, w3_hbm, w2_hbm, ys_hbm, w1b, w3b, w2b, w1f, w3f, w2f,
                xbuf, obuf, isem, osem, wsem, *, layer):
    i = pl.program_id(0)
    n_used = nu_ref[0]
    slot = i % 2
    groups = MOE_BLOCK // SUBLANES
    half = D_MODEL // 2

    def in_copies(blk, at_slot):
        return [pltpu.make_async_copy(
            xs_hbm.at[pl.ds(blk * groups, groups), s // TOKENS_PER_PACKED_TILE,
                      pl.ds(PACKED_TILE_ROWS * (s % TOKENS_PER_PACKED_TILE), PACKED_TILE_ROWS), :],
            xbuf.at[at_slot, :, :, s, :], isem.at[at_slot]) for s in range(SUBLANES)]

    def out_copies(blk, at_slot):
        return [pltpu.make_async_copy(obuf.at[at_slot, :, :, s, :], ys_hbm.at[pl.ds(blk * groups, groups), s],
                                      osem.at[at_slot]) for s in range(SUBLANES)]

    def write_rows(at_slot, y):
        for c in range(ROW_TILE_ROWS):
            obuf[at_slot, :, c, :, :] = y[:, c * LANES:(c + 1) * LANES].reshape(groups, SUBLANES, LANES)

    @pl.when(i == 0)
    def _():
        _start_all(in_copies(0, 0))

    @pl.when(i + 1 < n_used)
    def _():
        _start_all(in_copies(i + 1, 1 - slot))

    def weight_copies(e):
        return [pltpu.make_async_copy(src.at[layer, e], dst, wsem)
                for src, dst in ((w1_hbm, w1f), (w3_hbm, w3f), (w2_hbm, w2f))]

    @pl.when(i == 0)
    def _():
        _start_all(weight_copies(be_ref[0]))

    @pl.when((i < n_used) & ((i == 0) | (be_ref[i] != be_ref[jnp.maximum(i - 1, 0)])))
    def _():
        _wait_all(weight_copies(be_ref[i]))
        w1b[...] = w1f[...].astype(BF16)
        w3b[...] = w3f[...].astype(BF16)
        w2b[...] = w2f[...].astype(BF16)

        @pl.when(ne_ref[i] != be_ref[i])
        def _():
            _start_all(weight_copies(ne_ref[i]))

    @pl.when(i < n_used)
    def _():
        _wait_all(in_copies(i, slot))
        p = jnp.concatenate([xbuf[slot, :, c, :, :].reshape(MOE_BLOCK, LANES) for c in range(PACKED_TILE_ROWS)],
                            axis=1)
        xa = lax.bitcast_convert_type(p & jnp.uint32(0xFFFF0000), F32).astype(BF16)
        xb = lax.bitcast_convert_type(p << 16, F32).astype(BF16)
        h1 = _bdot(xa, w1b[0:half, :]) + _bdot(xb, w1b[half:, :])
        h3 = _bdot(xa, w3b[0:half, :]) + _bdot(xb, w3b[half:, :])
        write_rows(slot, _bdot((jax.nn.silu(h1) * h3).astype(BF16), w2b[...]))
        _start_all(out_copies(i, slot))

        @pl.when(i >= 1)
        def _():
            _wait_all(out_copies(i - 1, 1 - slot))

        @pl.when(i == n_used - 1)
        def _():
            _wait_all(out_copies(i, slot))

    @pl.when(i == n_used)
    def _():
        obuf[0] = jnp.zeros(obuf.shape[1:], F32)

    @pl.when(i >= n_used)
    def _():
        _start_all(out_copies(i, 0))

    @pl.when(i == pl.num_programs(0) - 1)
    def _():
        lax.fori_loop(n_used, pl.num_programs(0), lambda b, c: (_wait_all(out_copies(b, 0)), c)[1], 0)


def _moe_call(block_e, next_e, n_used, xs, w1, w3, w2, layer):
    n_blocks = block_e.shape[0]
    n_slots = n_blocks * MOE_BLOCK
    d = D_MODEL
    groups = MOE_BLOCK // SUBLANES
    grid_spec = pltpu.PrefetchScalarGridSpec(
        num_scalar_prefetch=3,
        grid=(n_blocks,),
        in_specs=[pl.BlockSpec(memory_space=pl.ANY)] * 4,
        out_specs=pl.BlockSpec(memory_space=pl.ANY),
        scratch_shapes=[pltpu.VMEM((d, D_EXPERT), BF16), pltpu.VMEM((d, D_EXPERT), BF16),
                        pltpu.VMEM((D_EXPERT, d), BF16),
                        pltpu.VMEM((d, D_EXPERT), F32), pltpu.VMEM((d, D_EXPERT), F32),
                        pltpu.VMEM((D_EXPERT, d), F32),
                        pltpu.VMEM((2, groups, PACKED_TILE_ROWS, SUBLANES, LANES), jnp.uint32),
                        pltpu.VMEM((2, groups, ROW_TILE_ROWS, SUBLANES, LANES), F32),
                        pltpu.SemaphoreType.DMA((2,)), pltpu.SemaphoreType.DMA((2,)),
                        pltpu.SemaphoreType.DMA(())],
    )
    xs_view = (n_slots // SUBLANES, SUBLANES // TOKENS_PER_PACKED_TILE, SUBLANES, LANES)
    ys_view = (n_slots // SUBLANES, SUBLANES, ROW_TILE_ROWS, LANES)
    ys = pl.pallas_call(
        functools.partial(_moe_kernel, layer=layer),
        grid_spec=grid_spec,
        out_shape=jax.ShapeDtypeStruct(ys_view, F32),
        compiler_params=_params("arbitrary"),
        name="moe_experts",
    )(block_e, next_e, n_used, xs.reshape(xs_view), w1, w3, w2)
    return ys.reshape(n_slots, ROW_TILE_ROWS, LANES)


COMBINE_TILE = 256


def _combine_kernel(pos0_ref, posn_ref, x1_ref, route_ref, g_ref, lng_ref, lnb_ref, ys_hbm, o_ref, ybuf, sem,
                    *, alpha):
    i = pl.program_id(0)
    nb = pl.num_programs(0)
    n_rows = TOP_K * COMBINE_TILE

    def issue(pos_ref, slot):
        def body(j, carry):
            for s in range(SUBLANES):
                pltpu.make_async_copy(ys_hbm.at[pos_ref[0, 0, SUBLANES * j + s]], ybuf.at[slot, j, :, s, :],
                                      sem.at[slot]).start(priority=s % N_DMA_PRIORITIES)
            return carry
        lax.fori_loop(0, n_rows // SUBLANES, body, 0)

    @pl.when(i == 0)
    def _():
        issue(pos0_ref, 0)

    @pl.when(i + 1 < nb)
    def _():
        issue(posn_ref, (i + 1) % 2)

    slot = i % 2
    pltpu.make_async_copy(ybuf.at[slot], ybuf.at[slot], sem.at[slot]).wait()
    yrows = jnp.concatenate([ybuf[slot, :, c, :, :].reshape(n_rows, LANES) for c in range(ROW_TILE_ROWS)], axis=1)
    route = route_ref[...]
    y = (yrows[0:COMBINE_TILE, :] * route[:, ROUTE_GATE:ROUTE_GATE + 1]
         + yrows[COMBINE_TILE:, :] * route[:, ROUTE_GATE + 1:ROUTE_GATE + 2])
    o_ref[...] = _layer_norm(alpha * x1_ref[...] + g_ref[...] * y) * lng_ref[...] + lnb_ref[...]


def _combine_call(pos, x1, route, gvecs, lng, lnb, ys, n_tok, tiles_per_seq, n_seq, alpha):
    d = D_MODEL
    n_tiles = n_tok // COMBINE_TILE
    n_lat_tiles = tiles_per_seq * n_seq

    def g_idx(i):
        return (jnp.where(i < n_lat_tiles, i // tiles_per_seq, n_seq), 0, 0)

    smem_blk = (1, 1, TOP_K * COMBINE_TILE)
    return pl.pallas_call(
        functools.partial(_combine_kernel, alpha=alpha),
        grid=(n_tiles,),
        in_specs=[
            pl.BlockSpec(smem_blk, lambda i: (0, 0, 0), memory_space=pltpu.SMEM),
            pl.BlockSpec(smem_blk, lambda i: (jnp.minimum(i + 1, n_tiles - 1), 0, 0), memory_space=pltpu.SMEM),
            pl.BlockSpec((COMBINE_TILE, d), lambda i: (i, 0)),
            pl.BlockSpec((COMBINE_TILE, ROUTER_COLS), lambda i: (i, 0)),
            pl.BlockSpec((None, 1, d), g_idx),
            pl.BlockSpec((1, d), lambda i: (0, 0)),
            pl.BlockSpec((1, d), lambda i: (0, 0)),
            pl.BlockSpec(memory_space=pl.ANY),
        ],
        out_specs=pl.BlockSpec((COMBINE_TILE, d), lambda i: (i, 0)),
        out_shape=jax.ShapeDtypeStruct((n_tok, d), F32),
        scratch_shapes=[pltpu.VMEM((2, TOP_K * COMBINE_TILE // SUBLANES, ROW_TILE_ROWS, SUBLANES, LANES), F32),
                        pltpu.SemaphoreType.DMA((2,))],
        compiler_params=_params("arbitrary"),
        name="moe_combine",
    )(pos, pos, x1, route, gvecs, lng, lnb, ys)


def _slots(route_t, tile_counts):
    n_tiles = tile_counts.shape[0]
    n = n_tiles * route_t.shape[2]
    tc = tile_counts.astype(jnp.int32)
    counts = jnp.sum(tc, axis=0)
    padded = (counts + MOE_BLOCK - 1) // MOE_BLOCK * MOE_BLOCK
    pends = jnp.cumsum(padded)
    base = (pends - padded)[None, :] + jnp.cumsum(tc, axis=0) - tc
    n_blocks = -(-n * TOP_K // MOE_BLOCK) + N_EXPERTS
    block_start = jnp.arange(n_blocks, dtype=jnp.int32) * MOE_BLOCK
    block_e = jnp.minimum(jnp.sum(pends[None, 0:N_EXPERTS] <= block_start[:, None], axis=1),
                          N_EXPERTS - 1).astype(jnp.int32)
    ids = jnp.arange(N_EXPERTS, dtype=jnp.int32)
    later = (ids[None, :] > ids[:, None]) & (padded[None, 0:N_EXPERTS] > 0)
    next_nonempty = jnp.min(jnp.where(later, ids[None, :], N_EXPERTS), axis=1)
    next_of = jnp.where(next_nonempty < N_EXPERTS, next_nonempty, ids)
    next_e = jnp.sum(jnp.where(block_e[:, None] == ids[None, :], next_of[None, :], 0), axis=1).astype(jnp.int32)
    n_used = (pends[N_EXPERTS - 1] // MOE_BLOCK).astype(jnp.int32).reshape(1)
    seg = jnp.stack([pends[0:N_EXPERTS], padded[0:N_EXPERTS]]).astype(jnp.int32)
    pos = []
    for k in range(TOP_K):
        e = route_t[:, ROUTE_EXPERT + k, :].astype(jnp.int32)
        hit = e[:, None, :] == jnp.arange(N_EXPERTS, dtype=jnp.int32)[None, :, None]
        b = jnp.sum(jnp.where(hit, base[:, 0:N_EXPERTS, None], 0), axis=1)
        pos.append((b + route_t[:, ROUTE_RANK + k, :].astype(jnp.int32)).reshape(n))
    return block_e, next_e, n_used, seg, pos


def _rope_tables(seq_len):
    m = HEAD_DIM // 4
    freqs = ROPE_BASE ** (-jnp.arange(m, dtype=F32) / m)
    t = jnp.arange(seq_len)
    row = (t // GRID_W).astype(F32)[:, None] * freqs[None, :]
    col = (t % GRID_W).astype(F32)[:, None] * freqs[None, :]
    cos = jnp.concatenate([jnp.cos(row), jnp.cos(row), jnp.cos(col), jnp.cos(col)], axis=-1)
    zero = jnp.zeros_like(row)
    sin_a = jnp.concatenate([zero, jnp.sin(row), zero, jnp.sin(col)], axis=-1)
    sin_b = jnp.concatenate([-jnp.sin(row), zero, -jnp.sin(col), zero], axis=-1)
    rep = LANES // HEAD_DIM
    return jnp.tile(cos, (1, rep)), jnp.tile(sin_a, (1, rep)), jnp.tile(sin_b, (1, rep))


def kernel(x, c, ctx, c_ctx, w_ada, b_ada, w_in, conv_w, attn_sink, gm_ws, gm_bs, w_out, ln1_g, ln1_b,
           w_rg, b_rg, w_re, b_re, w1, w3, w2, ln2_g, ln2_b):
    b_, s_, d_ = x.shape
    c_len = ctx.shape[1]
    depth = w_ada.shape[0]
    alpha = (2 * depth) ** 0.25
    n_lat = b_ * s_
    n_ctx = b_ * c_len
    ts = 1024

    cin = jnp.zeros((ADA_ROWS, d_), F32).at[0:b_].set(c).at[b_].set(c_ctx)
    mod = _ada_call(cin, w_ada, b_ada)

    cos, sin_a, sin_b = _rope_tables(s_)
    ones_c = jnp.ones((c_len, LANES), F32)
    zeros_c = jnp.zeros((c_len, LANES), F32)

    x_flat = x.reshape(n_lat, d_)
    c_flat = ctx.reshape(n_ctx, d_)
    x_off, c_off = 0, 0
    x_arr, c_arr = x_flat, c_flat

    for l in range(depth):
        last = l == depth - 1
        mx = mod[l, 0:b_].reshape(b_, 6, 1, d_)
        sh1, sc1, g1, sh2, sc2, g2 = (mx[:, i] for i in range(6))
        mc = jnp.broadcast_to(mod[l, b_].reshape(1, 6, 1, d_), (b_, 6, 1, d_))
        gm_w = gm_ws[l].astype(BF16)
        gm_b = jnp.repeat(gm_bs[l].T, GM_HEAD, axis=1)
        sink = attn_sink[l]

        qx, kvx, yx = _inproj_call(x_arr, x_off, b_, s_, ts, sc1, sh1, w_in, l, cos, sin_a, sin_b,
                                   conv_w[l], gm_w, gm_b)
        qc, kvc, yc = _inproj_call(c_arr, c_off, b_, c_len, c_len, mc[:, 1], mc[:, 0], w_in, l,
                                   ones_c, zeros_c, zeros_c, conv_w[l], gm_w, gm_b)
        att_x = _win_attn_call(sink, qx, kvx, kvc, b_, s_, c_len)

        w_r = jnp.zeros((d_, ROUTER_COLS), F32).at[:, 0:N_GROUPS].set(w_rg[l]) \
            .at[:, N_GROUPS:N_GROUPS + N_EXPERTS].set(w_re[l])
        w_r_hi = w_r.astype(BF16)
        w_r_lo = (w_r - w_r_hi.astype(F32)).astype(BF16)
        wr = jnp.concatenate([w_r_hi, w_r_lo], axis=1)
        br = jnp.zeros((1, ROUTER_COLS), F32).at[0, 0:N_GROUPS].set(b_rg[l]) \
            .at[0, N_GROUPS:N_GROUPS + N_EXPERTS].set(b_re[l])
        lng1, lnb1 = ln1_g[l].reshape(1, d_), ln1_b[l].reshape(1, d_)
        lng2, lnb2 = ln2_g[l].reshape(1, d_), ln2_b[l].reshape(1, d_)

        n_tot = n_lat if last else n_lat + n_ctx
        gvecs1 = jnp.concatenate([g1, mc[0:1, 2]], axis=0)
        scvecs2 = jnp.concatenate([sc2, mc[0:1, 4]], axis=0)
        shvecs2 = jnp.concatenate([sh2, mc[0:1, 3]], axis=0)
        if last:
            x1, h2t, route, route_t, tcnt = _outproj_call(att_x, att_x, yx, yx, x_arr, x_off, x_arr, x_off, n_lat, 0, s_,
                                                 w_out, l, gvecs1, lng1, lnb1, scvecs2, shvecs2, wr, br, alpha)
        else:
            att_c = _ctx_attn_call(sink, qc, kvc, b_, c_len)
            x1, h2t, route, route_t, tcnt = _outproj_call(att_x, att_c, yx, yc, x_arr, x_off, c_arr, c_off, n_lat, n_ctx,
                                                 s_, w_out, l, gvecs1, lng1, lnb1, scvecs2, shvecs2, wr, br, alpha)

        block_e, next_e, n_used, seg, pos = _slots(route_t, tcnt[:, 0, :])
        n_slots = block_e.shape[0] * MOE_BLOCK

        def per_tile(tile):
            return jnp.concatenate([p.reshape(n_tot // tile, 1, tile) for p in pos], axis=2)

        xs = _dispatch_call(seg, per_tile(DISPATCH_TILE), h2t, n_slots)
        ys = _moe_call(block_e, next_e, n_used, xs, w1, w3, w2, l)

        gvecs = jnp.concatenate([g2, mc[0:1, 5]], axis=0)
        out = _combine_call(per_tile(COMBINE_TILE), x1, route, gvecs, lng2, lnb2, ys, n_tot, s_ // COMBINE_TILE,
                            b_, alpha)
        x_arr, x_off = out, 0
        c_arr, c_off = out, n_lat

    return x_arr.reshape(b_, s_, d_)
```
